```python
import jax, jax.numpy as jnp
from jax import lax
import numpy as np

D_MODEL = 1024
BATCH = 8
SEQ = 8192
DEPTH = 1

HEAD_DIM = 64
N_Q_HEADS = 8
N_KV_HEADS = 2
GROUP = N_Q_HEADS // N_KV_HEADS
ATTN_WIDTH = N_Q_HEADS * HEAD_DIM
KV_WIDTH = N_KV_HEADS * HEAD_DIM
WINDOW = 128
BLOCK = 128
ROT_DIM = HEAD_DIM // 4
ROPE_THETA = 500000.0
CONV_WIDTH = D_MODEL - ATTN_WIDTH
CONV_K = 3
MIX_WIDTH = ATTN_WIDTH + CONV_WIDTH
IN_SPLITS = [ATTN_WIDTH, KV_WIDTH, KV_WIDTH, ATTN_WIDTH,
             CONV_WIDTH, CONV_WIDTH, CONV_WIDTH, CONV_WIDTH]
IN_WIDTH = sum(IN_SPLITS)
EPS = 1e-5

kernel_name = "hybrid_swa_sink_shortconv_block"


def rms_norm(x, g):
    xf = x.astype(jnp.float32)
    y = xf * lax.rsqrt(jnp.mean(xf * xf, axis=-1, keepdims=True) + EPS)
    return (y * g.astype(jnp.float32)).astype(x.dtype)


def partial_rope(t, pos):
    half = ROT_DIM // 2
    inv_freq = ROPE_THETA ** (-jnp.arange(0, ROT_DIM, 2, dtype=jnp.float32) / ROT_DIM)
    ang = pos.astype(jnp.float32)[:, None] * inv_freq[None, :]
    cos = jnp.cos(ang)[None, :, None, :]
    sin = jnp.sin(ang)[None, :, None, :]
    rot = t[..., :ROT_DIM].astype(jnp.float32)
    x1, x2 = rot[..., :half], rot[..., half:]
    r = jnp.concatenate([x1 * cos - x2 * sin, x2 * cos + x1 * sin], axis=-1)
    return jnp.concatenate([r.astype(t.dtype), t[..., ROT_DIM:]], axis=-1)


def sliding_window_sink_attention(q, k, v, sinks):
    B, S, _, D = q.shape
    nb = S // BLOCK
    qb = q.reshape(B, nb, BLOCK, N_KV_HEADS, GROUP, D)
    kb = k.reshape(B, nb, BLOCK, N_KV_HEADS, D)
    vb = v.reshape(B, nb, BLOCK, N_KV_HEADS, D)
    pad = ((0, 0), (1, 0), (0, 0), (0, 0), (0, 0))
    kk = jnp.concatenate([jnp.pad(kb, pad)[:, :-1], kb], axis=2)
    vv = jnp.concatenate([jnp.pad(vb, pad)[:, :-1], vb], axis=2)
    scale = 1.0 / np.sqrt(D)
    s = jnp.einsum('bnqkgd,bnskd->bnkgqs', qb, kk,
                   preferred_element_type=jnp.float32) * scale
    qi = jnp.arange(BLOCK)[:, None]
    kj = jnp.arange(2 * BLOCK)[None, :]
    delta = qi + BLOCK - kj
    band = (delta >= 0) & (delta < WINDOW)
    valid = (jnp.arange(nb)[:, None] * BLOCK + kj - BLOCK) >= 0
    mask = (band[None] & valid[:, None, :])[None, :, None, None]
    s = jnp.where(mask, s, -jnp.inf)
    sink = sinks.astype(jnp.float32).reshape(N_KV_HEADS, GROUP)[None, None, :, :, None, None]
    m = jnp.maximum(jnp.max(s, axis=-1, keepdims=True), sink)
    p = jnp.exp(s - m)
    denom = jnp.sum(p, axis=-1, keepdims=True) + jnp.exp(sink - m)
    prob = (p / denom).astype(v.dtype)
    o = jnp.einsum('bnkgqs,bnskd->bnqkgd', prob, vv,
                   preferred_element_type=jnp.float32)
    return o.reshape(B, S, N_Q_HEADS * D).astype(q.dtype)


def causal_short_conv(u, w):
    S = u.shape[1]
    up = jnp.pad(u, ((0, 0), (CONV_K - 1, 0), (0, 0)))
    y = w[0] * up[:, 0:S]
    for j in range(1, CONV_K):
        y = y + w[j] * up[:, j:j + S]
    return y


def _fwd_setup_inputs(seed: int = 0) -> dict:
    key = jax.random.key(seed)
    ks = jax.random.split(key, 8)
    f32 = jnp.float32
    x = jax.random.normal(ks[0], (BATCH, SEQ, D_MODEL), f32)
    norm_g = 1.0 + 0.02 * jax.random.normal(ks[1], (D_MODEL,), f32)
    w_in = jax.random.normal(ks[2], (D_MODEL, IN_WIDTH), f32) * D_MODEL ** -0.5
    sinks = 0.5 * jax.random.normal(ks[3], (N_Q_HEADS,), f32)
    conv_w = jax.random.normal(ks[4], (CONV_K, CONV_WIDTH), f32) * CONV_K ** -0.5
    w_out = jax.random.normal(ks[5], (MIX_WIDTH, D_MODEL), f32) * MIX_WIDTH ** -0.5
    final_g = 1.0 + 0.02 * jax.random.normal(ks[6], (D_MODEL,), f32)
    return {"x": x, "norm_g": norm_g, "w_in": w_in, "sinks": sinks,
            "conv_w": conv_w, "w_out": w_out, "final_g": final_g}


def _fwd_reference(x, norm_g, w_in, sinks, conv_w, w_out, final_g):
    B, S, _ = x.shape
    pos = jnp.arange(S, dtype=jnp.int32)
    h = x
    for _ in range(DEPTH):
        xn = rms_norm(h, norm_g)
        proj = jnp.einsum('bsd,de->bse', xn, w_in)
        offs = list(np.cumsum(IN_SPLITS)[:-1])
        q, k, v, g_attn, b_gate, c_gate, h_in, g_conv = jnp.split(proj, offs, axis=-1)
        q = partial_rope(q.reshape(B, S, N_Q_HEADS, HEAD_DIM), pos)
        k = partial_rope(k.reshape(B, S, N_KV_HEADS, HEAD_DIM), pos)
        v = v.reshape(B, S, N_KV_HEADS, HEAD_DIM)
        attn = sliding_window_sink_attention(q, k, v, sinks)
        y_attn = attn * jax.nn.silu(g_attn)
        y_conv = b_gate * causal_short_conv(c_gate * h_in, conv_w)
        y_conv = y_conv * jax.nn.silu(g_conv)
        mix = jnp.concatenate([y_attn, y_conv], axis=-1)
        h = h + jnp.einsum('bse,ed->bsd', mix, w_out)
    return rms_norm(h, final_g)


import jax as _jax
import jax.numpy as _jnp

TWIN_FORMAT = 'train_step'
FWD_PARAMS = ['x', 'norm_g', 'w_in', 'sinks', 'conv_w', 'w_out', 'final_g']
TWIN_WEIGHTS = ['norm_g', 'w_in', 'sinks', 'conv_w', 'w_out', 'final_g']
TWIN_DIFF_INPUT = 'x'
TWIN_INPUTS = ['x', 'norm_g', 'w_in', 'sinks', 'conv_w', 'w_out', 'final_g', 'loss_target', 'm_norm_g', 'm_w_in', 'm_sinks', 'm_conv_w', 'm_w_out', 'm_final_g', 'v_norm_g', 'v_w_in', 'v_sinks', 'v_conv_w', 'v_w_out', 'v_final_g']
TWIN_OUTPUTS = ['loss', 'grad_x', 'grad_norm_g', 'grad_w_in', 'grad_sinks', 'grad_conv_w', 'grad_w_out', 'grad_final_g', 'delta_norm_g', 'delta_w_in', 'delta_sinks', 'delta_conv_w', 'delta_w_out', 'delta_final_g', 'new_m_norm_g', 'new_m_w_in', 'new_m_sinks', 'new_m_conv_w', 'new_m_w_out', 'new_m_final_g', 'new_v_norm_g', 'new_v_w_in', 'new_v_sinks', 'new_v_conv_w', 'new_v_w_out', 'new_v_final_g']
TWIN_LEAF_KINDS = {'loss': 'loss', 'grad_x': 'grad_x', 'grad_norm_g': 'grad_w', 'grad_w_in': 'grad_w', 'grad_sinks': 'grad_w', 'grad_conv_w': 'grad_w', 'grad_w_out': 'grad_w', 'grad_final_g': 'grad_w', 'delta_norm_g': 'delta_w', 'delta_w_in': 'delta_w', 'delta_sinks': 'delta_w', 'delta_conv_w': 'delta_w', 'delta_w_out': 'delta_w', 'delta_final_g': 'delta_w', 'new_m_norm_g': 'new_m', 'new_m_w_in': 'new_m', 'new_m_sinks': 'new_m', 'new_m_conv_w': 'new_m', 'new_m_w_out': 'new_m', 'new_m_final_g': 'new_m', 'new_v_norm_g': 'new_v', 'new_v_w_in': 'new_v', 'new_v_sinks': 'new_v', 'new_v_conv_w': 'new_v', 'new_v_w_out': 'new_v', 'new_v_final_g': 'new_v'}


def _forward(args):
    return _fwd_reference(*[args[k] for k in FWD_PARAMS])


def _output_shape():
    out = _jax.eval_shape(lambda: _forward(_fwd_setup_inputs(0)))
    return out.shape, out.dtype

N_MICROBATCH = 1
ADAM_LR = 0.001
ADAM_B1 = 0.9
ADAM_B2 = 0.999
ADAM_EPS = 1e-08
ADAM_WD = 0.01
ADAM_STEP = 10
PER_EXAMPLE_BATCH_AXIS = {'x': 0, 'loss_target': 0}
SHARED_INPUTS = []
_WEIGHT_DTYPES = {'norm_g': _jnp.float32, 'w_in': _jnp.float32, 'sinks': _jnp.float32, 'conv_w': _jnp.float32, 'w_out': _jnp.float32, 'final_g': _jnp.float32}
MOMENT_SCALE = {'norm_g': 1.904663e-01, 'w_in': 1.079141e-01, 'sinks': 2.239798e-02, 'conv_w': 1.368956e-01, 'w_out': 9.561797e-02, 'final_g': 6.391215e+01}


def _to_microbatches(a, axis):
    t = _jnp.moveaxis(a, axis, 0)
    t = t.reshape((N_MICROBATCH, t.shape[0] // N_MICROBATCH) + t.shape[1:])
    return _jnp.moveaxis(t, 1, axis + 1)


def setup_inputs(seed: int = 0) -> dict:
    inp = _fwd_setup_inputs(seed)
    key = _jax.random.fold_in(_jax.random.key(seed), 7919)
    shape, _ = _output_shape()
    out = dict(inp)
    out["loss_target"] = _jax.random.normal(_jax.random.fold_in(key, 0), shape, _jnp.float32)
    for i, name in enumerate(TWIN_WEIGHTS):
        w = inp[name].astype(_jnp.float32)
        if MOMENT_SCALE is None:
            s = _jnp.sqrt(_jnp.mean(_jnp.square(w)) + 1e-30)
        else:
            s = MOMENT_SCALE[name]
        km, kv = _jax.random.split(_jax.random.fold_in(key, i + 1))
        out[name] = w
        out["m_" + name] = s * _jax.random.normal(km, w.shape, _jnp.float32)
        out["v_" + name] = (s * s) * _jax.random.uniform(kv, w.shape, _jnp.float32, 0.5, 1.5)
    if N_MICROBATCH > 1:
        for name, axis in PER_EXAMPLE_BATCH_AXIS.items():
            out[name] = _to_microbatches(out[name], axis)
    return {'x': out['x'], 'norm_g': out['norm_g'], 'w_in': out['w_in'], 'sinks': out['sinks'], 'conv_w': out['conv_w'], 'w_out': out['w_out'], 'final_g': out['final_g'], 'loss_target': out['loss_target'], 'm_norm_g': out['m_norm_g'], 'm_w_in': out['m_w_in'], 'm_sinks': out['m_sinks'], 'm_conv_w': out['m_conv_w'], 'm_w_out': out['m_w_out'], 'm_final_g': out['m_final_g'], 'v_norm_g': out['v_norm_g'], 'v_w_in': out['v_w_in'], 'v_sinks': out['v_sinks'], 'v_conv_w': out['v_conv_w'], 'v_w_out': out['v_w_out'], 'v_final_g': out['v_final_g']}


def _loss(weights, diff, rest, loss_target):
    with _jax.named_scope("forward"):
        args = {**rest, TWIN_DIFF_INPUT: diff, **{k: w.astype(_WEIGHT_DTYPES[k]) for k, w in weights.items()}}
        y = _forward(args)
    with _jax.named_scope("loss_head"):
        err = _jnp.square(y.astype(_jnp.float32) - loss_target)
        return 0.5 * _jnp.sum(_jnp.mean(err, axis=-1)) if err.ndim else 0.5 * err


def _adamw(w, g, m, v):
    m = ADAM_B1 * m + (1.0 - ADAM_B1) * g
    v = ADAM_B2 * v + (1.0 - ADAM_B2) * _jnp.square(g)
    m_hat = m / (1.0 - ADAM_B1 ** ADAM_STEP)
    v_hat = v / (1.0 - ADAM_B2 ** ADAM_STEP)
    delta = -ADAM_LR * (m_hat / (_jnp.sqrt(v_hat) + ADAM_EPS) + ADAM_WD * w)
    return delta, m, v


def reference(x, norm_g, w_in, sinks, conv_w, w_out, final_g, loss_target, m_norm_g, m_w_in, m_sinks, m_conv_w, m_w_out, m_final_g, v_norm_g, v_w_in, v_sinks, v_conv_w, v_w_out, v_final_g):
    given = dict(x=x, norm_g=norm_g, w_in=w_in, sinks=sinks, conv_w=conv_w, w_out=w_out, final_g=final_g, loss_target=loss_target, m_norm_g=m_norm_g, m_w_in=m_w_in, m_sinks=m_sinks, m_conv_w=m_conv_w, m_w_out=m_w_out, m_final_g=m_final_g, v_norm_g=v_norm_g, v_w_in=v_w_in, v_sinks=v_sinks, v_conv_w=v_conv_w, v_w_out=v_w_out, v_final_g=v_final_g)
    weights = {n: given[n] for n in TWIN_WEIGHTS}
    shared = {n: given[n] for n in SHARED_INPUTS}
    per_example = {n: given[n] for n in ['x']}
    grad_fn = _jax.value_and_grad(_loss, argnums=(0, 1))

    def one_microbatch(ex, loss_target):
        ex = dict(ex)
        diff = ex.pop(TWIN_DIFF_INPUT)
        return grad_fn(weights, diff, {**shared, **ex}, loss_target)

    if N_MICROBATCH == 1:
        loss, (grad_w, grad_x) = one_microbatch(per_example, given["loss_target"])
    else:
        def body(carry, xs):
            loss_sum, grad_sum = carry
            l_k, (gw_k, gx_k) = one_microbatch(xs[0], xs[1])
            with _jax.named_scope("update"):
                return (loss_sum + l_k, _jax.tree.map(_jnp.add, grad_sum, gw_k)), gx_k

        init = (_jnp.zeros((), _jnp.float32), _jax.tree.map(_jnp.zeros_like, weights))
        (loss, grad_w), grad_x = _jax.lax.scan(body, init, (per_example, given["loss_target"]))
    with _jax.named_scope("update"):
        delta_w, new_m, new_v = {}, {}, {}
        for n in TWIN_WEIGHTS:
            delta_w[n], new_m[n], new_v[n] = _adamw(weights[n], grad_w[n], given["m_" + n], given["v_" + n])
    return (loss, grad_x, *[grad_w[n] for n in TWIN_WEIGHTS], *[delta_w[n] for n in TWIN_WEIGHTS],
            *[new_m[n] for n in TWIN_WEIGHTS], *[new_v[n] for n in TWIN_WEIGHTS])
```

```python
import functools

import jax
import jax.numpy as jnp
from jax import lax
from jax.experimental import pallas as pl
from jax.experimental.pallas import tpu as pltpu

F32 = jnp.float32
BF16 = jnp.bfloat16
MESH = pl.DeviceIdType.MESH

D_MODEL = 1024
HEAD_DIM = 64
N_Q_HEADS = 8
GROUP = 4
ATTN_W = 512
KV_W = 128
BLK = 128
CONV_W = 512
CONV_K = 3
IN_W = 3328
PA_W = 1280
PC_W = 2048
EPS = 1e-5
ROPE_THETA = 500000.0
ROT_DIM = 16
N_DEV = 8
N_CHIP = 4
SHARD_IN = IN_W // N_DEV
SHARD_OUT = D_MODEL // N_DEV
SMALL_ROWS = 40

ADAM_LR = 0.001
ADAM_B1 = 0.9
ADAM_B2 = 0.999
ADAM_EPS = 1e-08
ADAM_WD = 0.01
ADAM_STEP = 10

ACT = jnp.float32

TM = 512
TQ = 512
TC = 512
VMEM_LIMIT = 56 * 1024 * 1024

NT_DIMS = (((1,), (1,)), ((), ()))
TN_DIMS = (((0,), (0,)), ((), ()))


def _params(sem=None):
    kw = dict(vmem_limit_bytes=VMEM_LIMIT)
    if sem is not None:
        kw["dimension_semantics"] = sem
    return pltpu.CompilerParams(**kw)


def _nt(a, b):
    return lax.dot_general(a, b, NT_DIMS, preferred_element_type=F32)


def _tn(a, b):
    return lax.dot_general(a, b, TN_DIMS, preferred_element_type=F32)


def _nn(a, b):
    return jnp.dot(a, b, preferred_element_type=F32)


def _silu(g):
    return g * jax.nn.sigmoid(g)


def _dsilu(g):
    s = jax.nn.sigmoid(g)
    return s * (1.0 + g * (1.0 - s))


def _all_gather(arrs, name):
    n_arr = len(arrs)

    def body(*refs):
        x_refs = refs[:n_arr]
        out_refs = refs[n_arr:2 * n_arr]
        send_sems, recv_sems, local_sems = refs[2 * n_arr:]
        x, y, c = lax.axis_index("x"), lax.axis_index("y"), lax.axis_index("c")
        me, sibling = (x, y, c), (x, y, 1 - c)
        chips = [(1 - x, y), (x, 1 - y), (1 - x, 1 - y)]

        def rows(a, px, py, pc):
            m = x_refs[a].shape[0]
            return out_refs[a].at[pl.ds((4 * px + 2 * py + pc) * m, m), :]

        def copy(a, k, block, to, src=None):
            return pltpu.make_async_remote_copy(
                src_ref=rows(a, *block) if src is None else src,
                dst_ref=rows(a, *block),
                send_sem=send_sems.at[a * 7 + k],
                recv_sem=recv_sems.at[a * 7 + k],
                device_id=to,
                device_id_type=MESH,
            )

        mine = [pltpu.make_async_copy(x_refs[a], rows(a, *me), local_sems.at[a]) for a in range(n_arr)]
        for cp in mine:
            cp.start()
        first = []
        for a in range(n_arr):
            first.append(copy(a, 0, me, sibling, src=x_refs[a]))
            first += [copy(a, 1 + j, me, (*chip, c), src=x_refs[a]) for j, chip in enumerate(chips)]
        for cp in first:
            cp.start()
        passed = []
        for j, chip in enumerate(chips):
            for a in range(n_arr):
                copy(a, 1 + j, (*chip, c), me).wait_recv()
                fwd = copy(a, 4 + j, (*chip, c), sibling)
                fwd.start()
                passed.append(fwd)
        for a in range(n_arr):
            copy(a, 0, sibling, me).wait_recv()
            for j, chip in enumerate(chips):
                copy(a, 4 + j, (*chip, 1 - c), me).wait_recv()
        for cp in first + passed:
            cp.wait_send()
        for cp in mine:
            cp.wait()

    vmem = pl.BlockSpec(memory_space=pltpu.VMEM)
    return pl.pallas_call(
        body,
        name=name,
        out_shape=[jax.ShapeDtypeStruct((N_DEV * a.shape[0], a.shape[1]), a.dtype) for a in arrs],
        in_specs=[vmem] * n_arr,
        out_specs=[vmem] * n_arr,
        scratch_shapes=[
            pltpu.SemaphoreType.DMA((7 * n_arr,)),
            pltpu.SemaphoreType.DMA((7 * n_arr,)),
            pltpu.SemaphoreType.DMA((n_arr,)),
        ],
        compiler_params=_params(),
    )(*arrs)


def _reduce_scatter(grads):
    n_arr = len(grads)

    def body(*refs):
        g_refs = refs[:n_arr]
        r_refs = refs[n_arr:2 * n_arr]
        a_refs = refs[2 * n_arr:3 * n_arr]
        p_bufs = refs[3 * n_arr:4 * n_arr]
        t_bufs = refs[4 * n_arr:5 * n_arr]
        sib_send, sib_recv, ici_send, ici_recv, load_sems, own_sems = refs[5 * n_arr:]
        x, y, c = lax.axis_index("x"), lax.axis_index("y"), lax.axis_index("c")
        sibling = (x, y, 1 - c)

        def chip_of(r):
            return (x ^ (r >> 1), y ^ (r & 1))

        def block_of(r, core):
            cx, cy = chip_of(r)
            return 4 * cx + 2 * cy + core

        to_sib = []
        for a in range(n_arr):
            for r in range(N_CHIP):
                cp = pltpu.make_async_remote_copy(
                    src_ref=g_refs[a].at[block_of(r, 1 - c)], dst_ref=a_refs[a].at[r],
                    send_sem=sib_send.at[a * N_CHIP + r], recv_sem=sib_recv.at[a * N_CHIP + r],
                    device_id=sibling, device_id_type=MESH)
                cp.start()
                to_sib.append(cp)
        loads = []
        for a in range(n_arr):
            for r in range(N_CHIP):
                cp = pltpu.make_async_copy(g_refs[a].at[block_of(r, c)], p_bufs[a].at[r],
                                           load_sems.at[(2 * a) * N_CHIP + r])
                cp.start()
                loads.append(cp)
        sends = []
        for r in (1, 2, 3, 0):
            for a in range(n_arr):
                k = a * N_CHIP + r
                to_sib[k].wait_recv()
                got = pltpu.make_async_copy(a_refs[a].at[r], t_bufs[a].at[r],
                                            load_sems.at[(2 * a + 1) * N_CHIP + r])
                got.start()
                loads[k].wait()
                got.wait()
                p_bufs[a][r] = p_bufs[a][r] + t_bufs[a][r]
                if r == 0:
                    cp = pltpu.make_async_copy(p_bufs[a].at[0], r_refs[a].at[0], own_sems.at[a])
                else:
                    cp = pltpu.make_async_remote_copy(
                        src_ref=p_bufs[a].at[r], dst_ref=r_refs[a].at[r],
                        send_sem=ici_send.at[a * N_CHIP + r], recv_sem=ici_recv.at[a * N_CHIP + r],
                        device_id=(*chip_of(r), c), device_id_type=MESH)
                cp.start()
                sends.append((r, cp))
        for r, cp in sends:
            if r == 0:
                cp.wait()
            else:
                cp.wait_send()
                cp.wait_recv()
        for cp in to_sib:
            cp.wait_send()

    any_spec = pl.BlockSpec(memory_space=pl.ANY)
    part = [jax.ShapeDtypeStruct((N_CHIP,) + g.shape[1:], F32) for g in grads]
    outs = pl.pallas_call(
        body,
        name="reduce_scatter_grads",
        out_shape=part + part,
        in_specs=[any_spec] * n_arr,
        out_specs=[any_spec] * (2 * n_arr),
        scratch_shapes=(
            [pltpu.VMEM((N_CHIP,) + g.shape[1:], F32) for g in grads]
            + [pltpu.VMEM((N_CHIP,) + g.shape[1:], F32) for g in grads]
            + [pltpu.SemaphoreType.DMA((n_arr * N_CHIP,))] * 4
            + [pltpu.SemaphoreType.DMA((2 * n_arr * N_CHIP,)), pltpu.SemaphoreType.DMA((n_arr,))]
        ),
        compiler_params=_params(),
    )(*grads)
    return outs[:n_arr]


def _fwd_proj(x, norm_g, wt):
    S = x.shape[0]

    def body(x_ref, g_ref, wt_ref, xn_ref, pa_ref, pc_ref):
        xv = x_ref[...]
        r = lax.rsqrt(jnp.mean(xv * xv, axis=-1, keepdims=True) + EPS)
        xn = (xv * r * g_ref[...]).astype(BF16)
        xn_ref[...] = xn
        pa_ref[:, 0:512] = _nt(xn, wt_ref[0:512, :]).astype(ACT)
        pa_ref[:, 512:1024] = _nt(xn, wt_ref[768:1280, :]).astype(ACT)
        pa_ref[:, 1024:1280] = _nt(xn, wt_ref[512:768, :]).astype(ACT)
        pc_ref[...] = _nt(xn, wt_ref[1280:3328, :]).astype(ACT)

    return pl.pallas_call(
        body,
        name="fwd_proj",
        grid=(S // TM,),
        in_specs=[
            pl.BlockSpec((TM, D_MODEL), lambda i: (i, 0)),
            pl.BlockSpec((1, D_MODEL), lambda i: (0, 0)),
            pl.BlockSpec((IN_W, D_MODEL), lambda i: (0, 0)),
        ],
        out_specs=[
            pl.BlockSpec((TM, D_MODEL), lambda i: (i, 0)),
            pl.BlockSpec((TM, PA_W), lambda i: (i, 0)),
            pl.BlockSpec((TM, PC_W), lambda i: (i, 0)),
        ],
        out_shape=[
            jax.ShapeDtypeStruct((S, D_MODEL), BF16),
            jax.ShapeDtypeStruct((S, PA_W), ACT),
            jax.ShapeDtypeStruct((S, PC_W), ACT),
        ],
        compiler_params=_params(("arbitrary",)),
    )(x, norm_g, wt)


def _rope(t, tab):
    return (t * tab[:, 0:128] + pltpu.roll(t, 120, 1) * tab[:, 128:256]
            + pltpu.roll(t, 8, 1) * tab[:, 256:384])


def _rope_t(d, tab):
    return (d * tab[:, 0:128] + pltpu.roll(d * tab[:, 128:256], 8, 1)
            + pltpu.roll(d * tab[:, 256:384], 120, 1))


def _fill_kv(kall, kvc_ref, kvp_ref, tabc_ref, tabp_ref):
    for lo, kv_ref, tab_ref, n in ((0, kvp_ref, tabp_ref, BLK), (BLK, kvc_ref, tabc_ref, TQ)):
        k = _rope(kv_ref[:, 0:128].astype(F32), tab_ref[...])
        v = kv_ref[:, 128:256].astype(F32)
        kall[0, lo:lo + n, :] = k.astype(BF16)
        kall[1, lo:lo + n, :] = pltpu.roll(k, 64, 1).astype(BF16)
        kall[2, lo:lo + n, :] = v.astype(BF16)
        kall[3, lo:lo + n, :] = pltpu.roll(v, 64, 1).astype(BF16)


def _band_mask(first_block):
    qi = lax.broadcasted_iota(jnp.int32, (BLK, 2 * BLK), 0)
    kj = lax.broadcasted_iota(jnp.int32, (BLK, 2 * BLK), 1)
    delta = qi + BLK - kj
    band = (delta >= 0) & (delta < BLK)
    if first_block is None:
        return band
    return band & ((kj >= BLK) | jnp.logical_not(first_block))


def _softmax(qm, kk, mask, sink):
    s = _nt(qm, kk) * 0.125
    s = jnp.where(mask, s, -jnp.inf)
    m = jnp.maximum(jnp.max(s, axis=-1, keepdims=True), sink)
    p = jnp.exp(s - m)
    es = jnp.exp(sink - m)
    denom = jnp.sum(p, axis=-1, keepdims=True) + es
    return p / denom, es / denom


def _attn_specs(tile):
    nb = TQ // BLK
    prev = lambda i: jnp.maximum(tile(i) * nb - 1, 0)
    return [
        pl.BlockSpec(memory_space=pltpu.SMEM),
        pl.BlockSpec((TQ, ATTN_W), lambda i: (tile(i), 0)),
        pl.BlockSpec((TQ, ATTN_W), lambda i: (tile(i), 1)),
        pl.BlockSpec((TQ, 2 * KV_W), lambda i: (tile(i), 4)),
        pl.BlockSpec((BLK, 2 * KV_W), lambda i: (prev(i), 4)),
        pl.BlockSpec((TQ, 384), lambda i: (tile(i), 0)),
        pl.BlockSpec((BLK, 384), lambda i: (prev(i), 0)),
    ]


def _attn_fwd(pa, tab, sinks):
    S = pa.shape[0]
    nb = TQ // BLK

    def body(sink_ref, q_ref, g_ref, kvc_ref, kvp_ref, tabc_ref, tabp_ref, o_ref, kall):
        i = pl.program_id(0)
        _fill_kv(kall, kvc_ref, kvp_ref, tabc_ref, tabp_ref)
        lane = lax.broadcasted_iota(jnp.int32, (BLK, 128), 1)
        half = [lane < HEAD_DIM, lane >= HEAD_DIM]
        for j in range(nb):
            mask = _band_mask(i == 0 if j == 0 else None)
            rq = slice(j * BLK, (j + 1) * BLK)
            rk = slice(j * BLK, (j + 2) * BLK)
            tab = tabc_ref[rq, :]
            for p in range(4):
                cols = slice(p * 128, (p + 1) * 128)
                qr = _rope(q_ref[rq, cols].astype(F32), tab)
                acc = jnp.zeros((BLK, 128), F32)
                for e in range(2):
                    h = 2 * p + e
                    swap = 0 if e == h // GROUP else 1
                    qm = jnp.where(half[e], qr, 0.0).astype(BF16)
                    prob, _ = _softmax(qm, kall[swap, rk, :], mask, sink_ref[h])
                    o = _nn(prob.astype(BF16), kall[2 + swap, rk, :])
                    acc = acc + jnp.where(half[e], o, 0.0)
                o_ref[rq, cols] = (acc * _silu(g_ref[rq, cols].astype(F32))).astype(BF16)

    return pl.pallas_call(
        body,
        name="attn_fwd",
        grid=(S // TQ,),
        in_specs=_attn_specs(lambda i: i),
        out_specs=pl.BlockSpec((TQ, ATTN_W), lambda i: (i, 0)),
        out_shape=jax.ShapeDtypeStruct((S, ATTN_W), BF16),
        scratch_shapes=[pltpu.VMEM((4, BLK + TQ, 128), BF16)],
        compiler_params=_params(("arbitrary",)),
    )(sinks, pa, pa, pa, pa, tab, tab)


def _shift_down(u, halo_ref, has_prev):
    def halo_u(r):
        hu = halo_ref[r:r + 1, 512:1024].astype(F32) * halo_ref[r:r + 1, 1024:1536].astype(F32)
        return jnp.where(has_prev, hu, 0.0)

    row = lax.broadcasted_iota(jnp.int32, u.shape, 0)
    um1 = jnp.where(row == 0, halo_u(7), pltpu.roll(u, 1, 0))
    um2 = jnp.where(row == 0, halo_u(6), jnp.where(row == 1, halo_u(7), pltpu.roll(u, 2, 0)))
    return um1, um2


def _conv_fwd(pc, conv_w):
    S = pc.shape[0]

    def body(pc_ref, halo_ref, w_ref, o_ref):
        b = pc_ref[:, 0:512].astype(F32)
        u = pc_ref[:, 512:1024].astype(F32) * pc_ref[:, 1024:1536].astype(F32)
        gc = pc_ref[:, 1536:2048].astype(F32)
        um1, um2 = _shift_down(u, halo_ref, pl.program_id(0) > 0)
        cv = w_ref[0:1, :] * um2 + w_ref[1:2, :] * um1 + w_ref[2:3, :] * u
        o_ref[...] = (b * cv * _silu(gc)).astype(BF16)

    return pl.pallas_call(
        body,
        name="conv_fwd",
        grid=(S // TC,),
        in_specs=[
            pl.BlockSpec((TC, PC_W), lambda i: (i, 0)),
            pl.BlockSpec((8, PC_W), lambda i: (jnp.maximum(i * (TC // 8) - 1, 0), 0)),
            pl.BlockSpec((CONV_K, CONV_W), lambda i: (0, 0)),
        ],
        out_specs=pl.BlockSpec((TC, CONV_W), lambda i: (i, 0)),
        out_shape=jax.ShapeDtypeStruct((S, CONV_W), BF16),
        compiler_params=_params(("arbitrary",)),
    )(pc, pc, conv_w)


def _out_loss(x, target, ya, yc, w_out, final_g):
    S = x.shape[0]

    def body(x_ref, t_ref, ya_ref, yc_ref, wo_ref, fg_ref, dh_ref, dmix_ref, gwo_ref, gfg_ref, loss_ref):
        @pl.when(pl.program_id(0) == 0)
        def _():
            gwo_ref[...] = jnp.zeros_like(gwo_ref)
            gfg_ref[...] = jnp.zeros_like(gfg_ref)
            loss_ref[...] = jnp.zeros_like(loss_ref)

        mix = jnp.concatenate([ya_ref[...], yc_ref[...]], axis=1)
        wo = wo_ref[...]
        fg = fg_ref[...]
        h = x_ref[...] + _nn(mix, wo)
        r = lax.rsqrt(jnp.mean(h * h, axis=-1, keepdims=True) + EPS)
        n = h * r
        err = n * fg - t_ref[...]
        loss_ref[...] += 0.5 * jnp.sum(jnp.mean(err * err, axis=-1, keepdims=True), axis=0, keepdims=True)
        dy = err * (1.0 / D_MODEL)
        gfg_ref[...] += jnp.sum(dy * n, axis=0, keepdims=True)
        dyg = dy * fg
        dh = r * (dyg - n * jnp.mean(dyg * n, axis=-1, keepdims=True))
        dh_ref[...] = dh
        dhb = dh.astype(BF16)
        dmix_ref[...] = _nt(dhb, wo).astype(ACT)
        gwo_ref[...] += _tn(mix, dhb)

    row = lambda i: (i, 0)
    fixed = lambda i: (0, 0)
    return pl.pallas_call(
        body,
        name="out_loss",
        grid=(S // TM,),
        in_specs=[
            pl.BlockSpec((TM, D_MODEL), row),
            pl.BlockSpec((TM, D_MODEL), row),
            pl.BlockSpec((TM, ATTN_W), row),
            pl.BlockSpec((TM, CONV_W), row),
            pl.BlockSpec((D_MODEL, D_MODEL), fixed),
            pl.BlockSpec((1, D_MODEL), fixed),
        ],
        out_specs=[
            pl.BlockSpec((TM, D_MODEL), row),
            pl.BlockSpec((TM, D_MODEL), row),
            pl.BlockSpec((D_MODEL, D_MODEL), fixed),
            pl.BlockSpec((1, D_MODEL), fixed),
            pl.BlockSpec((1, 1), fixed),
        ],
        out_shape=[
            jax.ShapeDtypeStruct((S, D_MODEL), F32),
            jax.ShapeDtypeStruct((S, D_MODEL), ACT),
            jax.ShapeDtypeStruct((D_MODEL, D_MODEL), F32),
            jax.ShapeDtypeStruct((1, D_MODEL), F32),
            jax.ShapeDtypeStruct((1, 1), F32),
        ],
        compiler_params=_params(("arbitrary",)),
    )(x, target, ya, yc, w_out, final_g)


def _attn_bwd(pa, dmix, tab, sinks):
    S = pa.shape[0]
    nt = S // TQ
    nb = TQ // BLK

    def body(sink_ref, q_ref, g_ref, kvc_ref, kvp_ref, tabc_ref, tabp_ref, dm_ref,
             d_ref, dsink_ref, kall, dkv, carry):
        step = pl.program_id(0)
        i = nt - 1 - step

        @pl.when(step == 0)
        def _():
            carry[...] = jnp.zeros_like(carry)
            dsink_ref[...] = jnp.zeros_like(dsink_ref)

        _fill_kv(kall, kvc_ref, kvp_ref, tabc_ref, tabp_ref)
        dkv[0:TQ, :] = jnp.zeros((TQ, 2 * KV_W), F32)
        dkv[TQ:TQ + BLK, :] = carry[...]
        lane = lax.broadcasted_iota(jnp.int32, (BLK, 128), 1)
        half = [lane < HEAD_DIM, lane >= HEAD_DIM]
        for j in range(nb):
            mask = _band_mask(i == 0 if j == 0 else None)
            rq = slice(j * BLK, (j + 1) * BLK)
            rk = slice(j * BLK, (j + 2) * BLK)
            tab = tabc_ref[rq, :]
            dk_blk = jnp.zeros((2 * BLK, 128), F32)
            dv_blk = jnp.zeros((2 * BLK, 128), F32)
            for p in range(4):
                cols = slice(p * 128, (p + 1) * 128)
                qr = _rope(q_ref[rq, cols].astype(F32), tab)
                g = g_ref[rq, cols].astype(F32)
                da = dm_ref[rq, cols].astype(F32)
                do = da * _silu(g)
                o_pair = jnp.zeros((BLK, 128), F32)
                dq_pair = jnp.zeros((BLK, 128), F32)
                for e in range(2):
                    h = 2 * p + e
                    swap = 0 if e == h // GROUP else 1
                    kk = kall[swap, rk, :]
                    vv = kall[2 + swap, rk, :]
                    qm = jnp.where(half[e], qr, 0.0).astype(BF16)
                    prob, psink = _softmax(qm, kk, mask, sink_ref[h])
                    pb = prob.astype(BF16)
                    o_pair = o_pair + jnp.where(half[e], _nn(pb, vv), 0.0)
                    doh = jnp.where(half[e], do, 0.0).astype(BF16)
                    dprob = _nt(doh, vv)
                    dsum = jnp.sum(dprob * prob, axis=-1, keepdims=True)
                    ds = ((prob * (dprob - dsum)) * 0.125).astype(BF16)
                    dq_pair = dq_pair + jnp.where(half[e], _nn(ds, kk), 0.0)
                    dkh = _tn(ds, qm)
                    dvh = _tn(pb, doh)
                    if swap:
                        dkh = pltpu.roll(dkh, 64, 1)
                        dvh = pltpu.roll(dvh, 64, 1)
                    dk_blk = dk_blk + dkh
                    dv_blk = dv_blk + dvh
                    dsink_ref[h:h + 1, :] += jnp.broadcast_to(
                        -jnp.sum(psink * dsum, axis=0, keepdims=True), (1, 128))
                d_ref[rq, cols] = _rope_t(dq_pair, tab).astype(BF16)
                d_ref[rq, 512 + p * 128:512 + (p + 1) * 128] = (da * o_pair * _dsilu(g)).astype(BF16)
            dkv[rk, 0:128] += dk_blk
            dkv[rk, 128:256] += dv_blk
        d_ref[:, 1024:1152] = _rope_t(dkv[BLK:BLK + TQ, 0:128], tabc_ref[...]).astype(BF16)
        d_ref[:, 1152:1280] = dkv[BLK:BLK + TQ, 128:256].astype(BF16)
        carry[...] = dkv[0:BLK, :]

    rev = lambda s: nt - 1 - s
    return pl.pallas_call(
        body,
        name="attn_bwd",
        grid=(nt,),
        in_specs=_attn_specs(rev) + [pl.BlockSpec((TQ, ATTN_W), lambda s: (nt - 1 - s, 0))],
        out_specs=[
            pl.BlockSpec((TQ, PA_W), lambda s: (nt - 1 - s, 0)),
            pl.BlockSpec((8, 128), lambda s: (0, 0)),
        ],
        out_shape=[
            jax.ShapeDtypeStruct((S, PA_W), BF16),
            jax.ShapeDtypeStruct((8, 128), F32),
        ],
        scratch_shapes=[
            pltpu.VMEM((4, BLK + TQ, 128), BF16),
            pltpu.VMEM((BLK + TQ, 2 * KV_W), F32),
            pltpu.VMEM((BLK, 2 * KV_W), F32),
        ],
        compiler_params=_params(("arbitrary",)),
    )(sinks, pa, pa, pa, pa, tab, tab, dmix)


def _conv_bwd(pc, dmix, conv_w):
    S = pc.shape[0]
    nt = S // TC

    def body(pc_ref, prev_ref, next_ref, dm_ref, dmn_ref, w_ref, d_ref, gw_ref):
        i = pl.program_id(0)

        @pl.when(i == 0)
        def _():
            gw_ref[...] = jnp.zeros_like(gw_ref)

        def parts(ref):
            return (ref[:, 0:512].astype(F32), ref[:, 512:1024].astype(F32),
                    ref[:, 1024:1536].astype(F32), ref[:, 1536:2048].astype(F32))

        w0, w1, w2 = w_ref[0:1, :], w_ref[1:2, :], w_ref[2:3, :]
        b, c, hh, gc = parts(pc_ref)
        u = c * hh
        um1, um2 = _shift_down(u, prev_ref, i > 0)
        cv = w0 * um2 + w1 * um1 + w2 * u
        sg = _silu(gc)
        dy = dm_ref[...].astype(F32)
        dcv = dy * b * sg

        def next_dcv(r):
            nd = (dmn_ref[r:r + 1, :].astype(F32) * next_ref[r:r + 1, 0:512].astype(F32)
                  * _silu(next_ref[r:r + 1, 1536:2048].astype(F32)))
            return jnp.where(i < nt - 1, nd, 0.0)

        row = lax.broadcasted_iota(jnp.int32, (TC, CONV_W), 0)
        dp1 = jnp.where(row == TC - 1, next_dcv(0), pltpu.roll(dcv, TC - 1, 0))
        dp2 = jnp.where(row == TC - 1, next_dcv(1),
                        jnp.where(row == TC - 2, next_dcv(0), pltpu.roll(dcv, TC - 2, 0)))
        du = w2 * dcv + w1 * dp1 + w0 * dp2
        d_ref[:, 0:512] = (dy * cv * sg).astype(BF16)
        d_ref[:, 512:1024] = (du * hh).astype(BF16)
        d_ref[:, 1024:1536] = (du * c).astype(BF16)
        d_ref[:, 1536:2048] = (dy * b * cv * _dsilu(gc)).astype(BF16)
        gw_ref[0:1, :] += jnp.sum(dcv * um2, axis=0, keepdims=True)
        gw_ref[1:2, :] += jnp.sum(dcv * um1, axis=0, keepdims=True)
        gw_ref[2:3, :] += jnp.sum(dcv * u, axis=0, keepdims=True)

    t8 = TC // 8
    return pl.pallas_call(
        body,
        name="conv_bwd",
        grid=(nt,),
        in_specs=[
            pl.BlockSpec((TC, PC_W), lambda i: (i, 0)),
            pl.BlockSpec((8, PC_W), lambda i: (jnp.maximum(i * t8 - 1, 0), 0)),
            pl.BlockSpec((8, PC_W), lambda i: (jnp.minimum((i + 1) * t8, nt * t8 - 1), 0)),
            pl.BlockSpec((TC, CONV_W), lambda i: (i, 1)),
            pl.BlockSpec((8, CONV_W), lambda i: (jnp.minimum((i + 1) * t8, nt * t8 - 1), 1)),
            pl.BlockSpec((CONV_K, CONV_W), lambda i: (0, 0)),
        ],
        out_specs=[
            pl.BlockSpec((TC, PC_W), lambda i: (i, 0)),
            pl.BlockSpec((CONV_K, CONV_W), lambda i: (0, 0)),
        ],
        out_shape=[
            jax.ShapeDtypeStruct((S, PC_W), BF16),
            jax.ShapeDtypeStruct((CONV_K, CONV_W), F32),
        ],
        compiler_params=_params(("arbitrary",)),
    )(pc, pc, pc, dmix, dmix, conv_w)


def _grad_x(da, dc, wt, x, dh, norm_g):
    S = x.shape[0]

    def body(da_ref, dc_ref, wt_ref, x_ref, dh_ref, g_ref, gx_ref, gng_ref):
        @pl.when(pl.program_id(0) == 0)
        def _():
            gng_ref[...] = jnp.zeros_like(gng_ref)

        dxn = (_nn(da_ref[:, 0:512], wt_ref[0:512, :]) + _nn(da_ref[:, 512:1024], wt_ref[768:1280, :])
               + _nn(da_ref[:, 1024:1280], wt_ref[512:768, :]) + _nn(dc_ref[...], wt_ref[1280:3328, :]))
        xv = x_ref[...]
        r = lax.rsqrt(jnp.mean(xv * xv, axis=-1, keepdims=True) + EPS)
        n = xv * r
        gng_ref[...] += jnp.sum(dxn * n, axis=0, keepdims=True)
        dxg = dxn * g_ref[...]
        gx_ref[...] = dh_ref[...] + r * (dxg - n * jnp.mean(dxg * n, axis=-1, keepdims=True))

    row = lambda i: (i, 0)
    fixed = lambda i: (0, 0)
    return pl.pallas_call(
        body,
        name="grad_x",
        grid=(S // TM,),
        in_specs=[
            pl.BlockSpec((TM, PA_W), row),
            pl.BlockSpec((TM, PC_W), row),
            pl.BlockSpec((IN_W, D_MODEL), fixed),
            pl.BlockSpec((TM, D_MODEL), row),
            pl.BlockSpec((TM, D_MODEL), row),
            pl.BlockSpec((1, D_MODEL), fixed),
        ],
        out_specs=[pl.BlockSpec((TM, D_MODEL), row), pl.BlockSpec((1, D_MODEL), fixed)],
        out_shape=[jax.ShapeDtypeStruct((S, D_MODEL), F32), jax.ShapeDtypeStruct((1, D_MODEL), F32)],
        compiler_params=_params(("arbitrary",)),
    )(da, dc, wt, x, dh, norm_g)


def _grad_w_in(da, dc, xn):
    S = xn.shape[0]

    def body(da_ref, dc_ref, xn_ref, gw_ref):
        @pl.when(pl.program_id(0) == 0)
        def _():
            gw_ref[...] = jnp.zeros_like(gw_ref)

        xn = xn_ref[...]
        gw_ref[0:512, :] += _tn(da_ref[:, 0:512], xn)
        gw_ref[768:1280, :] += _tn(da_ref[:, 512:1024], xn)
        gw_ref[512:768, :] += _tn(da_ref[:, 1024:1280], xn)
        gw_ref[1280:3328, :] += _tn(dc_ref[...], xn)

    row = lambda i: (i, 0)
    return pl.pallas_call(
        body,
        name="grad_w_in",
        grid=(S // TM,),
        in_specs=[pl.BlockSpec((TM, PA_W), row), pl.BlockSpec((TM, PC_W), row), pl.BlockSpec((TM, D_MODEL), row)],
        out_specs=pl.BlockSpec((IN_W, D_MODEL), lambda i: (0, 0)),
        out_shape=jax.ShapeDtypeStruct((IN_W, D_MODEL), F32),
        compiler_params=_params(("arbitrary",)),
    )(da, dc, xn)


def _sum_parts(parts, name):
    n = parts.shape[0]

    def body(p_ref, o_ref):
        acc = p_ref[0]
        for k in range(1, n):
            acc = acc + p_ref[k]
        o_ref[...] = acc

    return pl.pallas_call(
        body,
        name=name,
        out_shape=jax.ShapeDtypeStruct(parts.shape[1:], F32),
        compiler_params=_params(),
    )(parts)


def _adamw(w, g, m, v, name):
    c1 = 1.0 - ADAM_B1 ** ADAM_STEP
    c2 = 1.0 - ADAM_B2 ** ADAM_STEP

    def body(w_ref, g_ref, m_ref, v_ref, d_ref, nm_ref, nv_ref):
        g = g_ref[...]
        nm = ADAM_B1 * m_ref[...] + (1.0 - ADAM_B1) * g
        nv = ADAM_B2 * v_ref[...] + (1.0 - ADAM_B2) * (g * g)
        nm_ref[...] = nm
        nv_ref[...] = nv
        d_ref[...] = -ADAM_LR * ((nm / c1) / (jnp.sqrt(nv / c2) + ADAM_EPS) + ADAM_WD * w_ref[...])

    shape = jax.ShapeDtypeStruct(w.shape, F32)
    return pl.pallas_call(
        body,
        name=name,
        out_shape=[shape, shape, shape],
        compiler_params=_params(),
    )(w, g, m, v)


def _rope_table(S):
    pos = jnp.arange(S, dtype=jnp.int32).astype(F32)
    inv_freq = ROPE_THETA ** (-jnp.arange(0, ROT_DIM, 2, dtype=F32) / ROT_DIM)
    ang = pos[:, None] * inv_freq[None, :]
    cos, sin = jnp.cos(ang), jnp.sin(ang)
    z8 = jnp.zeros((S, 8), F32)
    c64 = jnp.concatenate([cos, cos, jnp.ones((S, 48), F32)], axis=1)
    s1 = jnp.concatenate([-sin, z8, jnp.zeros((S, 48), F32)], axis=1)
    s2 = jnp.concatenate([z8, sin, jnp.zeros((S, 48), F32)], axis=1)
    return jnp.concatenate([c64, c64, s1, s1, s2, s2], axis=1)


def kernel(x, norm_g, w_in, sinks, conv_w, w_out, final_g, loss_target, m_norm_g, m_w_in, m_sinks, m_conv_w, m_w_out, m_final_g, v_norm_g, v_w_in, v_sinks, v_conv_w, v_w_out, v_final_g):
    S = x.shape[1]
    me = 4 * lax.axis_index("x") + 2 * lax.axis_index("y") + lax.axis_index("c")
    x2 = x.reshape(S, D_MODEL)
    t2 = loss_target.reshape(S, D_MODEL)
    ng = norm_g.reshape(1, D_MODEL)
    fg = final_g.reshape(1, D_MODEL)

    cw_pad = jnp.zeros((8, 128), F32).at[0:CONV_K, 0:64].set(conv_w)
    wt, wo, cw_all = _all_gather([w_in.T.astype(BF16), w_out.astype(BF16), cw_pad], "all_gather_weights")
    cw = cw_all.reshape(N_DEV, 8, 128)[:, 0:CONV_K, 0:64].transpose(1, 0, 2).reshape(CONV_K, CONV_W)

    tab = _rope_table(S)
    xn, pa, pc = _fwd_proj(x2, ng, wt)
    ya = _attn_fwd(pa, tab, sinks)
    yc = _conv_fwd(pc, cw)
    dh, dmix, g_wo, g_fg, loss_part = _out_loss(x2, t2, ya, yc, wo, fg)
    da, g_sinks = _attn_bwd(pa, dmix, tab, sinks)
    dc, g_cw = _conv_bwd(pc, dmix, cw)
    grad_x, g_ng = _grad_x(da, dc, wt, x2, dh, ng)
    g_wt = _grad_w_in(da, dc, xn)

    r_wt, r_wo = _reduce_scatter([g_wt.reshape(N_DEV, SHARD_IN, D_MODEL), g_wo.reshape(N_DEV, SHARD_OUT, D_MODEL)])
    grad_w_in = _sum_parts(r_wt, "sum_grad_w_in").T
    grad_w_out = _sum_parts(r_wo, "sum_grad_w_out")
    small = jnp.concatenate([
        g_ng.reshape(8, 128), g_fg.reshape(8, 128), g_sinks,
        g_cw.reshape(12, 128), jnp.zeros((4, 128), F32)], axis=0)
    (small_all,) = _all_gather([small], "all_gather_small_grads")
    small_sum = _sum_parts(small_all.reshape(N_DEV, SMALL_ROWS, 128), "sum_small_grads")
    grad_norm_g = small_sum[0:8].reshape(D_MODEL)
    grad_final_g = small_sum[8:16].reshape(D_MODEL)
    grad_sinks = small_sum[16:24, 0]
    grad_conv_w = lax.dynamic_slice(small_sum[24:36].reshape(CONV_K, CONV_W), (0, me * 64), (CONV_K, 64))

    loss = lax.psum(loss_part[0, 0], ("x", "y", "c"))

    d_w_in, nm_w_in, nv_w_in = _adamw(w_in, grad_w_in, m_w_in, v_w_in, "adamw_w_in")
    d_w_out, nm_w_out, nv_w_out = _adamw(w_out, grad_w_out, m_w_out, v_w_out, "adamw_w_out")
    def pack(a, b, c_, d):
        return jnp.concatenate([
            a.reshape(8, 128), b.reshape(8, 128),
            jnp.zeros((8, 128), F32).at[0, 0:8].set(c_).at[1:1 + CONV_K, 0:64].set(d)], axis=1)

    d_s, nm_s, nv_s = _adamw(
        pack(norm_g, final_g, sinks, conv_w), pack(grad_norm_g, grad_final_g, grad_sinks, grad_conv_w),
        pack(m_norm_g, m_final_g, m_sinks, m_conv_w),
        pack(v_norm_g, v_final_g, v_sinks, v_conv_w),
        "adamw_small")

    def unpack(p):
        return (p[:, 0:128].reshape(D_MODEL), p[:, 128:256].reshape(D_MODEL), p[0, 256:264], p[1:1 + CONV_K, 256:320])

    d_ng, d_fg, d_sk, d_cw = unpack(d_s)
    nm_ng, nm_fg, nm_sk, nm_cw = unpack(nm_s)
    nv_ng, nv_fg, nv_sk, nv_cw = unpack(nv_s)

    return (loss, grad_x.reshape(1, S, D_MODEL), grad_norm_g, grad_w_in, grad_sinks, grad_conv_w, grad_w_out, grad_final_g,
            d_ng, d_w_in, d_sk, d_cw, d_w_out, d_fg,
            nm_ng, nm_w_in, nm_sk, nm_cw, nm_w_out, nm_fg,
            nv_ng, nv_w_in, nv_sk, nv_cw, nv_w_out, nv_fg)
```

```python
import numpy as np
import jax
import jax.numpy as jnp
from jax import lax
from jax.experimental import pallas as pl
from jax.experimental.pallas import tpu as pltpu

F32 = jnp.float32
BF16 = jnp.bfloat16
MESH = pl.DeviceIdType.MESH

D_MODEL = 1024
HEAD_DIM = 64
N_Q_HEADS = 8
GROUP = 4
ATTN_W = 512
KV_W = 128
BLK = 128
CONV_W = 512
CONV_K = 3
IN_W = 3328
PA_W = 1280
PC_W = 2048
EPS = 1e-5
ROPE_THETA = 500000.0
ROT_DIM = 16
N_DEV = 8
N_CHIP = 4
SHARD_IN = IN_W // N_DEV
SHARD_OUT = D_MODEL // N_DEV
SMALL_ROWS = 40

ADAM_LR = 0.001
ADAM_B1 = 0.9
ADAM_B2 = 0.999
ADAM_EPS = 1e-08
ADAM_WD = 0.01
ADAM_STEP = 10

ACT = jnp.bfloat16

TM = 512
TQ = 512
TC = 512
HALO = 16
VMEM_LIMIT = 56 * 1024 * 1024

NT_DIMS = (((1,), (1,)), ((), ()))
TN_DIMS = (((0,), (0,)), ((), ()))


def _params(sem=None):
    kw = dict(vmem_limit_bytes=VMEM_LIMIT)
    if sem is not None:
        kw["dimension_semantics"] = sem
    return pltpu.CompilerParams(**kw)


def _nt(a, b):
    return lax.dot_general(a, b, NT_DIMS, preferred_element_type=F32)


def _tn(a, b):
    return lax.dot_general(a, b, TN_DIMS, preferred_element_type=F32)


def _nn(a, b):
    return jnp.dot(a, b, preferred_element_type=F32)


def _silu(g):
    return g * jax.nn.sigmoid(g)


def _dsilu(g):
    s = jax.nn.sigmoid(g)
    return s * (1.0 + g * (1.0 - s))


def _all_gather(arrs, name):
    n_arr = len(arrs)

    def body(*refs):
        x_refs = refs[:n_arr]
        out_refs = refs[n_arr:2 * n_arr]
        send_sems, recv_sems, local_sems = refs[2 * n_arr:]
        x, y, c = lax.axis_index("x"), lax.axis_index("y"), lax.axis_index("c")
        me, sibling = (x, y, c), (x, y, 1 - c)
        chips = [(1 - x, y), (x, 1 - y), (1 - x, 1 - y)]

        def rows(a, px, py, pc):
            m = x_refs[a].shape[0]
            return out_refs[a].at[pl.ds((4 * px + 2 * py + pc) * m, m), :]

        def copy(a, k, block, to, src=None):
            return pltpu.make_async_remote_copy(
                src_ref=rows(a, *block) if src is None else src,
                dst_ref=rows(a, *block),
                send_sem=send_sems.at[a * 7 + k],
                recv_sem=recv_sems.at[a * 7 + k],
                device_id=to,
                device_id_type=MESH,
            )

        mine = [pltpu.make_async_copy(x_refs[a], rows(a, *me), local_sems.at[a]) for a in range(n_arr)]
        for cp in mine:
            cp.start()
        first = []
        for a in range(n_arr):
            first.append(copy(a, 0, me, sibling, src=x_refs[a]))
            first += [copy(a, 1 + j, me, (*chip, c), src=x_refs[a]) for j, chip in enumerate(chips)]
        for cp in first:
            cp.start()
        passed = []
        for j, chip in enumerate(chips):
            for a in range(n_arr):
                copy(a, 1 + j, (*chip, c), me).wait_recv()
                fwd = copy(a, 4 + j, (*chip, c), sibling)
                fwd.start()
                passed.append(fwd)
        for a in range(n_arr):
            copy(a, 0, sibling, me).wait_recv()
            for j, chip in enumerate(chips):
                copy(a, 4 + j, (*chip, 1 - c), me).wait_recv()
        for cp in first + passed:
            cp.wait_send()
        for cp in mine:
            cp.wait()

    vmem = pl.BlockSpec(memory_space=pltpu.VMEM)
    return pl.pallas_call(
        body,
        name=name,
        out_shape=[jax.ShapeDtypeStruct((N_DEV * a.shape[0], a.shape[1]), a.dtype) for a in arrs],
        in_specs=[vmem] * n_arr,
        out_specs=[vmem] * n_arr,
        scratch_shapes=[
            pltpu.SemaphoreType.DMA((7 * n_arr,)),
            pltpu.SemaphoreType.DMA((7 * n_arr,)),
            pltpu.SemaphoreType.DMA((n_arr,)),
        ],
        compiler_params=_params(),
    )(*arrs)


def _reduce_scatter(grads):
    n_arr = len(grads)

    def body(*refs):
        g_refs = refs[:n_arr]
        r_refs = refs[n_arr:2 * n_arr]
        a_refs = refs[2 * n_arr:3 * n_arr]
        p_bufs = refs[3 * n_arr:4 * n_arr]
        t_bufs = refs[4 * n_arr:5 * n_arr]
        sib_send, sib_recv, ici_send, ici_recv, load_sems, own_sems = refs[5 * n_arr:]
        x, y, c = lax.axis_index("x"), lax.axis_index("y"), lax.axis_index("c")
        sibling = (x, y, 1 - c)

        def chip_of(r):
            return (x ^ (r >> 1), y ^ (r & 1))

        def block_of(r, core):
            cx, cy = chip_of(r)
            return 4 * cx + 2 * cy + core

        to_sib = []
        for a in range(n_arr):
            for r in range(N_CHIP):
                cp = pltpu.make_async_remote_copy(
                    src_ref=g_refs[a].at[block_of(r, 1 - c)], dst_ref=a_refs[a].at[r],
                    send_sem=sib_send.at[a * N_CHIP + r], recv_sem=sib_recv.at[a * N_CHIP + r],
                    device_id=sibling, device_id_type=MESH)
                cp.start()
                to_sib.append(cp)
        loads = []
        for a in range(n_arr):
            for r in range(N_CHIP):
                cp = pltpu.make_async_copy(g_refs[a].at[block_of(r, c)], p_bufs[a].at[r],
                                           load_sems.at[(2 * a) * N_CHIP + r])
                cp.start()
                loads.append(cp)
        sends = []
        for r in (1, 2, 3, 0):
            for a in range(n_arr):
                k = a * N_CHIP + r
                to_sib[k].wait_recv()
                got = pltpu.make_async_copy(a_refs[a].at[r], t_bufs[a].at[r],
                                            load_sems.at[(2 * a + 1) * N_CHIP + r])
                got.start()
                loads[k].wait()
                got.wait()
                p_bufs[a][r] = p_bufs[a][r] + t_bufs[a][r]
                if r == 0:
                    cp = pltpu.make_async_copy(p_bufs[a].at[0], r_refs[a].at[0], own_sems.at[a])
                else:
                    cp = pltpu.make_async_remote_copy(
                        src_ref=p_bufs[a].at[r], dst_ref=r_refs[a].at[r],
                        send_sem=ici_send.at[a * N_CHIP + r], recv_sem=ici_recv.at[a * N_CHIP + r],
                        device_id=(*chip_of(r), c), device_id_type=MESH)
                cp.start()
                sends.append((r, cp))
        for r, cp in sends:
            if r == 0:
                cp.wait()
            else:
                cp.wait_send()
                cp.wait_recv()
        for cp in to_sib:
            cp.wait_send()

    any_spec = pl.BlockSpec(memory_space=pl.ANY)
    part = [jax.ShapeDtypeStruct((N_CHIP,) + g.shape[1:], F32) for g in grads]
    outs = pl.pallas_call(
        body,
        name="reduce_scatter_grads",
        out_shape=part + part,
        in_specs=[any_spec] * n_arr,
        out_specs=[any_spec] * (2 * n_arr),
        scratch_shapes=(
            [pltpu.VMEM((N_CHIP,) + g.shape[1:], F32) for g in grads]
            + [pltpu.VMEM((N_CHIP,) + g.shape[1:], F32) for g in grads]
            + [pltpu.SemaphoreType.DMA((n_arr * N_CHIP,))] * 4
            + [pltpu.SemaphoreType.DMA((2 * n_arr * N_CHIP,)), pltpu.SemaphoreType.DMA((n_arr,))]
        ),
        compiler_params=_params(),
    )(*grads)
    return outs[:n_arr]


def _fwd_proj(x, norm_g, wt):
    S = x.shape[0]

    def body(x_ref, g_ref, wt_ref, xn_ref, pa_ref, pc_ref):
        xv = x_ref[...]
        r = lax.rsqrt(jnp.mean(xv * xv, axis=-1, keepdims=True) + EPS)
        xn = (xv * r * g_ref[...]).astype(BF16)
        xn_ref[...] = xn
        pa_ref[:, 0:512] = _nt(xn, wt_ref[0:512, :]).astype(ACT)
        pa_ref[:, 512:1024] = _nt(xn, wt_ref[768:1280, :]).astype(ACT)
        pa_ref[:, 1024:1280] = _nt(xn, wt_ref[512:768, :]).astype(ACT)
        pc_ref[...] = _nt(xn, wt_ref[1280:3328, :]).astype(ACT)

    return pl.pallas_call(
        body,
        name="fwd_proj",
        grid=(S // TM,),
        in_specs=[
            pl.BlockSpec((TM, D_MODEL), lambda i: (i, 0)),
            pl.BlockSpec((1, D_MODEL), lambda i: (0, 0)),
            pl.BlockSpec((IN_W, D_MODEL), lambda i: (0, 0)),
        ],
        out_specs=[
            pl.BlockSpec((TM, D_MODEL), lambda i: (i, 0)),
            pl.BlockSpec((TM, PA_W), lambda i: (i, 0)),
            pl.BlockSpec((TM, PC_W), lambda i: (i, 0)),
        ],
        out_shape=[
            jax.ShapeDtypeStruct((S, D_MODEL), BF16),
            jax.ShapeDtypeStruct((S, PA_W), ACT),
            jax.ShapeDtypeStruct((S, PC_W), ACT),
        ],
        compiler_params=_params(("arbitrary",)),
    )(x, norm_g, wt)


def _rope(t, tab):
    return (t * tab[:, 0:128] + pltpu.roll(t, 120, 1) * tab[:, 128:256]
            + pltpu.roll(t, 8, 1) * tab[:, 256:384])


def _rope_t(d, tab):
    return (d * tab[:, 0:128] + pltpu.roll(d * tab[:, 128:256], 8, 1)
            + pltpu.roll(d * tab[:, 256:384], 120, 1))


def _fill_kv(kall, kvc_ref, kvp_ref, tabc_ref, tabp_ref):
    for lo, kv_ref, tab_ref, n in ((0, kvp_ref, tabp_ref, BLK), (BLK, kvc_ref, tabc_ref, TQ)):
        k = _rope(kv_ref[:, 0:128].astype(F32), tab_ref[...])
        v = kv_ref[:, 128:256].astype(F32)
        kall[0, lo:lo + n, :] = k.astype(BF16)
        kall[1, lo:lo + n, :] = pltpu.roll(k, 64, 1).astype(BF16)
        kall[2, lo:lo + n, :] = v.astype(BF16)
        kall[3, lo:lo + n, :] = pltpu.roll(v, 64, 1).astype(BF16)


HEADS = (((0, 0), (1, 0), (2, 1), (3, 1)), ((0, 1), (1, 1), (2, 0), (3, 0)))


def _upper():
    qi = lax.broadcasted_iota(jnp.int32, (BLK, BLK), 0)
    kj = lax.broadcasted_iota(jnp.int32, (BLK, BLK), 1)
    return kj > qi


def _merge(upper, both, rows):
    return jnp.where(upper, both[rows, 0:BLK], both[rows, BLK:2 * BLK])


def _split_store(ref, s, rows, upper, val):
    ref[s, rows, 0:BLK] = jnp.where(upper, val, 0.0).astype(BF16)
    ref[s, rows, BLK:2 * BLK] = jnp.where(upper, 0.0, val).astype(BF16)


def _stack_heads(ref, s, half, pairs):
    for a, (p, e) in enumerate(HEADS[s]):
        ref[s, a * BLK:(a + 1) * BLK, :] = jnp.where(half[e], pairs[p], 0.0).astype(BF16)


def _unstack_pair(half, outs, p):
    lo = 0 if p < 2 else 1
    rows = slice(p * BLK, (p + 1) * BLK)
    return jnp.where(half[0], outs[lo][rows, :], outs[1 - lo][rows, :])


def _softmax(sm, sink):
    m = jnp.maximum(jnp.max(sm, axis=-1, keepdims=True), sink)
    p = jnp.exp(sm - m)
    es = jnp.exp(sink - m)
    inv = 1.0 / (jnp.sum(p, axis=-1, keepdims=True) + es)
    return p * inv, es * inv


def _attn_specs(tile):
    nb = TQ // BLK
    prev = lambda i: jnp.maximum(tile(i) * nb - 1, 0)
    return [
        pl.BlockSpec(memory_space=pltpu.SMEM),
        pl.BlockSpec((TQ, ATTN_W), lambda i: (tile(i), 0)),
        pl.BlockSpec((TQ, ATTN_W), lambda i: (tile(i), 1)),
        pl.BlockSpec((TQ, 2 * KV_W), lambda i: (tile(i), 4)),
        pl.BlockSpec((BLK, 2 * KV_W), lambda i: (prev(i), 4)),
        pl.BlockSpec((TQ, 384), lambda i: (tile(i), 0)),
        pl.BlockSpec((BLK, 384), lambda i: (prev(i), 0)),
    ]


def _attn_fwd(pa, tab, sinks):
    S = pa.shape[0]
    nb = TQ // BLK

    def body(sink_ref, q_ref, g_ref, kvc_ref, kvp_ref, tabc_ref, tabp_ref, o_ref, kall, q_sc, p_sc):
        i = pl.program_id(0)
        _fill_kv(kall, kvc_ref, kvp_ref, tabc_ref, tabp_ref)
        lane = lax.broadcasted_iota(jnp.int32, (BLK, 128), 1)
        half = [lane < HEAD_DIM, lane >= HEAD_DIM]
        upper = _upper()
        for j in range(nb):
            rq = slice(j * BLK, (j + 1) * BLK)
            rk = slice(j * BLK, (j + 2) * BLK)
            tab = tabc_ref[rq, :]
            qr = [_rope(q_ref[rq, p * 128:(p + 1) * 128].astype(F32), tab) * 0.125 for p in range(4)]
            outs = []
            for s in range(2):
                _stack_heads(q_sc, s, half, qr)
                sf = _nt(q_sc[s], kall[s, rk, :])
                if j == 0:
                    sf = sf + jnp.where((i == 0) & (lax.broadcasted_iota(jnp.int32, (1, 2 * BLK), 1) < BLK),
                                        -jnp.inf, 0.0)
                for a, (p, e) in enumerate(HEADS[s]):
                    ra = slice(a * BLK, (a + 1) * BLK)
                    prob, _ = _softmax(_merge(upper, sf, ra), sink_ref[2 * p + e])
                    _split_store(p_sc, s, ra, upper, prob)
                outs.append(_nn(p_sc[s], kall[2 + s, rk, :]))
            for p in range(4):
                cols = slice(p * 128, (p + 1) * 128)
                o_ref[rq, cols] = (_unstack_pair(half, outs, p) * _silu(g_ref[rq, cols].astype(F32))).astype(BF16)

    return pl.pallas_call(
        body,
        name="attn_fwd",
        grid=(S // TQ,),
        in_specs=_attn_specs(lambda i: i),
        out_specs=pl.BlockSpec((TQ, ATTN_W), lambda i: (i, 0)),
        out_shape=jax.ShapeDtypeStruct((S, ATTN_W), BF16),
        scratch_shapes=[
            pltpu.VMEM((4, BLK + TQ, 128), BF16),
            pltpu.VMEM((2, 4 * BLK, 128), BF16),
            pltpu.VMEM((2, 4 * BLK, 2 * BLK), BF16),
        ],
        compiler_params=_params(("arbitrary",)),
    )(sinks, pa, pa, pa, pa, tab, tab)


def _shift_down(u, halo_ref, has_prev):
    def halo_u(r):
        hu = halo_ref[r:r + 1, 512:1024].astype(F32) * halo_ref[r:r + 1, 1024:1536].astype(F32)
        return jnp.where(has_prev, hu, 0.0)

    row = lax.broadcasted_iota(jnp.int32, u.shape, 0)
    um1 = jnp.where(row == 0, halo_u(HALO - 1), pltpu.roll(u, 1, 0))
    um2 = jnp.where(row == 0, halo_u(HALO - 2), jnp.where(row == 1, halo_u(HALO - 1), pltpu.roll(u, 2, 0)))
    return um1, um2


def _conv_fwd(pc, conv_w):
    S = pc.shape[0]

    def body(pc_ref, halo_ref, w_ref, o_ref):
        b = pc_ref[:, 0:512].astype(F32)
        u = pc_ref[:, 512:1024].astype(F32) * pc_ref[:, 1024:1536].astype(F32)
        gc = pc_ref[:, 1536:2048].astype(F32)
        um1, um2 = _shift_down(u, halo_ref, pl.program_id(0) > 0)
        cv = w_ref[0:1, :] * um2 + w_ref[1:2, :] * um1 + w_ref[2:3, :] * u
        o_ref[...] = (b * cv * _silu(gc)).astype(BF16)

    return pl.pallas_call(
        body,
        name="conv_fwd",
        grid=(S // TC,),
        in_specs=[
            pl.BlockSpec((TC, PC_W), lambda i: (i, 0)),
            pl.BlockSpec((HALO, PC_W), lambda i: (jnp.maximum(i * (TC // HALO) - 1, 0), 0)),
            pl.BlockSpec((CONV_K, CONV_W), lambda i: (0, 0)),
        ],
        out_specs=pl.BlockSpec((TC, CONV_W), lambda i: (i, 0)),
        out_shape=jax.ShapeDtypeStruct((S, CONV_W), BF16),
        compiler_params=_params(("arbitrary",)),
    )(pc, pc, conv_w)


def _out_loss(x, target, ya, yc, w_out, final_g):
    S = x.shape[0]

    def body(x_ref, t_ref, ya_ref, yc_ref, wo_ref, fg_ref, dh_ref, dmix_ref, gwo_ref, gfg_ref, loss_ref):
        @pl.when(pl.program_id(0) == 0)
        def _():
            gwo_ref[...] = jnp.zeros_like(gwo_ref)
            gfg_ref[...] = jnp.zeros_like(gfg_ref)
            loss_ref[...] = jnp.zeros_like(loss_ref)

        mix = jnp.concatenate([ya_ref[...], yc_ref[...]], axis=1)
        wo = wo_ref[...]
        fg = fg_ref[...]
        h = x_ref[...] + _nn(mix, wo)
        r = lax.rsqrt(jnp.mean(h * h, axis=-1, keepdims=True) + EPS)
        n = h * r
        err = n * fg - t_ref[...]
        loss_ref[...] += 0.5 * jnp.sum(jnp.mean(err * err, axis=-1, keepdims=True), axis=0, keepdims=True)
        dy = err * (1.0 / D_MODEL)
        gfg_ref[...] += jnp.sum(dy * n, axis=0, keepdims=True)
        dyg = dy * fg
        dh = r * (dyg - n * jnp.mean(dyg * n, axis=-1, keepdims=True))
        dh_ref[...] = dh
        dhb = dh.astype(BF16)
        dmix_ref[...] = _nt(dhb, wo).astype(ACT)
        gwo_ref[...] += _tn(mix, dhb)

    row = lambda i: (i, 0)
    fixed = lambda i: (0, 0)
    return pl.pallas_call(
        body,
        name="out_loss",
        grid=(S // TM,),
        in_specs=[
            pl.BlockSpec((TM, D_MODEL), row),
            pl.BlockSpec((TM, D_MODEL), row),
            pl.BlockSpec((TM, ATTN_W), row),
            pl.BlockSpec((TM, CONV_W), row),
            pl.BlockSpec((D_MODEL, D_MODEL), fixed),
            pl.BlockSpec((1, D_MODEL), fixed),
        ],
        out_specs=[
            pl.BlockSpec((TM, D_MODEL), row),
            pl.BlockSpec((TM, D_MODEL), row),
            pl.BlockSpec((D_MODEL, D_MODEL), fixed),
            pl.BlockSpec((1, D_MODEL), fixed),
            pl.BlockSpec((1, 1), fixed),
        ],
        out_shape=[
            jax.ShapeDtypeStruct((S, D_MODEL), F32),
            jax.ShapeDtypeStruct((S, D_MODEL), ACT),
            jax.ShapeDtypeStruct((D_MODEL, D_MODEL), F32),
            jax.ShapeDtypeStruct((1, D_MODEL), F32),
            jax.ShapeDtypeStruct((1, 1), F32),
        ],
        compiler_params=_params(("arbitrary",)),
    )(x, target, ya, yc, w_out, final_g)


def _attn_bwd(pa, dmix, tab, sinks):
    S = pa.shape[0]
    nt = S // TQ
    nb = TQ // BLK

    def body(sink_ref, q_ref, g_ref, kvc_ref, kvp_ref, tabc_ref, tabp_ref, dm_ref,
             d_ref, dsink_ref, kall, dkv, carry, q_sc, do_sc, p_sc, ds_sc):
        step = pl.program_id(0)
        i = nt - 1 - step

        @pl.when(step == 0)
        def _():
            carry[...] = jnp.zeros_like(carry)
            dsink_ref[...] = jnp.zeros_like(dsink_ref)

        _fill_kv(kall, kvc_ref, kvp_ref, tabc_ref, tabp_ref)
        dkv[0:TQ, :] = jnp.zeros((TQ, 2 * KV_W), F32)
        dkv[TQ:TQ + BLK, :] = carry[...]
        lane = lax.broadcasted_iota(jnp.int32, (BLK, 128), 1)
        half = [lane < HEAD_DIM, lane >= HEAD_DIM]
        upper = _upper()
        for j in range(nb):
            rq = slice(j * BLK, (j + 1) * BLK)
            rk = slice(j * BLK, (j + 2) * BLK)
            tab = tabc_ref[rq, :]
            pair = [slice(p * 128, (p + 1) * 128) for p in range(4)]
            qr = [_rope(q_ref[rq, c].astype(F32), tab) * 0.125 for c in pair]
            g = [g_ref[rq, c].astype(F32) for c in pair]
            da = [dm_ref[rq, c].astype(F32) for c in pair]
            do = [da[p] * _silu(g[p]) for p in range(4)]
            outs, dqs, dks, dvs = [], [], [], []
            for s in range(2):
                kk = kall[s, rk, :]
                vv = kall[2 + s, rk, :]
                _stack_heads(q_sc, s, half, qr)
                _stack_heads(do_sc, s, half, do)
                sf = _nt(q_sc[s], kk)
                if j == 0:
                    sf = sf + jnp.where((i == 0) & (lax.broadcasted_iota(jnp.int32, (1, 2 * BLK), 1) < BLK),
                                        -jnp.inf, 0.0)
                dpf = _nt(do_sc[s], vv)
                for a, (p, e) in enumerate(HEADS[s]):
                    h = 2 * p + e
                    ra = slice(a * BLK, (a + 1) * BLK)
                    prob, psink = _softmax(_merge(upper, sf, ra), sink_ref[h])
                    _split_store(p_sc, s, ra, upper, prob)
                    dprob = _merge(upper, dpf, ra)
                    dsum = jnp.sum(dprob * prob, axis=-1, keepdims=True)
                    _split_store(ds_sc, s, ra, upper, prob * (dprob - dsum))
                    dsink_ref[h:h + 1, :] += jnp.broadcast_to(
                        -jnp.sum(psink * dsum, axis=0, keepdims=True), (1, 128))
                outs.append(_nn(p_sc[s], vv))
                dqs.append(_nn(ds_sc[s], kk))
                dks.append(_tn(ds_sc[s], q_sc[s]))
                dvs.append(_tn(p_sc[s], do_sc[s]))
            for p in range(4):
                d_ref[rq, pair[p]] = _rope_t(_unstack_pair(half, dqs, p) * 0.125, tab).astype(BF16)
                d_ref[rq, 512 + p * 128:512 + (p + 1) * 128] = (
                    da[p] * _unstack_pair(half, outs, p) * _dsilu(g[p])).astype(BF16)
            dkv[rk, 0:128] += dks[0] + pltpu.roll(dks[1], 64, 1)
            dkv[rk, 128:256] += dvs[0] + pltpu.roll(dvs[1], 64, 1)
        d_ref[:, 1024:1152] = _rope_t(dkv[BLK:BLK + TQ, 0:128], tabc_ref[...]).astype(BF16)
        d_ref[:, 1152:1280] = dkv[BLK:BLK + TQ, 128:256].astype(BF16)
        carry[...] = dkv[0:BLK, :]

    rev = lambda s: nt - 1 - s
    return pl.pallas_call(
        body,
        name="attn_bwd",
        grid=(nt,),
        in_specs=_attn_specs(rev) + [pl.BlockSpec((TQ, ATTN_W), lambda s: (nt - 1 - s, 0))],
        out_specs=[
            pl.BlockSpec((TQ, PA_W), lambda s: (nt - 1 - s, 0)),
            pl.BlockSpec((8, 128), lambda s: (0, 0)),
        ],
        out_shape=[
            jax.ShapeDtypeStruct((S, PA_W), BF16),
            jax.ShapeDtypeStruct((8, 128), F32),
        ],
        scratch_shapes=[
            pltpu.VMEM((4, BLK + TQ, 128), BF16),
            pltpu.VMEM((BLK + TQ, 2 * KV_W), F32),
            pltpu.VMEM((BLK, 2 * KV_W), F32),
            pltpu.VMEM((2, 4 * BLK, 128), BF16),
            pltpu.VMEM((2, 4 * BLK, 128), BF16),
            pltpu.VMEM((2, 4 * BLK, 2 * BLK), BF16),
            pltpu.VMEM((2, 4 * BLK, 2 * BLK), BF16),
        ],
        compiler_params=_params(("arbitrary",)),
    )(sinks, pa, pa, pa, pa, tab, tab, dmix)


def _conv_bwd(pc, dmix, conv_w):
    S = pc.shape[0]
    nt = S // TC

    def body(pc_ref, prev_ref, next_ref, dm_ref, dmn_ref, w_ref, d_ref, gw_ref):
        i = pl.program_id(0)

        @pl.when(i == 0)
        def _():
            gw_ref[...] = jnp.zeros_like(gw_ref)

        def parts(ref):
            return (ref[:, 0:512].astype(F32), ref[:, 512:1024].astype(F32),
                    ref[:, 1024:1536].astype(F32), ref[:, 1536:2048].astype(F32))

        w0, w1, w2 = w_ref[0:1, :], w_ref[1:2, :], w_ref[2:3, :]
        b, c, hh, gc = parts(pc_ref)
        u = c * hh
        um1, um2 = _shift_down(u, prev_ref, i > 0)
        cv = w0 * um2 + w1 * um1 + w2 * u
        sg = _silu(gc)
        dy = dm_ref[...].astype(F32)
        dcv = dy * b * sg

        def next_dcv(r):
            nd = (dmn_ref[r:r + 1, :].astype(F32) * next_ref[r:r + 1, 0:512].astype(F32)
                  * _silu(next_ref[r:r + 1, 1536:2048].astype(F32)))
            return jnp.where(i < nt - 1, nd, 0.0)

        row = lax.broadcasted_iota(jnp.int32, (TC, CONV_W), 0)
        dp1 = jnp.where(row == TC - 1, next_dcv(0), pltpu.roll(dcv, TC - 1, 0))
        dp2 = jnp.where(row == TC - 1, next_dcv(1),
                        jnp.where(row == TC - 2, next_dcv(0), pltpu.roll(dcv, TC - 2, 0)))
        du = w2 * dcv + w1 * dp1 + w0 * dp2
        d_ref[:, 0:512] = (dy * cv * sg).astype(BF16)
        d_ref[:, 512:1024] = (du * hh).astype(BF16)
        d_ref[:, 1024:1536] = (du * c).astype(BF16)
        d_ref[:, 1536:2048] = (dy * b * cv * _dsilu(gc)).astype(BF16)
        gw_ref[0:1, :] += jnp.sum(dcv * um2, axis=0, keepdims=True)
        gw_ref[1:2, :] += jnp.sum(dcv * um1, axis=0, keepdims=True)
        gw_ref[2:3, :] += jnp.sum(dcv * u, axis=0, keepdims=True)

    t8 = TC // HALO
    return pl.pallas_call(
        body,
        name="conv_bwd",
        grid=(nt,),
        in_specs=[
            pl.BlockSpec((TC, PC_W), lambda i: (i, 0)),
            pl.BlockSpec((HALO, PC_W), lambda i: (jnp.maximum(i * t8 - 1, 0), 0)),
            pl.BlockSpec((HALO, PC_W), lambda i: (jnp.minimum((i + 1) * t8, nt * t8 - 1), 0)),
            pl.BlockSpec((TC, CONV_W), lambda i: (i, 1)),
            pl.BlockSpec((HALO, CONV_W), lambda i: (jnp.minimum((i + 1) * t8, nt * t8 - 1), 1)),
            pl.BlockSpec((CONV_K, CONV_W), lambda i: (0, 0)),
        ],
        out_specs=[
            pl.BlockSpec((TC, PC_W), lambda i: (i, 0)),
            pl.BlockSpec((CONV_K, CONV_W), lambda i: (0, 0)),
        ],
        out_shape=[
            jax.ShapeDtypeStruct((S, PC_W), BF16),
            jax.ShapeDtypeStruct((CONV_K, CONV_W), F32),
        ],
        compiler_params=_params(("arbitrary",)),
    )(pc, pc, pc, dmix, dmix, conv_w)


def _grad_x(da, dc, wt, x, dh, norm_g):
    S = x.shape[0]

    def body(da_ref, dc_ref, wt_ref, x_ref, dh_ref, g_ref, gx_ref, gng_ref):
        @pl.when(pl.program_id(0) == 0)
        def _():
            gng_ref[...] = jnp.zeros_like(gng_ref)

        dxn = (_nn(da_ref[:, 0:512], wt_ref[0:512, :]) + _nn(da_ref[:, 512:1024], wt_ref[768:1280, :])
               + _nn(da_ref[:, 1024:1280], wt_ref[512:768, :]) + _nn(dc_ref[...], wt_ref[1280:3328, :]))
        xv = x_ref[...]
        r = lax.rsqrt(jnp.mean(xv * xv, axis=-1, keepdims=True) + EPS)
        n = xv * r
        gng_ref[...] += jnp.sum(dxn * n, axis=0, keepdims=True)
        dxg = dxn * g_ref[...]
        gx_ref[...] = dh_ref[...] + r * (dxg - n * jnp.mean(dxg * n, axis=-1, keepdims=True))

    row = lambda i: (i, 0)
    fixed = lambda i: (0, 0)
    return pl.pallas_call(
        body,
        name="grad_x",
        grid=(S // TM,),
        in_specs=[
            pl.BlockSpec((TM, PA_W), row),
            pl.BlockSpec((TM, PC_W), row),
            pl.BlockSpec((IN_W, D_MODEL), fixed),
            pl.BlockSpec((TM, D_MODEL), row),
            pl.BlockSpec((TM, D_MODEL), row),
            pl.BlockSpec((1, D_MODEL), fixed),
        ],
        out_specs=[pl.BlockSpec((TM, D_MODEL), row), pl.BlockSpec((1, D_MODEL), fixed)],
        out_shape=[jax.ShapeDtypeStruct((S, D_MODEL), F32), jax.ShapeDtypeStruct((1, D_MODEL), F32)],
        compiler_params=_params(("arbitrary",)),
    )(da, dc, wt, x, dh, norm_g)


def _grad_w_in(da, dc, xn):
    S = xn.shape[0]

    def body(da_ref, dc_ref, xn_ref, gw_ref):
        @pl.when(pl.program_id(0) == 0)
        def _():
            gw_ref[...] = jnp.zeros_like(gw_ref)

        xn = xn_ref[...]
        gw_ref[0:512, :] += _tn(da_ref[:, 0:512], xn)
        gw_ref[768:1280, :] += _tn(da_ref[:, 512:1024], xn)
        gw_ref[512:768, :] += _tn(da_ref[:, 1024:1280], xn)
        gw_ref[1280:3328, :] += _tn(dc_ref[...], xn)

    row = lambda i: (i, 0)
    return pl.pallas_call(
        body,
        name="grad_w_in",
        grid=(S // TM,),
        in_specs=[pl.BlockSpec((TM, PA_W), row), pl.BlockSpec((TM, PC_W), row), pl.BlockSpec((TM, D_MODEL), row)],
        out_specs=pl.BlockSpec((IN_W, D_MODEL), lambda i: (0, 0)),
        out_shape=jax.ShapeDtypeStruct((IN_W, D_MODEL), F32),
        compiler_params=_params(("arbitrary",)),
    )(da, dc, xn)


def _sum_parts(parts, name):
    n = parts.shape[0]

    def body(p_ref, o_ref):
        acc = p_ref[0]
        for k in range(1, n):
            acc = acc + p_ref[k]
        o_ref[...] = acc

    return pl.pallas_call(
        body,
        name=name,
        out_shape=jax.ShapeDtypeStruct(parts.shape[1:], F32),
        compiler_params=_params(),
    )(parts)


def _adamw(w, g, m, v, name):
    c1 = 1.0 - ADAM_B1 ** ADAM_STEP
    c2 = 1.0 - ADAM_B2 ** ADAM_STEP

    def body(w_ref, g_ref, m_ref, v_ref, d_ref, nm_ref, nv_ref):
        g = g_ref[...]
        nm = ADAM_B1 * m_ref[...] + (1.0 - ADAM_B1) * g
        nv = ADAM_B2 * v_ref[...] + (1.0 - ADAM_B2) * (g * g)
        nm_ref[...] = nm
        nv_ref[...] = nv
        d_ref[...] = -ADAM_LR * ((nm / c1) / (jnp.sqrt(nv / c2) + ADAM_EPS) + ADAM_WD * w_ref[...])

    shape = jax.ShapeDtypeStruct(w.shape, F32)
    return pl.pallas_call(
        body,
        name=name,
        out_shape=[shape, shape, shape],
        compiler_params=_params(),
    )(w, g, m, v)


def _rope_table(S):
    half = ROT_DIM // 2
    pos = jnp.arange(S, dtype=jnp.int32).astype(F32)
    inv_freq = ROPE_THETA ** (-jnp.arange(0, ROT_DIM, 2, dtype=F32) / ROT_DIM)
    ang = inv_freq[:, None] * pos[None, :]
    cs = jnp.concatenate([jnp.cos(ang), jnp.sin(ang)], axis=0)
    sel = np.zeros((2 * half, 384), np.float32)
    ones = np.zeros((1, 384), np.float32)
    for l in range(128):
        r = l % HEAD_DIM
        if r < half:
            sel[r, l] = 1.0
            sel[half + r, 128 + l] = -1.0
        elif r < ROT_DIM:
            sel[r - half, l] = 1.0
            sel[half + r - half, 256 + l] = 1.0
        else:
            ones[0, l] = 1.0
    return lax.dot_general(cs, jnp.asarray(sel), TN_DIMS, precision=lax.Precision.HIGHEST) + jnp.asarray(ones)


def kernel(x, norm_g, w_in, sinks, conv_w, w_out, final_g, loss_target, m_norm_g, m_w_in, m_sinks, m_conv_w, m_w_out, m_final_g, v_norm_g, v_w_in, v_sinks, v_conv_w, v_w_out, v_final_g):
    S = x.shape[1]
    me = 4 * lax.axis_index("x") + 2 * lax.axis_index("y") + lax.axis_index("c")
    x2 = x.reshape(S, D_MODEL)
    t2 = loss_target.reshape(S, D_MODEL)
    ng = norm_g.reshape(1, D_MODEL)
    fg = final_g.reshape(1, D_MODEL)

    cw_pad = jnp.zeros((8, 128), F32).at[0:CONV_K, 0:64].set(conv_w)
    wt, wo, cw_all = _all_gather([w_in.T.astype(BF16), w_out.astype(BF16), cw_pad], "all_gather_weights")
    cw = cw_all.reshape(N_DEV, 8, 128)[:, 0:CONV_K, 0:64].transpose(1, 0, 2).reshape(CONV_K, CONV_W)

    tab = _rope_table(S)
    xn, pa, pc = _fwd_proj(x2, ng, wt)
    ya = _attn_fwd(pa, tab, sinks)
    yc = _conv_fwd(pc, cw)
    dh, dmix, g_wo, g_fg, loss_part = _out_loss(x2, t2, ya, yc, wo, fg)
    da, g_sinks = _attn_bwd(pa, dmix, tab, sinks)
    dc, g_cw = _conv_bwd(pc, dmix, cw)
    grad_x, g_ng = _grad_x(da, dc, wt, x2, dh, ng)
    g_wt = _grad_w_in(da, dc, xn)

    r_wt, r_wo = _reduce_scatter([g_wt.reshape(N_DEV, SHARD_IN, D_MODEL), g_wo.reshape(N_DEV, SHARD_OUT, D_MODEL)])
    grad_w_in = _sum_parts(r_wt, "sum_grad_w_in").T
    grad_w_out = _sum_parts(r_wo, "sum_grad_w_out")
    small = jnp.concatenate([
        g_ng.reshape(8, 128), g_fg.reshape(8, 128), g_sinks,
        g_cw.reshape(12, 128), jnp.zeros((4, 128), F32)], axis=0)
    (small_all,) = _all_gather([small], "all_gather_small_grads")
    small_sum = _sum_parts(small_all.reshape(N_DEV, SMALL_ROWS, 128), "sum_small_grads")
    grad_norm_g = small_sum[0:8].reshape(D_MODEL)
    grad_final_g = small_sum[8:16].reshape(D_MODEL)
    grad_sinks = small_sum[16:24, 0]
    grad_conv_w = lax.dynamic_slice(small_sum[24:36].reshape(CONV_K, CONV_W), (0, me * 64), (CONV_K, 64))

    loss = lax.psum(loss_part[0, 0], ("x", "y", "c"))

    d_w_in, nm_w_in, nv_w_in = _adamw(w_in, grad_w_in, m_w_in, v_w_in, "adamw_w_in")
    d_w_out, nm_w_out, nv_w_out = _adamw(w_out, grad_w_out, m_w_out, v_w_out, "adamw_w_out")
    def pack(a, b, c_, d):
        return jnp.concatenate([
            a.reshape(8, 128), b.reshape(8, 128),
            jnp.zeros((8, 128), F32).at[0, 0:8].set(c_).at[1:1 + CONV_K, 0:64].set(d)], axis=1)

    d_s, nm_s, nv_s = _adamw(
        pack(norm_g, final_g, sinks, conv_w), pack(grad_norm_g, grad_final_g, grad_sinks, grad_conv_w),
        pack(m_norm_g, m_final_g, m_sinks, m_conv_w),
        pack(v_norm_g, v_final_g, v_sinks, v_conv_w),
        "adamw_small")

    def unpack(p):
        return (p[:, 0:128].reshape(D_MODEL), p[:, 128:256].reshape(D_MODEL), p[0, 256:264], p[1:1 + CONV_K, 256:320])

    d_ng, d_fg, d_sk, d_cw = unpack(d_s)
    nm_ng, nm_fg, nm_sk, nm_cw = unpack(nm_s)
    nv_ng, nv_fg, nv_sk, nv_cw = unpack(nv_s)

    return (loss, grad_x.reshape(1, S, D_MODEL), grad_norm_g, grad_w_in, grad_sinks, grad_conv_w, grad_w_out, grad_final_g,
            d_ng, d_w_in, d_sk, d_cw, d_w_out, d_fg,
            nm_ng, nm_w_in, nm_sk, nm_cw, nm_w_out, nm_fg,
            nv_ng, nv_w_in, nv_sk, nv_cw, nv_w_out, nv_fg)
```

```python
import numpy as np
import jax
import jax.numpy as jnp
from jax import lax
from jax.experimental import pallas as pl
from jax.experimental.pallas import tpu as pltpu

F32 = jnp.float32
BF16 = jnp.bfloat16
MESH = pl.DeviceIdType.MESH

D_MODEL = 1024
HEAD_DIM = 64
N_Q_HEADS = 8
GROUP = 4
ATTN_W = 512
KV_W = 128
BLK = 128
CONV_W = 512
CONV_K = 3
IN_W = 3328
PA_W = 1280
PC_W = 2048
EPS = 1e-5
ROPE_THETA = 500000.0
ROT_DIM = 16
N_DEV = 8
N_CHIP = 4
SHARD_IN = IN_W // N_DEV
SHARD_OUT = D_MODEL // N_DEV
SMALL_ROWS = 40

ADAM_LR = 0.001
ADAM_B1 = 0.9
ADAM_B2 = 0.999
ADAM_EPS = 1e-08
ADAM_WD = 0.01
ADAM_STEP = 10

ACT = jnp.bfloat16

TM = 512
TQ = 512
TC = 512
HALO = 16
VMEM_LIMIT = 56 * 1024 * 1024

NT_DIMS = (((1,), (1,)), ((), ()))
TN_DIMS = (((0,), (0,)), ((), ()))


def _params(sem=None):
    kw = dict(vmem_limit_bytes=VMEM_LIMIT)
    if sem is not None:
        kw["dimension_semantics"] = sem
    return pltpu.CompilerParams(**kw)


def _nt(a, b):
    return lax.dot_general(a, b, NT_DIMS, preferred_element_type=F32)


def _tn(a, b):
    return lax.dot_general(a, b, TN_DIMS, preferred_element_type=F32)


def _nn(a, b):
    return jnp.dot(a, b, preferred_element_type=F32)


def _silu(g):
    return g * jax.nn.sigmoid(g)


def _dsilu(g):
    s = jax.nn.sigmoid(g)
    return s * (1.0 + g * (1.0 - s))


def _all_gather(arrs, name):
    n_arr = len(arrs)

    def body(*refs):
        x_refs = refs[:n_arr]
        out_refs = refs[n_arr:2 * n_arr]
        send_sems, recv_sems, local_sems = refs[2 * n_arr:]
        x, y, c = lax.axis_index("x"), lax.axis_index("y"), lax.axis_index("c")
        me, sibling = (x, y, c), (x, y, 1 - c)
        chips = [(1 - x, y), (x, 1 - y), (1 - x, 1 - y)]

        def rows(a, px, py, pc):
            m = x_refs[a].shape[0]
            return out_refs[a].at[pl.ds((4 * px + 2 * py + pc) * m, m), :]

        def copy(a, k, block, to, src=None):
            return pltpu.make_async_remote_copy(
                src_ref=rows(a, *block) if src is None else src,
                dst_ref=rows(a, *block),
                send_sem=send_sems.at[a * 7 + k],
                recv_sem=recv_sems.at[a * 7 + k],
                device_id=to,
                device_id_type=MESH,
            )

        mine = [pltpu.make_async_copy(x_refs[a], rows(a, *me), local_sems.at[a]) for a in range(n_arr)]
        for cp in mine:
            cp.start()
        first = []
        for a in range(n_arr):
            first.append(copy(a, 0, me, sibling, src=x_refs[a]))
            first += [copy(a, 1 + j, me, (*chip, c), src=x_refs[a]) for j, chip in enumerate(chips)]
        for cp in first:
            cp.start()
        passed = []
        for j, chip in enumerate(chips):
            for a in range(n_arr):
                copy(a, 1 + j, (*chip, c), me).wait_recv()
                fwd = copy(a, 4 + j, (*chip, c), sibling)
                fwd.start()
                passed.append(fwd)
        for a in range(n_arr):
            copy(a, 0, sibling, me).wait_recv()
            for j, chip in enumerate(chips):
                copy(a, 4 + j, (*chip, 1 - c), me).wait_recv()
        for cp in first + passed:
            cp.wait_send()
        for cp in mine:
            cp.wait()

    vmem = pl.BlockSpec(memory_space=pltpu.VMEM)
    return pl.pallas_call(
        body,
        name=name,
        out_shape=[jax.ShapeDtypeStruct((N_DEV * a.shape[0], a.shape[1]), a.dtype) for a in arrs],
        in_specs=[vmem] * n_arr,
        out_specs=[vmem] * n_arr,
        scratch_shapes=[
            pltpu.SemaphoreType.DMA((7 * n_arr,)),
            pltpu.SemaphoreType.DMA((7 * n_arr,)),
            pltpu.SemaphoreType.DMA((n_arr,)),
        ],
        compiler_params=_params(),
    )(*arrs)


class _ReduceScatter:
    def __init__(self, grads):
        self.shapes = [g.shape[1:] for g in grads]
        self.n = len(grads)
        self.items = tuple((a, r) for r in (1, 2, 3, 0) for a in range(self.n))
        self.steps = len(self.items) + 2

    def out_shape(self):
        own = [jax.ShapeDtypeStruct(s, F32) for s in self.shapes]
        ici = [jax.ShapeDtypeStruct((N_CHIP - 1,) + s, BF16) for s in self.shapes]
        land = [jax.ShapeDtypeStruct((N_CHIP,) + s, F32) for s in self.shapes]
        return own + ici + land

    def scratch_shapes(self):
        n_items = len(self.items)
        return ([pltpu.VMEM((2,) + s, F32) for s in self.shapes]
                + [pltpu.VMEM((N_CHIP - 1,) + s, BF16) for s in self.shapes]
                + [pltpu.VMEM(s, F32) for s in self.shapes]
                + [pltpu.SemaphoreType.DMA((self.n * N_CHIP,))] * 2
                + [pltpu.SemaphoreType.DMA((2 * n_items,))]
                + [pltpu.SemaphoreType.DMA((self.n * (N_CHIP - 1),))] * 2
                + [pltpu.SemaphoreType.DMA((self.n,))])

    def emit(self, step, n_steps, g_refs, out_refs, scratch):
        assert n_steps > self.steps
        n = self.n
        own_refs, ici_refs, land_refs = out_refs[:n], out_refs[n:2 * n], out_refs[2 * n:]
        stage, pair_bf, pair_own = scratch[:n], scratch[n:2 * n], scratch[2 * n:3 * n]
        sib_send, sib_recv, load_sems, ici_send, ici_recv, own_sems = scratch[3 * n:]
        x, y, c = lax.axis_index("x"), lax.axis_index("y"), lax.axis_index("c")

        def chip_of(r):
            return (x ^ (r >> 1), y ^ (r & 1))

        def block_of(r, core):
            cx, cy = chip_of(r)
            return 4 * cx + 2 * cy + core

        def to_sibling(a, r):
            return pltpu.make_async_remote_copy(
                src_ref=g_refs[a].at[block_of(r, 1 - c)], dst_ref=land_refs[a].at[r],
                send_sem=sib_send.at[a * N_CHIP + r], recv_sem=sib_recv.at[a * N_CHIP + r],
                device_id=(x, y, 1 - c), device_id_type=MESH)

        def loads(k):
            a, r = self.items[k]
            return (pltpu.make_async_copy(g_refs[a].at[block_of(r, c)], stage[a].at[0], load_sems.at[2 * k]),
                    pltpu.make_async_copy(land_refs[a].at[r], stage[a].at[1], load_sems.at[2 * k + 1]))

        def to_owner(k):
            a, r = self.items[k]
            if r == 0:
                return pltpu.make_async_copy(pair_own[a], own_refs[a], own_sems.at[a])
            return pltpu.make_async_remote_copy(
                src_ref=pair_bf[a].at[r - 1], dst_ref=ici_refs[a].at[r - 1],
                send_sem=ici_send.at[a * (N_CHIP - 1) + r - 1], recv_sem=ici_recv.at[a * (N_CHIP - 1) + r - 1],
                device_id=(*chip_of(r), c), device_id_type=MESH)

        @pl.when(step == 0)
        def _():
            for a, r in self.items:
                to_sibling(a, r).start()

        for k, (a, r) in enumerate(self.items):
            @pl.when(step == 1 + k)
            def _(k=k, a=a, r=r):
                to_sibling(a, r).wait_recv()
                for cp in loads(k):
                    cp.start()

            @pl.when(step == 2 + k)
            def _(k=k, a=a, r=r):
                for cp in loads(k):
                    cp.wait()
                total = stage[a][0] + stage[a][1]
                if r == 0:
                    pair_own[a][...] = total
                else:
                    pair_bf[a][r - 1] = total.astype(BF16)
                to_owner(k).start()

        def finish():
            @pl.when(step == n_steps - 1)
            def _():
                for k, (a, r) in enumerate(self.items):
                    if r == 0:
                        to_owner(k).wait()
                    else:
                        to_owner(k).wait_send()
                        to_owner(k).wait_recv()
                for a, r in self.items:
                    to_sibling(a, r).wait_send()

        return finish


def _fwd_proj(x, norm_g, wt):
    S = x.shape[0]

    def body(x_ref, g_ref, wt_ref, xn_ref, pa_ref, pc_ref):
        xv = x_ref[...]
        r = lax.rsqrt(jnp.mean(xv * xv, axis=-1, keepdims=True) + EPS)
        xn = (xv * r * g_ref[...]).astype(BF16)
        xn_ref[...] = xn
        pa_ref[:, 0:512] = _nt(xn, wt_ref[0:512, :]).astype(ACT)
        pa_ref[:, 512:1024] = _nt(xn, wt_ref[768:1280, :]).astype(ACT)
        pa_ref[:, 1024:1280] = _nt(xn, wt_ref[512:768, :]).astype(ACT)
        pc_ref[...] = _nt(xn, wt_ref[1280:3328, :]).astype(ACT)

    return pl.pallas_call(
        body,
        name="fwd_proj",
        grid=(S // TM,),
        in_specs=[
            pl.BlockSpec((TM, D_MODEL), lambda i: (i, 0)),
            pl.BlockSpec((1, D_MODEL), lambda i: (0, 0)),
            pl.BlockSpec((IN_W, D_MODEL), lambda i: (0, 0)),
        ],
        out_specs=[
            pl.BlockSpec((TM, D_MODEL), lambda i: (i, 0)),
            pl.BlockSpec((TM, PA_W), lambda i: (i, 0)),
            pl.BlockSpec((TM, PC_W), lambda i: (i, 0)),
        ],
        out_shape=[
            jax.ShapeDtypeStruct((S, D_MODEL), BF16),
            jax.ShapeDtypeStruct((S, PA_W), ACT),
            jax.ShapeDtypeStruct((S, PC_W), ACT),
        ],
        compiler_params=_params(("arbitrary",)),
    )(x, norm_g, wt)


def _rope(t, tab):
    return (t * tab[:, 0:128] + pltpu.roll(t, 120, 1) * tab[:, 128:256]
            + pltpu.roll(t, 8, 1) * tab[:, 256:384])


def _rope_t(d, tab):
    return (d * tab[:, 0:128] + pltpu.roll(d * tab[:, 128:256], 8, 1)
            + pltpu.roll(d * tab[:, 256:384], 120, 1))


def _fill_kv(kall, kvc_ref, kvp_ref, tabc_ref, tabp_ref):
    for lo, kv_ref, tab_ref, n in ((0, kvp_ref, tabp_ref, BLK), (BLK, kvc_ref, tabc_ref, TQ)):
        k = _rope(kv_ref[:, 0:128].astype(F32), tab_ref[...])
        v = kv_ref[:, 128:256].astype(F32)
        kall[0, lo:lo + n, :] = k.astype(BF16)
        kall[1, lo:lo + n, :] = pltpu.roll(k, 64, 1).astype(BF16)
        kall[2, lo:lo + n, :] = v.astype(BF16)
        kall[3, lo:lo + n, :] = pltpu.roll(v, 64, 1).astype(BF16)


HEADS = (((0, 0), (1, 0), (2, 1), (3, 1)), ((0, 1), (1, 1), (2, 0), (3, 0)))


def _upper():
    qi = lax.broadcasted_iota(jnp.int32, (BLK, BLK), 0)
    kj = lax.broadcasted_iota(jnp.int32, (BLK, BLK), 1)
    return kj > qi


def _merge(upper, both, rows):
    return jnp.where(upper, both[rows, 0:BLK], both[rows, BLK:2 * BLK])


def _split_store(ref, s, rows, upper, val):
    ref[s, rows, 0:BLK] = jnp.where(upper, val, 0.0).astype(BF16)
    ref[s, rows, BLK:2 * BLK] = jnp.where(upper, 0.0, val).astype(BF16)


def _stack_heads(ref, s, half, pairs):
    for a, (p, e) in enumerate(HEADS[s]):
        ref[s, a * BLK:(a + 1) * BLK, :] = jnp.where(half[e], pairs[p], 0.0).astype(BF16)


def _unstack_pair(half, outs, p):
    lo = 0 if p < 2 else 1
    rows = slice(p * BLK, (p + 1) * BLK)
    return jnp.where(half[0], outs[lo][rows, :], outs[1 - lo][rows, :])


def _softmax(sm, sink):
    m = jnp.maximum(jnp.max(sm, axis=-1, keepdims=True), sink)
    p = jnp.exp(sm - m)
    es = jnp.exp(sink - m)
    inv = 1.0 / (jnp.sum(p, axis=-1, keepdims=True) + es)
    return p * inv, es * inv


def _attn_specs(tile):
    nb = TQ // BLK
    prev = lambda i: jnp.maximum(tile(i) * nb - 1, 0)
    return [
        pl.BlockSpec(memory_space=pltpu.SMEM),
        pl.BlockSpec((TQ, ATTN_W), lambda i: (tile(i), 0)),
        pl.BlockSpec((TQ, ATTN_W), lambda i: (tile(i), 1)),
        pl.BlockSpec((TQ, 2 * KV_W), lambda i: (tile(i), 4)),
        pl.BlockSpec((BLK, 2 * KV_W), lambda i: (prev(i), 4)),
        pl.BlockSpec((TQ, 384), lambda i: (tile(i), 0)),
        pl.BlockSpec((BLK, 384), lambda i: (prev(i), 0)),
    ]


def _attn_fwd(pa, tab, sinks):
    S = pa.shape[0]
    nb = TQ // BLK

    def body(sink_ref, q_ref, g_ref, kvc_ref, kvp_ref, tabc_ref, tabp_ref, o_ref, kall, q_sc, p_sc):
        i = pl.program_id(0)
        _fill_kv(kall, kvc_ref, kvp_ref, tabc_ref, tabp_ref)
        lane = lax.broadcasted_iota(jnp.int32, (BLK, 128), 1)
        half = [lane < HEAD_DIM, lane >= HEAD_DIM]
        upper = _upper()
        for j in range(nb):
            rq = slice(j * BLK, (j + 1) * BLK)
            rk = slice(j * BLK, (j + 2) * BLK)
            tab = tabc_ref[rq, :]
            qr = [_rope(q_ref[rq, p * 128:(p + 1) * 128].astype(F32), tab) * 0.125 for p in range(4)]
            outs = []
            for s in range(2):
                _stack_heads(q_sc, s, half, qr)
                sf = _nt(q_sc[s], kall[s, rk, :])
                if j == 0:
                    sf = sf + jnp.where((i == 0) & (lax.broadcasted_iota(jnp.int32, (1, 2 * BLK), 1) < BLK),
                                        -jnp.inf, 0.0)
                for a, (p, e) in enumerate(HEADS[s]):
                    ra = slice(a * BLK, (a + 1) * BLK)
                    prob, _ = _softmax(_merge(upper, sf, ra), sink_ref[2 * p + e])
                    _split_store(p_sc, s, ra, upper, prob)
                outs.append(_nn(p_sc[s], kall[2 + s, rk, :]))
            for p in range(4):
                cols = slice(p * 128, (p + 1) * 128)
                o_ref[rq, cols] = (_unstack_pair(half, outs, p) * _silu(g_ref[rq, cols].astype(F32))).astype(BF16)

    return pl.pallas_call(
        body,
        name="attn_fwd",
        grid=(S // TQ,),
        in_specs=_attn_specs(lambda i: i),
        out_specs=pl.BlockSpec((TQ, ATTN_W), lambda i: (i, 0)),
        out_shape=jax.ShapeDtypeStruct((S, ATTN_W), BF16),
        scratch_shapes=[
            pltpu.VMEM((4, BLK + TQ, 128), BF16),
            pltpu.VMEM((2, 4 * BLK, 128), BF16),
            pltpu.VMEM((2, 4 * BLK, 2 * BLK), BF16),
        ],
        compiler_params=_params(("arbitrary",)),
    )(sinks, pa, pa, pa, pa, tab, tab)


def _shift_down(u, halo_ref, has_prev):
    def halo_u(r):
        hu = halo_ref[r:r + 1, 512:1024].astype(F32) * halo_ref[r:r + 1, 1024:1536].astype(F32)
        return jnp.where(has_prev, hu, 0.0)

    row = lax.broadcasted_iota(jnp.int32, u.shape, 0)
    um1 = jnp.where(row == 0, halo_u(HALO - 1), pltpu.roll(u, 1, 0))
    um2 = jnp.where(row == 0, halo_u(HALO - 2), jnp.where(row == 1, halo_u(HALO - 1), pltpu.roll(u, 2, 0)))
    return um1, um2


def _conv_fwd(pc, conv_w):
    S = pc.shape[0]

    def body(pc_ref, halo_ref, w_ref, o_ref):
        b = pc_ref[:, 0:512].astype(F32)
        u = pc_ref[:, 512:1024].astype(F32) * pc_ref[:, 1024:1536].astype(F32)
        gc = pc_ref[:, 1536:2048].astype(F32)
        um1, um2 = _shift_down(u, halo_ref, pl.program_id(0) > 0)
        cv = w_ref[0:1, :] * um2 + w_ref[1:2, :] * um1 + w_ref[2:3, :] * u
        o_ref[...] = (b * cv * _silu(gc)).astype(BF16)

    return pl.pallas_call(
        body,
        name="conv_fwd",
        grid=(S // TC,),
        in_specs=[
            pl.BlockSpec((TC, PC_W), lambda i: (i, 0)),
            pl.BlockSpec((HALO, PC_W), lambda i: (jnp.maximum(i * (TC // HALO) - 1, 0), 0)),
            pl.BlockSpec((CONV_K, CONV_W), lambda i: (0, 0)),
        ],
        out_specs=pl.BlockSpec((TC, CONV_W), lambda i: (i, 0)),
        out_shape=jax.ShapeDtypeStruct((S, CONV_W), BF16),
        compiler_params=_params(("arbitrary",)),
    )(pc, pc, conv_w)


def _out_loss(x, target, ya, yc, w_out, final_g):
    S = x.shape[0]

    def body(x_ref, t_ref, ya_ref, yc_ref, wo_ref, fg_ref, dh_ref, dmix_ref, gwo_ref, gfg_ref, loss_ref):
        @pl.when(pl.program_id(0) == 0)
        def _():
            gwo_ref[...] = jnp.zeros_like(gwo_ref)
            gfg_ref[...] = jnp.zeros_like(gfg_ref)
            loss_ref[...] = jnp.zeros_like(loss_ref)

        mix = jnp.concatenate([ya_ref[...], yc_ref[...]], axis=1)
        wo = wo_ref[...]
        fg = fg_ref[...]
        h = x_ref[...] + _nn(mix, wo)
        r = lax.rsqrt(jnp.mean(h * h, axis=-1, keepdims=True) + EPS)
        n = h * r
        err = n * fg - t_ref[...]
        loss_ref[...] += 0.5 * jnp.sum(jnp.mean(err * err, axis=-1, keepdims=True), axis=0, keepdims=True)
        dy = err * (1.0 / D_MODEL)
        gfg_ref[...] += jnp.sum(dy * n, axis=0, keepdims=True)
        dyg = dy * fg
        dh = r * (dyg - n * jnp.mean(dyg * n, axis=-1, keepdims=True))
        dh_ref[...] = dh
        dhb = dh.astype(BF16)
        dmix_ref[...] = _nt(dhb, wo).astype(ACT)
        gwo_ref[...] += _tn(mix, dhb)

    row = lambda i: (i, 0)
    fixed = lambda i: (0, 0)
    return pl.pallas_call(
        body,
        name="out_loss",
        grid=(S // TM,),
        in_specs=[
            pl.BlockSpec((TM, D_MODEL), row),
            pl.BlockSpec((TM, D_MODEL), row),
            pl.BlockSpec((TM, ATTN_W), row),
            pl.BlockSpec((TM, CONV_W), row),
            pl.BlockSpec((D_MODEL, D_MODEL), fixed),
            pl.BlockSpec((1, D_MODEL), fixed),
        ],
        out_specs=[
            pl.BlockSpec((TM, D_MODEL), row),
            pl.BlockSpec((TM, D_MODEL), row),
            pl.BlockSpec((D_MODEL, D_MODEL), fixed),
            pl.BlockSpec((1, D_MODEL), fixed),
            pl.BlockSpec((1, 1), fixed),
        ],
        out_shape=[
            jax.ShapeDtypeStruct((S, D_MODEL), F32),
            jax.ShapeDtypeStruct((S, D_MODEL), ACT),
            jax.ShapeDtypeStruct((D_MODEL, D_MODEL), F32),
            jax.ShapeDtypeStruct((1, D_MODEL), F32),
            jax.ShapeDtypeStruct((1, 1), F32),
        ],
        compiler_params=_params(("arbitrary",)),
    )(x, target, ya, yc, w_out, final_g)


def _attn_bwd(pa, dmix, tab, sinks):
    S = pa.shape[0]
    nt = S // TQ
    nb = TQ // BLK

    def body(sink_ref, q_ref, g_ref, kvc_ref, kvp_ref, tabc_ref, tabp_ref, dm_ref,
             d_ref, dsink_ref, kall, dkv, carry, q_sc, do_sc, p_sc, ds_sc):
        step = pl.program_id(0)
        i = nt - 1 - step

        @pl.when(step == 0)
        def _():
            carry[...] = jnp.zeros_like(carry)
            dsink_ref[...] = jnp.zeros_like(dsink_ref)

        _fill_kv(kall, kvc_ref, kvp_ref, tabc_ref, tabp_ref)
        dkv[0:TQ, :] = jnp.zeros((TQ, 2 * KV_W), F32)
        dkv[TQ:TQ + BLK, :] = carry[...]
        lane = lax.broadcasted_iota(jnp.int32, (BLK, 128), 1)
        half = [lane < HEAD_DIM, lane >= HEAD_DIM]
        upper = _upper()
        for j in range(nb):
            rq = slice(j * BLK, (j + 1) * BLK)
            rk = slice(j * BLK, (j + 2) * BLK)
            tab = tabc_ref[rq, :]
            pair = [slice(p * 128, (p + 1) * 128) for p in range(4)]
            qr = [_rope(q_ref[rq, c].astype(F32), tab) * 0.125 for c in pair]
            g = [g_ref[rq, c].astype(F32) for c in pair]
            da = [dm_ref[rq, c].astype(F32) for c in pair]
            do = [da[p] * _silu(g[p]) for p in range(4)]
            outs, dqs, dks, dvs = [], [], [], []
            for s in range(2):
                kk = kall[s, rk, :]
                vv = kall[2 + s, rk, :]
                _stack_heads(q_sc, s, half, qr)
                _stack_heads(do_sc, s, half, do)
                sf = _nt(q_sc[s], kk)
                if j == 0:
                    sf = sf + jnp.where((i == 0) & (lax.broadcasted_iota(jnp.int32, (1, 2 * BLK), 1) < BLK),
                                        -jnp.inf, 0.0)
                dpf = _nt(do_sc[s], vv)
                for a, (p, e) in enumerate(HEADS[s]):
                    h = 2 * p + e
                    ra = slice(a * BLK, (a + 1) * BLK)
                    prob, psink = _softmax(_merge(upper, sf, ra), sink_ref[h])
                    _split_store(p_sc, s, ra, upper, prob)
                    dprob = _merge(upper, dpf, ra)
                    dsum = jnp.sum(dprob * prob, axis=-1, keepdims=True)
                    _split_store(ds_sc, s, ra, upper, prob * (dprob - dsum))
                    dsink_ref[h:h + 1, :] += jnp.broadcast_to(
                        -jnp.sum(psink * dsum, axis=0, keepdims=True), (1, 128))
                outs.append(_nn(p_sc[s], vv))
                dqs.append(_nn(ds_sc[s], kk))
                dks.append(_tn(ds_sc[s], q_sc[s]))
                dvs.append(_tn(p_sc[s], do_sc[s]))
            for p in range(4):
                d_ref[rq, pair[p]] = _rope_t(_unstack_pair(half, dqs, p) * 0.125, tab).astype(BF16)
                d_ref[rq, 512 + p * 128:512 + (p + 1) * 128] = (
                    da[p] * _unstack_pair(half, outs, p) * _dsilu(g[p])).astype(BF16)
            dkv[rk, 0:128] += dks[0] + pltpu.roll(dks[1], 64, 1)
            dkv[rk, 128:256] += dvs[0] + pltpu.roll(dvs[1], 64, 1)
        d_ref[:, 1024:1152] = _rope_t(dkv[BLK:BLK + TQ, 0:128], tabc_ref[...]).astype(BF16)
        d_ref[:, 1152:1280] = dkv[BLK:BLK + TQ, 128:256].astype(BF16)
        carry[...] = dkv[0:BLK, :]

    rev = lambda s: nt - 1 - s
    return pl.pallas_call(
        body,
        name="attn_bwd",
        grid=(nt,),
        in_specs=_attn_specs(rev) + [pl.BlockSpec((TQ, ATTN_W), lambda s: (nt - 1 - s, 0))],
        out_specs=[
            pl.BlockSpec((TQ, PA_W), lambda s: (nt - 1 - s, 0)),
            pl.BlockSpec((8, 128), lambda s: (0, 0)),
        ],
        out_shape=[
            jax.ShapeDtypeStruct((S, PA_W), BF16),
            jax.ShapeDtypeStruct((8, 128), F32),
        ],
        scratch_shapes=[
            pltpu.VMEM((4, BLK + TQ, 128), BF16),
            pltpu.VMEM((BLK + TQ, 2 * KV_W), F32),
            pltpu.VMEM((BLK, 2 * KV_W), F32),
            pltpu.VMEM((2, 4 * BLK, 128), BF16),
            pltpu.VMEM((2, 4 * BLK, 128), BF16),
            pltpu.VMEM((2, 4 * BLK, 2 * BLK), BF16),
            pltpu.VMEM((2, 4 * BLK, 2 * BLK), BF16),
        ],
        compiler_params=_params(("arbitrary",)),
    )(sinks, pa, pa, pa, pa, tab, tab, dmix)


def _conv_bwd(pc, dmix, conv_w):
    S = pc.shape[0]
    nt = S // TC

    def body(pc_ref, prev_ref, next_ref, dm_ref, dmn_ref, w_ref, d_ref, gw_ref):
        i = pl.program_id(0)

        @pl.when(i == 0)
        def _():
            gw_ref[...] = jnp.zeros_like(gw_ref)

        def parts(ref):
            return (ref[:, 0:512].astype(F32), ref[:, 512:1024].astype(F32),
                    ref[:, 1024:1536].astype(F32), ref[:, 1536:2048].astype(F32))

        w0, w1, w2 = w_ref[0:1, :], w_ref[1:2, :], w_ref[2:3, :]
        b, c, hh, gc = parts(pc_ref)
        u = c * hh
        um1, um2 = _shift_down(u, prev_ref, i > 0)
        cv = w0 * um2 + w1 * um1 + w2 * u
        sg = _silu(gc)
        dy = dm_ref[...].astype(F32)
        dcv = dy * b * sg

        def next_dcv(r):
            nd = (dmn_ref[r:r + 1, :].astype(F32) * next_ref[r:r + 1, 0:512].astype(F32)
                  * _silu(next_ref[r:r + 1, 1536:2048].astype(F32)))
            return jnp.where(i < nt - 1, nd, 0.0)

        row = lax.broadcasted_iota(jnp.int32, (TC, CONV_W), 0)
        dp1 = jnp.where(row == TC - 1, next_dcv(0), pltpu.roll(dcv, TC - 1, 0))
        dp2 = jnp.where(row == TC - 1, next_dcv(1),
                        jnp.where(row == TC - 2, next_dcv(0), pltpu.roll(dcv, TC - 2, 0)))
        du = w2 * dcv + w1 * dp1 + w0 * dp2
        d_ref[:, 0:512] = (dy * cv * sg).astype(BF16)
        d_ref[:, 512:1024] = (du * hh).astype(BF16)
        d_ref[:, 1024:1536] = (du * c).astype(BF16)
        d_ref[:, 1536:2048] = (dy * b * cv * _dsilu(gc)).astype(BF16)
        gw_ref[0:1, :] += jnp.sum(dcv * um2, axis=0, keepdims=True)
        gw_ref[1:2, :] += jnp.sum(dcv * um1, axis=0, keepdims=True)
        gw_ref[2:3, :] += jnp.sum(dcv * u, axis=0, keepdims=True)

    t8 = TC // HALO
    return pl.pallas_call(
        body,
        name="conv_bwd",
        grid=(nt,),
        in_specs=[
            pl.BlockSpec((TC, PC_W), lambda i: (i, 0)),
            pl.BlockSpec((HALO, PC_W), lambda i: (jnp.maximum(i * t8 - 1, 0), 0)),
            pl.BlockSpec((HALO, PC_W), lambda i: (jnp.minimum((i + 1) * t8, nt * t8 - 1), 0)),
            pl.BlockSpec((TC, CONV_W), lambda i: (i, 1)),
            pl.BlockSpec((HALO, CONV_W), lambda i: (jnp.minimum((i + 1) * t8, nt * t8 - 1), 1)),
            pl.BlockSpec((CONV_K, CONV_W), lambda i: (0, 0)),
        ],
        out_specs=[
            pl.BlockSpec((TC, PC_W), lambda i: (i, 0)),
            pl.BlockSpec((CONV_K, CONV_W), lambda i: (0, 0)),
        ],
        out_shape=[
            jax.ShapeDtypeStruct((S, PC_W), BF16),
            jax.ShapeDtypeStruct((CONV_K, CONV_W), F32),
        ],
        compiler_params=_params(("arbitrary",)),
    )(pc, pc, pc, dmix, dmix, conv_w)


def _grad_x(da, dc, wt, x, dh, norm_g, grads):
    S = x.shape[0]
    n_steps = S // TM
    rs = _ReduceScatter(grads)
    n_rs_out = len(rs.out_shape())

    def body(da_ref, dc_ref, wt_ref, x_ref, dh_ref, g_ref, *rest):
        grad_refs, rest = rest[:rs.n], rest[rs.n:]
        gx_ref, gng_ref = rest[:2]
        rs_out, rs_scratch = rest[2:2 + n_rs_out], rest[2 + n_rs_out:]
        step = pl.program_id(0)
        finish = rs.emit(step, n_steps, grad_refs, rs_out, rs_scratch)

        @pl.when(step == 0)
        def _():
            gng_ref[...] = jnp.zeros_like(gng_ref)

        dxn = (_nn(da_ref[:, 0:512], wt_ref[0:512, :]) + _nn(da_ref[:, 512:1024], wt_ref[768:1280, :])
               + _nn(da_ref[:, 1024:1280], wt_ref[512:768, :]) + _nn(dc_ref[...], wt_ref[1280:3328, :]))
        xv = x_ref[...]
        r = lax.rsqrt(jnp.mean(xv * xv, axis=-1, keepdims=True) + EPS)
        n = xv * r
        gng_ref[...] += jnp.sum(dxn * n, axis=0, keepdims=True)
        dxg = dxn * g_ref[...]
        gx_ref[...] = dh_ref[...] + r * (dxg - n * jnp.mean(dxg * n, axis=-1, keepdims=True))
        finish()

    row = lambda i: (i, 0)
    fixed = lambda i: (0, 0)
    any_spec = pl.BlockSpec(memory_space=pl.ANY)
    outs = pl.pallas_call(
        body,
        name="grad_x_reduce_scatter",
        grid=(n_steps,),
        in_specs=[
            pl.BlockSpec((TM, PA_W), row),
            pl.BlockSpec((TM, PC_W), row),
            pl.BlockSpec((IN_W, D_MODEL), fixed),
            pl.BlockSpec((TM, D_MODEL), row),
            pl.BlockSpec((TM, D_MODEL), row),
            pl.BlockSpec((1, D_MODEL), fixed),
        ] + [any_spec] * rs.n,
        out_specs=[pl.BlockSpec((TM, D_MODEL), row), pl.BlockSpec((1, D_MODEL), fixed)] + [any_spec] * n_rs_out,
        out_shape=[jax.ShapeDtypeStruct((S, D_MODEL), F32), jax.ShapeDtypeStruct((1, D_MODEL), F32)] + rs.out_shape(),
        scratch_shapes=rs.scratch_shapes(),
        compiler_params=_params(("arbitrary",)),
    )(da, dc, wt, x, dh, norm_g, *grads)
    return outs[0], outs[1], outs[2:2 + rs.n], outs[2 + rs.n:2 + 2 * rs.n]


def _grad_w_in(da, dc, xn):
    S = xn.shape[0]

    def body(da_ref, dc_ref, xn_ref, gw_ref):
        @pl.when(pl.program_id(0) == 0)
        def _():
            gw_ref[...] = jnp.zeros_like(gw_ref)

        xn = xn_ref[...]
        gw_ref[0:512, :] += _tn(da_ref[:, 0:512], xn)
        gw_ref[768:1280, :] += _tn(da_ref[:, 512:1024], xn)
        gw_ref[512:768, :] += _tn(da_ref[:, 1024:1280], xn)
        gw_ref[1280:3328, :] += _tn(dc_ref[...], xn)

    row = lambda i: (i, 0)
    return pl.pallas_call(
        body,
        name="grad_w_in",
        grid=(S // TM,),
        in_specs=[pl.BlockSpec((TM, PA_W), row), pl.BlockSpec((TM, PC_W), row), pl.BlockSpec((TM, D_MODEL), row)],
        out_specs=pl.BlockSpec((IN_W, D_MODEL), lambda i: (0, 0)),
        out_shape=jax.ShapeDtypeStruct((IN_W, D_MODEL), F32),
        compiler_params=_params(("arbitrary",)),
    )(da, dc, xn)


def _sum_chips(own, others, name):
    def body(own_ref, p_ref, o_ref):
        acc = own_ref[...]
        for k in range(N_CHIP - 1):
            acc = acc + p_ref[k].astype(F32)
        o_ref[...] = acc

    return pl.pallas_call(
        body,
        name=name,
        out_shape=jax.ShapeDtypeStruct(own.shape, F32),
        compiler_params=_params(),
    )(own, others)


def _sum_parts(parts, name):
    n = parts.shape[0]

    def body(p_ref, o_ref):
        acc = p_ref[0]
        for k in range(1, n):
            acc = acc + p_ref[k]
        o_ref[...] = acc

    return pl.pallas_call(
        body,
        name=name,
        out_shape=jax.ShapeDtypeStruct(parts.shape[1:], F32),
        compiler_params=_params(),
    )(parts)


def _adamw(w, g, m, v, name):
    c1 = 1.0 - ADAM_B1 ** ADAM_STEP
    c2 = 1.0 - ADAM_B2 ** ADAM_STEP

    def body(w_ref, g_ref, m_ref, v_ref, d_ref, nm_ref, nv_ref):
        g = g_ref[...]
        nm = ADAM_B1 * m_ref[...] + (1.0 - ADAM_B1) * g
        nv = ADAM_B2 * v_ref[...] + (1.0 - ADAM_B2) * (g * g)
        nm_ref[...] = nm
        nv_ref[...] = nv
        d_ref[...] = -ADAM_LR * ((nm / c1) / (jnp.sqrt(nv / c2) + ADAM_EPS) + ADAM_WD * w_ref[...])

    shape = jax.ShapeDtypeStruct(w.shape, F32)
    return pl.pallas_call(
        body,
        name=name,
        out_shape=[shape, shape, shape],
        compiler_params=_params(),
    )(w, g, m, v)


def _rope_table(S):
    half = ROT_DIM // 2
    pos = jnp.arange(S, dtype=jnp.int32).astype(F32)
    inv_freq = ROPE_THETA ** (-jnp.arange(0, ROT_DIM, 2, dtype=F32) / ROT_DIM)
    ang = inv_freq[:, None] * pos[None, :]
    cs = jnp.concatenate([jnp.cos(ang), jnp.sin(ang)], axis=0)
    sel = np.zeros((2 * half, 384), np.float32)
    ones = np.zeros((1, 384), np.float32)
    for l in range(128):
        r = l % HEAD_DIM
        if r < half:
            sel[r, l] = 1.0
            sel[half + r, 128 + l] = -1.0
        elif r < ROT_DIM:
            sel[r - half, l] = 1.0
            sel[half + r - half, 256 + l] = 1.0
        else:
            ones[0, l] = 1.0
    return lax.dot_general(cs, jnp.asarray(sel), TN_DIMS, precision=lax.Precision.HIGHEST) + jnp.asarray(ones)


def kernel(x, norm_g, w_in, sinks, conv_w, w_out, final_g, loss_target, m_norm_g, m_w_in, m_sinks, m_conv_w, m_w_out, m_final_g, v_norm_g, v_w_in, v_sinks, v_conv_w, v_w_out, v_final_g):
    S = x.shape[1]
    me = 4 * lax.axis_index("x") + 2 * lax.axis_index("y") + lax.axis_index("c")
    x2 = x.reshape(S, D_MODEL)
    t2 = loss_target.reshape(S, D_MODEL)
    ng = norm_g.reshape(1, D_MODEL)
    fg = final_g.reshape(1, D_MODEL)

    cw_pad = jnp.zeros((8, 128), F32).at[0:CONV_K, 0:64].set(conv_w)
    wt, wo, cw_all = _all_gather([w_in.T.astype(BF16), w_out.astype(BF16), cw_pad], "all_gather_weights")
    cw = cw_all.reshape(N_DEV, 8, 128)[:, 0:CONV_K, 0:64].transpose(1, 0, 2).reshape(CONV_K, CONV_W)

    tab = _rope_table(S)
    xn, pa, pc = _fwd_proj(x2, ng, wt)
    ya = _attn_fwd(pa, tab, sinks)
    yc = _conv_fwd(pc, cw)
    dh, dmix, g_wo, g_fg, loss_part = _out_loss(x2, t2, ya, yc, wo, fg)
    da, g_sinks = _attn_bwd(pa, dmix, tab, sinks)
    dc, g_cw = _conv_bwd(pc, dmix, cw)
    g_wt = _grad_w_in(da, dc, xn)
    grad_x, g_ng, own, others = _grad_x(
        da, dc, wt, x2, dh, ng,
        [g_wt.reshape(N_DEV, SHARD_IN, D_MODEL), g_wo.reshape(N_DEV, SHARD_OUT, D_MODEL)])
    grad_w_in = _sum_chips(own[0], others[0], "sum_grad_w_in").T
    grad_w_out = _sum_chips(own[1], others[1], "sum_grad_w_out")
    small = jnp.concatenate([
        g_ng.reshape(8, 128), g_fg.reshape(8, 128), g_sinks,
        g_cw.reshape(12, 128), jnp.zeros((4, 128), F32)], axis=0)
    (small_all,) = _all_gather([small], "all_gather_small_grads")
    small_sum = _sum_parts(small_all.reshape(N_DEV, SMALL_ROWS, 128), "sum_small_grads")
    grad_norm_g = small_sum[0:8].reshape(D_MODEL)
    grad_final_g = small_sum[8:16].reshape(D_MODEL)
    grad_sinks = small_sum[16:24, 0]
    grad_conv_w = lax.dynamic_slice(small_sum[24:36].reshape(CONV_K, CONV_W), (0, me * 64), (CONV_K, 64))

    loss = lax.psum(loss_part[0, 0], ("x", "y", "c"))

    d_w_in, nm_w_in, nv_w_in = _adamw(w_in, grad_w_in, m_w_in, v_w_in, "adamw_w_in")
    d_w_out, nm_w_out, nv_w_out = _adamw(w_out, grad_w_out, m_w_out, v_w_out, "adamw_w_out")
    def pack(a, b, c_, d):
        return jnp.concatenate([
            a.reshape(8, 128), b.reshape(8, 128),
            jnp.zeros((8, 128), F32).at[0, 0:8].set(c_).at[1:1 + CONV_K, 0:64].set(d)], axis=1)

    d_s, nm_s, nv_s = _adamw(
        pack(norm_g, final_g, sinks, conv_w), pack(grad_norm_g, grad_final_g, grad_sinks, grad_conv_w),
        pack(m_norm_g, m_final_g, m_sinks, m_conv_w),
        pack(v_norm_g, v_final_g, v_sinks, v_conv_w),
        "adamw_small")

    def unpack(p):
        return (p[:, 0:128].reshape(D_MODEL), p[:, 128:256].reshape(D_MODEL), p[0, 256:264], p[1:1 + CONV_K, 256:320])

    d_ng, d_fg, d_sk, d_cw = unpack(d_s)
    nm_ng, nm_fg, nm_sk, nm_cw = unpack(nm_s)
    nv_ng, nv_fg, nv_sk, nv_cw = unpack(nv_s)

    return (loss, grad_x.reshape(1, S, D_MODEL), grad_norm_g, grad_w_in, grad_sinks, grad_conv_w, grad_w_out, grad_final_g,
            d_ng, d_w_in, d_sk, d_cw, d_w_out, d_fg,
            nm_ng, nm_w_in, nm_sk, nm_cw, nm_w_out, nm_fg,
            nv_ng, nv_w_in, nv_sk, nv_cw, nv_w_out, nv_fg)
```

```python
import numpy as np
import jax
import jax.numpy as jnp
from jax import lax
from jax.experimental import pallas as pl
from jax.experimental.pallas import tpu as pltpu

F32 = jnp.float32
BF16 = jnp.bfloat16
MESH = pl.DeviceIdType.MESH

D_MODEL = 1024
HEAD_DIM = 64
N_Q_HEADS = 8
GROUP = 4
ATTN_W = 512
KV_W = 128
BLK = 128
CONV_W = 512
CONV_K = 3
IN_W = 3328
PA_W = 1280
PC_W = 2048
EPS = 1e-5
ROPE_THETA = 500000.0
ROT_DIM = 16
N_DEV = 8
N_CHIP = 4
SHARD_IN = IN_W // N_DEV
SHARD_OUT = D_MODEL // N_DEV
SMALL_ROWS = 40

ADAM_LR = 0.001
ADAM_B1 = 0.9
ADAM_B2 = 0.999
ADAM_EPS = 1e-08
ADAM_WD = 0.01
ADAM_STEP = 10

ACT = jnp.bfloat16

TM = 512
TQ = 512
TC = 512
HALO = 16
VMEM_LIMIT = 56 * 1024 * 1024

NT_DIMS = (((1,), (1,)), ((), ()))
TN_DIMS = (((0,), (0,)), ((), ()))


def _params(sem=None):
    kw = dict(vmem_limit_bytes=VMEM_LIMIT)
    if sem is not None:
        kw["dimension_semantics"] = sem
    return pltpu.CompilerParams(**kw)


def _nt(a, b):
    return lax.dot_general(a, b, NT_DIMS, preferred_element_type=F32)


def _tn(a, b):
    return lax.dot_general(a, b, TN_DIMS, preferred_element_type=F32)


def _nn(a, b):
    return jnp.dot(a, b, preferred_element_type=F32)


def _silu(g):
    return g * jax.nn.sigmoid(g)


def _dsilu(g):
    s = jax.nn.sigmoid(g)
    return s * (1.0 + g * (1.0 - s))


def _all_gather(arrs, name):
    n_arr = len(arrs)

    def body(*refs):
        x_refs = refs[:n_arr]
        out_refs = refs[n_arr:2 * n_arr]
        send_sems, recv_sems, local_sems = refs[2 * n_arr:]
        x, y, c = lax.axis_index("x"), lax.axis_index("y"), lax.axis_index("c")
        me, sibling = (x, y, c), (x, y, 1 - c)
        chips = [(1 - x, y), (x, 1 - y), (1 - x, 1 - y)]

        def rows(a, px, py, pc):
            m = x_refs[a].shape[0]
            return out_refs[a].at[pl.ds((4 * px + 2 * py + pc) * m, m), :]

        def copy(a, k, block, to, src=None):
            return pltpu.make_async_remote_copy(
                src_ref=rows(a, *block) if src is None else src,
                dst_ref=rows(a, *block),
                send_sem=send_sems.at[a * 7 + k],
                recv_sem=recv_sems.at[a * 7 + k],
                device_id=to,
                device_id_type=MESH,
            )

        mine = [pltpu.make_async_copy(x_refs[a], rows(a, *me), local_sems.at[a]) for a in range(n_arr)]
        for cp in mine:
            cp.start()
        first = []
        for a in range(n_arr):
            first.append(copy(a, 0, me, sibling, src=x_refs[a]))
            first += [copy(a, 1 + j, me, (*chip, c), src=x_refs[a]) for j, chip in enumerate(chips)]
        for cp in first:
            cp.start()
        passed = []
        for j, chip in enumerate(chips):
            for a in range(n_arr):
                copy(a, 1 + j, (*chip, c), me).wait_recv()
                fwd = copy(a, 4 + j, (*chip, c), sibling)
                fwd.start()
                passed.append(fwd)
        for a in range(n_arr):
            copy(a, 0, sibling, me).wait_recv()
            for j, chip in enumerate(chips):
                copy(a, 4 + j, (*chip, 1 - c), me).wait_recv()
        for cp in first + passed:
            cp.wait_send()
        for cp in mine:
            cp.wait()

    vmem = pl.BlockSpec(memory_space=pltpu.VMEM)
    return pl.pallas_call(
        body,
        name=name,
        out_shape=[jax.ShapeDtypeStruct((N_DEV * a.shape[0], a.shape[1]), a.dtype) for a in arrs],
        in_specs=[vmem] * n_arr,
        out_specs=[vmem] * n_arr,
        scratch_shapes=[
            pltpu.SemaphoreType.DMA((7 * n_arr,)),
            pltpu.SemaphoreType.DMA((7 * n_arr,)),
            pltpu.SemaphoreType.DMA((n_arr,)),
        ],
        compiler_params=_params(),
    )(*arrs)


class _AllGatherInSteps:
    forward_step = 3

    def __init__(self, arrs):
        self.blocks = [(a.shape, a.dtype) for a in arrs]
        self.n = len(arrs)

    def out_shape(self):
        return [jax.ShapeDtypeStruct((N_DEV * s[0], s[1]), d) for s, d in self.blocks]

    def scratch_shapes(self):
        return [pltpu.SemaphoreType.DMA((7 * self.n,)), pltpu.SemaphoreType.DMA((7 * self.n,)),
                pltpu.SemaphoreType.DMA((self.n,))]

    def emit(self, step, n_steps, x_refs, out_refs, scratch):
        assert n_steps > self.forward_step + 1
        send_sems, recv_sems, local_sems = scratch
        x, y, c = lax.axis_index("x"), lax.axis_index("y"), lax.axis_index("c")
        me, sibling = (x, y, c), (x, y, 1 - c)
        chips = [(1 - x, y), (x, 1 - y), (1 - x, 1 - y)]

        def rows(a, px, py, pc):
            m = self.blocks[a][0][0]
            return out_refs[a].at[pl.ds((4 * px + 2 * py + pc) * m, m), :]

        def copy(a, k, block, to, src=None):
            return pltpu.make_async_remote_copy(
                src_ref=rows(a, *block) if src is None else src, dst_ref=rows(a, *block),
                send_sem=send_sems.at[a * 7 + k], recv_sem=recv_sems.at[a * 7 + k],
                device_id=to, device_id_type=MESH)

        def mine(a):
            return pltpu.make_async_copy(x_refs[a], rows(a, *me), local_sems.at[a])

        def first(a):
            return ([copy(a, 0, me, sibling, src=x_refs[a])]
                    + [copy(a, 1 + j, me, (*chip, c), src=x_refs[a]) for j, chip in enumerate(chips)])

        def passed(a):
            return [copy(a, 4 + j, (*chip, c), sibling) for j, chip in enumerate(chips)]

        @pl.when(step == 0)
        def _():
            for a in range(self.n):
                mine(a).start()
                for cp in first(a):
                    cp.start()

        @pl.when(step == self.forward_step)
        def _():
            for j, chip in enumerate(chips):
                for a in range(self.n):
                    copy(a, 1 + j, (*chip, c), me).wait_recv()
                    copy(a, 4 + j, (*chip, c), sibling).start()

        def finish():
            @pl.when(step == n_steps - 1)
            def _():
                for a in range(self.n):
                    copy(a, 0, sibling, me).wait_recv()
                    for j, chip in enumerate(chips):
                        copy(a, 4 + j, (*chip, 1 - c), me).wait_recv()
                    for cp in first(a) + passed(a):
                        cp.wait_send()
                    mine(a).wait()

        return finish


class _ReduceScatter:
    def __init__(self, grads):
        self.shapes = [g.shape[1:] for g in grads]
        self.n = len(grads)
        self.items = tuple((a, r) for r in (1, 2, 3, 0) for a in range(self.n))
        self.steps = len(self.items) + 2

    def out_shape(self):
        own = [jax.ShapeDtypeStruct(s, F32) for s in self.shapes]
        ici = [jax.ShapeDtypeStruct((N_CHIP - 1,) + s, BF16) for s in self.shapes]
        land = [jax.ShapeDtypeStruct((N_CHIP,) + s, F32) for s in self.shapes]
        return own + ici + land

    def scratch_shapes(self):
        n_items = len(self.items)
        return ([pltpu.VMEM((2,) + s, F32) for s in self.shapes]
                + [pltpu.VMEM((N_CHIP - 1,) + s, BF16) for s in self.shapes]
                + [pltpu.VMEM(s, F32) for s in self.shapes]
                + [pltpu.SemaphoreType.DMA((self.n * N_CHIP,))] * 2
                + [pltpu.SemaphoreType.DMA((2 * n_items,))]
                + [pltpu.SemaphoreType.DMA((self.n * (N_CHIP - 1),))] * 2
                + [pltpu.SemaphoreType.DMA((self.n,))])

    def emit(self, step, n_steps, g_refs, out_refs, scratch):
        assert n_steps > self.steps
        n = self.n
        own_refs, ici_refs, land_refs = out_refs[:n], out_refs[n:2 * n], out_refs[2 * n:]
        stage, pair_bf, pair_own = scratch[:n], scratch[n:2 * n], scratch[2 * n:3 * n]
        sib_send, sib_recv, load_sems, ici_send, ici_recv, own_sems = scratch[3 * n:]
        x, y, c = lax.axis_index("x"), lax.axis_index("y"), lax.axis_index("c")

        def chip_of(r):
            return (x ^ (r >> 1), y ^ (r & 1))

        def block_of(r, core):
            cx, cy = chip_of(r)
            return 4 * cx + 2 * cy + core

        def to_sibling(a, r):
            return pltpu.make_async_remote_copy(
                src_ref=g_refs[a].at[block_of(r, 1 - c)], dst_ref=land_refs[a].at[r],
                send_sem=sib_send.at[a * N_CHIP + r], recv_sem=sib_recv.at[a * N_CHIP + r],
                device_id=(x, y, 1 - c), device_id_type=MESH)

        def loads(k):
            a, r = self.items[k]
            return (pltpu.make_async_copy(g_refs[a].at[block_of(r, c)], stage[a].at[0], load_sems.at[2 * k]),
                    pltpu.make_async_copy(land_refs[a].at[r], stage[a].at[1], load_sems.at[2 * k + 1]))

        def to_owner(k):
            a, r = self.items[k]
            if r == 0:
                return pltpu.make_async_copy(pair_own[a], own_refs[a], own_sems.at[a])
            return pltpu.make_async_remote_copy(
                src_ref=pair_bf[a].at[r - 1], dst_ref=ici_refs[a].at[r - 1],
                send_sem=ici_send.at[a * (N_CHIP - 1) + r - 1], recv_sem=ici_recv.at[a * (N_CHIP - 1) + r - 1],
                device_id=(*chip_of(r), c), device_id_type=MESH)

        @pl.when(step == 0)
        def _():
            for a, r in self.items:
                to_sibling(a, r).start()

        for k, (a, r) in enumerate(self.items):
            @pl.when(step == 1 + k)
            def _(k=k, a=a, r=r):
                to_sibling(a, r).wait_recv()
                for cp in loads(k):
                    cp.start()

            @pl.when(step == 2 + k)
            def _(k=k, a=a, r=r):
                for cp in loads(k):
                    cp.wait()
                total = stage[a][0] + stage[a][1]
                if r == 0:
                    pair_own[a][...] = total
                else:
                    pair_bf[a][r - 1] = total.astype(BF16)
                to_owner(k).start()

        def finish():
            @pl.when(step == n_steps - 1)
            def _():
                for k, (a, r) in enumerate(self.items):
                    if r == 0:
                        to_owner(k).wait()
                    else:
                        to_owner(k).wait_send()
                        to_owner(k).wait_recv()
                for a, r in self.items:
                    to_sibling(a, r).wait_send()

        return finish


def _fwd_proj(x, norm_g, wt, later):
    S = x.shape[0]

    n_steps = S // TM
    ag = _AllGatherInSteps(later)

    def body(x_ref, g_ref, wt_ref, *rest):
        later_refs, rest = rest[:ag.n], rest[ag.n:]
        xn_ref, pa_ref, pc_ref = rest[:3]
        gathered, ag_scratch = rest[3:3 + ag.n], rest[3 + ag.n:]
        step = pl.program_id(0)
        finish = ag.emit(step, n_steps, later_refs, gathered, ag_scratch)
        xv = x_ref[...]
        r = lax.rsqrt(jnp.mean(xv * xv, axis=-1, keepdims=True) + EPS)
        xn = (xv * r * g_ref[...]).astype(BF16)
        xn_ref[...] = xn
        pa_ref[:, 0:512] = _nt(xn, wt_ref[0:512, :]).astype(ACT)
        pa_ref[:, 512:1024] = _nt(xn, wt_ref[768:1280, :]).astype(ACT)
        pa_ref[:, 1024:1280] = _nt(xn, wt_ref[512:768, :]).astype(ACT)
        pc_ref[...] = _nt(xn, wt_ref[1280:3328, :]).astype(ACT)
        finish()

    any_spec = pl.BlockSpec(memory_space=pl.ANY)
    outs = pl.pallas_call(
        body,
        name="fwd_proj_all_gather",
        grid=(n_steps,),
        in_specs=[
            pl.BlockSpec((TM, D_MODEL), lambda i: (i, 0)),
            pl.BlockSpec((1, D_MODEL), lambda i: (0, 0)),
            pl.BlockSpec((IN_W, D_MODEL), lambda i: (0, 0)),
        ] + [any_spec] * ag.n,
        out_specs=[
            pl.BlockSpec((TM, D_MODEL), lambda i: (i, 0)),
            pl.BlockSpec((TM, PA_W), lambda i: (i, 0)),
            pl.BlockSpec((TM, PC_W), lambda i: (i, 0)),
        ] + [any_spec] * ag.n,
        out_shape=[
            jax.ShapeDtypeStruct((S, D_MODEL), BF16),
            jax.ShapeDtypeStruct((S, PA_W), ACT),
            jax.ShapeDtypeStruct((S, PC_W), ACT),
        ] + ag.out_shape(),
        scratch_shapes=ag.scratch_shapes(),
        compiler_params=_params(("arbitrary",)),
    )(x, norm_g, wt, *later)
    return outs[0], outs[1], outs[2], outs[3:]


def _rope(t, tab):
    return (t * tab[:, 0:128] + pltpu.roll(t, 120, 1) * tab[:, 128:256]
            + pltpu.roll(t, 8, 1) * tab[:, 256:384])


def _rope_t(d, tab):
    return (d * tab[:, 0:128] + pltpu.roll(d * tab[:, 128:256], 8, 1)
            + pltpu.roll(d * tab[:, 256:384], 120, 1))


def _fill_kv(kall, kvc_ref, kvp_ref, tabc_ref, tabp_ref):
    for lo, kv_ref, tab_ref, n in ((0, kvp_ref, tabp_ref, BLK), (BLK, kvc_ref, tabc_ref, TQ)):
        k = _rope(kv_ref[:, 0:128].astype(F32), tab_ref[...])
        v = kv_ref[:, 128:256].astype(F32)
        kall[0, lo:lo + n, :] = k.astype(BF16)
        kall[1, lo:lo + n, :] = pltpu.roll(k, 64, 1).astype(BF16)
        kall[2, lo:lo + n, :] = v.astype(BF16)
        kall[3, lo:lo + n, :] = pltpu.roll(v, 64, 1).astype(BF16)


HEADS = (((0, 0), (1, 0), (2, 1), (3, 1)), ((0, 1), (1, 1), (2, 0), (3, 0)))


def _upper():
    qi = lax.broadcasted_iota(jnp.int32, (BLK, BLK), 0)
    kj = lax.broadcasted_iota(jnp.int32, (BLK, BLK), 1)
    return kj > qi


def _merge(upper, both, rows):
    return jnp.where(upper, both[rows, 0:BLK], both[rows, BLK:2 * BLK])


def _split_store(ref, s, rows, upper, val):
    ref[s, rows, 0:BLK] = jnp.where(upper, val, 0.0).astype(BF16)
    ref[s, rows, BLK:2 * BLK] = jnp.where(upper, 0.0, val).astype(BF16)


def _stack_heads(ref, s, half, pairs):
    for a, (p, e) in enumerate(HEADS[s]):
        ref[s, a * BLK:(a + 1) * BLK, :] = jnp.where(half[e], pairs[p], 0.0).astype(BF16)


def _unstack_pair(half, outs, p):
    lo = 0 if p < 2 else 1
    rows = slice(p * BLK, (p + 1) * BLK)
    return jnp.where(half[0], outs[lo][rows, :], outs[1 - lo][rows, :])


def _softmax(sm, sink):
    m = jnp.maximum(jnp.max(sm, axis=-1, keepdims=True), sink)
    p = jnp.exp(sm - m)
    es = jnp.exp(sink - m)
    inv = 1.0 / (jnp.sum(p, axis=-1, keepdims=True) + es)
    return p * inv, es * inv


def _attn_specs(tile):
    nb = TQ // BLK
    prev = lambda i: jnp.maximum(tile(i) * nb - 1, 0)
    return [
        pl.BlockSpec(memory_space=pltpu.SMEM),
        pl.BlockSpec((TQ, ATTN_W), lambda i: (tile(i), 0)),
        pl.BlockSpec((TQ, ATTN_W), lambda i: (tile(i), 1)),
        pl.BlockSpec((TQ, 2 * KV_W), lambda i: (tile(i), 4)),
        pl.BlockSpec((BLK, 2 * KV_W), lambda i: (prev(i), 4)),
        pl.BlockSpec((TQ, 384), lambda i: (tile(i), 0)),
        pl.BlockSpec((BLK, 384), lambda i: (prev(i), 0)),
    ]


def _attn_fwd(pa, tab, sinks):
    S = pa.shape[0]
    nb = TQ // BLK

    def body(sink_ref, q_ref, g_ref, kvc_ref, kvp_ref, tabc_ref, tabp_ref, o_ref, kall, q_sc, p_sc):
        i = pl.program_id(0)
        _fill_kv(kall, kvc_ref, kvp_ref, tabc_ref, tabp_ref)
        lane = lax.broadcasted_iota(jnp.int32, (BLK, 128), 1)
        half = [lane < HEAD_DIM, lane >= HEAD_DIM]
        upper = _upper()
        for j in range(nb):
            rq = slice(j * BLK, (j + 1) * BLK)
            rk = slice(j * BLK, (j + 2) * BLK)
            tab = tabc_ref[rq, :]
            qr = [_rope(q_ref[rq, p * 128:(p + 1) * 128].astype(F32), tab) * 0.125 for p in range(4)]
            outs = []
            for s in range(2):
                _stack_heads(q_sc, s, half, qr)
                sf = _nt(q_sc[s], kall[s, rk, :])
                if j == 0:
                    sf = sf + jnp.where((i == 0) & (lax.broadcasted_iota(jnp.int32, (1, 2 * BLK), 1) < BLK),
                                        -jnp.inf, 0.0)
                for a, (p, e) in enumerate(HEADS[s]):
                    ra = slice(a * BLK, (a + 1) * BLK)
                    prob, _ = _softmax(_merge(upper, sf, ra), sink_ref[2 * p + e])
                    _split_store(p_sc, s, ra, upper, prob)
                outs.append(_nn(p_sc[s], kall[2 + s, rk, :]))
            for p in range(4):
                cols = slice(p * 128, (p + 1) * 128)
                o_ref[rq, cols] = (_unstack_pair(half, outs, p) * _silu(g_ref[rq, cols].astype(F32))).astype(BF16)

    return pl.pallas_call(
        body,
        name="attn_fwd",
        grid=(S // TQ,),
        in_specs=_attn_specs(lambda i: i),
        out_specs=pl.BlockSpec((TQ, ATTN_W), lambda i: (i, 0)),
        out_shape=jax.ShapeDtypeStruct((S, ATTN_W), BF16),
        scratch_shapes=[
            pltpu.VMEM((4, BLK + TQ, 128), BF16),
            pltpu.VMEM((2, 4 * BLK, 128), BF16),
            pltpu.VMEM((2, 4 * BLK, 2 * BLK), BF16),
        ],
        compiler_params=_params(("arbitrary",)),
    )(sinks, pa, pa, pa, pa, tab, tab)


def _shift_down(u, halo_ref, has_prev):
    def halo_u(r):
        hu = halo_ref[r:r + 1, 512:1024].astype(F32) * halo_ref[r:r + 1, 1024:1536].astype(F32)
        return jnp.where(has_prev, hu, 0.0)

    row = lax.broadcasted_iota(jnp.int32, u.shape, 0)
    um1 = jnp.where(row == 0, halo_u(HALO - 1), pltpu.roll(u, 1, 0))
    um2 = jnp.where(row == 0, halo_u(HALO - 2), jnp.where(row == 1, halo_u(HALO - 1), pltpu.roll(u, 2, 0)))
    return um1, um2


def _conv_fwd(pc, conv_w):
    S = pc.shape[0]

    def body(pc_ref, halo_ref, w_ref, o_ref):
        b = pc_ref[:, 0:512].astype(F32)
        u = pc_ref[:, 512:1024].astype(F32) * pc_ref[:, 1024:1536].astype(F32)
        gc = pc_ref[:, 1536:2048].astype(F32)
        um1, um2 = _shift_down(u, halo_ref, pl.program_id(0) > 0)
        cv = w_ref[0:1, :] * um2 + w_ref[1:2, :] * um1 + w_ref[2:3, :] * u
        o_ref[...] = (b * cv * _silu(gc)).astype(BF16)

    return pl.pallas_call(
        body,
        name="conv_fwd",
        grid=(S // TC,),
        in_specs=[
            pl.BlockSpec((TC, PC_W), lambda i: (i, 0)),
            pl.BlockSpec((HALO, PC_W), lambda i: (jnp.maximum(i * (TC // HALO) - 1, 0), 0)),
            pl.BlockSpec((CONV_K, CONV_W), lambda i: (0, 0)),
        ],
        out_specs=pl.BlockSpec((TC, CONV_W), lambda i: (i, 0)),
        out_shape=jax.ShapeDtypeStruct((S, CONV_W), BF16),
        compiler_params=_params(("arbitrary",)),
    )(pc, pc, conv_w)


def _out_loss(x, target, ya, yc, w_out, final_g):
    S = x.shape[0]

    def body(x_ref, t_ref, ya_ref, yc_ref, wo_ref, fg_ref, dh_ref, dmix_ref, gwo_ref, gfg_ref, loss_ref):
        @pl.when(pl.program_id(0) == 0)
        def _():
            gwo_ref[...] = jnp.zeros_like(gwo_ref)
            gfg_ref[...] = jnp.zeros_like(gfg_ref)
            loss_ref[...] = jnp.zeros_like(loss_ref)

        mix = jnp.concatenate([ya_ref[...], yc_ref[...]], axis=1)
        wo = wo_ref[...]
        fg = fg_ref[...]
        h = x_ref[...] + _nn(mix, wo)
        r = lax.rsqrt(jnp.mean(h * h, axis=-1, keepdims=True) + EPS)
        n = h * r
        err = n * fg - t_ref[...]
        loss_ref[...] += 0.5 * jnp.sum(jnp.mean(err * err, axis=-1, keepdims=True), axis=0, keepdims=True)
        dy = err * (1.0 / D_MODEL)
        gfg_ref[...] += jnp.sum(dy * n, axis=0, keepdims=True)
        dyg = dy * fg
        dh = r * (dyg - n * jnp.mean(dyg * n, axis=-1, keepdims=True))
        dh_ref[...] = dh
        dhb = dh.astype(BF16)
        dmix_ref[...] = _nt(dhb, wo).astype(ACT)
        gwo_ref[...] += _tn(mix, dhb)

    row = lambda i: (i, 0)
    fixed = lambda i: (0, 0)
    return pl.pallas_call(
        body,
        name="out_loss",
        grid=(S // TM,),
        in_specs=[
            pl.BlockSpec((TM, D_MODEL), row),
            pl.BlockSpec((TM, D_MODEL), row),
            pl.BlockSpec((TM, ATTN_W), row),
            pl.BlockSpec((TM, CONV_W), row),
            pl.BlockSpec((D_MODEL, D_MODEL), fixed),
            pl.BlockSpec((1, D_MODEL), fixed),
        ],
        out_specs=[
            pl.BlockSpec((TM, D_MODEL), row),
            pl.BlockSpec((TM, D_MODEL), row),
            pl.BlockSpec((D_MODEL, D_MODEL), fixed),
            pl.BlockSpec((1, D_MODEL), fixed),
            pl.BlockSpec((1, 1), fixed),
        ],
        out_shape=[
            jax.ShapeDtypeStruct((S, D_MODEL), F32),
            jax.ShapeDtypeStruct((S, D_MODEL), ACT),
            jax.ShapeDtypeStruct((D_MODEL, D_MODEL), F32),
            jax.ShapeDtypeStruct((1, D_MODEL), F32),
            jax.ShapeDtypeStruct((1, 1), F32),
        ],
        compiler_params=_params(("arbitrary",)),
    )(x, target, ya, yc, w_out, final_g)


def _attn_bwd(pa, dmix, tab, sinks):
    S = pa.shape[0]
    nt = S // TQ
    nb = TQ // BLK

    def body(sink_ref, q_ref, g_ref, kvc_ref, kvp_ref, tabc_ref, tabp_ref, dm_ref,
             d_ref, dsink_ref, kall, dkv, carry, q_sc, do_sc, p_sc, ds_sc):
        step = pl.program_id(0)
        i = nt - 1 - step

        @pl.when(step == 0)
        def _():
            carry[...] = jnp.zeros_like(carry)
            dsink_ref[...] = jnp.zeros_like(dsink_ref)

        _fill_kv(kall, kvc_ref, kvp_ref, tabc_ref, tabp_ref)
        dkv[0:TQ, :] = jnp.zeros((TQ, 2 * KV_W), F32)
        dkv[TQ:TQ + BLK, :] = carry[...]
        lane = lax.broadcasted_iota(jnp.int32, (BLK, 128), 1)
        half = [lane < HEAD_DIM, lane >= HEAD_DIM]
        upper = _upper()
        for j in range(nb):
            rq = slice(j * BLK, (j + 1) * BLK)
            rk = slice(j * BLK, (j + 2) * BLK)
            tab = tabc_ref[rq, :]
            pair = [slice(p * 128, (p + 1) * 128) for p in range(4)]
            qr = [_rope(q_ref[rq, c].astype(F32), tab) * 0.125 for c in pair]
            g = [g_ref[rq, c].astype(F32) for c in pair]
            da = [dm_ref[rq, c].astype(F32) for c in pair]
            do = [da[p] * _silu(g[p]) for p in range(4)]
            outs, dqs, dks, dvs = [], [], [], []
            for s in range(2):
                kk = kall[s, rk, :]
                vv = kall[2 + s, rk, :]
                _stack_heads(q_sc, s, half, qr)
                _stack_heads(do_sc, s, half, do)
                sf = _nt(q_sc[s], kk)
                if j == 0:
                    sf = sf + jnp.where((i == 0) & (lax.broadcasted_iota(jnp.int32, (1, 2 * BLK), 1) < BLK),
                                        -jnp.inf, 0.0)
                dpf = _nt(do_sc[s], vv)
                for a, (p, e) in enumerate(HEADS[s]):
                    h = 2 * p + e
                    ra = slice(a * BLK, (a + 1) * BLK)
                    prob, psink = _softmax(_merge(upper, sf, ra), sink_ref[h])
                    _split_store(p_sc, s, ra, upper, prob)
                    dprob = _merge(upper, dpf, ra)
                    dsum = jnp.sum(dprob * prob, axis=-1, keepdims=True)
                    _split_store(ds_sc, s, ra, upper, prob * (dprob - dsum))
                    dsink_ref[h:h + 1, :] += jnp.broadcast_to(
                        -jnp.sum(psink * dsum, axis=0, keepdims=True), (1, 128))
                outs.append(_nn(p_sc[s], vv))
                dqs.append(_nn(ds_sc[s], kk))
                dks.append(_tn(ds_sc[s], q_sc[s]))
                dvs.append(_tn(p_sc[s], do_sc[s]))
            for p in range(4):
                d_ref[rq, pair[p]] = _rope_t(_unstack_pair(half, dqs, p) * 0.125, tab).astype(BF16)
                d_ref[rq, 512 + p * 128:512 + (p + 1) * 128] = (
                    da[p] * _unstack_pair(half, outs, p) * _dsilu(g[p])).astype(BF16)
            dkv[rk, 0:128] += dks[0] + pltpu.roll(dks[1], 64, 1)
            dkv[rk, 128:256] += dvs[0] + pltpu.roll(dvs[1], 64, 1)
        d_ref[:, 1024:1152] = _rope_t(dkv[BLK:BLK + TQ, 0:128], tabc_ref[...]).astype(BF16)
        d_ref[:, 1152:1280] = dkv[BLK:BLK + TQ, 128:256].astype(BF16)
        carry[...] = dkv[0:BLK, :]

    rev = lambda s: nt - 1 - s
    return pl.pallas_call(
        body,
        name="attn_bwd",
        grid=(nt,),
        in_specs=_attn_specs(rev) + [pl.BlockSpec((TQ, ATTN_W), lambda s: (nt - 1 - s, 0))],
        out_specs=[
            pl.BlockSpec((TQ, PA_W), lambda s: (nt - 1 - s, 0)),
            pl.BlockSpec((8, 128), lambda s: (0, 0)),
        ],
        out_shape=[
            jax.ShapeDtypeStruct((S, PA_W), BF16),
            jax.ShapeDtypeStruct((8, 128), F32),
        ],
        scratch_shapes=[
            pltpu.VMEM((4, BLK + TQ, 128), BF16),
            pltpu.VMEM((BLK + TQ, 2 * KV_W), F32),
            pltpu.VMEM((BLK, 2 * KV_W), F32),
            pltpu.VMEM((2, 4 * BLK, 128), BF16),
            pltpu.VMEM((2, 4 * BLK, 128), BF16),
            pltpu.VMEM((2, 4 * BLK, 2 * BLK), BF16),
            pltpu.VMEM((2, 4 * BLK, 2 * BLK), BF16),
        ],
        compiler_params=_params(("arbitrary",)),
    )(sinks, pa, pa, pa, pa, tab, tab, dmix)


def _conv_bwd(pc, dmix, conv_w):
    S = pc.shape[0]
    nt = S // TC

    def body(pc_ref, prev_ref, next_ref, dm_ref, dmn_ref, w_ref, d_ref, gw_ref):
        i = pl.program_id(0)

        @pl.when(i == 0)
        def _():
            gw_ref[...] = jnp.zeros_like(gw_ref)

        def parts(ref):
            return (ref[:, 0:512].astype(F32), ref[:, 512:1024].astype(F32),
                    ref[:, 1024:1536].astype(F32), ref[:, 1536:2048].astype(F32))

        w0, w1, w2 = w_ref[0:1, :], w_ref[1:2, :], w_ref[2:3, :]
        b, c, hh, gc = parts(pc_ref)
        u = c * hh
        um1, um2 = _shift_down(u, prev_ref, i > 0)
        cv = w0 * um2 + w1 * um1 + w2 * u
        sg = _silu(gc)
        dy = dm_ref[...].astype(F32)
        dcv = dy * b * sg

        def next_dcv(r):
            nd = (dmn_ref[r:r + 1, :].astype(F32) * next_ref[r:r + 1, 0:512].astype(F32)
                  * _silu(next_ref[r:r + 1, 1536:2048].astype(F32)))
            return jnp.where(i < nt - 1, nd, 0.0)

        row = lax.broadcasted_iota(jnp.int32, (TC, CONV_W), 0)
        dp1 = jnp.where(row == TC - 1, next_dcv(0), pltpu.roll(dcv, TC - 1, 0))
        dp2 = jnp.where(row == TC - 1, next_dcv(1),
                        jnp.where(row == TC - 2, next_dcv(0), pltpu.roll(dcv, TC - 2, 0)))
        du = w2 * dcv + w1 * dp1 + w0 * dp2
        d_ref[:, 0:512] = (dy * cv * sg).astype(BF16)
        d_ref[:, 512:1024] = (du * hh).astype(BF16)
        d_ref[:, 1024:1536] = (du * c).astype(BF16)
        d_ref[:, 1536:2048] = (dy * b * cv * _dsilu(gc)).astype(BF16)
        gw_ref[0:1, :] += jnp.sum(dcv * um2, axis=0, keepdims=True)
        gw_ref[1:2, :] += jnp.sum(dcv * um1, axis=0, keepdims=True)
        gw_ref[2:3, :] += jnp.sum(dcv * u, axis=0, keepdims=True)

    t8 = TC // HALO
    return pl.pallas_call(
        body,
        name="conv_bwd",
        grid=(nt,),
        in_specs=[
            pl.BlockSpec((TC, PC_W), lambda i: (i, 0)),
            pl.BlockSpec((HALO, PC_W), lambda i: (jnp.maximum(i * t8 - 1, 0), 0)),
            pl.BlockSpec((HALO, PC_W), lambda i: (jnp.minimum((i + 1) * t8, nt * t8 - 1), 0)),
            pl.BlockSpec((TC, CONV_W), lambda i: (i, 1)),
            pl.BlockSpec((HALO, CONV_W), lambda i: (jnp.minimum((i + 1) * t8, nt * t8 - 1), 1)),
            pl.BlockSpec((CONV_K, CONV_W), lambda i: (0, 0)),
        ],
        out_specs=[
            pl.BlockSpec((TC, PC_W), lambda i: (i, 0)),
            pl.BlockSpec((CONV_K, CONV_W), lambda i: (0, 0)),
        ],
        out_shape=[
            jax.ShapeDtypeStruct((S, PC_W), BF16),
            jax.ShapeDtypeStruct((CONV_K, CONV_W), F32),
        ],
        compiler_params=_params(("arbitrary",)),
    )(pc, pc, pc, dmix, dmix, conv_w)


def _grad_x(da, dc, wt, x, dh, norm_g, grads):
    S = x.shape[0]
    n_steps = S // TM
    rs = _ReduceScatter(grads)
    n_rs_out = len(rs.out_shape())

    def body(da_ref, dc_ref, wt_ref, x_ref, dh_ref, g_ref, *rest):
        grad_refs, rest = rest[:rs.n], rest[rs.n:]
        gx_ref, gng_ref = rest[:2]
        rs_out, rs_scratch = rest[2:2 + n_rs_out], rest[2 + n_rs_out:]
        step = pl.program_id(0)
        finish = rs.emit(step, n_steps, grad_refs, rs_out, rs_scratch)

        @pl.when(step == 0)
        def _():
            gng_ref[...] = jnp.zeros_like(gng_ref)

        dxn = (_nn(da_ref[:, 0:512], wt_ref[0:512, :]) + _nn(da_ref[:, 512:1024], wt_ref[768:1280, :])
               + _nn(da_ref[:, 1024:1280], wt_ref[512:768, :]) + _nn(dc_ref[...], wt_ref[1280:3328, :]))
        xv = x_ref[...]
        r = lax.rsqrt(jnp.mean(xv * xv, axis=-1, keepdims=True) + EPS)
        n = xv * r
        gng_ref[...] += jnp.sum(dxn * n, axis=0, keepdims=True)
        dxg = dxn * g_ref[...]
        gx_ref[...] = dh_ref[...] + r * (dxg - n * jnp.mean(dxg * n, axis=-1, keepdims=True))
        finish()

    row = lambda i: (i, 0)
    fixed = lambda i: (0, 0)
    any_spec = pl.BlockSpec(memory_space=pl.ANY)
    outs = pl.pallas_call(
        body,
        name="grad_x_reduce_scatter",
        grid=(n_steps,),
        in_specs=[
            pl.BlockSpec((TM, PA_W), row),
            pl.BlockSpec((TM, PC_W), row),
            pl.BlockSpec((IN_W, D_MODEL), fixed),
            pl.BlockSpec((TM, D_MODEL), row),
            pl.BlockSpec((TM, D_MODEL), row),
            pl.BlockSpec((1, D_MODEL), fixed),
        ] + [any_spec] * rs.n,
        out_specs=[pl.BlockSpec((TM, D_MODEL), row), pl.BlockSpec((1, D_MODEL), fixed)] + [any_spec] * n_rs_out,
        out_shape=[jax.ShapeDtypeStruct((S, D_MODEL), F32), jax.ShapeDtypeStruct((1, D_MODEL), F32)] + rs.out_shape(),
        scratch_shapes=rs.scratch_shapes(),
        compiler_params=_params(("arbitrary",)),
    )(da, dc, wt, x, dh, norm_g, *grads)
    return outs[0], outs[1], outs[2:2 + rs.n], outs[2 + rs.n:2 + 2 * rs.n]


def _grad_w_in(da, dc, xn):
    S = xn.shape[0]

    def body(da_ref, dc_ref, xn_ref, gw_ref):
        @pl.when(pl.program_id(0) == 0)
        def _():
            gw_ref[...] = jnp.zeros_like(gw_ref)

        xn = xn_ref[...]
        gw_ref[0:512, :] += _tn(da_ref[:, 0:512], xn)
        gw_ref[768:1280, :] += _tn(da_ref[:, 512:1024], xn)
        gw_ref[512:768, :] += _tn(da_ref[:, 1024:1280], xn)
        gw_ref[1280:3328, :] += _tn(dc_ref[...], xn)

    row = lambda i: (i, 0)
    return pl.pallas_call(
        body,
        name="grad_w_in",
        grid=(S // TM,),
        in_specs=[pl.BlockSpec((TM, PA_W), row), pl.BlockSpec((TM, PC_W), row), pl.BlockSpec((TM, D_MODEL), row)],
        out_specs=pl.BlockSpec((IN_W, D_MODEL), lambda i: (0, 0)),
        out_shape=jax.ShapeDtypeStruct((IN_W, D_MODEL), F32),
        compiler_params=_params(("arbitrary",)),
    )(da, dc, xn)


def _sum_chips(own, others, name):
    def body(own_ref, p_ref, o_ref):
        acc = own_ref[...]
        for k in range(N_CHIP - 1):
            acc = acc + p_ref[k].astype(F32)
        o_ref[...] = acc

    return pl.pallas_call(
        body,
        name=name,
        out_shape=jax.ShapeDtypeStruct(own.shape, F32),
        compiler_params=_params(),
    )(own, others)


def _sum_parts(parts, name):
    n = parts.shape[0]

    def body(p_ref, o_ref):
        acc = p_ref[0]
        for k in range(1, n):
            acc = acc + p_ref[k]
        o_ref[...] = acc

    return pl.pallas_call(
        body,
        name=name,
        out_shape=jax.ShapeDtypeStruct(parts.shape[1:], F32),
        compiler_params=_params(),
    )(parts)


def _adamw(w, g, m, v, name):
    c1 = 1.0 - ADAM_B1 ** ADAM_STEP
    c2 = 1.0 - ADAM_B2 ** ADAM_STEP

    def body(w_ref, g_ref, m_ref, v_ref, d_ref, nm_ref, nv_ref):
        g = g_ref[...]
        nm = ADAM_B1 * m_ref[...] + (1.0 - ADAM_B1) * g
        nv = ADAM_B2 * v_ref[...] + (1.0 - ADAM_B2) * (g * g)
        nm_ref[...] = nm
        nv_ref[...] = nv
        d_ref[...] = -ADAM_LR * ((nm / c1) / (jnp.sqrt(nv / c2) + ADAM_EPS) + ADAM_WD * w_ref[...])

    shape = jax.ShapeDtypeStruct(w.shape, F32)
    return pl.pallas_call(
        body,
        name=name,
        out_shape=[shape, shape, shape],
        compiler_params=_params(),
    )(w, g, m, v)


def _rope_table(S):
    half = ROT_DIM // 2
    pos = jnp.arange(S, dtype=jnp.int32).astype(F32)
    inv_freq = ROPE_THETA ** (-jnp.arange(0, ROT_DIM, 2, dtype=F32) / ROT_DIM)
    ang = inv_freq[:, None] * pos[None, :]
    cs = jnp.concatenate([jnp.cos(ang), jnp.sin(ang)], axis=0)
    sel = np.zeros((2 * half, 384), np.float32)
    ones = np.zeros((1, 384), np.float32)
    for l in range(128):
        r = l % HEAD_DIM
        if r < half:
            sel[r, l] = 1.0
            sel[half + r, 128 + l] = -1.0
        elif r < ROT_DIM:
            sel[r - half, l] = 1.0
            sel[half + r - half, 256 + l] = 1.0
        else:
            ones[0, l] = 1.0
    return lax.dot_general(cs, jnp.asarray(sel), TN_DIMS, precision=lax.Precision.HIGHEST) + jnp.asarray(ones)


def kernel(x, norm_g, w_in, sinks, conv_w, w_out, final_g, loss_target, m_norm_g, m_w_in, m_sinks, m_conv_w, m_w_out, m_final_g, v_norm_g, v_w_in, v_sinks, v_conv_w, v_w_out, v_final_g):
    S = x.shape[1]
    me = 4 * lax.axis_index("x") + 2 * lax.axis_index("y") + lax.axis_index("c")
    x2 = x.reshape(S, D_MODEL)
    t2 = loss_target.reshape(S, D_MODEL)
    ng = norm_g.reshape(1, D_MODEL)
    fg = final_g.reshape(1, D_MODEL)

    cw_pad = jnp.zeros((8, 128), F32).at[0:CONV_K, 0:64].set(conv_w)
    (wt,) = _all_gather([w_in.T.astype(BF16)], "all_gather_w_in")

    tab = _rope_table(S)
    xn, pa, pc, (wo, cw_all) = _fwd_proj(x2, ng, wt, [w_out.astype(BF16), cw_pad])
    cw = cw_all.reshape(N_DEV, 8, 128)[:, 0:CONV_K, 0:64].transpose(1, 0, 2).reshape(CONV_K, CONV_W)
    ya = _attn_fwd(pa, tab, sinks)
    yc = _conv_fwd(pc, cw)
    dh, dmix, g_wo, g_fg, loss_part = _out_loss(x2, t2, ya, yc, wo, fg)
    da, g_sinks = _attn_bwd(pa, dmix, tab, sinks)
    dc, g_cw = _conv_bwd(pc, dmix, cw)
    g_wt = _grad_w_in(da, dc, xn)
    grad_x, g_ng, own, others = _grad_x(
        da, dc, wt, x2, dh, ng,
        [g_wt.reshape(N_DEV, SHARD_IN, D_MODEL), g_wo.reshape(N_DEV, SHARD_OUT, D_MODEL)])
    grad_w_in = _sum_chips(own[0], others[0], "sum_grad_w_in").T
    grad_w_out = _sum_chips(own[1], others[1], "sum_grad_w_out")
    small = jnp.concatenate([
        g_ng.reshape(8, 128), g_fg.reshape(8, 128), g_sinks,
        g_cw.reshape(12, 128), jnp.broadcast_to(loss_part, (4, 128))], axis=0)
    (small_all,) = _all_gather([small], "all_gather_small_grads")
    small_sum = _sum_parts(small_all.reshape(N_DEV, SMALL_ROWS, 128), "sum_small_grads")
    grad_norm_g = small_sum[0:8].reshape(D_MODEL)
    grad_final_g = small_sum[8:16].reshape(D_MODEL)
    grad_sinks = small_sum[16:24, 0]
    grad_conv_w = lax.dynamic_slice(small_sum[24:36].reshape(CONV_K, CONV_W), (0, me * 64), (CONV_K, 64))
    loss = small_sum[36, 0]

    d_w_in, nm_w_in, nv_w_in = _adamw(w_in, grad_w_in, m_w_in, v_w_in, "adamw_w_in")
    d_w_out, nm_w_out, nv_w_out = _adamw(w_out, grad_w_out, m_w_out, v_w_out, "adamw_w_out")
    def pack(a, b, c_, d):
        return jnp.concatenate([
            a.reshape(8, 128), b.reshape(8, 128),
            jnp.zeros((8, 128), F32).at[0, 0:8].set(c_).at[1:1 + CONV_K, 0:64].set(d)], axis=1)

    d_s, nm_s, nv_s = _adamw(
        pack(norm_g, final_g, sinks, conv_w), pack(grad_norm_g, grad_final_g, grad_sinks, grad_conv_w),
        pack(m_norm_g, m_final_g, m_sinks, m_conv_w),
        pack(v_norm_g, v_final_g, v_sinks, v_conv_w),
        "adamw_small")

    def unpack(p):
        return (p[:, 0:128].reshape(D_MODEL), p[:, 128:256].reshape(D_MODEL), p[0, 256:264], p[1:1 + CONV_K, 256:320])

    d_ng, d_fg, d_sk, d_cw = unpack(d_s)
    nm_ng, nm_fg, nm_sk, nm_cw = unpack(nm_s)
    nv_ng, nv_fg, nv_sk, nv_cw = unpack(nv_s)

    return (loss, grad_x.reshape(1, S, D_MODEL), grad_norm_g, grad_w_in, grad_sinks, grad_conv_w, grad_w_out, grad_final_g,
            d_ng, d_w_in, d_sk, d_cw, d_w_out, d_fg,
            nm_ng, nm_w_in, nm_sk, nm_cw, nm_w_out, nm_fg,
            nv_ng, nv_w_in, nv_sk, nv_cw, nv_w_out, nv_fg)
```

```python
import numpy as np
import jax
import jax.numpy as jnp
from jax import lax
from jax.experimental import pallas as pl
from jax.experimental.pallas import tpu as pltpu

F32 = jnp.float32
BF16 = jnp.bfloat16
MESH = pl.DeviceIdType.MESH

D_MODEL = 1024
HEAD_DIM = 64
N_Q_HEADS = 8
GROUP = 4
ATTN_W = 512
KV_W = 128
BLK = 128
CONV_W = 512
CONV_K = 3
IN_W = 3328
PA_W = 1280
PC_W = 2048
EPS = 1e-5
ROPE_THETA = 500000.0
ROT_DIM = 16
N_DEV = 8
N_CHIP = 4
SHARD_IN = IN_W // N_DEV
SHARD_OUT = D_MODEL // N_DEV
SMALL_ROWS = 40

ADAM_LR = 0.001
ADAM_B1 = 0.9
ADAM_B2 = 0.999
ADAM_EPS = 1e-08
ADAM_WD = 0.01
ADAM_STEP = 10

ACT = jnp.bfloat16

TM = 512
TQ = 512
TC = 512
HALO = 16
VMEM_LIMIT = 56 * 1024 * 1024

NT_DIMS = (((1,), (1,)), ((), ()))
TN_DIMS = (((0,), (0,)), ((), ()))


def _params(sem=None):
    kw = dict(vmem_limit_bytes=VMEM_LIMIT)
    if sem is not None:
        kw["dimension_semantics"] = sem
    return pltpu.CompilerParams(**kw)


def _nt(a, b):
    return lax.dot_general(a, b, NT_DIMS, preferred_element_type=F32)


def _tn(a, b):
    return lax.dot_general(a, b, TN_DIMS, preferred_element_type=F32)


def _nn(a, b):
    return jnp.dot(a, b, preferred_element_type=F32)


def _silu(g):
    return g * jax.nn.sigmoid(g)


def _dsilu(g):
    s = jax.nn.sigmoid(g)
    return s * (1.0 + g * (1.0 - s))


def _all_gather(arrs, name):
    n_arr = len(arrs)

    def body(*refs):
        x_refs = refs[:n_arr]
        out_refs = refs[n_arr:2 * n_arr]
        send_sems, recv_sems, local_sems = refs[2 * n_arr:]
        x, y, c = lax.axis_index("x"), lax.axis_index("y"), lax.axis_index("c")
        me, sibling = (x, y, c), (x, y, 1 - c)
        chips = [(1 - x, y), (x, 1 - y), (1 - x, 1 - y)]

        def rows(a, px, py, pc):
            m = x_refs[a].shape[0]
            return out_refs[a].at[pl.ds((4 * px + 2 * py + pc) * m, m), :]

        def copy(a, k, block, to, src=None):
            return pltpu.make_async_remote_copy(
                src_ref=rows(a, *block) if src is None else src,
                dst_ref=rows(a, *block),
                send_sem=send_sems.at[a * 7 + k],
                recv_sem=recv_sems.at[a * 7 + k],
                device_id=to,
                device_id_type=MESH,
            )

        mine = [pltpu.make_async_copy(x_refs[a], rows(a, *me), local_sems.at[a]) for a in range(n_arr)]
        for cp in mine:
            cp.start()
        first = []
        for a in range(n_arr):
            first.append(copy(a, 0, me, sibling, src=x_refs[a]))
            first += [copy(a, 1 + j, me, (*chip, c), src=x_refs[a]) for j, chip in enumerate(chips)]
        for cp in first:
            cp.start()
        passed = []
        for j, chip in enumerate(chips):
            for a in range(n_arr):
                copy(a, 1 + j, (*chip, c), me).wait_recv()
                fwd = copy(a, 4 + j, (*chip, c), sibling)
                fwd.start()
                passed.append(fwd)
        for a in range(n_arr):
            copy(a, 0, sibling, me).wait_recv()
            for j, chip in enumerate(chips):
                copy(a, 4 + j, (*chip, 1 - c), me).wait_recv()
        for cp in first + passed:
            cp.wait_send()
        for cp in mine:
            cp.wait()

    vmem = pl.BlockSpec(memory_space=pltpu.VMEM)
    return pl.pallas_call(
        body,
        name=name,
        out_shape=[jax.ShapeDtypeStruct((N_DEV * a.shape[0], a.shape[1]), a.dtype) for a in arrs],
        in_specs=[vmem] * n_arr,
        out_specs=[vmem] * n_arr,
        scratch_shapes=[
            pltpu.SemaphoreType.DMA((7 * n_arr,)),
            pltpu.SemaphoreType.DMA((7 * n_arr,)),
            pltpu.SemaphoreType.DMA((n_arr,)),
        ],
        compiler_params=_params(),
    )(*arrs)


class _AllGatherInSteps:
    forward_step = 3

    def __init__(self, arrs):
        self.blocks = [(a.shape, a.dtype) for a in arrs]
        self.n = len(arrs)

    def out_shape(self):
        return [jax.ShapeDtypeStruct((N_DEV * s[0], s[1]), d) for s, d in self.blocks]

    def scratch_shapes(self):
        return [pltpu.SemaphoreType.DMA((7 * self.n,)), pltpu.SemaphoreType.DMA((7 * self.n,)),
                pltpu.SemaphoreType.DMA((self.n,))]

    def emit(self, step, n_steps, x_refs, out_refs, scratch):
        assert n_steps > self.forward_step + 1
        send_sems, recv_sems, local_sems = scratch
        x, y, c = lax.axis_index("x"), lax.axis_index("y"), lax.axis_index("c")
        me, sibling = (x, y, c), (x, y, 1 - c)
        chips = [(1 - x, y), (x, 1 - y), (1 - x, 1 - y)]

        def rows(a, px, py, pc):
            m = self.blocks[a][0][0]
            return out_refs[a].at[pl.ds((4 * px + 2 * py + pc) * m, m), :]

        def copy(a, k, block, to, src=None):
            return pltpu.make_async_remote_copy(
                src_ref=rows(a, *block) if src is None else src, dst_ref=rows(a, *block),
                send_sem=send_sems.at[a * 7 + k], recv_sem=recv_sems.at[a * 7 + k],
                device_id=to, device_id_type=MESH)

        def mine(a):
            return pltpu.make_async_copy(x_refs[a], rows(a, *me), local_sems.at[a])

        def first(a):
            return ([copy(a, 0, me, sibling, src=x_refs[a])]
                    + [copy(a, 1 + j, me, (*chip, c), src=x_refs[a]) for j, chip in enumerate(chips)])

        def passed(a):
            return [copy(a, 4 + j, (*chip, c), sibling) for j, chip in enumerate(chips)]

        @pl.when(step == 0)
        def _():
            for a in range(self.n):
                mine(a).start()
                for cp in first(a):
                    cp.start()

        @pl.when(step == self.forward_step)
        def _():
            for j, chip in enumerate(chips):
                for a in range(self.n):
                    copy(a, 1 + j, (*chip, c), me).wait_recv()
                    copy(a, 4 + j, (*chip, c), sibling).start()

        def finish():
            @pl.when(step == n_steps - 1)
            def _():
                for a in range(self.n):
                    copy(a, 0, sibling, me).wait_recv()
                    for j, chip in enumerate(chips):
                        copy(a, 4 + j, (*chip, 1 - c), me).wait_recv()
                    for cp in first(a) + passed(a):
                        cp.wait_send()
                    mine(a).wait()

        return finish


class _ReduceScatter:
    def __init__(self, grads):
        self.shapes = [g.shape[1:] for g in grads]
        self.n = len(grads)
        self.items = tuple((a, r) for r in (1, 2, 3, 0) for a in range(self.n))
        self.steps = len(self.items) + 2

    def out_shape(self):
        own = [jax.ShapeDtypeStruct(s, F32) for s in self.shapes]
        ici = [jax.ShapeDtypeStruct((N_CHIP - 1,) + s, BF16) for s in self.shapes]
        land = [jax.ShapeDtypeStruct((N_CHIP,) + s, F32) for s in self.shapes]
        return own + ici + land

    def scratch_shapes(self):
        n_items = len(self.items)
        return ([pltpu.VMEM((2,) + s, F32) for s in self.shapes]
                + [pltpu.VMEM((N_CHIP - 1,) + s, BF16) for s in self.shapes]
                + [pltpu.VMEM(s, F32) for s in self.shapes]
                + [pltpu.SemaphoreType.DMA((self.n * N_CHIP,))] * 2
                + [pltpu.SemaphoreType.DMA((2 * n_items,))]
                + [pltpu.SemaphoreType.DMA((self.n * (N_CHIP - 1),))] * 2
                + [pltpu.SemaphoreType.DMA((self.n,))])

    def emit(self, step, n_steps, g_refs, out_refs, scratch):
        assert n_steps > self.steps
        n = self.n
        own_refs, ici_refs, land_refs = out_refs[:n], out_refs[n:2 * n], out_refs[2 * n:]
        stage, pair_bf, pair_own = scratch[:n], scratch[n:2 * n], scratch[2 * n:3 * n]
        sib_send, sib_recv, load_sems, ici_send, ici_recv, own_sems = scratch[3 * n:]
        x, y, c = lax.axis_index("x"), lax.axis_index("y"), lax.axis_index("c")

        def chip_of(r):
            return (x ^ (r >> 1), y ^ (r & 1))

        def block_of(r, core):
            cx, cy = chip_of(r)
            return 4 * cx + 2 * cy + core

        def to_sibling(a, r):
            return pltpu.make_async_remote_copy(
                src_ref=g_refs[a].at[block_of(r, 1 - c)], dst_ref=land_refs[a].at[r],
                send_sem=sib_send.at[a * N_CHIP + r], recv_sem=sib_recv.at[a * N_CHIP + r],
                device_id=(x, y, 1 - c), device_id_type=MESH)

        def loads(k):
            a, r = self.items[k]
            return (pltpu.make_async_copy(g_refs[a].at[block_of(r, c)], stage[a].at[0], load_sems.at[2 * k]),
                    pltpu.make_async_copy(land_refs[a].at[r], stage[a].at[1], load_sems.at[2 * k + 1]))

        def to_owner(k):
            a, r = self.items[k]
            if r == 0:
                return pltpu.make_async_copy(pair_own[a], own_refs[a], own_sems.at[a])
            return pltpu.make_async_remote_copy(
                src_ref=pair_bf[a].at[r - 1], dst_ref=ici_refs[a].at[r - 1],
                send_sem=ici_send.at[a * (N_CHIP - 1) + r - 1], recv_sem=ici_recv.at[a * (N_CHIP - 1) + r - 1],
                device_id=(*chip_of(r), c), device_id_type=MESH)

        @pl.when(step == 0)
        def _():
            for a, r in self.items:
                to_sibling(a, r).start()

        for k, (a, r) in enumerate(self.items):
            @pl.when(step == 1 + k)
            def _(k=k, a=a, r=r):
                to_sibling(a, r).wait_recv()
                for cp in loads(k):
                    cp.start()

            @pl.when(step == 2 + k)
            def _(k=k, a=a, r=r):
                for cp in loads(k):
                    cp.wait()
                total = stage[a][0] + stage[a][1]
                if r == 0:
                    pair_own[a][...] = total
                else:
                    pair_bf[a][r - 1] = total.astype(BF16)
                to_owner(k).start()

        def finish():
            @pl.when(step == n_steps - 1)
            def _():
                for k, (a, r) in enumerate(self.items):
                    if r == 0:
                        to_owner(k).wait()
                    else:
                        to_owner(k).wait_send()
                        to_owner(k).wait_recv()
                for a, r in self.items:
                    to_sibling(a, r).wait_send()

        return finish


def _fwd_proj(x, norm_g, wt, later):
    S = x.shape[0]

    n_steps = S // TM
    ag = _AllGatherInSteps(later)

    def body(x_ref, g_ref, wt_ref, *rest):
        later_refs, rest = rest[:ag.n], rest[ag.n:]
        xn_ref, pa_ref, pc_ref = rest[:3]
        gathered, ag_scratch = rest[3:3 + ag.n], rest[3 + ag.n:]
        step = pl.program_id(0)
        finish = ag.emit(step, n_steps, later_refs, gathered, ag_scratch)
        xv = x_ref[...]
        r = lax.rsqrt(jnp.mean(xv * xv, axis=-1, keepdims=True) + EPS)
        xn = (xv * r * g_ref[...]).astype(BF16)
        xn_ref[...] = xn
        pa_ref[:, 0:512] = _nt(xn, wt_ref[0:512, :]).astype(ACT)
        pa_ref[:, 512:1024] = _nt(xn, wt_ref[768:1280, :]).astype(ACT)
        pa_ref[:, 1024:1280] = _nt(xn, wt_ref[512:768, :]).astype(ACT)
        pc_ref[...] = _nt(xn, wt_ref[1280:3328, :]).astype(ACT)
        finish()

    any_spec = pl.BlockSpec(memory_space=pl.ANY)
    outs = pl.pallas_call(
        body,
        name="fwd_proj_all_gather",
        grid=(n_steps,),
        in_specs=[
            pl.BlockSpec((TM, D_MODEL), lambda i: (i, 0)),
            pl.BlockSpec((1, D_MODEL), lambda i: (0, 0)),
            pl.BlockSpec((IN_W, D_MODEL), lambda i: (0, 0)),
        ] + [any_spec] * ag.n,
        out_specs=[
            pl.BlockSpec((TM, D_MODEL), lambda i: (i, 0)),
            pl.BlockSpec((TM, PA_W), lambda i: (i, 0)),
            pl.BlockSpec((TM, PC_W), lambda i: (i, 0)),
        ] + [any_spec] * ag.n,
        out_shape=[
            jax.ShapeDtypeStruct((S, D_MODEL), BF16),
            jax.ShapeDtypeStruct((S, PA_W), ACT),
            jax.ShapeDtypeStruct((S, PC_W), ACT),
        ] + ag.out_shape(),
        scratch_shapes=ag.scratch_shapes(),
        compiler_params=_params(("arbitrary",)),
    )(x, norm_g, wt, *later)
    return outs[0], outs[1], outs[2], outs[3:]


def _rope(t, tab):
    return (t * tab[:, 0:128] + pltpu.roll(t, 120, 1) * tab[:, 128:256]
            + pltpu.roll(t, 8, 1) * tab[:, 256:384])


def _rope_t(d, tab):
    return (d * tab[:, 0:128] + pltpu.roll(d * tab[:, 128:256], 8, 1)
            + pltpu.roll(d * tab[:, 256:384], 120, 1))


def _fill_kv(kall, kvc_ref, kvp_ref, tabc_ref, tabp_ref):
    for lo, kv_ref, tab_ref, n in ((0, kvp_ref, tabp_ref, BLK), (BLK, kvc_ref, tabc_ref, TQ)):
        k = _rope(kv_ref[:, 0:128].astype(F32), tab_ref[...])
        v = kv_ref[:, 128:256].astype(F32)
        kall[0, lo:lo + n, :] = k.astype(BF16)
        kall[1, lo:lo + n, :] = pltpu.roll(k, 64, 1).astype(BF16)
        kall[2, lo:lo + n, :] = v.astype(BF16)
        kall[3, lo:lo + n, :] = pltpu.roll(v, 64, 1).astype(BF16)


HEADS = (((0, 0), (1, 0), (2, 1), (3, 1)), ((0, 1), (1, 1), (2, 0), (3, 0)))


def _upper():
    kj = lax.broadcasted_iota(jnp.int32, (BLK, 4 * BLK), 0)
    qi = lax.broadcasted_iota(jnp.int32, (BLK, 4 * BLK), 1) & (BLK - 1)
    return kj > qi


def _merge(upper, both):
    return jnp.where(upper, both[0:BLK, :], both[BLK:2 * BLK, :])


def _split_store(ref, s, upper, val):
    ref[s, 0:BLK, :] = jnp.where(upper, val, 0.0).astype(BF16)
    ref[s, BLK:2 * BLK, :] = jnp.where(upper, 0.0, val).astype(BF16)


def _sink_rows(sink_ref):
    return [jnp.concatenate([jnp.full((1, BLK), sink_ref[2 * p + e], F32) for p, e in HEADS[s]], axis=1)
            for s in range(2)]


def _stack_heads(ref, s, half, pairs):
    for a, (p, e) in enumerate(HEADS[s]):
        ref[s, a * BLK:(a + 1) * BLK, :] = jnp.where(half[e], pairs[p], 0.0).astype(BF16)


def _unstack_pair(half, outs, p):
    lo = 0 if p < 2 else 1
    rows = slice(p * BLK, (p + 1) * BLK)
    return jnp.where(half[0], outs[lo][rows, :], outs[1 - lo][rows, :])


def _softmax(sm, sinks):
    m = jnp.maximum(jnp.max(sm, axis=0, keepdims=True), sinks)
    p = jnp.exp(sm - m)
    es = jnp.exp(sinks - m)
    inv = 1.0 / (jnp.sum(p, axis=0, keepdims=True) + es)
    return p * inv, es * inv


def _scores(kk, q_stack, first):
    st = _nt(kk, q_stack)
    prev = st[0:BLK, :]
    if first is not None:
        prev = prev + jnp.where(first, -jnp.inf, 0.0)
    return prev, st[BLK:2 * BLK, :]


def _attn_specs(tile):
    nb = TQ // BLK
    prev = lambda i: jnp.maximum(tile(i) * nb - 1, 0)
    return [
        pl.BlockSpec(memory_space=pltpu.SMEM),
        pl.BlockSpec((TQ, ATTN_W), lambda i: (tile(i), 0)),
        pl.BlockSpec((TQ, ATTN_W), lambda i: (tile(i), 1)),
        pl.BlockSpec((TQ, 2 * KV_W), lambda i: (tile(i), 4)),
        pl.BlockSpec((BLK, 2 * KV_W), lambda i: (prev(i), 4)),
        pl.BlockSpec((TQ, 384), lambda i: (tile(i), 0)),
        pl.BlockSpec((BLK, 384), lambda i: (prev(i), 0)),
    ]


def _attn_fwd(pa, tab, sinks):
    S = pa.shape[0]
    nb = TQ // BLK

    def body(sink_ref, q_ref, g_ref, kvc_ref, kvp_ref, tabc_ref, tabp_ref, o_ref, kall, q_sc, p_sc):
        i = pl.program_id(0)
        _fill_kv(kall, kvc_ref, kvp_ref, tabc_ref, tabp_ref)
        lane = lax.broadcasted_iota(jnp.int32, (BLK, 128), 1)
        half = [lane < HEAD_DIM, lane >= HEAD_DIM]
        upper = _upper()
        sinks = _sink_rows(sink_ref)
        for j in range(nb):
            rq = slice(j * BLK, (j + 1) * BLK)
            rk = slice(j * BLK, (j + 2) * BLK)
            tab = tabc_ref[rq, :]
            qr = [_rope(q_ref[rq, p * 128:(p + 1) * 128].astype(F32), tab) * 0.125 for p in range(4)]
            outs = []
            for s in range(2):
                _stack_heads(q_sc, s, half, qr)
                prev, cur = _scores(kall[s, rk, :], q_sc[s], i == 0 if j == 0 else None)
                prob, _ = _softmax(jnp.where(upper, prev, cur), sinks[s])
                _split_store(p_sc, s, upper, prob)
                outs.append(_tn(p_sc[s], kall[2 + s, rk, :]))
            for p in range(4):
                cols = slice(p * 128, (p + 1) * 128)
                o_ref[rq, cols] = (_unstack_pair(half, outs, p) * _silu(g_ref[rq, cols].astype(F32))).astype(BF16)

    return pl.pallas_call(
        body,
        name="attn_fwd",
        grid=(S // TQ,),
        in_specs=_attn_specs(lambda i: i),
        out_specs=pl.BlockSpec((TQ, ATTN_W), lambda i: (i, 0)),
        out_shape=jax.ShapeDtypeStruct((S, ATTN_W), BF16),
        scratch_shapes=[
            pltpu.VMEM((4, BLK + TQ, 128), BF16),
            pltpu.VMEM((2, 4 * BLK, 128), BF16),
            pltpu.VMEM((2, 2 * BLK, 4 * BLK), BF16),
        ],
        compiler_params=_params(("arbitrary",)),
    )(sinks, pa, pa, pa, pa, tab, tab)


def _shift_down(u, halo_ref, has_prev):
    def halo_u(r):
        hu = halo_ref[r:r + 1, 512:1024].astype(F32) * halo_ref[r:r + 1, 1024:1536].astype(F32)
        return jnp.where(has_prev, hu, 0.0)

    row = lax.broadcasted_iota(jnp.int32, u.shape, 0)
    um1 = jnp.where(row == 0, halo_u(HALO - 1), pltpu.roll(u, 1, 0))
    um2 = jnp.where(row == 0, halo_u(HALO - 2), jnp.where(row == 1, halo_u(HALO - 1), pltpu.roll(u, 2, 0)))
    return um1, um2


def _conv_fwd(pc, conv_w):
    S = pc.shape[0]

    def body(pc_ref, halo_ref, w_ref, o_ref):
        b = pc_ref[:, 0:512].astype(F32)
        u = pc_ref[:, 512:1024].astype(F32) * pc_ref[:, 1024:1536].astype(F32)
        gc = pc_ref[:, 1536:2048].astype(F32)
        um1, um2 = _shift_down(u, halo_ref, pl.program_id(0) > 0)
        cv = w_ref[0:1, :] * um2 + w_ref[1:2, :] * um1 + w_ref[2:3, :] * u
        o_ref[...] = (b * cv * _silu(gc)).astype(BF16)

    return pl.pallas_call(
        body,
        name="conv_fwd",
        grid=(S // TC,),
        in_specs=[
            pl.BlockSpec((TC, PC_W), lambda i: (i, 0)),
            pl.BlockSpec((HALO, PC_W), lambda i: (jnp.maximum(i * (TC // HALO) - 1, 0), 0)),
            pl.BlockSpec((CONV_K, CONV_W), lambda i: (0, 0)),
        ],
        out_specs=pl.BlockSpec((TC, CONV_W), lambda i: (i, 0)),
        out_shape=jax.ShapeDtypeStruct((S, CONV_W), BF16),
        compiler_params=_params(("arbitrary",)),
    )(pc, pc, conv_w)


def _out_loss(x, target, ya, yc, w_out, final_g):
    S = x.shape[0]

    def body(x_ref, t_ref, ya_ref, yc_ref, wo_ref, fg_ref, dh_ref, dmix_ref, gwo_ref, gfg_ref, loss_ref):
        @pl.when(pl.program_id(0) == 0)
        def _():
            gwo_ref[...] = jnp.zeros_like(gwo_ref)
            gfg_ref[...] = jnp.zeros_like(gfg_ref)
            loss_ref[...] = jnp.zeros_like(loss_ref)

        mix = jnp.concatenate([ya_ref[...], yc_ref[...]], axis=1)
        wo = wo_ref[...]
        fg = fg_ref[...]
        h = x_ref[...] + _nn(mix, wo)
        r = lax.rsqrt(jnp.mean(h * h, axis=-1, keepdims=True) + EPS)
        n = h * r
        err = n * fg - t_ref[...]
        loss_ref[...] += 0.5 * jnp.sum(jnp.mean(err * err, axis=-1, keepdims=True), axis=0, keepdims=True)
        dy = err * (1.0 / D_MODEL)
        gfg_ref[...] += jnp.sum(dy * n, axis=0, keepdims=True)
        dyg = dy * fg
        dh = r * (dyg - n * jnp.mean(dyg * n, axis=-1, keepdims=True))
        dh_ref[...] = dh
        dhb = dh.astype(BF16)
        dmix_ref[...] = _nt(dhb, wo).astype(ACT)
        gwo_ref[...] += _tn(mix, dhb)

    row = lambda i: (i, 0)
    fixed = lambda i: (0, 0)
    return pl.pallas_call(
        body,
        name="out_loss",
        grid=(S // TM,),
        in_specs=[
            pl.BlockSpec((TM, D_MODEL), row),
            pl.BlockSpec((TM, D_MODEL), row),
            pl.BlockSpec((TM, ATTN_W), row),
            pl.BlockSpec((TM, CONV_W), row),
            pl.BlockSpec((D_MODEL, D_MODEL), fixed),
            pl.BlockSpec((1, D_MODEL), fixed),
        ],
        out_specs=[
            pl.BlockSpec((TM, D_MODEL), row),
            pl.BlockSpec((TM, D_MODEL), row),
            pl.BlockSpec((D_MODEL, D_MODEL), fixed),
            pl.BlockSpec((1, D_MODEL), fixed),
            pl.BlockSpec((1, 1), fixed),
        ],
        out_shape=[
            jax.ShapeDtypeStruct((S, D_MODEL), F32),
            jax.ShapeDtypeStruct((S, D_MODEL), ACT),
            jax.ShapeDtypeStruct((D_MODEL, D_MODEL), F32),
            jax.ShapeDtypeStruct((1, D_MODEL), F32),
            jax.ShapeDtypeStruct((1, 1), F32),
        ],
        compiler_params=_params(("arbitrary",)),
    )(x, target, ya, yc, w_out, final_g)


def _attn_bwd(pa, dmix, tab, sinks):
    S = pa.shape[0]
    nt = S // TQ
    nb = TQ // BLK

    def body(sink_ref, q_ref, g_ref, kvc_ref, kvp_ref, tabc_ref, tabp_ref, dm_ref,
             d_ref, dsink_ref, kall, dkv, carry, q_sc, do_sc, p_sc, ds_sc, dsink_acc):
        step = pl.program_id(0)
        i = nt - 1 - step

        @pl.when(step == 0)
        def _():
            carry[...] = jnp.zeros_like(carry)
            dsink_acc[...] = jnp.zeros_like(dsink_acc)

        _fill_kv(kall, kvc_ref, kvp_ref, tabc_ref, tabp_ref)
        dkv[0:TQ, :] = jnp.zeros((TQ, 2 * KV_W), F32)
        dkv[TQ:TQ + BLK, :] = carry[...]
        lane = lax.broadcasted_iota(jnp.int32, (BLK, 128), 1)
        half = [lane < HEAD_DIM, lane >= HEAD_DIM]
        upper = _upper()
        sinks = _sink_rows(sink_ref)
        for j in range(nb):
            rq = slice(j * BLK, (j + 1) * BLK)
            rk = slice(j * BLK, (j + 2) * BLK)
            tab = tabc_ref[rq, :]
            pair = [slice(p * 128, (p + 1) * 128) for p in range(4)]
            qr = [_rope(q_ref[rq, c].astype(F32), tab) * 0.125 for c in pair]
            g = [g_ref[rq, c].astype(F32) for c in pair]
            da = [dm_ref[rq, c].astype(F32) for c in pair]
            do = [da[p] * _silu(g[p]) for p in range(4)]
            outs, dqs, dks, dvs = [], [], [], []
            for s in range(2):
                kk = kall[s, rk, :]
                vv = kall[2 + s, rk, :]
                _stack_heads(q_sc, s, half, qr)
                _stack_heads(do_sc, s, half, do)
                prev, cur = _scores(kk, q_sc[s], i == 0 if j == 0 else None)
                prob, psink = _softmax(jnp.where(upper, prev, cur), sinks[s])
                _split_store(p_sc, s, upper, prob)
                dprob = _merge(upper, _nt(vv, do_sc[s]))
                dsum = jnp.sum(dprob * prob, axis=0, keepdims=True)
                _split_store(ds_sc, s, upper, prob * (dprob - dsum))
                dsink_acc[s, 0:1, :] += psink * dsum
                outs.append(_tn(p_sc[s], vv))
                dqs.append(_tn(ds_sc[s], kk))
                dks.append(_nn(ds_sc[s], q_sc[s]))
                dvs.append(_nn(p_sc[s], do_sc[s]))
            for p in range(4):
                d_ref[rq, pair[p]] = _rope_t(_unstack_pair(half, dqs, p) * 0.125, tab).astype(BF16)
                d_ref[rq, 512 + p * 128:512 + (p + 1) * 128] = (
                    da[p] * _unstack_pair(half, outs, p) * _dsilu(g[p])).astype(BF16)
            dkv[rk, 0:128] += dks[0] + pltpu.roll(dks[1], 64, 1)
            dkv[rk, 128:256] += dvs[0] + pltpu.roll(dvs[1], 64, 1)
        d_ref[:, 1024:1152] = _rope_t(dkv[BLK:BLK + TQ, 0:128], tabc_ref[...]).astype(BF16)
        d_ref[:, 1152:1280] = dkv[BLK:BLK + TQ, 128:256].astype(BF16)
        carry[...] = dkv[0:BLK, :]

        @pl.when(step == nt - 1)
        def _():
            for s in range(2):
                for a, (p, e) in enumerate(HEADS[s]):
                    h = 2 * p + e
                    tot = jnp.sum(dsink_acc[s, 0:1, a * BLK:(a + 1) * BLK], axis=1, keepdims=True)
                    dsink_ref[h:h + 1, :] = jnp.broadcast_to(-tot, (1, 128))

    rev = lambda s: nt - 1 - s
    return pl.pallas_call(
        body,
        name="attn_bwd",
        grid=(nt,),
        in_specs=_attn_specs(rev) + [pl.BlockSpec((TQ, ATTN_W), lambda s: (nt - 1 - s, 0))],
        out_specs=[
            pl.BlockSpec((TQ, PA_W), lambda s: (nt - 1 - s, 0)),
            pl.BlockSpec((8, 128), lambda s: (0, 0)),
        ],
        out_shape=[
            jax.ShapeDtypeStruct((S, PA_W), BF16),
            jax.ShapeDtypeStruct((8, 128), F32),
        ],
        scratch_shapes=[
            pltpu.VMEM((4, BLK + TQ, 128), BF16),
            pltpu.VMEM((BLK + TQ, 2 * KV_W), F32),
            pltpu.VMEM((BLK, 2 * KV_W), F32),
            pltpu.VMEM((2, 4 * BLK, 128), BF16),
            pltpu.VMEM((2, 4 * BLK, 128), BF16),
            pltpu.VMEM((2, 2 * BLK, 4 * BLK), BF16),
            pltpu.VMEM((2, 2 * BLK, 4 * BLK), BF16),
            pltpu.VMEM((2, 8, 4 * BLK), F32),
        ],
        compiler_params=_params(("arbitrary",)),
    )(sinks, pa, pa, pa, pa, tab, tab, dmix)


def _conv_bwd(pc, dmix, conv_w):
    S = pc.shape[0]
    nt = S // TC

    def body(pc_ref, prev_ref, next_ref, dm_ref, dmn_ref, w_ref, d_ref, gw_ref):
        i = pl.program_id(0)

        @pl.when(i == 0)
        def _():
            gw_ref[...] = jnp.zeros_like(gw_ref)

        def parts(ref):
            return (ref[:, 0:512].astype(F32), ref[:, 512:1024].astype(F32),
                    ref[:, 1024:1536].astype(F32), ref[:, 1536:2048].astype(F32))

        w0, w1, w2 = w_ref[0:1, :], w_ref[1:2, :], w_ref[2:3, :]
        b, c, hh, gc = parts(pc_ref)
        u = c * hh
        um1, um2 = _shift_down(u, prev_ref, i > 0)
        cv = w0 * um2 + w1 * um1 + w2 * u
        sg = _silu(gc)
        dy = dm_ref[...].astype(F32)
        dcv = dy * b * sg

        def next_dcv(r):
            nd = (dmn_ref[r:r + 1, :].astype(F32) * next_ref[r:r + 1, 0:512].astype(F32)
                  * _silu(next_ref[r:r + 1, 1536:2048].astype(F32)))
            return jnp.where(i < nt - 1, nd, 0.0)

        row = lax.broadcasted_iota(jnp.int32, (TC, CONV_W), 0)
        dp1 = jnp.where(row == TC - 1, next_dcv(0), pltpu.roll(dcv, TC - 1, 0))
        dp2 = jnp.where(row == TC - 1, next_dcv(1),
                        jnp.where(row == TC - 2, next_dcv(0), pltpu.roll(dcv, TC - 2, 0)))
        du = w2 * dcv + w1 * dp1 + w0 * dp2
        d_ref[:, 0:512] = (dy * cv * sg).astype(BF16)
        d_ref[:, 512:1024] = (du * hh).astype(BF16)
        d_ref[:, 1024:1536] = (du * c).astype(BF16)
        d_ref[:, 1536:2048] = (dy * b * cv * _dsilu(gc)).astype(BF16)
        gw_ref[0:1, :] += jnp.sum(dcv * um2, axis=0, keepdims=True)
        gw_ref[1:2, :] += jnp.sum(dcv * um1, axis=0, keepdims=True)
        gw_ref[2:3, :] += jnp.sum(dcv * u, axis=0, keepdims=True)

    t8 = TC // HALO
    return pl.pallas_call(
        body,
        name="conv_bwd",
        grid=(nt,),
        in_specs=[
            pl.BlockSpec((TC, PC_W), lambda i: (i, 0)),
            pl.BlockSpec((HALO, PC_W), lambda i: (jnp.maximum(i * t8 - 1, 0), 0)),
            pl.BlockSpec((HALO, PC_W), lambda i: (jnp.minimum((i + 1) * t8, nt * t8 - 1), 0)),
            pl.BlockSpec((TC, CONV_W), lambda i: (i, 1)),
            pl.BlockSpec((HALO, CONV_W), lambda i: (jnp.minimum((i + 1) * t8, nt * t8 - 1), 1)),
            pl.BlockSpec((CONV_K, CONV_W), lambda i: (0, 0)),
        ],
        out_specs=[
            pl.BlockSpec((TC, PC_W), lambda i: (i, 0)),
            pl.BlockSpec((CONV_K, CONV_W), lambda i: (0, 0)),
        ],
        out_shape=[
            jax.ShapeDtypeStruct((S, PC_W), BF16),
            jax.ShapeDtypeStruct((CONV_K, CONV_W), F32),
        ],
        compiler_params=_params(("arbitrary",)),
    )(pc, pc, pc, dmix, dmix, conv_w)


def _grad_x(da, dc, wt, x, dh, norm_g, grads):
    S = x.shape[0]
    n_steps = S // TM
    rs = _ReduceScatter(grads)
    n_rs_out = len(rs.out_shape())

    def body(da_ref, dc_ref, wt_ref, x_ref, dh_ref, g_ref, *rest):
        grad_refs, rest = rest[:rs.n], rest[rs.n:]
        gx_ref, gng_ref = rest[:2]
        rs_out, rs_scratch = rest[2:2 + n_rs_out], rest[2 + n_rs_out:]
        step = pl.program_id(0)
        finish = rs.emit(step, n_steps, grad_refs, rs_out, rs_scratch)

        @pl.when(step == 0)
        def _():
            gng_ref[...] = jnp.zeros_like(gng_ref)

        dxn = (_nn(da_ref[:, 0:512], wt_ref[0:512, :]) + _nn(da_ref[:, 512:1024], wt_ref[768:1280, :])
               + _nn(da_ref[:, 1024:1280], wt_ref[512:768, :]) + _nn(dc_ref[...], wt_ref[1280:3328, :]))
        xv = x_ref[...]
        r = lax.rsqrt(jnp.mean(xv * xv, axis=-1, keepdims=True) + EPS)
        n = xv * r
        gng_ref[...] += jnp.sum(dxn * n, axis=0, keepdims=True)
        dxg = dxn * g_ref[...]
        gx_ref[...] = dh_ref[...] + r * (dxg - n * jnp.mean(dxg * n, axis=-1, keepdims=True))
        finish()

    row = lambda i: (i, 0)
    fixed = lambda i: (0, 0)
    any_spec = pl.BlockSpec(memory_space=pl.ANY)
    outs = pl.pallas_call(
        body,
        name="grad_x_reduce_scatter",
        grid=(n_steps,),
        in_specs=[
            pl.BlockSpec((TM, PA_W), row),
            pl.BlockSpec((TM, PC_W), row),
            pl.BlockSpec((IN_W, D_MODEL), fixed),
            pl.BlockSpec((TM, D_MODEL), row),
            pl.BlockSpec((TM, D_MODEL), row),
            pl.BlockSpec((1, D_MODEL), fixed),
        ] + [any_spec] * rs.n,
        out_specs=[pl.BlockSpec((TM, D_MODEL), row), pl.BlockSpec((1, D_MODEL), fixed)] + [any_spec] * n_rs_out,
        out_shape=[jax.ShapeDtypeStruct((S, D_MODEL), F32), jax.ShapeDtypeStruct((1, D_MODEL), F32)] + rs.out_shape(),
        scratch_shapes=rs.scratch_shapes(),
        compiler_params=_params(("arbitrary",)),
    )(da, dc, wt, x, dh, norm_g, *grads)
    return outs[0], outs[1], outs[2:2 + rs.n], outs[2 + rs.n:2 + 2 * rs.n]


def _grad_w_in(da, dc, xn):
    S = xn.shape[0]

    def body(da_ref, dc_ref, xn_ref, gw_ref):
        @pl.when(pl.program_id(0) == 0)
        def _():
            gw_ref[...] = jnp.zeros_like(gw_ref)

        xn = xn_ref[...]
        gw_ref[0:512, :] += _tn(da_ref[:, 0:512], xn)
        gw_ref[768:1280, :] += _tn(da_ref[:, 512:1024], xn)
        gw_ref[512:768, :] += _tn(da_ref[:, 1024:1280], xn)
        gw_ref[1280:3328, :] += _tn(dc_ref[...], xn)

    row = lambda i: (i, 0)
    return pl.pallas_call(
        body,
        name="grad_w_in",
        grid=(S // TM,),
        in_specs=[pl.BlockSpec((TM, PA_W), row), pl.BlockSpec((TM, PC_W), row), pl.BlockSpec((TM, D_MODEL), row)],
        out_specs=pl.BlockSpec((IN_W, D_MODEL), lambda i: (0, 0)),
        out_shape=jax.ShapeDtypeStruct((IN_W, D_MODEL), F32),
        compiler_params=_params(("arbitrary",)),
    )(da, dc, xn)


def _sum_chips(own, others, name):
    def body(own_ref, p_ref, o_ref):
        acc = own_ref[...]
        for k in range(N_CHIP - 1):
            acc = acc + p_ref[k].astype(F32)
        o_ref[...] = acc

    return pl.pallas_call(
        body,
        name=name,
        out_shape=jax.ShapeDtypeStruct(own.shape, F32),
        compiler_params=_params(),
    )(own, others)


def _sum_parts(parts, name):
    n = parts.shape[0]

    def body(p_ref, o_ref):
        acc = p_ref[0]
        for k in range(1, n):
            acc = acc + p_ref[k]
        o_ref[...] = acc

    return pl.pallas_call(
        body,
        name=name,
        out_shape=jax.ShapeDtypeStruct(parts.shape[1:], F32),
        compiler_params=_params(),
    )(parts)


def _adamw(w, g, m, v, name):
    c1 = 1.0 - ADAM_B1 ** ADAM_STEP
    c2 = 1.0 - ADAM_B2 ** ADAM_STEP

    def body(w_ref, g_ref, m_ref, v_ref, d_ref, nm_ref, nv_ref):
        g = g_ref[...]
        nm = ADAM_B1 * m_ref[...] + (1.0 - ADAM_B1) * g
        nv = ADAM_B2 * v_ref[...] + (1.0 - ADAM_B2) * (g * g)
        nm_ref[...] = nm
        nv_ref[...] = nv
        d_ref[...] = -ADAM_LR * ((nm / c1) / (jnp.sqrt(nv / c2) + ADAM_EPS) + ADAM_WD * w_ref[...])

    shape = jax.ShapeDtypeStruct(w.shape, F32)
    return pl.pallas_call(
        body,
        name=name,
        out_shape=[shape, shape, shape],
        compiler_params=_params(),
    )(w, g, m, v)


def _rope_table(S):
    half = ROT_DIM // 2
    pos = jnp.arange(S, dtype=jnp.int32).astype(F32)
    inv_freq = ROPE_THETA ** (-jnp.arange(0, ROT_DIM, 2, dtype=F32) / ROT_DIM)
    ang = inv_freq[:, None] * pos[None, :]
    cs = jnp.concatenate([jnp.cos(ang), jnp.sin(ang)], axis=0)
    sel = np.zeros((2 * half, 384), np.float32)
    ones = np.zeros((1, 384), np.float32)
    for l in range(128):
        r = l % HEAD_DIM
        if r < half:
            sel[r, l] = 1.0
            sel[half + r, 128 + l] = -1.0
        elif r < ROT_DIM:
            sel[r - half, l] = 1.0
            sel[half + r - half, 256 + l] = 1.0
        else:
            ones[0, l] = 1.0
    return lax.dot_general(cs, jnp.asarray(sel), TN_DIMS, precision=lax.Precision.HIGHEST) + jnp.asarray(ones)


def kernel(x, norm_g, w_in, sinks, conv_w, w_out, final_g, loss_target, m_norm_g, m_w_in, m_sinks, m_conv_w, m_w_out, m_final_g, v_norm_g, v_w_in, v_sinks, v_conv_w, v_w_out, v_final_g):
    S = x.shape[1]
    me = 4 * lax.axis_index("x") + 2 * lax.axis_index("y") + lax.axis_index("c")
    x2 = x.reshape(S, D_MODEL)
    t2 = loss_target.reshape(S, D_MODEL)
    ng = norm_g.reshape(1, D_MODEL)
    fg = final_g.reshape(1, D_MODEL)

    cw_pad = jnp.zeros((8, 128), F32).at[0:CONV_K, 0:64].set(conv_w)
    (wt,) = _all_gather([w_in.T.astype(BF16)], "all_gather_w_in")

    tab = _rope_table(S)
    xn, pa, pc, (wo, cw_all) = _fwd_proj(x2, ng, wt, [w_out.astype(BF16), cw_pad])
    cw = cw_all.reshape(N_DEV, 8, 128)[:, 0:CONV_K, 0:64].transpose(1, 0, 2).reshape(CONV_K, CONV_W)
    ya = _attn_fwd(pa, tab, sinks)
    yc = _conv_fwd(pc, cw)
    dh, dmix, g_wo, g_fg, loss_part = _out_loss(x2, t2, ya, yc, wo, fg)
    da, g_sinks = _attn_bwd(pa, dmix, tab, sinks)
    dc, g_cw = _conv_bwd(pc, dmix, cw)
    g_wt = _grad_w_in(da, dc, xn)
    grad_x, g_ng, own, others = _grad_x(
        da, dc, wt, x2, dh, ng,
        [g_wt.reshape(N_DEV, SHARD_IN, D_MODEL), g_wo.reshape(N_DEV, SHARD_OUT, D_MODEL)])
    grad_w_in = _sum_chips(own[0], others[0], "sum_grad_w_in").T
    grad_w_out = _sum_chips(own[1], others[1], "sum_grad_w_out")
    small = jnp.concatenate([
        g_ng.reshape(8, 128), g_fg.reshape(8, 128), g_sinks,
        g_cw.reshape(12, 128), jnp.broadcast_to(loss_part, (4, 128))], axis=0)
    (small_all,) = _all_gather([small], "all_gather_small_grads")
    small_sum = _sum_parts(small_all.reshape(N_DEV, SMALL_ROWS, 128), "sum_small_grads")
    grad_norm_g = small_sum[0:8].reshape(D_MODEL)
    grad_final_g = small_sum[8:16].reshape(D_MODEL)
    grad_sinks = small_sum[16:24, 0]
    grad_conv_w = lax.dynamic_slice(small_sum[24:36].reshape(CONV_K, CONV_W), (0, me * 64), (CONV_K, 64))
    loss = small_sum[36, 0]

    d_w_in, nm_w_in, nv_w_in = _adamw(w_in, grad_w_in, m_w_in, v_w_in, "adamw_w_in")
    d_w_out, nm_w_out, nv_w_out = _adamw(w_out, grad_w_out, m_w_out, v_w_out, "adamw_w_out")
    def pack(a, b, c_, d):
        return jnp.concatenate([
            a.reshape(8, 128), b.reshape(8, 128),
            jnp.zeros((8, 128), F32).at[0, 0:8].set(c_).at[1:1 + CONV_K, 0:64].set(d)], axis=1)

    d_s, nm_s, nv_s = _adamw(
        pack(norm_g, final_g, sinks, conv_w), pack(grad_norm_g, grad_final_g, grad_sinks, grad_conv_w),
        pack(m_norm_g, m_final_g, m_sinks, m_conv_w),
        pack(v_norm_g, v_final_g, v_sinks, v_conv_w),
        "adamw_small")

    def unpack(p):
        return (p[:, 0:128].reshape(D_MODEL), p[:, 128:256].reshape(D_MODEL), p[0, 256:264], p[1:1 + CONV_K, 256:320])

    d_ng, d_fg, d_sk, d_cw = unpack(d_s)
    nm_ng, nm_fg, nm_sk, nm_cw = unpack(nm_s)
    nv_ng, nv_fg, nv_sk, nv_cw = unpack(nv_s)

    return (loss, grad_x.reshape(1, S, D_MODEL), grad_norm_g, grad_w_in, grad_sinks, grad_conv_w, grad_w_out, grad_final_g,
            d_ng, d_w_in, d_sk, d_cw, d_w_out, d_fg,
            nm_ng, nm_w_in, nm_sk, nm_cw, nm_w_out, nm_fg,
            nv_ng, nv_w_in, nv_sk, nv_cw, nv_w_out, nv_fg)
```

```python
import numpy as np
import jax
import jax.numpy as jnp
from jax import lax
from jax.experimental import pallas as pl
from jax.experimental.pallas import tpu as pltpu

F32 = jnp.float32
BF16 = jnp.bfloat16
MESH = pl.DeviceIdType.MESH

D_MODEL = 1024
HEAD_DIM = 64
N_Q_HEADS = 8
GROUP = 4
ATTN_W = 512
KV_W = 128
BLK = 128
CONV_W = 512
CONV_K = 3
IN_W = 3328
PA_W = 1280
PC_W = 2048
EPS = 1e-5
ROPE_THETA = 500000.0
ROT_DIM = 16
N_DEV = 8
N_CHIP = 4
SHARD_IN = IN_W // N_DEV
SHARD_OUT = D_MODEL // N_DEV
SMALL_ROWS = 40

ADAM_LR = 0.001
ADAM_B1 = 0.9
ADAM_B2 = 0.999
ADAM_EPS = 1e-08
ADAM_WD = 0.01
ADAM_STEP = 10

ACT = jnp.bfloat16

TM = 512
TQ = 512
TC = 512
HALO = 16
VMEM_LIMIT = 56 * 1024 * 1024

NT_DIMS = (((1,), (1,)), ((), ()))
TN_DIMS = (((0,), (0,)), ((), ()))


def _params(sem=None):
    kw = dict(vmem_limit_bytes=VMEM_LIMIT)
    if sem is not None:
        kw["dimension_semantics"] = sem
    return pltpu.CompilerParams(**kw)


def _nt(a, b):
    return lax.dot_general(a, b, NT_DIMS, preferred_element_type=F32)


def _tn(a, b):
    return lax.dot_general(a, b, TN_DIMS, preferred_element_type=F32)


def _nn(a, b):
    return jnp.dot(a, b, preferred_element_type=F32)


def _silu(g):
    return g * jax.nn.sigmoid(g)


def _dsilu(g):
    s = jax.nn.sigmoid(g)
    return s * (1.0 + g * (1.0 - s))


def _all_gather(arrs, name):
    n_arr = len(arrs)

    def body(*refs):
        x_refs = refs[:n_arr]
        out_refs = refs[n_arr:2 * n_arr]
        send_sems, recv_sems, local_sems = refs[2 * n_arr:]
        x, y, c = lax.axis_index("x"), lax.axis_index("y"), lax.axis_index("c")
        me, sibling = (x, y, c), (x, y, 1 - c)
        chips = [(1 - x, y), (x, 1 - y), (1 - x, 1 - y)]

        def rows(a, px, py, pc):
            m = x_refs[a].shape[0]
            return out_refs[a].at[pl.ds((4 * px + 2 * py + pc) * m, m), :]

        def copy(a, k, block, to, src=None):
            return pltpu.make_async_remote_copy(
                src_ref=rows(a, *block) if src is None else src,
                dst_ref=rows(a, *block),
                send_sem=send_sems.at[a * 7 + k],
                recv_sem=recv_sems.at[a * 7 + k],
                device_id=to,
                device_id_type=MESH,
            )

        mine = [pltpu.make_async_copy(x_refs[a], rows(a, *me), local_sems.at[a]) for a in range(n_arr)]
        for cp in mine:
            cp.start()
        first = []
        for a in range(n_arr):
            first.append(copy(a, 0, me, sibling, src=x_refs[a]))
            first += [copy(a, 1 + j, me, (*chip, c), src=x_refs[a]) for j, chip in enumerate(chips)]
        for cp in first:
            cp.start()
        passed = []
        for j, chip in enumerate(chips):
            for a in range(n_arr):
                copy(a, 1 + j, (*chip, c), me).wait_recv()
                fwd = copy(a, 4 + j, (*chip, c), sibling)
                fwd.start()
                passed.append(fwd)
        for a in range(n_arr):
            copy(a, 0, sibling, me).wait_recv()
            for j, chip in enumerate(chips):
                copy(a, 4 + j, (*chip, 1 - c), me).wait_recv()
        for cp in first + passed:
            cp.wait_send()
        for cp in mine:
            cp.wait()

    vmem = pl.BlockSpec(memory_space=pltpu.VMEM)
    return pl.pallas_call(
        body,
        name=name,
        out_shape=[jax.ShapeDtypeStruct((N_DEV * a.shape[0], a.shape[1]), a.dtype) for a in arrs],
        in_specs=[vmem] * n_arr,
        out_specs=[vmem] * n_arr,
        scratch_shapes=[
            pltpu.SemaphoreType.DMA((7 * n_arr,)),
            pltpu.SemaphoreType.DMA((7 * n_arr,)),
            pltpu.SemaphoreType.DMA((n_arr,)),
        ],
        compiler_params=_params(),
    )(*arrs)


class _AllGatherInSteps:
    forward_step = 3

    def __init__(self, arrs):
        self.blocks = [(a.shape, a.dtype) for a in arrs]
        self.n = len(arrs)

    def out_shape(self):
        return [jax.ShapeDtypeStruct((N_DEV * s[0], s[1]), d) for s, d in self.blocks]

    def scratch_shapes(self):
        return [pltpu.SemaphoreType.DMA((7 * self.n,)), pltpu.SemaphoreType.DMA((7 * self.n,)),
                pltpu.SemaphoreType.DMA((self.n,))]

    def emit(self, step, n_steps, x_refs, out_refs, scratch):
        assert n_steps > self.forward_step + 1
        send_sems, recv_sems, local_sems = scratch
        x, y, c = lax.axis_index("x"), lax.axis_index("y"), lax.axis_index("c")
        me, sibling = (x, y, c), (x, y, 1 - c)
        chips = [(1 - x, y), (x, 1 - y), (1 - x, 1 - y)]

        def rows(a, px, py, pc):
            m = self.blocks[a][0][0]
            return out_refs[a].at[pl.ds((4 * px + 2 * py + pc) * m, m), :]

        def copy(a, k, block, to, src=None):
            return pltpu.make_async_remote_copy(
                src_ref=rows(a, *block) if src is None else src, dst_ref=rows(a, *block),
                send_sem=send_sems.at[a * 7 + k], recv_sem=recv_sems.at[a * 7 + k],
                device_id=to, device_id_type=MESH)

        def mine(a):
            return pltpu.make_async_copy(x_refs[a], rows(a, *me), local_sems.at[a])

        def first(a):
            return ([copy(a, 0, me, sibling, src=x_refs[a])]
                    + [copy(a, 1 + j, me, (*chip, c), src=x_refs[a]) for j, chip in enumerate(chips)])

        def passed(a):
            return [copy(a, 4 + j, (*chip, c), sibling) for j, chip in enumerate(chips)]

        @pl.when(step == 0)
        def _():
            for a in range(self.n):
                mine(a).start()
                for cp in first(a):
                    cp.start()

        @pl.when(step == self.forward_step)
        def _():
            for j, chip in enumerate(chips):
                for a in range(self.n):
                    copy(a, 1 + j, (*chip, c), me).wait_recv()
                    copy(a, 4 + j, (*chip, c), sibling).start()

        def finish():
            @pl.when(step == n_steps - 1)
            def _():
                for a in range(self.n):
                    copy(a, 0, sibling, me).wait_recv()
                    for j, chip in enumerate(chips):
                        copy(a, 4 + j, (*chip, 1 - c), me).wait_recv()
                    for cp in first(a) + passed(a):
                        cp.wait_send()
                    mine(a).wait()

        return finish


class _ReduceScatter:
    def __init__(self, grads):
        self.shapes = [g.shape[1:] for g in grads]
        self.n = len(grads)
        self.items = tuple((a, r) for r in (1, 2, 3, 0) for a in range(self.n))
        self.steps = len(self.items) + 2

    def out_shape(self):
        own = [jax.ShapeDtypeStruct(s, F32) for s in self.shapes]
        ici = [jax.ShapeDtypeStruct((N_CHIP - 1,) + s, BF16) for s in self.shapes]
        land = [jax.ShapeDtypeStruct((N_CHIP,) + s, F32) for s in self.shapes]
        return own + ici + land

    def scratch_shapes(self):
        n_items = len(self.items)
        return ([pltpu.VMEM((2,) + s, F32) for s in self.shapes]
                + [pltpu.VMEM((N_CHIP - 1,) + s, BF16) for s in self.shapes]
                + [pltpu.VMEM(s, F32) for s in self.shapes]
                + [pltpu.SemaphoreType.DMA((self.n * N_CHIP,))] * 2
                + [pltpu.SemaphoreType.DMA((2 * n_items,))]
                + [pltpu.SemaphoreType.DMA((self.n * (N_CHIP - 1),))] * 2
                + [pltpu.SemaphoreType.DMA((self.n,))])

    def emit(self, step, n_steps, g_refs, out_refs, scratch):
        assert n_steps > self.steps
        n = self.n
        own_refs, ici_refs, land_refs = out_refs[:n], out_refs[n:2 * n], out_refs[2 * n:]
        stage, pair_bf, pair_own = scratch[:n], scratch[n:2 * n], scratch[2 * n:3 * n]
        sib_send, sib_recv, load_sems, ici_send, ici_recv, own_sems = scratch[3 * n:]
        x, y, c = lax.axis_index("x"), lax.axis_index("y"), lax.axis_index("c")

        def chip_of(r):
            return (x ^ (r >> 1), y ^ (r & 1))

        def block_of(r, core):
            cx, cy = chip_of(r)
            return 4 * cx + 2 * cy + core

        def to_sibling(a, r):
            return pltpu.make_async_remote_copy(
                src_ref=g_refs[a].at[block_of(r, 1 - c)], dst_ref=land_refs[a].at[r],
                send_sem=sib_send.at[a * N_CHIP + r], recv_sem=sib_recv.at[a * N_CHIP + r],
                device_id=(x, y, 1 - c), device_id_type=MESH)

        def loads(k):
            a, r = self.items[k]
            return (pltpu.make_async_copy(g_refs[a].at[block_of(r, c)], stage[a].at[0], load_sems.at[2 * k]),
                    pltpu.make_async_copy(land_refs[a].at[r], stage[a].at[1], load_sems.at[2 * k + 1]))

        def to_owner(k):
            a, r = self.items[k]
            if r == 0:
                return pltpu.make_async_copy(pair_own[a], own_refs[a], own_sems.at[a])
            return pltpu.make_async_remote_copy(
                src_ref=pair_bf[a].at[r - 1], dst_ref=ici_refs[a].at[r - 1],
                send_sem=ici_send.at[a * (N_CHIP - 1) + r - 1], recv_sem=ici_recv.at[a * (N_CHIP - 1) + r - 1],
                device_id=(*chip_of(r), c), device_id_type=MESH)

        @pl.when(step == 0)
        def _():
            for a, r in self.items:
                to_sibling(a, r).start()

        for k, (a, r) in enumerate(self.items):
            @pl.when(step == 1 + k)
            def _(k=k, a=a, r=r):
                to_sibling(a, r).wait_recv()
                for cp in loads(k):
                    cp.start()

            @pl.when(step == 2 + k)
            def _(k=k, a=a, r=r):
                for cp in loads(k):
                    cp.wait()
                total = stage[a][0] + stage[a][1]
                if r == 0:
                    pair_own[a][...] = total
                else:
                    pair_bf[a][r - 1] = total.astype(BF16)
                to_owner(k).start()

        def finish():
            @pl.when(step == n_steps - 1)
            def _():
                for k, (a, r) in enumerate(self.items):
                    if r == 0:
                        to_owner(k).wait()
                    else:
                        to_owner(k).wait_send()
                        to_owner(k).wait_recv()
                for a, r in self.items:
                    to_sibling(a, r).wait_send()

        return finish


def _fwd_proj(x, norm_g, wt, later):
    S = x.shape[0]

    n_steps = S // TM
    ag = _AllGatherInSteps(later)

    def body(x_ref, g_ref, wt_ref, *rest):
        later_refs, rest = rest[:ag.n], rest[ag.n:]
        xn_ref, pa_ref, pc_ref = rest[:3]
        gathered, ag_scratch = rest[3:3 + ag.n], rest[3 + ag.n:]
        step = pl.program_id(0)
        finish = ag.emit(step, n_steps, later_refs, gathered, ag_scratch)
        xv = x_ref[...]
        r = lax.rsqrt(jnp.mean(xv * xv, axis=-1, keepdims=True) + EPS)
        xn = (xv * r * g_ref[...]).astype(BF16)
        xn_ref[...] = xn
        pa_ref[:, 0:512] = _nt(xn, wt_ref[0:512, :]).astype(ACT)
        pa_ref[:, 512:1024] = _nt(xn, wt_ref[768:1280, :]).astype(ACT)
        pa_ref[:, 1024:1280] = _nt(xn, wt_ref[512:768, :]).astype(ACT)
        pc_ref[...] = _nt(xn, wt_ref[1280:3328, :]).astype(ACT)
        finish()

    any_spec = pl.BlockSpec(memory_space=pl.ANY)
    outs = pl.pallas_call(
        body,
        name="fwd_proj_all_gather",
        grid=(n_steps,),
        in_specs=[
            pl.BlockSpec((TM, D_MODEL), lambda i: (i, 0)),
            pl.BlockSpec((1, D_MODEL), lambda i: (0, 0)),
            pl.BlockSpec((IN_W, D_MODEL), lambda i: (0, 0)),
        ] + [any_spec] * ag.n,
        out_specs=[
            pl.BlockSpec((TM, D_MODEL), lambda i: (i, 0)),
            pl.BlockSpec((TM, PA_W), lambda i: (i, 0)),
            pl.BlockSpec((TM, PC_W), lambda i: (i, 0)),
        ] + [any_spec] * ag.n,
        out_shape=[
            jax.ShapeDtypeStruct((S, D_MODEL), BF16),
            jax.ShapeDtypeStruct((S, PA_W), ACT),
            jax.ShapeDtypeStruct((S, PC_W), ACT),
        ] + ag.out_shape(),
        scratch_shapes=ag.scratch_shapes(),
        compiler_params=_params(("arbitrary",)),
    )(x, norm_g, wt, *later)
    return outs[0], outs[1], outs[2], outs[3:]


def _rope(t, tab):
    return (t * tab[:, 0:128] + pltpu.roll(t, 120, 1) * tab[:, 128:256]
            + pltpu.roll(t, 8, 1) * tab[:, 256:384])


def _rope_t(d, tab):
    return (d * tab[:, 0:128] + pltpu.roll(d * tab[:, 128:256], 8, 1)
            + pltpu.roll(d * tab[:, 256:384], 120, 1))


def _fill_kv(kall, kvc_ref, kvp_ref, tabc_ref, tabp_ref):
    for lo, kv_ref, tab_ref, n in ((0, kvp_ref, tabp_ref, BLK), (BLK, kvc_ref, tabc_ref, TQ)):
        k = _rope(kv_ref[:, 0:128].astype(F32), tab_ref[...])
        v = kv_ref[:, 128:256].astype(F32)
        kall[0, lo:lo + n, :] = k.astype(BF16)
        kall[1, lo:lo + n, :] = pltpu.roll(k, 64, 1).astype(BF16)
        kall[2, lo:lo + n, :] = v.astype(BF16)
        kall[3, lo:lo + n, :] = pltpu.roll(v, 64, 1).astype(BF16)


HEADS = (((0, 0), (1, 0), (2, 1), (3, 1)), ((0, 1), (1, 1), (2, 0), (3, 0)))


def _upper():
    kj = lax.broadcasted_iota(jnp.int32, (BLK, 4 * BLK), 0)
    qi = lax.broadcasted_iota(jnp.int32, (BLK, 4 * BLK), 1) & (BLK - 1)
    return kj > qi


def _merge(upper, both):
    return jnp.where(upper, both[0:BLK, :], both[BLK:2 * BLK, :])


def _split_store(ref, s, upper, val):
    ref[s, 0:BLK, :] = jnp.where(upper, val, 0.0).astype(BF16)
    ref[s, BLK:2 * BLK, :] = jnp.where(upper, 0.0, val).astype(BF16)


def _sink_rows(sink_ref):
    return [jnp.concatenate([jnp.full((1, BLK), sink_ref[2 * p + e], F32) for p, e in HEADS[s]], axis=1)
            for s in range(2)]


def _stack_heads(ref, s, half, pairs):
    for a, (p, e) in enumerate(HEADS[s]):
        ref[s, a * BLK:(a + 1) * BLK, :] = jnp.where(half[e], pairs[p], 0.0).astype(BF16)


def _unstack_pair(half, outs, p):
    lo = 0 if p < 2 else 1
    rows = slice(p * BLK, (p + 1) * BLK)
    return jnp.where(half[0], outs[lo][rows, :], outs[1 - lo][rows, :])


def _softmax(sm, sinks):
    m = jnp.maximum(jnp.max(sm, axis=0, keepdims=True), sinks)
    p = jnp.exp(sm - m)
    es = jnp.exp(sinks - m)
    inv = 1.0 / (jnp.sum(p, axis=0, keepdims=True) + es)
    return p * inv, es * inv


def _scores(kk, q_stack, first):
    st = _nt(kk, q_stack)
    prev = st[0:BLK, :]
    if first is not None:
        prev = prev + jnp.where(first, -jnp.inf, 0.0)
    return prev, st[BLK:2 * BLK, :]


def _attn_specs(tile):
    nb = TQ // BLK
    prev = lambda i: jnp.maximum(tile(i) * nb - 1, 0)
    return [
        pl.BlockSpec(memory_space=pltpu.SMEM),
        pl.BlockSpec((TQ, ATTN_W), lambda i: (tile(i), 0)),
        pl.BlockSpec((TQ, ATTN_W), lambda i: (tile(i), 1)),
        pl.BlockSpec((TQ, 2 * KV_W), lambda i: (tile(i), 4)),
        pl.BlockSpec((BLK, 2 * KV_W), lambda i: (prev(i), 4)),
        pl.BlockSpec((TQ, 384), lambda i: (tile(i), 0)),
        pl.BlockSpec((BLK, 384), lambda i: (prev(i), 0)),
    ]


def _attn_fwd(pa, tab, sinks):
    S = pa.shape[0]
    nb = TQ // BLK

    def body(sink_ref, q_ref, g_ref, kvc_ref, kvp_ref, tabc_ref, tabp_ref, o_ref, kall, q_sc, p_sc):
        i = pl.program_id(0)
        _fill_kv(kall, kvc_ref, kvp_ref, tabc_ref, tabp_ref)
        lane = lax.broadcasted_iota(jnp.int32, (BLK, 128), 1)
        half = [lane < HEAD_DIM, lane >= HEAD_DIM]
        upper = _upper()
        sinks = _sink_rows(sink_ref)
        for j in range(nb):
            rq = slice(j * BLK, (j + 1) * BLK)
            rk = slice(j * BLK, (j + 2) * BLK)
            tab = tabc_ref[rq, :]
            qr = [_rope(q_ref[rq, p * 128:(p + 1) * 128].astype(F32), tab) * 0.125 for p in range(4)]
            outs = []
            for s in range(2):
                _stack_heads(q_sc, s, half, qr)
                prev, cur = _scores(kall[s, rk, :], q_sc[s], i == 0 if j == 0 else None)
                prob, _ = _softmax(jnp.where(upper, prev, cur), sinks[s])
                _split_store(p_sc, s, upper, prob)
                outs.append(_tn(p_sc[s], kall[2 + s, rk, :]))
            for p in range(4):
                cols = slice(p * 128, (p + 1) * 128)
                o_ref[rq, cols] = (_unstack_pair(half, outs, p) * _silu(g_ref[rq, cols].astype(F32))).astype(BF16)

    return pl.pallas_call(
        body,
        name="attn_fwd",
        grid=(S // TQ,),
        in_specs=_attn_specs(lambda i: i),
        out_specs=pl.BlockSpec((TQ, ATTN_W), lambda i: (i, 0)),
        out_shape=jax.ShapeDtypeStruct((S, ATTN_W), BF16),
        scratch_shapes=[
            pltpu.VMEM((4, BLK + TQ, 128), BF16),
            pltpu.VMEM((2, 4 * BLK, 128), BF16),
            pltpu.VMEM((2, 2 * BLK, 4 * BLK), BF16),
        ],
        compiler_params=_params(("arbitrary",)),
    )(sinks, pa, pa, pa, pa, tab, tab)


def _shift_down(u, halo_ref, has_prev):
    def halo_u(r):
        hu = halo_ref[r:r + 1, 512:1024].astype(F32) * halo_ref[r:r + 1, 1024:1536].astype(F32)
        return jnp.where(has_prev, hu, 0.0)

    row = lax.broadcasted_iota(jnp.int32, u.shape, 0)
    um1 = jnp.where(row == 0, halo_u(HALO - 1), pltpu.roll(u, 1, 0))
    um2 = jnp.where(row == 0, halo_u(HALO - 2), jnp.where(row == 1, halo_u(HALO - 1), pltpu.roll(u, 2, 0)))
    return um1, um2


def _conv_tile(pc_ref, halo_ref, w_ref, has_prev):
    b = pc_ref[:, 0:512].astype(F32)
    c = pc_ref[:, 512:1024].astype(F32)
    hh = pc_ref[:, 1024:1536].astype(F32)
    gc = pc_ref[:, 1536:2048].astype(F32)
    u = c * hh
    um1, um2 = _shift_down(u, halo_ref, has_prev)
    cv = w_ref[0:1, :] * um2 + w_ref[1:2, :] * um1 + w_ref[2:3, :] * u
    return b, c, hh, gc, u, um1, um2, cv


def _prev_rows(width, col=0):
    return pl.BlockSpec((HALO, width), lambda i: (jnp.maximum(i * (TM // HALO) - 1, 0), col))


def _out_loss(x, target, ya, pc, conv_w, w_out, final_g):
    S = x.shape[0]

    def body(x_ref, t_ref, ya_ref, pc_ref, halo_ref, cw_ref, wo_ref, fg_ref,
             dh_ref, dmix_ref, gwo_ref, gfg_ref, loss_ref):
        @pl.when(pl.program_id(0) == 0)
        def _():
            gwo_ref[...] = jnp.zeros_like(gwo_ref)
            gfg_ref[...] = jnp.zeros_like(gfg_ref)
            loss_ref[...] = jnp.zeros_like(loss_ref)

        b, _, _, gc, _, _, _, cv = _conv_tile(pc_ref, halo_ref, cw_ref, pl.program_id(0) > 0)
        yc = (b * cv * _silu(gc)).astype(BF16)
        mix = jnp.concatenate([ya_ref[...], yc], axis=1)
        wo = wo_ref[...]
        fg = fg_ref[...]
        h = x_ref[...] + _nn(mix, wo)
        r = lax.rsqrt(jnp.mean(h * h, axis=-1, keepdims=True) + EPS)
        n = h * r
        err = n * fg - t_ref[...]
        loss_ref[...] += 0.5 * jnp.sum(jnp.mean(err * err, axis=-1, keepdims=True), axis=0, keepdims=True)
        dy = err * (1.0 / D_MODEL)
        gfg_ref[...] += jnp.sum(dy * n, axis=0, keepdims=True)
        dyg = dy * fg
        dh = r * (dyg - n * jnp.mean(dyg * n, axis=-1, keepdims=True))
        dh_ref[...] = dh
        dhb = dh.astype(BF16)
        dmix_ref[...] = _nt(dhb, wo).astype(ACT)
        gwo_ref[...] += _tn(mix, dhb)

    row = lambda i: (i, 0)
    fixed = lambda i: (0, 0)
    return pl.pallas_call(
        body,
        name="out_loss",
        grid=(S // TM,),
        in_specs=[
            pl.BlockSpec((TM, D_MODEL), row),
            pl.BlockSpec((TM, D_MODEL), row),
            pl.BlockSpec((TM, ATTN_W), row),
            pl.BlockSpec((TM, PC_W), row),
            _prev_rows(PC_W),
            pl.BlockSpec((CONV_K, CONV_W), fixed),
            pl.BlockSpec((D_MODEL, D_MODEL), fixed),
            pl.BlockSpec((1, D_MODEL), fixed),
        ],
        out_specs=[
            pl.BlockSpec((TM, D_MODEL), row),
            pl.BlockSpec((TM, D_MODEL), row),
            pl.BlockSpec((D_MODEL, D_MODEL), fixed),
            pl.BlockSpec((1, D_MODEL), fixed),
            pl.BlockSpec((1, 1), fixed),
        ],
        out_shape=[
            jax.ShapeDtypeStruct((S, D_MODEL), F32),
            jax.ShapeDtypeStruct((S, D_MODEL), ACT),
            jax.ShapeDtypeStruct((D_MODEL, D_MODEL), F32),
            jax.ShapeDtypeStruct((1, D_MODEL), F32),
            jax.ShapeDtypeStruct((1, 1), F32),
        ],
        compiler_params=_params(("arbitrary",)),
    )(x, target, ya, pc, pc, conv_w, w_out, final_g)


def _attn_bwd(pa, dmix, tab, sinks):
    S = pa.shape[0]
    nt = S // TQ
    nb = TQ // BLK

    def body(sink_ref, q_ref, g_ref, kvc_ref, kvp_ref, tabc_ref, tabp_ref, dm_ref,
             d_ref, dsink_ref, kall, dkv, carry, q_sc, do_sc, p_sc, ds_sc, dsink_acc):
        step = pl.program_id(0)
        i = nt - 1 - step

        @pl.when(step == 0)
        def _():
            carry[...] = jnp.zeros_like(carry)
            dsink_acc[...] = jnp.zeros_like(dsink_acc)

        _fill_kv(kall, kvc_ref, kvp_ref, tabc_ref, tabp_ref)
        dkv[0:TQ, :] = jnp.zeros((TQ, 2 * KV_W), F32)
        dkv[TQ:TQ + BLK, :] = carry[...]
        lane = lax.broadcasted_iota(jnp.int32, (BLK, 128), 1)
        half = [lane < HEAD_DIM, lane >= HEAD_DIM]
        upper = _upper()
        sinks = _sink_rows(sink_ref)
        for j in range(nb):
            rq = slice(j * BLK, (j + 1) * BLK)
            rk = slice(j * BLK, (j + 2) * BLK)
            tab = tabc_ref[rq, :]
            pair = [slice(p * 128, (p + 1) * 128) for p in range(4)]
            qr = [_rope(q_ref[rq, c].astype(F32), tab) * 0.125 for c in pair]
            g = [g_ref[rq, c].astype(F32) for c in pair]
            da = [dm_ref[rq, c].astype(F32) for c in pair]
            do = [da[p] * _silu(g[p]) for p in range(4)]
            outs, dqs, dks, dvs = [], [], [], []
            for s in range(2):
                kk = kall[s, rk, :]
                vv = kall[2 + s, rk, :]
                _stack_heads(q_sc, s, half, qr)
                _stack_heads(do_sc, s, half, do)
                prev, cur = _scores(kk, q_sc[s], i == 0 if j == 0 else None)
                prob, psink = _softmax(jnp.where(upper, prev, cur), sinks[s])
                _split_store(p_sc, s, upper, prob)
                dprob = _merge(upper, _nt(vv, do_sc[s]))
                dsum = jnp.sum(dprob * prob, axis=0, keepdims=True)
                _split_store(ds_sc, s, upper, prob * (dprob - dsum))
                dsink_acc[s, 0:1, :] += psink * dsum
                outs.append(_tn(p_sc[s], vv))
                dqs.append(_tn(ds_sc[s], kk))
                dks.append(_nn(ds_sc[s], q_sc[s]))
                dvs.append(_nn(p_sc[s], do_sc[s]))
            for p in range(4):
                d_ref[rq, pair[p]] = _rope_t(_unstack_pair(half, dqs, p) * 0.125, tab).astype(BF16)
                d_ref[rq, 512 + p * 128:512 + (p + 1) * 128] = (
                    da[p] * _unstack_pair(half, outs, p) * _dsilu(g[p])).astype(BF16)
            dkv[rk, 0:128] += dks[0] + pltpu.roll(dks[1], 64, 1)
            dkv[rk, 128:256] += dvs[0] + pltpu.roll(dvs[1], 64, 1)
        d_ref[:, 1024:1152] = _rope_t(dkv[BLK:BLK + TQ, 0:128], tabc_ref[...]).astype(BF16)
        d_ref[:, 1152:1280] = dkv[BLK:BLK + TQ, 128:256].astype(BF16)
        carry[...] = dkv[0:BLK, :]

        @pl.when(step == nt - 1)
        def _():
            for s in range(2):
                for a, (p, e) in enumerate(HEADS[s]):
                    h = 2 * p + e
                    tot = jnp.sum(dsink_acc[s, 0:1, a * BLK:(a + 1) * BLK], axis=1, keepdims=True)
                    dsink_ref[h:h + 1, :] = jnp.broadcast_to(-tot, (1, 128))

    rev = lambda s: nt - 1 - s
    return pl.pallas_call(
        body,
        name="attn_bwd",
        grid=(nt,),
        in_specs=_attn_specs(rev) + [pl.BlockSpec((TQ, ATTN_W), lambda s: (nt - 1 - s, 0))],
        out_specs=[
            pl.BlockSpec((TQ, PA_W), lambda s: (nt - 1 - s, 0)),
            pl.BlockSpec((8, 128), lambda s: (0, 0)),
        ],
        out_shape=[
            jax.ShapeDtypeStruct((S, PA_W), BF16),
            jax.ShapeDtypeStruct((8, 128), F32),
        ],
        scratch_shapes=[
            pltpu.VMEM((4, BLK + TQ, 128), BF16),
            pltpu.VMEM((BLK + TQ, 2 * KV_W), F32),
            pltpu.VMEM((BLK, 2 * KV_W), F32),
            pltpu.VMEM((2, 4 * BLK, 128), BF16),
            pltpu.VMEM((2, 4 * BLK, 128), BF16),
            pltpu.VMEM((2, 2 * BLK, 4 * BLK), BF16),
            pltpu.VMEM((2, 2 * BLK, 4 * BLK), BF16),
            pltpu.VMEM((2, 8, 4 * BLK), F32),
        ],
        compiler_params=_params(("arbitrary",)),
    )(sinks, pa, pa, pa, pa, tab, tab, dmix)


def _conv_bwd_tile(pc_ref, prev_ref, next_ref, dm_ref, dmn_ref, w_ref, d_ref, gw_ref, has_prev, has_next):
    rows = pc_ref.shape[0]
    w0, w1, w2 = w_ref[0:1, :], w_ref[1:2, :], w_ref[2:3, :]
    b, c, hh, gc, u, um1, um2, cv = _conv_tile(pc_ref, prev_ref, w_ref, has_prev)
    sg = _silu(gc)
    dy = dm_ref[...].astype(F32)
    dcv = dy * b * sg

    def next_dcv(r):
        nd = (dmn_ref[r:r + 1, :].astype(F32) * next_ref[r:r + 1, 0:512].astype(F32)
              * _silu(next_ref[r:r + 1, 1536:2048].astype(F32)))
        return jnp.where(has_next, nd, 0.0)

    row = lax.broadcasted_iota(jnp.int32, (rows, CONV_W), 0)
    dp1 = jnp.where(row == rows - 1, next_dcv(0), pltpu.roll(dcv, rows - 1, 0))
    dp2 = jnp.where(row == rows - 1, next_dcv(1),
                    jnp.where(row == rows - 2, next_dcv(0), pltpu.roll(dcv, rows - 2, 0)))
    du = w2 * dcv + w1 * dp1 + w0 * dp2
    d_ref[:, 0:512] = (dy * cv * sg).astype(BF16)
    d_ref[:, 512:1024] = (du * hh).astype(BF16)
    d_ref[:, 1024:1536] = (du * c).astype(BF16)
    d_ref[:, 1536:2048] = (dy * b * cv * _dsilu(gc)).astype(BF16)
    gw_ref[0:1, :] += jnp.sum(dcv * um2, axis=0, keepdims=True)
    gw_ref[1:2, :] += jnp.sum(dcv * um1, axis=0, keepdims=True)
    gw_ref[2:3, :] += jnp.sum(dcv * u, axis=0, keepdims=True)


def _grad_x(da, dc, wt, x, dh, norm_g, grads):
    S = x.shape[0]
    n_steps = S // TM
    rs = _ReduceScatter(grads)
    n_rs_out = len(rs.out_shape())

    def body(da_ref, dc_ref, wt_ref, x_ref, dh_ref, g_ref, *rest):
        grad_refs, rest = rest[:rs.n], rest[rs.n:]
        gx_ref, gng_ref = rest[:2]
        rs_out, rs_scratch = rest[2:2 + n_rs_out], rest[2 + n_rs_out:]
        step = pl.program_id(0)
        finish = rs.emit(step, n_steps, grad_refs, rs_out, rs_scratch)

        @pl.when(step == 0)
        def _():
            gng_ref[...] = jnp.zeros_like(gng_ref)

        dxn = (_nn(da_ref[:, 0:512], wt_ref[0:512, :]) + _nn(da_ref[:, 512:1024], wt_ref[768:1280, :])
               + _nn(da_ref[:, 1024:1280], wt_ref[512:768, :]) + _nn(dc_ref[...], wt_ref[1280:3328, :]))
        xv = x_ref[...]
        r = lax.rsqrt(jnp.mean(xv * xv, axis=-1, keepdims=True) + EPS)
        n = xv * r
        gng_ref[...] += jnp.sum(dxn * n, axis=0, keepdims=True)
        dxg = dxn * g_ref[...]
        gx_ref[...] = dh_ref[...] + r * (dxg - n * jnp.mean(dxg * n, axis=-1, keepdims=True))
        finish()

    row = lambda i: (i, 0)
    fixed = lambda i: (0, 0)
    any_spec = pl.BlockSpec(memory_space=pl.ANY)
    outs = pl.pallas_call(
        body,
        name="grad_x_reduce_scatter",
        grid=(n_steps,),
        in_specs=[
            pl.BlockSpec((TM, PA_W), row),
            pl.BlockSpec((TM, PC_W), row),
            pl.BlockSpec((IN_W, D_MODEL), fixed),
            pl.BlockSpec((TM, D_MODEL), row),
            pl.BlockSpec((TM, D_MODEL), row),
            pl.BlockSpec((1, D_MODEL), fixed),
        ] + [any_spec] * rs.n,
        out_specs=[pl.BlockSpec((TM, D_MODEL), row), pl.BlockSpec((1, D_MODEL), fixed)] + [any_spec] * n_rs_out,
        out_shape=[jax.ShapeDtypeStruct((S, D_MODEL), F32), jax.ShapeDtypeStruct((1, D_MODEL), F32)] + rs.out_shape(),
        scratch_shapes=rs.scratch_shapes(),
        compiler_params=_params(("arbitrary",)),
    )(da, dc, wt, x, dh, norm_g, *grads)
    return outs[0], outs[1], outs[2:2 + rs.n], outs[2 + rs.n:2 + 2 * rs.n]


def _grad_w_in(da, pc, dmix, conv_w, xn):
    S = xn.shape[0]
    nt = S // TM
    t16 = TM // HALO

    def body(da_ref, pc_ref, prev_ref, next_ref, dm_ref, dmn_ref, cw_ref, xn_ref, gw_ref, dc_ref, gcw_ref):
        i = pl.program_id(0)

        @pl.when(i == 0)
        def _():
            gw_ref[...] = jnp.zeros_like(gw_ref)
            gcw_ref[...] = jnp.zeros_like(gcw_ref)

        _conv_bwd_tile(pc_ref, prev_ref, next_ref, dm_ref, dmn_ref, cw_ref, dc_ref, gcw_ref, i > 0, i < nt - 1)
        xn = xn_ref[...]
        gw_ref[0:512, :] += _tn(da_ref[:, 0:512], xn)
        gw_ref[768:1280, :] += _tn(da_ref[:, 512:1024], xn)
        gw_ref[512:768, :] += _tn(da_ref[:, 1024:1280], xn)
        gw_ref[1280:3328, :] += _tn(dc_ref[...], xn)

    row = lambda i: (i, 0)
    fixed = lambda i: (0, 0)
    nxt = lambda i: jnp.minimum((i + 1) * t16, nt * t16 - 1)
    return pl.pallas_call(
        body,
        name="grad_w_in",
        grid=(nt,),
        in_specs=[
            pl.BlockSpec((TM, PA_W), row),
            pl.BlockSpec((TM, PC_W), row),
            _prev_rows(PC_W),
            pl.BlockSpec((HALO, PC_W), lambda i: (nxt(i), 0)),
            pl.BlockSpec((TM, CONV_W), lambda i: (i, 1)),
            pl.BlockSpec((HALO, CONV_W), lambda i: (nxt(i), 1)),
            pl.BlockSpec((CONV_K, CONV_W), fixed),
            pl.BlockSpec((TM, D_MODEL), row),
        ],
        out_specs=[
            pl.BlockSpec((IN_W, D_MODEL), fixed),
            pl.BlockSpec((TM, PC_W), row),
            pl.BlockSpec((CONV_K, CONV_W), fixed),
        ],
        out_shape=[
            jax.ShapeDtypeStruct((IN_W, D_MODEL), F32),
            jax.ShapeDtypeStruct((S, PC_W), BF16),
            jax.ShapeDtypeStruct((CONV_K, CONV_W), F32),
        ],
        compiler_params=_params(("arbitrary",)),
    )(da, pc, pc, pc, dmix, dmix, conv_w, xn)


def _sum_chips(own, others, name):
    def body(own_ref, p_ref, o_ref):
        acc = own_ref[...]
        for k in range(N_CHIP - 1):
            acc = acc + p_ref[k].astype(F32)
        o_ref[...] = acc

    return pl.pallas_call(
        body,
        name=name,
        out_shape=jax.ShapeDtypeStruct(own.shape, F32),
        compiler_params=_params(),
    )(own, others)


def _sum_parts(parts, name):
    n = parts.shape[0]

    def body(p_ref, o_ref):
        acc = p_ref[0]
        for k in range(1, n):
            acc = acc + p_ref[k]
        o_ref[...] = acc

    return pl.pallas_call(
        body,
        name=name,
        out_shape=jax.ShapeDtypeStruct(parts.shape[1:], F32),
        compiler_params=_params(),
    )(parts)


def _adamw(w, g, m, v, name):
    c1 = 1.0 - ADAM_B1 ** ADAM_STEP
    c2 = 1.0 - ADAM_B2 ** ADAM_STEP

    def body(w_ref, g_ref, m_ref, v_ref, d_ref, nm_ref, nv_ref):
        g = g_ref[...]
        nm = ADAM_B1 * m_ref[...] + (1.0 - ADAM_B1) * g
        nv = ADAM_B2 * v_ref[...] + (1.0 - ADAM_B2) * (g * g)
        nm_ref[...] = nm
        nv_ref[...] = nv
        d_ref[...] = -ADAM_LR * ((nm / c1) / (jnp.sqrt(nv / c2) + ADAM_EPS) + ADAM_WD * w_ref[...])

    shape = jax.ShapeDtypeStruct(w.shape, F32)
    return pl.pallas_call(
        body,
        name=name,
        out_shape=[shape, shape, shape],
        compiler_params=_params(),
    )(w, g, m, v)


def _rope_table(S):
    half = ROT_DIM // 2
    pos = jnp.arange(S, dtype=jnp.int32).astype(F32)
    inv_freq = ROPE_THETA ** (-jnp.arange(0, ROT_DIM, 2, dtype=F32) / ROT_DIM)
    ang = inv_freq[:, None] * pos[None, :]
    cs = jnp.concatenate([jnp.cos(ang), jnp.sin(ang)], axis=0)
    sel = np.zeros((2 * half, 384), np.float32)
    ones = np.zeros((1, 384), np.float32)
    for l in range(128):
        r = l % HEAD_DIM
        if r < half:
            sel[r, l] = 1.0
            sel[half + r, 128 + l] = -1.0
        elif r < ROT_DIM:
            sel[r - half, l] = 1.0
            sel[half + r - half, 256 + l] = 1.0
        else:
            ones[0, l] = 1.0
    return lax.dot_general(cs, jnp.asarray(sel), TN_DIMS, precision=lax.Precision.HIGHEST) + jnp.asarray(ones)


def kernel(x, norm_g, w_in, sinks, conv_w, w_out, final_g, loss_target, m_norm_g, m_w_in, m_sinks, m_conv_w, m_w_out, m_final_g, v_norm_g, v_w_in, v_sinks, v_conv_w, v_w_out, v_final_g):
    S = x.shape[1]
    me = 4 * lax.axis_index("x") + 2 * lax.axis_index("y") + lax.axis_index("c")
    x2 = x.reshape(S, D_MODEL)
    t2 = loss_target.reshape(S, D_MODEL)
    ng = norm_g.reshape(1, D_MODEL)
    fg = final_g.reshape(1, D_MODEL)

    cw_pad = jnp.zeros((8, 128), F32).at[0:CONV_K, 0:64].set(conv_w)
    (wt,) = _all_gather([w_in.T.astype(BF16)], "all_gather_w_in")

    tab = _rope_table(S)
    xn, pa, pc, (wo, cw_all) = _fwd_proj(x2, ng, wt, [w_out.astype(BF16), cw_pad])
    cw = cw_all.reshape(N_DEV, 8, 128)[:, 0:CONV_K, 0:64].transpose(1, 0, 2).reshape(CONV_K, CONV_W)
    ya = _attn_fwd(pa, tab, sinks)
    dh, dmix, g_wo, g_fg, loss_part = _out_loss(x2, t2, ya, pc, cw, wo, fg)
    da, g_sinks = _attn_bwd(pa, dmix, tab, sinks)
    g_wt, dc, g_cw = _grad_w_in(da, pc, dmix, cw, xn)
    grad_x, g_ng, own, others = _grad_x(
        da, dc, wt, x2, dh, ng,
        [g_wt.reshape(N_DEV, SHARD_IN, D_MODEL), g_wo.reshape(N_DEV, SHARD_OUT, D_MODEL)])
    grad_w_in = _sum_chips(own[0], others[0], "sum_grad_w_in").T
    grad_w_out = _sum_chips(own[1], others[1], "sum_grad_w_out")
    small = jnp.concatenate([
        g_ng.reshape(8, 128), g_fg.reshape(8, 128), g_sinks,
        g_cw.reshape(12, 128), jnp.broadcast_to(loss_part, (4, 128))], axis=0)
    (small_all,) = _all_gather([small], "all_gather_small_grads")
    small_sum = _sum_parts(small_all.reshape(N_DEV, SMALL_ROWS, 128), "sum_small_grads")
    grad_norm_g = small_sum[0:8].reshape(D_MODEL)
    grad_final_g = small_sum[8:16].reshape(D_MODEL)
    grad_sinks = small_sum[16:24, 0]
    grad_conv_w = lax.dynamic_slice(small_sum[24:36].reshape(CONV_K, CONV_W), (0, me * 64), (CONV_K, 64))
    loss = small_sum[36, 0]

    d_w_in, nm_w_in, nv_w_in = _adamw(w_in, grad_w_in, m_w_in, v_w_in, "adamw_w_in")
    d_w_out, nm_w_out, nv_w_out = _adamw(w_out, grad_w_out, m_w_out, v_w_out, "adamw_w_out")
    def pack(a, b, c_, d):
        return jnp.concatenate([
            a.reshape(8, 128), b.reshape(8, 128),
            jnp.zeros((8, 128), F32).at[0, 0:8].set(c_).at[1:1 + CONV_K, 0:64].set(d)], axis=1)

    d_s, nm_s, nv_s = _adamw(
        pack(norm_g, final_g, sinks, conv_w), pack(grad_norm_g, grad_final_g, grad_sinks, grad_conv_w),
        pack(m_norm_g, m_final_g, m_sinks, m_conv_w),
        pack(v_norm_g, v_final_g, v_sinks, v_conv_w),
        "adamw_small")

    def unpack(p):
        return (p[:, 0:128].reshape(D_MODEL), p[:, 128:256].reshape(D_MODEL), p[0, 256:264], p[1:1 + CONV_K, 256:320])

    d_ng, d_fg, d_sk, d_cw = unpack(d_s)
    nm_ng, nm_fg, nm_sk, nm_cw = unpack(nm_s)
    nv_ng, nv_fg, nv_sk, nv_cw = unpack(nv_s)

    return (loss, grad_x.reshape(1, S, D_MODEL), grad_norm_g, grad_w_in, grad_sinks, grad_conv_w, grad_w_out, grad_final_g,
            d_ng, d_w_in, d_sk, d_cw, d_w_out, d_fg,
            nm_ng, nm_w_in, nm_sk, nm_cw, nm_w_out, nm_fg,
            nv_ng, nv_w_in, nv_sk, nv_cw, nv_w_out, nv_fg)
```

```python
import numpy as np
import jax
import jax.numpy as jnp
from jax import lax
from jax.experimental import pallas as pl
from jax.experimental.pallas import tpu as pltpu

F32 = jnp.float32
BF16 = jnp.bfloat16
MESH = pl.DeviceIdType.MESH

D_MODEL = 1024
HEAD_DIM = 64
N_Q_HEADS = 8
GROUP = 4
ATTN_W = 512
KV_W = 128
BLK = 128
CONV_W = 512
CONV_K = 3
IN_W = 3328
PA_W = 1280
PC_W = 2048
EPS = 1e-5
ROPE_THETA = 500000.0
ROT_DIM = 16
N_DEV = 8
N_CHIP = 4
SHARD_IN = IN_W // N_DEV
SHARD_OUT = D_MODEL // N_DEV
SMALL_ROWS = 40

ADAM_LR = 0.001
ADAM_B1 = 0.9
ADAM_B2 = 0.999
ADAM_EPS = 1e-08
ADAM_WD = 0.01
ADAM_STEP = 10

ACT = jnp.bfloat16

TM = 512
TQ = 512
TC = 512
HALO = 16
VMEM_LIMIT = 56 * 1024 * 1024

NT_DIMS = (((1,), (1,)), ((), ()))
TN_DIMS = (((0,), (0,)), ((), ()))


def _params(sem=None):
    kw = dict(vmem_limit_bytes=VMEM_LIMIT)
    if sem is not None:
        kw["dimension_semantics"] = sem
    return pltpu.CompilerParams(**kw)


def _nt(a, b):
    return lax.dot_general(a, b, NT_DIMS, preferred_element_type=F32)


def _tn(a, b):
    return lax.dot_general(a, b, TN_DIMS, preferred_element_type=F32)


def _nn(a, b):
    return jnp.dot(a, b, preferred_element_type=F32)


def _silu(g):
    return g * jax.nn.sigmoid(g)


def _dsilu(g):
    s = jax.nn.sigmoid(g)
    return s * (1.0 + g * (1.0 - s))


def _all_gather(arrs, name):
    n_arr = len(arrs)

    def body(*refs):
        x_refs = refs[:n_arr]
        out_refs = refs[n_arr:2 * n_arr]
        send_sems, recv_sems, local_sems = refs[2 * n_arr:]
        x, y, c = lax.axis_index("x"), lax.axis_index("y"), lax.axis_index("c")
        me, sibling = (x, y, c), (x, y, 1 - c)
        chips = [(1 - x, y), (x, 1 - y), (1 - x, 1 - y)]

        def rows(a, px, py, pc):
            m = x_refs[a].shape[0]
            return out_refs[a].at[pl.ds((4 * px + 2 * py + pc) * m, m), :]

        def copy(a, k, block, to, src=None):
            return pltpu.make_async_remote_copy(
                src_ref=rows(a, *block) if src is None else src,
                dst_ref=rows(a, *block),
                send_sem=send_sems.at[a * 7 + k],
                recv_sem=recv_sems.at[a * 7 + k],
                device_id=to,
                device_id_type=MESH,
            )

        mine = [pltpu.make_async_copy(x_refs[a], rows(a, *me), local_sems.at[a]) for a in range(n_arr)]
        for cp in mine:
            cp.start()
        first = []
        for a in range(n_arr):
            first.append(copy(a, 0, me, sibling, src=x_refs[a]))
            first += [copy(a, 1 + j, me, (*chip, c), src=x_refs[a]) for j, chip in enumerate(chips)]
        for cp in first:
            cp.start()
        passed = []
        for j, chip in enumerate(chips):
            for a in range(n_arr):
                copy(a, 1 + j, (*chip, c), me).wait_recv()
                fwd = copy(a, 4 + j, (*chip, c), sibling)
                fwd.start()
                passed.append(fwd)
        for a in range(n_arr):
            copy(a, 0, sibling, me).wait_recv()
            for j, chip in enumerate(chips):
                copy(a, 4 + j, (*chip, 1 - c), me).wait_recv()
        for cp in first + passed:
            cp.wait_send()
        for cp in mine:
            cp.wait()

    vmem = pl.BlockSpec(memory_space=pltpu.VMEM)
    return pl.pallas_call(
        body,
        name=name,
        out_shape=[jax.ShapeDtypeStruct((N_DEV * a.shape[0], a.shape[1]), a.dtype) for a in arrs],
        in_specs=[vmem] * n_arr,
        out_specs=[vmem] * n_arr,
        scratch_shapes=[
            pltpu.SemaphoreType.DMA((7 * n_arr,)),
            pltpu.SemaphoreType.DMA((7 * n_arr,)),
            pltpu.SemaphoreType.DMA((n_arr,)),
        ],
        compiler_params=_params(),
    )(*arrs)


class _AllGatherInSteps:
    forward_step = 3

    def __init__(self, arrs):
        self.blocks = [(a.shape, a.dtype) for a in arrs]
        self.n = len(arrs)

    def out_shape(self):
        return [jax.ShapeDtypeStruct((N_DEV * s[0], s[1]), d) for s, d in self.blocks]

    def scratch_shapes(self):
        return [pltpu.SemaphoreType.DMA((7 * self.n,)), pltpu.SemaphoreType.DMA((7 * self.n,)),
                pltpu.SemaphoreType.DMA((self.n,))]

    def emit(self, step, n_steps, x_refs, out_refs, scratch):
        assert n_steps > self.forward_step + 1
        send_sems, recv_sems, local_sems = scratch
        x, y, c = lax.axis_index("x"), lax.axis_index("y"), lax.axis_index("c")
        me, sibling = (x, y, c), (x, y, 1 - c)
        chips = [(1 - x, y), (x, 1 - y), (1 - x, 1 - y)]

        def rows(a, px, py, pc):
            m = self.blocks[a][0][0]
            return out_refs[a].at[pl.ds((4 * px + 2 * py + pc) * m, m), :]

        def copy(a, k, block, to, src=None):
            return pltpu.make_async_remote_copy(
                src_ref=rows(a, *block) if src is None else src, dst_ref=rows(a, *block),
                send_sem=send_sems.at[a * 7 + k], recv_sem=recv_sems.at[a * 7 + k],
                device_id=to, device_id_type=MESH)

        def mine(a):
            return pltpu.make_async_copy(x_refs[a], rows(a, *me), local_sems.at[a])

        def first(a):
            return ([copy(a, 0, me, sibling, src=x_refs[a])]
                    + [copy(a, 1 + j, me, (*chip, c), src=x_refs[a]) for j, chip in enumerate(chips)])

        def passed(a):
            return [copy(a, 4 + j, (*chip, c), sibling) for j, chip in enumerate(chips)]

        @pl.when(step == 0)
        def _():
            for a in range(self.n):
                mine(a).start()
                for cp in first(a):
                    cp.start()

        @pl.when(step == self.forward_step)
        def _():
            for j, chip in enumerate(chips):
                for a in range(self.n):
                    copy(a, 1 + j, (*chip, c), me).wait_recv()
                    copy(a, 4 + j, (*chip, c), sibling).start()

        def finish():
            @pl.when(step == n_steps - 1)
            def _():
                for a in range(self.n):
                    copy(a, 0, sibling, me).wait_recv()
                    for j, chip in enumerate(chips):
                        copy(a, 4 + j, (*chip, 1 - c), me).wait_recv()
                    for cp in first(a) + passed(a):
                        cp.wait_send()
                    mine(a).wait()

        return finish


class _ReduceScatter:
    def __init__(self, grads):
        self.shapes = [g.shape[1:] for g in grads]
        self.n = len(grads)
        self.items = tuple((a, r) for r in (1, 2, 3, 0) for a in range(self.n))
        self.steps = len(self.items) + 2

    def out_shape(self):
        own = [jax.ShapeDtypeStruct(s, F32) for s in self.shapes]
        ici = [jax.ShapeDtypeStruct((N_CHIP - 1,) + s, BF16) for s in self.shapes]
        land = [jax.ShapeDtypeStruct((N_CHIP,) + s, F32) for s in self.shapes]
        return own + ici + land

    def scratch_shapes(self):
        n_items = len(self.items)
        return ([pltpu.VMEM((2,) + s, F32) for s in self.shapes]
                + [pltpu.VMEM((N_CHIP - 1,) + s, BF16) for s in self.shapes]
                + [pltpu.VMEM(s, F32) for s in self.shapes]
                + [pltpu.SemaphoreType.DMA((self.n * N_CHIP,))] * 2
                + [pltpu.SemaphoreType.DMA((2 * n_items,))]
                + [pltpu.SemaphoreType.DMA((self.n * (N_CHIP - 1),))] * 2
                + [pltpu.SemaphoreType.DMA((self.n,))])

    def emit(self, step, n_steps, g_refs, out_refs, scratch):
        assert n_steps > self.steps
        n = self.n
        own_refs, ici_refs, land_refs = out_refs[:n], out_refs[n:2 * n], out_refs[2 * n:]
        stage, pair_bf, pair_own = scratch[:n], scratch[n:2 * n], scratch[2 * n:3 * n]
        sib_send, sib_recv, load_sems, ici_send, ici_recv, own_sems = scratch[3 * n:]
        x, y, c = lax.axis_index("x"), lax.axis_index("y"), lax.axis_index("c")

        def chip_of(r):
            return (x ^ (r >> 1), y ^ (r & 1))

        def block_of(r, core):
            cx, cy = chip_of(r)
            return 4 * cx + 2 * cy + core

        def to_sibling(a, r):
            return pltpu.make_async_remote_copy(
                src_ref=g_refs[a].at[block_of(r, 1 - c)], dst_ref=land_refs[a].at[r],
                send_sem=sib_send.at[a * N_CHIP + r], recv_sem=sib_recv.at[a * N_CHIP + r],
                device_id=(x, y, 1 - c), device_id_type=MESH)

        def loads(k):
            a, r = self.items[k]
            return (pltpu.make_async_copy(g_refs[a].at[block_of(r, c)], stage[a].at[0], load_sems.at[2 * k]),
                    pltpu.make_async_copy(land_refs[a].at[r], stage[a].at[1], load_sems.at[2 * k + 1]))

        def to_owner(k):
            a, r = self.items[k]
            if r == 0:
                return pltpu.make_async_copy(pair_own[a], own_refs[a], own_sems.at[a])
            return pltpu.make_async_remote_copy(
                src_ref=pair_bf[a].at[r - 1], dst_ref=ici_refs[a].at[r - 1],
                send_sem=ici_send.at[a * (N_CHIP - 1) + r - 1], recv_sem=ici_recv.at[a * (N_CHIP - 1) + r - 1],
                device_id=(*chip_of(r), c), device_id_type=MESH)

        @pl.when(step == 0)
        def _():
            for a, r in self.items:
                to_sibling(a, r).start()

        for k, (a, r) in enumerate(self.items):
            @pl.when(step == 1 + k)
            def _(k=k, a=a, r=r):
                to_sibling(a, r).wait_recv()
                for cp in loads(k):
                    cp.start()

            @pl.when(step == 2 + k)
            def _(k=k, a=a, r=r):
                for cp in loads(k):
                    cp.wait()
                total = stage[a][0] + stage[a][1]
                if r == 0:
                    pair_own[a][...] = total
                else:
                    pair_bf[a][r - 1] = total.astype(BF16)
                to_owner(k).start()

        def finish():
            @pl.when(step == n_steps - 1)
            def _():
                for k, (a, r) in enumerate(self.items):
                    if r == 0:
                        to_owner(k).wait()
                    else:
                        to_owner(k).wait_send()
                        to_owner(k).wait_recv()
                for a, r in self.items:
                    to_sibling(a, r).wait_send()

        return finish


def _fwd_proj(x, norm_g, wt, later):
    S = x.shape[0]

    n_steps = S // TM
    ag = _AllGatherInSteps(later)

    def body(x_ref, g_ref, wt_ref, *rest):
        later_refs, rest = rest[:ag.n], rest[ag.n:]
        xn_ref, pa_ref, pc_ref = rest[:3]
        gathered, ag_scratch = rest[3:3 + ag.n], rest[3 + ag.n:]
        step = pl.program_id(0)
        finish = ag.emit(step, n_steps, later_refs, gathered, ag_scratch)
        xv = x_ref[...]
        r = lax.rsqrt(jnp.mean(xv * xv, axis=-1, keepdims=True) + EPS)
        xn = (xv * r * g_ref[...]).astype(BF16)
        xn_ref[...] = xn
        pa_ref[:, 0:512] = _nt(xn, wt_ref[0:512, :]).astype(ACT)
        pa_ref[:, 512:1024] = _nt(xn, wt_ref[768:1280, :]).astype(ACT)
        pa_ref[:, 1024:1280] = _nt(xn, wt_ref[512:768, :]).astype(ACT)
        pc_ref[...] = _nt(xn, wt_ref[1280:3328, :]).astype(ACT)
        finish()

    any_spec = pl.BlockSpec(memory_space=pl.ANY)
    outs = pl.pallas_call(
        body,
        name="fwd_proj_all_gather",
        grid=(n_steps,),
        in_specs=[
            pl.BlockSpec((TM, D_MODEL), lambda i: (i, 0)),
            pl.BlockSpec((1, D_MODEL), lambda i: (0, 0)),
            pl.BlockSpec((IN_W, D_MODEL), lambda i: (0, 0)),
        ] + [any_spec] * ag.n,
        out_specs=[
            pl.BlockSpec((TM, D_MODEL), lambda i: (i, 0)),
            pl.BlockSpec((TM, PA_W), lambda i: (i, 0)),
            pl.BlockSpec((TM, PC_W), lambda i: (i, 0)),
        ] + [any_spec] * ag.n,
        out_shape=[
            jax.ShapeDtypeStruct((S, D_MODEL), BF16),
            jax.ShapeDtypeStruct((S, PA_W), ACT),
            jax.ShapeDtypeStruct((S, PC_W), ACT),
        ] + ag.out_shape(),
        scratch_shapes=ag.scratch_shapes(),
        compiler_params=_params(("arbitrary",)),
    )(x, norm_g, wt, *later)
    return outs[0], outs[1], outs[2], outs[3:]


def _rope(t, tab):
    return (t * tab[:, 0:128] + pltpu.roll(t, 120, 1) * tab[:, 128:256]
            + pltpu.roll(t, 8, 1) * tab[:, 256:384])


def _rope_t(d, tab):
    return (d * tab[:, 0:128] + pltpu.roll(d * tab[:, 128:256], 8, 1)
            + pltpu.roll(d * tab[:, 256:384], 120, 1))


def _fill_kv(kall, kvc_ref, kvp_ref, tabc_ref, tabp_ref):
    for lo, kv_ref, tab_ref, n in ((0, kvp_ref, tabp_ref, BLK), (BLK, kvc_ref, tabc_ref, TQ)):
        k = _rope(kv_ref[:, 0:128].astype(F32), tab_ref[...])
        v = kv_ref[:, 128:256].astype(F32)
        kall[0, lo:lo + n, :] = k.astype(BF16)
        kall[1, lo:lo + n, :] = pltpu.roll(k, 64, 1).astype(BF16)
        kall[2, lo:lo + n, :] = v.astype(BF16)
        kall[3, lo:lo + n, :] = pltpu.roll(v, 64, 1).astype(BF16)


HEADS = (((0, 0), (1, 0), (2, 1), (3, 1)), ((0, 1), (1, 1), (2, 0), (3, 0)))


def _upper():
    kj = lax.broadcasted_iota(jnp.int32, (BLK, 4 * BLK), 0)
    qi = lax.broadcasted_iota(jnp.int32, (BLK, 4 * BLK), 1) & (BLK - 1)
    return kj > qi


def _merge(upper, both):
    return jnp.where(upper, both[0:BLK, :], both[BLK:2 * BLK, :])


def _split_store(ref, s, upper_b, val):
    vb = val.astype(BF16)
    first = vb * upper_b
    ref[s, 0:BLK, :] = first
    ref[s, BLK:2 * BLK, :] = vb - first


def _sink_rows(sink_ref):
    return [jnp.concatenate([jnp.full((1, BLK), sink_ref[2 * p + e], F32) for p, e in HEADS[s]], axis=1)
            for s in range(2)]


def _stack_heads(ref, s, half, pairs):
    for a, (p, e) in enumerate(HEADS[s]):
        ref[s, a * BLK:(a + 1) * BLK, :] = jnp.where(half[e], pairs[p], 0.0).astype(BF16)


def _unstack_pair(half, outs, p):
    lo = 0 if p < 2 else 1
    rows = slice(p * BLK, (p + 1) * BLK)
    return jnp.where(half[0], outs[lo][rows, :], outs[1 - lo][rows, :])


def _softmax(sm, sinks):
    m = jnp.maximum(jnp.max(sm, axis=0, keepdims=True), sinks)
    p = jnp.exp(sm - m)
    es = jnp.exp(sinks - m)
    inv = 1.0 / (jnp.sum(p, axis=0, keepdims=True) + es)
    return p * inv, es * inv


def _scores(kk, q_stack, first):
    st = _nt(kk, q_stack)
    prev = st[0:BLK, :]
    if first is not None:
        prev = prev + jnp.where(first, -jnp.inf, 0.0)
    return prev, st[BLK:2 * BLK, :]


def _attn_specs(tile):
    nb = TQ // BLK
    prev = lambda i: jnp.maximum(tile(i) * nb - 1, 0)
    return [
        pl.BlockSpec(memory_space=pltpu.SMEM),
        pl.BlockSpec((TQ, ATTN_W), lambda i: (tile(i), 0)),
        pl.BlockSpec((TQ, ATTN_W), lambda i: (tile(i), 1)),
        pl.BlockSpec((TQ, 2 * KV_W), lambda i: (tile(i), 4)),
        pl.BlockSpec((BLK, 2 * KV_W), lambda i: (prev(i), 4)),
        pl.BlockSpec((TQ, 384), lambda i: (tile(i), 0)),
        pl.BlockSpec((BLK, 384), lambda i: (prev(i), 0)),
    ]


def _attn_fwd(pa, tab, sinks):
    S = pa.shape[0]
    nb = TQ // BLK

    def body(sink_ref, q_ref, g_ref, kvc_ref, kvp_ref, tabc_ref, tabp_ref, o_ref, kall, q_sc, p_sc):
        i = pl.program_id(0)
        _fill_kv(kall, kvc_ref, kvp_ref, tabc_ref, tabp_ref)
        lane = lax.broadcasted_iota(jnp.int32, (BLK, 128), 1)
        half = [lane < HEAD_DIM, lane >= HEAD_DIM]
        upper = _upper()
        upper_b = upper.astype(BF16)
        sinks = _sink_rows(sink_ref)
        for j in range(nb):
            rq = slice(j * BLK, (j + 1) * BLK)
            rk = slice(j * BLK, (j + 2) * BLK)
            tab = tabc_ref[rq, :]
            qr = [_rope(q_ref[rq, p * 128:(p + 1) * 128].astype(F32), tab) * 0.125 for p in range(4)]
            outs = []
            for s in range(2):
                _stack_heads(q_sc, s, half, qr)
                prev, cur = _scores(kall[s, rk, :], q_sc[s], i == 0 if j == 0 else None)
                prob, _ = _softmax(jnp.where(upper, prev, cur), sinks[s])
                _split_store(p_sc, s, upper_b, prob)
                outs.append(_tn(p_sc[s], kall[2 + s, rk, :]))
            for p in range(4):
                cols = slice(p * 128, (p + 1) * 128)
                o_ref[rq, cols] = (_unstack_pair(half, outs, p) * _silu(g_ref[rq, cols].astype(F32))).astype(BF16)

    return pl.pallas_call(
        body,
        name="attn_fwd",
        grid=(S // TQ,),
        in_specs=_attn_specs(lambda i: i),
        out_specs=pl.BlockSpec((TQ, ATTN_W), lambda i: (i, 0)),
        out_shape=jax.ShapeDtypeStruct((S, ATTN_W), BF16),
        scratch_shapes=[
            pltpu.VMEM((4, BLK + TQ, 128), BF16),
            pltpu.VMEM((2, 4 * BLK, 128), BF16),
            pltpu.VMEM((2, 2 * BLK, 4 * BLK), BF16),
        ],
        compiler_params=_params(("arbitrary",)),
    )(sinks, pa, pa, pa, pa, tab, tab)


def _shift_down(u, halo_ref, has_prev):
    def halo_u(r):
        hu = halo_ref[r:r + 1, 512:1024].astype(F32) * halo_ref[r:r + 1, 1024:1536].astype(F32)
        return jnp.where(has_prev, hu, 0.0)

    row = lax.broadcasted_iota(jnp.int32, u.shape, 0)
    um1 = jnp.where(row == 0, halo_u(HALO - 1), pltpu.roll(u, 1, 0))
    um2 = jnp.where(row == 0, halo_u(HALO - 2), jnp.where(row == 1, halo_u(HALO - 1), pltpu.roll(u, 2, 0)))
    return um1, um2


def _conv_tile(pc_ref, halo_ref, w_ref, has_prev):
    b = pc_ref[:, 0:512].astype(F32)
    c = pc_ref[:, 512:1024].astype(F32)
    hh = pc_ref[:, 1024:1536].astype(F32)
    gc = pc_ref[:, 1536:2048].astype(F32)
    u = c * hh
    um1, um2 = _shift_down(u, halo_ref, has_prev)
    cv = w_ref[0:1, :] * um2 + w_ref[1:2, :] * um1 + w_ref[2:3, :] * u
    return b, c, hh, gc, u, um1, um2, cv


def _prev_rows(width, col=0):
    return pl.BlockSpec((HALO, width), lambda i: (jnp.maximum(i * (TM // HALO) - 1, 0), col))


def _out_loss(x, target, ya, pc, conv_w, w_out, final_g):
    S = x.shape[0]

    def body(x_ref, t_ref, ya_ref, pc_ref, halo_ref, cw_ref, wo_ref, fg_ref,
             dh_ref, dmix_ref, gwo_ref, gfg_ref, loss_ref):
        @pl.when(pl.program_id(0) == 0)
        def _():
            gwo_ref[...] = jnp.zeros_like(gwo_ref)
            gfg_ref[...] = jnp.zeros_like(gfg_ref)
            loss_ref[...] = jnp.zeros_like(loss_ref)

        b, _, _, gc, _, _, _, cv = _conv_tile(pc_ref, halo_ref, cw_ref, pl.program_id(0) > 0)
        yc = (b * cv * _silu(gc)).astype(BF16)
        mix = jnp.concatenate([ya_ref[...], yc], axis=1)
        wo = wo_ref[...]
        fg = fg_ref[...]
        h = x_ref[...] + _nn(mix, wo)
        r = lax.rsqrt(jnp.mean(h * h, axis=-1, keepdims=True) + EPS)
        n = h * r
        err = n * fg - t_ref[...]
        loss_ref[...] += 0.5 * jnp.sum(jnp.mean(err * err, axis=-1, keepdims=True), axis=0, keepdims=True)
        dy = err * (1.0 / D_MODEL)
        gfg_ref[...] += jnp.sum(dy * n, axis=0, keepdims=True)
        dyg = dy * fg
        dh = r * (dyg - n * jnp.mean(dyg * n, axis=-1, keepdims=True))
        dh_ref[...] = dh
        dhb = dh.astype(BF16)
        dmix_ref[...] = _nt(dhb, wo).astype(ACT)
        gwo_ref[...] += _tn(mix, dhb)

    row = lambda i: (i, 0)
    fixed = lambda i: (0, 0)
    return pl.pallas_call(
        body,
        name="out_loss",
        grid=(S // TM,),
        in_specs=[
            pl.BlockSpec((TM, D_MODEL), row),
            pl.BlockSpec((TM, D_MODEL), row),
            pl.BlockSpec((TM, ATTN_W), row),
            pl.BlockSpec((TM, PC_W), row),
            _prev_rows(PC_W),
            pl.BlockSpec((CONV_K, CONV_W), fixed),
            pl.BlockSpec((D_MODEL, D_MODEL), fixed),
            pl.BlockSpec((1, D_MODEL), fixed),
        ],
        out_specs=[
            pl.BlockSpec((TM, D_MODEL), row),
            pl.BlockSpec((TM, D_MODEL), row),
            pl.BlockSpec((D_MODEL, D_MODEL), fixed),
            pl.BlockSpec((1, D_MODEL), fixed),
            pl.BlockSpec((1, 1), fixed),
        ],
        out_shape=[
            jax.ShapeDtypeStruct((S, D_MODEL), F32),
            jax.ShapeDtypeStruct((S, D_MODEL), ACT),
            jax.ShapeDtypeStruct((D_MODEL, D_MODEL), F32),
            jax.ShapeDtypeStruct((1, D_MODEL), F32),
            jax.ShapeDtypeStruct((1, 1), F32),
        ],
        compiler_params=_params(("arbitrary",)),
    )(x, target, ya, pc, pc, conv_w, w_out, final_g)


def _attn_bwd(pa, dmix, tab, sinks):
    S = pa.shape[0]
    nt = S // TQ
    nb = TQ // BLK

    def body(sink_ref, q_ref, g_ref, kvc_ref, kvp_ref, tabc_ref, tabp_ref, dm_ref,
             d_ref, dsink_ref, kall, dkv, carry, q_sc, do_sc, p_sc, ds_sc, dsink_acc):
        step = pl.program_id(0)
        i = nt - 1 - step

        @pl.when(step == 0)
        def _():
            carry[...] = jnp.zeros_like(carry)
            dsink_acc[...] = jnp.zeros_like(dsink_acc)

        _fill_kv(kall, kvc_ref, kvp_ref, tabc_ref, tabp_ref)
        dkv[0:TQ, :] = jnp.zeros((TQ, 2 * KV_W), F32)
        dkv[TQ:TQ + BLK, :] = carry[...]
        lane = lax.broadcasted_iota(jnp.int32, (BLK, 128), 1)
        half = [lane < HEAD_DIM, lane >= HEAD_DIM]
        upper = _upper()
        upper_b = upper.astype(BF16)
        sinks = _sink_rows(sink_ref)
        for j in range(nb):
            rq = slice(j * BLK, (j + 1) * BLK)
            rk = slice(j * BLK, (j + 2) * BLK)
            tab = tabc_ref[rq, :]
            pair = [slice(p * 128, (p + 1) * 128) for p in range(4)]
            qr = [_rope(q_ref[rq, c].astype(F32), tab) * 0.125 for c in pair]
            g = [g_ref[rq, c].astype(F32) for c in pair]
            da = [dm_ref[rq, c].astype(F32) for c in pair]
            do = [da[p] * _silu(g[p]) for p in range(4)]
            outs, dqs, dks, dvs = [], [], [], []
            for s in range(2):
                kk = kall[s, rk, :]
                vv = kall[2 + s, rk, :]
                _stack_heads(q_sc, s, half, qr)
                _stack_heads(do_sc, s, half, do)
                prev, cur = _scores(kk, q_sc[s], i == 0 if j == 0 else None)
                prob, psink = _softmax(jnp.where(upper, prev, cur), sinks[s])
                _split_store(p_sc, s, upper_b, prob)
                dprob = _merge(upper, _nt(vv, do_sc[s]))
                dsum = jnp.sum(dprob * prob, axis=0, keepdims=True)
                _split_store(ds_sc, s, upper_b, prob * (dprob - dsum))
                dsink_acc[s, 0:1, :] += psink * dsum
                outs.append(_tn(p_sc[s], vv))
                dqs.append(_tn(ds_sc[s], kk))
                dks.append(_nn(ds_sc[s], q_sc[s]))
                dvs.append(_nn(p_sc[s], do_sc[s]))
            for p in range(4):
                d_ref[rq, pair[p]] = _rope_t(_unstack_pair(half, dqs, p) * 0.125, tab).astype(BF16)
                d_ref[rq, 512 + p * 128:512 + (p + 1) * 128] = (
                    da[p] * _unstack_pair(half, outs, p) * _dsilu(g[p])).astype(BF16)
            dkv[rk, 0:128] += dks[0] + pltpu.roll(dks[1], 64, 1)
            dkv[rk, 128:256] += dvs[0] + pltpu.roll(dvs[1], 64, 1)
        d_ref[:, 1024:1152] = _rope_t(dkv[BLK:BLK + TQ, 0:128], tabc_ref[...]).astype(BF16)
        d_ref[:, 1152:1280] = dkv[BLK:BLK + TQ, 128:256].astype(BF16)
        carry[...] = dkv[0:BLK, :]

        @pl.when(step == nt - 1)
        def _():
            for s in range(2):
                for a, (p, e) in enumerate(HEADS[s]):
                    h = 2 * p + e
                    tot = jnp.sum(dsink_acc[s, 0:1, a * BLK:(a + 1) * BLK], axis=1, keepdims=True)
                    dsink_ref[h:h + 1, :] = jnp.broadcast_to(-tot, (1, 128))

    rev = lambda s: nt - 1 - s
    return pl.pallas_call(
        body,
        name="attn_bwd",
        grid=(nt,),
        in_specs=_attn_specs(rev) + [pl.BlockSpec((TQ, ATTN_W), lambda s: (nt - 1 - s, 0))],
        out_specs=[
            pl.BlockSpec((TQ, PA_W), lambda s: (nt - 1 - s, 0)),
            pl.BlockSpec((8, 128), lambda s: (0, 0)),
        ],
        out_shape=[
            jax.ShapeDtypeStruct((S, PA_W), BF16),
            jax.ShapeDtypeStruct((8, 128), F32),
        ],
        scratch_shapes=[
            pltpu.VMEM((4, BLK + TQ, 128), BF16),
            pltpu.VMEM((BLK + TQ, 2 * KV_W), F32),
            pltpu.VMEM((BLK, 2 * KV_W), F32),
            pltpu.VMEM((2, 4 * BLK, 128), BF16),
            pltpu.VMEM((2, 4 * BLK, 128), BF16),
            pltpu.VMEM((2, 2 * BLK, 4 * BLK), BF16),
            pltpu.VMEM((2, 2 * BLK, 4 * BLK), BF16),
            pltpu.VMEM((2, 8, 4 * BLK), F32),
        ],
        compiler_params=_params(("arbitrary",)),
    )(sinks, pa, pa, pa, pa, tab, tab, dmix)


def _conv_bwd_tile(pc_ref, prev_ref, next_ref, dm_ref, dmn_ref, w_ref, d_ref, gw_ref, has_prev, has_next):
    rows = pc_ref.shape[0]
    w0, w1, w2 = w_ref[0:1, :], w_ref[1:2, :], w_ref[2:3, :]
    b, c, hh, gc, u, um1, um2, cv = _conv_tile(pc_ref, prev_ref, w_ref, has_prev)
    sg = _silu(gc)
    dy = dm_ref[...].astype(F32)
    dcv = dy * b * sg

    def next_dcv(r):
        nd = (dmn_ref[r:r + 1, :].astype(F32) * next_ref[r:r + 1, 0:512].astype(F32)
              * _silu(next_ref[r:r + 1, 1536:2048].astype(F32)))
        return jnp.where(has_next, nd, 0.0)

    row = lax.broadcasted_iota(jnp.int32, (rows, CONV_W), 0)
    dp1 = jnp.where(row == rows - 1, next_dcv(0), pltpu.roll(dcv, rows - 1, 0))
    dp2 = jnp.where(row == rows - 1, next_dcv(1),
                    jnp.where(row == rows - 2, next_dcv(0), pltpu.roll(dcv, rows - 2, 0)))
    du = w2 * dcv + w1 * dp1 + w0 * dp2
    d_ref[:, 0:512] = (dy * cv * sg).astype(BF16)
    d_ref[:, 512:1024] = (du * hh).astype(BF16)
    d_ref[:, 1024:1536] = (du * c).astype(BF16)
    d_ref[:, 1536:2048] = (dy * b * cv * _dsilu(gc)).astype(BF16)
    gw_ref[0:1, :] += jnp.sum(dcv * um2, axis=0, keepdims=True)
    gw_ref[1:2, :] += jnp.sum(dcv * um1, axis=0, keepdims=True)
    gw_ref[2:3, :] += jnp.sum(dcv * u, axis=0, keepdims=True)


def _grad_x(da, dc, wt, x, dh, norm_g, grads):
    S = x.shape[0]
    n_steps = S // TM
    rs = _ReduceScatter(grads)
    n_rs_out = len(rs.out_shape())

    def body(da_ref, dc_ref, wt_ref, x_ref, dh_ref, g_ref, *rest):
        grad_refs, rest = rest[:rs.n], rest[rs.n:]
        gx_ref, gng_ref = rest[:2]
        rs_out, rs_scratch = rest[2:2 + n_rs_out], rest[2 + n_rs_out:]
        step = pl.program_id(0)
        finish = rs.emit(step, n_steps, grad_refs, rs_out, rs_scratch)

        @pl.when(step == 0)
        def _():
            gng_ref[...] = jnp.zeros_like(gng_ref)

        dxn = (_nn(da_ref[:, 0:512], wt_ref[0:512, :]) + _nn(da_ref[:, 512:1024], wt_ref[768:1280, :])
               + _nn(da_ref[:, 1024:1280], wt_ref[512:768, :]) + _nn(dc_ref[...], wt_ref[1280:3328, :]))
        xv = x_ref[...]
        r = lax.rsqrt(jnp.mean(xv * xv, axis=-1, keepdims=True) + EPS)
        n = xv * r
        gng_ref[...] += jnp.sum(dxn * n, axis=0, keepdims=True)
        dxg = dxn * g_ref[...]
        gx_ref[...] = dh_ref[...] + r * (dxg - n * jnp.mean(dxg * n, axis=-1, keepdims=True))
        finish()

    row = lambda i: (i, 0)
    fixed = lambda i: (0, 0)
    any_spec = pl.BlockSpec(memory_space=pl.ANY)
    outs = pl.pallas_call(
        body,
        name="grad_x_reduce_scatter",
        grid=(n_steps,),
        in_specs=[
            pl.BlockSpec((TM, PA_W), row),
            pl.BlockSpec((TM, PC_W), row),
            pl.BlockSpec((IN_W, D_MODEL), fixed),
            pl.BlockSpec((TM, D_MODEL), row),
            pl.BlockSpec((TM, D_MODEL), row),
            pl.BlockSpec((1, D_MODEL), fixed),
        ] + [any_spec] * rs.n,
        out_specs=[pl.BlockSpec((TM, D_MODEL), row), pl.BlockSpec((1, D_MODEL), fixed)] + [any_spec] * n_rs_out,
        out_shape=[jax.ShapeDtypeStruct((S, D_MODEL), F32), jax.ShapeDtypeStruct((1, D_MODEL), F32)] + rs.out_shape(),
        scratch_shapes=rs.scratch_shapes(),
        compiler_params=_params(("arbitrary",)),
    )(da, dc, wt, x, dh, norm_g, *grads)
    return outs[0], outs[1], outs[2:2 + rs.n], outs[2 + rs.n:2 + 2 * rs.n]


def _grad_w_in(da, pc, dmix, conv_w, xn):
    S = xn.shape[0]
    nt = S // TM
    t16 = TM // HALO

    def body(da_ref, pc_ref, prev_ref, next_ref, dm_ref, dmn_ref, cw_ref, xn_ref, gw_ref, dc_ref, gcw_ref):
        i = pl.program_id(0)

        @pl.when(i == 0)
        def _():
            gw_ref[...] = jnp.zeros_like(gw_ref)
            gcw_ref[...] = jnp.zeros_like(gcw_ref)

        _conv_bwd_tile(pc_ref, prev_ref, next_ref, dm_ref, dmn_ref, cw_ref, dc_ref, gcw_ref, i > 0, i < nt - 1)
        xn = xn_ref[...]
        gw_ref[0:512, :] += _tn(da_ref[:, 0:512], xn)
        gw_ref[768:1280, :] += _tn(da_ref[:, 512:1024], xn)
        gw_ref[512:768, :] += _tn(da_ref[:, 1024:1280], xn)
        gw_ref[1280:3328, :] += _tn(dc_ref[...], xn)

    row = lambda i: (i, 0)
    fixed = lambda i: (0, 0)
    nxt = lambda i: jnp.minimum((i + 1) * t16, nt * t16 - 1)
    return pl.pallas_call(
        body,
        name="grad_w_in",
        grid=(nt,),
        in_specs=[
            pl.BlockSpec((TM, PA_W), row),
            pl.BlockSpec((TM, PC_W), row),
            _prev_rows(PC_W),
            pl.BlockSpec((HALO, PC_W), lambda i: (nxt(i), 0)),
            pl.BlockSpec((TM, CONV_W), lambda i: (i, 1)),
            pl.BlockSpec((HALO, CONV_W), lambda i: (nxt(i), 1)),
            pl.BlockSpec((CONV_K, CONV_W), fixed),
            pl.BlockSpec((TM, D_MODEL), row),
        ],
        out_specs=[
            pl.BlockSpec((IN_W, D_MODEL), fixed),
            pl.BlockSpec((TM, PC_W), row),
            pl.BlockSpec((CONV_K, CONV_W), fixed),
        ],
        out_shape=[
            jax.ShapeDtypeStruct((IN_W, D_MODEL), F32),
            jax.ShapeDtypeStruct((S, PC_W), BF16),
            jax.ShapeDtypeStruct((CONV_K, CONV_W), F32),
        ],
        compiler_params=_params(("arbitrary",)),
    )(da, pc, pc, pc, dmix, dmix, conv_w, xn)


def _sum_chips(own, others, name):
    def body(own_ref, p_ref, o_ref):
        acc = own_ref[...]
        for k in range(N_CHIP - 1):
            acc = acc + p_ref[k].astype(F32)
        o_ref[...] = acc

    return pl.pallas_call(
        body,
        name=name,
        out_shape=jax.ShapeDtypeStruct(own.shape, F32),
        compiler_params=_params(),
    )(own, others)


def _sum_parts(parts, name):
    n = parts.shape[0]

    def body(p_ref, o_ref):
        acc = p_ref[0]
        for k in range(1, n):
            acc = acc + p_ref[k]
        o_ref[...] = acc

    return pl.pallas_call(
        body,
        name=name,
        out_shape=jax.ShapeDtypeStruct(parts.shape[1:], F32),
        compiler_params=_params(),
    )(parts)


def _adamw(w, g, m, v, name):
    c1 = 1.0 - ADAM_B1 ** ADAM_STEP
    c2 = 1.0 - ADAM_B2 ** ADAM_STEP

    def body(w_ref, g_ref, m_ref, v_ref, d_ref, nm_ref, nv_ref):
        g = g_ref[...]
        nm = ADAM_B1 * m_ref[...] + (1.0 - ADAM_B1) * g
        nv = ADAM_B2 * v_ref[...] + (1.0 - ADAM_B2) * (g * g)
        nm_ref[...] = nm
        nv_ref[...] = nv
        d_ref[...] = -ADAM_LR * ((nm / c1) / (jnp.sqrt(nv / c2) + ADAM_EPS) + ADAM_WD * w_ref[...])

    shape = jax.ShapeDtypeStruct(w.shape, F32)
    return pl.pallas_call(
        body,
        name=name,
        out_shape=[shape, shape, shape],
        compiler_params=_params(),
    )(w, g, m, v)


def _rope_table(S):
    half = ROT_DIM // 2
    pos = jnp.arange(S, dtype=jnp.int32).astype(F32)
    inv_freq = ROPE_THETA ** (-jnp.arange(0, ROT_DIM, 2, dtype=F32) / ROT_DIM)
    ang = inv_freq[:, None] * pos[None, :]
    cs = jnp.concatenate([jnp.cos(ang), jnp.sin(ang)], axis=0)

    def body(cs_ref, o_ref):
        xt = jnp.concatenate([cs_ref[...], jnp.zeros((128 - 2 * half, TQ), F32)], axis=0).T
        r = lax.broadcasted_iota(jnp.int32, (TQ, 128), 1) & (HEAD_DIM - 1)
        first = lax.broadcasted_iota(jnp.int32, (TQ, 128), 1) < HEAD_DIM

        def at(shift_first, shift_second):
            return jnp.where(first, pltpu.roll(xt, shift_first, 1) if shift_first else xt,
                             pltpu.roll(xt, shift_second, 1))

        cos_lo, cos_hi = at(0, HEAD_DIM), at(half, HEAD_DIM + half)
        sin_lo, sin_hi = at(128 - half, HEAD_DIM - half), at(0, HEAD_DIM)
        o_ref[:, 0:128] = jnp.where(r < half, cos_lo, jnp.where(r < ROT_DIM, cos_hi, 1.0))
        o_ref[:, 128:256] = jnp.where(r < half, -sin_lo, 0.0)
        o_ref[:, 256:384] = jnp.where((r >= half) & (r < ROT_DIM), sin_hi, 0.0)

    return pl.pallas_call(
        body,
        name="rope_table",
        grid=(S // TQ,),
        in_specs=[pl.BlockSpec((2 * half, TQ), lambda i: (0, i))],
        out_specs=pl.BlockSpec((TQ, 384), lambda i: (i, 0)),
        out_shape=jax.ShapeDtypeStruct((S, 384), F32),
        compiler_params=_params(("arbitrary",)),
    )(cs)


def kernel(x, norm_g, w_in, sinks, conv_w, w_out, final_g, loss_target, m_norm_g, m_w_in, m_sinks, m_conv_w, m_w_out, m_final_g, v_norm_g, v_w_in, v_sinks, v_conv_w, v_w_out, v_final_g):
    S = x.shape[1]
    me = 4 * lax.axis_index("x") + 2 * lax.axis_index("y") + lax.axis_index("c")
    x2 = x.reshape(S, D_MODEL)
    t2 = loss_target.reshape(S, D_MODEL)
    ng = norm_g.reshape(1, D_MODEL)
    fg = final_g.reshape(1, D_MODEL)

    cw_pad = jnp.zeros((8, 128), F32).at[0:CONV_K, 0:64].set(conv_w)
    (wt,) = _all_gather([w_in.T.astype(BF16)], "all_gather_w_in")

    tab = _rope_table(S)
    xn, pa, pc, (wo, cw_all) = _fwd_proj(x2, ng, wt, [w_out.astype(BF16), cw_pad])
    cw = cw_all.reshape(N_DEV, 8, 128)[:, 0:CONV_K, 0:64].transpose(1, 0, 2).reshape(CONV_K, CONV_W)
    ya = _attn_fwd(pa, tab, sinks)
    dh, dmix, g_wo, g_fg, loss_part = _out_loss(x2, t2, ya, pc, cw, wo, fg)
    da, g_sinks = _attn_bwd(pa, dmix, tab, sinks)
    g_wt, dc, g_cw = _grad_w_in(da, pc, dmix, cw, xn)
    grad_x, g_ng, own, others = _grad_x(
        da, dc, wt, x2, dh, ng,
        [g_wt.reshape(N_DEV, SHARD_IN, D_MODEL), g_wo.reshape(N_DEV, SHARD_OUT, D_MODEL)])
    grad_w_in = _sum_chips(own[0], others[0], "sum_grad_w_in").T
    grad_w_out = _sum_chips(own[1], others[1], "sum_grad_w_out")
    small = jnp.concatenate([
        g_ng.reshape(8, 128), g_fg.reshape(8, 128), g_sinks,
        g_cw.reshape(12, 128), jnp.broadcast_to(loss_part, (4, 128))], axis=0)
    (small_all,) = _all_gather([small], "all_gather_small_grads")
    small_sum = _sum_parts(small_all.reshape(N_DEV, SMALL_ROWS, 128), "sum_small_grads")
    grad_norm_g = small_sum[0:8].reshape(D_MODEL)
    grad_final_g = small_sum[8:16].reshape(D_MODEL)
    grad_sinks = small_sum[16:24, 0]
    grad_conv_w = lax.dynamic_slice(small_sum[24:36].reshape(CONV_K, CONV_W), (0, me * 64), (CONV_K, 64))
    loss = small_sum[36, 0]

    d_w_in, nm_w_in, nv_w_in = _adamw(w_in, grad_w_in, m_w_in, v_w_in, "adamw_w_in")
    d_w_out, nm_w_out, nv_w_out = _adamw(w_out, grad_w_out, m_w_out, v_w_out, "adamw_w_out")
    def pack(a, b, c_, d):
        return jnp.concatenate([
            a.reshape(8, 128), b.reshape(8, 128),
            jnp.zeros((8, 128), F32).at[0, 0:8].set(c_).at[1:1 + CONV_K, 0:64].set(d)], axis=1)

    d_s, nm_s, nv_s = _adamw(
        pack(norm_g, final_g, sinks, conv_w), pack(grad_norm_g, grad_final_g, grad_sinks, grad_conv_w),
        pack(m_norm_g, m_final_g, m_sinks, m_conv_w),
        pack(v_norm_g, v_final_g, v_sinks, v_conv_w),
        "adamw_small")

    def unpack(p):
        return (p[:, 0:128].reshape(D_MODEL), p[:, 128:256].reshape(D_MODEL), p[0, 256:264], p[1:1 + CONV_K, 256:320])

    d_ng, d_fg, d_sk, d_cw = unpack(d_s)
    nm_ng, nm_fg, nm_sk, nm_cw = unpack(nm_s)
    nv_ng, nv_fg, nv_sk, nv_cw = unpack(nv_s)

    return (loss, grad_x.reshape(1, S, D_MODEL), grad_norm_g, grad_w_in, grad_sinks, grad_conv_w, grad_w_out, grad_final_g,
            d_ng, d_w_in, d_sk, d_cw, d_w_out, d_fg,
            nm_ng, nm_w_in, nm_sk, nm_cw, nm_w_out, nm_fg,
            nv_ng, nv_w_in, nv_sk, nv_cw, nv_w_out, nv_fg)
```

```python
import numpy as np
import jax
import jax.numpy as jnp
from jax import lax
from jax.experimental import pallas as pl
from jax.experimental.pallas import tpu as pltpu

F32 = jnp.float32
BF16 = jnp.bfloat16
MESH = pl.DeviceIdType.MESH

D_MODEL = 1024
HEAD_DIM = 64
N_Q_HEADS = 8
GROUP = 4
ATTN_W = 512
KV_W = 128
BLK = 128
CONV_W = 512
CONV_K = 3
IN_W = 3328
PA_W = 1280
PC_W = 2048
EPS = 1e-5
ROPE_THETA = 500000.0
ROT_DIM = 16
N_DEV = 8
N_CHIP = 4
SHARD_IN = IN_W // N_DEV
SHARD_OUT = D_MODEL // N_DEV
SMALL_ROWS = 40

ADAM_LR = 0.001
ADAM_B1 = 0.9
ADAM_B2 = 0.999
ADAM_EPS = 1e-08
ADAM_WD = 0.01
ADAM_STEP = 10

ACT = jnp.bfloat16

TM = 512
TQ = 512
TC = 512
HALO = 16
VMEM_LIMIT = 56 * 1024 * 1024

NT_DIMS = (((1,), (1,)), ((), ()))
TN_DIMS = (((0,), (0,)), ((), ()))


def _params(sem=None):
    kw = dict(vmem_limit_bytes=VMEM_LIMIT)
    if sem is not None:
        kw["dimension_semantics"] = sem
    return pltpu.CompilerParams(**kw)


def _nt(a, b):
    return lax.dot_general(a, b, NT_DIMS, preferred_element_type=F32)


def _tn(a, b):
    return lax.dot_general(a, b, TN_DIMS, preferred_element_type=F32)


def _nn(a, b):
    return jnp.dot(a, b, preferred_element_type=F32)


def _silu(g):
    return g * jax.nn.sigmoid(g)


def _dsilu(g):
    s = jax.nn.sigmoid(g)
    return s * (1.0 + g * (1.0 - s))


def _all_gather(arrs, name):
    n_arr = len(arrs)

    def body(*refs):
        x_refs = refs[:n_arr]
        out_refs = refs[n_arr:2 * n_arr]
        send_sems, recv_sems, local_sems = refs[2 * n_arr:]
        x, y, c = lax.axis_index("x"), lax.axis_index("y"), lax.axis_index("c")
        me, sibling = (x, y, c), (x, y, 1 - c)
        chips = [(1 - x, y), (x, 1 - y), (1 - x, 1 - y)]

        def rows(a, px, py, pc):
            m = x_refs[a].shape[0]
            return out_refs[a].at[pl.ds((4 * px + 2 * py + pc) * m, m), :]

        def copy(a, k, block, to, src=None):
            return pltpu.make_async_remote_copy(
                src_ref=rows(a, *block) if src is None else src,
                dst_ref=rows(a, *block),
                send_sem=send_sems.at[a * 7 + k],
                recv_sem=recv_sems.at[a * 7 + k],
                device_id=to,
                device_id_type=MESH,
            )

        mine = [pltpu.make_async_copy(x_refs[a], rows(a, *me), local_sems.at[a]) for a in range(n_arr)]
        for cp in mine:
            cp.start()
        first = []
        for a in range(n_arr):
            first.append(copy(a, 0, me, sibling, src=x_refs[a]))
            first += [copy(a, 1 + j, me, (*chip, c), src=x_refs[a]) for j, chip in enumerate(chips)]
        for cp in first:
            cp.start()
        passed = []
        for j, chip in enumerate(chips):
            for a in range(n_arr):
                copy(a, 1 + j, (*chip, c), me).wait_recv()
                fwd = copy(a, 4 + j, (*chip, c), sibling)
                fwd.start()
                passed.append(fwd)
        for a in range(n_arr):
            copy(a, 0, sibling, me).wait_recv()
            for j, chip in enumerate(chips):
                copy(a, 4 + j, (*chip, 1 - c), me).wait_recv()
        for cp in first + passed:
            cp.wait_send()
        for cp in mine:
            cp.wait()

    vmem = pl.BlockSpec(memory_space=pltpu.VMEM)
    return pl.pallas_call(
        body,
        name=name,
        out_shape=[jax.ShapeDtypeStruct((N_DEV * a.shape[0], a.shape[1]), a.dtype) for a in arrs],
        in_specs=[vmem] * n_arr,
        out_specs=[vmem] * n_arr,
        scratch_shapes=[
            pltpu.SemaphoreType.DMA((7 * n_arr,)),
            pltpu.SemaphoreType.DMA((7 * n_arr,)),
            pltpu.SemaphoreType.DMA((n_arr,)),
        ],
        compiler_params=_params(),
    )(*arrs)


class _AllGatherInSteps:
    forward_step = 3

    def __init__(self, arrs):
        self.blocks = [(a.shape, a.dtype) for a in arrs]
        self.n = len(arrs)

    def out_shape(self):
        return [jax.ShapeDtypeStruct((N_DEV * s[0], s[1]), d) for s, d in self.blocks]

    def scratch_shapes(self):
        return [pltpu.SemaphoreType.DMA((7 * self.n,)), pltpu.SemaphoreType.DMA((7 * self.n,)),
                pltpu.SemaphoreType.DMA((self.n,))]

    def emit(self, step, n_steps, x_refs, out_refs, scratch):
        assert n_steps > self.forward_step + 1
        send_sems, recv_sems, local_sems = scratch
        x, y, c = lax.axis_index("x"), lax.axis_index("y"), lax.axis_index("c")
        me, sibling = (x, y, c), (x, y, 1 - c)
        chips = [(1 - x, y), (x, 1 - y), (1 - x, 1 - y)]

        def rows(a, px, py, pc):
            m = self.blocks[a][0][0]
            return out_refs[a].at[pl.ds((4 * px + 2 * py + pc) * m, m), :]

        def copy(a, k, block, to, src=None):
            return pltpu.make_async_remote_copy(
                src_ref=rows(a, *block) if src is None else src, dst_ref=rows(a, *block),
                send_sem=send_sems.at[a * 7 + k], recv_sem=recv_sems.at[a * 7 + k],
                device_id=to, device_id_type=MESH)

        def mine(a):
            return pltpu.make_async_copy(x_refs[a], rows(a, *me), local_sems.at[a])

        def first(a):
            return ([copy(a, 0, me, sibling, src=x_refs[a])]
                    + [copy(a, 1 + j, me, (*chip, c), src=x_refs[a]) for j, chip in enumerate(chips)])

        def passed(a):
            return [copy(a, 4 + j, (*chip, c), sibling) for j, chip in enumerate(chips)]

        @pl.when(step == 0)
        def _():
            for a in range(self.n):
                mine(a).start()
                for cp in first(a):
                    cp.start()

        @pl.when(step == self.forward_step)
        def _():
            for j, chip in enumerate(chips):
                for a in range(self.n):
                    copy(a, 1 + j, (*chip, c), me).wait_recv()
                    copy(a, 4 + j, (*chip, c), sibling).start()

        def finish():
            @pl.when(step == n_steps - 1)
            def _():
                for a in range(self.n):
                    copy(a, 0, sibling, me).wait_recv()
                    for j, chip in enumerate(chips):
                        copy(a, 4 + j, (*chip, 1 - c), me).wait_recv()
                    for cp in first(a) + passed(a):
                        cp.wait_send()
                    mine(a).wait()

        return finish


class _ReduceScatter:
    def __init__(self, grads):
        self.shapes = [g.shape[1:] for g in grads]
        self.n = len(grads)
        self.items = tuple((a, r) for r in (1, 2, 3, 0) for a in range(self.n))
        self.steps = len(self.items) + 2

    def out_shape(self):
        own = [jax.ShapeDtypeStruct(s, F32) for s in self.shapes]
        ici = [jax.ShapeDtypeStruct((N_CHIP - 1,) + s, BF16) for s in self.shapes]
        land = [jax.ShapeDtypeStruct((N_CHIP,) + s, F32) for s in self.shapes]
        return own + ici + land

    def scratch_shapes(self):
        n_items = len(self.items)
        return ([pltpu.VMEM((2,) + s, F32) for s in self.shapes]
                + [pltpu.VMEM((N_CHIP - 1,) + s, BF16) for s in self.shapes]
                + [pltpu.VMEM(s, F32) for s in self.shapes]
                + [pltpu.SemaphoreType.DMA((self.n * N_CHIP,))] * 2
                + [pltpu.SemaphoreType.DMA((2 * n_items,))]
                + [pltpu.SemaphoreType.DMA((self.n * (N_CHIP - 1),))] * 2
                + [pltpu.SemaphoreType.DMA((self.n,))])

    def emit(self, step, n_steps, g_refs, out_refs, scratch):
        assert n_steps > self.steps
        n = self.n
        own_refs, ici_refs, land_refs = out_refs[:n], out_refs[n:2 * n], out_refs[2 * n:]
        stage, pair_bf, pair_own = scratch[:n], scratch[n:2 * n], scratch[2 * n:3 * n]
        sib_send, sib_recv, load_sems, ici_send, ici_recv, own_sems = scratch[3 * n:]
        x, y, c = lax.axis_index("x"), lax.axis_index("y"), lax.axis_index("c")

        def chip_of(r):
            return (x ^ (r >> 1), y ^ (r & 1))

        def block_of(r, core):
            cx, cy = chip_of(r)
            return 4 * cx + 2 * cy + core

        def to_sibling(a, r):
            return pltpu.make_async_remote_copy(
                src_ref=g_refs[a].at[block_of(r, 1 - c)], dst_ref=land_refs[a].at[r],
                send_sem=sib_send.at[a * N_CHIP + r], recv_sem=sib_recv.at[a * N_CHIP + r],
                device_id=(x, y, 1 - c), device_id_type=MESH)

        def loads(k):
            a, r = self.items[k]
            return (pltpu.make_async_copy(g_refs[a].at[block_of(r, c)], stage[a].at[0], load_sems.at[2 * k]),
                    pltpu.make_async_copy(land_refs[a].at[r], stage[a].at[1], load_sems.at[2 * k + 1]))

        def to_owner(k):
            a, r = self.items[k]
            if r == 0:
                return pltpu.make_async_copy(pair_own[a], own_refs[a], own_sems.at[a])
            return pltpu.make_async_remote_copy(
                src_ref=pair_bf[a].at[r - 1], dst_ref=ici_refs[a].at[r - 1],
                send_sem=ici_send.at[a * (N_CHIP - 1) + r - 1], recv_sem=ici_recv.at[a * (N_CHIP - 1) + r - 1],
                device_id=(*chip_of(r), c), device_id_type=MESH)

        @pl.when(step == 0)
        def _():
            for a, r in self.items:
                to_sibling(a, r).start()

        for k, (a, r) in enumerate(self.items):
            @pl.when(step == 1 + k)
            def _(k=k, a=a, r=r):
                to_sibling(a, r).wait_recv()
                for cp in loads(k):
                    cp.start()

            @pl.when(step == 2 + k)
            def _(k=k, a=a, r=r):
                for cp in loads(k):
                    cp.wait()
                total = stage[a][0] + stage[a][1]
                if r == 0:
                    pair_own[a][...] = total
                else:
                    pair_bf[a][r - 1] = total.astype(BF16)
                to_owner(k).start()

        def finish():
            @pl.when(step == n_steps - 1)
            def _():
                for k, (a, r) in enumerate(self.items):
                    if r == 0:
                        to_owner(k).wait()
                    else:
                        to_owner(k).wait_send()
                        to_owner(k).wait_recv()
                for a, r in self.items:
                    to_sibling(a, r).wait_send()

        return finish


def _fwd_proj(x, norm_g, wt, later):
    S = x.shape[0]

    n_steps = S // TM
    ag = _AllGatherInSteps(later)

    def body(x_ref, g_ref, wt_ref, *rest):
        later_refs, rest = rest[:ag.n], rest[ag.n:]
        xn_ref, pa_ref, pc_ref = rest[:3]
        gathered, ag_scratch = rest[3:3 + ag.n], rest[3 + ag.n:]
        step = pl.program_id(0)
        finish = ag.emit(step, n_steps, later_refs, gathered, ag_scratch)
        xv = x_ref[...]
        r = lax.rsqrt(jnp.mean(xv * xv, axis=-1, keepdims=True) + EPS)
        xn = (xv * r * g_ref[...]).astype(BF16)
        xn_ref[...] = xn
        pa_ref[:, 0:512] = _nt(xn, wt_ref[0:512, :]).astype(ACT)
        pa_ref[:, 512:1024] = _nt(xn, wt_ref[768:1280, :]).astype(ACT)
        pa_ref[:, 1024:1280] = _nt(xn, wt_ref[512:768, :]).astype(ACT)
        pc_ref[...] = _nt(xn, wt_ref[1280:3328, :]).astype(ACT)
        finish()

    any_spec = pl.BlockSpec(memory_space=pl.ANY)
    outs = pl.pallas_call(
        body,
        name="fwd_proj_all_gather",
        grid=(n_steps,),
        in_specs=[
            pl.BlockSpec((TM, D_MODEL), lambda i: (i, 0)),
            pl.BlockSpec((1, D_MODEL), lambda i: (0, 0)),
            pl.BlockSpec((IN_W, D_MODEL), lambda i: (0, 0)),
        ] + [any_spec] * ag.n,
        out_specs=[
            pl.BlockSpec((TM, D_MODEL), lambda i: (i, 0)),
            pl.BlockSpec((TM, PA_W), lambda i: (i, 0)),
            pl.BlockSpec((TM, PC_W), lambda i: (i, 0)),
        ] + [any_spec] * ag.n,
        out_shape=[
            jax.ShapeDtypeStruct((S, D_MODEL), BF16),
            jax.ShapeDtypeStruct((S, PA_W), ACT),
            jax.ShapeDtypeStruct((S, PC_W), ACT),
        ] + ag.out_shape(),
        scratch_shapes=ag.scratch_shapes(),
        compiler_params=_params(("arbitrary",)),
    )(x, norm_g, wt, *later)
    return outs[0], outs[1], outs[2], outs[3:]


def _rope(t, tab):
    return (t * tab[:, 0:128] + pltpu.roll(t, 120, 1) * tab[:, 128:256]
            + pltpu.roll(t, 8, 1) * tab[:, 256:384])


def _rope_t(d, tab):
    return (d * tab[:, 0:128] + pltpu.roll(d * tab[:, 128:256], 8, 1)
            + pltpu.roll(d * tab[:, 256:384], 120, 1))


def _fill_kv(kall, kvc_ref, kvp_ref, tabc_ref, tabp_ref):
    for lo, kv_ref, tab_ref, n in ((0, kvp_ref, tabp_ref, BLK), (BLK, kvc_ref, tabc_ref, TQ)):
        k = _rope(kv_ref[:, 0:128].astype(F32), tab_ref[...])
        v = kv_ref[:, 128:256].astype(F32)
        kall[0, lo:lo + n, :] = k.astype(BF16)
        kall[1, lo:lo + n, :] = pltpu.roll(k, 64, 1).astype(BF16)
        kall[2, lo:lo + n, :] = v.astype(BF16)
        kall[3, lo:lo + n, :] = pltpu.roll(v, 64, 1).astype(BF16)


HEADS = (((0, 0), (1, 0), (2, 1), (3, 1)), ((0, 1), (1, 1), (2, 0), (3, 0)))


def _upper():
    kj = lax.broadcasted_iota(jnp.int32, (BLK, 4 * BLK), 0)
    qi = lax.broadcasted_iota(jnp.int32, (BLK, 4 * BLK), 1) & (BLK - 1)
    return kj > qi


def _merge(upper, both):
    return jnp.where(upper, both[0:BLK, :], both[BLK:2 * BLK, :])


def _split_store(ref, s, upper_b, val):
    vb = val.astype(BF16)
    first = vb * upper_b
    ref[s, 0:BLK, :] = first
    ref[s, BLK:2 * BLK, :] = vb - first


def _sink_rows(sink_ref):
    return [jnp.concatenate([jnp.full((1, BLK), sink_ref[2 * p + e], F32) for p, e in HEADS[s]], axis=1)
            for s in range(2)]


def _stack_heads(ref, s, half, pairs):
    for a, (p, e) in enumerate(HEADS[s]):
        ref[s, a * BLK:(a + 1) * BLK, :] = jnp.where(half[e], pairs[p], 0.0).astype(BF16)


def _unstack_pair(half, outs, p):
    lo = 0 if p < 2 else 1
    rows = slice(p * BLK, (p + 1) * BLK)
    return jnp.where(half[0], outs[lo][rows, :], outs[1 - lo][rows, :])


def _softmax(sm, sinks):
    m = jnp.maximum(jnp.max(sm, axis=0, keepdims=True), sinks)
    p = jnp.exp(sm - m)
    es = jnp.exp(sinks - m)
    inv = 1.0 / (jnp.sum(p, axis=0, keepdims=True) + es)
    return p * inv, es * inv


def _scores(kk, q_stack, first):
    st = _nt(kk, q_stack)
    prev = st[0:BLK, :]
    if first is not None:
        prev = prev + jnp.where(first, -jnp.inf, 0.0)
    return prev, st[BLK:2 * BLK, :]


def _attn_specs(tile):
    nb = TQ // BLK
    prev = lambda i: jnp.maximum(tile(i) * nb - 1, 0)
    return [
        pl.BlockSpec(memory_space=pltpu.SMEM),
        pl.BlockSpec((TQ, ATTN_W), lambda i: (tile(i), 0)),
        pl.BlockSpec((TQ, ATTN_W), lambda i: (tile(i), 1)),
        pl.BlockSpec((TQ, 2 * KV_W), lambda i: (tile(i), 4)),
        pl.BlockSpec((BLK, 2 * KV_W), lambda i: (prev(i), 4)),
        pl.BlockSpec((TQ, 384), lambda i: (tile(i), 0)),
        pl.BlockSpec((BLK, 384), lambda i: (prev(i), 0)),
    ]


def _attn_fwd(pa, tab, sinks):
    S = pa.shape[0]
    nb = TQ // BLK

    def body(sink_ref, q_ref, g_ref, kvc_ref, kvp_ref, tabc_ref, tabp_ref, o_ref, kall, q_sc, p_sc):
        i = pl.program_id(0)
        _fill_kv(kall, kvc_ref, kvp_ref, tabc_ref, tabp_ref)
        lane = lax.broadcasted_iota(jnp.int32, (BLK, 128), 1)
        half = [lane < HEAD_DIM, lane >= HEAD_DIM]
        upper = _upper()
        upper_b = upper.astype(BF16)
        sinks = _sink_rows(sink_ref)
        for j in range(nb):
            rq = slice(j * BLK, (j + 1) * BLK)
            rk = slice(j * BLK, (j + 2) * BLK)
            tab = tabc_ref[rq, :]
            qr = [_rope(q_ref[rq, p * 128:(p + 1) * 128].astype(F32), tab) * 0.125 for p in range(4)]
            outs = []
            for s in range(2):
                _stack_heads(q_sc, s, half, qr)
                prev, cur = _scores(kall[s, rk, :], q_sc[s], i == 0 if j == 0 else None)
                prob, _ = _softmax(jnp.where(upper, prev, cur), sinks[s])
                _split_store(p_sc, s, upper_b, prob)
                outs.append(_tn(p_sc[s], kall[2 + s, rk, :]))
            for p in range(4):
                cols = slice(p * 128, (p + 1) * 128)
                o_ref[rq, cols] = (_unstack_pair(half, outs, p) * _silu(g_ref[rq, cols].astype(F32))).astype(BF16)

    return pl.pallas_call(
        body,
        name="attn_fwd",
        grid=(S // TQ,),
        in_specs=_attn_specs(lambda i: i),
        out_specs=pl.BlockSpec((TQ, ATTN_W), lambda i: (i, 0)),
        out_shape=jax.ShapeDtypeStruct((S, ATTN_W), BF16),
        scratch_shapes=[
            pltpu.VMEM((4, BLK + TQ, 128), BF16),
            pltpu.VMEM((2, 4 * BLK, 128), BF16),
            pltpu.VMEM((2, 2 * BLK, 4 * BLK), BF16),
        ],
        compiler_params=_params(("arbitrary",)),
    )(sinks, pa, pa, pa, pa, tab, tab)


def _shift_down(u, halo_ref, has_prev):
    def halo_u(r):
        hu = halo_ref[r:r + 1, 512:1024].astype(F32) * halo_ref[r:r + 1, 1024:1536].astype(F32)
        return jnp.where(has_prev, hu, 0.0)

    row = lax.broadcasted_iota(jnp.int32, u.shape, 0)
    um1 = jnp.where(row == 0, halo_u(HALO - 1), pltpu.roll(u, 1, 0))
    um2 = jnp.where(row == 0, halo_u(HALO - 2), jnp.where(row == 1, halo_u(HALO - 1), pltpu.roll(u, 2, 0)))
    return um1, um2


def _conv_tile(pc_ref, halo_ref, w_ref, has_prev):
    b = pc_ref[:, 0:512].astype(F32)
    c = pc_ref[:, 512:1024].astype(F32)
    hh = pc_ref[:, 1024:1536].astype(F32)
    gc = pc_ref[:, 1536:2048].astype(F32)
    u = c * hh
    um1, um2 = _shift_down(u, halo_ref, has_prev)
    cv = w_ref[0:1, :] * um2 + w_ref[1:2, :] * um1 + w_ref[2:3, :] * u
    return b, c, hh, gc, u, um1, um2, cv


def _prev_rows(width, col=0):
    return pl.BlockSpec((HALO, width), lambda i: (jnp.maximum(i * (TM // HALO) - 1, 0), col))


def _out_loss(x, target, ya, pc, conv_w, w_out, final_g):
    S = x.shape[0]

    def body(x_ref, t_ref, ya_ref, pc_ref, halo_ref, cw_ref, wo_ref, fg_ref,
             dh_ref, dmix_ref, gwo_ref, gfg_ref, loss_ref):
        @pl.when(pl.program_id(0) == 0)
        def _():
            gwo_ref[...] = jnp.zeros_like(gwo_ref)
            gfg_ref[...] = jnp.zeros_like(gfg_ref)
            loss_ref[...] = jnp.zeros_like(loss_ref)

        b, _, _, gc, _, _, _, cv = _conv_tile(pc_ref, halo_ref, cw_ref, pl.program_id(0) > 0)
        yc = (b * cv * _silu(gc)).astype(BF16)
        mix = jnp.concatenate([ya_ref[...], yc], axis=1)
        wo = wo_ref[...]
        fg = fg_ref[...]
        h = x_ref[...] + _nn(mix, wo)
        r = lax.rsqrt(jnp.mean(h * h, axis=-1, keepdims=True) + EPS)
        n = h * r
        err = n * fg - t_ref[...]
        loss_ref[...] += 0.5 * jnp.sum(jnp.mean(err * err, axis=-1, keepdims=True), axis=0, keepdims=True)
        dy = err * (1.0 / D_MODEL)
        gfg_ref[...] += jnp.sum(dy * n, axis=0, keepdims=True)
        dyg = dy * fg
        dh = r * (dyg - n * jnp.mean(dyg * n, axis=-1, keepdims=True))
        dh_ref[...] = dh
        dhb = dh.astype(BF16)
        dmix_ref[...] = _nt(dhb, wo).astype(ACT)
        gwo_ref[...] += _tn(mix, dhb)

    row = lambda i: (i, 0)
    fixed = lambda i: (0, 0)
    return pl.pallas_call(
        body,
        name="out_loss",
        grid=(S // TM,),
        in_specs=[
            pl.BlockSpec((TM, D_MODEL), row),
            pl.BlockSpec((TM, D_MODEL), row),
            pl.BlockSpec((TM, ATTN_W), row),
            pl.BlockSpec((TM, PC_W), row),
            _prev_rows(PC_W),
            pl.BlockSpec((CONV_K, CONV_W), fixed),
            pl.BlockSpec((D_MODEL, D_MODEL), fixed),
            pl.BlockSpec((1, D_MODEL), fixed),
        ],
        out_specs=[
            pl.BlockSpec((TM, D_MODEL), row),
            pl.BlockSpec((TM, D_MODEL), row),
            pl.BlockSpec((D_MODEL, D_MODEL), fixed),
            pl.BlockSpec((1, D_MODEL), fixed),
            pl.BlockSpec((1, 1), fixed),
        ],
        out_shape=[
            jax.ShapeDtypeStruct((S, D_MODEL), F32),
            jax.ShapeDtypeStruct((S, D_MODEL), ACT),
            jax.ShapeDtypeStruct((D_MODEL, D_MODEL), F32),
            jax.ShapeDtypeStruct((1, D_MODEL), F32),
            jax.ShapeDtypeStruct((1, 1), F32),
        ],
        compiler_params=_params(("arbitrary",)),
    )(x, target, ya, pc, pc, conv_w, w_out, final_g)


def _attn_bwd(pa, dmix, tab, sinks):
    S = pa.shape[0]
    nt = S // TQ
    nb = TQ // BLK

    def body(sink_ref, q_ref, g_ref, kvc_ref, kvp_ref, tabc_ref, tabp_ref, dm_ref,
             d_ref, dsink_ref, kall, dkv, carry, q_sc, do_sc, p_sc, ds_sc, dsink_acc):
        step = pl.program_id(0)
        i = nt - 1 - step

        @pl.when(step == 0)
        def _():
            carry[...] = jnp.zeros_like(carry)
            dsink_acc[...] = jnp.zeros_like(dsink_acc)

        _fill_kv(kall, kvc_ref, kvp_ref, tabc_ref, tabp_ref)
        dkv[0:TQ, :] = jnp.zeros((TQ, 2 * KV_W), F32)
        dkv[TQ:TQ + BLK, :] = carry[...]
        lane = lax.broadcasted_iota(jnp.int32, (BLK, 128), 1)
        half = [lane < HEAD_DIM, lane >= HEAD_DIM]
        upper = _upper()
        upper_b = upper.astype(BF16)
        sinks = _sink_rows(sink_ref)
        for j in range(nb):
            rq = slice(j * BLK, (j + 1) * BLK)
            rk = slice(j * BLK, (j + 2) * BLK)
            tab = tabc_ref[rq, :]
            pair = [slice(p * 128, (p + 1) * 128) for p in range(4)]
            qr = [_rope(q_ref[rq, c].astype(F32), tab) * 0.125 for c in pair]
            g = [g_ref[rq, c].astype(F32) for c in pair]
            da = [dm_ref[rq, c].astype(F32) for c in pair]
            do = [da[p] * _silu(g[p]) for p in range(4)]
            outs, dqs, dks, dvs = [], [], [], []
            for s in range(2):
                kk = kall[s, rk, :]
                vv = kall[2 + s, rk, :]
                _stack_heads(q_sc, s, half, qr)
                _stack_heads(do_sc, s, half, do)
                prev, cur = _scores(kk, q_sc[s], i == 0 if j == 0 else None)
                prob, psink = _softmax(jnp.where(upper, prev, cur), sinks[s])
                _split_store(p_sc, s, upper_b, prob)
                dprob = _merge(upper, _nt(vv, do_sc[s]))
                dsum = jnp.sum(dprob * prob, axis=0, keepdims=True)
                _split_store(ds_sc, s, upper_b, prob * (dprob - dsum))
                dsink_acc[s, 0:1, :] += psink * dsum
                outs.append(_tn(p_sc[s], vv))
                dqs.append(_tn(ds_sc[s], kk))
                dks.append(_nn(ds_sc[s], q_sc[s]))
                dvs.append(_nn(p_sc[s], do_sc[s]))
            for p in range(4):
                d_ref[rq, pair[p]] = _rope_t(_unstack_pair(half, dqs, p) * 0.125, tab).astype(BF16)
                d_ref[rq, 512 + p * 128:512 + (p + 1) * 128] = (
                    da[p] * _unstack_pair(half, outs, p) * _dsilu(g[p])).astype(BF16)
            dkv[rk, 0:128] += dks[0] + pltpu.roll(dks[1], 64, 1)
            dkv[rk, 128:256] += dvs[0] + pltpu.roll(dvs[1], 64, 1)
        d_ref[:, 1024:1152] = _rope_t(dkv[BLK:BLK + TQ, 0:128], tabc_ref[...]).astype(BF16)
        d_ref[:, 1152:1280] = dkv[BLK:BLK + TQ, 128:256].astype(BF16)
        carry[...] = dkv[0:BLK, :]

        @pl.when(step == nt - 1)
        def _():
            for s in range(2):
                for a, (p, e) in enumerate(HEADS[s]):
                    h = 2 * p + e
                    tot = jnp.sum(dsink_acc[s, 0:1, a * BLK:(a + 1) * BLK], axis=1, keepdims=True)
                    dsink_ref[h:h + 1, :] = jnp.broadcast_to(-tot, (1, 128))

    rev = lambda s: nt - 1 - s
    return pl.pallas_call(
        body,
        name="attn_bwd",
        grid=(nt,),
        in_specs=_attn_specs(rev) + [pl.BlockSpec((TQ, ATTN_W), lambda s: (nt - 1 - s, 0))],
        out_specs=[
            pl.BlockSpec((TQ, PA_W), lambda s: (nt - 1 - s, 0)),
            pl.BlockSpec((8, 128), lambda s: (0, 0)),
        ],
        out_shape=[
            jax.ShapeDtypeStruct((S, PA_W), BF16),
            jax.ShapeDtypeStruct((8, 128), F32),
        ],
        scratch_shapes=[
            pltpu.VMEM((4, BLK + TQ, 128), BF16),
            pltpu.VMEM((BLK + TQ, 2 * KV_W), F32),
            pltpu.VMEM((BLK, 2 * KV_W), F32),
            pltpu.VMEM((2, 4 * BLK, 128), BF16),
            pltpu.VMEM((2, 4 * BLK, 128), BF16),
            pltpu.VMEM((2, 2 * BLK, 4 * BLK), BF16),
            pltpu.VMEM((2, 2 * BLK, 4 * BLK), BF16),
            pltpu.VMEM((2, 8, 4 * BLK), F32),
        ],
        compiler_params=_params(("arbitrary",)),
    )(sinks, pa, pa, pa, pa, tab, tab, dmix)


def _conv_bwd_tile(pc_ref, prev_ref, next_ref, dm_ref, dmn_ref, w_ref, d_ref, gw_ref, has_prev, has_next):
    rows = pc_ref.shape[0]
    w0, w1, w2 = w_ref[0:1, :], w_ref[1:2, :], w_ref[2:3, :]
    b, c, hh, gc, u, um1, um2, cv = _conv_tile(pc_ref, prev_ref, w_ref, has_prev)
    sg = _silu(gc)
    dy = dm_ref[...].astype(F32)
    dcv = dy * b * sg

    def next_dcv(r):
        nd = (dmn_ref[r:r + 1, :].astype(F32) * next_ref[r:r + 1, 0:512].astype(F32)
              * _silu(next_ref[r:r + 1, 1536:2048].astype(F32)))
        return jnp.where(has_next, nd, 0.0)

    row = lax.broadcasted_iota(jnp.int32, (rows, CONV_W), 0)
    dp1 = jnp.where(row == rows - 1, next_dcv(0), pltpu.roll(dcv, rows - 1, 0))
    dp2 = jnp.where(row == rows - 1, next_dcv(1),
                    jnp.where(row == rows - 2, next_dcv(0), pltpu.roll(dcv, rows - 2, 0)))
    du = w2 * dcv + w1 * dp1 + w0 * dp2
    d_ref[:, 0:512] = (dy * cv * sg).astype(BF16)
    d_ref[:, 512:1024] = (du * hh).astype(BF16)
    d_ref[:, 1024:1536] = (du * c).astype(BF16)
    d_ref[:, 1536:2048] = (dy * b * cv * _dsilu(gc)).astype(BF16)
    gw_ref[0:1, :] += jnp.sum(dcv * um2, axis=0, keepdims=True)
    gw_ref[1:2, :] += jnp.sum(dcv * um1, axis=0, keepdims=True)
    gw_ref[2:3, :] += jnp.sum(dcv * u, axis=0, keepdims=True)


def _grad_x(da, dc, wt, x, dh, norm_g, grads):
    S = x.shape[0]
    n_steps = S // TM
    rs = _ReduceScatter(grads)
    n_rs_out = len(rs.out_shape())

    def body(da_ref, dc_ref, wt_ref, x_ref, dh_ref, g_ref, *rest):
        grad_refs, rest = rest[:rs.n], rest[rs.n:]
        gx_ref, gng_ref = rest[:2]
        rs_out, rs_scratch = rest[2:2 + n_rs_out], rest[2 + n_rs_out:]
        step = pl.program_id(0)
        finish = rs.emit(step, n_steps, grad_refs, rs_out, rs_scratch)

        @pl.when(step == 0)
        def _():
            gng_ref[...] = jnp.zeros_like(gng_ref)

        dxn = (_nn(da_ref[:, 0:512], wt_ref[0:512, :]) + _nn(da_ref[:, 512:1024], wt_ref[768:1280, :])
               + _nn(da_ref[:, 1024:1280], wt_ref[512:768, :]) + _nn(dc_ref[...], wt_ref[1280:3328, :]))
        xv = x_ref[...]
        r = lax.rsqrt(jnp.mean(xv * xv, axis=-1, keepdims=True) + EPS)
        n = xv * r
        gng_ref[...] += jnp.sum(dxn * n, axis=0, keepdims=True)
        dxg = dxn * g_ref[...]
        gx_ref[...] = dh_ref[...] + r * (dxg - n * jnp.mean(dxg * n, axis=-1, keepdims=True))
        finish()

    row = lambda i: (i, 0)
    fixed = lambda i: (0, 0)
    any_spec = pl.BlockSpec(memory_space=pl.ANY)
    outs = pl.pallas_call(
        body,
        name="grad_x_reduce_scatter",
        grid=(n_steps,),
        in_specs=[
            pl.BlockSpec((TM, PA_W), row),
            pl.BlockSpec((TM, PC_W), row),
            pl.BlockSpec((IN_W, D_MODEL), fixed),
            pl.BlockSpec((TM, D_MODEL), row),
            pl.BlockSpec((TM, D_MODEL), row),
            pl.BlockSpec((1, D_MODEL), fixed),
        ] + [any_spec] * rs.n,
        out_specs=[pl.BlockSpec((TM, D_MODEL), row), pl.BlockSpec((1, D_MODEL), fixed)] + [any_spec] * n_rs_out,
        out_shape=[jax.ShapeDtypeStruct((S, D_MODEL), F32), jax.ShapeDtypeStruct((1, D_MODEL), F32)] + rs.out_shape(),
        scratch_shapes=rs.scratch_shapes(),
        compiler_params=_params(("arbitrary",)),
    )(da, dc, wt, x, dh, norm_g, *grads)
    return outs[0], outs[1], outs[2:2 + rs.n], outs[2 + rs.n:2 + 2 * rs.n]


def _grad_w_in(da, pc, dmix, conv_w, xn):
    S = xn.shape[0]
    nt = S // TM
    t16 = TM // HALO

    def body(da_ref, pc_ref, prev_ref, next_ref, dm_ref, dmn_ref, cw_ref, xn_ref, gw_ref, dc_ref, gcw_ref):
        i = pl.program_id(0)

        @pl.when(i == 0)
        def _():
            gw_ref[...] = jnp.zeros_like(gw_ref)
            gcw_ref[...] = jnp.zeros_like(gcw_ref)

        xn = xn_ref[...]
        gw_ref[0:512, :] += _tn(da_ref[:, 0:512], xn)
        gw_ref[768:1280, :] += _tn(da_ref[:, 512:1024], xn)
        gw_ref[512:768, :] += _tn(da_ref[:, 1024:1280], xn)
        _conv_bwd_tile(pc_ref, prev_ref, next_ref, dm_ref, dmn_ref, cw_ref, dc_ref, gcw_ref, i > 0, i < nt - 1)
        gw_ref[1280:3328, :] += _tn(dc_ref[...], xn)

    row = lambda i: (i, 0)
    fixed = lambda i: (0, 0)
    nxt = lambda i: jnp.minimum((i + 1) * t16, nt * t16 - 1)
    return pl.pallas_call(
        body,
        name="grad_w_in",
        grid=(nt,),
        in_specs=[
            pl.BlockSpec((TM, PA_W), row),
            pl.BlockSpec((TM, PC_W), row),
            _prev_rows(PC_W),
            pl.BlockSpec((HALO, PC_W), lambda i: (nxt(i), 0)),
            pl.BlockSpec((TM, CONV_W), lambda i: (i, 1)),
            pl.BlockSpec((HALO, CONV_W), lambda i: (nxt(i), 1)),
            pl.BlockSpec((CONV_K, CONV_W), fixed),
            pl.BlockSpec((TM, D_MODEL), row),
        ],
        out_specs=[
            pl.BlockSpec((IN_W, D_MODEL), fixed),
            pl.BlockSpec((TM, PC_W), row),
            pl.BlockSpec((CONV_K, CONV_W), fixed),
        ],
        out_shape=[
            jax.ShapeDtypeStruct((IN_W, D_MODEL), F32),
            jax.ShapeDtypeStruct((S, PC_W), BF16),
            jax.ShapeDtypeStruct((CONV_K, CONV_W), F32),
        ],
        compiler_params=_params(("arbitrary",)),
    )(da, pc, pc, pc, dmix, dmix, conv_w, xn)


def _adam_update(w, g, m, v):
    c1 = 1.0 - ADAM_B1 ** ADAM_STEP
    c2 = 1.0 - ADAM_B2 ** ADAM_STEP
    nm = ADAM_B1 * m + (1.0 - ADAM_B1) * g
    nv = ADAM_B2 * v + (1.0 - ADAM_B2) * (g * g)
    return -ADAM_LR * ((nm / c1) / (jnp.sqrt(nv / c2) + ADAM_EPS) + ADAM_WD * w), nm, nv


def _sum_chips_adamw(own, others, w, m, v, name):
    def body(own_ref, p_ref, w_ref, m_ref, v_ref, g_ref, d_ref, nm_ref, nv_ref):
        g = own_ref[...]
        for k in range(N_CHIP - 1):
            g = g + p_ref[k].astype(F32)
        g_ref[...] = g
        d_ref[...], nm_ref[...], nv_ref[...] = _adam_update(w_ref[...], g, m_ref[...], v_ref[...])

    shape = jax.ShapeDtypeStruct(w.shape, F32)
    return pl.pallas_call(
        body,
        name=name,
        out_shape=[shape] * 4,
        compiler_params=_params(),
    )(own, others, w, m, v)


def _sum_parts(parts, name):
    n = parts.shape[0]

    def body(p_ref, o_ref):
        acc = p_ref[0]
        for k in range(1, n):
            acc = acc + p_ref[k]
        o_ref[...] = acc

    return pl.pallas_call(
        body,
        name=name,
        out_shape=jax.ShapeDtypeStruct(parts.shape[1:], F32),
        compiler_params=_params(),
    )(parts)


def _adamw(w, g, m, v, name):
    def body(w_ref, g_ref, m_ref, v_ref, d_ref, nm_ref, nv_ref):
        d_ref[...], nm_ref[...], nv_ref[...] = _adam_update(w_ref[...], g_ref[...], m_ref[...], v_ref[...])

    shape = jax.ShapeDtypeStruct(w.shape, F32)
    return pl.pallas_call(
        body,
        name=name,
        out_shape=[shape, shape, shape],
        compiler_params=_params(),
    )(w, g, m, v)


def _rope_table(S):
    half = ROT_DIM // 2
    pos = jnp.arange(S, dtype=jnp.int32).astype(F32)
    inv_freq = ROPE_THETA ** (-jnp.arange(0, ROT_DIM, 2, dtype=F32) / ROT_DIM)
    ang = inv_freq[:, None] * pos[None, :]
    cs = jnp.concatenate([jnp.cos(ang), jnp.sin(ang)], axis=0)

    def body(cs_ref, o_ref):
        xt = jnp.concatenate([cs_ref[...], jnp.zeros((128 - 2 * half, TQ), F32)], axis=0).T
        r = lax.broadcasted_iota(jnp.int32, (TQ, 128), 1) & (HEAD_DIM - 1)
        first = lax.broadcasted_iota(jnp.int32, (TQ, 128), 1) < HEAD_DIM

        def at(shift_first, shift_second):
            return jnp.where(first, pltpu.roll(xt, shift_first, 1) if shift_first else xt,
                             pltpu.roll(xt, shift_second, 1))

        cos_lo, cos_hi = at(0, HEAD_DIM), at(half, HEAD_DIM + half)
        sin_lo, sin_hi = at(128 - half, HEAD_DIM - half), at(0, HEAD_DIM)
        o_ref[:, 0:128] = jnp.where(r < half, cos_lo, jnp.where(r < ROT_DIM, cos_hi, 1.0))
        o_ref[:, 128:256] = jnp.where(r < half, -sin_lo, 0.0)
        o_ref[:, 256:384] = jnp.where((r >= half) & (r < ROT_DIM), sin_hi, 0.0)

    return pl.pallas_call(
        body,
        name="rope_table",
        grid=(S // TQ,),
        in_specs=[pl.BlockSpec((2 * half, TQ), lambda i: (0, i))],
        out_specs=pl.BlockSpec((TQ, 384), lambda i: (i, 0)),
        out_shape=jax.ShapeDtypeStruct((S, 384), F32),
        compiler_params=_params(("arbitrary",)),
    )(cs)


def kernel(x, norm_g, w_in, sinks, conv_w, w_out, final_g, loss_target, m_norm_g, m_w_in, m_sinks, m_conv_w, m_w_out, m_final_g, v_norm_g, v_w_in, v_sinks, v_conv_w, v_w_out, v_final_g):
    S = x.shape[1]
    me = 4 * lax.axis_index("x") + 2 * lax.axis_index("y") + lax.axis_index("c")
    x2 = x.reshape(S, D_MODEL)
    t2 = loss_target.reshape(S, D_MODEL)
    ng = norm_g.reshape(1, D_MODEL)
    fg = final_g.reshape(1, D_MODEL)

    cw_pad = jnp.zeros((8, 128), F32).at[0:CONV_K, 0:64].set(conv_w)
    (wt,) = _all_gather([w_in.T.astype(BF16)], "all_gather_w_in")

    tab = _rope_table(S)
    xn, pa, pc, (wo, cw_all) = _fwd_proj(x2, ng, wt, [w_out.astype(BF16), cw_pad])
    cw = cw_all.reshape(N_DEV, 8, 128)[:, 0:CONV_K, 0:64].transpose(1, 0, 2).reshape(CONV_K, CONV_W)
    ya = _attn_fwd(pa, tab, sinks)
    dh, dmix, g_wo, g_fg, loss_part = _out_loss(x2, t2, ya, pc, cw, wo, fg)
    da, g_sinks = _attn_bwd(pa, dmix, tab, sinks)
    g_wt, dc, g_cw = _grad_w_in(da, pc, dmix, cw, xn)
    grad_x, g_ng, own, others = _grad_x(
        da, dc, wt, x2, dh, ng,
        [g_wt.reshape(N_DEV, SHARD_IN, D_MODEL), g_wo.reshape(N_DEV, SHARD_OUT, D_MODEL)])
    gt, dt, nmt, nvt = _sum_chips_adamw(own[0], others[0], w_in.T, m_w_in.T, v_w_in.T, "adamw_w_in")
    grad_w_in, d_w_in, nm_w_in, nv_w_in = gt.T, dt.T, nmt.T, nvt.T
    grad_w_out, d_w_out, nm_w_out, nv_w_out = _sum_chips_adamw(
        own[1], others[1], w_out, m_w_out, v_w_out, "adamw_w_out")
    small = jnp.concatenate([
        g_ng.reshape(8, 128), g_fg.reshape(8, 128), g_sinks,
        g_cw.reshape(12, 128), jnp.broadcast_to(loss_part, (4, 128))], axis=0)
    (small_all,) = _all_gather([small], "all_gather_small_grads")
    small_sum = _sum_parts(small_all.reshape(N_DEV, SMALL_ROWS, 128), "sum_small_grads")
    grad_norm_g = small_sum[0:8].reshape(D_MODEL)
    grad_final_g = small_sum[8:16].reshape(D_MODEL)
    grad_sinks = small_sum[16:24, 0]
    grad_conv_w = lax.dynamic_slice(small_sum[24:36].reshape(CONV_K, CONV_W), (0, me * 64), (CONV_K, 64))
    loss = small_sum[36, 0]

    def pack(a, b, c_, d):
        return jnp.concatenate([
            a.reshape(8, 128), b.reshape(8, 128),
            jnp.zeros((8, 128), F32).at[0, 0:8].set(c_).at[1:1 + CONV_K, 0:64].set(d)], axis=1)

    d_s, nm_s, nv_s = _adamw(
        pack(norm_g, final_g, sinks, conv_w), pack(grad_norm_g, grad_final_g, grad_sinks, grad_conv_w),
        pack(m_norm_g, m_final_g, m_sinks, m_conv_w),
        pack(v_norm_g, v_final_g, v_sinks, v_conv_w),
        "adamw_small")

    def unpack(p):
        return (p[:, 0:128].reshape(D_MODEL), p[:, 128:256].reshape(D_MODEL), p[0, 256:264], p[1:1 + CONV_K, 256:320])

    d_ng, d_fg, d_sk, d_cw = unpack(d_s)
    nm_ng, nm_fg, nm_sk, nm_cw = unpack(nm_s)
    nv_ng, nv_fg, nv_sk, nv_cw = unpack(nv_s)

    return (loss, grad_x.reshape(1, S, D_MODEL), grad_norm_g, grad_w_in, grad_sinks, grad_conv_w, grad_w_out, grad_final_g,
            d_ng, d_w_in, d_sk, d_cw, d_w_out, d_fg,
            nm_ng, nm_w_in, nm_sk, nm_cw, nm_w_out, nm_fg,
            nv_ng, nv_w_in, nv_sk, nv_cw, nv_w_out, nv_fg)
```

```python
import numpy as np
import jax
import jax.numpy as jnp
from jax import lax
from jax.experimental import pallas as pl
from jax.experimental.pallas import tpu as pltpu

F32 = jnp.float32
BF16 = jnp.bfloat16
MESH = pl.DeviceIdType.MESH

D_MODEL = 1024
HEAD_DIM = 64
N_Q_HEADS = 8
GROUP = 4
ATTN_W = 512
KV_W = 128
BLK = 128
CONV_W = 512
CONV_K = 3
IN_W = 3328
PA_W = 1280
PC_W = 2048
EPS = 1e-5
ROPE_THETA = 500000.0
ROT_DIM = 16
N_DEV = 8
N_CHIP = 4
SHARD_IN = IN_W // N_DEV
SHARD_OUT = D_MODEL // N_DEV
SMALL_ROWS = 40

ADAM_LR = 0.001
ADAM_B1 = 0.9
ADAM_B2 = 0.999
ADAM_EPS = 1e-08
ADAM_WD = 0.01
ADAM_STEP = 10

ACT = jnp.bfloat16

TM = 512
TQ = 512
TC = 512
HALO = 16
VMEM_LIMIT = 56 * 1024 * 1024

NT_DIMS = (((1,), (1,)), ((), ()))
TN_DIMS = (((0,), (0,)), ((), ()))


def _params(sem=None):
    kw = dict(vmem_limit_bytes=VMEM_LIMIT)
    if sem is not None:
        kw["dimension_semantics"] = sem
    return pltpu.CompilerParams(**kw)


def _nt(a, b):
    return lax.dot_general(a, b, NT_DIMS, preferred_element_type=F32)


def _tn(a, b):
    return lax.dot_general(a, b, TN_DIMS, preferred_element_type=F32)


def _nn(a, b):
    return jnp.dot(a, b, preferred_element_type=F32)


def _silu(g):
    return g * jax.nn.sigmoid(g)


def _dsilu(g):
    s = jax.nn.sigmoid(g)
    return s * (1.0 + g * (1.0 - s))


def _all_gather(arrs, name):
    n_arr = len(arrs)

    def body(*refs):
        x_refs = refs[:n_arr]
        out_refs = refs[n_arr:2 * n_arr]
        send_sems, recv_sems, local_sems = refs[2 * n_arr:]
        x, y, c = lax.axis_index("x"), lax.axis_index("y"), lax.axis_index("c")
        me, sibling = (x, y, c), (x, y, 1 - c)
        chips = [(1 - x, y), (x, 1 - y), (1 - x, 1 - y)]

        def rows(a, px, py, pc):
            m = x_refs[a].shape[0]
            return out_refs[a].at[pl.ds((4 * px + 2 * py + pc) * m, m), :]

        def copy(a, k, block, to, src=None):
            return pltpu.make_async_remote_copy(
                src_ref=rows(a, *block) if src is None else src,
                dst_ref=rows(a, *block),
                send_sem=send_sems.at[a * 7 + k],
                recv_sem=recv_sems.at[a * 7 + k],
                device_id=to,
                device_id_type=MESH,
            )

        mine = [pltpu.make_async_copy(x_refs[a], rows(a, *me), local_sems.at[a]) for a in range(n_arr)]
        for cp in mine:
            cp.start()
        first = []
        for a in range(n_arr):
            first.append(copy(a, 0, me, sibling, src=x_refs[a]))
            first += [copy(a, 1 + j, me, (*chip, c), src=x_refs[a]) for j, chip in enumerate(chips)]
        for cp in first:
            cp.start()
        passed = []
        for j, chip in enumerate(chips):
            for a in range(n_arr):
                copy(a, 1 + j, (*chip, c), me).wait_recv()
                fwd = copy(a, 4 + j, (*chip, c), sibling)
                fwd.start()
                passed.append(fwd)
        for a in range(n_arr):
            copy(a, 0, sibling, me).wait_recv()
            for j, chip in enumerate(chips):
                copy(a, 4 + j, (*chip, 1 - c), me).wait_recv()
        for cp in first + passed:
            cp.wait_send()
        for cp in mine:
            cp.wait()

    vmem = pl.BlockSpec(memory_space=pltpu.VMEM)
    return pl.pallas_call(
        body,
        name=name,
        out_shape=[jax.ShapeDtypeStruct((N_DEV * a.shape[0], a.shape[1]), a.dtype) for a in arrs],
        in_specs=[vmem] * n_arr,
        out_specs=[vmem] * n_arr,
        scratch_shapes=[
            pltpu.SemaphoreType.DMA((7 * n_arr,)),
            pltpu.SemaphoreType.DMA((7 * n_arr,)),
            pltpu.SemaphoreType.DMA((n_arr,)),
        ],
        compiler_params=_params(),
    )(*arrs)


class _AllGatherInSteps:
    def __init__(self, arrs, forward_step):
        self.blocks = [(a.shape, a.dtype) for a in arrs]
        self.n = len(arrs)
        self.forward_step = forward_step

    def out_shape(self):
        return [jax.ShapeDtypeStruct((N_DEV * s[0], s[1]), d) for s, d in self.blocks]

    def scratch_shapes(self):
        return [pltpu.SemaphoreType.DMA((7 * self.n,)), pltpu.SemaphoreType.DMA((7 * self.n,)),
                pltpu.SemaphoreType.DMA((self.n,))]

    def emit(self, step, n_steps, x_refs, out_refs, scratch):
        assert n_steps > self.forward_step + 1
        send_sems, recv_sems, local_sems = scratch
        x, y, c = lax.axis_index("x"), lax.axis_index("y"), lax.axis_index("c")
        me, sibling = (x, y, c), (x, y, 1 - c)
        chips = [(1 - x, y), (x, 1 - y), (1 - x, 1 - y)]

        def rows(a, px, py, pc):
            m = self.blocks[a][0][0]
            return out_refs[a].at[pl.ds((4 * px + 2 * py + pc) * m, m), :]

        def copy(a, k, block, to, src=None):
            return pltpu.make_async_remote_copy(
                src_ref=rows(a, *block) if src is None else src, dst_ref=rows(a, *block),
                send_sem=send_sems.at[a * 7 + k], recv_sem=recv_sems.at[a * 7 + k],
                device_id=to, device_id_type=MESH)

        def mine(a):
            return pltpu.make_async_copy(x_refs[a], rows(a, *me), local_sems.at[a])

        def first(a):
            return ([copy(a, 0, me, sibling, src=x_refs[a])]
                    + [copy(a, 1 + j, me, (*chip, c), src=x_refs[a]) for j, chip in enumerate(chips)])

        def passed(a):
            return [copy(a, 4 + j, (*chip, c), sibling) for j, chip in enumerate(chips)]

        @pl.when(step == 0)
        def _():
            for a in range(self.n):
                mine(a).start()
                for cp in first(a):
                    cp.start()

        @pl.when(step == self.forward_step)
        def _():
            for j, chip in enumerate(chips):
                for a in range(self.n):
                    copy(a, 1 + j, (*chip, c), me).wait_recv()
                    copy(a, 4 + j, (*chip, c), sibling).start()

        def finish():
            @pl.when(step == n_steps - 1)
            def _():
                for a in range(self.n):
                    copy(a, 0, sibling, me).wait_recv()
                    for j, chip in enumerate(chips):
                        copy(a, 4 + j, (*chip, 1 - c), me).wait_recv()
                    for cp in first(a) + passed(a):
                        cp.wait_send()
                    mine(a).wait()

        return finish


class _ReduceScatter:
    def __init__(self, grads):
        self.shapes = [g.shape[1:] for g in grads]
        self.n = len(grads)
        self.items = tuple((a, r) for r in (1, 2, 3, 0) for a in range(self.n))
        self.steps = len(self.items) + 2

    def out_shape(self):
        own = [jax.ShapeDtypeStruct(s, F32) for s in self.shapes]
        ici = [jax.ShapeDtypeStruct((N_CHIP - 1,) + s, BF16) for s in self.shapes]
        land = [jax.ShapeDtypeStruct((N_CHIP,) + s, F32) for s in self.shapes]
        return own + ici + land

    def scratch_shapes(self):
        n_items = len(self.items)
        return ([pltpu.VMEM((2,) + s, F32) for s in self.shapes]
                + [pltpu.VMEM((N_CHIP - 1,) + s, BF16) for s in self.shapes]
                + [pltpu.VMEM(s, F32) for s in self.shapes]
                + [pltpu.SemaphoreType.DMA((self.n * N_CHIP,))] * 2
                + [pltpu.SemaphoreType.DMA((2 * n_items,))]
                + [pltpu.SemaphoreType.DMA((self.n * (N_CHIP - 1),))] * 2
                + [pltpu.SemaphoreType.DMA((self.n,))])

    def emit(self, step, n_steps, g_refs, out_refs, scratch):
        assert n_steps > self.steps
        n = self.n
        own_refs, ici_refs, land_refs = out_refs[:n], out_refs[n:2 * n], out_refs[2 * n:]
        stage, pair_bf, pair_own = scratch[:n], scratch[n:2 * n], scratch[2 * n:3 * n]
        sib_send, sib_recv, load_sems, ici_send, ici_recv, own_sems = scratch[3 * n:]
        x, y, c = lax.axis_index("x"), lax.axis_index("y"), lax.axis_index("c")

        def chip_of(r):
            return (x ^ (r >> 1), y ^ (r & 1))

        def block_of(r, core):
            cx, cy = chip_of(r)
            return 4 * cx + 2 * cy + core

        def to_sibling(a, r):
            return pltpu.make_async_remote_copy(
                src_ref=g_refs[a].at[block_of(r, 1 - c)], dst_ref=land_refs[a].at[r],
                send_sem=sib_send.at[a * N_CHIP + r], recv_sem=sib_recv.at[a * N_CHIP + r],
                device_id=(x, y, 1 - c), device_id_type=MESH)

        def loads(k):
            a, r = self.items[k]
            return (pltpu.make_async_copy(g_refs[a].at[block_of(r, c)], stage[a].at[0], load_sems.at[2 * k]),
                    pltpu.make_async_copy(land_refs[a].at[r], stage[a].at[1], load_sems.at[2 * k + 1]))

        def to_owner(k):
            a, r = self.items[k]
            if r == 0:
                return pltpu.make_async_copy(pair_own[a], own_refs[a], own_sems.at[a])
            return pltpu.make_async_remote_copy(
                src_ref=pair_bf[a].at[r - 1], dst_ref=ici_refs[a].at[r - 1],
                send_sem=ici_send.at[a * (N_CHIP - 1) + r - 1], recv_sem=ici_recv.at[a * (N_CHIP - 1) + r - 1],
                device_id=(*chip_of(r), c), device_id_type=MESH)

        @pl.when(step == 0)
        def _():
            for a, r in self.items:
                to_sibling(a, r).start()

        for k, (a, r) in enumerate(self.items):
            @pl.when(step == 1 + k)
            def _(k=k, a=a, r=r):
                to_sibling(a, r).wait_recv()
                for cp in loads(k):
                    cp.start()

            @pl.when(step == 2 + k)
            def _(k=k, a=a, r=r):
                for cp in loads(k):
                    cp.wait()
                total = stage[a][0] + stage[a][1]
                if r == 0:
                    pair_own[a][...] = total
                else:
                    pair_bf[a][r - 1] = total.astype(BF16)
                to_owner(k).start()

        def finish():
            @pl.when(step == n_steps - 1)
            def _():
                for k, (a, r) in enumerate(self.items):
                    if r == 0:
                        to_owner(k).wait()
                    else:
                        to_owner(k).wait_send()
                        to_owner(k).wait_recv()
                for a, r in self.items:
                    to_sibling(a, r).wait_send()

        return finish


def _prologue(x, norm_g, w_shard):
    S = x.shape[0]
    n_steps = S // TM
    half = ROT_DIM // 2
    pos = jnp.arange(S, dtype=jnp.int32).astype(F32)
    inv_freq = ROPE_THETA ** (-jnp.arange(0, ROT_DIM, 2, dtype=F32) / ROT_DIM)
    ang = inv_freq[:, None] * pos[None, :]
    cs = jnp.concatenate([jnp.cos(ang), jnp.sin(ang)], axis=0)
    ag = _AllGatherInSteps([w_shard], forward_step=n_steps - 3)

    def body(x_ref, g_ref, cs_ref, w_ref, xn_ref, tab_ref, wt_ref, *ag_scratch):
        step = pl.program_id(0)
        finish = ag.emit(step, n_steps, [w_ref], [wt_ref], ag_scratch)
        xv = x_ref[...]
        r = lax.rsqrt(jnp.mean(xv * xv, axis=-1, keepdims=True) + EPS)
        xn_ref[...] = (xv * r * g_ref[...]).astype(BF16)

        xt = jnp.concatenate([cs_ref[...], jnp.zeros((128 - 2 * half, TM), F32)], axis=0).T
        lane = lax.broadcasted_iota(jnp.int32, (TM, 128), 1)
        rr = lane & (HEAD_DIM - 1)
        first = lane < HEAD_DIM

        def at(shift_first, shift_second):
            return jnp.where(first, pltpu.roll(xt, shift_first, 1) if shift_first else xt,
                             pltpu.roll(xt, shift_second, 1))

        cos_lo, cos_hi = at(0, HEAD_DIM), at(half, HEAD_DIM + half)
        sin_lo, sin_hi = at(128 - half, HEAD_DIM - half), at(0, HEAD_DIM)
        tab_ref[:, 0:128] = jnp.where(rr < half, cos_lo, jnp.where(rr < ROT_DIM, cos_hi, 1.0))
        tab_ref[:, 128:256] = jnp.where(rr < half, -sin_lo, 0.0)
        tab_ref[:, 256:384] = jnp.where((rr >= half) & (rr < ROT_DIM), sin_hi, 0.0)
        finish()

    any_spec = pl.BlockSpec(memory_space=pl.ANY)
    return pl.pallas_call(
        body,
        name="prologue_all_gather_w_in",
        grid=(n_steps,),
        in_specs=[
            pl.BlockSpec((TM, D_MODEL), lambda i: (i, 0)),
            pl.BlockSpec((1, D_MODEL), lambda i: (0, 0)),
            pl.BlockSpec((2 * half, TM), lambda i: (0, i)),
            any_spec,
        ],
        out_specs=[
            pl.BlockSpec((TM, D_MODEL), lambda i: (i, 0)),
            pl.BlockSpec((TM, 384), lambda i: (i, 0)),
            any_spec,
        ],
        out_shape=[
            jax.ShapeDtypeStruct((S, D_MODEL), BF16),
            jax.ShapeDtypeStruct((S, 384), F32),
        ] + ag.out_shape(),
        scratch_shapes=ag.scratch_shapes(),
        compiler_params=_params(("arbitrary",)),
    )(x, norm_g, cs, w_shard)


def _fwd_proj(xn, wt, later):
    S = xn.shape[0]

    n_steps = S // TM
    ag = _AllGatherInSteps(later, forward_step=3)

    def body(xn_ref, wt_ref, *rest):
        later_refs, rest = rest[:ag.n], rest[ag.n:]
        pa_ref, pc_ref = rest[:2]
        gathered, ag_scratch = rest[2:2 + ag.n], rest[2 + ag.n:]
        step = pl.program_id(0)
        finish = ag.emit(step, n_steps, later_refs, gathered, ag_scratch)
        xn = xn_ref[...]
        pa_ref[:, 0:512] = _nt(xn, wt_ref[0:512, :]).astype(ACT)
        pa_ref[:, 512:1024] = _nt(xn, wt_ref[768:1280, :]).astype(ACT)
        pa_ref[:, 1024:1280] = _nt(xn, wt_ref[512:768, :]).astype(ACT)
        pc_ref[...] = _nt(xn, wt_ref[1280:3328, :]).astype(ACT)
        finish()

    any_spec = pl.BlockSpec(memory_space=pl.ANY)
    outs = pl.pallas_call(
        body,
        name="fwd_proj_all_gather",
        grid=(n_steps,),
        in_specs=[
            pl.BlockSpec((TM, D_MODEL), lambda i: (i, 0)),
            pl.BlockSpec((IN_W, D_MODEL), lambda i: (0, 0)),
        ] + [any_spec] * ag.n,
        out_specs=[
            pl.BlockSpec((TM, PA_W), lambda i: (i, 0)),
            pl.BlockSpec((TM, PC_W), lambda i: (i, 0)),
        ] + [any_spec] * ag.n,
        out_shape=[
            jax.ShapeDtypeStruct((S, PA_W), ACT),
            jax.ShapeDtypeStruct((S, PC_W), ACT),
        ] + ag.out_shape(),
        scratch_shapes=ag.scratch_shapes(),
        compiler_params=_params(("arbitrary",)),
    )(xn, wt, *later)
    return outs[0], outs[1], outs[2:]


def _rope(t, tab):
    return (t * tab[:, 0:128] + pltpu.roll(t, 120, 1) * tab[:, 128:256]
            + pltpu.roll(t, 8, 1) * tab[:, 256:384])


def _rope_t(d, tab):
    return (d * tab[:, 0:128] + pltpu.roll(d * tab[:, 128:256], 8, 1)
            + pltpu.roll(d * tab[:, 256:384], 120, 1))


def _fill_kv(kall, kvc_ref, kvp_ref, tabc_ref, tabp_ref):
    for lo, kv_ref, tab_ref, n in ((0, kvp_ref, tabp_ref, BLK), (BLK, kvc_ref, tabc_ref, TQ)):
        k = _rope(kv_ref[:, 0:128].astype(F32), tab_ref[...])
        v = kv_ref[:, 128:256].astype(F32)
        kall[0, lo:lo + n, :] = k.astype(BF16)
        kall[1, lo:lo + n, :] = pltpu.roll(k, 64, 1).astype(BF16)
        kall[2, lo:lo + n, :] = v.astype(BF16)
        kall[3, lo:lo + n, :] = pltpu.roll(v, 64, 1).astype(BF16)


HEADS = (((0, 0), (1, 0), (2, 1), (3, 1)), ((0, 1), (1, 1), (2, 0), (3, 0)))


def _upper():
    kj = lax.broadcasted_iota(jnp.int32, (BLK, 4 * BLK), 0)
    qi = lax.broadcasted_iota(jnp.int32, (BLK, 4 * BLK), 1) & (BLK - 1)
    return kj > qi


def _merge(upper, both):
    return jnp.where(upper, both[0:BLK, :], both[BLK:2 * BLK, :])


def _split_store(ref, s, upper_b, val):
    vb = val.astype(BF16)
    first = vb * upper_b
    ref[s, 0:BLK, :] = first
    ref[s, BLK:2 * BLK, :] = vb - first


def _sink_rows(sink_ref):
    return [jnp.concatenate([jnp.full((1, BLK), sink_ref[2 * p + e], F32) for p, e in HEADS[s]], axis=1)
            for s in range(2)]


def _stack_heads(ref, s, half, pairs):
    for a, (p, e) in enumerate(HEADS[s]):
        ref[s, a * BLK:(a + 1) * BLK, :] = jnp.where(half[e], pairs[p], 0.0).astype(BF16)


def _unstack_pair(half, outs, p):
    lo = 0 if p < 2 else 1
    rows = slice(p * BLK, (p + 1) * BLK)
    return jnp.where(half[0], outs[lo][rows, :], outs[1 - lo][rows, :])


def _softmax(sm, sinks):
    m = jnp.maximum(jnp.max(sm, axis=0, keepdims=True), sinks)
    p = jnp.exp(sm - m)
    es = jnp.exp(sinks - m)
    inv = 1.0 / (jnp.sum(p, axis=0, keepdims=True) + es)
    return p * inv, es * inv


def _scores(kk, q_stack, first):
    st = _nt(kk, q_stack)
    prev = st[0:BLK, :]
    if first is not None:
        prev = prev + jnp.where(first, -jnp.inf, 0.0)
    return prev, st[BLK:2 * BLK, :]


def _attn_specs(tile):
    nb = TQ // BLK
    prev = lambda i: jnp.maximum(tile(i) * nb - 1, 0)
    return [
        pl.BlockSpec(memory_space=pltpu.SMEM),
        pl.BlockSpec((TQ, ATTN_W), lambda i: (tile(i), 0)),
        pl.BlockSpec((TQ, ATTN_W), lambda i: (tile(i), 1)),
        pl.BlockSpec((TQ, 2 * KV_W), lambda i: (tile(i), 4)),
        pl.BlockSpec((BLK, 2 * KV_W), lambda i: (prev(i), 4)),
        pl.BlockSpec((TQ, 384), lambda i: (tile(i), 0)),
        pl.BlockSpec((BLK, 384), lambda i: (prev(i), 0)),
    ]


def _attn_fwd(pa, tab, sinks):
    S = pa.shape[0]
    nb = TQ // BLK

    def body(sink_ref, q_ref, g_ref, kvc_ref, kvp_ref, tabc_ref, tabp_ref, o_ref, kall, q_sc, p_sc):
        i = pl.program_id(0)
        _fill_kv(kall, kvc_ref, kvp_ref, tabc_ref, tabp_ref)
        lane = lax.broadcasted_iota(jnp.int32, (BLK, 128), 1)
        half = [lane < HEAD_DIM, lane >= HEAD_DIM]
        upper = _upper()
        upper_b = upper.astype(BF16)
        sinks = _sink_rows(sink_ref)
        for j in range(nb):
            rq = slice(j * BLK, (j + 1) * BLK)
            rk = slice(j * BLK, (j + 2) * BLK)
            tab = tabc_ref[rq, :]
            qr = [_rope(q_ref[rq, p * 128:(p + 1) * 128].astype(F32), tab) * 0.125 for p in range(4)]
            outs = []
            for s in range(2):
                _stack_heads(q_sc, s, half, qr)
                prev, cur = _scores(kall[s, rk, :], q_sc[s], i == 0 if j == 0 else None)
                prob, _ = _softmax(jnp.where(upper, prev, cur), sinks[s])
                _split_store(p_sc, s, upper_b, prob)
                outs.append(_tn(p_sc[s], kall[2 + s, rk, :]))
            for p in range(4):
                cols = slice(p * 128, (p + 1) * 128)
                o_ref[rq, cols] = (_unstack_pair(half, outs, p) * _silu(g_ref[rq, cols].astype(F32))).astype(BF16)

    return pl.pallas_call(
        body,
        name="attn_fwd",
        grid=(S // TQ,),
        in_specs=_attn_specs(lambda i: i),
        out_specs=pl.BlockSpec((TQ, ATTN_W), lambda i: (i, 0)),
        out_shape=jax.ShapeDtypeStruct((S, ATTN_W), BF16),
        scratch_shapes=[
            pltpu.VMEM((4, BLK + TQ, 128), BF16),
            pltpu.VMEM((2, 4 * BLK, 128), BF16),
            pltpu.VMEM((2, 2 * BLK, 4 * BLK), BF16),
        ],
        compiler_params=_params(("arbitrary",)),
    )(sinks, pa, pa, pa, pa, tab, tab)


def _shift_down(u, halo_ref, has_prev):
    def halo_u(r):
        hu = halo_ref[r:r + 1, 512:1024].astype(F32) * halo_ref[r:r + 1, 1024:1536].astype(F32)
        return jnp.where(has_prev, hu, 0.0)

    row = lax.broadcasted_iota(jnp.int32, u.shape, 0)
    um1 = jnp.where(row == 0, halo_u(HALO - 1), pltpu.roll(u, 1, 0))
    um2 = jnp.where(row == 0, halo_u(HALO - 2), jnp.where(row == 1, halo_u(HALO - 1), pltpu.roll(u, 2, 0)))
    return um1, um2


def _conv_tile(pc_ref, halo_ref, w_ref, has_prev):
    b = pc_ref[:, 0:512].astype(F32)
    c = pc_ref[:, 512:1024].astype(F32)
    hh = pc_ref[:, 1024:1536].astype(F32)
    gc = pc_ref[:, 1536:2048].astype(F32)
    u = c * hh
    um1, um2 = _shift_down(u, halo_ref, has_prev)
    cv = w_ref[0:1, :] * um2 + w_ref[1:2, :] * um1 + w_ref[2:3, :] * u
    return b, c, hh, gc, u, um1, um2, cv


def _prev_rows(width, col=0):
    return pl.BlockSpec((HALO, width), lambda i: (jnp.maximum(i * (TM // HALO) - 1, 0), col))


def _out_loss(x, target, ya, pc, conv_w, w_out, final_g):
    S = x.shape[0]

    def body(x_ref, t_ref, ya_ref, pc_ref, halo_ref, cw_ref, wo_ref, fg_ref,
             dh_ref, dmix_ref, gwo_ref, gfg_ref, loss_ref):
        @pl.when(pl.program_id(0) == 0)
        def _():
            gwo_ref[...] = jnp.zeros_like(gwo_ref)
            gfg_ref[...] = jnp.zeros_like(gfg_ref)
            loss_ref[...] = jnp.zeros_like(loss_ref)

        b, _, _, gc, _, _, _, cv = _conv_tile(pc_ref, halo_ref, cw_ref, pl.program_id(0) > 0)
        yc = (b * cv * _silu(gc)).astype(BF16)
        mix = jnp.concatenate([ya_ref[...], yc], axis=1)
        wo = wo_ref[...]
        fg = fg_ref[...]
        h = x_ref[...] + _nn(mix, wo)
        r = lax.rsqrt(jnp.mean(h * h, axis=-1, keepdims=True) + EPS)
        n = h * r
        err = n * fg - t_ref[...]
        loss_ref[...] += 0.5 * jnp.sum(jnp.mean(err * err, axis=-1, keepdims=True), axis=0, keepdims=True)
        dy = err * (1.0 / D_MODEL)
        gfg_ref[...] += jnp.sum(dy * n, axis=0, keepdims=True)
        dyg = dy * fg
        dh = r * (dyg - n * jnp.mean(dyg * n, axis=-1, keepdims=True))
        dh_ref[...] = dh
        dhb = dh.astype(BF16)
        dmix_ref[...] = _nt(dhb, wo).astype(ACT)
        gwo_ref[...] += _tn(mix, dhb)

    row = lambda i: (i, 0)
    fixed = lambda i: (0, 0)
    return pl.pallas_call(
        body,
        name="out_loss",
        grid=(S // TM,),
        in_specs=[
            pl.BlockSpec((TM, D_MODEL), row),
            pl.BlockSpec((TM, D_MODEL), row),
            pl.BlockSpec((TM, ATTN_W), row),
            pl.BlockSpec((TM, PC_W), row),
            _prev_rows(PC_W),
            pl.BlockSpec((CONV_K, CONV_W), fixed),
            pl.BlockSpec((D_MODEL, D_MODEL), fixed),
            pl.BlockSpec((1, D_MODEL), fixed),
        ],
        out_specs=[
            pl.BlockSpec((TM, D_MODEL), row),
            pl.BlockSpec((TM, D_MODEL), row),
            pl.BlockSpec((D_MODEL, D_MODEL), fixed),
            pl.BlockSpec((1, D_MODEL), fixed),
            pl.BlockSpec((1, 1), fixed),
        ],
        out_shape=[
            jax.ShapeDtypeStruct((S, D_MODEL), F32),
            jax.ShapeDtypeStruct((S, D_MODEL), ACT),
            jax.ShapeDtypeStruct((D_MODEL, D_MODEL), F32),
            jax.ShapeDtypeStruct((1, D_MODEL), F32),
            jax.ShapeDtypeStruct((1, 1), F32),
        ],
        compiler_params=_params(("arbitrary",)),
    )(x, target, ya, pc, pc, conv_w, w_out, final_g)


def _attn_bwd(pa, dmix, tab, sinks):
    S = pa.shape[0]
    nt = S // TQ
    nb = TQ // BLK

    def body(sink_ref, q_ref, g_ref, kvc_ref, kvp_ref, tabc_ref, tabp_ref, dm_ref,
             d_ref, dsink_ref, kall, dkv, carry, q_sc, do_sc, p_sc, ds_sc, dsink_acc):
        step = pl.program_id(0)
        i = nt - 1 - step

        @pl.when(step == 0)
        def _():
            carry[...] = jnp.zeros_like(carry)
            dsink_acc[...] = jnp.zeros_like(dsink_acc)

        _fill_kv(kall, kvc_ref, kvp_ref, tabc_ref, tabp_ref)
        dkv[0:TQ, :] = jnp.zeros((TQ, 2 * KV_W), F32)
        dkv[TQ:TQ + BLK, :] = carry[...]
        lane = lax.broadcasted_iota(jnp.int32, (BLK, 128), 1)
        half = [lane < HEAD_DIM, lane >= HEAD_DIM]
        upper = _upper()
        upper_b = upper.astype(BF16)
        sinks = _sink_rows(sink_ref)
        for j in range(nb):
            rq = slice(j * BLK, (j + 1) * BLK)
            rk = slice(j * BLK, (j + 2) * BLK)
            tab = tabc_ref[rq, :]
            pair = [slice(p * 128, (p + 1) * 128) for p in range(4)]
            qr = [_rope(q_ref[rq, c].astype(F32), tab) * 0.125 for c in pair]
            g = [g_ref[rq, c].astype(F32) for c in pair]
            da = [dm_ref[rq, c].astype(F32) for c in pair]
            do = [da[p] * _silu(g[p]) for p in range(4)]
            outs, dqs, dks, dvs = [], [], [], []
            for s in range(2):
                kk = kall[s, rk, :]
                vv = kall[2 + s, rk, :]
                _stack_heads(q_sc, s, half, qr)
                _stack_heads(do_sc, s, half, do)
                prev, cur = _scores(kk, q_sc[s], i == 0 if j == 0 else None)
                prob, psink = _softmax(jnp.where(upper, prev, cur), sinks[s])
                _split_store(p_sc, s, upper_b, prob)
                dprob = _merge(upper, _nt(vv, do_sc[s]))
                dsum = jnp.sum(dprob * prob, axis=0, keepdims=True)
                _split_store(ds_sc, s, upper_b, prob * (dprob - dsum))
                dsink_acc[s, 0:1, :] += psink * dsum
                outs.append(_tn(p_sc[s], vv))
                dqs.append(_tn(ds_sc[s], kk))
                dks.append(_nn(ds_sc[s], q_sc[s]))
                dvs.append(_nn(p_sc[s], do_sc[s]))
            for p in range(4):
                d_ref[rq, pair[p]] = _rope_t(_unstack_pair(half, dqs, p) * 0.125, tab).astype(BF16)
                d_ref[rq, 512 + p * 128:512 + (p + 1) * 128] = (
                    da[p] * _unstack_pair(half, outs, p) * _dsilu(g[p])).astype(BF16)
            dkv[rk, 0:128] += dks[0] + pltpu.roll(dks[1], 64, 1)
            dkv[rk, 128:256] += dvs[0] + pltpu.roll(dvs[1], 64, 1)
        d_ref[:, 1024:1152] = _rope_t(dkv[BLK:BLK + TQ, 0:128], tabc_ref[...]).astype(BF16)
        d_ref[:, 1152:1280] = dkv[BLK:BLK + TQ, 128:256].astype(BF16)
        carry[...] = dkv[0:BLK, :]

        @pl.when(step == nt - 1)
        def _():
            for s in range(2):
                for a, (p, e) in enumerate(HEADS[s]):
                    h = 2 * p + e
                    tot = jnp.sum(dsink_acc[s, 0:1, a * BLK:(a + 1) * BLK], axis=1, keepdims=True)
                    dsink_ref[h:h + 1, :] = jnp.broadcast_to(-tot, (1, 128))

    rev = lambda s: nt - 1 - s
    return pl.pallas_call(
        body,
        name="attn_bwd",
        grid=(nt,),
        in_specs=_attn_specs(rev) + [pl.BlockSpec((TQ, ATTN_W), lambda s: (nt - 1 - s, 0))],
        out_specs=[
            pl.BlockSpec((TQ, PA_W), lambda s: (nt - 1 - s, 0)),
            pl.BlockSpec((8, 128), lambda s: (0, 0)),
        ],
        out_shape=[
            jax.ShapeDtypeStruct((S, PA_W), BF16),
            jax.ShapeDtypeStruct((8, 128), F32),
        ],
        scratch_shapes=[
            pltpu.VMEM((4, BLK + TQ, 128), BF16),
            pltpu.VMEM((BLK + TQ, 2 * KV_W), F32),
            pltpu.VMEM((BLK, 2 * KV_W), F32),
            pltpu.VMEM((2, 4 * BLK, 128), BF16),
            pltpu.VMEM((2, 4 * BLK, 128), BF16),
            pltpu.VMEM((2, 2 * BLK, 4 * BLK), BF16),
            pltpu.VMEM((2, 2 * BLK, 4 * BLK), BF16),
            pltpu.VMEM((2, 8, 4 * BLK), F32),
        ],
        compiler_params=_params(("arbitrary",)),
    )(sinks, pa, pa, pa, pa, tab, tab, dmix)


def _conv_bwd_tile(pc_ref, prev_ref, next_ref, dm_ref, dmn_ref, w_ref, d_ref, gw_ref, has_prev, has_next):
    rows = pc_ref.shape[0]
    w0, w1, w2 = w_ref[0:1, :], w_ref[1:2, :], w_ref[2:3, :]
    b, c, hh, gc, u, um1, um2, cv = _conv_tile(pc_ref, prev_ref, w_ref, has_prev)
    sg = _silu(gc)
    dy = dm_ref[...].astype(F32)
    dcv = dy * b * sg

    def next_dcv(r):
        nd = (dmn_ref[r:r + 1, :].astype(F32) * next_ref[r:r + 1, 0:512].astype(F32)
              * _silu(next_ref[r:r + 1, 1536:2048].astype(F32)))
        return jnp.where(has_next, nd, 0.0)

    row = lax.broadcasted_iota(jnp.int32, (rows, CONV_W), 0)
    dp1 = jnp.where(row == rows - 1, next_dcv(0), pltpu.roll(dcv, rows - 1, 0))
    dp2 = jnp.where(row == rows - 1, next_dcv(1),
                    jnp.where(row == rows - 2, next_dcv(0), pltpu.roll(dcv, rows - 2, 0)))
    du = w2 * dcv + w1 * dp1 + w0 * dp2
    d_ref[:, 0:512] = (dy * cv * sg).astype(BF16)
    d_ref[:, 512:1024] = (du * hh).astype(BF16)
    d_ref[:, 1024:1536] = (du * c).astype(BF16)
    d_ref[:, 1536:2048] = (dy * b * cv * _dsilu(gc)).astype(BF16)
    gw_ref[0:1, :] += jnp.sum(dcv * um2, axis=0, keepdims=True)
    gw_ref[1:2, :] += jnp.sum(dcv * um1, axis=0, keepdims=True)
    gw_ref[2:3, :] += jnp.sum(dcv * u, axis=0, keepdims=True)


def _grad_x(da, dc, wt, x, dh, norm_g, grads):
    S = x.shape[0]
    n_steps = S // TM
    rs = _ReduceScatter(grads)
    n_rs_out = len(rs.out_shape())

    def body(da_ref, dc_ref, wt_ref, x_ref, dh_ref, g_ref, *rest):
        grad_refs, rest = rest[:rs.n], rest[rs.n:]
        gx_ref, gng_ref = rest[:2]
        rs_out, rs_scratch = rest[2:2 + n_rs_out], rest[2 + n_rs_out:]
        step = pl.program_id(0)
        finish = rs.emit(step, n_steps, grad_refs, rs_out, rs_scratch)

        @pl.when(step == 0)
        def _():
            gng_ref[...] = jnp.zeros_like(gng_ref)

        dxn = (_nn(da_ref[:, 0:512], wt_ref[0:512, :]) + _nn(da_ref[:, 512:1024], wt_ref[768:1280, :])
               + _nn(da_ref[:, 1024:1280], wt_ref[512:768, :]) + _nn(dc_ref[...], wt_ref[1280:3328, :]))
        xv = x_ref[...]
        r = lax.rsqrt(jnp.mean(xv * xv, axis=-1, keepdims=True) + EPS)
        n = xv * r
        gng_ref[...] += jnp.sum(dxn * n, axis=0, keepdims=True)
        dxg = dxn * g_ref[...]
        gx_ref[...] = dh_ref[...] + r * (dxg - n * jnp.mean(dxg * n, axis=-1, keepdims=True))
        finish()

    row = lambda i: (i, 0)
    fixed = lambda i: (0, 0)
    any_spec = pl.BlockSpec(memory_space=pl.ANY)
    outs = pl.pallas_call(
        body,
        name="grad_x_reduce_scatter",
        grid=(n_steps,),
        in_specs=[
            pl.BlockSpec((TM, PA_W), row),
            pl.BlockSpec((TM, PC_W), row),
            pl.BlockSpec((IN_W, D_MODEL), fixed),
            pl.BlockSpec((TM, D_MODEL), row),
            pl.BlockSpec((TM, D_MODEL), row),
            pl.BlockSpec((1, D_MODEL), fixed),
        ] + [any_spec] * rs.n,
        out_specs=[pl.BlockSpec((TM, D_MODEL), row), pl.BlockSpec((1, D_MODEL), fixed)] + [any_spec] * n_rs_out,
        out_shape=[jax.ShapeDtypeStruct((S, D_MODEL), F32), jax.ShapeDtypeStruct((1, D_MODEL), F32)] + rs.out_shape(),
        scratch_shapes=rs.scratch_shapes(),
        compiler_params=_params(("arbitrary",)),
    )(da, dc, wt, x, dh, norm_g, *grads)
    return outs[0], outs[1], outs[2:2 + rs.n], outs[2 + rs.n:2 + 2 * rs.n]


def _grad_w_in(da, pc, dmix, conv_w, xn):
    S = xn.shape[0]
    nt = S // TM
    t16 = TM // HALO

    def body(da_ref, pc_ref, prev_ref, next_ref, dm_ref, dmn_ref, cw_ref, xn_ref, gw_ref, dc_ref, gcw_ref):
        i = pl.program_id(0)

        @pl.when(i == 0)
        def _():
            gw_ref[...] = jnp.zeros_like(gw_ref)
            gcw_ref[...] = jnp.zeros_like(gcw_ref)

        xn = xn_ref[...]
        gw_ref[0:512, :] += _tn(da_ref[:, 0:512], xn)
        gw_ref[768:1280, :] += _tn(da_ref[:, 512:1024], xn)
        gw_ref[512:768, :] += _tn(da_ref[:, 1024:1280], xn)
        _conv_bwd_tile(pc_ref, prev_ref, next_ref, dm_ref, dmn_ref, cw_ref, dc_ref, gcw_ref, i > 0, i < nt - 1)
        gw_ref[1280:3328, :] += _tn(dc_ref[...], xn)

    row = lambda i: (i, 0)
    fixed = lambda i: (0, 0)
    nxt = lambda i: jnp.minimum((i + 1) * t16, nt * t16 - 1)
    return pl.pallas_call(
        body,
        name="grad_w_in",
        grid=(nt,),
        in_specs=[
            pl.BlockSpec((TM, PA_W), row),
            pl.BlockSpec((TM, PC_W), row),
            _prev_rows(PC_W),
            pl.BlockSpec((HALO, PC_W), lambda i: (nxt(i), 0)),
            pl.BlockSpec((TM, CONV_W), lambda i: (i, 1)),
            pl.BlockSpec((HALO, CONV_W), lambda i: (nxt(i), 1)),
            pl.BlockSpec((CONV_K, CONV_W), fixed),
            pl.BlockSpec((TM, D_MODEL), row),
        ],
        out_specs=[
            pl.BlockSpec((IN_W, D_MODEL), fixed),
            pl.BlockSpec((TM, PC_W), row),
            pl.BlockSpec((CONV_K, CONV_W), fixed),
        ],
        out_shape=[
            jax.ShapeDtypeStruct((IN_W, D_MODEL), F32),
            jax.ShapeDtypeStruct((S, PC_W), BF16),
            jax.ShapeDtypeStruct((CONV_K, CONV_W), F32),
        ],
        compiler_params=_params(("arbitrary",)),
    )(da, pc, pc, pc, dmix, dmix, conv_w, xn)


def _adam_update(w, g, m, v):
    c1 = 1.0 - ADAM_B1 ** ADAM_STEP
    c2 = 1.0 - ADAM_B2 ** ADAM_STEP
    nm = ADAM_B1 * m + (1.0 - ADAM_B1) * g
    nv = ADAM_B2 * v + (1.0 - ADAM_B2) * (g * g)
    return -ADAM_LR * ((nm / c1) / (jnp.sqrt(nv / c2) + ADAM_EPS) + ADAM_WD * w), nm, nv


def _sum_chips_adamw(own, others, w, m, v, name):
    def body(own_ref, p_ref, w_ref, m_ref, v_ref, g_ref, d_ref, nm_ref, nv_ref):
        g = own_ref[...]
        for k in range(N_CHIP - 1):
            g = g + p_ref[k].astype(F32)
        g_ref[...] = g
        d_ref[...], nm_ref[...], nv_ref[...] = _adam_update(w_ref[...], g, m_ref[...], v_ref[...])

    shape = jax.ShapeDtypeStruct(w.shape, F32)
    return pl.pallas_call(
        body,
        name=name,
        out_shape=[shape] * 4,
        compiler_params=_params(),
    )(own, others, w, m, v)


def _sum_parts(parts, name):
    n = parts.shape[0]

    def body(p_ref, o_ref):
        acc = p_ref[0]
        for k in range(1, n):
            acc = acc + p_ref[k]
        o_ref[...] = acc

    return pl.pallas_call(
        body,
        name=name,
        out_shape=jax.ShapeDtypeStruct(parts.shape[1:], F32),
        compiler_params=_params(),
    )(parts)


def _adamw(w, g, m, v, name):
    def body(w_ref, g_ref, m_ref, v_ref, d_ref, nm_ref, nv_ref):
        d_ref[...], nm_ref[...], nv_ref[...] = _adam_update(w_ref[...], g_ref[...], m_ref[...], v_ref[...])

    shape = jax.ShapeDtypeStruct(w.shape, F32)
    return pl.pallas_call(
        body,
        name=name,
        out_shape=[shape, shape, shape],
        compiler_params=_params(),
    )(w, g, m, v)


def kernel(x, norm_g, w_in, sinks, conv_w, w_out, final_g, loss_target, m_norm_g, m_w_in, m_sinks, m_conv_w, m_w_out, m_final_g, v_norm_g, v_w_in, v_sinks, v_conv_w, v_w_out, v_final_g):
    S = x.shape[1]
    me = 4 * lax.axis_index("x") + 2 * lax.axis_index("y") + lax.axis_index("c")
    x2 = x.reshape(S, D_MODEL)
    t2 = loss_target.reshape(S, D_MODEL)
    ng = norm_g.reshape(1, D_MODEL)
    fg = final_g.reshape(1, D_MODEL)

    cw_pad = jnp.zeros((8, 128), F32).at[0:CONV_K, 0:64].set(conv_w)
    xn, tab, wt = _prologue(x2, ng, w_in.T.astype(BF16))
    pa, pc, (wo, cw_all) = _fwd_proj(xn, wt, [w_out.astype(BF16), cw_pad])
    cw = cw_all.reshape(N_DEV, 8, 128)[:, 0:CONV_K, 0:64].transpose(1, 0, 2).reshape(CONV_K, CONV_W)
    ya = _attn_fwd(pa, tab, sinks)
    dh, dmix, g_wo, g_fg, loss_part = _out_loss(x2, t2, ya, pc, cw, wo, fg)
    da, g_sinks = _attn_bwd(pa, dmix, tab, sinks)
    g_wt, dc, g_cw = _grad_w_in(da, pc, dmix, cw, xn)
    grad_x, g_ng, own, others = _grad_x(
        da, dc, wt, x2, dh, ng,
        [g_wt.reshape(N_DEV, SHARD_IN, D_MODEL), g_wo.reshape(N_DEV, SHARD_OUT, D_MODEL)])
    gt, dt, nmt, nvt = _sum_chips_adamw(own[0], others[0], w_in.T, m_w_in.T, v_w_in.T, "adamw_w_in")
    grad_w_in, d_w_in, nm_w_in, nv_w_in = gt.T, dt.T, nmt.T, nvt.T
    grad_w_out, d_w_out, nm_w_out, nv_w_out = _sum_chips_adamw(
        own[1], others[1], w_out, m_w_out, v_w_out, "adamw_w_out")
    small = jnp.concatenate([
        g_ng.reshape(8, 128), g_fg.reshape(8, 128), g_sinks,
        g_cw.reshape(12, 128), jnp.broadcast_to(loss_part, (4, 128))], axis=0)
    (small_all,) = _all_gather([small], "all_gather_small_grads")
    small_sum = _sum_parts(small_all.reshape(N_DEV, SMALL_ROWS, 128), "sum_small_grads")
    grad_norm_g = small_sum[0:8].reshape(D_MODEL)
    grad_final_g = small_sum[8:16].reshape(D_MODEL)
    grad_sinks = small_sum[16:24, 0]
    grad_conv_w = lax.dynamic_slice(small_sum[24:36].reshape(CONV_K, CONV_W), (0, me * 64), (CONV_K, 64))
    loss = small_sum[36, 0]

    def pack(a, b, c_, d):
        return jnp.concatenate([
            a.reshape(8, 128), b.reshape(8, 128),
            jnp.zeros((8, 128), F32).at[0, 0:8].set(c_).at[1:1 + CONV_K, 0:64].set(d)], axis=1)

    d_s, nm_s, nv_s = _adamw(
        pack(norm_g, final_g, sinks, conv_w), pack(grad_norm_g, grad_final_g, grad_sinks, grad_conv_w),
        pack(m_norm_g, m_final_g, m_sinks, m_conv_w),
        pack(v_norm_g, v_final_g, v_sinks, v_conv_w),
        "adamw_small")

    def unpack(p):
        return (p[:, 0:128].reshape(D_MODEL), p[:, 128:256].reshape(D_MODEL), p[0, 256:264], p[1:1 + CONV_K, 256:320])

    d_ng, d_fg, d_sk, d_cw = unpack(d_s)
    nm_ng, nm_fg, nm_sk, nm_cw = unpack(nm_s)
    nv_ng, nv_fg, nv_sk, nv_cw = unpack(nv_s)

    return (loss, grad_x.reshape(1, S, D_MODEL), grad_norm_g, grad_w_in, grad_sinks, grad_conv_w, grad_w_out, grad_final_g,
            d_ng, d_w_in, d_sk, d_cw, d_w_out, d_fg,
            nm_ng, nm_w_in, nm_sk, nm_cw, nm_w_out, nm_fg,
            nv_ng, nv_w_in, nv_sk, nv_cw, nv_w_out, nv_fg)
```

```python
import numpy as np
import jax
import jax.numpy as jnp
from jax import lax
from jax.experimental import pallas as pl
from jax.experimental.pallas import tpu as pltpu

F32 = jnp.float32
BF16 = jnp.bfloat16
MESH = pl.DeviceIdType.MESH

D_MODEL = 1024
HEAD_DIM = 64
N_Q_HEADS = 8
GROUP = 4
ATTN_W = 512
KV_W = 128
BLK = 128
CONV_W = 512
CONV_K = 3
IN_W = 3328
PA_W = 1280
PC_W = 2048
EPS = 1e-5
ROPE_THETA = 500000.0
ROT_DIM = 16
N_DEV = 8
N_CHIP = 4
SHARD_IN = IN_W // N_DEV
SHARD_OUT = D_MODEL // N_DEV

ADAM_LR = 0.001
ADAM_B1 = 0.9
ADAM_B2 = 0.999
ADAM_EPS = 1e-08
ADAM_WD = 0.01
ADAM_STEP = 10

ACT = jnp.bfloat16

TM = 512
TQ = 1024
HALO = 16
VMEM_LIMIT = 56 * 1024 * 1024

NT_DIMS = (((1,), (1,)), ((), ()))
TN_DIMS = (((0,), (0,)), ((), ()))


def _params(sem=None):
    kw = dict(vmem_limit_bytes=VMEM_LIMIT)
    if sem is not None:
        kw["dimension_semantics"] = sem
    return pltpu.CompilerParams(**kw)


def _nt(a, b):
    return lax.dot_general(a, b, NT_DIMS, preferred_element_type=F32)


def _tn(a, b):
    return lax.dot_general(a, b, TN_DIMS, preferred_element_type=F32)


def _nn(a, b):
    return jnp.dot(a, b, preferred_element_type=F32)


def _silu(g):
    return g * jax.nn.sigmoid(g)


def _dsilu(g):
    s = jax.nn.sigmoid(g)
    return s * (1.0 + g * (1.0 - s))


def _all_gather(arrs, name):
    n_arr = len(arrs)

    def body(*refs):
        x_refs = refs[:n_arr]
        out_refs = refs[n_arr:2 * n_arr]
        send_sems, recv_sems, local_sems = refs[2 * n_arr:]
        x, y, c = lax.axis_index("x"), lax.axis_index("y"), lax.axis_index("c")
        me, sibling = (x, y, c), (x, y, 1 - c)
        chips = [(1 - x, y), (x, 1 - y), (1 - x, 1 - y)]

        def rows(a, px, py, pc):
            m = x_refs[a].shape[0]
            return out_refs[a].at[pl.ds((4 * px + 2 * py + pc) * m, m), :]

        def copy(a, k, block, to, src=None):
            return pltpu.make_async_remote_copy(
                src_ref=rows(a, *block) if src is None else src,
                dst_ref=rows(a, *block),
                send_sem=send_sems.at[a * 7 + k],
                recv_sem=recv_sems.at[a * 7 + k],
                device_id=to,
                device_id_type=MESH,
            )

        mine = [pltpu.make_async_copy(x_refs[a], rows(a, *me), local_sems.at[a]) for a in range(n_arr)]
        for cp in mine:
            cp.start()
        first = []
        for a in range(n_arr):
            first.append(copy(a, 0, me, sibling, src=x_refs[a]))
            first += [copy(a, 1 + j, me, (*chip, c), src=x_refs[a]) for j, chip in enumerate(chips)]
        for cp in first:
            cp.start()
        passed = []
        for j, chip in enumerate(chips):
            for a in range(n_arr):
                copy(a, 1 + j, (*chip, c), me).wait_recv()
                fwd = copy(a, 4 + j, (*chip, c), sibling)
                fwd.start()
                passed.append(fwd)
        for a in range(n_arr):
            copy(a, 0, sibling, me).wait_recv()
            for j, chip in enumerate(chips):
                copy(a, 4 + j, (*chip, 1 - c), me).wait_recv()
        for cp in first + passed:
            cp.wait_send()
        for cp in mine:
            cp.wait()

    vmem = pl.BlockSpec(memory_space=pltpu.VMEM)
    return pl.pallas_call(
        body,
        name=name,
        out_shape=[jax.ShapeDtypeStruct((N_DEV * a.shape[0], a.shape[1]), a.dtype) for a in arrs],
        in_specs=[vmem] * n_arr,
        out_specs=[vmem] * n_arr,
        scratch_shapes=[
            pltpu.SemaphoreType.DMA((7 * n_arr,)),
            pltpu.SemaphoreType.DMA((7 * n_arr,)),
            pltpu.SemaphoreType.DMA((n_arr,)),
        ],
        compiler_params=_params(),
    )(*arrs)


class _AllGatherInSteps:
    def __init__(self, arrs, forward_step):
        self.blocks = [(a.shape, a.dtype) for a in arrs]
        self.n = len(arrs)
        self.forward_step = forward_step

    def out_shape(self):
        return [jax.ShapeDtypeStruct((N_DEV * s[0], s[1]), d) for s, d in self.blocks]

    def scratch_shapes(self):
        return [pltpu.SemaphoreType.DMA((7 * self.n,)), pltpu.SemaphoreType.DMA((7 * self.n,)),
                pltpu.SemaphoreType.DMA((self.n,))]

    def emit(self, step, n_steps, x_refs, out_refs, scratch):
        assert n_steps > self.forward_step + 1
        send_sems, recv_sems, local_sems = scratch
        x, y, c = lax.axis_index("x"), lax.axis_index("y"), lax.axis_index("c")
        me, sibling = (x, y, c), (x, y, 1 - c)
        chips = [(1 - x, y), (x, 1 - y), (1 - x, 1 - y)]

        def rows(a, px, py, pc):
            m = self.blocks[a][0][0]
            return out_refs[a].at[pl.ds((4 * px + 2 * py + pc) * m, m), :]

        def copy(a, k, block, to, src=None):
            return pltpu.make_async_remote_copy(
                src_ref=rows(a, *block) if src is None else src, dst_ref=rows(a, *block),
                send_sem=send_sems.at[a * 7 + k], recv_sem=recv_sems.at[a * 7 + k],
                device_id=to, device_id_type=MESH)

        def mine(a):
            return pltpu.make_async_copy(x_refs[a], rows(a, *me), local_sems.at[a])

        def first(a):
            return ([copy(a, 0, me, sibling, src=x_refs[a])]
                    + [copy(a, 1 + j, me, (*chip, c), src=x_refs[a]) for j, chip in enumerate(chips)])

        def passed(a):
            return [copy(a, 4 + j, (*chip, c), sibling) for j, chip in enumerate(chips)]

        @pl.when(step == 0)
        def _():
            for a in range(self.n):
                mine(a).start()
                for cp in first(a):
                    cp.start()

        @pl.when(step == self.forward_step)
        def _():
            for j, chip in enumerate(chips):
                for a in range(self.n):
                    copy(a, 1 + j, (*chip, c), me).wait_recv()
                    copy(a, 4 + j, (*chip, c), sibling).start()

        def finish():
            @pl.when(step == n_steps - 1)
            def _():
                for a in range(self.n):
                    copy(a, 0, sibling, me).wait_recv()
                    for j, chip in enumerate(chips):
                        copy(a, 4 + j, (*chip, 1 - c), me).wait_recv()
                    for cp in first(a) + passed(a):
                        cp.wait_send()
                    mine(a).wait()

        return finish


class _ReduceScatter:
    def __init__(self, grads):
        self.shapes = [g.shape[1:] for g in grads]
        self.n = len(grads)
        self.items = tuple((a, r) for r in (1, 2, 3, 0) for a in range(self.n))
        self.steps = len(self.items) + 2

    def out_shape(self):
        own = [jax.ShapeDtypeStruct(s, F32) for s in self.shapes]
        ici = [jax.ShapeDtypeStruct((N_CHIP - 1,) + s, BF16) for s in self.shapes]
        land = [jax.ShapeDtypeStruct((N_CHIP,) + s, F32) for s in self.shapes]
        return own + ici + land

    def scratch_shapes(self):
        n_items = len(self.items)
        return ([pltpu.VMEM((2,) + s, F32) for s in self.shapes]
                + [pltpu.VMEM((N_CHIP - 1,) + s, BF16) for s in self.shapes]
                + [pltpu.VMEM(s, F32) for s in self.shapes]
                + [pltpu.SemaphoreType.DMA((self.n * N_CHIP,))] * 2
                + [pltpu.SemaphoreType.DMA((2 * n_items,))]
                + [pltpu.SemaphoreType.DMA((self.n * (N_CHIP - 1),))] * 2
                + [pltpu.SemaphoreType.DMA((self.n,))])

    def emit(self, step, n_steps, g_refs, out_refs, scratch):
        assert n_steps > self.steps
        n = self.n
        own_refs, ici_refs, land_refs = out_refs[:n], out_refs[n:2 * n], out_refs[2 * n:]
        stage, pair_bf, pair_own = scratch[:n], scratch[n:2 * n], scratch[2 * n:3 * n]
        sib_send, sib_recv, load_sems, ici_send, ici_recv, own_sems = scratch[3 * n:]
        x, y, c = lax.axis_index("x"), lax.axis_index("y"), lax.axis_index("c")

        def chip_of(r):
            return (x ^ (r >> 1), y ^ (r & 1))

        def block_of(r, core):
            cx, cy = chip_of(r)
            return 4 * cx + 2 * cy + core

        def to_sibling(a, r):
            return pltpu.make_async_remote_copy(
                src_ref=g_refs[a].at[block_of(r, 1 - c)], dst_ref=land_refs[a].at[r],
                send_sem=sib_send.at[a * N_CHIP + r], recv_sem=sib_recv.at[a * N_CHIP + r],
                device_id=(x, y, 1 - c), device_id_type=MESH)

        def loads(k):
            a, r = self.items[k]
            return (pltpu.make_async_copy(g_refs[a].at[block_of(r, c)], stage[a].at[0], load_sems.at[2 * k]),
                    pltpu.make_async_copy(land_refs[a].at[r], stage[a].at[1], load_sems.at[2 * k + 1]))

        def to_owner(k):
            a, r = self.items[k]
            if r == 0:
                return pltpu.make_async_copy(pair_own[a], own_refs[a], own_sems.at[a])
            return pltpu.make_async_remote_copy(
                src_ref=pair_bf[a].at[r - 1], dst_ref=ici_refs[a].at[r - 1],
                send_sem=ici_send.at[a * (N_CHIP - 1) + r - 1], recv_sem=ici_recv.at[a * (N_CHIP - 1) + r - 1],
                device_id=(*chip_of(r), c), device_id_type=MESH)

        @pl.when(step == 0)
        def _():
            for a, r in self.items:
                to_sibling(a, r).start()

        for k, (a, r) in enumerate(self.items):
            @pl.when(step == 1 + k)
            def _(k=k, a=a, r=r):
                to_sibling(a, r).wait_recv()
                for cp in loads(k):
                    cp.start()

            @pl.when(step == 2 + k)
            def _(k=k, a=a, r=r):
                for cp in loads(k):
                    cp.wait()
                total = stage[a][0] + stage[a][1]
                if r == 0:
                    pair_own[a][...] = total
                else:
                    pair_bf[a][r - 1] = total.astype(BF16)
                to_owner(k).start()

        def finish():
            @pl.when(step == n_steps - 1)
            def _():
                for k, (a, r) in enumerate(self.items):
                    if r == 0:
                        to_owner(k).wait()
                    else:
                        to_owner(k).wait_send()
                        to_owner(k).wait_recv()
                for a, r in self.items:
                    to_sibling(a, r).wait_send()

        return finish


def _prologue(x, norm_g, w_shard):
    S = x.shape[0]
    n_steps = S // TM
    half = ROT_DIM // 2
    pos = jnp.arange(S, dtype=jnp.int32).astype(F32)
    inv_freq = ROPE_THETA ** (-jnp.arange(0, ROT_DIM, 2, dtype=F32) / ROT_DIM)
    ang = inv_freq[:, None] * pos[None, :]
    cs = jnp.concatenate([jnp.cos(ang), jnp.sin(ang)], axis=0)
    ag = _AllGatherInSteps([w_shard], forward_step=n_steps - 3)

    def body(x_ref, g_ref, cs_ref, w_ref, xn_ref, tab_ref, wt_ref, *ag_scratch):
        step = pl.program_id(0)
        finish = ag.emit(step, n_steps, [w_ref], [wt_ref], ag_scratch)
        xv = x_ref[...]
        r = lax.rsqrt(jnp.mean(xv * xv, axis=-1, keepdims=True) + EPS)
        xn_ref[...] = (xv * r * g_ref[...]).astype(BF16)

        xt = jnp.concatenate([cs_ref[...], jnp.zeros((128 - 2 * half, TM), F32)], axis=0).T
        lane = lax.broadcasted_iota(jnp.int32, (TM, 128), 1)
        rr = lane & (HEAD_DIM - 1)
        first = lane < HEAD_DIM

        def at(shift_first, shift_second):
            return jnp.where(first, pltpu.roll(xt, shift_first, 1) if shift_first else xt,
                             pltpu.roll(xt, shift_second, 1))

        cos_lo, cos_hi = at(0, HEAD_DIM), at(half, HEAD_DIM + half)
        sin_lo, sin_hi = at(128 - half, HEAD_DIM - half), at(0, HEAD_DIM)
        tab_ref[:, 0:128] = jnp.where(rr < half, cos_lo, jnp.where(rr < ROT_DIM, cos_hi, 1.0))
        tab_ref[:, 128:256] = jnp.where(rr < half, -sin_lo, 0.0)
        tab_ref[:, 256:384] = jnp.where((rr >= half) & (rr < ROT_DIM), sin_hi, 0.0)
        finish()

    any_spec = pl.BlockSpec(memory_space=pl.ANY)
    return pl.pallas_call(
        body,
        name="prologue_all_gather_w_in",
        grid=(n_steps,),
        in_specs=[
            pl.BlockSpec((TM, D_MODEL), lambda i: (i, 0)),
            pl.BlockSpec((1, D_MODEL), lambda i: (0, 0)),
            pl.BlockSpec((2 * half, TM), lambda i: (0, i)),
            any_spec,
        ],
        out_specs=[
            pl.BlockSpec((TM, D_MODEL), lambda i: (i, 0)),
            pl.BlockSpec((TM, 384), lambda i: (i, 0)),
            any_spec,
        ],
        out_shape=[
            jax.ShapeDtypeStruct((S, D_MODEL), BF16),
            jax.ShapeDtypeStruct((S, 384), F32),
        ] + ag.out_shape(),
        scratch_shapes=ag.scratch_shapes(),
        compiler_params=_params(("arbitrary",)),
    )(x, norm_g, cs, w_shard)


def _fwd_proj(xn, wt, later):
    S = xn.shape[0]

    n_steps = S // TM
    ag = _AllGatherInSteps(later, forward_step=3)

    def body(xn_ref, wt_ref, *rest):
        later_refs, rest = rest[:ag.n], rest[ag.n:]
        pa_ref, pc_ref = rest[:2]
        gathered, ag_scratch = rest[2:2 + ag.n], rest[2 + ag.n:]
        step = pl.program_id(0)
        finish = ag.emit(step, n_steps, later_refs, gathered, ag_scratch)
        xn = xn_ref[...]
        pa_ref[:, 0:512] = _nt(xn, wt_ref[0:512, :]).astype(ACT)
        pa_ref[:, 512:1024] = _nt(xn, wt_ref[768:1280, :]).astype(ACT)
        pa_ref[:, 1024:1280] = _nt(xn, wt_ref[512:768, :]).astype(ACT)
        pc_ref[...] = _nt(xn, wt_ref[1280:3328, :]).astype(ACT)
        finish()

    any_spec = pl.BlockSpec(memory_space=pl.ANY)
    outs = pl.pallas_call(
        body,
        name="fwd_proj_all_gather",
        grid=(n_steps,),
        in_specs=[
            pl.BlockSpec((TM, D_MODEL), lambda i: (i, 0)),
            pl.BlockSpec((IN_W, D_MODEL), lambda i: (0, 0)),
        ] + [any_spec] * ag.n,
        out_specs=[
            pl.BlockSpec((TM, PA_W), lambda i: (i, 0)),
            pl.BlockSpec((TM, PC_W), lambda i: (i, 0)),
        ] + [any_spec] * ag.n,
        out_shape=[
            jax.ShapeDtypeStruct((S, PA_W), ACT),
            jax.ShapeDtypeStruct((S, PC_W), ACT),
        ] + ag.out_shape(),
        scratch_shapes=ag.scratch_shapes(),
        compiler_params=_params(("arbitrary",)),
    )(xn, wt, *later)
    return outs[0], outs[1], outs[2:]


def _rope(t, tab):
    return (t * tab[:, 0:128] + pltpu.roll(t, 120, 1) * tab[:, 128:256]
            + pltpu.roll(t, 8, 1) * tab[:, 256:384])


def _rope_t(d, tab):
    return (d * tab[:, 0:128] + pltpu.roll(d * tab[:, 128:256], 8, 1)
            + pltpu.roll(d * tab[:, 256:384], 120, 1))


def _fill_kv(kall, kvc_ref, kvp_ref, tabc_ref, tabp_ref):
    for lo, kv_ref, tab_ref, n in ((0, kvp_ref, tabp_ref, BLK), (BLK, kvc_ref, tabc_ref, TQ)):
        k = _rope(kv_ref[:, 0:128].astype(F32), tab_ref[...])
        v = kv_ref[:, 128:256].astype(F32)
        kall[0, lo:lo + n, :] = k.astype(BF16)
        kall[1, lo:lo + n, :] = pltpu.roll(k, 64, 1).astype(BF16)
        kall[2, lo:lo + n, :] = v.astype(BF16)
        kall[3, lo:lo + n, :] = pltpu.roll(v, 64, 1).astype(BF16)


HEADS = (((0, 0), (1, 0), (2, 1), (3, 1)), ((0, 1), (1, 1), (2, 0), (3, 0)))


def _upper():
    kj = lax.broadcasted_iota(jnp.int32, (BLK, 4 * BLK), 0)
    qi = lax.broadcasted_iota(jnp.int32, (BLK, 4 * BLK), 1) & (BLK - 1)
    return kj > qi


def _merge(upper, both):
    return jnp.where(upper, both[0:BLK, :], both[BLK:2 * BLK, :])


def _split_store(ref, s, upper_b, val):
    vb = val.astype(BF16)
    first = vb * upper_b
    ref[s, 0:BLK, :] = first
    ref[s, BLK:2 * BLK, :] = vb - first


def _sink_rows(sink_ref):
    return [jnp.concatenate([jnp.full((1, BLK), sink_ref[2 * p + e], F32) for p, e in HEADS[s]], axis=1)
            for s in range(2)]


def _stack_heads(ref, s, half, pairs):
    for a, (p, e) in enumerate(HEADS[s]):
        ref[s, a * BLK:(a + 1) * BLK, :] = jnp.where(half[e], pairs[p], 0.0).astype(BF16)


def _unstack_pair(half, outs, p):
    lo = 0 if p < 2 else 1
    rows = slice(p * BLK, (p + 1) * BLK)
    return jnp.where(half[0], outs[lo][rows, :], outs[1 - lo][rows, :])


def _softmax(sm, sinks):
    m = jnp.maximum(jnp.max(sm, axis=0, keepdims=True), sinks)
    p = jnp.exp(sm - m)
    es = jnp.exp(sinks - m)
    inv = 1.0 / (jnp.sum(p, axis=0, keepdims=True) + es)
    return p * inv, es * inv


def _scores(kk, q_stack, first):
    st = _nt(kk, q_stack)
    prev = st[0:BLK, :]
    if first is not None:
        prev = prev + jnp.where(first, -jnp.inf, 0.0)
    return prev, st[BLK:2 * BLK, :]


def _attn_specs(tile):
    nb = TQ // BLK
    prev = lambda i: jnp.maximum(tile(i) * nb - 1, 0)
    return [
        pl.BlockSpec(memory_space=pltpu.SMEM),
        pl.BlockSpec((TQ, ATTN_W), lambda i: (tile(i), 0)),
        pl.BlockSpec((TQ, ATTN_W), lambda i: (tile(i), 1)),
        pl.BlockSpec((TQ, 2 * KV_W), lambda i: (tile(i), 4)),
        pl.BlockSpec((BLK, 2 * KV_W), lambda i: (prev(i), 4)),
        pl.BlockSpec((TQ, 384), lambda i: (tile(i), 0)),
        pl.BlockSpec((BLK, 384), lambda i: (prev(i), 0)),
    ]


def _attn_fwd(pa, tab, sinks):
    S = pa.shape[0]
    nb = TQ // BLK

    def body(sink_ref, q_ref, g_ref, kvc_ref, kvp_ref, tabc_ref, tabp_ref, o_ref, kall, q_sc, p_sc):
        i = pl.program_id(0)
        _fill_kv(kall, kvc_ref, kvp_ref, tabc_ref, tabp_ref)
        lane = lax.broadcasted_iota(jnp.int32, (BLK, 128), 1)
        half = [lane < HEAD_DIM, lane >= HEAD_DIM]
        upper = _upper()
        upper_b = upper.astype(BF16)
        sinks = _sink_rows(sink_ref)
        for j in range(nb):
            rq = slice(j * BLK, (j + 1) * BLK)
            rk = slice(j * BLK, (j + 2) * BLK)
            tab = tabc_ref[rq, :]
            qr = [_rope(q_ref[rq, p * 128:(p + 1) * 128].astype(F32), tab) * 0.125 for p in range(4)]
            outs = []
            for s in range(2):
                _stack_heads(q_sc, s, half, qr)
                prev, cur = _scores(kall[s, rk, :], q_sc[s], i == 0 if j == 0 else None)
                prob, _ = _softmax(jnp.where(upper, prev, cur), sinks[s])
                _split_store(p_sc, s, upper_b, prob)
                outs.append(_tn(p_sc[s], kall[2 + s, rk, :]))
            for p in range(4):
                cols = slice(p * 128, (p + 1) * 128)
                o_ref[rq, cols] = (_unstack_pair(half, outs, p) * _silu(g_ref[rq, cols].astype(F32))).astype(BF16)

    return pl.pallas_call(
        body,
        name="attn_fwd",
        grid=(S // TQ,),
        in_specs=_attn_specs(lambda i: i),
        out_specs=pl.BlockSpec((TQ, ATTN_W), lambda i: (i, 0)),
        out_shape=jax.ShapeDtypeStruct((S, ATTN_W), BF16),
        scratch_shapes=[
            pltpu.VMEM((4, BLK + TQ, 128), BF16),
            pltpu.VMEM((2, 4 * BLK, 128), BF16),
            pltpu.VMEM((2, 2 * BLK, 4 * BLK), BF16),
        ],
        compiler_params=_params(("arbitrary",)),
    )(sinks, pa, pa, pa, pa, tab, tab)


def _shift_down(u, halo_ref, has_prev):
    def halo_u(r):
        hu = halo_ref[r:r + 1, 512:1024].astype(F32) * halo_ref[r:r + 1, 1024:1536].astype(F32)
        return jnp.where(has_prev, hu, 0.0)

    row = lax.broadcasted_iota(jnp.int32, u.shape, 0)
    um1 = jnp.where(row == 0, halo_u(HALO - 1), pltpu.roll(u, 1, 0))
    um2 = jnp.where(row == 0, halo_u(HALO - 2), jnp.where(row == 1, halo_u(HALO - 1), pltpu.roll(u, 2, 0)))
    return um1, um2


def _conv_tile(pc_ref, halo_ref, w_ref, has_prev):
    b = pc_ref[:, 0:512].astype(F32)
    c = pc_ref[:, 512:1024].astype(F32)
    hh = pc_ref[:, 1024:1536].astype(F32)
    gc = pc_ref[:, 1536:2048].astype(F32)
    u = c * hh
    um1, um2 = _shift_down(u, halo_ref, has_prev)
    cv = w_ref[0:1, :] * um2 + w_ref[1:2, :] * um1 + w_ref[2:3, :] * u
    return b, c, hh, gc, u, um1, um2, cv


def _prev_rows(width, col=0):
    return pl.BlockSpec((HALO, width), lambda i: (jnp.maximum(i * (TM // HALO) - 1, 0), col))


def _out_loss(x, target, ya, pc, conv_w, w_out, final_g):
    S = x.shape[0]

    def body(x_ref, t_ref, ya_ref, pc_ref, halo_ref, cw_ref, wo_ref, fg_ref,
             dh_ref, dmix_ref, gwo_ref, gfg_ref, loss_ref):
        @pl.when(pl.program_id(0) == 0)
        def _():
            gwo_ref[...] = jnp.zeros_like(gwo_ref)
            gfg_ref[...] = jnp.zeros_like(gfg_ref)
            loss_ref[...] = jnp.zeros_like(loss_ref)

        b, _, _, gc, _, _, _, cv = _conv_tile(pc_ref, halo_ref, cw_ref, pl.program_id(0) > 0)
        yc = (b * cv * _silu(gc)).astype(BF16)
        mix = jnp.concatenate([ya_ref[...], yc], axis=1)
        wo = wo_ref[...]
        fg = fg_ref[...]
        h = x_ref[...] + _nn(mix, wo)
        r = lax.rsqrt(jnp.mean(h * h, axis=-1, keepdims=True) + EPS)
        n = h * r
        err = n * fg - t_ref[...]
        loss_ref[...] += jnp.broadcast_to(
            0.5 * jnp.sum(jnp.mean(err * err, axis=-1, keepdims=True), axis=0, keepdims=True), (8, 128))
        dy = err * (1.0 / D_MODEL)
        gfg_ref[...] += jnp.sum(dy * n, axis=0, keepdims=True)
        dyg = dy * fg
        dh = r * (dyg - n * jnp.mean(dyg * n, axis=-1, keepdims=True))
        dh_ref[...] = dh
        dhb = dh.astype(BF16)
        dmix_ref[...] = _nt(dhb, wo).astype(ACT)
        gwo_ref[...] += _tn(mix, dhb)

    row = lambda i: (i, 0)
    fixed = lambda i: (0, 0)
    return pl.pallas_call(
        body,
        name="out_loss",
        grid=(S // TM,),
        in_specs=[
            pl.BlockSpec((TM, D_MODEL), row),
            pl.BlockSpec((TM, D_MODEL), row),
            pl.BlockSpec((TM, ATTN_W), row),
            pl.BlockSpec((TM, PC_W), row),
            _prev_rows(PC_W),
            pl.BlockSpec((CONV_K, CONV_W), fixed),
            pl.BlockSpec((D_MODEL, D_MODEL), fixed),
            pl.BlockSpec((1, D_MODEL), fixed),
        ],
        out_specs=[
            pl.BlockSpec((TM, D_MODEL), row),
            pl.BlockSpec((TM, D_MODEL), row),
            pl.BlockSpec((D_MODEL, D_MODEL), fixed),
            pl.BlockSpec((1, D_MODEL), fixed),
            pl.BlockSpec((8, 128), fixed),
        ],
        out_shape=[
            jax.ShapeDtypeStruct((S, D_MODEL), F32),
            jax.ShapeDtypeStruct((S, D_MODEL), ACT),
            jax.ShapeDtypeStruct((D_MODEL, D_MODEL), F32),
            jax.ShapeDtypeStruct((1, D_MODEL), F32),
            jax.ShapeDtypeStruct((8, 128), F32),
        ],
        compiler_params=_params(("arbitrary",)),
    )(x, target, ya, pc, pc, conv_w, w_out, final_g)


def _attn_bwd(pa, dmix, tab, sinks):
    S = pa.shape[0]
    nt = S // TQ
    nb = TQ // BLK

    def body(sink_ref, q_ref, g_ref, kvc_ref, kvp_ref, tabc_ref, tabp_ref, dm_ref,
             d_ref, dsink_ref, kall, dkv, carry, q_sc, do_sc, p_sc, ds_sc, dsink_acc):
        step = pl.program_id(0)
        i = nt - 1 - step

        @pl.when(step == 0)
        def _():
            carry[...] = jnp.zeros_like(carry)
            dsink_acc[...] = jnp.zeros_like(dsink_acc)

        _fill_kv(kall, kvc_ref, kvp_ref, tabc_ref, tabp_ref)
        dkv[0:TQ, :] = jnp.zeros((TQ, 2 * KV_W), F32)
        dkv[TQ:TQ + BLK, :] = carry[...]
        lane = lax.broadcasted_iota(jnp.int32, (BLK, 128), 1)
        half = [lane < HEAD_DIM, lane >= HEAD_DIM]
        upper = _upper()
        upper_b = upper.astype(BF16)
        sinks = _sink_rows(sink_ref)
        for j in range(nb):
            rq = slice(j * BLK, (j + 1) * BLK)
            rk = slice(j * BLK, (j + 2) * BLK)
            tab = tabc_ref[rq, :]
            pair = [slice(p * 128, (p + 1) * 128) for p in range(4)]
            qr = [_rope(q_ref[rq, c].astype(F32), tab) * 0.125 for c in pair]
            g = [g_ref[rq, c].astype(F32) for c in pair]
            da = [dm_ref[rq, c].astype(F32) for c in pair]
            do = [da[p] * _silu(g[p]) for p in range(4)]
            outs, dqs, dks, dvs = [], [], [], []
            for s in range(2):
                kk = kall[s, rk, :]
                vv = kall[2 + s, rk, :]
                _stack_heads(q_sc, s, half, qr)
                _stack_heads(do_sc, s, half, do)
                prev, cur = _scores(kk, q_sc[s], i == 0 if j == 0 else None)
                prob, psink = _softmax(jnp.where(upper, prev, cur), sinks[s])
                _split_store(p_sc, s, upper_b, prob)
                dprob = _merge(upper, _nt(vv, do_sc[s]))
                dsum = jnp.sum(dprob * prob, axis=0, keepdims=True)
                _split_store(ds_sc, s, upper_b, prob * (dprob - dsum))
                dsink_acc[s, 0:1, :] += psink * dsum
                outs.append(_tn(p_sc[s], vv))
                dqs.append(_tn(ds_sc[s], kk))
                dks.append(_nn(ds_sc[s], q_sc[s]))
                dvs.append(_nn(p_sc[s], do_sc[s]))
            for p in range(4):
                d_ref[rq, pair[p]] = _rope_t(_unstack_pair(half, dqs, p) * 0.125, tab).astype(BF16)
                d_ref[rq, 512 + p * 128:512 + (p + 1) * 128] = (
                    da[p] * _unstack_pair(half, outs, p) * _dsilu(g[p])).astype(BF16)
            dkv[rk, 0:128] += dks[0] + pltpu.roll(dks[1], 64, 1)
            dkv[rk, 128:256] += dvs[0] + pltpu.roll(dvs[1], 64, 1)
        d_ref[:, 1024:1152] = _rope_t(dkv[BLK:BLK + TQ, 0:128], tabc_ref[...]).astype(BF16)
        d_ref[:, 1152:1280] = dkv[BLK:BLK + TQ, 128:256].astype(BF16)
        carry[...] = dkv[0:BLK, :]

        @pl.when(step == nt - 1)
        def _():
            lanes = lax.broadcasted_iota(jnp.int32, (8, 128), 1)
            out = jnp.zeros((8, 128), F32)
            for s in range(2):
                for a, (p, e) in enumerate(HEADS[s]):
                    tot = jnp.sum(dsink_acc[s, 0:1, a * BLK:(a + 1) * BLK], axis=1, keepdims=True)
                    out = jnp.where(lanes == 2 * p + e, -tot, out)
            dsink_ref[...] = out

    rev = lambda s: nt - 1 - s
    return pl.pallas_call(
        body,
        name="attn_bwd",
        grid=(nt,),
        in_specs=_attn_specs(rev) + [pl.BlockSpec((TQ, ATTN_W), lambda s: (nt - 1 - s, 0))],
        out_specs=[
            pl.BlockSpec((TQ, PA_W), lambda s: (nt - 1 - s, 0)),
            pl.BlockSpec((8, 128), lambda s: (0, 0)),
        ],
        out_shape=[
            jax.ShapeDtypeStruct((S, PA_W), BF16),
            jax.ShapeDtypeStruct((8, 128), F32),
        ],
        scratch_shapes=[
            pltpu.VMEM((4, BLK + TQ, 128), BF16),
            pltpu.VMEM((BLK + TQ, 2 * KV_W), F32),
            pltpu.VMEM((BLK, 2 * KV_W), F32),
            pltpu.VMEM((2, 4 * BLK, 128), BF16),
            pltpu.VMEM((2, 4 * BLK, 128), BF16),
            pltpu.VMEM((2, 2 * BLK, 4 * BLK), BF16),
            pltpu.VMEM((2, 2 * BLK, 4 * BLK), BF16),
            pltpu.VMEM((2, 8, 4 * BLK), F32),
        ],
        compiler_params=_params(("arbitrary",)),
    )(sinks, pa, pa, pa, pa, tab, tab, dmix)


def _conv_bwd_tile(pc_ref, prev_ref, next_ref, dm_ref, dmn_ref, w_ref, d_ref, gw_ref, has_prev, has_next):
    rows = pc_ref.shape[0]
    w0, w1, w2 = w_ref[0:1, :], w_ref[1:2, :], w_ref[2:3, :]
    b, c, hh, gc, u, um1, um2, cv = _conv_tile(pc_ref, prev_ref, w_ref, has_prev)
    sg = _silu(gc)
    dy = dm_ref[...].astype(F32)
    dcv = dy * b * sg

    def next_dcv(r):
        nd = (dmn_ref[r:r + 1, :].astype(F32) * next_ref[r:r + 1, 0:512].astype(F32)
              * _silu(next_ref[r:r + 1, 1536:2048].astype(F32)))
        return jnp.where(has_next, nd, 0.0)

    row = lax.broadcasted_iota(jnp.int32, (rows, CONV_W), 0)
    dp1 = jnp.where(row == rows - 1, next_dcv(0), pltpu.roll(dcv, rows - 1, 0))
    dp2 = jnp.where(row == rows - 1, next_dcv(1),
                    jnp.where(row == rows - 2, next_dcv(0), pltpu.roll(dcv, rows - 2, 0)))
    du = w2 * dcv + w1 * dp1 + w0 * dp2
    d_ref[:, 0:512] = (dy * cv * sg).astype(BF16)
    d_ref[:, 512:1024] = (du * hh).astype(BF16)
    d_ref[:, 1024:1536] = (du * c).astype(BF16)
    d_ref[:, 1536:2048] = (dy * b * cv * _dsilu(gc)).astype(BF16)
    gw_ref[0:1, :] += jnp.sum(dcv * um2, axis=0, keepdims=True)
    gw_ref[1:2, :] += jnp.sum(dcv * um1, axis=0, keepdims=True)
    gw_ref[2:3, :] += jnp.sum(dcv * u, axis=0, keepdims=True)


def _grad_x(da, dc, wt, x, dh, norm_g, grads):
    S = x.shape[0]
    n_steps = S // TM
    rs = _ReduceScatter(grads)
    n_rs_out = len(rs.out_shape())

    def body(da_ref, dc_ref, wt_ref, x_ref, dh_ref, g_ref, *rest):
        grad_refs, rest = rest[:rs.n], rest[rs.n:]
        gx_ref, gng_ref = rest[:2]
        rs_out, rs_scratch = rest[2:2 + n_rs_out], rest[2 + n_rs_out:]
        step = pl.program_id(0)
        finish = rs.emit(step, n_steps, grad_refs, rs_out, rs_scratch)

        @pl.when(step == 0)
        def _():
            gng_ref[...] = jnp.zeros_like(gng_ref)

        dxn = (_nn(da_ref[:, 0:512], wt_ref[0:512, :]) + _nn(da_ref[:, 512:1024], wt_ref[768:1280, :])
               + _nn(da_ref[:, 1024:1280], wt_ref[512:768, :]) + _nn(dc_ref[...], wt_ref[1280:3328, :]))
        xv = x_ref[...]
        r = lax.rsqrt(jnp.mean(xv * xv, axis=-1, keepdims=True) + EPS)
        n = xv * r
        gng_ref[...] += jnp.sum(dxn * n, axis=0, keepdims=True)
        dxg = dxn * g_ref[...]
        gx_ref[...] = dh_ref[...] + r * (dxg - n * jnp.mean(dxg * n, axis=-1, keepdims=True))
        finish()

    row = lambda i: (i, 0)
    fixed = lambda i: (0, 0)
    any_spec = pl.BlockSpec(memory_space=pl.ANY)
    outs = pl.pallas_call(
        body,
        name="grad_x_reduce_scatter",
        grid=(n_steps,),
        in_specs=[
            pl.BlockSpec((TM, PA_W), row),
            pl.BlockSpec((TM, PC_W), row),
            pl.BlockSpec((IN_W, D_MODEL), fixed),
            pl.BlockSpec((TM, D_MODEL), row),
            pl.BlockSpec((TM, D_MODEL), row),
            pl.BlockSpec((1, D_MODEL), fixed),
        ] + [any_spec] * rs.n,
        out_specs=[pl.BlockSpec((TM, D_MODEL), row), pl.BlockSpec((1, D_MODEL), fixed)] + [any_spec] * n_rs_out,
        out_shape=[jax.ShapeDtypeStruct((S, D_MODEL), F32), jax.ShapeDtypeStruct((1, D_MODEL), F32)] + rs.out_shape(),
        scratch_shapes=rs.scratch_shapes(),
        compiler_params=_params(("arbitrary",)),
    )(da, dc, wt, x, dh, norm_g, *grads)
    return outs[0], outs[1], outs[2:2 + rs.n], outs[2 + rs.n:2 + 2 * rs.n]


def _grad_w_in(da, pc, dmix, conv_w, xn):
    S = xn.shape[0]
    nt = S // TM
    t16 = TM // HALO

    def body(da_ref, pc_ref, prev_ref, next_ref, dm_ref, dmn_ref, cw_ref, xn_ref, gw_ref, dc_ref, gcw_ref):
        i = pl.program_id(0)

        @pl.when(i == 0)
        def _():
            gw_ref[...] = jnp.zeros_like(gw_ref)
            gcw_ref[...] = jnp.zeros_like(gcw_ref)

        xn = xn_ref[...]
        gw_ref[0:512, :] += _tn(da_ref[:, 0:512], xn)
        gw_ref[768:1280, :] += _tn(da_ref[:, 512:1024], xn)
        gw_ref[512:768, :] += _tn(da_ref[:, 1024:1280], xn)
        _conv_bwd_tile(pc_ref, prev_ref, next_ref, dm_ref, dmn_ref, cw_ref, dc_ref, gcw_ref, i > 0, i < nt - 1)
        gw_ref[1280:3328, :] += _tn(dc_ref[...], xn)

    row = lambda i: (i, 0)
    fixed = lambda i: (0, 0)
    nxt = lambda i: jnp.minimum((i + 1) * t16, nt * t16 - 1)
    return pl.pallas_call(
        body,
        name="grad_w_in",
        grid=(nt,),
        in_specs=[
            pl.BlockSpec((TM, PA_W), row),
            pl.BlockSpec((TM, PC_W), row),
            _prev_rows(PC_W),
            pl.BlockSpec((HALO, PC_W), lambda i: (nxt(i), 0)),
            pl.BlockSpec((TM, CONV_W), lambda i: (i, 1)),
            pl.BlockSpec((HALO, CONV_W), lambda i: (nxt(i), 1)),
            pl.BlockSpec((CONV_K, CONV_W), fixed),
            pl.BlockSpec((TM, D_MODEL), row),
        ],
        out_specs=[
            pl.BlockSpec((IN_W, D_MODEL), fixed),
            pl.BlockSpec((TM, PC_W), row),
            pl.BlockSpec((CONV_K, CONV_W), fixed),
        ],
        out_shape=[
            jax.ShapeDtypeStruct((IN_W, D_MODEL), F32),
            jax.ShapeDtypeStruct((S, PC_W), BF16),
            jax.ShapeDtypeStruct((CONV_K, CONV_W), F32),
        ],
        compiler_params=_params(("arbitrary",)),
    )(da, pc, pc, pc, dmix, dmix, conv_w, xn)


def _adam_update(w, g, m, v):
    c1 = 1.0 - ADAM_B1 ** ADAM_STEP
    c2 = 1.0 - ADAM_B2 ** ADAM_STEP
    nm = ADAM_B1 * m + (1.0 - ADAM_B1) * g
    nv = ADAM_B2 * v + (1.0 - ADAM_B2) * (g * g)
    return -ADAM_LR * ((nm / c1) / (jnp.sqrt(nv / c2) + ADAM_EPS) + ADAM_WD * w), nm, nv


def _sum_chips_adamw(own, others, w, m, v, name):
    def body(own_ref, p_ref, w_ref, m_ref, v_ref, g_ref, d_ref, nm_ref, nv_ref):
        g = own_ref[...]
        for k in range(N_CHIP - 1):
            g = g + p_ref[k].astype(F32)
        g_ref[...] = g
        d_ref[...], nm_ref[...], nv_ref[...] = _adam_update(w_ref[...], g, m_ref[...], v_ref[...])

    shape = jax.ShapeDtypeStruct(w.shape, F32)
    return pl.pallas_call(
        body,
        name=name,
        out_shape=[shape] * 4,
        compiler_params=_params(),
    )(own, others, w, m, v)


def _small_adamw(parts, params):
    def body(ng_all, fg_all, sk_all, loss_all, cw_all, *rest):
        prm, outs, (sk_sum, loss_sum, cw_sum) = rest[:12], rest[12:29], rest[29:]
        me = 4 * lax.axis_index("x") + 2 * lax.axis_index("y") + lax.axis_index("c")

        def total(ref):
            acc = ref[0]
            for d in range(1, N_DEV):
                acc = acc + ref[d]
            return acc

        sk_sum[...] = total(sk_all)
        loss_sum[...] = total(loss_all)
        cw_sum[...] = total(cw_all)
        grads = (total(ng_all), total(fg_all), sk_sum[0:1, 0:8],
                 cw_sum[pl.ds(pl.multiple_of(me * 8, 8), CONV_K), 0:64])
        outs[0][...] = loss_sum[0:1, 0:1]
        for k, g in enumerate(grads):
            w_ref, m_ref, v_ref = prm[3 * k:3 * k + 3]
            g_ref, d_ref, nm_ref, nv_ref = outs[1 + 4 * k:5 + 4 * k]
            g_ref[...] = g
            d_ref[...], nm_ref[...], nv_ref[...] = _adam_update(w_ref[...], g, m_ref[...], v_ref[...])

    flat = [a for p in params for a in p]
    out_shape = [jax.ShapeDtypeStruct((1, 1), F32)]
    for p in params:
        out_shape += [jax.ShapeDtypeStruct(p[0].shape, F32)] * 4
    return pl.pallas_call(
        body,
        name="adamw_small",
        out_shape=out_shape,
        scratch_shapes=[pltpu.VMEM((8, 128), F32), pltpu.VMEM((8, 128), F32), pltpu.VMEM((N_DEV * 8, 128), F32)],
        compiler_params=_params(),
    )(*parts, *flat)


def kernel(x, norm_g, w_in, sinks, conv_w, w_out, final_g, loss_target, m_norm_g, m_w_in, m_sinks, m_conv_w, m_w_out, m_final_g, v_norm_g, v_w_in, v_sinks, v_conv_w, v_w_out, v_final_g):
    S = x.shape[1]
    x2 = x.reshape(S, D_MODEL)
    t2 = loss_target.reshape(S, D_MODEL)
    ng = norm_g.reshape(1, D_MODEL)
    fg = final_g.reshape(1, D_MODEL)

    cw_pad = jnp.zeros((8, 128), F32).at[0:CONV_K, 0:64].set(conv_w)
    xn, tab, wt = _prologue(x2, ng, w_in.T.astype(BF16))
    pa, pc, (wo, cw_all) = _fwd_proj(xn, wt, [w_out.astype(BF16), cw_pad])
    cw = cw_all.reshape(N_DEV, 8, 128)[:, 0:CONV_K, 0:64].transpose(1, 0, 2).reshape(CONV_K, CONV_W)
    ya = _attn_fwd(pa, tab, sinks)
    dh, dmix, g_wo, g_fg, loss_part = _out_loss(x2, t2, ya, pc, cw, wo, fg)
    da, g_sinks = _attn_bwd(pa, dmix, tab, sinks)
    g_wt, dc, g_cw = _grad_w_in(da, pc, dmix, cw, xn)
    grad_x, g_ng, own, others = _grad_x(
        da, dc, wt, x2, dh, ng,
        [g_wt.reshape(N_DEV, SHARD_IN, D_MODEL), g_wo.reshape(N_DEV, SHARD_OUT, D_MODEL)])
    gt, dt, nmt, nvt = _sum_chips_adamw(own[0], others[0], w_in.T, m_w_in.T, v_w_in.T, "adamw_w_in")
    grad_w_in, d_w_in, nm_w_in, nv_w_in = gt.T, dt.T, nmt.T, nvt.T
    grad_w_out, d_w_out, nm_w_out, nv_w_out = _sum_chips_adamw(
        own[1], others[1], w_out, m_w_out, v_w_out, "adamw_w_out")
    cw_pack = jnp.pad(g_cw.reshape(CONV_K, N_DEV, 64).transpose(1, 0, 2),
                      ((0, 0), (0, 8 - CONV_K), (0, 64))).reshape(N_DEV * 8, 128)
    gathered = _all_gather([g_ng.reshape(8, 128), g_fg.reshape(8, 128), g_sinks, loss_part, cw_pack],
                           "all_gather_small_grads")
    parts = [a.reshape(N_DEV, a.shape[0] // N_DEV, 128) for a in gathered]
    vec = lambda a: a.reshape(8, 128)
    row = lambda a: a.reshape(1, 8)
    res = _small_adamw(parts, [
        (vec(norm_g), vec(m_norm_g), vec(v_norm_g)), (vec(final_g), vec(m_final_g), vec(v_final_g)),
        (row(sinks), row(m_sinks), row(v_sinks)), (conv_w, m_conv_w, v_conv_w)])
    loss = res[0].reshape(())
    grad_norm_g, d_ng, nm_ng, nv_ng = [a.reshape(D_MODEL) for a in res[1:5]]
    grad_final_g, d_fg, nm_fg, nv_fg = [a.reshape(D_MODEL) for a in res[5:9]]
    grad_sinks, d_sk, nm_sk, nv_sk = [a.reshape(N_Q_HEADS) for a in res[9:13]]
    grad_conv_w, d_cw, nm_cw, nv_cw = res[13:17]

    return (loss, grad_x.reshape(1, S, D_MODEL), grad_norm_g, grad_w_in, grad_sinks, grad_conv_w, grad_w_out, grad_final_g,
            d_ng, d_w_in, d_sk, d_cw, d_w_out, d_fg,
            nm_ng, nm_w_in, nm_sk, nm_cw, nm_w_out, nm_fg,
            nv_ng, nv_w_in, nv_sk, nv_cw, nv_w_out, nv_fg)
```

```python
import numpy as np
import jax
import jax.numpy as jnp
from jax import lax
from jax.experimental import pallas as pl
from jax.experimental.pallas import tpu as pltpu

F32 = jnp.float32
BF16 = jnp.bfloat16
MESH = pl.DeviceIdType.MESH

D_MODEL = 1024
HEAD_DIM = 64
N_Q_HEADS = 8
GROUP = 4
ATTN_W = 512
KV_W = 128
BLK = 128
CONV_W = 512
CONV_K = 3
IN_W = 3328
PA_W = 1280
PC_W = 2048
EPS = 1e-5
ROPE_THETA = 500000.0
ROT_DIM = 16
N_DEV = 8
N_CHIP = 4
SHARD_IN = IN_W // N_DEV
SHARD_OUT = D_MODEL // N_DEV

ADAM_LR = 0.001
ADAM_B1 = 0.9
ADAM_B2 = 0.999
ADAM_EPS = 1e-08
ADAM_WD = 0.01
ADAM_STEP = 10

ACT = jnp.bfloat16

TM = 512
TQ = 1024
HALO = 16
VMEM_LIMIT = 56 * 1024 * 1024

NT_DIMS = (((1,), (1,)), ((), ()))
TN_DIMS = (((0,), (0,)), ((), ()))


def _params(sem=None):
    kw = dict(vmem_limit_bytes=VMEM_LIMIT)
    if sem is not None:
        kw["dimension_semantics"] = sem
    return pltpu.CompilerParams(**kw)


def _nt(a, b):
    return lax.dot_general(a, b, NT_DIMS, preferred_element_type=F32)


def _tn(a, b):
    return lax.dot_general(a, b, TN_DIMS, preferred_element_type=F32)


def _nn(a, b):
    return jnp.dot(a, b, preferred_element_type=F32)


def _silu(g):
    return g * jax.nn.sigmoid(g)


def _dsilu(g):
    s = jax.nn.sigmoid(g)
    return s * (1.0 + g * (1.0 - s))


def _all_gather(arrs, name):
    n_arr = len(arrs)

    def body(*refs):
        x_refs = refs[:n_arr]
        out_refs = refs[n_arr:2 * n_arr]
        send_sems, recv_sems, local_sems = refs[2 * n_arr:]
        x, y, c = lax.axis_index("x"), lax.axis_index("y"), lax.axis_index("c")
        me, sibling = (x, y, c), (x, y, 1 - c)
        chips = [(1 - x, y), (x, 1 - y), (1 - x, 1 - y)]

        def rows(a, px, py, pc):
            m = x_refs[a].shape[0]
            return out_refs[a].at[pl.ds((4 * px + 2 * py + pc) * m, m), :]

        def copy(a, k, block, to, src=None):
            return pltpu.make_async_remote_copy(
                src_ref=rows(a, *block) if src is None else src,
                dst_ref=rows(a, *block),
                send_sem=send_sems.at[a * 7 + k],
                recv_sem=recv_sems.at[a * 7 + k],
                device_id=to,
                device_id_type=MESH,
            )

        mine = [pltpu.make_async_copy(x_refs[a], rows(a, *me), local_sems.at[a]) for a in range(n_arr)]
        for cp in mine:
            cp.start()
        first = []
        for a in range(n_arr):
            first.append(copy(a, 0, me, sibling, src=x_refs[a]))
            first += [copy(a, 1 + j, me, (*chip, c), src=x_refs[a]) for j, chip in enumerate(chips)]
        for cp in first:
            cp.start()
        passed = []
        for j, chip in enumerate(chips):
            for a in range(n_arr):
                copy(a, 1 + j, (*chip, c), me).wait_recv()
                fwd = copy(a, 4 + j, (*chip, c), sibling)
                fwd.start()
                passed.append(fwd)
        for a in range(n_arr):
            copy(a, 0, sibling, me).wait_recv()
            for j, chip in enumerate(chips):
                copy(a, 4 + j, (*chip, 1 - c), me).wait_recv()
        for cp in first + passed:
            cp.wait_send()
        for cp in mine:
            cp.wait()

    vmem = pl.BlockSpec(memory_space=pltpu.VMEM)
    return pl.pallas_call(
        body,
        name=name,
        out_shape=[jax.ShapeDtypeStruct((N_DEV * a.shape[0], a.shape[1]), a.dtype) for a in arrs],
        in_specs=[vmem] * n_arr,
        out_specs=[vmem] * n_arr,
        scratch_shapes=[
            pltpu.SemaphoreType.DMA((7 * n_arr,)),
            pltpu.SemaphoreType.DMA((7 * n_arr,)),
            pltpu.SemaphoreType.DMA((n_arr,)),
        ],
        compiler_params=_params(),
    )(*arrs)


class _AllGatherInSteps:
    def __init__(self, arrs, forward_step):
        self.blocks = [(a.shape, a.dtype) for a in arrs]
        self.n = len(arrs)
        self.forward_step = forward_step

    def out_shape(self):
        return [jax.ShapeDtypeStruct((N_DEV * s[0], s[1]), d) for s, d in self.blocks]

    def scratch_shapes(self):
        return [pltpu.SemaphoreType.DMA((7 * self.n,)), pltpu.SemaphoreType.DMA((7 * self.n,)),
                pltpu.SemaphoreType.DMA((self.n,))]

    def emit(self, step, n_steps, x_refs, out_refs, scratch):
        assert n_steps > self.forward_step + 1
        send_sems, recv_sems, local_sems = scratch
        x, y, c = lax.axis_index("x"), lax.axis_index("y"), lax.axis_index("c")
        me, sibling = (x, y, c), (x, y, 1 - c)
        chips = [(1 - x, y), (x, 1 - y), (1 - x, 1 - y)]

        def rows(a, px, py, pc):
            m = self.blocks[a][0][0]
            return out_refs[a].at[pl.ds((4 * px + 2 * py + pc) * m, m), :]

        def copy(a, k, block, to, src=None):
            return pltpu.make_async_remote_copy(
                src_ref=rows(a, *block) if src is None else src, dst_ref=rows(a, *block),
                send_sem=send_sems.at[a * 7 + k], recv_sem=recv_sems.at[a * 7 + k],
                device_id=to, device_id_type=MESH)

        def mine(a):
            return pltpu.make_async_copy(x_refs[a], rows(a, *me), local_sems.at[a])

        def first(a):
            return ([copy(a, 0, me, sibling, src=x_refs[a])]
                    + [copy(a, 1 + j, me, (*chip, c), src=x_refs[a]) for j, chip in enumerate(chips)])

        def passed(a):
            return [copy(a, 4 + j, (*chip, c), sibling) for j, chip in enumerate(chips)]

        @pl.when(step == 0)
        def _():
            for a in range(self.n):
                mine(a).start()
                for cp in first(a):
                    cp.start()

        @pl.when(step == self.forward_step)
        def _():
            for j, chip in enumerate(chips):
                for a in range(self.n):
                    copy(a, 1 + j, (*chip, c), me).wait_recv()
                    copy(a, 4 + j, (*chip, c), sibling).start()

        def finish():
            @pl.when(step == n_steps - 1)
            def _():
                for a in range(self.n):
                    copy(a, 0, sibling, me).wait_recv()
                    for j, chip in enumerate(chips):
                        copy(a, 4 + j, (*chip, 1 - c), me).wait_recv()
                    for cp in first(a) + passed(a):
                        cp.wait_send()
                    mine(a).wait()

        return finish


class _ReduceScatter:
    def __init__(self, grads):
        self.shapes = [g.shape[1:] for g in grads]
        self.n = len(grads)
        self.items = tuple((a, r) for r in (1, 2, 3, 0) for a in range(self.n))
        self.steps = len(self.items) + 2

    def out_shape(self):
        own = [jax.ShapeDtypeStruct(s, F32) for s in self.shapes]
        ici = [jax.ShapeDtypeStruct((N_CHIP - 1,) + s, BF16) for s in self.shapes]
        land = [jax.ShapeDtypeStruct((N_CHIP,) + s, F32) for s in self.shapes]
        return own + ici + land

    def scratch_shapes(self):
        n_items = len(self.items)
        return ([pltpu.VMEM((2,) + s, F32) for s in self.shapes]
                + [pltpu.VMEM((N_CHIP - 1,) + s, BF16) for s in self.shapes]
                + [pltpu.VMEM(s, F32) for s in self.shapes]
                + [pltpu.SemaphoreType.DMA((self.n * N_CHIP,))] * 2
                + [pltpu.SemaphoreType.DMA((2 * n_items,))]
                + [pltpu.SemaphoreType.DMA((self.n * (N_CHIP - 1),))] * 2
                + [pltpu.SemaphoreType.DMA((self.n,))])

    def emit(self, step, n_steps, g_refs, out_refs, scratch):
        assert n_steps > self.steps
        n = self.n
        own_refs, ici_refs, land_refs = out_refs[:n], out_refs[n:2 * n], out_refs[2 * n:]
        stage, pair_bf, pair_own = scratch[:n], scratch[n:2 * n], scratch[2 * n:3 * n]
        sib_send, sib_recv, load_sems, ici_send, ici_recv, own_sems = scratch[3 * n:]
        x, y, c = lax.axis_index("x"), lax.axis_index("y"), lax.axis_index("c")

        def chip_of(r):
            return (x ^ (r >> 1), y ^ (r & 1))

        def block_of(r, core):
            cx, cy = chip_of(r)
            return 4 * cx + 2 * cy + core

        def to_sibling(a, r):
            return pltpu.make_async_remote_copy(
                src_ref=g_refs[a].at[block_of(r, 1 - c)], dst_ref=land_refs[a].at[r],
                send_sem=sib_send.at[a * N_CHIP + r], recv_sem=sib_recv.at[a * N_CHIP + r],
                device_id=(x, y, 1 - c), device_id_type=MESH)

        def loads(k):
            a, r = self.items[k]
            return (pltpu.make_async_copy(g_refs[a].at[block_of(r, c)], stage[a].at[0], load_sems.at[2 * k]),
                    pltpu.make_async_copy(land_refs[a].at[r], stage[a].at[1], load_sems.at[2 * k + 1]))

        def to_owner(k):
            a, r = self.items[k]
            if r == 0:
                return pltpu.make_async_copy(pair_own[a], own_refs[a], own_sems.at[a])
            return pltpu.make_async_remote_copy(
                src_ref=pair_bf[a].at[r - 1], dst_ref=ici_refs[a].at[r - 1],
                send_sem=ici_send.at[a * (N_CHIP - 1) + r - 1], recv_sem=ici_recv.at[a * (N_CHIP - 1) + r - 1],
                device_id=(*chip_of(r), c), device_id_type=MESH)

        @pl.when(step == 0)
        def _():
            for a, r in self.items:
                to_sibling(a, r).start()

        for k, (a, r) in enumerate(self.items):
            @pl.when(step == 1 + k)
            def _(k=k, a=a, r=r):
                to_sibling(a, r).wait_recv()
                for cp in loads(k):
                    cp.start()

            @pl.when(step == 2 + k)
            def _(k=k, a=a, r=r):
                for cp in loads(k):
                    cp.wait()
                total = stage[a][0] + stage[a][1]
                if r == 0:
                    pair_own[a][...] = total
                else:
                    pair_bf[a][r - 1] = total.astype(BF16)
                to_owner(k).start()

        def finish():
            @pl.when(step == n_steps - 1)
            def _():
                for k, (a, r) in enumerate(self.items):
                    if r == 0:
                        to_owner(k).wait()
                    else:
                        to_owner(k).wait_send()
                        to_owner(k).wait_recv()
                for a, r in self.items:
                    to_sibling(a, r).wait_send()

        return finish


def _prologue(x, norm_g, w_shard):
    S = x.shape[0]
    n_steps = S // TM
    half = ROT_DIM // 2
    pos = jnp.arange(S, dtype=jnp.int32).astype(F32)
    inv_freq = ROPE_THETA ** (-jnp.arange(0, ROT_DIM, 2, dtype=F32) / ROT_DIM)
    ang = inv_freq[:, None] * pos[None, :]
    cs = jnp.concatenate([jnp.cos(ang), jnp.sin(ang)], axis=0)
    ag = _AllGatherInSteps([w_shard], forward_step=n_steps - 3)

    def body(x_ref, g_ref, cs_ref, w_ref, xn_ref, tab_ref, wt_ref, *ag_scratch):
        step = pl.program_id(0)
        finish = ag.emit(step, n_steps, [w_ref], [wt_ref], ag_scratch)
        xv = x_ref[...]
        r = lax.rsqrt(jnp.mean(xv * xv, axis=-1, keepdims=True) + EPS)
        xn_ref[...] = (xv * r * g_ref[...]).astype(BF16)

        xt = jnp.concatenate([cs_ref[...], jnp.zeros((128 - 2 * half, TM), F32)], axis=0).T
        lane = lax.broadcasted_iota(jnp.int32, (TM, 128), 1)
        rr = lane & (HEAD_DIM - 1)
        first = lane < HEAD_DIM

        def at(shift_first, shift_second):
            return jnp.where(first, pltpu.roll(xt, shift_first, 1) if shift_first else xt,
                             pltpu.roll(xt, shift_second, 1))

        cos_lo, cos_hi = at(0, HEAD_DIM), at(half, HEAD_DIM + half)
        sin_lo, sin_hi = at(128 - half, HEAD_DIM - half), at(0, HEAD_DIM)
        tab_ref[:, 0:128] = jnp.where(rr < half, cos_lo, jnp.where(rr < ROT_DIM, cos_hi, 1.0))
        tab_ref[:, 128:256] = jnp.where(rr < half, -sin_lo, 0.0)
        tab_ref[:, 256:384] = jnp.where((rr >= half) & (rr < ROT_DIM), sin_hi, 0.0)
        finish()

    any_spec = pl.BlockSpec(memory_space=pl.ANY)
    return pl.pallas_call(
        body,
        name="prologue_all_gather_w_in",
        grid=(n_steps,),
        in_specs=[
            pl.BlockSpec((TM, D_MODEL), lambda i: (i, 0)),
            pl.BlockSpec((1, D_MODEL), lambda i: (0, 0)),
            pl.BlockSpec((2 * half, TM), lambda i: (0, i)),
            any_spec,
        ],
        out_specs=[
            pl.BlockSpec((TM, D_MODEL), lambda i: (i, 0)),
            pl.BlockSpec((TM, 384), lambda i: (i, 0)),
            any_spec,
        ],
        out_shape=[
            jax.ShapeDtypeStruct((S, D_MODEL), BF16),
            jax.ShapeDtypeStruct((S, 384), F32),
        ] + ag.out_shape(),
        scratch_shapes=ag.scratch_shapes(),
        compiler_params=_params(("arbitrary",)),
    )(x, norm_g, cs, w_shard)


def _fwd_proj(xn, wt, later):
    S = xn.shape[0]

    n_steps = S // TM
    ag = _AllGatherInSteps(later, forward_step=3)

    def body(xn_ref, wt_ref, *rest):
        later_refs, rest = rest[:ag.n], rest[ag.n:]
        pa_ref, pc_ref = rest[:2]
        gathered, ag_scratch = rest[2:2 + ag.n], rest[2 + ag.n:]
        step = pl.program_id(0)
        finish = ag.emit(step, n_steps, later_refs, gathered, ag_scratch)
        xn = xn_ref[...]
        pa_ref[:, 0:512] = _nt(xn, wt_ref[0:512, :]).astype(ACT)
        pa_ref[:, 512:1024] = _nt(xn, wt_ref[768:1280, :]).astype(ACT)
        pa_ref[:, 1024:1280] = _nt(xn, wt_ref[512:768, :]).astype(ACT)
        pc_ref[...] = _nt(xn, wt_ref[1280:3328, :]).astype(ACT)
        finish()

    any_spec = pl.BlockSpec(memory_space=pl.ANY)
    outs = pl.pallas_call(
        body,
        name="fwd_proj_all_gather",
        grid=(n_steps,),
        in_specs=[
            pl.BlockSpec((TM, D_MODEL), lambda i: (i, 0)),
            pl.BlockSpec((IN_W, D_MODEL), lambda i: (0, 0)),
        ] + [any_spec] * ag.n,
        out_specs=[
            pl.BlockSpec((TM, PA_W), lambda i: (i, 0)),
            pl.BlockSpec((TM, PC_W), lambda i: (i, 0)),
        ] + [any_spec] * ag.n,
        out_shape=[
            jax.ShapeDtypeStruct((S, PA_W), ACT),
            jax.ShapeDtypeStruct((S, PC_W), ACT),
        ] + ag.out_shape(),
        scratch_shapes=ag.scratch_shapes(),
        compiler_params=_params(("arbitrary",)),
    )(xn, wt, *later)
    return outs[0], outs[1], outs[2:]


def _rope(t, tab):
    return (t * tab[:, 0:128] + pltpu.roll(t, 120, 1) * tab[:, 128:256]
            + pltpu.roll(t, 8, 1) * tab[:, 256:384])


def _rope_t(d, tab):
    return (d * tab[:, 0:128] + pltpu.roll(d * tab[:, 128:256], 8, 1)
            + pltpu.roll(d * tab[:, 256:384], 120, 1))


def _fill_kv(kall, kvc_ref, kvp_ref, tabc_ref, tabp_ref):
    for lo, kv_ref, tab_ref, n in ((0, kvp_ref, tabp_ref, BLK), (BLK, kvc_ref, tabc_ref, TQ)):
        k = _rope(kv_ref[:, 0:128].astype(F32), tab_ref[...])
        v = kv_ref[:, 128:256].astype(F32)
        kall[0, lo:lo + n, :] = k.astype(BF16)
        kall[1, lo:lo + n, :] = pltpu.roll(k, 64, 1).astype(BF16)
        kall[2, lo:lo + n, :] = v.astype(BF16)
        kall[3, lo:lo + n, :] = pltpu.roll(v, 64, 1).astype(BF16)


HEADS = (((0, 0), (1, 0), (2, 1), (3, 1)), ((0, 1), (1, 1), (2, 0), (3, 0)))


def _upper():
    kj = lax.broadcasted_iota(jnp.int32, (BLK, 4 * BLK), 0)
    qi = lax.broadcasted_iota(jnp.int32, (BLK, 4 * BLK), 1) & (BLK - 1)
    return kj > qi


def _merge(upper, both):
    return jnp.where(upper, both[0:BLK, :], both[BLK:2 * BLK, :])


def _split_store(ref, s, upper_b, val):
    vb = val.astype(BF16)
    first = vb * upper_b
    ref[s, 0:BLK, :] = first
    ref[s, BLK:2 * BLK, :] = vb - first


def _sink_rows(sink_ref):
    return [jnp.concatenate([jnp.full((1, BLK), sink_ref[2 * p + e], F32) for p, e in HEADS[s]], axis=1)
            for s in range(2)]


def _stack_heads(ref, s, half, pairs):
    for a, (p, e) in enumerate(HEADS[s]):
        ref[s, a * BLK:(a + 1) * BLK, :] = jnp.where(half[e], pairs[p], 0.0).astype(BF16)


def _unstack_pair(half, outs, p):
    lo = 0 if p < 2 else 1
    rows = slice(p * BLK, (p + 1) * BLK)
    return jnp.where(half[0], outs[lo][rows, :], outs[1 - lo][rows, :])


def _softmax(sm, sinks):
    m = jnp.maximum(jnp.max(sm, axis=0, keepdims=True), sinks)
    p = jnp.exp(sm - m)
    es = jnp.exp(sinks - m)
    inv = 1.0 / (jnp.sum(p, axis=0, keepdims=True) + es)
    return p * inv, es * inv


def _scores(kk, q_stack, first):
    st = _nt(kk, q_stack)
    prev = st[0:BLK, :]
    if first is not None:
        prev = prev + jnp.where(first, -jnp.inf, 0.0)
    return prev, st[BLK:2 * BLK, :]


def _attn_specs(tile):
    nb = TQ // BLK
    prev = lambda i: jnp.maximum(tile(i) * nb - 1, 0)
    return [
        pl.BlockSpec(memory_space=pltpu.SMEM),
        pl.BlockSpec((TQ, ATTN_W), lambda i: (tile(i), 0)),
        pl.BlockSpec((TQ, ATTN_W), lambda i: (tile(i), 1)),
        pl.BlockSpec((TQ, 2 * KV_W), lambda i: (tile(i), 4)),
        pl.BlockSpec((BLK, 2 * KV_W), lambda i: (prev(i), 4)),
        pl.BlockSpec((TQ, 384), lambda i: (tile(i), 0)),
        pl.BlockSpec((BLK, 384), lambda i: (prev(i), 0)),
    ]


def _attn_fwd(pa, tab, sinks):
    S = pa.shape[0]
    nb = TQ // BLK

    def body(sink_ref, q_ref, g_ref, kvc_ref, kvp_ref, tabc_ref, tabp_ref, o_ref, att_ref, kall, q_sc, p_sc):
        i = pl.program_id(0)
        _fill_kv(kall, kvc_ref, kvp_ref, tabc_ref, tabp_ref)
        lane = lax.broadcasted_iota(jnp.int32, (BLK, 128), 1)
        half = [lane < HEAD_DIM, lane >= HEAD_DIM]
        upper = _upper()
        upper_b = upper.astype(BF16)
        sinks = _sink_rows(sink_ref)
        for j in range(nb):
            rq = slice(j * BLK, (j + 1) * BLK)
            rk = slice(j * BLK, (j + 2) * BLK)
            tab = tabc_ref[rq, :]
            qr = [_rope(q_ref[rq, p * 128:(p + 1) * 128].astype(F32), tab) * 0.125 for p in range(4)]
            outs = []
            for s in range(2):
                _stack_heads(q_sc, s, half, qr)
                prev, cur = _scores(kall[s, rk, :], q_sc[s], i == 0 if j == 0 else None)
                prob, _ = _softmax(jnp.where(upper, prev, cur), sinks[s])
                _split_store(p_sc, s, upper_b, prob)
                outs.append(_tn(p_sc[s], kall[2 + s, rk, :]))
            for p in range(4):
                cols = slice(p * 128, (p + 1) * 128)
                att = _unstack_pair(half, outs, p)
                att_ref[rq, cols] = att.astype(BF16)
                o_ref[rq, cols] = (att * _silu(g_ref[rq, cols].astype(F32))).astype(BF16)

    return pl.pallas_call(
        body,
        name="attn_fwd",
        grid=(S // TQ,),
        in_specs=_attn_specs(lambda i: i),
        out_specs=[pl.BlockSpec((TQ, ATTN_W), lambda i: (i, 0))] * 2,
        out_shape=[jax.ShapeDtypeStruct((S, ATTN_W), BF16)] * 2,
        scratch_shapes=[
            pltpu.VMEM((4, BLK + TQ, 128), BF16),
            pltpu.VMEM((2, 4 * BLK, 128), BF16),
            pltpu.VMEM((2, 2 * BLK, 4 * BLK), BF16),
        ],
        compiler_params=_params(("arbitrary",)),
    )(sinks, pa, pa, pa, pa, tab, tab)


def _shift_down(u, halo_ref, has_prev):
    def halo_u(r):
        hu = halo_ref[r:r + 1, 512:1024].astype(F32) * halo_ref[r:r + 1, 1024:1536].astype(F32)
        return jnp.where(has_prev, hu, 0.0)

    row = lax.broadcasted_iota(jnp.int32, u.shape, 0)
    um1 = jnp.where(row == 0, halo_u(HALO - 1), pltpu.roll(u, 1, 0))
    um2 = jnp.where(row == 0, halo_u(HALO - 2), jnp.where(row == 1, halo_u(HALO - 1), pltpu.roll(u, 2, 0)))
    return um1, um2


def _conv_tile(pc_ref, halo_ref, w_ref, has_prev):
    b = pc_ref[:, 0:512].astype(F32)
    c = pc_ref[:, 512:1024].astype(F32)
    hh = pc_ref[:, 1024:1536].astype(F32)
    gc = pc_ref[:, 1536:2048].astype(F32)
    u = c * hh
    um1, um2 = _shift_down(u, halo_ref, has_prev)
    cv = w_ref[0:1, :] * um2 + w_ref[1:2, :] * um1 + w_ref[2:3, :] * u
    return b, c, hh, gc, u, um1, um2, cv


def _prev_rows(width, col=0):
    return pl.BlockSpec((HALO, width), lambda i: (jnp.maximum(i * (TM // HALO) - 1, 0), col))


def _out_loss(x, target, ya, pc, conv_w, w_out, final_g):
    S = x.shape[0]

    def body(x_ref, t_ref, ya_ref, pc_ref, halo_ref, cw_ref, wo_ref, fg_ref,
             dh_ref, dmix_ref, gwo_ref, gfg_ref, loss_ref):
        @pl.when(pl.program_id(0) == 0)
        def _():
            gwo_ref[...] = jnp.zeros_like(gwo_ref)
            gfg_ref[...] = jnp.zeros_like(gfg_ref)
            loss_ref[...] = jnp.zeros_like(loss_ref)

        b, _, _, gc, _, _, _, cv = _conv_tile(pc_ref, halo_ref, cw_ref, pl.program_id(0) > 0)
        yc = (b * cv * _silu(gc)).astype(BF16)
        mix = jnp.concatenate([ya_ref[...], yc], axis=1)
        wo = wo_ref[...]
        fg = fg_ref[...]
        h = x_ref[...] + _nn(mix, wo)
        r = lax.rsqrt(jnp.mean(h * h, axis=-1, keepdims=True) + EPS)
        n = h * r
        err = n * fg - t_ref[...]
        loss_ref[...] += jnp.broadcast_to(
            0.5 * jnp.sum(jnp.mean(err * err, axis=-1, keepdims=True), axis=0, keepdims=True), (8, 128))
        gfg_ref[...] += jnp.sum(err * n, axis=0, keepdims=True) * (1.0 / D_MODEL)
        dyg = err * (fg * (1.0 / D_MODEL))
        dh = r * (dyg - n * jnp.mean(dyg * n, axis=-1, keepdims=True))
        dh_ref[...] = dh
        dhb = dh.astype(BF16)
        dmix_ref[...] = _nt(dhb, wo).astype(ACT)
        gwo_ref[...] += _tn(mix, dhb)

    row = lambda i: (i, 0)
    fixed = lambda i: (0, 0)
    return pl.pallas_call(
        body,
        name="out_loss",
        grid=(S // TM,),
        in_specs=[
            pl.BlockSpec((TM, D_MODEL), row),
            pl.BlockSpec((TM, D_MODEL), row),
            pl.BlockSpec((TM, ATTN_W), row),
            pl.BlockSpec((TM, PC_W), row),
            _prev_rows(PC_W),
            pl.BlockSpec((CONV_K, CONV_W), fixed),
            pl.BlockSpec((D_MODEL, D_MODEL), fixed),
            pl.BlockSpec((1, D_MODEL), fixed),
        ],
        out_specs=[
            pl.BlockSpec((TM, D_MODEL), row),
            pl.BlockSpec((TM, D_MODEL), row),
            pl.BlockSpec((D_MODEL, D_MODEL), fixed),
            pl.BlockSpec((1, D_MODEL), fixed),
            pl.BlockSpec((8, 128), fixed),
        ],
        out_shape=[
            jax.ShapeDtypeStruct((S, D_MODEL), F32),
            jax.ShapeDtypeStruct((S, D_MODEL), ACT),
            jax.ShapeDtypeStruct((D_MODEL, D_MODEL), F32),
            jax.ShapeDtypeStruct((1, D_MODEL), F32),
            jax.ShapeDtypeStruct((8, 128), F32),
        ],
        compiler_params=_params(("arbitrary",)),
    )(x, target, ya, pc, pc, conv_w, w_out, final_g)


def _attn_bwd(pa, dmix, att, tab, sinks):
    S = pa.shape[0]
    nt = S // TQ
    nb = TQ // BLK

    def body(sink_ref, q_ref, g_ref, kvc_ref, kvp_ref, tabc_ref, tabp_ref, dm_ref, att_ref,
             d_ref, dsink_ref, kall, dkv, carry, q_sc, do_sc, p_sc, ds_sc, dsink_acc):
        step = pl.program_id(0)
        i = nt - 1 - step

        @pl.when(step == 0)
        def _():
            carry[...] = jnp.zeros_like(carry)
            dsink_acc[...] = jnp.zeros_like(dsink_acc)

        _fill_kv(kall, kvc_ref, kvp_ref, tabc_ref, tabp_ref)
        dkv[0:TQ, :] = jnp.zeros((TQ, 2 * KV_W), F32)
        dkv[TQ:TQ + BLK, :] = carry[...]
        lane = lax.broadcasted_iota(jnp.int32, (BLK, 128), 1)
        half = [lane < HEAD_DIM, lane >= HEAD_DIM]
        upper = _upper()
        upper_b = upper.astype(BF16)
        sinks = _sink_rows(sink_ref)
        for j in range(nb):
            rq = slice(j * BLK, (j + 1) * BLK)
            rk = slice(j * BLK, (j + 2) * BLK)
            tab = tabc_ref[rq, :]
            pair = [slice(p * 128, (p + 1) * 128) for p in range(4)]
            qr = [_rope(q_ref[rq, c].astype(F32), tab) * 0.125 for c in pair]
            g = [g_ref[rq, c].astype(F32) for c in pair]
            da = [dm_ref[rq, c].astype(F32) for c in pair]
            do = [da[p] * _silu(g[p]) for p in range(4)]
            dqs, dks, dvs = [], [], []
            for s in range(2):
                kk = kall[s, rk, :]
                vv = kall[2 + s, rk, :]
                _stack_heads(q_sc, s, half, qr)
                _stack_heads(do_sc, s, half, do)
                prev, cur = _scores(kk, q_sc[s], i == 0 if j == 0 else None)
                prob, psink = _softmax(jnp.where(upper, prev, cur), sinks[s])
                _split_store(p_sc, s, upper_b, prob)
                dprob = _merge(upper, _nt(vv, do_sc[s]))
                dsum = jnp.sum(dprob * prob, axis=0, keepdims=True)
                _split_store(ds_sc, s, upper_b, prob * (dprob - dsum))
                dsink_acc[s, 0:1, :] += psink * dsum
                dqs.append(_tn(ds_sc[s], kk))
                dks.append(_nn(ds_sc[s], q_sc[s]))
                dvs.append(_nn(p_sc[s], do_sc[s]))
            for p in range(4):
                d_ref[rq, pair[p]] = _rope_t(_unstack_pair(half, dqs, p) * 0.125, tab).astype(BF16)
                d_ref[rq, 512 + p * 128:512 + (p + 1) * 128] = (
                    da[p] * att_ref[rq, pair[p]].astype(F32) * _dsilu(g[p])).astype(BF16)
            dkv[rk, 0:128] += dks[0] + pltpu.roll(dks[1], 64, 1)
            dkv[rk, 128:256] += dvs[0] + pltpu.roll(dvs[1], 64, 1)
        d_ref[:, 1024:1152] = _rope_t(dkv[BLK:BLK + TQ, 0:128], tabc_ref[...]).astype(BF16)
        d_ref[:, 1152:1280] = dkv[BLK:BLK + TQ, 128:256].astype(BF16)
        carry[...] = dkv[0:BLK, :]

        @pl.when(step == nt - 1)
        def _():
            lanes = lax.broadcasted_iota(jnp.int32, (8, 128), 1)
            out = jnp.zeros((8, 128), F32)
            for s in range(2):
                for a, (p, e) in enumerate(HEADS[s]):
                    tot = jnp.sum(dsink_acc[s, 0:1, a * BLK:(a + 1) * BLK], axis=1, keepdims=True)
                    out = jnp.where(lanes == 2 * p + e, -tot, out)
            dsink_ref[...] = out

    rev = lambda s: nt - 1 - s
    return pl.pallas_call(
        body,
        name="attn_bwd",
        grid=(nt,),
        in_specs=_attn_specs(rev) + [pl.BlockSpec((TQ, ATTN_W), lambda s: (nt - 1 - s, 0))] * 2,
        out_specs=[
            pl.BlockSpec((TQ, PA_W), lambda s: (nt - 1 - s, 0)),
            pl.BlockSpec((8, 128), lambda s: (0, 0)),
        ],
        out_shape=[
            jax.ShapeDtypeStruct((S, PA_W), BF16),
            jax.ShapeDtypeStruct((8, 128), F32),
        ],
        scratch_shapes=[
            pltpu.VMEM((4, BLK + TQ, 128), BF16),
            pltpu.VMEM((BLK + TQ, 2 * KV_W), F32),
            pltpu.VMEM((BLK, 2 * KV_W), F32),
            pltpu.VMEM((2, 4 * BLK, 128), BF16),
            pltpu.VMEM((2, 4 * BLK, 128), BF16),
            pltpu.VMEM((2, 2 * BLK, 4 * BLK), BF16),
            pltpu.VMEM((2, 2 * BLK, 4 * BLK), BF16),
            pltpu.VMEM((2, 8, 4 * BLK), F32),
        ],
        compiler_params=_params(("arbitrary",)),
    )(sinks, pa, pa, pa, pa, tab, tab, dmix, att)


def _conv_bwd_tile(pc_ref, prev_ref, next_ref, dm_ref, dmn_ref, w_ref, d_ref, gw_ref, has_prev, has_next):
    rows = pc_ref.shape[0]
    w0, w1, w2 = w_ref[0:1, :], w_ref[1:2, :], w_ref[2:3, :]
    b, c, hh, gc, u, um1, um2, cv = _conv_tile(pc_ref, prev_ref, w_ref, has_prev)
    sg = _silu(gc)
    dy = dm_ref[...].astype(F32)
    dcv = dy * b * sg

    def next_dcv(r):
        nd = (dmn_ref[r:r + 1, :].astype(F32) * next_ref[r:r + 1, 0:512].astype(F32)
              * _silu(next_ref[r:r + 1, 1536:2048].astype(F32)))
        return jnp.where(has_next, nd, 0.0)

    row = lax.broadcasted_iota(jnp.int32, (rows, CONV_W), 0)
    dp1 = jnp.where(row == rows - 1, next_dcv(0), pltpu.roll(dcv, rows - 1, 0))
    dp2 = jnp.where(row == rows - 1, next_dcv(1),
                    jnp.where(row == rows - 2, next_dcv(0), pltpu.roll(dcv, rows - 2, 0)))
    du = w2 * dcv + w1 * dp1 + w0 * dp2
    d_ref[:, 0:512] = (dy * cv * sg).astype(BF16)
    d_ref[:, 512:1024] = (du * hh).astype(BF16)
    d_ref[:, 1024:1536] = (du * c).astype(BF16)
    d_ref[:, 1536:2048] = (dy * b * cv * _dsilu(gc)).astype(BF16)
    gw_ref[0:1, :] += jnp.sum(dcv * um2, axis=0, keepdims=True)
    gw_ref[1:2, :] += jnp.sum(dcv * um1, axis=0, keepdims=True)
    gw_ref[2:3, :] += jnp.sum(dcv * u, axis=0, keepdims=True)


def _grad_x(da, dc, wt, x, dh, norm_g, grads):
    S = x.shape[0]
    n_steps = S // TM
    rs = _ReduceScatter(grads)
    n_rs_out = len(rs.out_shape())

    def body(da_ref, dc_ref, wt_ref, x_ref, dh_ref, g_ref, *rest):
        grad_refs, rest = rest[:rs.n], rest[rs.n:]
        gx_ref, gng_ref = rest[:2]
        rs_out, rs_scratch = rest[2:2 + n_rs_out], rest[2 + n_rs_out:]
        step = pl.program_id(0)
        finish = rs.emit(step, n_steps, grad_refs, rs_out, rs_scratch)

        @pl.when(step == 0)
        def _():
            gng_ref[...] = jnp.zeros_like(gng_ref)

        dxn = (_nn(da_ref[:, 0:512], wt_ref[0:512, :]) + _nn(da_ref[:, 512:1024], wt_ref[768:1280, :])
               + _nn(da_ref[:, 1024:1280], wt_ref[512:768, :]) + _nn(dc_ref[...], wt_ref[1280:3328, :]))
        xv = x_ref[...]
        r = lax.rsqrt(jnp.mean(xv * xv, axis=-1, keepdims=True) + EPS)
        n = xv * r
        gng_ref[...] += jnp.sum(dxn * n, axis=0, keepdims=True)
        dxg = dxn * g_ref[...]
        gx_ref[...] = dh_ref[...] + r * (dxg - n * jnp.mean(dxg * n, axis=-1, keepdims=True))
        finish()

    row = lambda i: (i, 0)
    fixed = lambda i: (0, 0)
    any_spec = pl.BlockSpec(memory_space=pl.ANY)
    outs = pl.pallas_call(
        body,
        name="grad_x_reduce_scatter",
        grid=(n_steps,),
        in_specs=[
            pl.BlockSpec((TM, PA_W), row),
            pl.BlockSpec((TM, PC_W), row),
            pl.BlockSpec((IN_W, D_MODEL), fixed),
            pl.BlockSpec((TM, D_MODEL), row),
            pl.BlockSpec((TM, D_MODEL), row),
            pl.BlockSpec((1, D_MODEL), fixed),
        ] + [any_spec] * rs.n,
        out_specs=[pl.BlockSpec((TM, D_MODEL), row), pl.BlockSpec((1, D_MODEL), fixed)] + [any_spec] * n_rs_out,
        out_shape=[jax.ShapeDtypeStruct((S, D_MODEL), F32), jax.ShapeDtypeStruct((1, D_MODEL), F32)] + rs.out_shape(),
        scratch_shapes=rs.scratch_shapes(),
        compiler_params=_params(("arbitrary",)),
    )(da, dc, wt, x, dh, norm_g, *grads)
    return outs[0], outs[1], outs[2:2 + rs.n], outs[2 + rs.n:2 + 2 * rs.n]


def _grad_w_in(da, pc, dmix, conv_w, xn):
    S = xn.shape[0]
    nt = S // TM
    t16 = TM // HALO

    def body(da_ref, pc_ref, prev_ref, next_ref, dm_ref, dmn_ref, cw_ref, xn_ref, gw_ref, dc_ref, gcw_ref):
        i = pl.program_id(0)

        @pl.when(i == 0)
        def _():
            gw_ref[...] = jnp.zeros_like(gw_ref)
            gcw_ref[...] = jnp.zeros_like(gcw_ref)

        xn = xn_ref[...]
        gw_ref[0:512, :] += _tn(da_ref[:, 0:512], xn)
        gw_ref[768:1280, :] += _tn(da_ref[:, 512:1024], xn)
        gw_ref[512:768, :] += _tn(da_ref[:, 1024:1280], xn)
        _conv_bwd_tile(pc_ref, prev_ref, next_ref, dm_ref, dmn_ref, cw_ref, dc_ref, gcw_ref, i > 0, i < nt - 1)
        gw_ref[1280:3328, :] += _tn(dc_ref[...], xn)

    row = lambda i: (i, 0)
    fixed = lambda i: (0, 0)
    nxt = lambda i: jnp.minimum((i + 1) * t16, nt * t16 - 1)
    return pl.pallas_call(
        body,
        name="grad_w_in",
        grid=(nt,),
        in_specs=[
            pl.BlockSpec((TM, PA_W), row),
            pl.BlockSpec((TM, PC_W), row),
            _prev_rows(PC_W),
            pl.BlockSpec((HALO, PC_W), lambda i: (nxt(i), 0)),
            pl.BlockSpec((TM, CONV_W), lambda i: (i, 1)),
            pl.BlockSpec((HALO, CONV_W), lambda i: (nxt(i), 1)),
            pl.BlockSpec((CONV_K, CONV_W), fixed),
            pl.BlockSpec((TM, D_MODEL), row),
        ],
        out_specs=[
            pl.BlockSpec((IN_W, D_MODEL), fixed),
            pl.BlockSpec((TM, PC_W), row),
            pl.BlockSpec((CONV_K, CONV_W), fixed),
        ],
        out_shape=[
            jax.ShapeDtypeStruct((IN_W, D_MODEL), F32),
            jax.ShapeDtypeStruct((S, PC_W), BF16),
            jax.ShapeDtypeStruct((CONV_K, CONV_W), F32),
        ],
        compiler_params=_params(("arbitrary",)),
    )(da, pc, pc, pc, dmix, dmix, conv_w, xn)


def _adam_update(w, g, m, v):
    c1 = 1.0 - ADAM_B1 ** ADAM_STEP
    c2 = 1.0 - ADAM_B2 ** ADAM_STEP
    nm = ADAM_B1 * m + (1.0 - ADAM_B1) * g
    nv = ADAM_B2 * v + (1.0 - ADAM_B2) * (g * g)
    return -ADAM_LR * ((nm / c1) / (jnp.sqrt(nv / c2) + ADAM_EPS) + ADAM_WD * w), nm, nv


def _sum_chips_adamw(own, others, w, m, v, name):
    def body(own_ref, p_ref, w_ref, m_ref, v_ref, g_ref, d_ref, nm_ref, nv_ref):
        g = own_ref[...]
        for k in range(N_CHIP - 1):
            g = g + p_ref[k].astype(F32)
        g_ref[...] = g
        d_ref[...], nm_ref[...], nv_ref[...] = _adam_update(w_ref[...], g, m_ref[...], v_ref[...])

    shape = jax.ShapeDtypeStruct(w.shape, F32)
    return pl.pallas_call(
        body,
        name=name,
        out_shape=[shape] * 4,
        compiler_params=_params(),
    )(own, others, w, m, v)


def _small_adamw(parts, params):
    def body(ng_all, fg_all, sk_all, loss_all, cw_all, *rest):
        prm, outs, (sk_sum, loss_sum, cw_sum) = rest[:12], rest[12:29], rest[29:]
        me = 4 * lax.axis_index("x") + 2 * lax.axis_index("y") + lax.axis_index("c")

        def total(ref):
            acc = ref[0]
            for d in range(1, N_DEV):
                acc = acc + ref[d]
            return acc

        sk_sum[...] = total(sk_all)
        loss_sum[...] = total(loss_all)
        cw_sum[...] = total(cw_all)
        grads = (total(ng_all), total(fg_all), sk_sum[0:1, 0:8],
                 cw_sum[pl.ds(pl.multiple_of(me * 8, 8), CONV_K), 0:64])
        outs[0][...] = loss_sum[0:1, 0:1]
        for k, g in enumerate(grads):
            w_ref, m_ref, v_ref = prm[3 * k:3 * k + 3]
            g_ref, d_ref, nm_ref, nv_ref = outs[1 + 4 * k:5 + 4 * k]
            g_ref[...] = g
            d_ref[...], nm_ref[...], nv_ref[...] = _adam_update(w_ref[...], g, m_ref[...], v_ref[...])

    flat = [a for p in params for a in p]
    out_shape = [jax.ShapeDtypeStruct((1, 1), F32)]
    for p in params:
        out_shape += [jax.ShapeDtypeStruct(p[0].shape, F32)] * 4
    return pl.pallas_call(
        body,
        name="adamw_small",
        out_shape=out_shape,
        scratch_shapes=[pltpu.VMEM((8, 128), F32), pltpu.VMEM((8, 128), F32), pltpu.VMEM((N_DEV * 8, 128), F32)],
        compiler_params=_params(),
    )(*parts, *flat)


def kernel(x, norm_g, w_in, sinks, conv_w, w_out, final_g, loss_target, m_norm_g, m_w_in, m_sinks, m_conv_w, m_w_out, m_final_g, v_norm_g, v_w_in, v_sinks, v_conv_w, v_w_out, v_final_g):
    S = x.shape[1]
    x2 = x.reshape(S, D_MODEL)
    t2 = loss_target.reshape(S, D_MODEL)
    ng = norm_g.reshape(1, D_MODEL)
    fg = final_g.reshape(1, D_MODEL)

    cw_pad = jnp.zeros((8, 128), F32).at[0:CONV_K, 0:64].set(conv_w)
    xn, tab, wt = _prologue(x2, ng, w_in.T.astype(BF16))
    pa, pc, (wo, cw_all) = _fwd_proj(xn, wt, [w_out.astype(BF16), cw_pad])
    cw = cw_all.reshape(N_DEV, 8, 128)[:, 0:CONV_K, 0:64].transpose(1, 0, 2).reshape(CONV_K, CONV_W)
    ya, att = _attn_fwd(pa, tab, sinks)
    dh, dmix, g_wo, g_fg, loss_part = _out_loss(x2, t2, ya, pc, cw, wo, fg)
    da, g_sinks = _attn_bwd(pa, dmix, att, tab, sinks)
    g_wt, dc, g_cw = _grad_w_in(da, pc, dmix, cw, xn)
    grad_x, g_ng, own, others = _grad_x(
        da, dc, wt, x2, dh, ng,
        [g_wt.reshape(N_DEV, SHARD_IN, D_MODEL), g_wo.reshape(N_DEV, SHARD_OUT, D_MODEL)])
    gt, dt, nmt, nvt = _sum_chips_adamw(own[0], others[0], w_in.T, m_w_in.T, v_w_in.T, "adamw_w_in")
    grad_w_in, d_w_in, nm_w_in, nv_w_in = gt.T, dt.T, nmt.T, nvt.T
    grad_w_out, d_w_out, nm_w_out, nv_w_out = _sum_chips_adamw(
        own[1], others[1], w_out, m_w_out, v_w_out, "adamw_w_out")
    cw_pack = jnp.pad(g_cw.reshape(CONV_K, N_DEV, 64).transpose(1, 0, 2),
                      ((0, 0), (0, 8 - CONV_K), (0, 64))).reshape(N_DEV * 8, 128)
    gathered = _all_gather([g_ng.reshape(8, 128), g_fg.reshape(8, 128), g_sinks, loss_part, cw_pack],
                           "all_gather_small_grads")
    parts = [a.reshape(N_DEV, a.shape[0] // N_DEV, 128) for a in gathered]
    vec = lambda a: a.reshape(8, 128)
    row = lambda a: a.reshape(1, 8)
    res = _small_adamw(parts, [
        (vec(norm_g), vec(m_norm_g), vec(v_norm_g)), (vec(final_g), vec(m_final_g), vec(v_final_g)),
        (row(sinks), row(m_sinks), row(v_sinks)), (conv_w, m_conv_w, v_conv_w)])
    loss = res[0].reshape(())
    grad_norm_g, d_ng, nm_ng, nv_ng = [a.reshape(D_MODEL) for a in res[1:5]]
    grad_final_g, d_fg, nm_fg, nv_fg = [a.reshape(D_MODEL) for a in res[5:9]]
    grad_sinks, d_sk, nm_sk, nv_sk = [a.reshape(N_Q_HEADS) for a in res[9:13]]
    grad_conv_w, d_cw, nm_cw, nv_cw = res[13:17]

    return (loss, grad_x.reshape(1, S, D_MODEL), grad_norm_g, grad_w_in, grad_sinks, grad_conv_w, grad_w_out, grad_final_g,
            d_ng, d_w_in, d_sk, d_cw, d_w_out, d_fg,
            nm_ng, nm_w_in, nm_sk, nm_cw, nm_w_out, nm_fg,
            nv_ng, nv_w_in, nv_sk, nv_cw, nv_w_out, nv_fg)
```

```python
import numpy as np
import jax
import jax.numpy as jnp
from jax import lax
from jax.experimental import pallas as pl
from jax.experimental.pallas import tpu as pltpu

F32 = jnp.float32
BF16 = jnp.bfloat16
MESH = pl.DeviceIdType.MESH

D_MODEL = 1024
HEAD_DIM = 64
N_Q_HEADS = 8
GROUP = 4
ATTN_W = 512
KV_W = 128
BLK = 128
CONV_W = 512
CONV_K = 3
IN_W = 3328
PA_W = 1280
PC_W = 2048
EPS = 1e-5
ROPE_THETA = 500000.0
ROT_DIM = 16
N_DEV = 8
N_CHIP = 4
SHARD_IN = IN_W // N_DEV
SHARD_OUT = D_MODEL // N_DEV

ADAM_LR = 0.001
ADAM_B1 = 0.9
ADAM_B2 = 0.999
ADAM_EPS = 1e-08
ADAM_WD = 0.01
ADAM_STEP = 10

ACT = jnp.bfloat16

TM = 512
TQ = 1024
HALO = 16
VMEM_LIMIT = 56 * 1024 * 1024

NT_DIMS = (((1,), (1,)), ((), ()))
TN_DIMS = (((0,), (0,)), ((), ()))


def _params(sem=None):
    kw = dict(vmem_limit_bytes=VMEM_LIMIT)
    if sem is not None:
        kw["dimension_semantics"] = sem
    return pltpu.CompilerParams(**kw)


def _nt(a, b):
    return lax.dot_general(a, b, NT_DIMS, preferred_element_type=F32)


def _tn(a, b):
    return lax.dot_general(a, b, TN_DIMS, preferred_element_type=F32)


def _nn(a, b):
    return jnp.dot(a, b, preferred_element_type=F32)


def _silu(g):
    return g * jax.nn.sigmoid(g)


def _dsilu(g):
    s = jax.nn.sigmoid(g)
    return s * (1.0 + g * (1.0 - s))


class _AllGatherInSteps:
    def __init__(self, arrs, forward_step):
        self.blocks = [(a.shape, a.dtype) for a in arrs]
        self.n = len(arrs)
        self.forward_step = forward_step

    def out_shape(self):
        return [jax.ShapeDtypeStruct((N_DEV * s[0], s[1]), d) for s, d in self.blocks]

    def scratch_shapes(self):
        return [pltpu.SemaphoreType.DMA((7 * self.n,)), pltpu.SemaphoreType.DMA((7 * self.n,)),
                pltpu.SemaphoreType.DMA((self.n,))]

    def emit(self, step, n_steps, x_refs, out_refs, scratch):
        assert n_steps > self.forward_step + 1
        send_sems, recv_sems, local_sems = scratch
        x, y, c = lax.axis_index("x"), lax.axis_index("y"), lax.axis_index("c")
        me, sibling = (x, y, c), (x, y, 1 - c)
        chips = [(1 - x, y), (x, 1 - y), (1 - x, 1 - y)]

        def rows(a, px, py, pc):
            m = self.blocks[a][0][0]
            return out_refs[a].at[pl.ds((4 * px + 2 * py + pc) * m, m), :]

        def copy(a, k, block, to, src=None):
            return pltpu.make_async_remote_copy(
                src_ref=rows(a, *block) if src is None else src, dst_ref=rows(a, *block),
                send_sem=send_sems.at[a * 7 + k], recv_sem=recv_sems.at[a * 7 + k],
                device_id=to, device_id_type=MESH)

        def mine(a):
            return pltpu.make_async_copy(x_refs[a], rows(a, *me), local_sems.at[a])

        def first(a):
            return ([copy(a, 0, me, sibling, src=x_refs[a])]
                    + [copy(a, 1 + j, me, (*chip, c), src=x_refs[a]) for j, chip in enumerate(chips)])

        def passed(a):
            return [copy(a, 4 + j, (*chip, c), sibling) for j, chip in enumerate(chips)]

        @pl.when(step == 0)
        def _():
            for a in range(self.n):
                mine(a).start()
                for cp in first(a):
                    cp.start()

        @pl.when(step == self.forward_step)
        def _():
            for j, chip in enumerate(chips):
                for a in range(self.n):
                    copy(a, 1 + j, (*chip, c), me).wait_recv()
                    copy(a, 4 + j, (*chip, c), sibling).start()

        def finish():
            @pl.when(step == n_steps - 1)
            def _():
                for a in range(self.n):
                    copy(a, 0, sibling, me).wait_recv()
                    for j, chip in enumerate(chips):
                        copy(a, 4 + j, (*chip, 1 - c), me).wait_recv()
                    for cp in first(a) + passed(a):
                        cp.wait_send()
                    mine(a).wait()

        return finish


class _ReduceScatter:
    def __init__(self, grads):
        self.shapes = [g.shape[1:] for g in grads]
        self.n = len(grads)
        self.items = tuple((a, r) for r in (1, 2, 3, 0) for a in range(self.n))
        self.steps = len(self.items) + 2

    def out_shape(self):
        own = [jax.ShapeDtypeStruct(s, F32) for s in self.shapes]
        ici = [jax.ShapeDtypeStruct((N_CHIP - 1,) + s, BF16) for s in self.shapes]
        land = [jax.ShapeDtypeStruct((N_CHIP,) + s, F32) for s in self.shapes]
        return own + ici + land

    def scratch_shapes(self):
        n_items = len(self.items)
        return ([pltpu.VMEM((2,) + s, F32) for s in self.shapes]
                + [pltpu.VMEM((N_CHIP - 1,) + s, BF16) for s in self.shapes]
                + [pltpu.VMEM(s, F32) for s in self.shapes]
                + [pltpu.SemaphoreType.DMA((self.n * N_CHIP,))] * 2
                + [pltpu.SemaphoreType.DMA((2 * n_items,))]
                + [pltpu.SemaphoreType.DMA((self.n * (N_CHIP - 1),))] * 2
                + [pltpu.SemaphoreType.DMA((self.n,))])

    def emit(self, step, n_steps, g_refs, out_refs, scratch):
        assert n_steps > self.steps
        n = self.n
        own_refs, ici_refs, land_refs = out_refs[:n], out_refs[n:2 * n], out_refs[2 * n:]
        stage, pair_bf, pair_own = scratch[:n], scratch[n:2 * n], scratch[2 * n:3 * n]
        sib_send, sib_recv, load_sems, ici_send, ici_recv, own_sems = scratch[3 * n:]
        x, y, c = lax.axis_index("x"), lax.axis_index("y"), lax.axis_index("c")

        def chip_of(r):
            return (x ^ (r >> 1), y ^ (r & 1))

        def block_of(r, core):
            cx, cy = chip_of(r)
            return 4 * cx + 2 * cy + core

        def to_sibling(a, r):
            return pltpu.make_async_remote_copy(
                src_ref=g_refs[a].at[block_of(r, 1 - c)], dst_ref=land_refs[a].at[r],
                send_sem=sib_send.at[a * N_CHIP + r], recv_sem=sib_recv.at[a * N_CHIP + r],
                device_id=(x, y, 1 - c), device_id_type=MESH)

        def loads(k):
            a, r = self.items[k]
            return (pltpu.make_async_copy(g_refs[a].at[block_of(r, c)], stage[a].at[0], load_sems.at[2 * k]),
                    pltpu.make_async_copy(land_refs[a].at[r], stage[a].at[1], load_sems.at[2 * k + 1]))

        def to_owner(k):
            a, r = self.items[k]
            if r == 0:
                return pltpu.make_async_copy(pair_own[a], own_refs[a], own_sems.at[a])
            return pltpu.make_async_remote_copy(
                src_ref=pair_bf[a].at[r - 1], dst_ref=ici_refs[a].at[r - 1],
                send_sem=ici_send.at[a * (N_CHIP - 1) + r - 1], recv_sem=ici_recv.at[a * (N_CHIP - 1) + r - 1],
                device_id=(*chip_of(r), c), device_id_type=MESH)

        @pl.when(step == 0)
        def _():
            for a, r in self.items:
                to_sibling(a, r).start()

        for k, (a, r) in enumerate(self.items):
            @pl.when(step == 1 + k)
            def _(k=k, a=a, r=r):
                to_sibling(a, r).wait_recv()
                for cp in loads(k):
                    cp.start()

            @pl.when(step == 2 + k)
            def _(k=k, a=a, r=r):
                for cp in loads(k):
                    cp.wait()
                total = stage[a][0] + stage[a][1]
                if r == 0:
                    pair_own[a][...] = total
                else:
                    pair_bf[a][r - 1] = total.astype(BF16)
                to_owner(k).start()

        def finish():
            @pl.when(step == n_steps - 1)
            def _():
                for k, (a, r) in enumerate(self.items):
                    if r == 0:
                        to_owner(k).wait()
                    else:
                        to_owner(k).wait_send()
                        to_owner(k).wait_recv()
                for a, r in self.items:
                    to_sibling(a, r).wait_send()

        return finish


def _prologue(x, norm_g, w_shard):
    S = x.shape[0]
    n_steps = S // TM
    half = ROT_DIM // 2
    pos = jnp.arange(S, dtype=jnp.int32).astype(F32)
    inv_freq = ROPE_THETA ** (-jnp.arange(0, ROT_DIM, 2, dtype=F32) / ROT_DIM)
    ang = inv_freq[:, None] * pos[None, :]
    cs = jnp.concatenate([jnp.cos(ang), jnp.sin(ang)], axis=0)
    ag = _AllGatherInSteps([w_shard], forward_step=n_steps - 3)

    def body(x_ref, g_ref, cs_ref, w_ref, xn_ref, tab_ref, wt_ref, *ag_scratch):
        step = pl.program_id(0)
        finish = ag.emit(step, n_steps, [w_ref], [wt_ref], ag_scratch)
        xv = x_ref[...]
        r = lax.rsqrt(jnp.mean(xv * xv, axis=-1, keepdims=True) + EPS)
        xn_ref[...] = (xv * r * g_ref[...]).astype(BF16)

        xt = jnp.concatenate([cs_ref[...], jnp.zeros((128 - 2 * half, TM), F32)], axis=0).T
        lane = lax.broadcasted_iota(jnp.int32, (TM, 128), 1)
        rr = lane & (HEAD_DIM - 1)
        first = lane < HEAD_DIM

        def at(shift_first, shift_second):
            return jnp.where(first, pltpu.roll(xt, shift_first, 1) if shift_first else xt,
                             pltpu.roll(xt, shift_second, 1))

        cos_lo, cos_hi = at(0, HEAD_DIM), at(half, HEAD_DIM + half)
        sin_lo, sin_hi = at(128 - half, HEAD_DIM - half), at(0, HEAD_DIM)
        tab_ref[:, 0:128] = jnp.where(rr < half, cos_lo, jnp.where(rr < ROT_DIM, cos_hi, 1.0))
        tab_ref[:, 128:256] = jnp.where(rr < half, -sin_lo, 0.0)
        tab_ref[:, 256:384] = jnp.where((rr >= half) & (rr < ROT_DIM), sin_hi, 0.0)
        finish()

    any_spec = pl.BlockSpec(memory_space=pl.ANY)
    return pl.pallas_call(
        body,
        name="prologue_all_gather_w_in",
        grid=(n_steps,),
        in_specs=[
            pl.BlockSpec((TM, D_MODEL), lambda i: (i, 0)),
            pl.BlockSpec((1, D_MODEL), lambda i: (0, 0)),
            pl.BlockSpec((2 * half, TM), lambda i: (0, i)),
            any_spec,
        ],
        out_specs=[
            pl.BlockSpec((TM, D_MODEL), lambda i: (i, 0)),
            pl.BlockSpec((TM, 384), lambda i: (i, 0)),
            any_spec,
        ],
        out_shape=[
            jax.ShapeDtypeStruct((S, D_MODEL), BF16),
            jax.ShapeDtypeStruct((S, 384), F32),
        ] + ag.out_shape(),
        scratch_shapes=ag.scratch_shapes(),
        compiler_params=_params(("arbitrary",)),
    )(x, norm_g, cs, w_shard)


def _fwd_proj(xn, wt, later):
    S = xn.shape[0]

    n_steps = S // TM
    ag = _AllGatherInSteps(later, forward_step=3)

    def body(xn_ref, wt_ref, *rest):
        later_refs, rest = rest[:ag.n], rest[ag.n:]
        pa_ref, pc_ref = rest[:2]
        gathered, ag_scratch = rest[2:2 + ag.n], rest[2 + ag.n:]
        step = pl.program_id(0)
        finish = ag.emit(step, n_steps, later_refs, gathered, ag_scratch)
        xn = xn_ref[...]
        pa_ref[:, 0:512] = _nt(xn, wt_ref[0:512, :]).astype(ACT)
        pa_ref[:, 512:1024] = _nt(xn, wt_ref[768:1280, :]).astype(ACT)
        pa_ref[:, 1024:1280] = _nt(xn, wt_ref[512:768, :]).astype(ACT)
        pc_ref[...] = _nt(xn, wt_ref[1280:3328, :]).astype(ACT)
        finish()

    any_spec = pl.BlockSpec(memory_space=pl.ANY)
    outs = pl.pallas_call(
        body,
        name="fwd_proj_all_gather",
        grid=(n_steps,),
        in_specs=[
            pl.BlockSpec((TM, D_MODEL), lambda i: (i, 0)),
            pl.BlockSpec((IN_W, D_MODEL), lambda i: (0, 0)),
        ] + [any_spec] * ag.n,
        out_specs=[
            pl.BlockSpec((TM, PA_W), lambda i: (i, 0)),
            pl.BlockSpec((TM, PC_W), lambda i: (i, 0)),
        ] + [any_spec] * ag.n,
        out_shape=[
            jax.ShapeDtypeStruct((S, PA_W), ACT),
            jax.ShapeDtypeStruct((S, PC_W), ACT),
        ] + ag.out_shape(),
        scratch_shapes=ag.scratch_shapes(),
        compiler_params=_params(("arbitrary",)),
    )(xn, wt, *later)
    return outs[0], outs[1], outs[2:]


def _rope(t, tab):
    return (t * tab[:, 0:128] + pltpu.roll(t, 120, 1) * tab[:, 128:256]
            + pltpu.roll(t, 8, 1) * tab[:, 256:384])


def _rope_t(d, tab):
    return (d * tab[:, 0:128] + pltpu.roll(d * tab[:, 128:256], 8, 1)
            + pltpu.roll(d * tab[:, 256:384], 120, 1))


def _fill_kv(kall, kvc_ref, kvp_ref, tabc_ref, tabp_ref):
    for lo, kv_ref, tab_ref, n in ((0, kvp_ref, tabp_ref, BLK), (BLK, kvc_ref, tabc_ref, TQ)):
        k = _rope(kv_ref[:, 0:128].astype(F32), tab_ref[...])
        v = kv_ref[:, 128:256].astype(F32)
        kall[0, lo:lo + n, :] = k.astype(BF16)
        kall[1, lo:lo + n, :] = pltpu.roll(k, 64, 1).astype(BF16)
        kall[2, lo:lo + n, :] = v.astype(BF16)
        kall[3, lo:lo + n, :] = pltpu.roll(v, 64, 1).astype(BF16)


HEADS = (((0, 0), (1, 0), (2, 1), (3, 1)), ((0, 1), (1, 1), (2, 0), (3, 0)))


def _upper():
    kj = lax.broadcasted_iota(jnp.int32, (BLK, 4 * BLK), 0)
    qi = lax.broadcasted_iota(jnp.int32, (BLK, 4 * BLK), 1) & (BLK - 1)
    return kj > qi


def _merge(upper, both):
    return jnp.where(upper, both[0:BLK, :], both[BLK:2 * BLK, :])


def _split_store(ref, s, upper_b, val):
    vb = val.astype(BF16)
    first = vb * upper_b
    ref[s, 0:BLK, :] = first
    ref[s, BLK:2 * BLK, :] = vb - first


def _sink_rows(sink_ref):
    return [jnp.concatenate([jnp.full((1, BLK), sink_ref[2 * p + e], F32) for p, e in HEADS[s]], axis=1)
            for s in range(2)]


def _stack_heads(ref, s, half, pairs):
    for a, (p, e) in enumerate(HEADS[s]):
        ref[s, a * BLK:(a + 1) * BLK, :] = jnp.where(half[e], pairs[p], 0.0).astype(BF16)


def _unstack_pair(half, outs, p):
    lo = 0 if p < 2 else 1
    rows = slice(p * BLK, (p + 1) * BLK)
    return jnp.where(half[0], outs[lo][rows, :], outs[1 - lo][rows, :])


def _softmax(sm, sinks):
    m = jnp.maximum(jnp.max(sm, axis=0, keepdims=True), sinks)
    p = jnp.exp(sm - m)
    es = jnp.exp(sinks - m)
    inv = 1.0 / (jnp.sum(p, axis=0, keepdims=True) + es)
    return p * inv, es * inv


def _scores(kk, q_stack, first):
    st = _nt(kk, q_stack)
    prev = st[0:BLK, :]
    if first is not None:
        prev = prev + jnp.where(first, -jnp.inf, 0.0)
    return prev, st[BLK:2 * BLK, :]


def _attn_specs(tile):
    nb = TQ // BLK
    prev = lambda i: jnp.maximum(tile(i) * nb - 1, 0)
    return [
        pl.BlockSpec(memory_space=pltpu.SMEM),
        pl.BlockSpec((TQ, ATTN_W), lambda i: (tile(i), 0)),
        pl.BlockSpec((TQ, ATTN_W), lambda i: (tile(i), 1)),
        pl.BlockSpec((TQ, 2 * KV_W), lambda i: (tile(i), 4)),
        pl.BlockSpec((BLK, 2 * KV_W), lambda i: (prev(i), 4)),
        pl.BlockSpec((TQ, 384), lambda i: (tile(i), 0)),
        pl.BlockSpec((BLK, 384), lambda i: (prev(i), 0)),
    ]


def _attn_fwd(pa, tab, sinks):
    S = pa.shape[0]
    nb = TQ // BLK

    def body(sink_ref, q_ref, g_ref, kvc_ref, kvp_ref, tabc_ref, tabp_ref, o_ref, att_ref, kall, q_sc, p_sc):
        i = pl.program_id(0)
        _fill_kv(kall, kvc_ref, kvp_ref, tabc_ref, tabp_ref)
        lane = lax.broadcasted_iota(jnp.int32, (BLK, 128), 1)
        half = [lane < HEAD_DIM, lane >= HEAD_DIM]
        upper = _upper()
        upper_b = upper.astype(BF16)
        sinks = _sink_rows(sink_ref)
        for j in range(nb):
            rq = slice(j * BLK, (j + 1) * BLK)
            rk = slice(j * BLK, (j + 2) * BLK)
            tab = tabc_ref[rq, :]
            qr = [_rope(q_ref[rq, p * 128:(p + 1) * 128].astype(F32), tab) * 0.125 for p in range(4)]
            outs = []
            for s in range(2):
                _stack_heads(q_sc, s, half, qr)
                prev, cur = _scores(kall[s, rk, :], q_sc[s], i == 0 if j == 0 else None)
                prob, _ = _softmax(jnp.where(upper, prev, cur), sinks[s])
                _split_store(p_sc, s, upper_b, prob)
                outs.append(_tn(p_sc[s], kall[2 + s, rk, :]))
            for p in range(4):
                cols = slice(p * 128, (p + 1) * 128)
                att = _unstack_pair(half, outs, p)
                att_ref[rq, cols] = att.astype(BF16)
                o_ref[rq, cols] = (att * _silu(g_ref[rq, cols].astype(F32))).astype(BF16)

    return pl.pallas_call(
        body,
        name="attn_fwd",
        grid=(S // TQ,),
        in_specs=_attn_specs(lambda i: i),
        out_specs=[pl.BlockSpec((TQ, ATTN_W), lambda i: (i, 0))] * 2,
        out_shape=[jax.ShapeDtypeStruct((S, ATTN_W), BF16)] * 2,
        scratch_shapes=[
            pltpu.VMEM((4, BLK + TQ, 128), BF16),
            pltpu.VMEM((2, 4 * BLK, 128), BF16),
            pltpu.VMEM((2, 2 * BLK, 4 * BLK), BF16),
        ],
        compiler_params=_params(("arbitrary",)),
    )(sinks, pa, pa, pa, pa, tab, tab)


def _shift_down(u, halo_ref, has_prev):
    def halo_u(r):
        hu = halo_ref[r:r + 1, 512:1024].astype(F32) * halo_ref[r:r + 1, 1024:1536].astype(F32)
        return jnp.where(has_prev, hu, 0.0)

    row = lax.broadcasted_iota(jnp.int32, u.shape, 0)
    um1 = jnp.where(row == 0, halo_u(HALO - 1), pltpu.roll(u, 1, 0))
    um2 = jnp.where(row == 0, halo_u(HALO - 2), jnp.where(row == 1, halo_u(HALO - 1), pltpu.roll(u, 2, 0)))
    return um1, um2


def _conv_tile(pc_ref, halo_ref, w_ref, has_prev):
    b = pc_ref[:, 0:512].astype(F32)
    c = pc_ref[:, 512:1024].astype(F32)
    hh = pc_ref[:, 1024:1536].astype(F32)
    gc = pc_ref[:, 1536:2048].astype(F32)
    u = c * hh
    um1, um2 = _shift_down(u, halo_ref, has_prev)
    cv = w_ref[0:1, :] * um2 + w_ref[1:2, :] * um1 + w_ref[2:3, :] * u
    return b, c, hh, gc, u, um1, um2, cv


def _prev_rows(width, col=0):
    return pl.BlockSpec((HALO, width), lambda i: (jnp.maximum(i * (TM // HALO) - 1, 0), col))


def _out_loss(x, target, ya, pc, conv_w, w_out, final_g):
    S = x.shape[0]

    def body(x_ref, t_ref, ya_ref, pc_ref, halo_ref, cw_ref, wo_ref, fg_ref,
             dh_ref, dmix_ref, gwo_ref, gfg_ref, loss_ref):
        @pl.when(pl.program_id(0) == 0)
        def _():
            gwo_ref[...] = jnp.zeros_like(gwo_ref)
            gfg_ref[...] = jnp.zeros_like(gfg_ref)
            loss_ref[...] = jnp.zeros_like(loss_ref)

        b, _, _, gc, _, _, _, cv = _conv_tile(pc_ref, halo_ref, cw_ref, pl.program_id(0) > 0)
        yc = (b * cv * _silu(gc)).astype(BF16)
        mix = jnp.concatenate([ya_ref[...], yc], axis=1)
        wo = wo_ref[...]
        fg = fg_ref[...]
        h = x_ref[...] + _nn(mix, wo)
        r = lax.rsqrt(jnp.mean(h * h, axis=-1, keepdims=True) + EPS)
        n = h * r
        err = n * fg - t_ref[...]
        loss_ref[...] += jnp.broadcast_to(
            0.5 * jnp.sum(jnp.mean(err * err, axis=-1, keepdims=True), axis=0, keepdims=True), (8, 128))
        gfg_ref[...] += jnp.sum(err * n, axis=0, keepdims=True) * (1.0 / D_MODEL)
        dyg = err * (fg * (1.0 / D_MODEL))
        dh = r * (dyg - n * jnp.mean(dyg * n, axis=-1, keepdims=True))
        dh_ref[...] = dh
        dhb = dh.astype(BF16)
        dmix_ref[...] = _nt(dhb, wo).astype(ACT)
        gwo_ref[...] += _tn(mix, dhb)

    row = lambda i: (i, 0)
    fixed = lambda i: (0, 0)
    return pl.pallas_call(
        body,
        name="out_loss",
        grid=(S // TM,),
        in_specs=[
            pl.BlockSpec((TM, D_MODEL), row),
            pl.BlockSpec((TM, D_MODEL), row),
            pl.BlockSpec((TM, ATTN_W), row),
            pl.BlockSpec((TM, PC_W), row),
            _prev_rows(PC_W),
            pl.BlockSpec((CONV_K, CONV_W), fixed),
            pl.BlockSpec((D_MODEL, D_MODEL), fixed),
            pl.BlockSpec((1, D_MODEL), fixed),
        ],
        out_specs=[
            pl.BlockSpec((TM, D_MODEL), row),
            pl.BlockSpec((TM, D_MODEL), row),
            pl.BlockSpec((D_MODEL, D_MODEL), fixed),
            pl.BlockSpec((1, D_MODEL), fixed),
            pl.BlockSpec((8, 128), fixed),
        ],
        out_shape=[
            jax.ShapeDtypeStruct((S, D_MODEL), F32),
            jax.ShapeDtypeStruct((S, D_MODEL), ACT),
            jax.ShapeDtypeStruct((D_MODEL, D_MODEL), F32),
            jax.ShapeDtypeStruct((1, D_MODEL), F32),
            jax.ShapeDtypeStruct((8, 128), F32),
        ],
        compiler_params=_params(("arbitrary",)),
    )(x, target, ya, pc, pc, conv_w, w_out, final_g)


def _attn_bwd(pa, dmix, att, tab, sinks):
    S = pa.shape[0]
    nt = S // TQ
    nb = TQ // BLK

    def body(sink_ref, q_ref, g_ref, kvc_ref, kvp_ref, tabc_ref, tabp_ref, dm_ref, att_ref,
             d_ref, dsink_ref, kall, dkv, carry, q_sc, do_sc, p_sc, ds_sc, dsink_acc):
        step = pl.program_id(0)
        i = nt - 1 - step

        @pl.when(step == 0)
        def _():
            carry[...] = jnp.zeros_like(carry)
            dsink_acc[...] = jnp.zeros_like(dsink_acc)

        _fill_kv(kall, kvc_ref, kvp_ref, tabc_ref, tabp_ref)
        dkv[0:TQ, :] = jnp.zeros((TQ, 2 * KV_W), F32)
        dkv[TQ:TQ + BLK, :] = carry[...]
        lane = lax.broadcasted_iota(jnp.int32, (BLK, 128), 1)
        half = [lane < HEAD_DIM, lane >= HEAD_DIM]
        upper = _upper()
        upper_b = upper.astype(BF16)
        sinks = _sink_rows(sink_ref)
        for j in range(nb):
            rq = slice(j * BLK, (j + 1) * BLK)
            rk = slice(j * BLK, (j + 2) * BLK)
            tab = tabc_ref[rq, :]
            pair = [slice(p * 128, (p + 1) * 128) for p in range(4)]
            qr = [_rope(q_ref[rq, c].astype(F32), tab) * 0.125 for c in pair]
            g = [g_ref[rq, c].astype(F32) for c in pair]
            da = [dm_ref[rq, c].astype(F32) for c in pair]
            do = [da[p] * _silu(g[p]) for p in range(4)]
            dqs, dks, dvs = [], [], []
            for s in range(2):
                kk = kall[s, rk, :]
                vv = kall[2 + s, rk, :]
                _stack_heads(q_sc, s, half, qr)
                _stack_heads(do_sc, s, half, do)
                prev, cur = _scores(kk, q_sc[s], i == 0 if j == 0 else None)
                prob, psink = _softmax(jnp.where(upper, prev, cur), sinks[s])
                _split_store(p_sc, s, upper_b, prob)
                dprob = _merge(upper, _nt(vv, do_sc[s]))
                dsum = jnp.sum(dprob * prob, axis=0, keepdims=True)
                _split_store(ds_sc, s, upper_b, prob * (dprob - dsum))
                dsink_acc[s, 0:1, :] += psink * dsum
                dqs.append(_tn(ds_sc[s], kk))
                dks.append(_nn(ds_sc[s], q_sc[s]))
                dvs.append(_nn(p_sc[s], do_sc[s]))
            for p in range(4):
                d_ref[rq, pair[p]] = _rope_t(_unstack_pair(half, dqs, p) * 0.125, tab).astype(BF16)
                d_ref[rq, 512 + p * 128:512 + (p + 1) * 128] = (
                    da[p] * att_ref[rq, pair[p]].astype(F32) * _dsilu(g[p])).astype(BF16)
            dkv[rk, 0:128] += dks[0] + pltpu.roll(dks[1], 64, 1)
            dkv[rk, 128:256] += dvs[0] + pltpu.roll(dvs[1], 64, 1)
        d_ref[:, 1024:1152] = _rope_t(dkv[BLK:BLK + TQ, 0:128], tabc_ref[...]).astype(BF16)
        d_ref[:, 1152:1280] = dkv[BLK:BLK + TQ, 128:256].astype(BF16)
        carry[...] = dkv[0:BLK, :]

        @pl.when(step == nt - 1)
        def _():
            lanes = lax.broadcasted_iota(jnp.int32, (8, 128), 1)
            out = jnp.zeros((8, 128), F32)
            for s in range(2):
                for a, (p, e) in enumerate(HEADS[s]):
                    tot = jnp.sum(dsink_acc[s, 0:1, a * BLK:(a + 1) * BLK], axis=1, keepdims=True)
                    out = jnp.where(lanes == 2 * p + e, -tot, out)
            dsink_ref[...] = out

    rev = lambda s: nt - 1 - s
    return pl.pallas_call(
        body,
        name="attn_bwd",
        grid=(nt,),
        in_specs=_attn_specs(rev) + [pl.BlockSpec((TQ, ATTN_W), lambda s: (nt - 1 - s, 0))] * 2,
        out_specs=[
            pl.BlockSpec((TQ, PA_W), lambda s: (nt - 1 - s, 0)),
            pl.BlockSpec((8, 128), lambda s: (0, 0)),
        ],
        out_shape=[
            jax.ShapeDtypeStruct((S, PA_W), BF16),
            jax.ShapeDtypeStruct((8, 128), F32),
        ],
        scratch_shapes=[
            pltpu.VMEM((4, BLK + TQ, 128), BF16),
            pltpu.VMEM((BLK + TQ, 2 * KV_W), F32),
            pltpu.VMEM((BLK, 2 * KV_W), F32),
            pltpu.VMEM((2, 4 * BLK, 128), BF16),
            pltpu.VMEM((2, 4 * BLK, 128), BF16),
            pltpu.VMEM((2, 2 * BLK, 4 * BLK), BF16),
            pltpu.VMEM((2, 2 * BLK, 4 * BLK), BF16),
            pltpu.VMEM((2, 8, 4 * BLK), F32),
        ],
        compiler_params=_params(("arbitrary",)),
    )(sinks, pa, pa, pa, pa, tab, tab, dmix, att)


def _conv_bwd_tile(pc_ref, prev_ref, next_ref, dm_ref, dmn_ref, w_ref, d_ref, gw_ref, has_prev, has_next,
                   on_piece):
    rows = pc_ref.shape[0]
    w0, w1, w2 = w_ref[0:1, :], w_ref[1:2, :], w_ref[2:3, :]
    b, c, hh, gc, u, um1, um2, cv = _conv_tile(pc_ref, prev_ref, w_ref, has_prev)
    sg = _silu(gc)
    dy = dm_ref[...].astype(F32)
    dcv = dy * b * sg

    def next_dcv(r):
        nd = (dmn_ref[r:r + 1, :].astype(F32) * next_ref[r:r + 1, 0:512].astype(F32)
              * _silu(next_ref[r:r + 1, 1536:2048].astype(F32)))
        return jnp.where(has_next, nd, 0.0)

    row = lax.broadcasted_iota(jnp.int32, (rows, CONV_W), 0)
    dp1 = jnp.where(row == rows - 1, next_dcv(0), pltpu.roll(dcv, rows - 1, 0))
    dp2 = jnp.where(row == rows - 1, next_dcv(1),
                    jnp.where(row == rows - 2, next_dcv(0), pltpu.roll(dcv, rows - 2, 0)))
    du = w2 * dcv + w1 * dp1 + w0 * dp2
    pieces = (lambda: dy * cv * sg, lambda: du * hh, lambda: du * c, lambda: dy * b * cv * _dsilu(gc))
    for k, piece in enumerate(pieces):
        d_ref[:, k * CONV_W:(k + 1) * CONV_W] = piece().astype(BF16)
        on_piece(k)
    gw_ref[0:1, :] += jnp.sum(dcv * um2, axis=0, keepdims=True)
    gw_ref[1:2, :] += jnp.sum(dcv * um1, axis=0, keepdims=True)
    gw_ref[2:3, :] += jnp.sum(dcv * u, axis=0, keepdims=True)


def _grad_x(da, dc, wt, x, dh, norm_g, small, grads):
    S = x.shape[0]
    n_steps = S // TM
    rs = _ReduceScatter(grads)
    n_rs_out = len(rs.out_shape())
    small_rows = 8 + small.shape[0]

    def body(da_ref, dc_ref, wt_ref, x_ref, dh_ref, g_ref, small_ref, *rest):
        grad_refs, rest = rest[:rs.n], rest[rs.n:]
        gx_ref, all_ref = rest[:2]
        rs_out, rest = rest[2:2 + n_rs_out], rest[2 + n_rs_out:]
        gng, stage, small_send, small_recv, small_own = rest[:5]
        rs_scratch = rest[5:]
        step = pl.program_id(0)
        finish = rs.emit(step, n_steps, grad_refs, rs_out, rs_scratch)

        @pl.when(step == 0)
        def _():
            gng[...] = jnp.zeros_like(gng)

        dxn = (_nn(da_ref[:, 0:512], wt_ref[0:512, :]) + _nn(da_ref[:, 512:1024], wt_ref[768:1280, :])
               + _nn(da_ref[:, 1024:1280], wt_ref[512:768, :]) + _nn(dc_ref[...], wt_ref[1280:3328, :]))
        xv = x_ref[...]
        r = lax.rsqrt(jnp.mean(xv * xv, axis=-1, keepdims=True) + EPS)
        n = xv * r
        gng[...] += jnp.sum(dxn * n, axis=0, keepdims=True)
        dxg = dxn * g_ref[...]
        gx_ref[...] = dh_ref[...] + r * (dxg - n * jnp.mean(dxg * n, axis=-1, keepdims=True))

        @pl.when(step == n_steps - 1)
        def _():
            x_, y_, c_ = lax.axis_index("x"), lax.axis_index("y"), lax.axis_index("c")
            me = 4 * x_ + 2 * y_ + c_
            for q in range(8):
                stage[q:q + 1, :] = gng[:, q * 128:(q + 1) * 128]
            stage[8:small_rows, :] = small_ref[...]
            own = pltpu.make_async_copy(stage, all_ref.at[me], small_own)
            own.start()
            sends = []
            for k in range(1, N_DEV):
                cp = pltpu.make_async_remote_copy(
                    src_ref=stage, dst_ref=all_ref.at[me],
                    send_sem=small_send.at[k - 1], recv_sem=small_recv.at[k - 1],
                    device_id=(x_ ^ (k >> 2), y_ ^ ((k >> 1) & 1), c_ ^ (k & 1)), device_id_type=MESH)
                cp.start()
                sends.append(cp)
            for cp in sends:
                cp.wait_send()
                cp.wait_recv()
            own.wait()

        finish()

    row = lambda i: (i, 0)
    fixed = lambda i: (0, 0)
    any_spec = pl.BlockSpec(memory_space=pl.ANY)
    outs = pl.pallas_call(
        body,
        name="grad_x_reduce_scatter",
        grid=(n_steps,),
        in_specs=[
            pl.BlockSpec((TM, PA_W), row),
            pl.BlockSpec((TM, PC_W), row),
            pl.BlockSpec((IN_W, D_MODEL), fixed),
            pl.BlockSpec((TM, D_MODEL), row),
            pl.BlockSpec((TM, D_MODEL), row),
            pl.BlockSpec((1, D_MODEL), fixed),
            pl.BlockSpec(small.shape, fixed),
        ] + [any_spec] * rs.n,
        out_specs=[pl.BlockSpec((TM, D_MODEL), row), any_spec] + [any_spec] * n_rs_out,
        out_shape=[jax.ShapeDtypeStruct((S, D_MODEL), F32),
                   jax.ShapeDtypeStruct((N_DEV, small_rows, 128), F32)] + rs.out_shape(),
        scratch_shapes=[
            pltpu.VMEM((1, D_MODEL), F32),
            pltpu.VMEM((small_rows, 128), F32),
            pltpu.SemaphoreType.DMA((N_DEV - 1,)),
            pltpu.SemaphoreType.DMA((N_DEV - 1,)),
            pltpu.SemaphoreType.DMA,
        ] + rs.scratch_shapes(),
        compiler_params=_params(("arbitrary",)),
    )(da, dc, wt, x, dh, norm_g, small, *grads)
    return outs[0], outs[1], outs[2:2 + rs.n], outs[2 + rs.n:2 + 2 * rs.n]


def _grad_w_in(da, pc, dmix, conv_w, xn):
    S = xn.shape[0]
    nt = S // TM
    t16 = TM // HALO

    def body(da_ref, pc_ref, prev_ref, next_ref, dm_ref, dmn_ref, cw_ref, xn_ref, gw_ref, dc_ref, gcw_ref):
        i = pl.program_id(0)

        @pl.when(i == 0)
        def _():
            gw_ref[...] = jnp.zeros_like(gw_ref)
            gcw_ref[...] = jnp.zeros_like(gcw_ref)

        xn = xn_ref[...]
        gw_ref[0:512, :] += _tn(da_ref[:, 0:512], xn)
        gw_ref[768:1280, :] += _tn(da_ref[:, 512:1024], xn)
        gw_ref[512:768, :] += _tn(da_ref[:, 1024:1280], xn)
        def piece_grad(k):
            rows = slice(PA_W + k * CONV_W, PA_W + (k + 1) * CONV_W)
            gw_ref[rows, :] += _tn(dc_ref[:, k * CONV_W:(k + 1) * CONV_W], xn)

        _conv_bwd_tile(pc_ref, prev_ref, next_ref, dm_ref, dmn_ref, cw_ref, dc_ref, gcw_ref, i > 0, i < nt - 1,
                       piece_grad)

    row = lambda i: (i, 0)
    fixed = lambda i: (0, 0)
    nxt = lambda i: jnp.minimum((i + 1) * t16, nt * t16 - 1)
    return pl.pallas_call(
        body,
        name="grad_w_in",
        grid=(nt,),
        in_specs=[
            pl.BlockSpec((TM, PA_W), row),
            pl.BlockSpec((TM, PC_W), row),
            _prev_rows(PC_W),
            pl.BlockSpec((HALO, PC_W), lambda i: (nxt(i), 0)),
            pl.BlockSpec((TM, CONV_W), lambda i: (i, 1)),
            pl.BlockSpec((HALO, CONV_W), lambda i: (nxt(i), 1)),
            pl.BlockSpec((CONV_K, CONV_W), fixed),
            pl.BlockSpec((TM, D_MODEL), row),
        ],
        out_specs=[
            pl.BlockSpec((IN_W, D_MODEL), fixed),
            pl.BlockSpec((TM, PC_W), row),
            pl.BlockSpec((CONV_K, CONV_W), fixed),
        ],
        out_shape=[
            jax.ShapeDtypeStruct((IN_W, D_MODEL), F32),
            jax.ShapeDtypeStruct((S, PC_W), BF16),
            jax.ShapeDtypeStruct((CONV_K, CONV_W), F32),
        ],
        compiler_params=_params(("arbitrary",)),
    )(da, pc, pc, pc, dmix, dmix, conv_w, xn)


def _adam_update(w, g, m, v):
    c1 = 1.0 - ADAM_B1 ** ADAM_STEP
    c2 = 1.0 - ADAM_B2 ** ADAM_STEP
    nm = ADAM_B1 * m + (1.0 - ADAM_B1) * g
    nv = ADAM_B2 * v + (1.0 - ADAM_B2) * (g * g)
    return -ADAM_LR * ((nm / c1) / (jnp.sqrt(nv / c2) + ADAM_EPS) + ADAM_WD * w), nm, nv


def _sum_chips_adamw(own, others, w, m, v, name):
    def body(own_ref, p_ref, w_ref, m_ref, v_ref, g_ref, d_ref, nm_ref, nv_ref):
        g = own_ref[...]
        for k in range(N_CHIP - 1):
            g = g + p_ref[k].astype(F32)
        g_ref[...] = g
        d_ref[...], nm_ref[...], nv_ref[...] = _adam_update(w_ref[...], g, m_ref[...], v_ref[...])

    shape = jax.ShapeDtypeStruct(w.shape, F32)
    return pl.pallas_call(
        body,
        name=name,
        out_shape=[shape] * 4,
        compiler_params=_params(),
    )(own, others, w, m, v)


SMALL_ROWS = 96


def _small_adamw(parts, params):
    def body(parts_ref, *rest):
        prm, outs, total = rest[:12], rest[12:29], rest[29]
        me = 4 * lax.axis_index("x") + 2 * lax.axis_index("y") + lax.axis_index("c")
        acc = parts_ref[0]
        for d in range(1, N_DEV):
            acc = acc + parts_ref[d]
        total[...] = acc
        grads = (total[0:8, :], total[8:16, :], total[16:17, 0:8],
                 total[pl.ds(pl.multiple_of(32 + me * 8, 8), CONV_K), 0:64])
        outs[0][...] = total[24:25, 0:1]
        for k, g in enumerate(grads):
            w_ref, m_ref, v_ref = prm[3 * k:3 * k + 3]
            g_ref, d_ref, nm_ref, nv_ref = outs[1 + 4 * k:5 + 4 * k]
            g_ref[...] = g
            d_ref[...], nm_ref[...], nv_ref[...] = _adam_update(w_ref[...], g, m_ref[...], v_ref[...])

    flat = [a for p in params for a in p]
    out_shape = [jax.ShapeDtypeStruct((1, 1), F32)]
    for p in params:
        out_shape += [jax.ShapeDtypeStruct(p[0].shape, F32)] * 4
    return pl.pallas_call(
        body,
        name="adamw_small",
        out_shape=out_shape,
        scratch_shapes=[pltpu.VMEM((SMALL_ROWS, 128), F32)],
        compiler_params=_params(),
    )(parts, *flat)


def kernel(x, norm_g, w_in, sinks, conv_w, w_out, final_g, loss_target, m_norm_g, m_w_in, m_sinks, m_conv_w, m_w_out, m_final_g, v_norm_g, v_w_in, v_sinks, v_conv_w, v_w_out, v_final_g):
    S = x.shape[1]
    x2 = x.reshape(S, D_MODEL)
    t2 = loss_target.reshape(S, D_MODEL)
    ng = norm_g.reshape(1, D_MODEL)
    fg = final_g.reshape(1, D_MODEL)

    cw_pad = jnp.zeros((8, 128), F32).at[0:CONV_K, 0:64].set(conv_w)
    xn, tab, wt = _prologue(x2, ng, w_in.T.astype(BF16))
    pa, pc, (wo, cw_all) = _fwd_proj(xn, wt, [w_out.astype(BF16), cw_pad])
    cw = cw_all.reshape(N_DEV, 8, 128)[:, 0:CONV_K, 0:64].transpose(1, 0, 2).reshape(CONV_K, CONV_W)
    ya, att = _attn_fwd(pa, tab, sinks)
    dh, dmix, g_wo, g_fg, loss_part = _out_loss(x2, t2, ya, pc, cw, wo, fg)
    da, g_sinks = _attn_bwd(pa, dmix, att, tab, sinks)
    g_wt, dc, g_cw = _grad_w_in(da, pc, dmix, cw, xn)
    cw_pack = jnp.pad(g_cw.reshape(CONV_K, N_DEV, 64).transpose(1, 0, 2),
                      ((0, 0), (0, 8 - CONV_K), (0, 64))).reshape(N_DEV * 8, 128)
    small = jnp.concatenate([g_fg.reshape(8, 128), g_sinks, loss_part, cw_pack], axis=0)
    grad_x, parts, own, others = _grad_x(
        da, dc, wt, x2, dh, ng, small,
        [g_wt.reshape(N_DEV, SHARD_IN, D_MODEL), g_wo.reshape(N_DEV, SHARD_OUT, D_MODEL)])
    gt, dt, nmt, nvt = _sum_chips_adamw(own[0], others[0], w_in.T, m_w_in.T, v_w_in.T, "adamw_w_in")
    grad_w_in, d_w_in, nm_w_in, nv_w_in = gt.T, dt.T, nmt.T, nvt.T
    grad_w_out, d_w_out, nm_w_out, nv_w_out = _sum_chips_adamw(
        own[1], others[1], w_out, m_w_out, v_w_out, "adamw_w_out")
    vec = lambda a: a.reshape(8, 128)
    row = lambda a: a.reshape(1, 8)
    res = _small_adamw(parts, [
        (vec(norm_g), vec(m_norm_g), vec(v_norm_g)), (vec(final_g), vec(m_final_g), vec(v_final_g)),
        (row(sinks), row(m_sinks), row(v_sinks)), (conv_w, m_conv_w, v_conv_w)])
    loss = res[0].reshape(())
    grad_norm_g, d_ng, nm_ng, nv_ng = [a.reshape(D_MODEL) for a in res[1:5]]
    grad_final_g, d_fg, nm_fg, nv_fg = [a.reshape(D_MODEL) for a in res[5:9]]
    grad_sinks, d_sk, nm_sk, nv_sk = [a.reshape(N_Q_HEADS) for a in res[9:13]]
    grad_conv_w, d_cw, nm_cw, nv_cw = res[13:17]

    return (loss, grad_x.reshape(1, S, D_MODEL), grad_norm_g, grad_w_in, grad_sinks, grad_conv_w, grad_w_out, grad_final_g,
            d_ng, d_w_in, d_sk, d_cw, d_w_out, d_fg,
            nm_ng, nm_w_in, nm_sk, nm_cw, nm_w_out, nm_fg,
            nv_ng, nv_w_in, nv_sk, nv_cw, nv_w_out, nv_fg)
```

```python
import numpy as np
import jax
import jax.numpy as jnp
from jax import lax
from jax.experimental import pallas as pl
from jax.experimental.pallas import tpu as pltpu

F32 = jnp.float32
BF16 = jnp.bfloat16
MESH = pl.DeviceIdType.MESH

D_MODEL = 1024
HEAD_DIM = 64
N_Q_HEADS = 8
GROUP = 4
ATTN_W = 512
KV_W = 128
BLK = 128
CONV_W = 512
CONV_K = 3
IN_W = 3328
PA_W = 1280
PC_W = 2048
EPS = 1e-5
ROPE_THETA = 500000.0
ROT_DIM = 16
N_DEV = 8
N_CHIP = 4
SHARD_IN = IN_W // N_DEV
SHARD_OUT = D_MODEL // N_DEV

ADAM_LR = 0.001
ADAM_B1 = 0.9
ADAM_B2 = 0.999
ADAM_EPS = 1e-08
ADAM_WD = 0.01
ADAM_STEP = 10

ACT = jnp.bfloat16

TM = 512
TQ = 1024
HALO = 16
VMEM_LIMIT = 56 * 1024 * 1024

NT_DIMS = (((1,), (1,)), ((), ()))
TN_DIMS = (((0,), (0,)), ((), ()))


def _params(sem=None):
    kw = dict(vmem_limit_bytes=VMEM_LIMIT)
    if sem is not None:
        kw["dimension_semantics"] = sem
    return pltpu.CompilerParams(**kw)


def _nt(a, b):
    return lax.dot_general(a, b, NT_DIMS, preferred_element_type=F32)


def _tn(a, b):
    return lax.dot_general(a, b, TN_DIMS, preferred_element_type=F32)


def _nn(a, b):
    return jnp.dot(a, b, preferred_element_type=F32)


def _silu(g):
    return g * jax.nn.sigmoid(g)


def _dsilu(g):
    s = jax.nn.sigmoid(g)
    return s * (1.0 + g * (1.0 - s))


class _AllGatherInSteps:
    def __init__(self, arrs, forward_step):
        self.blocks = [(a.shape, a.dtype) for a in arrs]
        self.n = len(arrs)
        self.forward_step = forward_step

    def out_shape(self):
        return [jax.ShapeDtypeStruct((N_DEV * s[0], s[1]), d) for s, d in self.blocks]

    def scratch_shapes(self):
        return [pltpu.SemaphoreType.DMA((7 * self.n,)), pltpu.SemaphoreType.DMA((7 * self.n,)),
                pltpu.SemaphoreType.DMA((self.n,))]

    def emit(self, step, n_steps, x_refs, out_refs, scratch):
        assert n_steps > self.forward_step + 1
        send_sems, recv_sems, local_sems = scratch
        x, y, c = lax.axis_index("x"), lax.axis_index("y"), lax.axis_index("c")
        me, sibling = (x, y, c), (x, y, 1 - c)
        chips = [(1 - x, y), (x, 1 - y), (1 - x, 1 - y)]

        def rows(a, px, py, pc):
            m = self.blocks[a][0][0]
            return out_refs[a].at[pl.ds((4 * px + 2 * py + pc) * m, m), :]

        def copy(a, k, block, to, src=None):
            return pltpu.make_async_remote_copy(
                src_ref=rows(a, *block) if src is None else src, dst_ref=rows(a, *block),
                send_sem=send_sems.at[a * 7 + k], recv_sem=recv_sems.at[a * 7 + k],
                device_id=to, device_id_type=MESH)

        def mine(a):
            return pltpu.make_async_copy(x_refs[a], rows(a, *me), local_sems.at[a])

        def first(a):
            return ([copy(a, 0, me, sibling, src=x_refs[a])]
                    + [copy(a, 1 + j, me, (*chip, c), src=x_refs[a]) for j, chip in enumerate(chips)])

        def passed(a):
            return [copy(a, 4 + j, (*chip, c), sibling) for j, chip in enumerate(chips)]

        @pl.when(step == 0)
        def _():
            for a in range(self.n):
                mine(a).start()
                for cp in first(a):
                    cp.start()

        @pl.when(step == self.forward_step)
        def _():
            for j, chip in enumerate(chips):
                for a in range(self.n):
                    copy(a, 1 + j, (*chip, c), me).wait_recv()
                    copy(a, 4 + j, (*chip, c), sibling).start()

        def finish():
            @pl.when(step == n_steps - 1)
            def _():
                for a in range(self.n):
                    copy(a, 0, sibling, me).wait_recv()
                    for j, chip in enumerate(chips):
                        copy(a, 4 + j, (*chip, 1 - c), me).wait_recv()
                    for cp in first(a) + passed(a):
                        cp.wait_send()
                    mine(a).wait()

        return finish


class _ReduceScatter:
    def __init__(self, grads):
        self.shapes = [g.shape[1:] for g in grads]
        self.n = len(grads)
        self.items = tuple((a, r) for r in (1, 2, 3, 0) for a in range(self.n))
        self.steps = len(self.items) + 2

    def out_shape(self):
        own = [jax.ShapeDtypeStruct(s, F32) for s in self.shapes]
        ici = [jax.ShapeDtypeStruct((N_CHIP - 1,) + s, BF16) for s in self.shapes]
        land = [jax.ShapeDtypeStruct((N_CHIP,) + s, F32) for s in self.shapes]
        return own + ici + land

    def scratch_shapes(self):
        n_items = len(self.items)
        return ([pltpu.VMEM((2,) + s, F32) for s in self.shapes]
                + [pltpu.VMEM((N_CHIP - 1,) + s, BF16) for s in self.shapes]
                + [pltpu.VMEM(s, F32) for s in self.shapes]
                + [pltpu.SemaphoreType.DMA((self.n * N_CHIP,))] * 2
                + [pltpu.SemaphoreType.DMA((2 * n_items,))]
                + [pltpu.SemaphoreType.DMA((self.n * (N_CHIP - 1),))] * 2
                + [pltpu.SemaphoreType.DMA((self.n,))])

    def emit(self, step, n_steps, g_refs, out_refs, scratch):
        assert n_steps > self.steps
        n = self.n
        own_refs, ici_refs, land_refs = out_refs[:n], out_refs[n:2 * n], out_refs[2 * n:]
        stage, pair_bf, pair_own = scratch[:n], scratch[n:2 * n], scratch[2 * n:3 * n]
        sib_send, sib_recv, load_sems, ici_send, ici_recv, own_sems = scratch[3 * n:]
        x, y, c = lax.axis_index("x"), lax.axis_index("y"), lax.axis_index("c")

        def chip_of(r):
            return (x ^ (r >> 1), y ^ (r & 1))

        def block_of(r, core):
            cx, cy = chip_of(r)
            return 4 * cx + 2 * cy + core

        def to_sibling(a, r):
            return pltpu.make_async_remote_copy(
                src_ref=g_refs[a].at[block_of(r, 1 - c)], dst_ref=land_refs[a].at[r],
                send_sem=sib_send.at[a * N_CHIP + r], recv_sem=sib_recv.at[a * N_CHIP + r],
                device_id=(x, y, 1 - c), device_id_type=MESH)

        def loads(k):
            a, r = self.items[k]
            return (pltpu.make_async_copy(g_refs[a].at[block_of(r, c)], stage[a].at[0], load_sems.at[2 * k]),
                    pltpu.make_async_copy(land_refs[a].at[r], stage[a].at[1], load_sems.at[2 * k + 1]))

        def to_owner(k):
            a, r = self.items[k]
            if r == 0:
                return pltpu.make_async_copy(pair_own[a], own_refs[a], own_sems.at[a])
            return pltpu.make_async_remote_copy(
                src_ref=pair_bf[a].at[r - 1], dst_ref=ici_refs[a].at[r - 1],
                send_sem=ici_send.at[a * (N_CHIP - 1) + r - 1], recv_sem=ici_recv.at[a * (N_CHIP - 1) + r - 1],
                device_id=(*chip_of(r), c), device_id_type=MESH)

        @pl.when(step == 0)
        def _():
            for a, r in self.items:
                to_sibling(a, r).start()

        for k, (a, r) in enumerate(self.items):
            @pl.when(step == 1 + k)
            def _(k=k, a=a, r=r):
                to_sibling(a, r).wait_recv()
                for cp in loads(k):
                    cp.start()

            @pl.when(step == 2 + k)
            def _(k=k, a=a, r=r):
                for cp in loads(k):
                    cp.wait()
                total = stage[a][0] + stage[a][1]
                if r == 0:
                    pair_own[a][...] = total
                else:
                    pair_bf[a][r - 1] = total.astype(BF16)
                to_owner(k).start()

        def finish():
            @pl.when(step == n_steps - 1)
            def _():
                for k, (a, r) in enumerate(self.items):
                    if r == 0:
                        to_owner(k).wait()
                    else:
                        to_owner(k).wait_send()
                        to_owner(k).wait_recv()
                for a, r in self.items:
                    to_sibling(a, r).wait_send()

        return finish


def _prologue(x, norm_g, w_shard):
    S = x.shape[0]
    n_steps = S // TM
    half = ROT_DIM // 2
    pos = jnp.arange(S, dtype=jnp.int32).astype(F32)
    inv_freq = ROPE_THETA ** (-jnp.arange(0, ROT_DIM, 2, dtype=F32) / ROT_DIM)
    ang = inv_freq[:, None] * pos[None, :]
    cs = jnp.concatenate([jnp.cos(ang), jnp.sin(ang)], axis=0)
    ag = _AllGatherInSteps([w_shard], forward_step=n_steps - 3)

    def body(x_ref, g_ref, cs_ref, w_ref, xn_ref, tab_ref, wt_ref, *ag_scratch):
        step = pl.program_id(0)
        finish = ag.emit(step, n_steps, [w_ref], [wt_ref], ag_scratch)
        xv = x_ref[...]
        r = lax.rsqrt(jnp.mean(xv * xv, axis=-1, keepdims=True) + EPS)
        xn_ref[...] = (xv * r * g_ref[...]).astype(BF16)

        xt = jnp.concatenate([cs_ref[...], jnp.zeros((128 - 2 * half, TM), F32)], axis=0).T
        lane = lax.broadcasted_iota(jnp.int32, (TM, 128), 1)
        rr = lane & (HEAD_DIM - 1)
        first = lane < HEAD_DIM

        def at(shift_first, shift_second):
            return jnp.where(first, pltpu.roll(xt, shift_first, 1) if shift_first else xt,
                             pltpu.roll(xt, shift_second, 1))

        cos_lo, cos_hi = at(0, HEAD_DIM), at(half, HEAD_DIM + half)
        sin_lo, sin_hi = at(128 - half, HEAD_DIM - half), at(0, HEAD_DIM)
        tab_ref[:, 0:128] = jnp.where(rr < half, cos_lo, jnp.where(rr < ROT_DIM, cos_hi, 1.0))
        tab_ref[:, 128:256] = jnp.where(rr < half, -sin_lo, 0.0)
        tab_ref[:, 256:384] = jnp.where((rr >= half) & (rr < ROT_DIM), sin_hi, 0.0)
        finish()

    any_spec = pl.BlockSpec(memory_space=pl.ANY)
    return pl.pallas_call(
        body,
        name="prologue_all_gather_w_in",
        grid=(n_steps,),
        in_specs=[
            pl.BlockSpec((TM, D_MODEL), lambda i: (i, 0)),
            pl.BlockSpec((1, D_MODEL), lambda i: (0, 0)),
            pl.BlockSpec((2 * half, TM), lambda i: (0, i)),
            any_spec,
        ],
        out_specs=[
            pl.BlockSpec((TM, D_MODEL), lambda i: (i, 0)),
            pl.BlockSpec((TM, 384), lambda i: (i, 0)),
            any_spec,
        ],
        out_shape=[
            jax.ShapeDtypeStruct((S, D_MODEL), BF16),
            jax.ShapeDtypeStruct((S, 384), F32),
        ] + ag.out_shape(),
        scratch_shapes=ag.scratch_shapes(),
        compiler_params=_params(("arbitrary",)),
    )(x, norm_g, cs, w_shard)


def _fwd_proj(xn, wt, later):
    S = xn.shape[0]

    n_steps = S // TM
    ag = _AllGatherInSteps(later, forward_step=3)

    def body(xn_ref, wt_ref, *rest):
        later_refs, rest = rest[:ag.n], rest[ag.n:]
        pa_ref, pc_ref = rest[:2]
        gathered, ag_scratch = rest[2:2 + ag.n], rest[2 + ag.n:]
        step = pl.program_id(0)
        finish = ag.emit(step, n_steps, later_refs, gathered, ag_scratch)
        xn = xn_ref[...]
        pa_ref[:, 0:512] = _nt(xn, wt_ref[0:512, :]).astype(ACT)
        pa_ref[:, 512:1024] = _nt(xn, wt_ref[768:1280, :]).astype(ACT)
        pa_ref[:, 1024:1280] = _nt(xn, wt_ref[512:768, :]).astype(ACT)
        pc_ref[...] = _nt(xn, wt_ref[1280:3328, :]).astype(ACT)
        finish()

    any_spec = pl.BlockSpec(memory_space=pl.ANY)
    outs = pl.pallas_call(
        body,
        name="fwd_proj_all_gather",
        grid=(n_steps,),
        in_specs=[
            pl.BlockSpec((TM, D_MODEL), lambda i: (i, 0)),
            pl.BlockSpec((IN_W, D_MODEL), lambda i: (0, 0)),
        ] + [any_spec] * ag.n,
        out_specs=[
            pl.BlockSpec((TM, PA_W), lambda i: (i, 0)),
            pl.BlockSpec((TM, PC_W), lambda i: (i, 0)),
        ] + [any_spec] * ag.n,
        out_shape=[
            jax.ShapeDtypeStruct((S, PA_W), ACT),
            jax.ShapeDtypeStruct((S, PC_W), ACT),
        ] + ag.out_shape(),
        scratch_shapes=ag.scratch_shapes(),
        compiler_params=_params(("arbitrary",)),
    )(xn, wt, *later)
    return outs[0], outs[1], outs[2:]


def _rope(t, tab):
    return (t * tab[:, 0:128] + pltpu.roll(t, 120, 1) * tab[:, 128:256]
            + pltpu.roll(t, 8, 1) * tab[:, 256:384])


def _rope_t(d, tab):
    return (d * tab[:, 0:128] + pltpu.roll(d * tab[:, 128:256], 8, 1)
            + pltpu.roll(d * tab[:, 256:384], 120, 1))


def _fill_kv(kall, kvc_ref, kvp_ref, tabc_ref, tabp_ref):
    for lo, kv_ref, tab_ref, n in ((0, kvp_ref, tabp_ref, BLK), (BLK, kvc_ref, tabc_ref, TQ)):
        k = _rope(kv_ref[:, 0:128].astype(F32), tab_ref[...])
        v = kv_ref[:, 128:256].astype(F32)
        kall[0, lo:lo + n, :] = k.astype(BF16)
        kall[1, lo:lo + n, :] = pltpu.roll(k, 64, 1).astype(BF16)
        kall[2, lo:lo + n, :] = v.astype(BF16)
        kall[3, lo:lo + n, :] = pltpu.roll(v, 64, 1).astype(BF16)


HEADS = (((0, 0), (1, 0), (2, 1), (3, 1)), ((0, 1), (1, 1), (2, 0), (3, 0)))


def _upper():
    kj = lax.broadcasted_iota(jnp.int32, (BLK, 4 * BLK), 0)
    qi = lax.broadcasted_iota(jnp.int32, (BLK, 4 * BLK), 1) & (BLK - 1)
    return kj > qi


def _merge(upper, both):
    return jnp.where(upper, both[0:BLK, :], both[BLK:2 * BLK, :])


def _split_store(ref, s, upper_b, vb):
    first = vb * upper_b
    ref[s, 0:BLK, :] = first
    ref[s, BLK:2 * BLK, :] = vb - first


def _sink_rows(sink_ref):
    return [jnp.concatenate([jnp.full((1, BLK), sink_ref[2 * p + e], F32) for p, e in HEADS[s]], axis=1)
            for s in range(2)]


def _stack_heads(ref, s, half, pairs):
    for a, (p, e) in enumerate(HEADS[s]):
        ref[s, a * BLK:(a + 1) * BLK, :] = jnp.where(half[e], pairs[p], 0.0).astype(BF16)


def _unstack_pair(half, outs, p):
    lo = 0 if p < 2 else 1
    rows = slice(p * BLK, (p + 1) * BLK)
    return jnp.where(half[0], outs[lo][rows, :], outs[1 - lo][rows, :])


def _softmax(sm, sinks):
    m = jnp.maximum(jnp.max(sm, axis=0, keepdims=True), sinks)
    p = jnp.exp(sm - m)
    es = jnp.exp(sinks - m)
    inv = 1.0 / (jnp.sum(p, axis=0, keepdims=True) + es)
    return p * inv, es * inv


def _scores(kk, q_stack, first):
    st = _nt(kk, q_stack)
    prev = st[0:BLK, :]
    if first is not None:
        prev = prev + jnp.where(first, -jnp.inf, 0.0)
    return prev, st[BLK:2 * BLK, :]


def _attn_specs(tile):
    nb = TQ // BLK
    prev = lambda i: jnp.maximum(tile(i) * nb - 1, 0)
    return [
        pl.BlockSpec(memory_space=pltpu.SMEM),
        pl.BlockSpec((TQ, ATTN_W), lambda i: (tile(i), 0)),
        pl.BlockSpec((TQ, ATTN_W), lambda i: (tile(i), 1)),
        pl.BlockSpec((TQ, 2 * KV_W), lambda i: (tile(i), 4)),
        pl.BlockSpec((BLK, 2 * KV_W), lambda i: (prev(i), 4)),
        pl.BlockSpec((TQ, 384), lambda i: (tile(i), 0)),
        pl.BlockSpec((BLK, 384), lambda i: (prev(i), 0)),
    ]


def _attn_fwd(pa, tab, sinks):
    S = pa.shape[0]
    nb = TQ // BLK

    def body(sink_ref, q_ref, g_ref, kvc_ref, kvp_ref, tabc_ref, tabp_ref, o_ref, att_ref, pm_ref, ps_ref,
             kall, q_sc, p_sc):
        i = pl.program_id(0)
        _fill_kv(kall, kvc_ref, kvp_ref, tabc_ref, tabp_ref)
        lane = lax.broadcasted_iota(jnp.int32, (BLK, 128), 1)
        half = [lane < HEAD_DIM, lane >= HEAD_DIM]
        upper = _upper()
        upper_b = upper.astype(BF16)
        sinks = _sink_rows(sink_ref)
        for j in range(nb):
            rq = slice(j * BLK, (j + 1) * BLK)
            rk = slice(j * BLK, (j + 2) * BLK)
            tab = tabc_ref[rq, :]
            qr = [_rope(q_ref[rq, p * 128:(p + 1) * 128].astype(F32), tab) * 0.125 for p in range(4)]
            outs = []
            for s in range(2):
                _stack_heads(q_sc, s, half, qr)
                prev, cur = _scores(kall[s, rk, :], q_sc[s], i == 0 if j == 0 else None)
                prob, psink = _softmax(jnp.where(upper, prev, cur), sinks[s])
                pb = prob.astype(BF16)
                pm_ref[(2 * j + s) * BLK:(2 * j + s + 1) * BLK, :] = pb
                ps_ref[2 * j + s:2 * j + s + 1, :] = psink
                _split_store(p_sc, s, upper_b, pb)
                outs.append(_tn(p_sc[s], kall[2 + s, rk, :]))
            for p in range(4):
                cols = slice(p * 128, (p + 1) * 128)
                att = _unstack_pair(half, outs, p)
                att_ref[rq, cols] = att.astype(BF16)
                o_ref[rq, cols] = (att * _silu(g_ref[rq, cols].astype(F32))).astype(BF16)

    return pl.pallas_call(
        body,
        name="attn_fwd",
        grid=(S // TQ,),
        in_specs=_attn_specs(lambda i: i),
        out_specs=[pl.BlockSpec((TQ, ATTN_W), lambda i: (i, 0))] * 2 + [
            pl.BlockSpec((2 * TQ, 4 * BLK), lambda i: (i, 0)),
            pl.BlockSpec((2 * nb, 4 * BLK), lambda i: (i, 0)),
        ],
        out_shape=[jax.ShapeDtypeStruct((S, ATTN_W), BF16)] * 2 + [
            jax.ShapeDtypeStruct((2 * S, 4 * BLK), BF16),
            jax.ShapeDtypeStruct((2 * S // BLK, 4 * BLK), F32),
        ],
        scratch_shapes=[
            pltpu.VMEM((4, BLK + TQ, 128), BF16),
            pltpu.VMEM((2, 4 * BLK, 128), BF16),
            pltpu.VMEM((2, 2 * BLK, 4 * BLK), BF16),
        ],
        compiler_params=_params(("arbitrary",)),
    )(sinks, pa, pa, pa, pa, tab, tab)


def _shift_down(u, halo_ref, has_prev):
    def halo_u(r):
        hu = halo_ref[r:r + 1, 512:1024].astype(F32) * halo_ref[r:r + 1, 1024:1536].astype(F32)
        return jnp.where(has_prev, hu, 0.0)

    row = lax.broadcasted_iota(jnp.int32, u.shape, 0)
    um1 = jnp.where(row == 0, halo_u(HALO - 1), pltpu.roll(u, 1, 0))
    um2 = jnp.where(row == 0, halo_u(HALO - 2), jnp.where(row == 1, halo_u(HALO - 1), pltpu.roll(u, 2, 0)))
    return um1, um2


def _conv_tile(pc_ref, halo_ref, w_ref, has_prev):
    b = pc_ref[:, 0:512].astype(F32)
    c = pc_ref[:, 512:1024].astype(F32)
    hh = pc_ref[:, 1024:1536].astype(F32)
    gc = pc_ref[:, 1536:2048].astype(F32)
    u = c * hh
    um1, um2 = _shift_down(u, halo_ref, has_prev)
    cv = w_ref[0:1, :] * um2 + w_ref[1:2, :] * um1 + w_ref[2:3, :] * u
    return b, c, hh, gc, u, um1, um2, cv


def _prev_rows(width, col=0):
    return pl.BlockSpec((HALO, width), lambda i: (jnp.maximum(i * (TM // HALO) - 1, 0), col))


def _out_loss(x, target, ya, pc, conv_w, w_out, final_g):
    S = x.shape[0]

    def body(x_ref, t_ref, ya_ref, pc_ref, halo_ref, cw_ref, wo_ref, fg_ref,
             dh_ref, dmix_ref, gwo_ref, gfg_ref, loss_ref):
        @pl.when(pl.program_id(0) == 0)
        def _():
            gwo_ref[...] = jnp.zeros_like(gwo_ref)
            gfg_ref[...] = jnp.zeros_like(gfg_ref)
            loss_ref[...] = jnp.zeros_like(loss_ref)

        b, _, _, gc, _, _, _, cv = _conv_tile(pc_ref, halo_ref, cw_ref, pl.program_id(0) > 0)
        yc = (b * cv * _silu(gc)).astype(BF16)
        mix = jnp.concatenate([ya_ref[...], yc], axis=1)
        wo = wo_ref[...]
        fg = fg_ref[...]
        h = x_ref[...] + _nn(mix, wo)
        r = lax.rsqrt(jnp.mean(h * h, axis=-1, keepdims=True) + EPS)
        n = h * r
        err = n * fg - t_ref[...]
        loss_ref[...] += jnp.broadcast_to(
            0.5 * jnp.sum(jnp.mean(err * err, axis=-1, keepdims=True), axis=0, keepdims=True), (8, 128))
        gfg_ref[...] += jnp.sum(err * n, axis=0, keepdims=True) * (1.0 / D_MODEL)
        dyg = err * (fg * (1.0 / D_MODEL))
        dh = r * (dyg - n * jnp.mean(dyg * n, axis=-1, keepdims=True))
        dh_ref[...] = dh
        dhb = dh.astype(BF16)
        dmix_ref[...] = _nt(dhb, wo).astype(ACT)
        gwo_ref[...] += _tn(mix, dhb)

    row = lambda i: (i, 0)
    fixed = lambda i: (0, 0)
    return pl.pallas_call(
        body,
        name="out_loss",
        grid=(S // TM,),
        in_specs=[
            pl.BlockSpec((TM, D_MODEL), row),
            pl.BlockSpec((TM, D_MODEL), row),
            pl.BlockSpec((TM, ATTN_W), row),
            pl.BlockSpec((TM, PC_W), row),
            _prev_rows(PC_W),
            pl.BlockSpec((CONV_K, CONV_W), fixed),
            pl.BlockSpec((D_MODEL, D_MODEL), fixed),
            pl.BlockSpec((1, D_MODEL), fixed),
        ],
        out_specs=[
            pl.BlockSpec((TM, D_MODEL), row),
            pl.BlockSpec((TM, D_MODEL), row),
            pl.BlockSpec((D_MODEL, D_MODEL), fixed),
            pl.BlockSpec((1, D_MODEL), fixed),
            pl.BlockSpec((8, 128), fixed),
        ],
        out_shape=[
            jax.ShapeDtypeStruct((S, D_MODEL), F32),
            jax.ShapeDtypeStruct((S, D_MODEL), ACT),
            jax.ShapeDtypeStruct((D_MODEL, D_MODEL), F32),
            jax.ShapeDtypeStruct((1, D_MODEL), F32),
            jax.ShapeDtypeStruct((8, 128), F32),
        ],
        compiler_params=_params(("arbitrary",)),
    )(x, target, ya, pc, pc, conv_w, w_out, final_g)


def _attn_bwd(pa, dmix, att, probs, psinks, tab, sinks):
    S = pa.shape[0]
    nt = S // TQ
    nb = TQ // BLK

    def body(sink_ref, q_ref, g_ref, kvc_ref, kvp_ref, tabc_ref, tabp_ref, dm_ref, att_ref, pm_ref, ps_ref,
             d_ref, dsink_ref, kall, dkv, carry, q_sc, do_sc, p_sc, ds_sc, dsink_acc):
        step = pl.program_id(0)

        @pl.when(step == 0)
        def _():
            carry[...] = jnp.zeros_like(carry)
            dsink_acc[...] = jnp.zeros_like(dsink_acc)

        _fill_kv(kall, kvc_ref, kvp_ref, tabc_ref, tabp_ref)
        dkv[0:TQ, :] = jnp.zeros((TQ, 2 * KV_W), F32)
        dkv[TQ:TQ + BLK, :] = carry[...]
        lane = lax.broadcasted_iota(jnp.int32, (BLK, 128), 1)
        half = [lane < HEAD_DIM, lane >= HEAD_DIM]
        upper = _upper()
        upper_b = upper.astype(BF16)
        for j in range(nb):
            rq = slice(j * BLK, (j + 1) * BLK)
            rk = slice(j * BLK, (j + 2) * BLK)
            tab = tabc_ref[rq, :]
            pair = [slice(p * 128, (p + 1) * 128) for p in range(4)]
            qr = [_rope(q_ref[rq, c].astype(F32), tab) * 0.125 for c in pair]
            g = [g_ref[rq, c].astype(F32) for c in pair]
            da = [dm_ref[rq, c].astype(F32) for c in pair]
            do = [da[p] * _silu(g[p]) for p in range(4)]
            dqs, dks, dvs = [], [], []
            for s in range(2):
                kk = kall[s, rk, :]
                vv = kall[2 + s, rk, :]
                _stack_heads(q_sc, s, half, qr)
                _stack_heads(do_sc, s, half, do)
                pb = pm_ref[(2 * j + s) * BLK:(2 * j + s + 1) * BLK, :]
                prob = pb.astype(F32)
                _split_store(p_sc, s, upper_b, pb)
                dprob = _merge(upper, _nt(vv, do_sc[s]))
                dsum = jnp.sum(dprob * prob, axis=0, keepdims=True)
                _split_store(ds_sc, s, upper_b, (prob * (dprob - dsum)).astype(BF16))
                dsink_acc[s, 0:1, :] += ps_ref[2 * j + s:2 * j + s + 1, :] * dsum
                dqs.append(_tn(ds_sc[s], kk))
                dks.append(_nn(ds_sc[s], q_sc[s]))
                dvs.append(_nn(p_sc[s], do_sc[s]))
            for p in range(4):
                d_ref[rq, pair[p]] = _rope_t(_unstack_pair(half, dqs, p) * 0.125, tab).astype(BF16)
                d_ref[rq, 512 + p * 128:512 + (p + 1) * 128] = (
                    da[p] * att_ref[rq, pair[p]].astype(F32) * _dsilu(g[p])).astype(BF16)
            dkv[rk, 0:128] += dks[0] + pltpu.roll(dks[1], 64, 1)
            dkv[rk, 128:256] += dvs[0] + pltpu.roll(dvs[1], 64, 1)
        d_ref[:, 1024:1152] = _rope_t(dkv[BLK:BLK + TQ, 0:128], tabc_ref[...]).astype(BF16)
        d_ref[:, 1152:1280] = dkv[BLK:BLK + TQ, 128:256].astype(BF16)
        carry[...] = dkv[0:BLK, :]

        @pl.when(step == nt - 1)
        def _():
            lanes = lax.broadcasted_iota(jnp.int32, (8, 128), 1)
            out = jnp.zeros((8, 128), F32)
            for s in range(2):
                for a, (p, e) in enumerate(HEADS[s]):
                    tot = jnp.sum(dsink_acc[s, 0:1, a * BLK:(a + 1) * BLK], axis=1, keepdims=True)
                    out = jnp.where(lanes == 2 * p + e, -tot, out)
            dsink_ref[...] = out

    rev = lambda s: nt - 1 - s
    return pl.pallas_call(
        body,
        name="attn_bwd",
        grid=(nt,),
        in_specs=_attn_specs(rev) + [pl.BlockSpec((TQ, ATTN_W), lambda s: (nt - 1 - s, 0))] * 2 + [
            pl.BlockSpec((2 * TQ, 4 * BLK), lambda s: (nt - 1 - s, 0)),
            pl.BlockSpec((2 * nb, 4 * BLK), lambda s: (nt - 1 - s, 0)),
        ],
        out_specs=[
            pl.BlockSpec((TQ, PA_W), lambda s: (nt - 1 - s, 0)),
            pl.BlockSpec((8, 128), lambda s: (0, 0)),
        ],
        out_shape=[
            jax.ShapeDtypeStruct((S, PA_W), BF16),
            jax.ShapeDtypeStruct((8, 128), F32),
        ],
        scratch_shapes=[
            pltpu.VMEM((4, BLK + TQ, 128), BF16),
            pltpu.VMEM((BLK + TQ, 2 * KV_W), F32),
            pltpu.VMEM((BLK, 2 * KV_W), F32),
            pltpu.VMEM((2, 4 * BLK, 128), BF16),
            pltpu.VMEM((2, 4 * BLK, 128), BF16),
            pltpu.VMEM((2, 2 * BLK, 4 * BLK), BF16),
            pltpu.VMEM((2, 2 * BLK, 4 * BLK), BF16),
            pltpu.VMEM((2, 8, 4 * BLK), F32),
        ],
        compiler_params=_params(("arbitrary",)),
    )(sinks, pa, pa, pa, pa, tab, tab, dmix, att, probs, psinks)


def _conv_bwd_tile(pc_ref, prev_ref, next_ref, dm_ref, dmn_ref, w_ref, d_ref, gw_ref, has_prev, has_next,
                   on_piece):
    rows = pc_ref.shape[0]
    w0, w1, w2 = w_ref[0:1, :], w_ref[1:2, :], w_ref[2:3, :]
    b, c, hh, gc, u, um1, um2, cv = _conv_tile(pc_ref, prev_ref, w_ref, has_prev)
    sg = _silu(gc)
    dy = dm_ref[...].astype(F32)
    dcv = dy * b * sg

    def next_dcv(r):
        nd = (dmn_ref[r:r + 1, :].astype(F32) * next_ref[r:r + 1, 0:512].astype(F32)
              * _silu(next_ref[r:r + 1, 1536:2048].astype(F32)))
        return jnp.where(has_next, nd, 0.0)

    row = lax.broadcasted_iota(jnp.int32, (rows, CONV_W), 0)
    dp1 = jnp.where(row == rows - 1, next_dcv(0), pltpu.roll(dcv, rows - 1, 0))
    dp2 = jnp.where(row == rows - 1, next_dcv(1),
                    jnp.where(row == rows - 2, next_dcv(0), pltpu.roll(dcv, rows - 2, 0)))
    du = w2 * dcv + w1 * dp1 + w0 * dp2
    pieces = (lambda: dy * cv * sg, lambda: du * hh, lambda: du * c, lambda: dy * b * cv * _dsilu(gc))
    for k, piece in enumerate(pieces):
        d_ref[:, k * CONV_W:(k + 1) * CONV_W] = piece().astype(BF16)
        on_piece(k)
    gw_ref[0:1, :] += jnp.sum(dcv * um2, axis=0, keepdims=True)
    gw_ref[1:2, :] += jnp.sum(dcv * um1, axis=0, keepdims=True)
    gw_ref[2:3, :] += jnp.sum(dcv * u, axis=0, keepdims=True)


def _grad_x(da, dc, wt, x, dh, norm_g, small, grads):
    S = x.shape[0]
    n_steps = S // TM
    rs = _ReduceScatter(grads)
    n_rs_out = len(rs.out_shape())
    small_rows = 8 + small.shape[0]

    def body(da_ref, dc_ref, wt_ref, x_ref, dh_ref, g_ref, small_ref, *rest):
        grad_refs, rest = rest[:rs.n], rest[rs.n:]
        gx_ref, all_ref = rest[:2]
        rs_out, rest = rest[2:2 + n_rs_out], rest[2 + n_rs_out:]
        gng, stage, small_send, small_recv, small_own = rest[:5]
        rs_scratch = rest[5:]
        step = pl.program_id(0)
        finish = rs.emit(step, n_steps, grad_refs, rs_out, rs_scratch)

        @pl.when(step == 0)
        def _():
            gng[...] = jnp.zeros_like(gng)

        dxn = (_nn(da_ref[:, 0:512], wt_ref[0:512, :]) + _nn(da_ref[:, 512:1024], wt_ref[768:1280, :])
               + _nn(da_ref[:, 1024:1280], wt_ref[512:768, :]) + _nn(dc_ref[...], wt_ref[1280:3328, :]))
        xv = x_ref[...]
        r = lax.rsqrt(jnp.mean(xv * xv, axis=-1, keepdims=True) + EPS)
        n = xv * r
        gng[...] += jnp.sum(dxn * n, axis=0, keepdims=True)
        dxg = dxn * g_ref[...]
        gx_ref[...] = dh_ref[...] + r * (dxg - n * jnp.mean(dxg * n, axis=-1, keepdims=True))

        @pl.when(step == n_steps - 1)
        def _():
            x_, y_, c_ = lax.axis_index("x"), lax.axis_index("y"), lax.axis_index("c")
            me = 4 * x_ + 2 * y_ + c_
            for q in range(8):
                stage[q:q + 1, :] = gng[:, q * 128:(q + 1) * 128]
            stage[8:small_rows, :] = small_ref[...]
            own = pltpu.make_async_copy(stage, all_ref.at[me], small_own)
            own.start()
            sends = []
            for k in range(1, N_DEV):
                cp = pltpu.make_async_remote_copy(
                    src_ref=stage, dst_ref=all_ref.at[me],
                    send_sem=small_send.at[k - 1], recv_sem=small_recv.at[k - 1],
                    device_id=(x_ ^ (k >> 2), y_ ^ ((k >> 1) & 1), c_ ^ (k & 1)), device_id_type=MESH)
                cp.start()
                sends.append(cp)
            for cp in sends:
                cp.wait_send()
                cp.wait_recv()
            own.wait()

        finish()

    row = lambda i: (i, 0)
    fixed = lambda i: (0, 0)
    any_spec = pl.BlockSpec(memory_space=pl.ANY)
    outs = pl.pallas_call(
        body,
        name="grad_x_reduce_scatter",
        grid=(n_steps,),
        in_specs=[
            pl.BlockSpec((TM, PA_W), row),
            pl.BlockSpec((TM, PC_W), row),
            pl.BlockSpec((IN_W, D_MODEL), fixed),
            pl.BlockSpec((TM, D_MODEL), row),
            pl.BlockSpec((TM, D_MODEL), row),
            pl.BlockSpec((1, D_MODEL), fixed),
            pl.BlockSpec(small.shape, fixed),
        ] + [any_spec] * rs.n,
        out_specs=[pl.BlockSpec((TM, D_MODEL), row), any_spec] + [any_spec] * n_rs_out,
        out_shape=[jax.ShapeDtypeStruct((S, D_MODEL), F32),
                   jax.ShapeDtypeStruct((N_DEV, small_rows, 128), F32)] + rs.out_shape(),
        scratch_shapes=[
            pltpu.VMEM((1, D_MODEL), F32),
            pltpu.VMEM((small_rows, 128), F32),
            pltpu.SemaphoreType.DMA((N_DEV - 1,)),
            pltpu.SemaphoreType.DMA((N_DEV - 1,)),
            pltpu.SemaphoreType.DMA,
        ] + rs.scratch_shapes(),
        compiler_params=_params(("arbitrary",)),
    )(da, dc, wt, x, dh, norm_g, small, *grads)
    return outs[0], outs[1], outs[2:2 + rs.n], outs[2 + rs.n:2 + 2 * rs.n]


def _grad_w_in(da, pc, dmix, conv_w, xn):
    S = xn.shape[0]
    nt = S // TM
    t16 = TM // HALO

    def body(da_ref, pc_ref, prev_ref, next_ref, dm_ref, dmn_ref, cw_ref, xn_ref, gw_ref, dc_ref, gcw_ref):
        i = pl.program_id(0)

        @pl.when(i == 0)
        def _():
            gw_ref[...] = jnp.zeros_like(gw_ref)
            gcw_ref[...] = jnp.zeros_like(gcw_ref)

        xn = xn_ref[...]
        gw_ref[0:512, :] += _tn(da_ref[:, 0:512], xn)
        gw_ref[768:1280, :] += _tn(da_ref[:, 512:1024], xn)
        gw_ref[512:768, :] += _tn(da_ref[:, 1024:1280], xn)
        def piece_grad(k):
            rows = slice(PA_W + k * CONV_W, PA_W + (k + 1) * CONV_W)
            gw_ref[rows, :] += _tn(dc_ref[:, k * CONV_W:(k + 1) * CONV_W], xn)

        _conv_bwd_tile(pc_ref, prev_ref, next_ref, dm_ref, dmn_ref, cw_ref, dc_ref, gcw_ref, i > 0, i < nt - 1,
                       piece_grad)

    row = lambda i: (i, 0)
    fixed = lambda i: (0, 0)
    nxt = lambda i: jnp.minimum((i + 1) * t16, nt * t16 - 1)
    return pl.pallas_call(
        body,
        name="grad_w_in",
        grid=(nt,),
        in_specs=[
            pl.BlockSpec((TM, PA_W), row),
            pl.BlockSpec((TM, PC_W), row),
            _prev_rows(PC_W),
            pl.BlockSpec((HALO, PC_W), lambda i: (nxt(i), 0)),
            pl.BlockSpec((TM, CONV_W), lambda i: (i, 1)),
            pl.BlockSpec((HALO, CONV_W), lambda i: (nxt(i), 1)),
            pl.BlockSpec((CONV_K, CONV_W), fixed),
            pl.BlockSpec((TM, D_MODEL), row),
        ],
        out_specs=[
            pl.BlockSpec((IN_W, D_MODEL), fixed),
            pl.BlockSpec((TM, PC_W), row),
            pl.BlockSpec((CONV_K, CONV_W), fixed),
        ],
        out_shape=[
            jax.ShapeDtypeStruct((IN_W, D_MODEL), F32),
            jax.ShapeDtypeStruct((S, PC_W), BF16),
            jax.ShapeDtypeStruct((CONV_K, CONV_W), F32),
        ],
        compiler_params=_params(("arbitrary",)),
    )(da, pc, pc, pc, dmix, dmix, conv_w, xn)


def _adam_update(w, g, m, v):
    c1 = 1.0 - ADAM_B1 ** ADAM_STEP
    c2 = 1.0 - ADAM_B2 ** ADAM_STEP
    nm = ADAM_B1 * m + (1.0 - ADAM_B1) * g
    nv = ADAM_B2 * v + (1.0 - ADAM_B2) * (g * g)
    return -ADAM_LR * ((nm / c1) / (jnp.sqrt(nv / c2) + ADAM_EPS) + ADAM_WD * w), nm, nv


def _sum_chips_adamw(own, others, w, m, v, name):
    def body(own_ref, p_ref, w_ref, m_ref, v_ref, g_ref, d_ref, nm_ref, nv_ref):
        g = own_ref[...]
        for k in range(N_CHIP - 1):
            g = g + p_ref[k].astype(F32)
        g_ref[...] = g
        d_ref[...], nm_ref[...], nv_ref[...] = _adam_update(w_ref[...], g, m_ref[...], v_ref[...])

    shape = jax.ShapeDtypeStruct(w.shape, F32)
    return pl.pallas_call(
        body,
        name=name,
        out_shape=[shape] * 4,
        compiler_params=_params(),
    )(own, others, w, m, v)


SMALL_ROWS = 96


def _small_adamw(parts, params):
    def body(parts_ref, *rest):
        prm, outs, total = rest[:12], rest[12:29], rest[29]
        me = 4 * lax.axis_index("x") + 2 * lax.axis_index("y") + lax.axis_index("c")
        acc = parts_ref[0]
        for d in range(1, N_DEV):
            acc = acc + parts_ref[d]
        total[...] = acc
        grads = (total[0:8, :], total[8:16, :], total[16:17, 0:8],
                 total[pl.ds(pl.multiple_of(32 + me * 8, 8), CONV_K), 0:64])
        outs[0][...] = total[24:25, 0:1]
        for k, g in enumerate(grads):
            w_ref, m_ref, v_ref = prm[3 * k:3 * k + 3]
            g_ref, d_ref, nm_ref, nv_ref = outs[1 + 4 * k:5 + 4 * k]
            g_ref[...] = g
            d_ref[...], nm_ref[...], nv_ref[...] = _adam_update(w_ref[...], g, m_ref[...], v_ref[...])

    flat = [a for p in params for a in p]
    out_shape = [jax.ShapeDtypeStruct((1, 1), F32)]
    for p in params:
        out_shape += [jax.ShapeDtypeStruct(p[0].shape, F32)] * 4
    return pl.pallas_call(
        body,
        name="adamw_small",
        out_shape=out_shape,
        scratch_shapes=[pltpu.VMEM((SMALL_ROWS, 128), F32)],
        compiler_params=_params(),
    )(parts, *flat)


def kernel(x, norm_g, w_in, sinks, conv_w, w_out, final_g, loss_target, m_norm_g, m_w_in, m_sinks, m_conv_w, m_w_out, m_final_g, v_norm_g, v_w_in, v_sinks, v_conv_w, v_w_out, v_final_g):
    S = x.shape[1]
    x2 = x.reshape(S, D_MODEL)
    t2 = loss_target.reshape(S, D_MODEL)
    ng = norm_g.reshape(1, D_MODEL)
    fg = final_g.reshape(1, D_MODEL)

    cw_pad = jnp.zeros((8, 128), F32).at[0:CONV_K, 0:64].set(conv_w)
    xn, tab, wt = _prologue(x2, ng, w_in.T.astype(BF16))
    pa, pc, (wo, cw_all) = _fwd_proj(xn, wt, [w_out.astype(BF16), cw_pad])
    cw = cw_all.reshape(N_DEV, 8, 128)[:, 0:CONV_K, 0:64].transpose(1, 0, 2).reshape(CONV_K, CONV_W)
    ya, att, probs, psinks = _attn_fwd(pa, tab, sinks)
    dh, dmix, g_wo, g_fg, loss_part = _out_loss(x2, t2, ya, pc, cw, wo, fg)
    da, g_sinks = _attn_bwd(pa, dmix, att, probs, psinks, tab, sinks)
    g_wt, dc, g_cw = _grad_w_in(da, pc, dmix, cw, xn)
    cw_pack = jnp.pad(g_cw.reshape(CONV_K, N_DEV, 64).transpose(1, 0, 2),
                      ((0, 0), (0, 8 - CONV_K), (0, 64))).reshape(N_DEV * 8, 128)
    small = jnp.concatenate([g_fg.reshape(8, 128), g_sinks, loss_part, cw_pack], axis=0)
    grad_x, parts, own, others = _grad_x(
        da, dc, wt, x2, dh, ng, small,
        [g_wt.reshape(N_DEV, SHARD_IN, D_MODEL), g_wo.reshape(N_DEV, SHARD_OUT, D_MODEL)])
    gt, dt, nmt, nvt = _sum_chips_adamw(own[0], others[0], w_in.T, m_w_in.T, v_w_in.T, "adamw_w_in")
    grad_w_in, d_w_in, nm_w_in, nv_w_in = gt.T, dt.T, nmt.T, nvt.T
    grad_w_out, d_w_out, nm_w_out, nv_w_out = _sum_chips_adamw(
        own[1], others[1], w_out, m_w_out, v_w_out, "adamw_w_out")
    vec = lambda a: a.reshape(8, 128)
    row = lambda a: a.reshape(1, 8)
    res = _small_adamw(parts, [
        (vec(norm_g), vec(m_norm_g), vec(v_norm_g)), (vec(final_g), vec(m_final_g), vec(v_final_g)),
        (row(sinks), row(m_sinks), row(v_sinks)), (conv_w, m_conv_w, v_conv_w)])
    loss = res[0].reshape(())
    grad_norm_g, d_ng, nm_ng, nv_ng = [a.reshape(D_MODEL) for a in res[1:5]]
    grad_final_g, d_fg, nm_fg, nv_fg = [a.reshape(D_MODEL) for a in res[5:9]]
    grad_sinks, d_sk, nm_sk, nv_sk = [a.reshape(N_Q_HEADS) for a in res[9:13]]
    grad_conv_w, d_cw, nm_cw, nv_cw = res[13:17]

    return (loss, grad_x.reshape(1, S, D_MODEL), grad_norm_g, grad_w_in, grad_sinks, grad_conv_w, grad_w_out, grad_final_g,
            d_ng, d_w_in, d_sk, d_cw, d_w_out, d_fg,
            nm_ng, nm_w_in, nm_sk, nm_cw, nm_w_out, nm_fg,
            nv_ng, nv_w_in, nv_sk, nv_cw, nv_w_out, nv_fg)
```

```python
import numpy as np
import jax
import jax.numpy as jnp
from jax import lax
from jax.experimental import pallas as pl
from jax.experimental.pallas import tpu as pltpu

F32 = jnp.float32
BF16 = jnp.bfloat16
MESH = pl.DeviceIdType.MESH

D_MODEL = 1024
HEAD_DIM = 64
N_Q_HEADS = 8
GROUP = 4
ATTN_W = 512
KV_W = 128
BLK = 128
CONV_W = 512
CONV_K = 3
IN_W = 3328
PA_W = 1280
PC_W = 2048
EPS = 1e-5
ROPE_THETA = 500000.0
ROT_DIM = 16
N_DEV = 8
N_CHIP = 4
SHARD_IN = IN_W // N_DEV
SHARD_OUT = D_MODEL // N_DEV

ADAM_LR = 0.001
ADAM_B1 = 0.9
ADAM_B2 = 0.999
ADAM_EPS = 1e-08
ADAM_WD = 0.01
ADAM_STEP = 10

ACT = jnp.bfloat16

TM = 512
TQ = 1024
HALO = 16
VMEM_LIMIT = 56 * 1024 * 1024

NT_DIMS = (((1,), (1,)), ((), ()))
TN_DIMS = (((0,), (0,)), ((), ()))


def _params(sem=None):
    kw = dict(vmem_limit_bytes=VMEM_LIMIT)
    if sem is not None:
        kw["dimension_semantics"] = sem
    return pltpu.CompilerParams(**kw)


def _nt(a, b):
    return lax.dot_general(a, b, NT_DIMS, preferred_element_type=F32)


def _tn(a, b):
    return lax.dot_general(a, b, TN_DIMS, preferred_element_type=F32)


def _nn(a, b):
    return jnp.dot(a, b, preferred_element_type=F32)


def _silu(g):
    return g * jax.nn.sigmoid(g)


def _silu_and_grad(g):
    s = jax.nn.sigmoid(g)
    return g * s, s * (1.0 + g * (1.0 - s))


class _AllGatherInSteps:
    def __init__(self, arrs, forward_step):
        self.blocks = [(a.shape, a.dtype) for a in arrs]
        self.n = len(arrs)
        self.forward_step = forward_step

    def out_shape(self):
        return [jax.ShapeDtypeStruct((N_DEV * s[0], s[1]), d) for s, d in self.blocks]

    def scratch_shapes(self):
        return [pltpu.SemaphoreType.DMA((7 * self.n,)), pltpu.SemaphoreType.DMA((7 * self.n,)),
                pltpu.SemaphoreType.DMA((self.n,))]

    def emit(self, step, n_steps, x_refs, out_refs, scratch):
        assert n_steps > self.forward_step + 1
        send_sems, recv_sems, local_sems = scratch
        x, y, c = lax.axis_index("x"), lax.axis_index("y"), lax.axis_index("c")
        me, sibling = (x, y, c), (x, y, 1 - c)
        chips = [(1 - x, y), (x, 1 - y), (1 - x, 1 - y)]

        def rows(a, px, py, pc):
            m = self.blocks[a][0][0]
            return out_refs[a].at[pl.ds((4 * px + 2 * py + pc) * m, m), :]

        def copy(a, k, block, to, src=None):
            return pltpu.make_async_remote_copy(
                src_ref=rows(a, *block) if src is None else src, dst_ref=rows(a, *block),
                send_sem=send_sems.at[a * 7 + k], recv_sem=recv_sems.at[a * 7 + k],
                device_id=to, device_id_type=MESH)

        def mine(a):
            return pltpu.make_async_copy(x_refs[a], rows(a, *me), local_sems.at[a])

        def first(a):
            return ([copy(a, 0, me, sibling, src=x_refs[a])]
                    + [copy(a, 1 + j, me, (*chip, c), src=x_refs[a]) for j, chip in enumerate(chips)])

        def passed(a):
            return [copy(a, 4 + j, (*chip, c), sibling) for j, chip in enumerate(chips)]

        @pl.when(step == 0)
        def _():
            for a in range(self.n):
                mine(a).start()
                for cp in first(a):
                    cp.start()

        @pl.when(step == self.forward_step)
        def _():
            for j, chip in enumerate(chips):
                for a in range(self.n):
                    copy(a, 1 + j, (*chip, c), me).wait_recv()
                    copy(a, 4 + j, (*chip, c), sibling).start()

        def finish():
            @pl.when(step == n_steps - 1)
            def _():
                for a in range(self.n):
                    copy(a, 0, sibling, me).wait_recv()
                    for j, chip in enumerate(chips):
                        copy(a, 4 + j, (*chip, 1 - c), me).wait_recv()
                    for cp in first(a) + passed(a):
                        cp.wait_send()
                    mine(a).wait()

        return finish


class _AllGatherViaNeighbours:
    def __init__(self, arr, first, second):
        (self.m, self.ncol), self.dtype = arr.shape, arr.dtype
        assert self.m % 32 == 0
        self.first, self.second = first, second

    def out_shape(self):
        return [jax.ShapeDtypeStruct((N_DEV * self.m, self.ncol), self.dtype)]

    def scratch_shapes(self):
        return [pltpu.SemaphoreType.DMA((9,)), pltpu.SemaphoreType.DMA((9,)), pltpu.SemaphoreType.DMA]

    def emit(self, step, n_steps, x_ref, out_ref, scratch):
        assert 0 < self.first < self.second < n_steps - 1
        send_sems, recv_sems, local_sem = scratch
        x, y, c = lax.axis_index("x"), lax.axis_index("y"), lax.axis_index("c")
        half = self.m // 2
        sibling, xn, yn = (x, y, 1 - c), (1 - x, y, c), (x, 1 - y, c)

        def rows(dev, part=None):
            px, py, pc = dev
            base = (4 * px + 2 * py + pc) * self.m
            if part is None:
                return out_ref.at[pl.ds(base, self.m), :]
            return out_ref.at[pl.ds(base + part * half, half), :]

        def copy(k, dev, to, part=None, src=None):
            return pltpu.make_async_remote_copy(
                src_ref=rows(dev, part) if src is None else src, dst_ref=rows(dev, part),
                send_sem=send_sems.at[k], recv_sem=recv_sems.at[k], device_id=to, device_id_type=MESH)

        me, dg = (x, y, c), (1 - x, 1 - y, c)
        mine = pltpu.make_async_copy(x_ref, rows(me), local_sem)
        sends = [
            copy(0, me, sibling, src=x_ref), copy(1, me, xn, src=x_ref), copy(2, me, yn, src=x_ref),
            copy(3, xn, yn, part=0), copy(4, yn, xn, part=1),
            copy(5, xn, sibling), copy(6, yn, sibling), copy(7, dg, sibling, part=0), copy(8, dg, sibling, part=1),
        ]
        other = lambda dev: (dev[0], dev[1], 1 - c)
        arrivals = [
            copy(0, other(me), sibling), copy(1, xn, xn), copy(2, yn, yn), copy(3, dg, yn, part=0),
            copy(4, dg, xn, part=1), copy(5, other(xn), sibling), copy(6, other(yn), sibling),
            copy(7, other(dg), sibling, part=0), copy(8, other(dg), sibling, part=1),
        ]

        @pl.when(step == 0)
        def _():
            mine.start()
            for k in (0, 1, 2):
                sends[k].start()

        @pl.when(step == self.first)
        def _():
            arrivals[1].wait_recv()
            sends[3].start()
            sends[5].start()
            arrivals[2].wait_recv()
            sends[4].start()
            sends[6].start()

        @pl.when(step == self.second)
        def _():
            arrivals[3].wait_recv()
            sends[7].start()
            arrivals[4].wait_recv()
            sends[8].start()

        def finish():
            @pl.when(step == n_steps - 1)
            def _():
                for k in (0, 5, 6, 7, 8):
                    arrivals[k].wait_recv()
                for cp in sends:
                    cp.wait_send()
                mine.wait()

        return finish


class _ReduceScatter:
    def __init__(self, grads):
        self.shapes = [g.shape[1:] for g in grads]
        self.n = len(grads)
        self.items = tuple((a, r) for r in (1, 2, 3, 0) for a in range(self.n))
        self.steps = len(self.items) + 2

    def out_shape(self):
        own = [jax.ShapeDtypeStruct(s, F32) for s in self.shapes]
        ici = [jax.ShapeDtypeStruct((N_CHIP - 1,) + s, BF16) for s in self.shapes]
        land = [jax.ShapeDtypeStruct((N_CHIP,) + s, F32) for s in self.shapes]
        return own + ici + land

    def scratch_shapes(self):
        n_items = len(self.items)
        return ([pltpu.VMEM((2,) + s, F32) for s in self.shapes]
                + [pltpu.VMEM((N_CHIP - 1,) + s, BF16) for s in self.shapes]
                + [pltpu.VMEM(s, F32) for s in self.shapes]
                + [pltpu.SemaphoreType.DMA((self.n * N_CHIP,))] * 2
                + [pltpu.SemaphoreType.DMA((2 * n_items,))]
                + [pltpu.SemaphoreType.DMA((self.n * (N_CHIP - 1),))] * 2
                + [pltpu.SemaphoreType.DMA((self.n,))])

    def emit(self, step, n_steps, g_refs, out_refs, scratch):
        assert n_steps > self.steps
        n = self.n
        own_refs, ici_refs, land_refs = out_refs[:n], out_refs[n:2 * n], out_refs[2 * n:]
        stage, pair_bf, pair_own = scratch[:n], scratch[n:2 * n], scratch[2 * n:3 * n]
        sib_send, sib_recv, load_sems, ici_send, ici_recv, own_sems = scratch[3 * n:]
        x, y, c = lax.axis_index("x"), lax.axis_index("y"), lax.axis_index("c")

        def chip_of(r):
            return (x ^ (r >> 1), y ^ (r & 1))

        def block_of(r, core):
            cx, cy = chip_of(r)
            return 4 * cx + 2 * cy + core

        def to_sibling(a, r):
            return pltpu.make_async_remote_copy(
                src_ref=g_refs[a].at[block_of(r, 1 - c)], dst_ref=land_refs[a].at[r],
                send_sem=sib_send.at[a * N_CHIP + r], recv_sem=sib_recv.at[a * N_CHIP + r],
                device_id=(x, y, 1 - c), device_id_type=MESH)

        def loads(k):
            a, r = self.items[k]
            return (pltpu.make_async_copy(g_refs[a].at[block_of(r, c)], stage[a].at[0], load_sems.at[2 * k]),
                    pltpu.make_async_copy(land_refs[a].at[r], stage[a].at[1], load_sems.at[2 * k + 1]))

        def to_owner(k):
            a, r = self.items[k]
            if r == 0:
                return pltpu.make_async_copy(pair_own[a], own_refs[a], own_sems.at[a])
            return pltpu.make_async_remote_copy(
                src_ref=pair_bf[a].at[r - 1], dst_ref=ici_refs[a].at[r - 1],
                send_sem=ici_send.at[a * (N_CHIP - 1) + r - 1], recv_sem=ici_recv.at[a * (N_CHIP - 1) + r - 1],
                device_id=(*chip_of(r), c), device_id_type=MESH)

        @pl.when(step == 0)
        def _():
            for a, r in self.items:
                to_sibling(a, r).start()

        for k, (a, r) in enumerate(self.items):
            @pl.when(step == 1 + k)
            def _(k=k, a=a, r=r):
                to_sibling(a, r).wait_recv()
                for cp in loads(k):
                    cp.start()

            @pl.when(step == 2 + k)
            def _(k=k, a=a, r=r):
                for cp in loads(k):
                    cp.wait()
                total = stage[a][0] + stage[a][1]
                if r == 0:
                    pair_own[a][...] = total
                else:
                    pair_bf[a][r - 1] = total.astype(BF16)
                to_owner(k).start()

        def finish():
            @pl.when(step == n_steps - 1)
            def _():
                for k, (a, r) in enumerate(self.items):
                    if r == 0:
                        to_owner(k).wait()
                    else:
                        to_owner(k).wait_send()
                        to_owner(k).wait_recv()
                for a, r in self.items:
                    to_sibling(a, r).wait_send()

        return finish


def _prologue(x, norm_g, w_shard):
    S = x.shape[0]
    n_steps = S // TM
    half = ROT_DIM // 2
    pos = jnp.arange(S, dtype=jnp.int32).astype(F32)
    inv_freq = ROPE_THETA ** (-jnp.arange(0, ROT_DIM, 2, dtype=F32) / ROT_DIM)
    ang = inv_freq[:, None] * pos[None, :]
    cs = jnp.concatenate([jnp.cos(ang), jnp.sin(ang)], axis=0)
    ag = _AllGatherViaNeighbours(w_shard, first=n_steps // 2 - 2, second=n_steps - 4)

    def body(x_ref, g_ref, cs_ref, w_ref, xn_ref, tab_ref, wt_ref, *ag_scratch):
        step = pl.program_id(0)
        finish = ag.emit(step, n_steps, w_ref, wt_ref, ag_scratch)
        xv = x_ref[...]
        r = lax.rsqrt(jnp.mean(xv * xv, axis=-1, keepdims=True) + EPS)
        xn_ref[...] = (xv * r * g_ref[...]).astype(BF16)

        xt = jnp.concatenate([cs_ref[...], jnp.zeros((128 - 2 * half, TM), F32)], axis=0).T
        lane = lax.broadcasted_iota(jnp.int32, (TM, 128), 1)
        rr = lane & (HEAD_DIM - 1)
        first = lane < HEAD_DIM

        def at(shift_first, shift_second):
            return jnp.where(first, pltpu.roll(xt, shift_first, 1) if shift_first else xt,
                             pltpu.roll(xt, shift_second, 1))

        cos_lo, cos_hi = at(0, HEAD_DIM), at(half, HEAD_DIM + half)
        sin_lo, sin_hi = at(128 - half, HEAD_DIM - half), at(0, HEAD_DIM)
        tab_ref[:, 0:128] = jnp.where(rr < half, cos_lo, jnp.where(rr < ROT_DIM, cos_hi, 1.0))
        tab_ref[:, 128:256] = jnp.where(rr < half, -sin_lo, 0.0)
        tab_ref[:, 256:384] = jnp.where((rr >= half) & (rr < ROT_DIM), sin_hi, 0.0)
        finish()

    any_spec = pl.BlockSpec(memory_space=pl.ANY)
    return pl.pallas_call(
        body,
        name="prologue_all_gather_w_in",
        grid=(n_steps,),
        in_specs=[
            pl.BlockSpec((TM, D_MODEL), lambda i: (i, 0)),
            pl.BlockSpec((1, D_MODEL), lambda i: (0, 0)),
            pl.BlockSpec((2 * half, TM), lambda i: (0, i)),
            any_spec,
        ],
        out_specs=[
            pl.BlockSpec((TM, D_MODEL), lambda i: (i, 0)),
            pl.BlockSpec((TM, 384), lambda i: (i, 0)),
            any_spec,
        ],
        out_shape=[
            jax.ShapeDtypeStruct((S, D_MODEL), BF16),
            jax.ShapeDtypeStruct((S, 384), F32),
        ] + ag.out_shape(),
        scratch_shapes=ag.scratch_shapes(),
        compiler_params=_params(("arbitrary",)),
    )(x, norm_g, cs, w_shard)


def _fwd_proj(xn, wt, later):
    S = xn.shape[0]

    n_steps = S // TM
    ag = _AllGatherInSteps(later, forward_step=6)

    def body(xn_ref, wt_ref, *rest):
        later_refs, rest = rest[:ag.n], rest[ag.n:]
        pa_ref, pc_ref = rest[:2]
        gathered, ag_scratch = rest[2:2 + ag.n], rest[2 + ag.n:]
        step = pl.program_id(0)
        finish = ag.emit(step, n_steps, later_refs, gathered, ag_scratch)
        xn = xn_ref[...]
        pa_ref[:, 0:512] = _nt(xn, wt_ref[0:512, :]).astype(ACT)
        pa_ref[:, 512:1024] = _nt(xn, wt_ref[768:1280, :]).astype(ACT)
        pa_ref[:, 1024:1280] = _nt(xn, wt_ref[512:768, :]).astype(ACT)
        pc_ref[...] = _nt(xn, wt_ref[1280:3328, :]).astype(ACT)
        finish()

    any_spec = pl.BlockSpec(memory_space=pl.ANY)
    outs = pl.pallas_call(
        body,
        name="fwd_proj_all_gather",
        grid=(n_steps,),
        in_specs=[
            pl.BlockSpec((TM, D_MODEL), lambda i: (i, 0)),
            pl.BlockSpec((IN_W, D_MODEL), lambda i: (0, 0)),
        ] + [any_spec] * ag.n,
        out_specs=[
            pl.BlockSpec((TM, PA_W), lambda i: (i, 0)),
            pl.BlockSpec((TM, PC_W), lambda i: (i, 0)),
        ] + [any_spec] * ag.n,
        out_shape=[
            jax.ShapeDtypeStruct((S, PA_W), ACT),
            jax.ShapeDtypeStruct((S, PC_W), ACT),
        ] + ag.out_shape(),
        scratch_shapes=ag.scratch_shapes(),
        compiler_params=_params(("arbitrary",)),
    )(xn, wt, *later)
    return outs[0], outs[1], outs[2:]


def _rope(t, tab):
    return (t * tab[:, 0:128] + pltpu.roll(t, 120, 1) * tab[:, 128:256]
            + pltpu.roll(t, 8, 1) * tab[:, 256:384])


def _rope_t(d, tab):
    return (d * tab[:, 0:128] + pltpu.roll(d * tab[:, 128:256], 8, 1)
            + pltpu.roll(d * tab[:, 256:384], 120, 1))


def _fill_kv(kall, kvc_ref, kvp_ref, tabc_ref, tabp_ref):
    for lo, kv_ref, tab_ref, n in ((0, kvp_ref, tabp_ref, BLK), (BLK, kvc_ref, tabc_ref, TQ)):
        k = _rope(kv_ref[:, 0:128].astype(F32), tab_ref[...])
        v = kv_ref[:, 128:256].astype(F32)
        kall[0, lo:lo + n, :] = k.astype(BF16)
        kall[1, lo:lo + n, :] = pltpu.roll(k, 64, 1).astype(BF16)
        kall[2, lo:lo + n, :] = v.astype(BF16)
        kall[3, lo:lo + n, :] = pltpu.roll(v, 64, 1).astype(BF16)


HEADS = (((0, 0), (1, 0), (2, 1), (3, 1)), ((0, 1), (1, 1), (2, 0), (3, 0)))


def _upper():
    kj = lax.broadcasted_iota(jnp.int32, (BLK, 4 * BLK), 0)
    qi = lax.broadcasted_iota(jnp.int32, (BLK, 4 * BLK), 1) & (BLK - 1)
    return kj > qi


def _merge(upper, both):
    return jnp.where(upper, both[0:BLK, :], both[BLK:2 * BLK, :])


def _split_store(ref, s, upper_b, vb):
    first = vb * upper_b
    ref[s, 0:BLK, :] = first
    ref[s, BLK:2 * BLK, :] = vb - first


def _sink_rows(sink_ref):
    return [jnp.concatenate([jnp.full((1, BLK), sink_ref[2 * p + e], F32) for p, e in HEADS[s]], axis=1)
            for s in range(2)]


def _stack_heads(ref, s, half, pairs):
    for a, (p, e) in enumerate(HEADS[s]):
        ref[s, a * BLK:(a + 1) * BLK, :] = jnp.where(half[e], pairs[p], 0.0).astype(BF16)


def _unstack_pair(half, outs, p):
    lo = 0 if p < 2 else 1
    rows = slice(p * BLK, (p + 1) * BLK)
    return jnp.where(half[0], outs[lo][rows, :], outs[1 - lo][rows, :])


def _softmax(sm, sinks):
    m = jnp.maximum(jnp.max(sm, axis=0, keepdims=True), sinks)
    p = jnp.exp(sm - m)
    es = jnp.exp(sinks - m)
    inv = 1.0 / (jnp.sum(p, axis=0, keepdims=True) + es)
    return p * inv, es * inv


def _scores(kk, q_stack, first):
    st = _nt(kk, q_stack)
    prev = st[0:BLK, :]
    if first is not None:
        prev = prev + jnp.where(first, -jnp.inf, 0.0)
    return prev, st[BLK:2 * BLK, :]


def _attn_specs(tile):
    nb = TQ // BLK
    prev = lambda i: jnp.maximum(tile(i) * nb - 1, 0)
    return [
        pl.BlockSpec(memory_space=pltpu.SMEM),
        pl.BlockSpec((TQ, ATTN_W), lambda i: (tile(i), 0)),
        pl.BlockSpec((TQ, ATTN_W), lambda i: (tile(i), 1)),
        pl.BlockSpec((TQ, 2 * KV_W), lambda i: (tile(i), 4)),
        pl.BlockSpec((BLK, 2 * KV_W), lambda i: (prev(i), 4)),
        pl.BlockSpec((TQ, 384), lambda i: (tile(i), 0)),
        pl.BlockSpec((BLK, 384), lambda i: (prev(i), 0)),
    ]


def _attn_fwd(pa, tab, sinks):
    S = pa.shape[0]
    nb = TQ // BLK

    def body(sink_ref, q_ref, g_ref, kvc_ref, kvp_ref, tabc_ref, tabp_ref, o_ref, att_ref, pm_ref, ps_ref,
             kall, q_sc, p_sc):
        i = pl.program_id(0)
        _fill_kv(kall, kvc_ref, kvp_ref, tabc_ref, tabp_ref)
        lane = lax.broadcasted_iota(jnp.int32, (BLK, 128), 1)
        half = [lane < HEAD_DIM, lane >= HEAD_DIM]
        upper = _upper()
        upper_b = upper.astype(BF16)
        sinks = _sink_rows(sink_ref)
        for j in range(nb):
            rq = slice(j * BLK, (j + 1) * BLK)
            rk = slice(j * BLK, (j + 2) * BLK)
            tab = tabc_ref[rq, :]
            qr = [_rope(q_ref[rq, p * 128:(p + 1) * 128].astype(F32), tab) * 0.125 for p in range(4)]
            outs = []
            for s in range(2):
                _stack_heads(q_sc, s, half, qr)
                prev, cur = _scores(kall[s, rk, :], q_sc[s], i == 0 if j == 0 else None)
                prob, psink = _softmax(jnp.where(upper, prev, cur), sinks[s])
                pb = prob.astype(BF16)
                pm_ref[(2 * j + s) * BLK:(2 * j + s + 1) * BLK, :] = pb
                ps_ref[2 * j + s:2 * j + s + 1, :] = psink
                _split_store(p_sc, s, upper_b, pb)
                outs.append(_tn(p_sc[s], kall[2 + s, rk, :]))
            for p in range(4):
                cols = slice(p * 128, (p + 1) * 128)
                att = _unstack_pair(half, outs, p)
                att_ref[rq, cols] = att.astype(BF16)
                o_ref[rq, cols] = (att * _silu(g_ref[rq, cols].astype(F32))).astype(BF16)

    return pl.pallas_call(
        body,
        name="attn_fwd",
        grid=(S // TQ,),
        in_specs=_attn_specs(lambda i: i),
        out_specs=[pl.BlockSpec((TQ, ATTN_W), lambda i: (i, 0))] * 2 + [
            pl.BlockSpec((2 * TQ, 4 * BLK), lambda i: (i, 0)),
            pl.BlockSpec((2 * nb, 4 * BLK), lambda i: (i, 0)),
        ],
        out_shape=[jax.ShapeDtypeStruct((S, ATTN_W), BF16)] * 2 + [
            jax.ShapeDtypeStruct((2 * S, 4 * BLK), BF16),
            jax.ShapeDtypeStruct((2 * S // BLK, 4 * BLK), F32),
        ],
        scratch_shapes=[
            pltpu.VMEM((4, BLK + TQ, 128), BF16),
            pltpu.VMEM((2, 4 * BLK, 128), BF16),
            pltpu.VMEM((2, 2 * BLK, 4 * BLK), BF16),
        ],
        compiler_params=_params(("arbitrary",)),
    )(sinks, pa, pa, pa, pa, tab, tab)


def _shift_down(u, halo_ref, has_prev):
    def halo_u(r):
        hu = halo_ref[r:r + 1, 512:1024].astype(F32) * halo_ref[r:r + 1, 1024:1536].astype(F32)
        return jnp.where(has_prev, hu, 0.0)

    row = lax.broadcasted_iota(jnp.int32, u.shape, 0)
    um1 = jnp.where(row == 0, halo_u(HALO - 1), pltpu.roll(u, 1, 0))
    um2 = jnp.where(row == 0, halo_u(HALO - 2), jnp.where(row == 1, halo_u(HALO - 1), pltpu.roll(u, 2, 0)))
    return um1, um2


def _conv_tile(pc_ref, halo_ref, w_ref, has_prev):
    b = pc_ref[:, 0:512].astype(F32)
    c = pc_ref[:, 512:1024].astype(F32)
    hh = pc_ref[:, 1024:1536].astype(F32)
    gc = pc_ref[:, 1536:2048].astype(F32)
    u = c * hh
    um1, um2 = _shift_down(u, halo_ref, has_prev)
    cv = w_ref[0:1, :] * um2 + w_ref[1:2, :] * um1 + w_ref[2:3, :] * u
    return b, c, hh, gc, u, um1, um2, cv


def _prev_rows(width, col=0):
    return pl.BlockSpec((HALO, width), lambda i: (jnp.maximum(i * (TM // HALO) - 1, 0), col))


def _out_loss(x, target, ya, pc, conv_w, w_out, final_g):
    S = x.shape[0]

    def body(x_ref, t_ref, ya_ref, pc_ref, halo_ref, cw_ref, wo_ref, fg_ref,
             dh_ref, dmix_ref, gwo_ref, gfg_ref, loss_ref):
        @pl.when(pl.program_id(0) == 0)
        def _():
            gwo_ref[...] = jnp.zeros_like(gwo_ref)
            gfg_ref[...] = jnp.zeros_like(gfg_ref)
            loss_ref[...] = jnp.zeros_like(loss_ref)

        b, _, _, gc, _, _, _, cv = _conv_tile(pc_ref, halo_ref, cw_ref, pl.program_id(0) > 0)
        yc = (b * cv * _silu(gc)).astype(BF16)
        mix = jnp.concatenate([ya_ref[...], yc], axis=1)
        wo = wo_ref[...]
        fg = fg_ref[...]
        h = x_ref[...] + _nn(mix, wo)
        r = lax.rsqrt(jnp.mean(h * h, axis=-1, keepdims=True) + EPS)
        n = h * r
        err = n * fg - t_ref[...]
        loss_ref[...] += jnp.broadcast_to(
            0.5 * jnp.sum(jnp.mean(err * err, axis=-1, keepdims=True), axis=0, keepdims=True), (8, 128))
        gfg_ref[...] += jnp.sum(err * n, axis=0, keepdims=True) * (1.0 / D_MODEL)
        dyg = err * (fg * (1.0 / D_MODEL))
        dh = r * (dyg - n * jnp.mean(dyg * n, axis=-1, keepdims=True))
        dh_ref[...] = dh
        dhb = dh.astype(BF16)
        dmix_ref[...] = _nt(dhb, wo).astype(ACT)
        gwo_ref[...] += _tn(mix, dhb)

    row = lambda i: (i, 0)
    fixed = lambda i: (0, 0)
    return pl.pallas_call(
        body,
        name="out_loss",
        grid=(S // TM,),
        in_specs=[
            pl.BlockSpec((TM, D_MODEL), row),
            pl.BlockSpec((TM, D_MODEL), row),
            pl.BlockSpec((TM, ATTN_W), row),
            pl.BlockSpec((TM, PC_W), row),
            _prev_rows(PC_W),
            pl.BlockSpec((CONV_K, CONV_W), fixed),
            pl.BlockSpec((D_MODEL, D_MODEL), fixed),
            pl.BlockSpec((1, D_MODEL), fixed),
        ],
        out_specs=[
            pl.BlockSpec((TM, D_MODEL), row),
            pl.BlockSpec((TM, D_MODEL), row),
            pl.BlockSpec((D_MODEL, D_MODEL), fixed),
            pl.BlockSpec((1, D_MODEL), fixed),
            pl.BlockSpec((8, 128), fixed),
        ],
        out_shape=[
            jax.ShapeDtypeStruct((S, D_MODEL), F32),
            jax.ShapeDtypeStruct((S, D_MODEL), ACT),
            jax.ShapeDtypeStruct((D_MODEL, D_MODEL), F32),
            jax.ShapeDtypeStruct((1, D_MODEL), F32),
            jax.ShapeDtypeStruct((8, 128), F32),
        ],
        compiler_params=_params(("arbitrary",)),
    )(x, target, ya, pc, pc, conv_w, w_out, final_g)


def _attn_bwd(pa, dmix, att, probs, psinks, tab, sinks):
    S = pa.shape[0]
    nt = S // TQ
    nb = TQ // BLK

    def body(sink_ref, q_ref, g_ref, kvc_ref, kvp_ref, tabc_ref, tabp_ref, dm_ref, att_ref, pm_ref, ps_ref,
             d_ref, dsink_ref, kall, dkv, carry, q_sc, do_sc, p_sc, ds_sc, dsink_acc):
        step = pl.program_id(0)

        @pl.when(step == 0)
        def _():
            carry[...] = jnp.zeros_like(carry)
            dsink_acc[...] = jnp.zeros_like(dsink_acc)

        _fill_kv(kall, kvc_ref, kvp_ref, tabc_ref, tabp_ref)
        dkv[0:TQ, :] = jnp.zeros((TQ, 2 * KV_W), F32)
        dkv[TQ:TQ + BLK, :] = carry[...]
        lane = lax.broadcasted_iota(jnp.int32, (BLK, 128), 1)
        half = [lane < HEAD_DIM, lane >= HEAD_DIM]
        upper = _upper()
        upper_b = upper.astype(BF16)
        for j in range(nb):
            rq = slice(j * BLK, (j + 1) * BLK)
            rk = slice(j * BLK, (j + 2) * BLK)
            tab = tabc_ref[rq, :]
            pair = [slice(p * 128, (p + 1) * 128) for p in range(4)]
            qr = [_rope(q_ref[rq, c].astype(F32), tab) * 0.125 for c in pair]
            g = [g_ref[rq, c].astype(F32) for c in pair]
            da = [dm_ref[rq, c].astype(F32) for c in pair]
            gate = [_silu_and_grad(g[p]) for p in range(4)]
            do = [da[p] * gate[p][0] for p in range(4)]
            dqs, dks, dvs = [], [], []
            for s in range(2):
                kk = kall[s, rk, :]
                vv = kall[2 + s, rk, :]
                _stack_heads(q_sc, s, half, qr)
                _stack_heads(do_sc, s, half, do)
                pb = pm_ref[(2 * j + s) * BLK:(2 * j + s + 1) * BLK, :]
                prob = pb.astype(F32)
                _split_store(p_sc, s, upper_b, pb)
                dprob = _merge(upper, _nt(vv, do_sc[s]))
                dsum = jnp.sum(dprob * prob, axis=0, keepdims=True)
                _split_store(ds_sc, s, upper_b, (prob * (dprob - dsum)).astype(BF16))
                dsink_acc[s, 0:1, :] += ps_ref[2 * j + s:2 * j + s + 1, :] * dsum
                dqs.append(_tn(ds_sc[s], kk))
                dks.append(_nn(ds_sc[s], q_sc[s]))
                dvs.append(_nn(p_sc[s], do_sc[s]))
            for p in range(4):
                d_ref[rq, pair[p]] = _rope_t(_unstack_pair(half, dqs, p) * 0.125, tab).astype(BF16)
                d_ref[rq, 512 + p * 128:512 + (p + 1) * 128] = (
                    da[p] * att_ref[rq, pair[p]].astype(F32) * gate[p][1]).astype(BF16)
            dkv[rk, 0:128] += dks[0] + pltpu.roll(dks[1], 64, 1)
            dkv[rk, 128:256] += dvs[0] + pltpu.roll(dvs[1], 64, 1)
        d_ref[:, 1024:1152] = _rope_t(dkv[BLK:BLK + TQ, 0:128], tabc_ref[...]).astype(BF16)
        d_ref[:, 1152:1280] = dkv[BLK:BLK + TQ, 128:256].astype(BF16)
        carry[...] = dkv[0:BLK, :]

        @pl.when(step == nt - 1)
        def _():
            lanes = lax.broadcasted_iota(jnp.int32, (8, 128), 1)
            out = jnp.zeros((8, 128), F32)
            for s in range(2):
                for a, (p, e) in enumerate(HEADS[s]):
                    tot = jnp.sum(dsink_acc[s, 0:1, a * BLK:(a + 1) * BLK], axis=1, keepdims=True)
                    out = jnp.where(lanes == 2 * p + e, -tot, out)
            dsink_ref[...] = out

    rev = lambda s: nt - 1 - s
    return pl.pallas_call(
        body,
        name="attn_bwd",
        grid=(nt,),
        in_specs=_attn_specs(rev) + [pl.BlockSpec((TQ, ATTN_W), lambda s: (nt - 1 - s, 0))] * 2 + [
            pl.BlockSpec((2 * TQ, 4 * BLK), lambda s: (nt - 1 - s, 0)),
            pl.BlockSpec((2 * nb, 4 * BLK), lambda s: (nt - 1 - s, 0)),
        ],
        out_specs=[
            pl.BlockSpec((TQ, PA_W), lambda s: (nt - 1 - s, 0)),
            pl.BlockSpec((8, 128), lambda s: (0, 0)),
        ],
        out_shape=[
            jax.ShapeDtypeStruct((S, PA_W), BF16),
            jax.ShapeDtypeStruct((8, 128), F32),
        ],
        scratch_shapes=[
            pltpu.VMEM((4, BLK + TQ, 128), BF16),
            pltpu.VMEM((BLK + TQ, 2 * KV_W), F32),
            pltpu.VMEM((BLK, 2 * KV_W), F32),
            pltpu.VMEM((2, 4 * BLK, 128), BF16),
            pltpu.VMEM((2, 4 * BLK, 128), BF16),
            pltpu.VMEM((2, 2 * BLK, 4 * BLK), BF16),
            pltpu.VMEM((2, 2 * BLK, 4 * BLK), BF16),
            pltpu.VMEM((2, 8, 4 * BLK), F32),
        ],
        compiler_params=_params(("arbitrary",)),
    )(sinks, pa, pa, pa, pa, tab, tab, dmix, att, probs, psinks)


def _conv_bwd_tile(pc_ref, prev_ref, next_ref, dm_ref, dmn_ref, w_ref, d_ref, gw_ref, has_prev, has_next,
                   on_piece):
    rows = pc_ref.shape[0]
    w0, w1, w2 = w_ref[0:1, :], w_ref[1:2, :], w_ref[2:3, :]
    b, c, hh, gc, u, um1, um2, cv = _conv_tile(pc_ref, prev_ref, w_ref, has_prev)
    sg, dsg = _silu_and_grad(gc)
    dy = dm_ref[...].astype(F32)
    dyb = dy * b
    dcv = dyb * sg

    def next_dcv(r):
        nd = (dmn_ref[r:r + 1, :].astype(F32) * next_ref[r:r + 1, 0:512].astype(F32)
              * _silu(next_ref[r:r + 1, 1536:2048].astype(F32)))
        return jnp.where(has_next, nd, 0.0)

    row = lax.broadcasted_iota(jnp.int32, (rows, CONV_W), 0)
    dp1 = jnp.where(row == rows - 1, next_dcv(0), pltpu.roll(dcv, rows - 1, 0))
    dp2 = jnp.where(row == rows - 1, next_dcv(1),
                    jnp.where(row == rows - 2, next_dcv(0), pltpu.roll(dcv, rows - 2, 0)))
    du = w2 * dcv + w1 * dp1 + w0 * dp2
    pieces = (lambda: dy * cv * sg, lambda: du * hh, lambda: du * c, lambda: dyb * cv * dsg)
    for k, piece in enumerate(pieces):
        d_ref[:, k * CONV_W:(k + 1) * CONV_W] = piece().astype(BF16)
        on_piece(k)
    gw_ref[0:1, :] += jnp.sum(dcv * um2, axis=0, keepdims=True)
    gw_ref[1:2, :] += jnp.sum(dcv * um1, axis=0, keepdims=True)
    gw_ref[2:3, :] += jnp.sum(dcv * u, axis=0, keepdims=True)


def _grad_x(da, dc, wt, x, dh, norm_g, small, grads):
    S = x.shape[0]
    n_steps = S // TM
    rs = _ReduceScatter(grads)
    n_rs_out = len(rs.out_shape())
    small_rows = 8 + small.shape[0]

    def body(da_ref, dc_ref, wt_ref, x_ref, dh_ref, g_ref, small_ref, *rest):
        grad_refs, rest = rest[:rs.n], rest[rs.n:]
        gx_ref, all_ref = rest[:2]
        rs_out, rest = rest[2:2 + n_rs_out], rest[2 + n_rs_out:]
        gng, wta, stage, small_send, small_recv, small_own = rest[:6]
        rs_scratch = rest[6:]
        step = pl.program_id(0)
        finish = rs.emit(step, n_steps, grad_refs, rs_out, rs_scratch)

        @pl.when(step == 0)
        def _():
            gng[...] = jnp.zeros_like(gng)
            wta[0:512, :] = wt_ref[0:512, :]
            wta[512:1024, :] = wt_ref[768:1280, :]
            wta[1024:1280, :] = wt_ref[512:768, :]

        dxn = _nn(da_ref[...], wta[...]) + _nn(dc_ref[...], wt_ref[1280:3328, :])
        xv = x_ref[...]
        r = lax.rsqrt(jnp.mean(xv * xv, axis=-1, keepdims=True) + EPS)
        n = xv * r
        gng[...] += jnp.sum(dxn * n, axis=0, keepdims=True)
        dxg = dxn * g_ref[...]
        gx_ref[...] = dh_ref[...] + r * (dxg - n * jnp.mean(dxg * n, axis=-1, keepdims=True))

        @pl.when(step == n_steps - 1)
        def _():
            x_, y_, c_ = lax.axis_index("x"), lax.axis_index("y"), lax.axis_index("c")
            me = 4 * x_ + 2 * y_ + c_
            for q in range(8):
                stage[q:q + 1, :] = gng[:, q * 128:(q + 1) * 128]
            stage[8:small_rows, :] = small_ref[...]
            own = pltpu.make_async_copy(stage, all_ref.at[me], small_own)
            own.start()
            sends = []
            for k in range(1, N_DEV):
                cp = pltpu.make_async_remote_copy(
                    src_ref=stage, dst_ref=all_ref.at[me],
                    send_sem=small_send.at[k - 1], recv_sem=small_recv.at[k - 1],
                    device_id=(x_ ^ (k >> 2), y_ ^ ((k >> 1) & 1), c_ ^ (k & 1)), device_id_type=MESH)
                cp.start()
                sends.append(cp)
            for cp in sends:
                cp.wait_send()
                cp.wait_recv()
            own.wait()

        finish()

    row = lambda i: (i, 0)
    fixed = lambda i: (0, 0)
    any_spec = pl.BlockSpec(memory_space=pl.ANY)
    outs = pl.pallas_call(
        body,
        name="grad_x_reduce_scatter",
        grid=(n_steps,),
        in_specs=[
            pl.BlockSpec((TM, PA_W), row),
            pl.BlockSpec((TM, PC_W), row),
            pl.BlockSpec((IN_W, D_MODEL), fixed),
            pl.BlockSpec((TM, D_MODEL), row),
            pl.BlockSpec((TM, D_MODEL), row),
            pl.BlockSpec((1, D_MODEL), fixed),
            pl.BlockSpec(small.shape, fixed),
        ] + [any_spec] * rs.n,
        out_specs=[pl.BlockSpec((TM, D_MODEL), row), any_spec] + [any_spec] * n_rs_out,
        out_shape=[jax.ShapeDtypeStruct((S, D_MODEL), F32),
                   jax.ShapeDtypeStruct((N_DEV, small_rows, 128), F32)] + rs.out_shape(),
        scratch_shapes=[
            pltpu.VMEM((1, D_MODEL), F32),
            pltpu.VMEM((PA_W, D_MODEL), BF16),
            pltpu.VMEM((small_rows, 128), F32),
            pltpu.SemaphoreType.DMA((N_DEV - 1,)),
            pltpu.SemaphoreType.DMA((N_DEV - 1,)),
            pltpu.SemaphoreType.DMA,
        ] + rs.scratch_shapes(),
        compiler_params=_params(("arbitrary",)),
    )(da, dc, wt, x, dh, norm_g, small, *grads)
    return outs[0], outs[1], outs[2:2 + rs.n], outs[2 + rs.n:2 + 2 * rs.n]


def _grad_w_in(da, pc, dmix, conv_w, xn):
    S = xn.shape[0]
    nt = S // TM
    t16 = TM // HALO

    def body(da_ref, pc_ref, prev_ref, next_ref, dm_ref, dmn_ref, cw_ref, xn_ref, gw_ref, dc_ref, gcw_ref):
        i = pl.program_id(0)

        @pl.when(i == 0)
        def _():
            gw_ref[...] = jnp.zeros_like(gw_ref)
            gcw_ref[...] = jnp.zeros_like(gcw_ref)

        xn = xn_ref[...]
        gw_ref[0:512, :] += _tn(da_ref[:, 0:512], xn)
        gw_ref[768:1280, :] += _tn(da_ref[:, 512:1024], xn)
        gw_ref[512:768, :] += _tn(da_ref[:, 1024:1280], xn)
        def piece_grad(k):
            rows = slice(PA_W + k * CONV_W, PA_W + (k + 1) * CONV_W)
            gw_ref[rows, :] += _tn(dc_ref[:, k * CONV_W:(k + 1) * CONV_W], xn)

        _conv_bwd_tile(pc_ref, prev_ref, next_ref, dm_ref, dmn_ref, cw_ref, dc_ref, gcw_ref, i > 0, i < nt - 1,
                       piece_grad)

    row = lambda i: (i, 0)
    fixed = lambda i: (0, 0)
    nxt = lambda i: jnp.minimum((i + 1) * t16, nt * t16 - 1)
    return pl.pallas_call(
        body,
        name="grad_w_in",
        grid=(nt,),
        in_specs=[
            pl.BlockSpec((TM, PA_W), row),
            pl.BlockSpec((TM, PC_W), row),
            _prev_rows(PC_W),
            pl.BlockSpec((HALO, PC_W), lambda i: (nxt(i), 0)),
            pl.BlockSpec((TM, CONV_W), lambda i: (i, 1)),
            pl.BlockSpec((HALO, CONV_W), lambda i: (nxt(i), 1)),
            pl.BlockSpec((CONV_K, CONV_W), fixed),
            pl.BlockSpec((TM, D_MODEL), row),
        ],
        out_specs=[
            pl.BlockSpec((IN_W, D_MODEL), fixed),
            pl.BlockSpec((TM, PC_W), row),
            pl.BlockSpec((CONV_K, CONV_W), fixed),
        ],
        out_shape=[
            jax.ShapeDtypeStruct((IN_W, D_MODEL), F32),
            jax.ShapeDtypeStruct((S, PC_W), BF16),
            jax.ShapeDtypeStruct((CONV_K, CONV_W), F32),
        ],
        compiler_params=_params(("arbitrary",)),
    )(da, pc, pc, pc, dmix, dmix, conv_w, xn)


def _adam_update(w, g, m, v):
    c1 = 1.0 - ADAM_B1 ** ADAM_STEP
    c2 = 1.0 - ADAM_B2 ** ADAM_STEP
    nm = ADAM_B1 * m + (1.0 - ADAM_B1) * g
    nv = ADAM_B2 * v + (1.0 - ADAM_B2) * (g * g)
    return -ADAM_LR * ((nm / c1) / (jnp.sqrt(nv / c2) + ADAM_EPS) + ADAM_WD * w), nm, nv


def _sum_chips_adamw(own, others, w, m, v, name):
    def body(own_ref, p_ref, w_ref, m_ref, v_ref, g_ref, d_ref, nm_ref, nv_ref):
        g = own_ref[...]
        for k in range(N_CHIP - 1):
            g = g + p_ref[k].astype(F32)
        g_ref[...] = g
        d_ref[...], nm_ref[...], nv_ref[...] = _adam_update(w_ref[...], g, m_ref[...], v_ref[...])

    shape = jax.ShapeDtypeStruct(w.shape, F32)
    return pl.pallas_call(
        body,
        name=name,
        out_shape=[shape] * 4,
        compiler_params=_params(),
    )(own, others, w, m, v)


SMALL_ROWS = 96


def _small_adamw(parts, params):
    def body(parts_ref, *rest):
        prm, outs, total = rest[:12], rest[12:29], rest[29]
        me = 4 * lax.axis_index("x") + 2 * lax.axis_index("y") + lax.axis_index("c")
        acc = parts_ref[0]
        for d in range(1, N_DEV):
            acc = acc + parts_ref[d]
        total[...] = acc
        grads = (total[0:8, :], total[8:16, :], total[16:17, 0:8],
                 total[pl.ds(pl.multiple_of(32 + me * 8, 8), CONV_K), 0:64])
        outs[0][...] = total[24:25, 0:1]
        for k, g in enumerate(grads):
            w_ref, m_ref, v_ref = prm[3 * k:3 * k + 3]
            g_ref, d_ref, nm_ref, nv_ref = outs[1 + 4 * k:5 + 4 * k]
            g_ref[...] = g
            d_ref[...], nm_ref[...], nv_ref[...] = _adam_update(w_ref[...], g, m_ref[...], v_ref[...])

    flat = [a for p in params for a in p]
    out_shape = [jax.ShapeDtypeStruct((1, 1), F32)]
    for p in params:
        out_shape += [jax.ShapeDtypeStruct(p[0].shape, F32)] * 4
    return pl.pallas_call(
        body,
        name="adamw_small",
        out_shape=out_shape,
        scratch_shapes=[pltpu.VMEM((SMALL_ROWS, 128), F32)],
        compiler_params=_params(),
    )(parts, *flat)


def kernel(x, norm_g, w_in, sinks, conv_w, w_out, final_g, loss_target, m_norm_g, m_w_in, m_sinks, m_conv_w, m_w_out, m_final_g, v_norm_g, v_w_in, v_sinks, v_conv_w, v_w_out, v_final_g):
    S = x.shape[1]
    x2 = x.reshape(S, D_MODEL)
    t2 = loss_target.reshape(S, D_MODEL)
    ng = norm_g.reshape(1, D_MODEL)
    fg = final_g.reshape(1, D_MODEL)

    cw_pad = jnp.zeros((8, 128), F32).at[0:CONV_K, 0:64].set(conv_w)
    xn, tab, wt = _prologue(x2, ng, w_in.T.astype(BF16))
    pa, pc, (wo, cw_all) = _fwd_proj(xn, wt, [w_out.astype(BF16), cw_pad])
    cw = cw_all.reshape(N_DEV, 8, 128)[:, 0:CONV_K, 0:64].transpose(1, 0, 2).reshape(CONV_K, CONV_W)
    ya, att, probs, psinks = _attn_fwd(pa, tab, sinks)
    dh, dmix, g_wo, g_fg, loss_part = _out_loss(x2, t2, ya, pc, cw, wo, fg)
    da, g_sinks = _attn_bwd(pa, dmix, att, probs, psinks, tab, sinks)
    g_wt, dc, g_cw = _grad_w_in(da, pc, dmix, cw, xn)
    cw_pack = jnp.pad(g_cw.reshape(CONV_K, N_DEV, 64).transpose(1, 0, 2),
                      ((0, 0), (0, 8 - CONV_K), (0, 64))).reshape(N_DEV * 8, 128)
    small = jnp.concatenate([g_fg.reshape(8, 128), g_sinks, loss_part, cw_pack], axis=0)
    grad_x, parts, own, others = _grad_x(
        da, dc, wt, x2, dh, ng, small,
        [g_wt.reshape(N_DEV, SHARD_IN, D_MODEL), g_wo.reshape(N_DEV, SHARD_OUT, D_MODEL)])
    gt, dt, nmt, nvt = _sum_chips_adamw(own[0], others[0], w_in.T, m_w_in.T, v_w_in.T, "adamw_w_in")
    grad_w_in, d_w_in, nm_w_in, nv_w_in = gt.T, dt.T, nmt.T, nvt.T
    grad_w_out, d_w_out, nm_w_out, nv_w_out = _sum_chips_adamw(
        own[1], others[1], w_out, m_w_out, v_w_out, "adamw_w_out")
    vec = lambda a: a.reshape(8, 128)
    row = lambda a: a.reshape(1, 8)
    res = _small_adamw(parts, [
        (vec(norm_g), vec(m_norm_g), vec(v_norm_g)), (vec(final_g), vec(m_final_g), vec(v_final_g)),
        (row(sinks), row(m_sinks), row(v_sinks)), (conv_w, m_conv_w, v_conv_w)])
    loss = res[0].reshape(())
    grad_norm_g, d_ng, nm_ng, nv_ng = [a.reshape(D_MODEL) for a in res[1:5]]
    grad_final_g, d_fg, nm_fg, nv_fg = [a.reshape(D_MODEL) for a in res[5:9]]
    grad_sinks, d_sk, nm_sk, nv_sk = [a.reshape(N_Q_HEADS) for a in res[9:13]]
    grad_conv_w, d_cw, nm_cw, nv_cw = res[13:17]

    return (loss, grad_x.reshape(1, S, D_MODEL), grad_norm_g, grad_w_in, grad_sinks, grad_conv_w, grad_w_out, grad_final_g,
            d_ng, d_w_in, d_sk, d_cw, d_w_out, d_fg,
            nm_ng, nm_w_in, nm_sk, nm_cw, nm_w_out, nm_fg,
            nv_ng, nv_w_in, nv_sk, nv_cw, nv_w_out, nv_fg)
```

```python
import numpy as np
import jax
import jax.numpy as jnp
from jax import lax
from jax.experimental import pallas as pl
from jax.experimental.pallas import tpu as pltpu

F32 = jnp.float32
BF16 = jnp.bfloat16
MESH = pl.DeviceIdType.MESH

D_MODEL = 1024
HEAD_DIM = 64
N_Q_HEADS = 8
GROUP = 4
ATTN_W = 512
KV_W = 128
BLK = 128
CONV_W = 512
CONV_K = 3
IN_W = 3328
PA_W = 1280
PC_W = 2048
EPS = 1e-5
ROPE_THETA = 500000.0
ROT_DIM = 16
N_DEV = 8
N_CHIP = 4
SHARD_IN = IN_W // N_DEV
SHARD_OUT = D_MODEL // N_DEV

ADAM_LR = 0.001
ADAM_B1 = 0.9
ADAM_B2 = 0.999
ADAM_EPS = 1e-08
ADAM_WD = 0.01
ADAM_STEP = 10

ACT = jnp.bfloat16

TM = 512
TQ = 1024
HALO = 16
VMEM_LIMIT = 56 * 1024 * 1024

NT_DIMS = (((1,), (1,)), ((), ()))
TN_DIMS = (((0,), (0,)), ((), ()))


def _params(sem=None):
    kw = dict(vmem_limit_bytes=VMEM_LIMIT)
    if sem is not None:
        kw["dimension_semantics"] = sem
    return pltpu.CompilerParams(**kw)


def _nt(a, b):
    return lax.dot_general(a, b, NT_DIMS, preferred_element_type=F32)


def _tn(a, b):
    return lax.dot_general(a, b, TN_DIMS, preferred_element_type=F32)


def _nn(a, b):
    return jnp.dot(a, b, preferred_element_type=F32)


def _silu(g):
    return g * jax.nn.sigmoid(g)


def _silu_and_grad(g):
    s = jax.nn.sigmoid(g)
    return g * s, s * (1.0 + g * (1.0 - s))


class _AllGatherInSteps:
    def __init__(self, arrs, forward_step):
        self.blocks = [(a.shape, a.dtype) for a in arrs]
        self.n = len(arrs)
        self.forward_step = forward_step

    def out_shape(self):
        return [jax.ShapeDtypeStruct((N_DEV * s[0], s[1]), d) for s, d in self.blocks]

    def scratch_shapes(self):
        return [pltpu.SemaphoreType.DMA((7 * self.n,)), pltpu.SemaphoreType.DMA((7 * self.n,)),
                pltpu.SemaphoreType.DMA((self.n,))]

    def emit(self, step, n_steps, x_refs, out_refs, scratch):
        assert n_steps > self.forward_step + 1
        send_sems, recv_sems, local_sems = scratch
        x, y, c = lax.axis_index("x"), lax.axis_index("y"), lax.axis_index("c")
        me, sibling = (x, y, c), (x, y, 1 - c)
        chips = [(1 - x, y), (x, 1 - y), (1 - x, 1 - y)]

        def rows(a, px, py, pc):
            m = self.blocks[a][0][0]
            return out_refs[a].at[pl.ds((4 * px + 2 * py + pc) * m, m), :]

        def copy(a, k, block, to, src=None):
            return pltpu.make_async_remote_copy(
                src_ref=rows(a, *block) if src is None else src, dst_ref=rows(a, *block),
                send_sem=send_sems.at[a * 7 + k], recv_sem=recv_sems.at[a * 7 + k],
                device_id=to, device_id_type=MESH)

        def mine(a):
            return pltpu.make_async_copy(x_refs[a], rows(a, *me), local_sems.at[a])

        def first(a):
            return ([copy(a, 0, me, sibling, src=x_refs[a])]
                    + [copy(a, 1 + j, me, (*chip, c), src=x_refs[a]) for j, chip in enumerate(chips)])

        def passed(a):
            return [copy(a, 4 + j, (*chip, c), sibling) for j, chip in enumerate(chips)]

        @pl.when(step == 0)
        def _():
            for a in range(self.n):
                mine(a).start()
                for cp in first(a):
                    cp.start()

        @pl.when(step == self.forward_step)
        def _():
            for j, chip in enumerate(chips):
                for a in range(self.n):
                    copy(a, 1 + j, (*chip, c), me).wait_recv()
                    copy(a, 4 + j, (*chip, c), sibling).start()

        def finish():
            @pl.when(step == n_steps - 1)
            def _():
                for a in range(self.n):
                    copy(a, 0, sibling, me).wait_recv()
                    for j, chip in enumerate(chips):
                        copy(a, 4 + j, (*chip, 1 - c), me).wait_recv()
                    for cp in first(a) + passed(a):
                        cp.wait_send()
                    mine(a).wait()

        return finish


class _AllGatherViaNeighbours:
    def __init__(self, arr, first, second):
        (self.m, self.ncol), self.dtype = arr.shape, arr.dtype
        assert self.m % 32 == 0
        self.first, self.second = first, second

    def out_shape(self):
        return [jax.ShapeDtypeStruct((N_DEV * self.m, self.ncol), self.dtype)]

    def scratch_shapes(self):
        return [pltpu.SemaphoreType.DMA((9,)), pltpu.SemaphoreType.DMA((9,)), pltpu.SemaphoreType.DMA]

    def emit(self, step, n_steps, x_ref, out_ref, scratch):
        assert 0 < self.first < self.second < n_steps - 1
        send_sems, recv_sems, local_sem = scratch
        x, y, c = lax.axis_index("x"), lax.axis_index("y"), lax.axis_index("c")
        half = self.m // 2
        sibling, xn, yn = (x, y, 1 - c), (1 - x, y, c), (x, 1 - y, c)

        def rows(dev, part=None):
            px, py, pc = dev
            base = (4 * px + 2 * py + pc) * self.m
            if part is None:
                return out_ref.at[pl.ds(base, self.m), :]
            return out_ref.at[pl.ds(base + part * half, half), :]

        def copy(k, dev, to, part=None, src=None):
            return pltpu.make_async_remote_copy(
                src_ref=rows(dev, part) if src is None else src, dst_ref=rows(dev, part),
                send_sem=send_sems.at[k], recv_sem=recv_sems.at[k], device_id=to, device_id_type=MESH)

        me, dg = (x, y, c), (1 - x, 1 - y, c)
        mine = pltpu.make_async_copy(x_ref, rows(me), local_sem)
        sends = [
            copy(0, me, sibling, src=x_ref), copy(1, me, xn, src=x_ref), copy(2, me, yn, src=x_ref),
            copy(3, xn, yn, part=0), copy(4, yn, xn, part=1),
            copy(5, xn, sibling), copy(6, yn, sibling), copy(7, dg, sibling, part=0), copy(8, dg, sibling, part=1),
        ]
        other = lambda dev: (dev[0], dev[1], 1 - c)
        arrivals = [
            copy(0, other(me), sibling), copy(1, xn, xn), copy(2, yn, yn), copy(3, dg, yn, part=0),
            copy(4, dg, xn, part=1), copy(5, other(xn), sibling), copy(6, other(yn), sibling),
            copy(7, other(dg), sibling, part=0), copy(8, other(dg), sibling, part=1),
        ]

        @pl.when(step == 0)
        def _():
            mine.start()
            for k in (0, 1, 2):
                sends[k].start()

        @pl.when(step == self.first)
        def _():
            arrivals[1].wait_recv()
            sends[3].start()
            sends[5].start()
            arrivals[2].wait_recv()
            sends[4].start()
            sends[6].start()

        @pl.when(step == self.second)
        def _():
            arrivals[3].wait_recv()
            sends[7].start()
            arrivals[4].wait_recv()
            sends[8].start()

        def finish():
            @pl.when(step == n_steps - 1)
            def _():
                for k in (0, 5, 6, 7, 8):
                    arrivals[k].wait_recv()
                for cp in sends:
                    cp.wait_send()
                mine.wait()

        return finish


class _ReduceScatter:
    def __init__(self, grads):
        self.shapes = [g.shape[1:] for g in grads]
        self.n = len(grads)
        self.items = tuple((a, r) for r in (1, 2, 3, 0) for a in range(self.n))
        self.steps = len(self.items) + 2

    def out_shape(self):
        own = [jax.ShapeDtypeStruct(s, F32) for s in self.shapes]
        ici = [jax.ShapeDtypeStruct((N_CHIP - 1,) + s, BF16) for s in self.shapes]
        land = [jax.ShapeDtypeStruct((N_CHIP,) + s, F32) for s in self.shapes]
        return own + ici + land

    def scratch_shapes(self):
        n_items = len(self.items)
        return ([pltpu.VMEM((2,) + s, F32) for s in self.shapes]
                + [pltpu.VMEM((N_CHIP - 1,) + s, BF16) for s in self.shapes]
                + [pltpu.VMEM(s, F32) for s in self.shapes]
                + [pltpu.SemaphoreType.DMA((self.n * N_CHIP,))] * 2
                + [pltpu.SemaphoreType.DMA((2 * n_items,))]
                + [pltpu.SemaphoreType.DMA((self.n * (N_CHIP - 1),))] * 2
                + [pltpu.SemaphoreType.DMA((self.n,))])

    def emit(self, step, n_steps, g_refs, out_refs, scratch):
        assert n_steps > self.steps
        n = self.n
        own_refs, ici_refs, land_refs = out_refs[:n], out_refs[n:2 * n], out_refs[2 * n:]
        stage, pair_bf, pair_own = scratch[:n], scratch[n:2 * n], scratch[2 * n:3 * n]
        sib_send, sib_recv, load_sems, ici_send, ici_recv, own_sems = scratch[3 * n:]
        x, y, c = lax.axis_index("x"), lax.axis_index("y"), lax.axis_index("c")

        def chip_of(r):
            return (x ^ (r >> 1), y ^ (r & 1))

        def block_of(r, core):
            cx, cy = chip_of(r)
            return 4 * cx + 2 * cy + core

        def to_sibling(a, r):
            return pltpu.make_async_remote_copy(
                src_ref=g_refs[a].at[block_of(r, 1 - c)], dst_ref=land_refs[a].at[r],
                send_sem=sib_send.at[a * N_CHIP + r], recv_sem=sib_recv.at[a * N_CHIP + r],
                device_id=(x, y, 1 - c), device_id_type=MESH)

        def loads(k):
            a, r = self.items[k]
            return (pltpu.make_async_copy(g_refs[a].at[block_of(r, c)], stage[a].at[0], load_sems.at[2 * k]),
                    pltpu.make_async_copy(land_refs[a].at[r], stage[a].at[1], load_sems.at[2 * k + 1]))

        def to_owner(k):
            a, r = self.items[k]
            if r == 0:
                return pltpu.make_async_copy(pair_own[a], own_refs[a], own_sems.at[a])
            return pltpu.make_async_remote_copy(
                src_ref=pair_bf[a].at[r - 1], dst_ref=ici_refs[a].at[r - 1],
                send_sem=ici_send.at[a * (N_CHIP - 1) + r - 1], recv_sem=ici_recv.at[a * (N_CHIP - 1) + r - 1],
                device_id=(*chip_of(r), c), device_id_type=MESH)

        @pl.when(step == 0)
        def _():
            for a, r in self.items:
                to_sibling(a, r).start()

        for k, (a, r) in enumerate(self.items):
            @pl.when(step == 1 + k)
            def _(k=k, a=a, r=r):
                to_sibling(a, r).wait_recv()
                for cp in loads(k):
                    cp.start()

            @pl.when(step == 2 + k)
            def _(k=k, a=a, r=r):
                for cp in loads(k):
                    cp.wait()
                total = stage[a][0] + stage[a][1]
                if r == 0:
                    pair_own[a][...] = total
                else:
                    pair_bf[a][r - 1] = total.astype(BF16)
                to_owner(k).start()

        def finish():
            @pl.when(step == n_steps - 1)
            def _():
                for k, (a, r) in enumerate(self.items):
                    if r == 0:
                        to_owner(k).wait()
                    else:
                        to_owner(k).wait_send()
                        to_owner(k).wait_recv()
                for a, r in self.items:
                    to_sibling(a, r).wait_send()

        return finish


def _prologue(x, norm_g, w_shard):
    S = x.shape[0]
    n_steps = S // TM
    half = ROT_DIM // 2
    pos = jnp.arange(S, dtype=jnp.int32).astype(F32)
    inv_freq = ROPE_THETA ** (-jnp.arange(0, ROT_DIM, 2, dtype=F32) / ROT_DIM)
    ang = inv_freq[:, None] * pos[None, :]
    cs = jnp.concatenate([jnp.cos(ang), jnp.sin(ang)], axis=0)
    ag = _AllGatherViaNeighbours(w_shard, first=n_steps // 2 - 2, second=n_steps - 4)

    def body(x_ref, g_ref, cs_ref, w_ref, xn_ref, tab_ref, wt_ref, *ag_scratch):
        step = pl.program_id(0)
        finish = ag.emit(step, n_steps, w_ref, wt_ref, ag_scratch)
        xv = x_ref[...]
        r = lax.rsqrt(jnp.mean(xv * xv, axis=-1, keepdims=True) + EPS)
        xn_ref[...] = (xv * r * g_ref[...]).astype(BF16)

        xt = jnp.concatenate([cs_ref[...], jnp.zeros((128 - 2 * half, TM), F32)], axis=0).T
        lane = lax.broadcasted_iota(jnp.int32, (TM, 128), 1)
        rr = lane & (HEAD_DIM - 1)
        first = lane < HEAD_DIM

        def at(shift_first, shift_second):
            return jnp.where(first, pltpu.roll(xt, shift_first, 1) if shift_first else xt,
                             pltpu.roll(xt, shift_second, 1))

        cos_lo, cos_hi = at(0, HEAD_DIM), at(half, HEAD_DIM + half)
        sin_lo, sin_hi = at(128 - half, HEAD_DIM - half), at(0, HEAD_DIM)
        tab_ref[:, 0:128] = jnp.where(rr < half, cos_lo, jnp.where(rr < ROT_DIM, cos_hi, 1.0))
        tab_ref[:, 128:256] = jnp.where(rr < half, -sin_lo, 0.0)
        tab_ref[:, 256:384] = jnp.where((rr >= half) & (rr < ROT_DIM), sin_hi, 0.0)
        finish()

    any_spec = pl.BlockSpec(memory_space=pl.ANY)
    return pl.pallas_call(
        body,
        name="prologue_all_gather_w_in",
        grid=(n_steps,),
        in_specs=[
            pl.BlockSpec((TM, D_MODEL), lambda i: (i, 0)),
            pl.BlockSpec((1, D_MODEL), lambda i: (0, 0)),
            pl.BlockSpec((2 * half, TM), lambda i: (0, i)),
            any_spec,
        ],
        out_specs=[
            pl.BlockSpec((TM, D_MODEL), lambda i: (i, 0)),
            pl.BlockSpec((TM, 384), lambda i: (i, 0)),
            any_spec,
        ],
        out_shape=[
            jax.ShapeDtypeStruct((S, D_MODEL), BF16),
            jax.ShapeDtypeStruct((S, 384), F32),
        ] + ag.out_shape(),
        scratch_shapes=ag.scratch_shapes(),
        compiler_params=_params(("arbitrary",)),
    )(x, norm_g, cs, w_shard)


def _fwd_proj(xn, wt, later):
    S = xn.shape[0]

    n_steps = S // TM
    ag = _AllGatherInSteps(later, forward_step=6)

    def body(xn_ref, wt_ref, *rest):
        later_refs, rest = rest[:ag.n], rest[ag.n:]
        pa_ref, pc_ref = rest[:2]
        gathered, ag_scratch = rest[2:2 + ag.n], rest[2 + ag.n:]
        step = pl.program_id(0)
        finish = ag.emit(step, n_steps, later_refs, gathered, ag_scratch)
        xn = xn_ref[...]
        pa_ref[:, 0:512] = _nt(xn, wt_ref[0:512, :]).astype(ACT)
        pa_ref[:, 512:1024] = _nt(xn, wt_ref[768:1280, :]).astype(ACT)
        pa_ref[:, 1024:1280] = _nt(xn, wt_ref[512:768, :]).astype(ACT)
        pc_ref[...] = _nt(xn, wt_ref[1280:3328, :]).astype(ACT)
        finish()

    any_spec = pl.BlockSpec(memory_space=pl.ANY)
    outs = pl.pallas_call(
        body,
        name="fwd_proj_all_gather",
        grid=(n_steps,),
        in_specs=[
            pl.BlockSpec((TM, D_MODEL), lambda i: (i, 0)),
            pl.BlockSpec((IN_W, D_MODEL), lambda i: (0, 0)),
        ] + [any_spec] * ag.n,
        out_specs=[
            pl.BlockSpec((TM, PA_W), lambda i: (i, 0)),
            pl.BlockSpec((TM, PC_W), lambda i: (i, 0)),
        ] + [any_spec] * ag.n,
        out_shape=[
            jax.ShapeDtypeStruct((S, PA_W), ACT),
            jax.ShapeDtypeStruct((S, PC_W), ACT),
        ] + ag.out_shape(),
        scratch_shapes=ag.scratch_shapes(),
        compiler_params=_params(("arbitrary",)),
    )(xn, wt, *later)
    return outs[0], outs[1], outs[2:]


def _rope(t, tab):
    return (t * tab[:, 0:128] + pltpu.roll(t, 120, 1) * tab[:, 128:256]
            + pltpu.roll(t, 8, 1) * tab[:, 256:384])


def _rope_t(d, tab):
    return (d * tab[:, 0:128] + pltpu.roll(d * tab[:, 128:256], 8, 1)
            + pltpu.roll(d * tab[:, 256:384], 120, 1))


def _fill_kv(kall, kvc_ref, kvp_ref, tabc_ref, tabp_ref):
    for lo, kv_ref, tab_ref, n in ((0, kvp_ref, tabp_ref, BLK), (BLK, kvc_ref, tabc_ref, TQ)):
        k = _rope(kv_ref[:, 0:128].astype(F32), tab_ref[...])
        v = kv_ref[:, 128:256].astype(F32)
        kall[0, lo:lo + n, :] = k.astype(BF16)
        kall[1, lo:lo + n, :] = pltpu.roll(k, 64, 1).astype(BF16)
        kall[2, lo:lo + n, :] = v.astype(BF16)
        kall[3, lo:lo + n, :] = pltpu.roll(v, 64, 1).astype(BF16)


HEADS = (((0, 0), (1, 0), (2, 1), (3, 1)), ((0, 1), (1, 1), (2, 0), (3, 0)))


def _upper():
    kj = lax.broadcasted_iota(jnp.int32, (BLK, 4 * BLK), 0)
    qi = lax.broadcasted_iota(jnp.int32, (BLK, 4 * BLK), 1) & (BLK - 1)
    return kj > qi


def _merge(upper, both):
    return jnp.where(upper, both[0:BLK, :], both[BLK:2 * BLK, :])


def _split_store(ref, s, upper_b, vb):
    first = vb * upper_b
    ref[s, 0:BLK, :] = first
    ref[s, BLK:2 * BLK, :] = vb - first


def _sink_rows(sink_ref):
    return [jnp.concatenate([jnp.full((1, BLK), sink_ref[2 * p + e], F32) for p, e in HEADS[s]], axis=1)
            for s in range(2)]


def _stack_heads(ref, slot, half, pairs, s=None):
    for a, (p, e) in enumerate(HEADS[slot if s is None else s]):
        ref[slot, a * BLK:(a + 1) * BLK, :] = jnp.where(half[e], pairs[p], 0.0).astype(BF16)


def _unstack_pair(half, outs, p):
    lo = 0 if p < 2 else 1
    rows = slice(p * BLK, (p + 1) * BLK)
    return jnp.where(half[0], outs[lo][rows, :], outs[1 - lo][rows, :])


def _softmax(sm, sinks):
    m = jnp.maximum(jnp.max(sm, axis=0, keepdims=True), sinks)
    p = jnp.exp(sm - m)
    es = jnp.exp(sinks - m)
    inv = 1.0 / (jnp.sum(p, axis=0, keepdims=True) + es)
    return p * inv, es * inv


def _scores(kk, q_stack, first):
    st = _nt(kk, q_stack)
    prev = st[0:BLK, :]
    if first is not None:
        prev = prev + jnp.where(first, -jnp.inf, 0.0)
    return prev, st[BLK:2 * BLK, :]


def _attn_specs(tile):
    nb = TQ // BLK
    prev = lambda i: jnp.maximum(tile(i) * nb - 1, 0)
    return [
        pl.BlockSpec(memory_space=pltpu.SMEM),
        pl.BlockSpec((TQ, ATTN_W), lambda i: (tile(i), 0)),
        pl.BlockSpec((TQ, ATTN_W), lambda i: (tile(i), 1)),
        pl.BlockSpec((TQ, 2 * KV_W), lambda i: (tile(i), 4)),
        pl.BlockSpec((BLK, 2 * KV_W), lambda i: (prev(i), 4)),
        pl.BlockSpec((TQ, 384), lambda i: (tile(i), 0)),
        pl.BlockSpec((BLK, 384), lambda i: (prev(i), 0)),
    ]


def _attn_fwd(pa, tab, sinks):
    S = pa.shape[0]
    nb = TQ // BLK

    def body(sink_ref, q_ref, g_ref, kvc_ref, kvp_ref, tabc_ref, tabp_ref, o_ref, att_ref, pm_ref, ps_ref,
             qs_ref, kall, p_sc):
        i = pl.program_id(0)
        _fill_kv(kall, kvc_ref, kvp_ref, tabc_ref, tabp_ref)
        lane = lax.broadcasted_iota(jnp.int32, (BLK, 128), 1)
        half = [lane < HEAD_DIM, lane >= HEAD_DIM]
        upper = _upper()
        upper_b = upper.astype(BF16)
        sinks = _sink_rows(sink_ref)
        for j in range(nb):
            rq = slice(j * BLK, (j + 1) * BLK)
            rk = slice(j * BLK, (j + 2) * BLK)
            tab = tabc_ref[rq, :]
            qr = [_rope(q_ref[rq, p * 128:(p + 1) * 128].astype(F32), tab) * 0.125 for p in range(4)]
            outs = []
            for s in range(2):
                _stack_heads(qs_ref, 2 * j + s, half, qr, s)
                prev, cur = _scores(kall[s, rk, :], qs_ref[2 * j + s], i == 0 if j == 0 else None)
                prob, psink = _softmax(jnp.where(upper, prev, cur), sinks[s])
                pb = prob.astype(BF16)
                pm_ref[(2 * j + s) * BLK:(2 * j + s + 1) * BLK, :] = pb
                ps_ref[2 * j + s:2 * j + s + 1, :] = psink
                _split_store(p_sc, s, upper_b, pb)
                outs.append(_tn(p_sc[s], kall[2 + s, rk, :]))
            for p in range(4):
                cols = slice(p * 128, (p + 1) * 128)
                att = _unstack_pair(half, outs, p)
                att_ref[rq, cols] = att.astype(BF16)
                o_ref[rq, cols] = (att * _silu(g_ref[rq, cols].astype(F32))).astype(BF16)

    return pl.pallas_call(
        body,
        name="attn_fwd",
        grid=(S // TQ,),
        in_specs=_attn_specs(lambda i: i),
        out_specs=[pl.BlockSpec((TQ, ATTN_W), lambda i: (i, 0))] * 2 + [
            pl.BlockSpec((2 * TQ, 4 * BLK), lambda i: (i, 0)),
            pl.BlockSpec((2 * nb, 4 * BLK), lambda i: (i, 0)),
            pl.BlockSpec((2 * nb, 4 * BLK, 128), lambda i: (i, 0, 0)),
        ],
        out_shape=[jax.ShapeDtypeStruct((S, ATTN_W), BF16)] * 2 + [
            jax.ShapeDtypeStruct((2 * S, 4 * BLK), BF16),
            jax.ShapeDtypeStruct((2 * S // BLK, 4 * BLK), F32),
            jax.ShapeDtypeStruct((2 * S // BLK, 4 * BLK, 128), BF16),
        ],
        scratch_shapes=[
            pltpu.VMEM((4, BLK + TQ, 128), BF16),
            pltpu.VMEM((2, 2 * BLK, 4 * BLK), BF16),
        ],
        compiler_params=_params(("arbitrary",)),
    )(sinks, pa, pa, pa, pa, tab, tab)


def _shift_down(u, halo_ref, has_prev):
    def halo_u(r):
        hu = halo_ref[r:r + 1, 512:1024].astype(F32) * halo_ref[r:r + 1, 1024:1536].astype(F32)
        return jnp.where(has_prev, hu, 0.0)

    row = lax.broadcasted_iota(jnp.int32, u.shape, 0)
    um1 = jnp.where(row == 0, halo_u(HALO - 1), pltpu.roll(u, 1, 0))
    um2 = jnp.where(row == 0, halo_u(HALO - 2), jnp.where(row == 1, halo_u(HALO - 1), pltpu.roll(u, 2, 0)))
    return um1, um2


def _conv_tile(pc_ref, halo_ref, w_ref, has_prev):
    b = pc_ref[:, 0:512].astype(F32)
    c = pc_ref[:, 512:1024].astype(F32)
    hh = pc_ref[:, 1024:1536].astype(F32)
    gc = pc_ref[:, 1536:2048].astype(F32)
    u = c * hh
    um1, um2 = _shift_down(u, halo_ref, has_prev)
    cv = w_ref[0:1, :] * um2 + w_ref[1:2, :] * um1 + w_ref[2:3, :] * u
    return b, c, hh, gc, u, um1, um2, cv


def _prev_rows(width, col=0):
    return pl.BlockSpec((HALO, width), lambda i: (jnp.maximum(i * (TM // HALO) - 1, 0), col))


def _out_loss(x, target, ya, pc, conv_w, w_out, final_g):
    S = x.shape[0]

    def body(x_ref, t_ref, ya_ref, pc_ref, halo_ref, cw_ref, wo_ref, fg_ref,
             dh_ref, dmix_ref, gwo_ref, gfg_ref, loss_ref):
        @pl.when(pl.program_id(0) == 0)
        def _():
            gwo_ref[...] = jnp.zeros_like(gwo_ref)
            gfg_ref[...] = jnp.zeros_like(gfg_ref)
            loss_ref[...] = jnp.zeros_like(loss_ref)

        b, _, _, gc, _, _, _, cv = _conv_tile(pc_ref, halo_ref, cw_ref, pl.program_id(0) > 0)
        yc = (b * cv * _silu(gc)).astype(BF16)
        mix = jnp.concatenate([ya_ref[...], yc], axis=1)
        wo = wo_ref[...]
        fg = fg_ref[...]
        h = x_ref[...] + _nn(mix, wo)
        r = lax.rsqrt(jnp.mean(h * h, axis=-1, keepdims=True) + EPS)
        n = h * r
        err = n * fg - t_ref[...]
        loss_ref[...] += jnp.broadcast_to(
            0.5 * jnp.sum(jnp.mean(err * err, axis=-1, keepdims=True), axis=0, keepdims=True), (8, 128))
        gfg_ref[...] += jnp.sum(err * n, axis=0, keepdims=True) * (1.0 / D_MODEL)
        dyg = err * (fg * (1.0 / D_MODEL))
        dh = r * (dyg - n * jnp.mean(dyg * n, axis=-1, keepdims=True))
        dh_ref[...] = dh
        dhb = dh.astype(BF16)
        dmix_ref[...] = _nt(dhb, wo).astype(ACT)
        gwo_ref[...] += _tn(mix, dhb)

    row = lambda i: (i, 0)
    fixed = lambda i: (0, 0)
    return pl.pallas_call(
        body,
        name="out_loss",
        grid=(S // TM,),
        in_specs=[
            pl.BlockSpec((TM, D_MODEL), row),
            pl.BlockSpec((TM, D_MODEL), row),
            pl.BlockSpec((TM, ATTN_W), row),
            pl.BlockSpec((TM, PC_W), row),
            _prev_rows(PC_W),
            pl.BlockSpec((CONV_K, CONV_W), fixed),
            pl.BlockSpec((D_MODEL, D_MODEL), fixed),
            pl.BlockSpec((1, D_MODEL), fixed),
        ],
        out_specs=[
            pl.BlockSpec((TM, D_MODEL), row),
            pl.BlockSpec((TM, D_MODEL), row),
            pl.BlockSpec((D_MODEL, D_MODEL), fixed),
            pl.BlockSpec((1, D_MODEL), fixed),
            pl.BlockSpec((8, 128), fixed),
        ],
        out_shape=[
            jax.ShapeDtypeStruct((S, D_MODEL), F32),
            jax.ShapeDtypeStruct((S, D_MODEL), ACT),
            jax.ShapeDtypeStruct((D_MODEL, D_MODEL), F32),
            jax.ShapeDtypeStruct((1, D_MODEL), F32),
            jax.ShapeDtypeStruct((8, 128), F32),
        ],
        compiler_params=_params(("arbitrary",)),
    )(x, target, ya, pc, pc, conv_w, w_out, final_g)


def _attn_bwd(pa, dmix, att, probs, psinks, q_stack, tab):
    S = pa.shape[0]
    nt = S // TQ
    nb = TQ // BLK

    def body(g_ref, kvc_ref, kvp_ref, tabc_ref, tabp_ref, dm_ref, att_ref, pm_ref, ps_ref, qs_ref,
             d_ref, dsink_ref, kall, dkv, carry, do_sc, p_sc, ds_sc, dsink_acc):
        step = pl.program_id(0)

        @pl.when(step == 0)
        def _():
            carry[...] = jnp.zeros_like(carry)
            dsink_acc[...] = jnp.zeros_like(dsink_acc)

        _fill_kv(kall, kvc_ref, kvp_ref, tabc_ref, tabp_ref)
        dkv[0:TQ, :] = jnp.zeros((TQ, 2 * KV_W), F32)
        dkv[TQ:TQ + BLK, :] = carry[...]
        lane = lax.broadcasted_iota(jnp.int32, (BLK, 128), 1)
        half = [lane < HEAD_DIM, lane >= HEAD_DIM]
        upper = _upper()
        upper_b = upper.astype(BF16)
        for j in range(nb):
            rq = slice(j * BLK, (j + 1) * BLK)
            rk = slice(j * BLK, (j + 2) * BLK)
            tab = tabc_ref[rq, :]
            pair = [slice(p * 128, (p + 1) * 128) for p in range(4)]
            g = [g_ref[rq, c].astype(F32) for c in pair]
            da = [dm_ref[rq, c].astype(F32) for c in pair]
            gate = [_silu_and_grad(g[p]) for p in range(4)]
            do = [da[p] * gate[p][0] for p in range(4)]
            dqs, dks, dvs = [], [], []
            for s in range(2):
                kk = kall[s, rk, :]
                vv = kall[2 + s, rk, :]
                _stack_heads(do_sc, s, half, do)
                pb = pm_ref[(2 * j + s) * BLK:(2 * j + s + 1) * BLK, :]
                prob = pb.astype(F32)
                _split_store(p_sc, s, upper_b, pb)
                dprob = _merge(upper, _nt(vv, do_sc[s]))
                dsum = jnp.sum(dprob * prob, axis=0, keepdims=True)
                _split_store(ds_sc, s, upper_b, (prob * (dprob - dsum)).astype(BF16))
                dsink_acc[s, 0:1, :] += ps_ref[2 * j + s:2 * j + s + 1, :] * dsum
                dqs.append(_tn(ds_sc[s], kk))
                dks.append(_nn(ds_sc[s], qs_ref[2 * j + s]))
                dvs.append(_nn(p_sc[s], do_sc[s]))
            for p in range(4):
                d_ref[rq, pair[p]] = _rope_t(_unstack_pair(half, dqs, p) * 0.125, tab).astype(BF16)
                d_ref[rq, 512 + p * 128:512 + (p + 1) * 128] = (
                    da[p] * att_ref[rq, pair[p]].astype(F32) * gate[p][1]).astype(BF16)
            dkv[rk, 0:128] += dks[0] + pltpu.roll(dks[1], 64, 1)
            dkv[rk, 128:256] += dvs[0] + pltpu.roll(dvs[1], 64, 1)
        d_ref[:, 1024:1152] = _rope_t(dkv[BLK:BLK + TQ, 0:128], tabc_ref[...]).astype(BF16)
        d_ref[:, 1152:1280] = dkv[BLK:BLK + TQ, 128:256].astype(BF16)
        carry[...] = dkv[0:BLK, :]

        @pl.when(step == nt - 1)
        def _():
            lanes = lax.broadcasted_iota(jnp.int32, (8, 128), 1)
            out = jnp.zeros((8, 128), F32)
            for s in range(2):
                for a, (p, e) in enumerate(HEADS[s]):
                    tot = jnp.sum(dsink_acc[s, 0:1, a * BLK:(a + 1) * BLK], axis=1, keepdims=True)
                    out = jnp.where(lanes == 2 * p + e, -tot, out)
            dsink_ref[...] = out

    rev = lambda s: nt - 1 - s
    return pl.pallas_call(
        body,
        name="attn_bwd",
        grid=(nt,),
        in_specs=_attn_specs(rev)[2:] + [pl.BlockSpec((TQ, ATTN_W), lambda s: (nt - 1 - s, 0))] * 2 + [
            pl.BlockSpec((2 * TQ, 4 * BLK), lambda s: (nt - 1 - s, 0)),
            pl.BlockSpec((2 * nb, 4 * BLK), lambda s: (nt - 1 - s, 0)),
            pl.BlockSpec((2 * nb, 4 * BLK, 128), lambda s: (nt - 1 - s, 0, 0)),
        ],
        out_specs=[
            pl.BlockSpec((TQ, PA_W), lambda s: (nt - 1 - s, 0)),
            pl.BlockSpec((8, 128), lambda s: (0, 0)),
        ],
        out_shape=[
            jax.ShapeDtypeStruct((S, PA_W), BF16),
            jax.ShapeDtypeStruct((8, 128), F32),
        ],
        scratch_shapes=[
            pltpu.VMEM((4, BLK + TQ, 128), BF16),
            pltpu.VMEM((BLK + TQ, 2 * KV_W), F32),
            pltpu.VMEM((BLK, 2 * KV_W), F32),
            pltpu.VMEM((2, 4 * BLK, 128), BF16),
            pltpu.VMEM((2, 2 * BLK, 4 * BLK), BF16),
            pltpu.VMEM((2, 2 * BLK, 4 * BLK), BF16),
            pltpu.VMEM((2, 8, 4 * BLK), F32),
        ],
        compiler_params=_params(("arbitrary",)),
    )(pa, pa, pa, tab, tab, dmix, att, probs, psinks, q_stack)


def _conv_bwd_tile(pc_ref, prev_ref, next_ref, dm_ref, dmn_ref, w_ref, d_ref, gw_ref, has_prev, has_next,
                   on_piece):
    rows = pc_ref.shape[0]
    w0, w1, w2 = w_ref[0:1, :], w_ref[1:2, :], w_ref[2:3, :]
    b, c, hh, gc, u, um1, um2, cv = _conv_tile(pc_ref, prev_ref, w_ref, has_prev)
    sg, dsg = _silu_and_grad(gc)
    dy = dm_ref[...].astype(F32)
    dyb = dy * b
    dcv = dyb * sg

    def next_dcv(r):
        nd = (dmn_ref[r:r + 1, :].astype(F32) * next_ref[r:r + 1, 0:512].astype(F32)
              * _silu(next_ref[r:r + 1, 1536:2048].astype(F32)))
        return jnp.where(has_next, nd, 0.0)

    row = lax.broadcasted_iota(jnp.int32, (rows, CONV_W), 0)
    dp1 = jnp.where(row == rows - 1, next_dcv(0), pltpu.roll(dcv, rows - 1, 0))
    dp2 = jnp.where(row == rows - 1, next_dcv(1),
                    jnp.where(row == rows - 2, next_dcv(0), pltpu.roll(dcv, rows - 2, 0)))
    du = w2 * dcv + w1 * dp1 + w0 * dp2
    pieces = (lambda: dy * cv * sg, lambda: du * hh, lambda: du * c, lambda: dyb * cv * dsg)
    for k, piece in enumerate(pieces):
        d_ref[:, k * CONV_W:(k + 1) * CONV_W] = piece().astype(BF16)
        on_piece(k)
    gw_ref[0:1, :] += jnp.sum(dcv * um2, axis=0, keepdims=True)
    gw_ref[1:2, :] += jnp.sum(dcv * um1, axis=0, keepdims=True)
    gw_ref[2:3, :] += jnp.sum(dcv * u, axis=0, keepdims=True)


def _grad_x(da, dc, wt, x, dh, norm_g, small, grads):
    S = x.shape[0]
    n_steps = S // TM
    rs = _ReduceScatter(grads)
    n_rs_out = len(rs.out_shape())
    small_rows = 8 + small.shape[0]

    def body(da_ref, dc_ref, wt_ref, x_ref, dh_ref, g_ref, small_ref, *rest):
        grad_refs, rest = rest[:rs.n], rest[rs.n:]
        gx_ref, all_ref = rest[:2]
        rs_out, rest = rest[2:2 + n_rs_out], rest[2 + n_rs_out:]
        gng, wta, stage, small_send, small_recv, small_own = rest[:6]
        rs_scratch = rest[6:]
        step = pl.program_id(0)
        finish = rs.emit(step, n_steps, grad_refs, rs_out, rs_scratch)

        @pl.when(step == 0)
        def _():
            gng[...] = jnp.zeros_like(gng)
            wta[0:512, :] = wt_ref[0:512, :]
            wta[512:1024, :] = wt_ref[768:1280, :]
            wta[1024:1280, :] = wt_ref[512:768, :]

        dxn = _nn(da_ref[...], wta[...]) + _nn(dc_ref[...], wt_ref[1280:3328, :])
        xv = x_ref[...]
        r = lax.rsqrt(jnp.mean(xv * xv, axis=-1, keepdims=True) + EPS)
        n = xv * r
        gng[...] += jnp.sum(dxn * n, axis=0, keepdims=True)
        dxg = dxn * g_ref[...]
        gx_ref[...] = dh_ref[...] + r * (dxg - n * jnp.mean(dxg * n, axis=-1, keepdims=True))

        @pl.when(step == n_steps - 1)
        def _():
            x_, y_, c_ = lax.axis_index("x"), lax.axis_index("y"), lax.axis_index("c")
            me = 4 * x_ + 2 * y_ + c_
            for q in range(8):
                stage[q:q + 1, :] = gng[:, q * 128:(q + 1) * 128]
            stage[8:small_rows, :] = small_ref[...]
            own = pltpu.make_async_copy(stage, all_ref.at[me], small_own)
            own.start()
            sends = []
            for k in range(1, N_DEV):
                cp = pltpu.make_async_remote_copy(
                    src_ref=stage, dst_ref=all_ref.at[me],
                    send_sem=small_send.at[k - 1], recv_sem=small_recv.at[k - 1],
                    device_id=(x_ ^ (k >> 2), y_ ^ ((k >> 1) & 1), c_ ^ (k & 1)), device_id_type=MESH)
                cp.start()
                sends.append(cp)
            for cp in sends:
                cp.wait_send()
                cp.wait_recv()
            own.wait()

        finish()

    row = lambda i: (i, 0)
    fixed = lambda i: (0, 0)
    any_spec = pl.BlockSpec(memory_space=pl.ANY)
    outs = pl.pallas_call(
        body,
        name="grad_x_reduce_scatter",
        grid=(n_steps,),
        in_specs=[
            pl.BlockSpec((TM, PA_W), row),
            pl.BlockSpec((TM, PC_W), row),
            pl.BlockSpec((IN_W, D_MODEL), fixed),
            pl.BlockSpec((TM, D_MODEL), row),
            pl.BlockSpec((TM, D_MODEL), row),
            pl.BlockSpec((1, D_MODEL), fixed),
            pl.BlockSpec(small.shape, fixed),
        ] + [any_spec] * rs.n,
        out_specs=[pl.BlockSpec((TM, D_MODEL), row), any_spec] + [any_spec] * n_rs_out,
        out_shape=[jax.ShapeDtypeStruct((S, D_MODEL), F32),
                   jax.ShapeDtypeStruct((N_DEV, small_rows, 128), F32)] + rs.out_shape(),
        scratch_shapes=[
            pltpu.VMEM((1, D_MODEL), F32),
            pltpu.VMEM((PA_W, D_MODEL), BF16),
            pltpu.VMEM((small_rows, 128), F32),
            pltpu.SemaphoreType.DMA((N_DEV - 1,)),
            pltpu.SemaphoreType.DMA((N_DEV - 1,)),
            pltpu.SemaphoreType.DMA,
        ] + rs.scratch_shapes(),
        compiler_params=_params(("arbitrary",)),
    )(da, dc, wt, x, dh, norm_g, small, *grads)
    return outs[0], outs[1], outs[2:2 + rs.n], outs[2 + rs.n:2 + 2 * rs.n]


def _grad_w_in(da, pc, dmix, conv_w, xn):
    S = xn.shape[0]
    nt = S // TM
    t16 = TM // HALO

    def body(da_ref, pc_ref, prev_ref, next_ref, dm_ref, dmn_ref, cw_ref, xn_ref, gw_ref, dc_ref, gcw_ref):
        i = pl.program_id(0)

        @pl.when(i == 0)
        def _():
            gw_ref[...] = jnp.zeros_like(gw_ref)
            gcw_ref[...] = jnp.zeros_like(gcw_ref)

        xn = xn_ref[...]
        gw_ref[0:512, :] += _tn(da_ref[:, 0:512], xn)
        gw_ref[768:1280, :] += _tn(da_ref[:, 512:1024], xn)
        gw_ref[512:768, :] += _tn(da_ref[:, 1024:1280], xn)
        def piece_grad(k):
            rows = slice(PA_W + k * CONV_W, PA_W + (k + 1) * CONV_W)
            gw_ref[rows, :] += _tn(dc_ref[:, k * CONV_W:(k + 1) * CONV_W], xn)

        _conv_bwd_tile(pc_ref, prev_ref, next_ref, dm_ref, dmn_ref, cw_ref, dc_ref, gcw_ref, i > 0, i < nt - 1,
                       piece_grad)

    row = lambda i: (i, 0)
    fixed = lambda i: (0, 0)
    nxt = lambda i: jnp.minimum((i + 1) * t16, nt * t16 - 1)
    return pl.pallas_call(
        body,
        name="grad_w_in",
        grid=(nt,),
        in_specs=[
            pl.BlockSpec((TM, PA_W), row),
            pl.BlockSpec((TM, PC_W), row),
            _prev_rows(PC_W),
            pl.BlockSpec((HALO, PC_W), lambda i: (nxt(i), 0)),
            pl.BlockSpec((TM, CONV_W), lambda i: (i, 1)),
            pl.BlockSpec((HALO, CONV_W), lambda i: (nxt(i), 1)),
            pl.BlockSpec((CONV_K, CONV_W), fixed),
            pl.BlockSpec((TM, D_MODEL), row),
        ],
        out_specs=[
            pl.BlockSpec((IN_W, D_MODEL), fixed),
            pl.BlockSpec((TM, PC_W), row),
            pl.BlockSpec((CONV_K, CONV_W), fixed),
        ],
        out_shape=[
            jax.ShapeDtypeStruct((IN_W, D_MODEL), F32),
            jax.ShapeDtypeStruct((S, PC_W), BF16),
            jax.ShapeDtypeStruct((CONV_K, CONV_W), F32),
        ],
        compiler_params=_params(("arbitrary",)),
    )(da, pc, pc, pc, dmix, dmix, conv_w, xn)


def _adam_update(w, g, m, v):
    c1 = 1.0 - ADAM_B1 ** ADAM_STEP
    c2 = 1.0 - ADAM_B2 ** ADAM_STEP
    nm = ADAM_B1 * m + (1.0 - ADAM_B1) * g
    nv = ADAM_B2 * v + (1.0 - ADAM_B2) * (g * g)
    return -ADAM_LR * ((nm / c1) / (jnp.sqrt(nv / c2) + ADAM_EPS) + ADAM_WD * w), nm, nv


def _sum_chips_adamw(own, others, w, m, v, name):
    def body(own_ref, p_ref, w_ref, m_ref, v_ref, g_ref, d_ref, nm_ref, nv_ref):
        g = own_ref[...]
        for k in range(N_CHIP - 1):
            g = g + p_ref[k].astype(F32)
        g_ref[...] = g
        d_ref[...], nm_ref[...], nv_ref[...] = _adam_update(w_ref[...], g, m_ref[...], v_ref[...])

    shape = jax.ShapeDtypeStruct(w.shape, F32)
    return pl.pallas_call(
        body,
        name=name,
        out_shape=[shape] * 4,
        compiler_params=_params(),
    )(own, others, w, m, v)


SMALL_ROWS = 96


def _small_adamw(parts, params):
    def body(parts_ref, *rest):
        prm, outs, total = rest[:12], rest[12:29], rest[29]
        me = 4 * lax.axis_index("x") + 2 * lax.axis_index("y") + lax.axis_index("c")
        acc = parts_ref[0]
        for d in range(1, N_DEV):
            acc = acc + parts_ref[d]
        total[...] = acc
        grads = (total[0:8, :], total[8:16, :], total[16:17, 0:8],
                 total[pl.ds(pl.multiple_of(32 + me * 8, 8), CONV_K), 0:64])
        outs[0][...] = total[24:25, 0:1]
        for k, g in enumerate(grads):
            w_ref, m_ref, v_ref = prm[3 * k:3 * k + 3]
            g_ref, d_ref, nm_ref, nv_ref = outs[1 + 4 * k:5 + 4 * k]
            g_ref[...] = g
            d_ref[...], nm_ref[...], nv_ref[...] = _adam_update(w_ref[...], g, m_ref[...], v_ref[...])

    flat = [a for p in params for a in p]
    out_shape = [jax.ShapeDtypeStruct((1, 1), F32)]
    for p in params:
        out_shape += [jax.ShapeDtypeStruct(p[0].shape, F32)] * 4
    return pl.pallas_call(
        body,
        name="adamw_small",
        out_shape=out_shape,
        scratch_shapes=[pltpu.VMEM((SMALL_ROWS, 128), F32)],
        compiler_params=_params(),
    )(parts, *flat)


def kernel(x, norm_g, w_in, sinks, conv_w, w_out, final_g, loss_target, m_norm_g, m_w_in, m_sinks, m_conv_w, m_w_out, m_final_g, v_norm_g, v_w_in, v_sinks, v_conv_w, v_w_out, v_final_g):
    S = x.shape[1]
    x2 = x.reshape(S, D_MODEL)
    t2 = loss_target.reshape(S, D_MODEL)
    ng = norm_g.reshape(1, D_MODEL)
    fg = final_g.reshape(1, D_MODEL)

    cw_pad = jnp.zeros((8, 128), F32).at[0:CONV_K, 0:64].set(conv_w)
    xn, tab, wt = _prologue(x2, ng, w_in.T.astype(BF16))
    pa, pc, (wo, cw_all) = _fwd_proj(xn, wt, [w_out.astype(BF16), cw_pad])
    cw = cw_all.reshape(N_DEV, 8, 128)[:, 0:CONV_K, 0:64].transpose(1, 0, 2).reshape(CONV_K, CONV_W)
    ya, att, probs, psinks, q_stack = _attn_fwd(pa, tab, sinks)
    dh, dmix, g_wo, g_fg, loss_part = _out_loss(x2, t2, ya, pc, cw, wo, fg)
    da, g_sinks = _attn_bwd(pa, dmix, att, probs, psinks, q_stack, tab)
    g_wt, dc, g_cw = _grad_w_in(da, pc, dmix, cw, xn)
    cw_pack = jnp.pad(g_cw.reshape(CONV_K, N_DEV, 64).transpose(1, 0, 2),
                      ((0, 0), (0, 8 - CONV_K), (0, 64))).reshape(N_DEV * 8, 128)
    small = jnp.concatenate([g_fg.reshape(8, 128), g_sinks, loss_part, cw_pack], axis=0)
    grad_x, parts, own, others = _grad_x(
        da, dc, wt, x2, dh, ng, small,
        [g_wt.reshape(N_DEV, SHARD_IN, D_MODEL), g_wo.reshape(N_DEV, SHARD_OUT, D_MODEL)])
    gt, dt, nmt, nvt = _sum_chips_adamw(own[0], others[0], w_in.T, m_w_in.T, v_w_in.T, "adamw_w_in")
    grad_w_in, d_w_in, nm_w_in, nv_w_in = gt.T, dt.T, nmt.T, nvt.T
    grad_w_out, d_w_out, nm_w_out, nv_w_out = _sum_chips_adamw(
        own[1], others[1], w_out, m_w_out, v_w_out, "adamw_w_out")
    vec = lambda a: a.reshape(8, 128)
    row = lambda a: a.reshape(1, 8)
    res = _small_adamw(parts, [
        (vec(norm_g), vec(m_norm_g), vec(v_norm_g)), (vec(final_g), vec(m_final_g), vec(v_final_g)),
        (row(sinks), row(m_sinks), row(v_sinks)), (conv_w, m_conv_w, v_conv_w)])
    loss = res[0].reshape(())
    grad_norm_g, d_ng, nm_ng, nv_ng = [a.reshape(D_MODEL) for a in res[1:5]]
    grad_final_g, d_fg, nm_fg, nv_fg = [a.reshape(D_MODEL) for a in res[5:9]]
    grad_sinks, d_sk, nm_sk, nv_sk = [a.reshape(N_Q_HEADS) for a in res[9:13]]
    grad_conv_w, d_cw, nm_cw, nv_cw = res[13:17]

    return (loss, grad_x.reshape(1, S, D_MODEL), grad_norm_g, grad_w_in, grad_sinks, grad_conv_w, grad_w_out, grad_final_g,
            d_ng, d_w_in, d_sk, d_cw, d_w_out, d_fg,
            nm_ng, nm_w_in, nm_sk, nm_cw, nm_w_out, nm_fg,
            nv_ng, nv_w_in, nv_sk, nv_cw, nv_w_out, nv_fg)
```

```python
import numpy as np
import jax
import jax.numpy as jnp
from jax import lax
from jax.experimental import pallas as pl
from jax.experimental.pallas import tpu as pltpu

F32 = jnp.float32
BF16 = jnp.bfloat16
MESH = pl.DeviceIdType.MESH

D_MODEL = 1024
HEAD_DIM = 64
N_Q_HEADS = 8
GROUP = 4
ATTN_W = 512
KV_W = 128
BLK = 128
CONV_W = 512
CONV_K = 3
IN_W = 3328
PA_W = 1280
PC_W = 2048
EPS = 1e-5
ROPE_THETA = 500000.0
ROT_DIM = 16
N_DEV = 8
N_CHIP = 4
SHARD_IN = IN_W // N_DEV
SHARD_OUT = D_MODEL // N_DEV

ADAM_LR = 0.001
ADAM_B1 = 0.9
ADAM_B2 = 0.999
ADAM_EPS = 1e-08
ADAM_WD = 0.01
ADAM_STEP = 10

ACT = jnp.bfloat16

TM = 512
TQ = 1024
HALO = 16
VMEM_LIMIT = 56 * 1024 * 1024

NT_DIMS = (((1,), (1,)), ((), ()))
TN_DIMS = (((0,), (0,)), ((), ()))


def _params(sem=None):
    kw = dict(vmem_limit_bytes=VMEM_LIMIT)
    if sem is not None:
        kw["dimension_semantics"] = sem
    return pltpu.CompilerParams(**kw)


def _nt(a, b):
    return lax.dot_general(a, b, NT_DIMS, preferred_element_type=F32)


def _tn(a, b):
    return lax.dot_general(a, b, TN_DIMS, preferred_element_type=F32)


def _nn(a, b):
    return jnp.dot(a, b, preferred_element_type=F32)


def _silu(g):
    return g * jax.nn.sigmoid(g)


def _silu_and_grad(g):
    s = jax.nn.sigmoid(g)
    return g * s, s * (1.0 + g * (1.0 - s))


class _AllGatherInSteps:
    def __init__(self, arrs, forward_step):
        self.blocks = [(a.shape, a.dtype) for a in arrs]
        self.n = len(arrs)
        self.forward_step = forward_step

    def out_shape(self):
        return [jax.ShapeDtypeStruct((N_DEV * s[0], s[1]), d) for s, d in self.blocks]

    def scratch_shapes(self):
        return [pltpu.SemaphoreType.DMA((7 * self.n,)), pltpu.SemaphoreType.DMA((7 * self.n,)),
                pltpu.SemaphoreType.DMA((self.n,))]

    def emit(self, step, n_steps, x_refs, out_refs, scratch):
        assert n_steps > self.forward_step + 1
        send_sems, recv_sems, local_sems = scratch
        x, y, c = lax.axis_index("x"), lax.axis_index("y"), lax.axis_index("c")
        me, sibling = (x, y, c), (x, y, 1 - c)
        chips = [(1 - x, y), (x, 1 - y), (1 - x, 1 - y)]

        def rows(a, px, py, pc):
            m = self.blocks[a][0][0]
            return out_refs[a].at[pl.ds((4 * px + 2 * py + pc) * m, m), :]

        def copy(a, k, block, to, src=None):
            return pltpu.make_async_remote_copy(
                src_ref=rows(a, *block) if src is None else src, dst_ref=rows(a, *block),
                send_sem=send_sems.at[a * 7 + k], recv_sem=recv_sems.at[a * 7 + k],
                device_id=to, device_id_type=MESH)

        def mine(a):
            return pltpu.make_async_copy(x_refs[a], rows(a, *me), local_sems.at[a])

        def first(a):
            return ([copy(a, 0, me, sibling, src=x_refs[a])]
                    + [copy(a, 1 + j, me, (*chip, c), src=x_refs[a]) for j, chip in enumerate(chips)])

        def passed(a):
            return [copy(a, 4 + j, (*chip, c), sibling) for j, chip in enumerate(chips)]

        @pl.when(step == 0)
        def _():
            for a in range(self.n):
                mine(a).start()
                for cp in first(a):
                    cp.start()

        @pl.when(step == self.forward_step)
        def _():
            for j, chip in enumerate(chips):
                for a in range(self.n):
                    copy(a, 1 + j, (*chip, c), me).wait_recv()
                    copy(a, 4 + j, (*chip, c), sibling).start()

        def finish():
            @pl.when(step == n_steps - 1)
            def _():
                for a in range(self.n):
                    copy(a, 0, sibling, me).wait_recv()
                    for j, chip in enumerate(chips):
                        copy(a, 4 + j, (*chip, 1 - c), me).wait_recv()
                    for cp in first(a) + passed(a):
                        cp.wait_send()
                    mine(a).wait()

        return finish


class _AllGatherViaNeighbours:
    def __init__(self, arr, first, second):
        (self.m, self.ncol), self.dtype = arr.shape, arr.dtype
        assert self.m % 32 == 0
        self.first, self.second = first, second

    def out_shape(self):
        return [jax.ShapeDtypeStruct((N_DEV * self.m, self.ncol), self.dtype)]

    def scratch_shapes(self):
        return [pltpu.SemaphoreType.DMA((9,)), pltpu.SemaphoreType.DMA((9,)), pltpu.SemaphoreType.DMA]

    def emit(self, step, n_steps, x_ref, out_ref, scratch):
        assert 0 < self.first < self.second < n_steps - 1
        send_sems, recv_sems, local_sem = scratch
        x, y, c = lax.axis_index("x"), lax.axis_index("y"), lax.axis_index("c")
        half = self.m // 2
        sibling, xn, yn = (x, y, 1 - c), (1 - x, y, c), (x, 1 - y, c)

        def rows(dev, part=None):
            px, py, pc = dev
            base = (4 * px + 2 * py + pc) * self.m
            if part is None:
                return out_ref.at[pl.ds(base, self.m), :]
            return out_ref.at[pl.ds(base + part * half, half), :]

        def copy(k, dev, to, part=None, src=None):
            return pltpu.make_async_remote_copy(
                src_ref=rows(dev, part) if src is None else src, dst_ref=rows(dev, part),
                send_sem=send_sems.at[k], recv_sem=recv_sems.at[k], device_id=to, device_id_type=MESH)

        me, dg = (x, y, c), (1 - x, 1 - y, c)
        mine = pltpu.make_async_copy(x_ref, rows(me), local_sem)
        sends = [
            copy(0, me, sibling, src=x_ref), copy(1, me, xn, src=x_ref), copy(2, me, yn, src=x_ref),
            copy(3, xn, yn, part=0), copy(4, yn, xn, part=1),
            copy(5, xn, sibling), copy(6, yn, sibling), copy(7, dg, sibling, part=0), copy(8, dg, sibling, part=1),
        ]
        other = lambda dev: (dev[0], dev[1], 1 - c)
        arrivals = [
            copy(0, other(me), sibling), copy(1, xn, xn), copy(2, yn, yn), copy(3, dg, yn, part=0),
            copy(4, dg, xn, part=1), copy(5, other(xn), sibling), copy(6, other(yn), sibling),
            copy(7, other(dg), sibling, part=0), copy(8, other(dg), sibling, part=1),
        ]

        @pl.when(step == 0)
        def _():
            mine.start()
            for k in (0, 1, 2):
                sends[k].start()

        @pl.when(step == self.first)
        def _():
            arrivals[1].wait_recv()
            sends[3].start()
            sends[5].start()
            arrivals[2].wait_recv()
            sends[4].start()
            sends[6].start()

        @pl.when(step == self.second)
        def _():
            arrivals[3].wait_recv()
            sends[7].start()
            arrivals[4].wait_recv()
            sends[8].start()

        def finish():
            @pl.when(step == n_steps - 1)
            def _():
                for k in (0, 5, 6, 7, 8):
                    arrivals[k].wait_recv()
                for cp in sends:
                    cp.wait_send()
                mine.wait()

        return finish


class _ReduceScatter:
    def __init__(self, grads):
        self.shapes = [g.shape[1:] for g in grads]
        self.n = len(grads)
        self.items = tuple((a, r) for r in (1, 2, 3, 0) for a in range(self.n))
        self.steps = len(self.items) + 2

    def out_shape(self):
        own = [jax.ShapeDtypeStruct(s, F32) for s in self.shapes]
        ici = [jax.ShapeDtypeStruct((N_CHIP - 1,) + s, BF16) for s in self.shapes]
        land = [jax.ShapeDtypeStruct((N_CHIP,) + s, F32) for s in self.shapes]
        return own + ici + land

    def scratch_shapes(self):
        n_items = len(self.items)
        return ([pltpu.VMEM((2,) + s, F32) for s in self.shapes]
                + [pltpu.VMEM((N_CHIP - 1,) + s, BF16) for s in self.shapes]
                + [pltpu.VMEM(s, F32) for s in self.shapes]
                + [pltpu.SemaphoreType.DMA((self.n * N_CHIP,))] * 2
                + [pltpu.SemaphoreType.DMA((2 * n_items,))]
                + [pltpu.SemaphoreType.DMA((self.n * (N_CHIP - 1),))] * 2
                + [pltpu.SemaphoreType.DMA((self.n,))])

    def emit(self, step, n_steps, g_refs, out_refs, scratch):
        assert n_steps > self.steps
        n = self.n
        own_refs, ici_refs, land_refs = out_refs[:n], out_refs[n:2 * n], out_refs[2 * n:]
        stage, pair_bf, pair_own = scratch[:n], scratch[n:2 * n], scratch[2 * n:3 * n]
        sib_send, sib_recv, load_sems, ici_send, ici_recv, own_sems = scratch[3 * n:]
        x, y, c = lax.axis_index("x"), lax.axis_index("y"), lax.axis_index("c")

        def chip_of(r):
            return (x ^ (r >> 1), y ^ (r & 1))

        def block_of(r, core):
            cx, cy = chip_of(r)
            return 4 * cx + 2 * cy + core

        def to_sibling(a, r):
            return pltpu.make_async_remote_copy(
                src_ref=g_refs[a].at[block_of(r, 1 - c)], dst_ref=land_refs[a].at[r],
                send_sem=sib_send.at[a * N_CHIP + r], recv_sem=sib_recv.at[a * N_CHIP + r],
                device_id=(x, y, 1 - c), device_id_type=MESH)

        def loads(k):
            a, r = self.items[k]
            return (pltpu.make_async_copy(g_refs[a].at[block_of(r, c)], stage[a].at[0], load_sems.at[2 * k]),
                    pltpu.make_async_copy(land_refs[a].at[r], stage[a].at[1], load_sems.at[2 * k + 1]))

        def to_owner(k):
            a, r = self.items[k]
            if r == 0:
                return pltpu.make_async_copy(pair_own[a], own_refs[a], own_sems.at[a])
            return pltpu.make_async_remote_copy(
                src_ref=pair_bf[a].at[r - 1], dst_ref=ici_refs[a].at[r - 1],
                send_sem=ici_send.at[a * (N_CHIP - 1) + r - 1], recv_sem=ici_recv.at[a * (N_CHIP - 1) + r - 1],
                device_id=(*chip_of(r), c), device_id_type=MESH)

        @pl.when(step == 0)
        def _():
            for a, r in self.items:
                to_sibling(a, r).start()

        for k, (a, r) in enumerate(self.items):
            @pl.when(step == 1 + k)
            def _(k=k, a=a, r=r):
                to_sibling(a, r).wait_recv()
                for cp in loads(k):
                    cp.start()

            @pl.when(step == 2 + k)
            def _(k=k, a=a, r=r):
                for cp in loads(k):
                    cp.wait()
                total = stage[a][0] + stage[a][1]
                if r == 0:
                    pair_own[a][...] = total
                else:
                    pair_bf[a][r - 1] = total.astype(BF16)
                to_owner(k).start()

        def finish():
            @pl.when(step == n_steps - 1)
            def _():
                for k, (a, r) in enumerate(self.items):
                    if r == 0:
                        to_owner(k).wait()
                    else:
                        to_owner(k).wait_send()
                        to_owner(k).wait_recv()
                for a, r in self.items:
                    to_sibling(a, r).wait_send()

        return finish


def _prologue(x, norm_g, w_shard):
    S = x.shape[0]
    n_steps = S // TM
    half = ROT_DIM // 2
    pos = jnp.arange(S, dtype=jnp.int32).astype(F32)
    inv_freq = ROPE_THETA ** (-jnp.arange(0, ROT_DIM, 2, dtype=F32) / ROT_DIM)
    ang = inv_freq[:, None] * pos[None, :]
    cs = jnp.concatenate([jnp.cos(ang), jnp.sin(ang)], axis=0)
    ag = _AllGatherViaNeighbours(w_shard, first=n_steps // 2 - 2, second=n_steps - 4)

    def body(x_ref, g_ref, cs_ref, w_ref, xn_ref, nrm_ref, r_ref, tab_ref, wt_ref, *ag_scratch):
        step = pl.program_id(0)
        finish = ag.emit(step, n_steps, w_ref, wt_ref, ag_scratch)
        xv = x_ref[...]
        r = lax.rsqrt(jnp.mean(xv * xv, axis=-1, keepdims=True) + EPS)
        n = xv * r
        r_ref[...] = r
        nrm_ref[...] = n
        xn_ref[...] = (n * g_ref[...]).astype(BF16)

        xt = jnp.concatenate([cs_ref[...], jnp.zeros((128 - 2 * half, TM), F32)], axis=0).T
        lane = lax.broadcasted_iota(jnp.int32, (TM, 128), 1)
        rr = lane & (HEAD_DIM - 1)
        first = lane < HEAD_DIM

        def at(shift_first, shift_second):
            return jnp.where(first, pltpu.roll(xt, shift_first, 1) if shift_first else xt,
                             pltpu.roll(xt, shift_second, 1))

        cos_lo, cos_hi = at(0, HEAD_DIM), at(half, HEAD_DIM + half)
        sin_lo, sin_hi = at(128 - half, HEAD_DIM - half), at(0, HEAD_DIM)
        tab_ref[:, 0:128] = jnp.where(rr < half, cos_lo, jnp.where(rr < ROT_DIM, cos_hi, 1.0))
        tab_ref[:, 128:256] = jnp.where(rr < half, -sin_lo, 0.0)
        tab_ref[:, 256:384] = jnp.where((rr >= half) & (rr < ROT_DIM), sin_hi, 0.0)
        finish()

    any_spec = pl.BlockSpec(memory_space=pl.ANY)
    return pl.pallas_call(
        body,
        name="prologue_all_gather_w_in",
        grid=(n_steps,),
        in_specs=[
            pl.BlockSpec((TM, D_MODEL), lambda i: (i, 0)),
            pl.BlockSpec((1, D_MODEL), lambda i: (0, 0)),
            pl.BlockSpec((2 * half, TM), lambda i: (0, i)),
            any_spec,
        ],
        out_specs=[
            pl.BlockSpec((TM, D_MODEL), lambda i: (i, 0)),
            pl.BlockSpec((TM, D_MODEL), lambda i: (i, 0)),
            pl.BlockSpec((TM, 1), lambda i: (i, 0)),
            pl.BlockSpec((TM, 384), lambda i: (i, 0)),
            any_spec,
        ],
        out_shape=[
            jax.ShapeDtypeStruct((S, D_MODEL), BF16),
            jax.ShapeDtypeStruct((S, D_MODEL), F32),
            jax.ShapeDtypeStruct((S, 1), F32),
            jax.ShapeDtypeStruct((S, 384), F32),
        ] + ag.out_shape(),
        scratch_shapes=ag.scratch_shapes(),
        compiler_params=_params(("arbitrary",)),
    )(x, norm_g, cs, w_shard)


def _fwd_proj(xn, wt, later):
    S = xn.shape[0]

    n_steps = S // TM
    ag = _AllGatherInSteps(later, forward_step=6)

    def body(xn_ref, wt_ref, *rest):
        later_refs, rest = rest[:ag.n], rest[ag.n:]
        pa_ref, pc_ref = rest[:2]
        gathered, ag_scratch = rest[2:2 + ag.n], rest[2 + ag.n:]
        step = pl.program_id(0)
        finish = ag.emit(step, n_steps, later_refs, gathered, ag_scratch)
        xn = xn_ref[...]
        pa_ref[:, 0:512] = _nt(xn, wt_ref[0:512, :]).astype(ACT)
        pa_ref[:, 512:1024] = _nt(xn, wt_ref[768:1280, :]).astype(ACT)
        pa_ref[:, 1024:1280] = _nt(xn, wt_ref[512:768, :]).astype(ACT)
        pc_ref[...] = _nt(xn, wt_ref[1280:3328, :]).astype(ACT)
        finish()

    any_spec = pl.BlockSpec(memory_space=pl.ANY)
    outs = pl.pallas_call(
        body,
        name="fwd_proj_all_gather",
        grid=(n_steps,),
        in_specs=[
            pl.BlockSpec((TM, D_MODEL), lambda i: (i, 0)),
            pl.BlockSpec((IN_W, D_MODEL), lambda i: (0, 0)),
        ] + [any_spec] * ag.n,
        out_specs=[
            pl.BlockSpec((TM, PA_W), lambda i: (i, 0)),
            pl.BlockSpec((TM, PC_W), lambda i: (i, 0)),
        ] + [any_spec] * ag.n,
        out_shape=[
            jax.ShapeDtypeStruct((S, PA_W), ACT),
            jax.ShapeDtypeStruct((S, PC_W), ACT),
        ] + ag.out_shape(),
        scratch_shapes=ag.scratch_shapes(),
        compiler_params=_params(("arbitrary",)),
    )(xn, wt, *later)
    return outs[0], outs[1], outs[2:]


def _rope(t, tab):
    return (t * tab[:, 0:128] + pltpu.roll(t, 120, 1) * tab[:, 128:256]
            + pltpu.roll(t, 8, 1) * tab[:, 256:384])


def _rope_t(d, tab):
    return (d * tab[:, 0:128] + pltpu.roll(d * tab[:, 128:256], 8, 1)
            + pltpu.roll(d * tab[:, 256:384], 120, 1))


def _fill_kv(kall, kvc_ref, kvp_ref, tabc_ref, tabp_ref):
    for lo, kv_ref, tab_ref, n in ((0, kvp_ref, tabp_ref, BLK), (BLK, kvc_ref, tabc_ref, TQ)):
        k = _rope(kv_ref[:, 0:128].astype(F32), tab_ref[...])
        v = kv_ref[:, 128:256].astype(F32)
        kall[0, lo:lo + n, :] = k.astype(BF16)
        kall[1, lo:lo + n, :] = pltpu.roll(k, 64, 1).astype(BF16)
        kall[2, lo:lo + n, :] = v.astype(BF16)
        kall[3, lo:lo + n, :] = pltpu.roll(v, 64, 1).astype(BF16)


HEADS = (((0, 0), (1, 0), (2, 1), (3, 1)), ((0, 1), (1, 1), (2, 0), (3, 0)))


def _upper():
    kj = lax.broadcasted_iota(jnp.int32, (BLK, 4 * BLK), 0)
    qi = lax.broadcasted_iota(jnp.int32, (BLK, 4 * BLK), 1) & (BLK - 1)
    return kj > qi


def _merge(upper, both):
    return jnp.where(upper, both[0:BLK, :], both[BLK:2 * BLK, :])


def _split_store(ref, s, upper_b, vb):
    first = vb * upper_b
    ref[s, 0:BLK, :] = first
    ref[s, BLK:2 * BLK, :] = vb - first


def _sink_rows(sink_ref):
    return [jnp.concatenate([jnp.full((1, BLK), sink_ref[2 * p + e], F32) for p, e in HEADS[s]], axis=1)
            for s in range(2)]


def _stack_heads(ref, slot, half, pairs, s=None):
    for a, (p, e) in enumerate(HEADS[slot if s is None else s]):
        ref[slot, a * BLK:(a + 1) * BLK, :] = jnp.where(half[e], pairs[p], 0.0).astype(BF16)


def _unstack_pair(half, outs, p):
    lo = 0 if p < 2 else 1
    rows = slice(p * BLK, (p + 1) * BLK)
    return jnp.where(half[0], outs[lo][rows, :], outs[1 - lo][rows, :])


def _softmax(sm, sinks):
    m = jnp.maximum(jnp.max(sm, axis=0, keepdims=True), sinks)
    p = jnp.exp(sm - m)
    es = jnp.exp(sinks - m)
    inv = 1.0 / (jnp.sum(p, axis=0, keepdims=True) + es)
    return p * inv, es * inv


def _scores(kk, q_stack, first):
    st = _nt(kk, q_stack)
    prev = st[0:BLK, :]
    if first is not None:
        prev = prev + jnp.where(first, -jnp.inf, 0.0)
    return prev, st[BLK:2 * BLK, :]


def _attn_specs(tile):
    nb = TQ // BLK
    prev = lambda i: jnp.maximum(tile(i) * nb - 1, 0)
    return [
        pl.BlockSpec(memory_space=pltpu.SMEM),
        pl.BlockSpec((TQ, ATTN_W), lambda i: (tile(i), 0)),
        pl.BlockSpec((TQ, ATTN_W), lambda i: (tile(i), 1)),
        pl.BlockSpec((TQ, 2 * KV_W), lambda i: (tile(i), 4)),
        pl.BlockSpec((BLK, 2 * KV_W), lambda i: (prev(i), 4)),
        pl.BlockSpec((TQ, 384), lambda i: (tile(i), 0)),
        pl.BlockSpec((BLK, 384), lambda i: (prev(i), 0)),
    ]


def _attn_fwd(pa, tab, sinks):
    S = pa.shape[0]
    nb = TQ // BLK

    def body(sink_ref, q_ref, g_ref, kvc_ref, kvp_ref, tabc_ref, tabp_ref, o_ref, att_ref, pm_ref, ps_ref,
             qs_ref, kall, p_sc):
        i = pl.program_id(0)
        _fill_kv(kall, kvc_ref, kvp_ref, tabc_ref, tabp_ref)
        lane = lax.broadcasted_iota(jnp.int32, (BLK, 128), 1)
        half = [lane < HEAD_DIM, lane >= HEAD_DIM]
        upper = _upper()
        upper_b = upper.astype(BF16)
        sinks = _sink_rows(sink_ref)
        for j in range(nb):
            rq = slice(j * BLK, (j + 1) * BLK)
            rk = slice(j * BLK, (j + 2) * BLK)
            tab = tabc_ref[rq, :]
            qr = [_rope(q_ref[rq, p * 128:(p + 1) * 128].astype(F32), tab) * 0.125 for p in range(4)]
            outs = []
            for s in range(2):
                _stack_heads(qs_ref, 2 * j + s, half, qr, s)
                prev, cur = _scores(kall[s, rk, :], qs_ref[2 * j + s], i == 0 if j == 0 else None)
                prob, psink = _softmax(jnp.where(upper, prev, cur), sinks[s])
                pb = prob.astype(BF16)
                pm_ref[(2 * j + s) * BLK:(2 * j + s + 1) * BLK, :] = pb
                ps_ref[2 * j + s:2 * j + s + 1, :] = psink
                _split_store(p_sc, s, upper_b, pb)
                outs.append(_tn(p_sc[s], kall[2 + s, rk, :]))
            for p in range(4):
                cols = slice(p * 128, (p + 1) * 128)
                att = _unstack_pair(half, outs, p)
                att_ref[rq, cols] = att.astype(BF16)
                o_ref[rq, cols] = (att * _silu(g_ref[rq, cols].astype(F32))).astype(BF16)

    return pl.pallas_call(
        body,
        name="attn_fwd",
        grid=(S // TQ,),
        in_specs=_attn_specs(lambda i: i),
        out_specs=[pl.BlockSpec((TQ, ATTN_W), lambda i: (i, 0))] * 2 + [
            pl.BlockSpec((2 * TQ, 4 * BLK), lambda i: (i, 0)),
            pl.BlockSpec((2 * nb, 4 * BLK), lambda i: (i, 0)),
            pl.BlockSpec((2 * nb, 4 * BLK, 128), lambda i: (i, 0, 0)),
        ],
        out_shape=[jax.ShapeDtypeStruct((S, ATTN_W), BF16)] * 2 + [
            jax.ShapeDtypeStruct((2 * S, 4 * BLK), BF16),
            jax.ShapeDtypeStruct((2 * S // BLK, 4 * BLK), F32),
            jax.ShapeDtypeStruct((2 * S // BLK, 4 * BLK, 128), BF16),
        ],
        scratch_shapes=[
            pltpu.VMEM((4, BLK + TQ, 128), BF16),
            pltpu.VMEM((2, 2 * BLK, 4 * BLK), BF16),
        ],
        compiler_params=_params(("arbitrary",)),
    )(sinks, pa, pa, pa, pa, tab, tab)


def _shift_down(u, halo_ref, has_prev):
    def halo_u(r):
        hu = halo_ref[r:r + 1, 512:1024].astype(F32) * halo_ref[r:r + 1, 1024:1536].astype(F32)
        return jnp.where(has_prev, hu, 0.0)

    row = lax.broadcasted_iota(jnp.int32, u.shape, 0)
    um1 = jnp.where(row == 0, halo_u(HALO - 1), pltpu.roll(u, 1, 0))
    um2 = jnp.where(row == 0, halo_u(HALO - 2), jnp.where(row == 1, halo_u(HALO - 1), pltpu.roll(u, 2, 0)))
    return um1, um2


def _conv_tile(pc_ref, halo_ref, w_ref, has_prev):
    b = pc_ref[:, 0:512].astype(F32)
    c = pc_ref[:, 512:1024].astype(F32)
    hh = pc_ref[:, 1024:1536].astype(F32)
    gc = pc_ref[:, 1536:2048].astype(F32)
    u = c * hh
    um1, um2 = _shift_down(u, halo_ref, has_prev)
    cv = w_ref[0:1, :] * um2 + w_ref[1:2, :] * um1 + w_ref[2:3, :] * u
    return b, c, hh, gc, u, um1, um2, cv


def _prev_rows(width, col=0):
    return pl.BlockSpec((HALO, width), lambda i: (jnp.maximum(i * (TM // HALO) - 1, 0), col))


def _out_loss(x, target, ya, pc, conv_w, w_out, final_g):
    S = x.shape[0]

    def body(x_ref, t_ref, ya_ref, pc_ref, halo_ref, cw_ref, wo_ref, fg_ref,
             dh_ref, dmix_ref, gwo_ref, gfg_ref, loss_ref):
        @pl.when(pl.program_id(0) == 0)
        def _():
            gwo_ref[...] = jnp.zeros_like(gwo_ref)
            gfg_ref[...] = jnp.zeros_like(gfg_ref)
            loss_ref[...] = jnp.zeros_like(loss_ref)

        b, _, _, gc, _, _, _, cv = _conv_tile(pc_ref, halo_ref, cw_ref, pl.program_id(0) > 0)
        yc = (b * cv * _silu(gc)).astype(BF16)
        mix = jnp.concatenate([ya_ref[...], yc], axis=1)
        wo = wo_ref[...]
        fg = fg_ref[...]
        h = x_ref[...] + _nn(mix, wo)
        r = lax.rsqrt(jnp.mean(h * h, axis=-1, keepdims=True) + EPS)
        n = h * r
        err = n * fg - t_ref[...]
        loss_ref[...] += jnp.broadcast_to(
            0.5 * jnp.sum(jnp.mean(err * err, axis=-1, keepdims=True), axis=0, keepdims=True), (8, 128))
        gfg_ref[...] += jnp.sum(err * n, axis=0, keepdims=True) * (1.0 / D_MODEL)
        dyg = err * (fg * (1.0 / D_MODEL))
        dh = r * (dyg - n * jnp.mean(dyg * n, axis=-1, keepdims=True))
        dh_ref[...] = dh
        dhb = dh.astype(BF16)
        dmix_ref[...] = _nt(dhb, wo).astype(ACT)
        gwo_ref[...] += _tn(mix, dhb)

    row = lambda i: (i, 0)
    fixed = lambda i: (0, 0)
    return pl.pallas_call(
        body,
        name="out_loss",
        grid=(S // TM,),
        in_specs=[
            pl.BlockSpec((TM, D_MODEL), row),
            pl.BlockSpec((TM, D_MODEL), row),
            pl.BlockSpec((TM, ATTN_W), row),
            pl.BlockSpec((TM, PC_W), row),
            _prev_rows(PC_W),
            pl.BlockSpec((CONV_K, CONV_W), fixed),
            pl.BlockSpec((D_MODEL, D_MODEL), fixed),
            pl.BlockSpec((1, D_MODEL), fixed),
        ],
        out_specs=[
            pl.BlockSpec((TM, D_MODEL), row),
            pl.BlockSpec((TM, D_MODEL), row),
            pl.BlockSpec((D_MODEL, D_MODEL), fixed),
            pl.BlockSpec((1, D_MODEL), fixed),
            pl.BlockSpec((8, 128), fixed),
        ],
        out_shape=[
            jax.ShapeDtypeStruct((S, D_MODEL), F32),
            jax.ShapeDtypeStruct((S, D_MODEL), ACT),
            jax.ShapeDtypeStruct((D_MODEL, D_MODEL), F32),
            jax.ShapeDtypeStruct((1, D_MODEL), F32),
            jax.ShapeDtypeStruct((8, 128), F32),
        ],
        compiler_params=_params(("arbitrary",)),
    )(x, target, ya, pc, pc, conv_w, w_out, final_g)


def _attn_bwd(pa, dmix, att, probs, psinks, q_stack, tab):
    S = pa.shape[0]
    nt = S // TQ
    nb = TQ // BLK

    def body(g_ref, kvc_ref, kvp_ref, tabc_ref, tabp_ref, dm_ref, att_ref, pm_ref, ps_ref, qs_ref,
             d_ref, dsink_ref, kall, dkv, carry, do_sc, p_sc, ds_sc, dsink_acc):
        step = pl.program_id(0)

        @pl.when(step == 0)
        def _():
            carry[...] = jnp.zeros_like(carry)
            dsink_acc[...] = jnp.zeros_like(dsink_acc)

        _fill_kv(kall, kvc_ref, kvp_ref, tabc_ref, tabp_ref)
        dkv[0:TQ, :] = jnp.zeros((TQ, 2 * KV_W), F32)
        dkv[TQ:TQ + BLK, :] = carry[...]
        lane = lax.broadcasted_iota(jnp.int32, (BLK, 128), 1)
        half = [lane < HEAD_DIM, lane >= HEAD_DIM]
        upper = _upper()
        upper_b = upper.astype(BF16)
        for j in range(nb):
            rq = slice(j * BLK, (j + 1) * BLK)
            rk = slice(j * BLK, (j + 2) * BLK)
            tab = tabc_ref[rq, :]
            pair = [slice(p * 128, (p + 1) * 128) for p in range(4)]
            g = [g_ref[rq, c].astype(F32) for c in pair]
            da = [dm_ref[rq, c].astype(F32) for c in pair]
            gate = [_silu_and_grad(g[p]) for p in range(4)]
            do = [da[p] * gate[p][0] for p in range(4)]
            dqs, dks, dvs = [], [], []
            for s in range(2):
                kk = kall[s, rk, :]
                vv = kall[2 + s, rk, :]
                _stack_heads(do_sc, s, half, do)
                pb = pm_ref[(2 * j + s) * BLK:(2 * j + s + 1) * BLK, :]
                prob = pb.astype(F32)
                _split_store(p_sc, s, upper_b, pb)
                dprob = _merge(upper, _nt(vv, do_sc[s]))
                dsum = jnp.sum(dprob * prob, axis=0, keepdims=True)
                _split_store(ds_sc, s, upper_b, (prob * (dprob - dsum)).astype(BF16))
                dsink_acc[s, 0:1, :] += ps_ref[2 * j + s:2 * j + s + 1, :] * dsum
                dqs.append(_tn(ds_sc[s], kk))
                dks.append(_nn(ds_sc[s], qs_ref[2 * j + s]))
                dvs.append(_nn(p_sc[s], do_sc[s]))
            for p in range(4):
                d_ref[rq, pair[p]] = _rope_t(_unstack_pair(half, dqs, p) * 0.125, tab).astype(BF16)
                d_ref[rq, 512 + p * 128:512 + (p + 1) * 128] = (
                    da[p] * att_ref[rq, pair[p]].astype(F32) * gate[p][1]).astype(BF16)
            dkv[rk, 0:128] += dks[0] + pltpu.roll(dks[1], 64, 1)
            dkv[rk, 128:256] += dvs[0] + pltpu.roll(dvs[1], 64, 1)
        d_ref[:, 1024:1152] = _rope_t(dkv[BLK:BLK + TQ, 0:128], tabc_ref[...]).astype(BF16)
        d_ref[:, 1152:1280] = dkv[BLK:BLK + TQ, 128:256].astype(BF16)
        carry[...] = dkv[0:BLK, :]

        @pl.when(step == nt - 1)
        def _():
            lanes = lax.broadcasted_iota(jnp.int32, (8, 128), 1)
            out = jnp.zeros((8, 128), F32)
            for s in range(2):
                for a, (p, e) in enumerate(HEADS[s]):
                    tot = jnp.sum(dsink_acc[s, 0:1, a * BLK:(a + 1) * BLK], axis=1, keepdims=True)
                    out = jnp.where(lanes == 2 * p + e, -tot, out)
            dsink_ref[...] = out

    rev = lambda s: nt - 1 - s
    return pl.pallas_call(
        body,
        name="attn_bwd",
        grid=(nt,),
        in_specs=_attn_specs(rev)[2:] + [pl.BlockSpec((TQ, ATTN_W), lambda s: (nt - 1 - s, 0))] * 2 + [
            pl.BlockSpec((2 * TQ, 4 * BLK), lambda s: (nt - 1 - s, 0)),
            pl.BlockSpec((2 * nb, 4 * BLK), lambda s: (nt - 1 - s, 0)),
            pl.BlockSpec((2 * nb, 4 * BLK, 128), lambda s: (nt - 1 - s, 0, 0)),
        ],
        out_specs=[
            pl.BlockSpec((TQ, PA_W), lambda s: (nt - 1 - s, 0)),
            pl.BlockSpec((8, 128), lambda s: (0, 0)),
        ],
        out_shape=[
            jax.ShapeDtypeStruct((S, PA_W), BF16),
            jax.ShapeDtypeStruct((8, 128), F32),
        ],
        scratch_shapes=[
            pltpu.VMEM((4, BLK + TQ, 128), BF16),
            pltpu.VMEM((BLK + TQ, 2 * KV_W), F32),
            pltpu.VMEM((BLK, 2 * KV_W), F32),
            pltpu.VMEM((2, 4 * BLK, 128), BF16),
            pltpu.VMEM((2, 2 * BLK, 4 * BLK), BF16),
            pltpu.VMEM((2, 2 * BLK, 4 * BLK), BF16),
            pltpu.VMEM((2, 8, 4 * BLK), F32),
        ],
        compiler_params=_params(("arbitrary",)),
    )(pa, pa, pa, tab, tab, dmix, att, probs, psinks, q_stack)


def _conv_bwd_tile(pc_ref, prev_ref, next_ref, dm_ref, dmn_ref, w_ref, d_ref, gw_ref, has_prev, has_next,
                   on_piece):
    rows = pc_ref.shape[0]
    w0, w1, w2 = w_ref[0:1, :], w_ref[1:2, :], w_ref[2:3, :]
    b, c, hh, gc, u, um1, um2, cv = _conv_tile(pc_ref, prev_ref, w_ref, has_prev)
    sg, dsg = _silu_and_grad(gc)
    dy = dm_ref[...].astype(F32)
    dyb = dy * b
    dcv = dyb * sg

    def next_dcv(r):
        nd = (dmn_ref[r:r + 1, :].astype(F32) * next_ref[r:r + 1, 0:512].astype(F32)
              * _silu(next_ref[r:r + 1, 1536:2048].astype(F32)))
        return jnp.where(has_next, nd, 0.0)

    row = lax.broadcasted_iota(jnp.int32, (rows, CONV_W), 0)
    dp1 = jnp.where(row == rows - 1, next_dcv(0), pltpu.roll(dcv, rows - 1, 0))
    dp2 = jnp.where(row == rows - 1, next_dcv(1),
                    jnp.where(row == rows - 2, next_dcv(0), pltpu.roll(dcv, rows - 2, 0)))
    du = w2 * dcv + w1 * dp1 + w0 * dp2
    pieces = (lambda: dy * cv * sg, lambda: du * hh, lambda: du * c, lambda: dyb * cv * dsg)
    for k, piece in enumerate(pieces):
        d_ref[:, k * CONV_W:(k + 1) * CONV_W] = piece().astype(BF16)
        on_piece(k)
    gw_ref[0:1, :] += jnp.sum(dcv * um2, axis=0, keepdims=True)
    gw_ref[1:2, :] += jnp.sum(dcv * um1, axis=0, keepdims=True)
    gw_ref[2:3, :] += jnp.sum(dcv * u, axis=0, keepdims=True)


def _grad_x(da, dc, wt, nrm, r, dh, norm_g, small, grads):
    S = nrm.shape[0]
    n_steps = S // TM
    rs = _ReduceScatter(grads)
    n_rs_out = len(rs.out_shape())
    small_rows = 8 + small.shape[0]

    def body(da_ref, dc_ref, wt_ref, n_ref, r_ref, dh_ref, g_ref, small_ref, *rest):
        grad_refs, rest = rest[:rs.n], rest[rs.n:]
        gx_ref, all_ref = rest[:2]
        rs_out, rest = rest[2:2 + n_rs_out], rest[2 + n_rs_out:]
        gng, stage, small_send, small_recv, small_own = rest[:5]
        rs_scratch = rest[5:]
        step = pl.program_id(0)
        finish = rs.emit(step, n_steps, grad_refs, rs_out, rs_scratch)

        @pl.when(step == 0)
        def _():
            gng[...] = jnp.zeros_like(gng)

        dxn = (_nn(da_ref[:, 0:512], wt_ref[0:512, :]) + _nn(da_ref[:, 512:1024], wt_ref[768:1280, :])
               + _nn(da_ref[:, 1024:1280], wt_ref[512:768, :]) + _nn(dc_ref[...], wt_ref[1280:3328, :]))
        n = n_ref[...]
        r = r_ref[...]
        gng[...] += jnp.sum(dxn * n, axis=0, keepdims=True)
        dxg = dxn * g_ref[...]
        gx_ref[...] = dh_ref[...] + r * (dxg - n * jnp.mean(dxg * n, axis=-1, keepdims=True))

        @pl.when(step == n_steps - 1)
        def _():
            x_, y_, c_ = lax.axis_index("x"), lax.axis_index("y"), lax.axis_index("c")
            me = 4 * x_ + 2 * y_ + c_
            for q in range(8):
                stage[q:q + 1, :] = gng[:, q * 128:(q + 1) * 128]
            stage[8:small_rows, :] = small_ref[...]
            own = pltpu.make_async_copy(stage, all_ref.at[me], small_own)
            own.start()
            sends = []
            for k in range(1, N_DEV):
                cp = pltpu.make_async_remote_copy(
                    src_ref=stage, dst_ref=all_ref.at[me],
                    send_sem=small_send.at[k - 1], recv_sem=small_recv.at[k - 1],
                    device_id=(x_ ^ (k >> 2), y_ ^ ((k >> 1) & 1), c_ ^ (k & 1)), device_id_type=MESH)
                cp.start()
                sends.append(cp)
            for cp in sends:
                cp.wait_send()
                cp.wait_recv()
            own.wait()

        finish()

    row = lambda i: (i, 0)
    fixed = lambda i: (0, 0)
    any_spec = pl.BlockSpec(memory_space=pl.ANY)
    outs = pl.pallas_call(
        body,
        name="grad_x_reduce_scatter",
        grid=(n_steps,),
        in_specs=[
            pl.BlockSpec((TM, PA_W), row),
            pl.BlockSpec((TM, PC_W), row),
            pl.BlockSpec((IN_W, D_MODEL), fixed),
            pl.BlockSpec((TM, D_MODEL), row),
            pl.BlockSpec((TM, 1), row),
            pl.BlockSpec((TM, D_MODEL), row),
            pl.BlockSpec((1, D_MODEL), fixed),
            pl.BlockSpec(small.shape, fixed),
        ] + [any_spec] * rs.n,
        out_specs=[pl.BlockSpec((TM, D_MODEL), row), any_spec] + [any_spec] * n_rs_out,
        out_shape=[jax.ShapeDtypeStruct((S, D_MODEL), F32),
                   jax.ShapeDtypeStruct((N_DEV, small_rows, 128), F32)] + rs.out_shape(),
        scratch_shapes=[
            pltpu.VMEM((1, D_MODEL), F32),
            pltpu.VMEM((small_rows, 128), F32),
            pltpu.SemaphoreType.DMA((N_DEV - 1,)),
            pltpu.SemaphoreType.DMA((N_DEV - 1,)),
            pltpu.SemaphoreType.DMA,
        ] + rs.scratch_shapes(),
        compiler_params=_params(("arbitrary",)),
    )(da, dc, wt, nrm, r, dh, norm_g, small, *grads)
    return outs[0], outs[1], outs[2:2 + rs.n], outs[2 + rs.n:2 + 2 * rs.n]


def _grad_w_in(da, pc, dmix, conv_w, xn):
    S = xn.shape[0]
    nt = S // TM
    t16 = TM // HALO

    def body(da_ref, pc_ref, prev_ref, next_ref, dm_ref, dmn_ref, cw_ref, xn_ref, gw_ref, dc_ref, gcw_ref):
        i = pl.program_id(0)

        @pl.when(i == 0)
        def _():
            gw_ref[...] = jnp.zeros_like(gw_ref)
            gcw_ref[...] = jnp.zeros_like(gcw_ref)

        xn = xn_ref[...]
        gw_ref[0:512, :] += _tn(da_ref[:, 0:512], xn)
        gw_ref[768:1280, :] += _tn(da_ref[:, 512:1024], xn)
        gw_ref[512:768, :] += _tn(da_ref[:, 1024:1280], xn)
        def piece_grad(k):
            rows = slice(PA_W + k * CONV_W, PA_W + (k + 1) * CONV_W)
            gw_ref[rows, :] += _tn(dc_ref[:, k * CONV_W:(k + 1) * CONV_W], xn)

        _conv_bwd_tile(pc_ref, prev_ref, next_ref, dm_ref, dmn_ref, cw_ref, dc_ref, gcw_ref, i > 0, i < nt - 1,
                       piece_grad)

    row = lambda i: (i, 0)
    fixed = lambda i: (0, 0)
    nxt = lambda i: jnp.minimum((i + 1) * t16, nt * t16 - 1)
    return pl.pallas_call(
        body,
        name="grad_w_in",
        grid=(nt,),
        in_specs=[
            pl.BlockSpec((TM, PA_W), row),
            pl.BlockSpec((TM, PC_W), row),
            _prev_rows(PC_W),
            pl.BlockSpec((HALO, PC_W), lambda i: (nxt(i), 0)),
            pl.BlockSpec((TM, CONV_W), lambda i: (i, 1)),
            pl.BlockSpec((HALO, CONV_W), lambda i: (nxt(i), 1)),
            pl.BlockSpec((CONV_K, CONV_W), fixed),
            pl.BlockSpec((TM, D_MODEL), row),
        ],
        out_specs=[
            pl.BlockSpec((IN_W, D_MODEL), fixed),
            pl.BlockSpec((TM, PC_W), row),
            pl.BlockSpec((CONV_K, CONV_W), fixed),
        ],
        out_shape=[
            jax.ShapeDtypeStruct((IN_W, D_MODEL), F32),
            jax.ShapeDtypeStruct((S, PC_W), BF16),
            jax.ShapeDtypeStruct((CONV_K, CONV_W), F32),
        ],
        compiler_params=_params(("arbitrary",)),
    )(da, pc, pc, pc, dmix, dmix, conv_w, xn)


def _adam_update(w, g, m, v):
    c1 = 1.0 - ADAM_B1 ** ADAM_STEP
    c2 = 1.0 - ADAM_B2 ** ADAM_STEP
    nm = ADAM_B1 * m + (1.0 - ADAM_B1) * g
    nv = ADAM_B2 * v + (1.0 - ADAM_B2) * (g * g)
    return -ADAM_LR * ((nm / c1) / (jnp.sqrt(nv / c2) + ADAM_EPS) + ADAM_WD * w), nm, nv


def _sum_chips_adamw(own, others, w, m, v, name):
    def body(own_ref, p_ref, w_ref, m_ref, v_ref, g_ref, d_ref, nm_ref, nv_ref):
        g = own_ref[...]
        for k in range(N_CHIP - 1):
            g = g + p_ref[k].astype(F32)
        g_ref[...] = g
        d_ref[...], nm_ref[...], nv_ref[...] = _adam_update(w_ref[...], g, m_ref[...], v_ref[...])

    rows, cols = w.shape
    half = rows // 2
    blk = pl.BlockSpec((half, cols), lambda i: (i, 0))
    shape = jax.ShapeDtypeStruct(w.shape, F32)
    return pl.pallas_call(
        body,
        name=name,
        grid=(2,),
        in_specs=[blk, pl.BlockSpec((N_CHIP - 1, half, cols), lambda i: (0, i, 0)), blk, blk, blk],
        out_specs=[blk] * 4,
        out_shape=[shape] * 4,
        compiler_params=_params(("arbitrary",)),
    )(own, others, w, m, v)


SMALL_ROWS = 96


def _small_adamw(parts, params):
    def body(parts_ref, *rest):
        prm, outs, total = rest[:12], rest[12:29], rest[29]
        me = 4 * lax.axis_index("x") + 2 * lax.axis_index("y") + lax.axis_index("c")
        acc = parts_ref[0]
        for d in range(1, N_DEV):
            acc = acc + parts_ref[d]
        total[...] = acc
        grads = (total[0:8, :], total[8:16, :], total[16:17, 0:8],
                 total[pl.ds(pl.multiple_of(32 + me * 8, 8), CONV_K), 0:64])
        outs[0][...] = total[24:25, 0:1]
        for k, g in enumerate(grads):
            w_ref, m_ref, v_ref = prm[3 * k:3 * k + 3]
            g_ref, d_ref, nm_ref, nv_ref = outs[1 + 4 * k:5 + 4 * k]
            g_ref[...] = g
            d_ref[...], nm_ref[...], nv_ref[...] = _adam_update(w_ref[...], g, m_ref[...], v_ref[...])

    flat = [a for p in params for a in p]
    out_shape = [jax.ShapeDtypeStruct((1, 1), F32)]
    for p in params:
        out_shape += [jax.ShapeDtypeStruct(p[0].shape, F32)] * 4
    return pl.pallas_call(
        body,
        name="adamw_small",
        out_shape=out_shape,
        scratch_shapes=[pltpu.VMEM((SMALL_ROWS, 128), F32)],
        compiler_params=_params(),
    )(parts, *flat)


def kernel(x, norm_g, w_in, sinks, conv_w, w_out, final_g, loss_target, m_norm_g, m_w_in, m_sinks, m_conv_w, m_w_out, m_final_g, v_norm_g, v_w_in, v_sinks, v_conv_w, v_w_out, v_final_g):
    S = x.shape[1]
    x2 = x.reshape(S, D_MODEL)
    t2 = loss_target.reshape(S, D_MODEL)
    ng = norm_g.reshape(1, D_MODEL)
    fg = final_g.reshape(1, D_MODEL)

    cw_pad = jnp.zeros((8, 128), F32).at[0:CONV_K, 0:64].set(conv_w)
    xn, nrm, rinv, tab, wt = _prologue(x2, ng, w_in.T.astype(BF16))
    pa, pc, (wo, cw_all) = _fwd_proj(xn, wt, [w_out.astype(BF16), cw_pad])
    cw = cw_all.reshape(N_DEV, 8, 128)[:, 0:CONV_K, 0:64].transpose(1, 0, 2).reshape(CONV_K, CONV_W)
    ya, att, probs, psinks, q_stack = _attn_fwd(pa, tab, sinks)
    dh, dmix, g_wo, g_fg, loss_part = _out_loss(x2, t2, ya, pc, cw, wo, fg)
    da, g_sinks = _attn_bwd(pa, dmix, att, probs, psinks, q_stack, tab)
    g_wt, dc, g_cw = _grad_w_in(da, pc, dmix, cw, xn)
    cw_pack = jnp.pad(g_cw.reshape(CONV_K, N_DEV, 64).transpose(1, 0, 2),
                      ((0, 0), (0, 8 - CONV_K), (0, 64))).reshape(N_DEV * 8, 128)
    small = jnp.concatenate([g_fg.reshape(8, 128), g_sinks, loss_part, cw_pack], axis=0)
    grad_x, parts, own, others = _grad_x(
        da, dc, wt, nrm, rinv, dh, ng, small,
        [g_wt.reshape(N_DEV, SHARD_IN, D_MODEL), g_wo.reshape(N_DEV, SHARD_OUT, D_MODEL)])
    gt, dt, nmt, nvt = _sum_chips_adamw(own[0], others[0], w_in.T, m_w_in.T, v_w_in.T, "adamw_w_in")
    grad_w_in, d_w_in, nm_w_in, nv_w_in = gt.T, dt.T, nmt.T, nvt.T
    grad_w_out, d_w_out, nm_w_out, nv_w_out = _sum_chips_adamw(
        own[1], others[1], w_out, m_w_out, v_w_out, "adamw_w_out")
    vec = lambda a: a.reshape(8, 128)
    row = lambda a: a.reshape(1, 8)
    res = _small_adamw(parts, [
        (vec(norm_g), vec(m_norm_g), vec(v_norm_g)), (vec(final_g), vec(m_final_g), vec(v_final_g)),
        (row(sinks), row(m_sinks), row(v_sinks)), (conv_w, m_conv_w, v_conv_w)])
    loss = res[0].reshape(())
    grad_norm_g, d_ng, nm_ng, nv_ng = [a.reshape(D_MODEL) for a in res[1:5]]
    grad_final_g, d_fg, nm_fg, nv_fg = [a.reshape(D_MODEL) for a in res[5:9]]
    grad_sinks, d_sk, nm_sk, nv_sk = [a.reshape(N_Q_HEADS) for a in res[9:13]]
    grad_conv_w, d_cw, nm_cw, nv_cw = res[13:17]

    return (loss, grad_x.reshape(1, S, D_MODEL), grad_norm_g, grad_w_in, grad_sinks, grad_conv_w, grad_w_out, grad_final_g,
            d_ng, d_w_in, d_sk, d_cw, d_w_out, d_fg,
            nm_ng, nm_w_in, nm_sk, nm_cw, nm_w_out, nm_fg,
            nv_ng, nv_w_in, nv_sk, nv_cw, nv_w_out, nv_fg)
```

```python
import numpy as np
import jax
import jax.numpy as jnp
from jax import lax
from jax.experimental import pallas as pl
from jax.experimental.pallas import tpu as pltpu

F32 = jnp.float32
BF16 = jnp.bfloat16
MESH = pl.DeviceIdType.MESH

D_MODEL = 1024
HEAD_DIM = 64
N_Q_HEADS = 8
GROUP = 4
ATTN_W = 512
KV_W = 128
BLK = 128
CONV_W = 512
CONV_K = 3
IN_W = 3328
PA_W = 1280
PC_W = 2048
EPS = 1e-5
ROPE_THETA = 500000.0
ROT_DIM = 16
N_DEV = 8
N_CHIP = 4
SHARD_IN = IN_W // N_DEV
SHARD_OUT = D_MODEL // N_DEV

ADAM_LR = 0.001
ADAM_B1 = 0.9
ADAM_B2 = 0.999
ADAM_EPS = 1e-08
ADAM_WD = 0.01
ADAM_STEP = 10

ACT = jnp.bfloat16

TM = 512
TQ = 1024
HALO = 16
VMEM_LIMIT = 56 * 1024 * 1024

NT_DIMS = (((1,), (1,)), ((), ()))
TN_DIMS = (((0,), (0,)), ((), ()))


def _params(sem=None):
    kw = dict(vmem_limit_bytes=VMEM_LIMIT)
    if sem is not None:
        kw["dimension_semantics"] = sem
    return pltpu.CompilerParams(**kw)


def _nt(a, b):
    return lax.dot_general(a, b, NT_DIMS, preferred_element_type=F32)


def _tn(a, b):
    return lax.dot_general(a, b, TN_DIMS, preferred_element_type=F32)


def _nn(a, b):
    return jnp.dot(a, b, preferred_element_type=F32)


def _silu(g):
    return g * jax.nn.sigmoid(g)


def _silu_and_grad(g):
    s = jax.nn.sigmoid(g)
    return g * s, s * (1.0 + g * (1.0 - s))


class _AllGatherInSteps:
    def __init__(self, arrs, forward_step):
        self.blocks = [(a.shape, a.dtype) for a in arrs]
        self.n = len(arrs)
        self.forward_step = forward_step

    def out_shape(self):
        return [jax.ShapeDtypeStruct((N_DEV * s[0], s[1]), d) for s, d in self.blocks]

    def scratch_shapes(self):
        return [pltpu.SemaphoreType.DMA((7 * self.n,)), pltpu.SemaphoreType.DMA((7 * self.n,)),
                pltpu.SemaphoreType.DMA((self.n,))]

    def emit(self, step, n_steps, x_refs, out_refs, scratch):
        assert n_steps > self.forward_step + 1
        send_sems, recv_sems, local_sems = scratch
        x, y, c = lax.axis_index("x"), lax.axis_index("y"), lax.axis_index("c")
        me, sibling = (x, y, c), (x, y, 1 - c)
        chips = [(1 - x, y), (x, 1 - y), (1 - x, 1 - y)]

        def rows(a, px, py, pc):
            m = self.blocks[a][0][0]
            return out_refs[a].at[pl.ds((4 * px + 2 * py + pc) * m, m), :]

        def copy(a, k, block, to, src=None):
            return pltpu.make_async_remote_copy(
                src_ref=rows(a, *block) if src is None else src, dst_ref=rows(a, *block),
                send_sem=send_sems.at[a * 7 + k], recv_sem=recv_sems.at[a * 7 + k],
                device_id=to, device_id_type=MESH)

        def mine(a):
            return pltpu.make_async_copy(x_refs[a], rows(a, *me), local_sems.at[a])

        def first(a):
            return ([copy(a, 0, me, sibling, src=x_refs[a])]
                    + [copy(a, 1 + j, me, (*chip, c), src=x_refs[a]) for j, chip in enumerate(chips)])

        def passed(a):
            return [copy(a, 4 + j, (*chip, c), sibling) for j, chip in enumerate(chips)]

        @pl.when(step == 0)
        def _():
            for a in range(self.n):
                mine(a).start()
                for cp in first(a):
                    cp.start()

        @pl.when(step == self.forward_step)
        def _():
            for j, chip in enumerate(chips):
                for a in range(self.n):
                    copy(a, 1 + j, (*chip, c), me).wait_recv()
                    copy(a, 4 + j, (*chip, c), sibling).start()

        def finish():
            @pl.when(step == n_steps - 1)
            def _():
                for a in range(self.n):
                    copy(a, 0, sibling, me).wait_recv()
                    for j, chip in enumerate(chips):
                        copy(a, 4 + j, (*chip, 1 - c), me).wait_recv()
                    for cp in first(a) + passed(a):
                        cp.wait_send()
                    mine(a).wait()

        return finish


class _AllGatherViaNeighbours:
    def __init__(self, arr, first, second):
        (self.m, self.ncol), self.dtype = arr.shape, arr.dtype
        assert self.m % 32 == 0
        self.first, self.second = first, second

    def out_shape(self):
        return [jax.ShapeDtypeStruct((N_DEV * self.m, self.ncol), self.dtype)]

    def scratch_shapes(self):
        return [pltpu.SemaphoreType.DMA((9,)), pltpu.SemaphoreType.DMA((9,)), pltpu.SemaphoreType.DMA]

    def emit(self, step, n_steps, x_ref, out_ref, scratch):
        assert 0 < self.first < self.second < n_steps - 1
        send_sems, recv_sems, local_sem = scratch
        x, y, c = lax.axis_index("x"), lax.axis_index("y"), lax.axis_index("c")
        half = self.m // 2
        sibling, xn, yn = (x, y, 1 - c), (1 - x, y, c), (x, 1 - y, c)

        def rows(dev, part=None):
            px, py, pc = dev
            base = (4 * px + 2 * py + pc) * self.m
            if part is None:
                return out_ref.at[pl.ds(base, self.m), :]
            return out_ref.at[pl.ds(base + part * half, half), :]

        def copy(k, dev, to, part=None, src=None):
            return pltpu.make_async_remote_copy(
                src_ref=rows(dev, part) if src is None else src, dst_ref=rows(dev, part),
                send_sem=send_sems.at[k], recv_sem=recv_sems.at[k], device_id=to, device_id_type=MESH)

        me, dg = (x, y, c), (1 - x, 1 - y, c)
        mine = pltpu.make_async_copy(x_ref, rows(me), local_sem)
        sends = [
            copy(0, me, sibling, src=x_ref), copy(1, me, xn, src=x_ref), copy(2, me, yn, src=x_ref),
            copy(3, xn, yn, part=0), copy(4, yn, xn, part=1),
            copy(5, xn, sibling), copy(6, yn, sibling), copy(7, dg, sibling, part=0), copy(8, dg, sibling, part=1),
        ]
        other = lambda dev: (dev[0], dev[1], 1 - c)
        arrivals = [
            copy(0, other(me), sibling), copy(1, xn, xn), copy(2, yn, yn), copy(3, dg, yn, part=0),
            copy(4, dg, xn, part=1), copy(5, other(xn), sibling), copy(6, other(yn), sibling),
            copy(7, other(dg), sibling, part=0), copy(8, other(dg), sibling, part=1),
        ]

        @pl.when(step == 0)
        def _():
            mine.start()
            for k in (0, 1, 2):
                sends[k].start()

        @pl.when(step == self.first)
        def _():
            arrivals[1].wait_recv()
            sends[3].start()
            sends[5].start()
            arrivals[2].wait_recv()
            sends[4].start()
            sends[6].start()

        @pl.when(step == self.second)
        def _():
            arrivals[3].wait_recv()
            sends[7].start()
            arrivals[4].wait_recv()
            sends[8].start()

        def finish():
            @pl.when(step == n_steps - 1)
            def _():
                for k in (0, 5, 6, 7, 8):
                    arrivals[k].wait_recv()
                for cp in sends:
                    cp.wait_send()
                mine.wait()

        return finish


class _ReduceScatter:
    def __init__(self, grads):
        self.shapes = [g.shape[1:] for g in grads]
        self.n = len(grads)
        self.items = tuple((a, r) for r in (1, 2, 3, 0) for a in range(self.n))
        self.steps = len(self.items) + 2

    def out_shape(self):
        own = [jax.ShapeDtypeStruct(s, F32) for s in self.shapes]
        ici = [jax.ShapeDtypeStruct((N_CHIP - 1,) + s, BF16) for s in self.shapes]
        land = [jax.ShapeDtypeStruct((N_CHIP,) + s, F32) for s in self.shapes]
        return own + ici + land

    def scratch_shapes(self):
        n_items = len(self.items)
        return ([pltpu.VMEM((2,) + s, F32) for s in self.shapes]
                + [pltpu.VMEM((N_CHIP - 1,) + s, BF16) for s in self.shapes]
                + [pltpu.VMEM(s, F32) for s in self.shapes]
                + [pltpu.SemaphoreType.DMA((self.n * N_CHIP,))] * 2
                + [pltpu.SemaphoreType.DMA((2 * n_items,))]
                + [pltpu.SemaphoreType.DMA((self.n * (N_CHIP - 1),))] * 2
                + [pltpu.SemaphoreType.DMA((self.n,))])

    def emit(self, step, n_steps, g_refs, out_refs, scratch):
        assert n_steps > self.steps
        n = self.n
        own_refs, ici_refs, land_refs = out_refs[:n], out_refs[n:2 * n], out_refs[2 * n:]
        stage, pair_bf, pair_own = scratch[:n], scratch[n:2 * n], scratch[2 * n:3 * n]
        sib_send, sib_recv, load_sems, ici_send, ici_recv, own_sems = scratch[3 * n:]
        x, y, c = lax.axis_index("x"), lax.axis_index("y"), lax.axis_index("c")

        def chip_of(r):
            return (x ^ (r >> 1), y ^ (r & 1))

        def block_of(r, core):
            cx, cy = chip_of(r)
            return 4 * cx + 2 * cy + core

        def to_sibling(a, r):
            return pltpu.make_async_remote_copy(
                src_ref=g_refs[a].at[block_of(r, 1 - c)], dst_ref=land_refs[a].at[r],
                send_sem=sib_send.at[a * N_CHIP + r], recv_sem=sib_recv.at[a * N_CHIP + r],
                device_id=(x, y, 1 - c), device_id_type=MESH)

        def loads(k):
            a, r = self.items[k]
            return (pltpu.make_async_copy(g_refs[a].at[block_of(r, c)], stage[a].at[0], load_sems.at[2 * k]),
                    pltpu.make_async_copy(land_refs[a].at[r], stage[a].at[1], load_sems.at[2 * k + 1]))

        def to_owner(k):
            a, r = self.items[k]
            if r == 0:
                return pltpu.make_async_copy(pair_own[a], own_refs[a], own_sems.at[a])
            return pltpu.make_async_remote_copy(
                src_ref=pair_bf[a].at[r - 1], dst_ref=ici_refs[a].at[r - 1],
                send_sem=ici_send.at[a * (N_CHIP - 1) + r - 1], recv_sem=ici_recv.at[a * (N_CHIP - 1) + r - 1],
                device_id=(*chip_of(r), c), device_id_type=MESH)

        @pl.when(step == 0)
        def _():
            for a, r in self.items:
                to_sibling(a, r).start()

        for k, (a, r) in enumerate(self.items):
            @pl.when(step == 1 + k)
            def _(k=k, a=a, r=r):
                to_sibling(a, r).wait_recv()
                for cp in loads(k):
                    cp.start()

            @pl.when(step == 2 + k)
            def _(k=k, a=a, r=r):
                for cp in loads(k):
                    cp.wait()
                total = stage[a][0] + stage[a][1]
                if r == 0:
                    pair_own[a][...] = total
                else:
                    pair_bf[a][r - 1] = total.astype(BF16)
                to_owner(k).start()

        def finish():
            @pl.when(step == n_steps - 1)
            def _():
                for k, (a, r) in enumerate(self.items):
                    if r == 0:
                        to_owner(k).wait()
                    else:
                        to_owner(k).wait_send()
                        to_owner(k).wait_recv()
                for a, r in self.items:
                    to_sibling(a, r).wait_send()

        return finish


def _prologue(x, norm_g, w_shard):
    S = x.shape[0]
    n_steps = S // TM
    half = ROT_DIM // 2
    pos = jnp.arange(S, dtype=jnp.int32).astype(F32)
    inv_freq = ROPE_THETA ** (-jnp.arange(0, ROT_DIM, 2, dtype=F32) / ROT_DIM)
    ang = inv_freq[:, None] * pos[None, :]
    cs = jnp.concatenate([jnp.cos(ang), jnp.sin(ang)], axis=0)
    ag = _AllGatherViaNeighbours(w_shard, first=n_steps // 2 - 2, second=n_steps - 4)

    def body(x_ref, g_ref, cs_ref, w_ref, xn_ref, tab_ref, wt_ref, *ag_scratch):
        step = pl.program_id(0)
        finish = ag.emit(step, n_steps, w_ref, wt_ref, ag_scratch)
        xv = x_ref[...]
        r = lax.rsqrt(jnp.mean(xv * xv, axis=-1, keepdims=True) + EPS)
        xn_ref[...] = (xv * r * g_ref[...]).astype(BF16)

        xt = jnp.concatenate([cs_ref[...], jnp.zeros((128 - 2 * half, TM), F32)], axis=0).T
        lane = lax.broadcasted_iota(jnp.int32, (TM, 128), 1)
        rr = lane & (HEAD_DIM - 1)
        first = lane < HEAD_DIM

        def at(shift_first, shift_second):
            return jnp.where(first, pltpu.roll(xt, shift_first, 1) if shift_first else xt,
                             pltpu.roll(xt, shift_second, 1))

        cos_lo, cos_hi = at(0, HEAD_DIM), at(half, HEAD_DIM + half)
        sin_lo, sin_hi = at(128 - half, HEAD_DIM - half), at(0, HEAD_DIM)
        tab_ref[:, 0:128] = jnp.where(rr < half, cos_lo, jnp.where(rr < ROT_DIM, cos_hi, 1.0))
        tab_ref[:, 128:256] = jnp.where(rr < half, -sin_lo, 0.0)
        tab_ref[:, 256:384] = jnp.where((rr >= half) & (rr < ROT_DIM), sin_hi, 0.0)
        finish()

    any_spec = pl.BlockSpec(memory_space=pl.ANY)
    return pl.pallas_call(
        body,
        name="prologue_all_gather_w_in",
        grid=(n_steps,),
        in_specs=[
            pl.BlockSpec((TM, D_MODEL), lambda i: (i, 0)),
            pl.BlockSpec((1, D_MODEL), lambda i: (0, 0)),
            pl.BlockSpec((2 * half, TM), lambda i: (0, i)),
            any_spec,
        ],
        out_specs=[
            pl.BlockSpec((TM, D_MODEL), lambda i: (i, 0)),
            pl.BlockSpec((TM, 384), lambda i: (i, 0)),
            any_spec,
        ],
        out_shape=[
            jax.ShapeDtypeStruct((S, D_MODEL), BF16),
            jax.ShapeDtypeStruct((S, 384), F32),
        ] + ag.out_shape(),
        scratch_shapes=ag.scratch_shapes(),
        compiler_params=_params(("arbitrary",)),
    )(x, norm_g, cs, w_shard)


def _fwd_proj(xn, wt, later):
    S = xn.shape[0]
    tm = 2 * TM
    n_steps = S // tm
    ag = _AllGatherInSteps(later, forward_step=n_steps // 2)

    def body(xn_ref, wt_ref, *rest):
        later_refs, rest = rest[:ag.n], rest[ag.n:]
        pa_ref, pc_ref = rest[:2]
        gathered, ag_scratch = rest[2:2 + ag.n], rest[2 + ag.n:]
        step = pl.program_id(0)
        finish = ag.emit(step, n_steps, later_refs, gathered, ag_scratch)
        xn = xn_ref[...]
        pa_ref[:, 0:512] = _nt(xn, wt_ref[0:512, :]).astype(ACT)
        pa_ref[:, 512:1024] = _nt(xn, wt_ref[768:1280, :]).astype(ACT)
        pa_ref[:, 1024:1280] = _nt(xn, wt_ref[512:768, :]).astype(ACT)
        pc_ref[...] = _nt(xn, wt_ref[1280:3328, :]).astype(ACT)
        finish()

    any_spec = pl.BlockSpec(memory_space=pl.ANY)
    outs = pl.pallas_call(
        body,
        name="fwd_proj_all_gather",
        grid=(n_steps,),
        in_specs=[
            pl.BlockSpec((tm, D_MODEL), lambda i: (i, 0)),
            pl.BlockSpec((IN_W, D_MODEL), lambda i: (0, 0)),
        ] + [any_spec] * ag.n,
        out_specs=[
            pl.BlockSpec((tm, PA_W), lambda i: (i, 0)),
            pl.BlockSpec((tm, PC_W), lambda i: (i, 0)),
        ] + [any_spec] * ag.n,
        out_shape=[
            jax.ShapeDtypeStruct((S, PA_W), ACT),
            jax.ShapeDtypeStruct((S, PC_W), ACT),
        ] + ag.out_shape(),
        scratch_shapes=ag.scratch_shapes(),
        compiler_params=_params(("arbitrary",)),
    )(xn, wt, *later)
    return outs[0], outs[1], outs[2:]


def _rope(t, tab):
    return (t * tab[:, 0:128] + pltpu.roll(t, 120, 1) * tab[:, 128:256]
            + pltpu.roll(t, 8, 1) * tab[:, 256:384])


def _rope_t(d, tab):
    return (d * tab[:, 0:128] + pltpu.roll(d * tab[:, 128:256], 8, 1)
            + pltpu.roll(d * tab[:, 256:384], 120, 1))


def _fill_kv(kall, kvc_ref, kvp_ref, tabc_ref, tabp_ref):
    for lo, kv_ref, tab_ref, n in ((0, kvp_ref, tabp_ref, BLK), (BLK, kvc_ref, tabc_ref, TQ)):
        k = _rope(kv_ref[:, 0:128].astype(F32), tab_ref[...])
        v = kv_ref[:, 128:256].astype(F32)
        kall[0, lo:lo + n, :] = k.astype(BF16)
        kall[1, lo:lo + n, :] = pltpu.roll(k, 64, 1).astype(BF16)
        kall[2, lo:lo + n, :] = v.astype(BF16)
        kall[3, lo:lo + n, :] = pltpu.roll(v, 64, 1).astype(BF16)


HEADS = (((0, 0), (1, 0), (2, 1), (3, 1)), ((0, 1), (1, 1), (2, 0), (3, 0)))


def _upper():
    kj = lax.broadcasted_iota(jnp.int32, (BLK, 4 * BLK), 0)
    qi = lax.broadcasted_iota(jnp.int32, (BLK, 4 * BLK), 1) & (BLK - 1)
    return kj > qi


def _merge(upper, both):
    return jnp.where(upper, both[0:BLK, :], both[BLK:2 * BLK, :])


def _split_store(ref, s, upper_b, vb):
    first = vb * upper_b
    ref[s, 0:BLK, :] = first
    ref[s, BLK:2 * BLK, :] = vb - first


def _sink_rows(sink_ref):
    return [jnp.concatenate([jnp.full((1, BLK), sink_ref[2 * p + e], F32) for p, e in HEADS[s]], axis=1)
            for s in range(2)]


def _stack_heads(ref, slot, half, pairs, s=None):
    for a, (p, e) in enumerate(HEADS[slot if s is None else s]):
        ref[slot, a * BLK:(a + 1) * BLK, :] = jnp.where(half[e], pairs[p], 0.0).astype(BF16)


def _unstack_pair(half, outs, p):
    lo = 0 if p < 2 else 1
    rows = slice(p * BLK, (p + 1) * BLK)
    return jnp.where(half[0], outs[lo][rows, :], outs[1 - lo][rows, :])


def _softmax(sm, sinks):
    m = jnp.maximum(jnp.max(sm, axis=0, keepdims=True), sinks)
    p = jnp.exp(sm - m)
    es = jnp.exp(sinks - m)
    inv = 1.0 / (jnp.sum(p, axis=0, keepdims=True) + es)
    return p * inv, es * inv


def _scores(kk, q_stack, first):
    st = _nt(kk, q_stack)
    prev = st[0:BLK, :]
    if first is not None:
        prev = prev + jnp.where(first, -jnp.inf, 0.0)
    return prev, st[BLK:2 * BLK, :]


def _attn_specs(tile):
    nb = TQ // BLK
    prev = lambda i: jnp.maximum(tile(i) * nb - 1, 0)
    return [
        pl.BlockSpec(memory_space=pltpu.SMEM),
        pl.BlockSpec((TQ, ATTN_W), lambda i: (tile(i), 0)),
        pl.BlockSpec((TQ, ATTN_W), lambda i: (tile(i), 1)),
        pl.BlockSpec((TQ, 2 * KV_W), lambda i: (tile(i), 4)),
        pl.BlockSpec((BLK, 2 * KV_W), lambda i: (prev(i), 4)),
        pl.BlockSpec((TQ, 384), lambda i: (tile(i), 0)),
        pl.BlockSpec((BLK, 384), lambda i: (prev(i), 0)),
    ]


def _attn_fwd(pa, tab, sinks):
    S = pa.shape[0]
    nb = TQ // BLK

    def body(sink_ref, q_ref, g_ref, kvc_ref, kvp_ref, tabc_ref, tabp_ref, o_ref, att_ref, pm_ref, ps_ref,
             qs_ref, kall, p_sc):
        i = pl.program_id(0)
        _fill_kv(kall, kvc_ref, kvp_ref, tabc_ref, tabp_ref)
        lane = lax.broadcasted_iota(jnp.int32, (BLK, 128), 1)
        half = [lane < HEAD_DIM, lane >= HEAD_DIM]
        upper = _upper()
        upper_b = upper.astype(BF16)
        sinks = _sink_rows(sink_ref)
        for j in range(nb):
            rq = slice(j * BLK, (j + 1) * BLK)
            rk = slice(j * BLK, (j + 2) * BLK)
            tab = tabc_ref[rq, :]
            qr = [_rope(q_ref[rq, p * 128:(p + 1) * 128].astype(F32), tab) * 0.125 for p in range(4)]
            outs = []
            for s in range(2):
                _stack_heads(qs_ref, 2 * j + s, half, qr, s)
                prev, cur = _scores(kall[s, rk, :], qs_ref[2 * j + s], i == 0 if j == 0 else None)
                prob, psink = _softmax(jnp.where(upper, prev, cur), sinks[s])
                pb = prob.astype(BF16)
                pm_ref[(2 * j + s) * BLK:(2 * j + s + 1) * BLK, :] = pb
                ps_ref[2 * j + s:2 * j + s + 1, :] = psink
                _split_store(p_sc, s, upper_b, pb)
                outs.append(_tn(p_sc[s], kall[2 + s, rk, :]))
            for p in range(4):
                cols = slice(p * 128, (p + 1) * 128)
                att = _unstack_pair(half, outs, p)
                att_ref[rq, cols] = att.astype(BF16)
                o_ref[rq, cols] = (att * _silu(g_ref[rq, cols].astype(F32))).astype(BF16)

    return pl.pallas_call(
        body,
        name="attn_fwd",
        grid=(S // TQ,),
        in_specs=_attn_specs(lambda i: i),
        out_specs=[pl.BlockSpec((TQ, ATTN_W), lambda i: (i, 0))] * 2 + [
            pl.BlockSpec((2 * TQ, 4 * BLK), lambda i: (i, 0)),
            pl.BlockSpec((2 * nb, 4 * BLK), lambda i: (i, 0)),
            pl.BlockSpec((2 * nb, 4 * BLK, 128), lambda i: (i, 0, 0)),
        ],
        out_shape=[jax.ShapeDtypeStruct((S, ATTN_W), BF16)] * 2 + [
            jax.ShapeDtypeStruct((2 * S, 4 * BLK), BF16),
            jax.ShapeDtypeStruct((2 * S // BLK, 4 * BLK), F32),
            jax.ShapeDtypeStruct((2 * S // BLK, 4 * BLK, 128), BF16),
        ],
        scratch_shapes=[
            pltpu.VMEM((4, BLK + TQ, 128), BF16),
            pltpu.VMEM((2, 2 * BLK, 4 * BLK), BF16),
        ],
        compiler_params=_params(("arbitrary",)),
    )(sinks, pa, pa, pa, pa, tab, tab)


def _shift_down(u, halo_ref, has_prev):
    def halo_u(r):
        hu = halo_ref[r:r + 1, 512:1024].astype(F32) * halo_ref[r:r + 1, 1024:1536].astype(F32)
        return jnp.where(has_prev, hu, 0.0)

    row = lax.broadcasted_iota(jnp.int32, u.shape, 0)
    um1 = jnp.where(row == 0, halo_u(HALO - 1), pltpu.roll(u, 1, 0))
    um2 = jnp.where(row == 0, halo_u(HALO - 2), jnp.where(row == 1, halo_u(HALO - 1), pltpu.roll(u, 2, 0)))
    return um1, um2


def _conv_tile(pc_ref, halo_ref, w_ref, has_prev):
    b = pc_ref[:, 0:512].astype(F32)
    c = pc_ref[:, 512:1024].astype(F32)
    hh = pc_ref[:, 1024:1536].astype(F32)
    gc = pc_ref[:, 1536:2048].astype(F32)
    u = c * hh
    um1, um2 = _shift_down(u, halo_ref, has_prev)
    cv = w_ref[0:1, :] * um2 + w_ref[1:2, :] * um1 + w_ref[2:3, :] * u
    return b, c, hh, gc, u, um1, um2, cv


def _prev_rows(width, col=0):
    return pl.BlockSpec((HALO, width), lambda i: (jnp.maximum(i * (TM // HALO) - 1, 0), col))


def _out_loss(x, target, ya, pc, conv_w, w_out, final_g):
    S = x.shape[0]

    def body(x_ref, t_ref, ya_ref, pc_ref, halo_ref, cw_ref, wo_ref, fg_ref,
             dh_ref, dmix_ref, gwo_ref, gfg_ref, loss_ref):
        @pl.when(pl.program_id(0) == 0)
        def _():
            gwo_ref[...] = jnp.zeros_like(gwo_ref)
            gfg_ref[...] = jnp.zeros_like(gfg_ref)
            loss_ref[...] = jnp.zeros_like(loss_ref)

        b, _, _, gc, _, _, _, cv = _conv_tile(pc_ref, halo_ref, cw_ref, pl.program_id(0) > 0)
        yc = (b * cv * _silu(gc)).astype(BF16)
        mix = jnp.concatenate([ya_ref[...], yc], axis=1)
        wo = wo_ref[...]
        fg = fg_ref[...]
        h = x_ref[...] + _nn(mix, wo)
        r = lax.rsqrt(jnp.mean(h * h, axis=-1, keepdims=True) + EPS)
        n = h * r
        err = n * fg - t_ref[...]
        loss_ref[...] += jnp.broadcast_to(
            0.5 * jnp.sum(jnp.mean(err * err, axis=-1, keepdims=True), axis=0, keepdims=True), (8, 128))
        gfg_ref[...] += jnp.sum(err * n, axis=0, keepdims=True) * (1.0 / D_MODEL)
        dyg = err * (fg * (1.0 / D_MODEL))
        dh = r * (dyg - n * jnp.mean(dyg * n, axis=-1, keepdims=True))
        dh_ref[...] = dh
        dhb = dh.astype(BF16)
        dmix_ref[...] = _nt(dhb, wo).astype(ACT)
        gwo_ref[...] += _tn(mix, dhb)

    row = lambda i: (i, 0)
    fixed = lambda i: (0, 0)
    return pl.pallas_call(
        body,
        name="out_loss",
        grid=(S // TM,),
        in_specs=[
            pl.BlockSpec((TM, D_MODEL), row),
            pl.BlockSpec((TM, D_MODEL), row),
            pl.BlockSpec((TM, ATTN_W), row),
            pl.BlockSpec((TM, PC_W), row),
            _prev_rows(PC_W),
            pl.BlockSpec((CONV_K, CONV_W), fixed),
            pl.BlockSpec((D_MODEL, D_MODEL), fixed),
            pl.BlockSpec((1, D_MODEL), fixed),
        ],
        out_specs=[
            pl.BlockSpec((TM, D_MODEL), row),
            pl.BlockSpec((TM, D_MODEL), row),
            pl.BlockSpec((D_MODEL, D_MODEL), fixed),
            pl.BlockSpec((1, D_MODEL), fixed),
            pl.BlockSpec((8, 128), fixed),
        ],
        out_shape=[
            jax.ShapeDtypeStruct((S, D_MODEL), F32),
            jax.ShapeDtypeStruct((S, D_MODEL), ACT),
            jax.ShapeDtypeStruct((D_MODEL, D_MODEL), F32),
            jax.ShapeDtypeStruct((1, D_MODEL), F32),
            jax.ShapeDtypeStruct((8, 128), F32),
        ],
        compiler_params=_params(("arbitrary",)),
    )(x, target, ya, pc, pc, conv_w, w_out, final_g)


def _attn_bwd(pa, dmix, att, probs, psinks, q_stack, tab):
    S = pa.shape[0]
    nt = S // TQ
    nb = TQ // BLK

    def body(g_ref, kvc_ref, kvp_ref, tabc_ref, tabp_ref, dm_ref, att_ref, pm_ref, ps_ref, qs_ref,
             d_ref, dsink_ref, kall, dkv, carry, do_sc, p_sc, ds_sc, dsink_acc):
        step = pl.program_id(0)

        @pl.when(step == 0)
        def _():
            carry[...] = jnp.zeros_like(carry)
            dsink_acc[...] = jnp.zeros_like(dsink_acc)

        _fill_kv(kall, kvc_ref, kvp_ref, tabc_ref, tabp_ref)
        dkv[0:TQ, :] = jnp.zeros((TQ, 2 * KV_W), F32)
        dkv[TQ:TQ + BLK, :] = carry[...]
        lane = lax.broadcasted_iota(jnp.int32, (BLK, 128), 1)
        half = [lane < HEAD_DIM, lane >= HEAD_DIM]
        upper = _upper()
        upper_b = upper.astype(BF16)
        for j in range(nb):
            rq = slice(j * BLK, (j + 1) * BLK)
            rk = slice(j * BLK, (j + 2) * BLK)
            tab = tabc_ref[rq, :]
            pair = [slice(p * 128, (p + 1) * 128) for p in range(4)]
            g = [g_ref[rq, c].astype(F32) for c in pair]
            da = [dm_ref[rq, c].astype(F32) for c in pair]
            gate = [_silu_and_grad(g[p]) for p in range(4)]
            do = [da[p] * gate[p][0] for p in range(4)]
            dqs, dks, dvs = [], [], []
            for s in range(2):
                kk = kall[s, rk, :]
                vv = kall[2 + s, rk, :]
                _stack_heads(do_sc, s, half, do)
                pb = pm_ref[(2 * j + s) * BLK:(2 * j + s + 1) * BLK, :]
                prob = pb.astype(F32)
                _split_store(p_sc, s, upper_b, pb)
                dprob = _merge(upper, _nt(vv, do_sc[s]))
                dsum = jnp.sum(dprob * prob, axis=0, keepdims=True)
                _split_store(ds_sc, s, upper_b, (prob * (dprob - dsum)).astype(BF16))
                dsink_acc[s, 0:1, :] += ps_ref[2 * j + s:2 * j + s + 1, :] * dsum
                dqs.append(_tn(ds_sc[s], kk))
                dks.append(_nn(ds_sc[s], qs_ref[2 * j + s]))
                dvs.append(_nn(p_sc[s], do_sc[s]))
            for p in range(4):
                d_ref[rq, pair[p]] = _rope_t(_unstack_pair(half, dqs, p) * 0.125, tab).astype(BF16)
                d_ref[rq, 512 + p * 128:512 + (p + 1) * 128] = (
                    da[p] * att_ref[rq, pair[p]].astype(F32) * gate[p][1]).astype(BF16)
            dkv[rk, 0:128] += dks[0] + pltpu.roll(dks[1], 64, 1)
            dkv[rk, 128:256] += dvs[0] + pltpu.roll(dvs[1], 64, 1)
        d_ref[:, 1024:1152] = _rope_t(dkv[BLK:BLK + TQ, 0:128], tabc_ref[...]).astype(BF16)
        d_ref[:, 1152:1280] = dkv[BLK:BLK + TQ, 128:256].astype(BF16)
        carry[...] = dkv[0:BLK, :]

        @pl.when(step == nt - 1)
        def _():
            lanes = lax.broadcasted_iota(jnp.int32, (8, 128), 1)
            out = jnp.zeros((8, 128), F32)
            for s in range(2):
                for a, (p, e) in enumerate(HEADS[s]):
                    tot = jnp.sum(dsink_acc[s, 0:1, a * BLK:(a + 1) * BLK], axis=1, keepdims=True)
                    out = jnp.where(lanes == 2 * p + e, -tot, out)
            dsink_ref[...] = out

    rev = lambda s: nt - 1 - s
    return pl.pallas_call(
        body,
        name="attn_bwd",
        grid=(nt,),
        in_specs=_attn_specs(rev)[2:] + [pl.BlockSpec((TQ, ATTN_W), lambda s: (nt - 1 - s, 0))] * 2 + [
            pl.BlockSpec((2 * TQ, 4 * BLK), lambda s: (nt - 1 - s, 0)),
            pl.BlockSpec((2 * nb, 4 * BLK), lambda s: (nt - 1 - s, 0)),
            pl.BlockSpec((2 * nb, 4 * BLK, 128), lambda s: (nt - 1 - s, 0, 0)),
        ],
        out_specs=[
            pl.BlockSpec((TQ, PA_W), lambda s: (nt - 1 - s, 0)),
            pl.BlockSpec((8, 128), lambda s: (0, 0)),
        ],
        out_shape=[
            jax.ShapeDtypeStruct((S, PA_W), BF16),
            jax.ShapeDtypeStruct((8, 128), F32),
        ],
        scratch_shapes=[
            pltpu.VMEM((4, BLK + TQ, 128), BF16),
            pltpu.VMEM((BLK + TQ, 2 * KV_W), F32),
            pltpu.VMEM((BLK, 2 * KV_W), F32),
            pltpu.VMEM((2, 4 * BLK, 128), BF16),
            pltpu.VMEM((2, 2 * BLK, 4 * BLK), BF16),
            pltpu.VMEM((2, 2 * BLK, 4 * BLK), BF16),
            pltpu.VMEM((2, 8, 4 * BLK), F32),
        ],
        compiler_params=_params(("arbitrary",)),
    )(pa, pa, pa, tab, tab, dmix, att, probs, psinks, q_stack)


def _conv_bwd_tile(pc_ref, prev_ref, next_ref, dm_ref, dmn_ref, w_ref, d_ref, gw_ref, has_prev, has_next,
                   on_piece):
    rows = pc_ref.shape[0]
    w0, w1, w2 = w_ref[0:1, :], w_ref[1:2, :], w_ref[2:3, :]
    b, c, hh, gc, u, um1, um2, cv = _conv_tile(pc_ref, prev_ref, w_ref, has_prev)
    sg, dsg = _silu_and_grad(gc)
    dy = dm_ref[...].astype(F32)
    dyb = dy * b
    dcv = dyb * sg

    def next_dcv(r):
        nd = (dmn_ref[r:r + 1, :].astype(F32) * next_ref[r:r + 1, 0:512].astype(F32)
              * _silu(next_ref[r:r + 1, 1536:2048].astype(F32)))
        return jnp.where(has_next, nd, 0.0)

    row = lax.broadcasted_iota(jnp.int32, (rows, CONV_W), 0)
    dp1 = jnp.where(row == rows - 1, next_dcv(0), pltpu.roll(dcv, rows - 1, 0))
    dp2 = jnp.where(row == rows - 1, next_dcv(1),
                    jnp.where(row == rows - 2, next_dcv(0), pltpu.roll(dcv, rows - 2, 0)))
    du = w2 * dcv + w1 * dp1 + w0 * dp2
    pieces = (lambda: dy * cv * sg, lambda: du * hh, lambda: du * c, lambda: dyb * cv * dsg)
    for k, piece in enumerate(pieces):
        d_ref[:, k * CONV_W:(k + 1) * CONV_W] = piece().astype(BF16)
        on_piece(k)
    gw_ref[0:1, :] += jnp.sum(dcv * um2, axis=0, keepdims=True)
    gw_ref[1:2, :] += jnp.sum(dcv * um1, axis=0, keepdims=True)
    gw_ref[2:3, :] += jnp.sum(dcv * u, axis=0, keepdims=True)


def _grad_x(da, dc, wt, x, dh, norm_g, small, grads):
    S = x.shape[0]
    n_steps = S // TM
    rs = _ReduceScatter(grads)
    n_rs_out = len(rs.out_shape())
    small_rows = 8 + small.shape[0]

    def body(da_ref, dc_ref, wt_ref, x_ref, dh_ref, g_ref, small_ref, *rest):
        grad_refs, rest = rest[:rs.n], rest[rs.n:]
        gx_ref, all_ref = rest[:2]
        rs_out, rest = rest[2:2 + n_rs_out], rest[2 + n_rs_out:]
        gng, stage, small_send, small_recv, small_own = rest[:5]
        rs_scratch = rest[5:]
        step = pl.program_id(0)
        finish = rs.emit(step, n_steps, grad_refs, rs_out, rs_scratch)

        @pl.when(step == 0)
        def _():
            gng[...] = jnp.zeros_like(gng)

        dxn = (_nn(da_ref[:, 0:512], wt_ref[0:512, :]) + _nn(da_ref[:, 512:1024], wt_ref[768:1280, :])
               + _nn(da_ref[:, 1024:1280], wt_ref[512:768, :]) + _nn(dc_ref[...], wt_ref[1280:3328, :]))
        xv = x_ref[...]
        r = lax.rsqrt(jnp.mean(xv * xv, axis=-1, keepdims=True) + EPS)
        n = xv * r
        gng[...] += jnp.sum(dxn * n, axis=0, keepdims=True)
        dxg = dxn * g_ref[...]
        gx_ref[...] = dh_ref[...] + r * (dxg - n * jnp.mean(dxg * n, axis=-1, keepdims=True))

        @pl.when(step == n_steps - 1)
        def _():
            x_, y_, c_ = lax.axis_index("x"), lax.axis_index("y"), lax.axis_index("c")
            me = 4 * x_ + 2 * y_ + c_
            for q in range(8):
                stage[q:q + 1, :] = gng[:, q * 128:(q + 1) * 128]
            stage[8:small_rows, :] = small_ref[...]
            own = pltpu.make_async_copy(stage, all_ref.at[me], small_own)
            own.start()
            sends = []
            for k in range(1, N_DEV):
                cp = pltpu.make_async_remote_copy(
                    src_ref=stage, dst_ref=all_ref.at[me],
                    send_sem=small_send.at[k - 1], recv_sem=small_recv.at[k - 1],
                    device_id=(x_ ^ (k >> 2), y_ ^ ((k >> 1) & 1), c_ ^ (k & 1)), device_id_type=MESH)
                cp.start()
                sends.append(cp)
            for cp in sends:
                cp.wait_send()
                cp.wait_recv()
            own.wait()

        finish()

    row = lambda i: (i, 0)
    fixed = lambda i: (0, 0)
    any_spec = pl.BlockSpec(memory_space=pl.ANY)
    outs = pl.pallas_call(
        body,
        name="grad_x_reduce_scatter",
        grid=(n_steps,),
        in_specs=[
            pl.BlockSpec((TM, PA_W), row),
            pl.BlockSpec((TM, PC_W), row),
            pl.BlockSpec((IN_W, D_MODEL), fixed),
            pl.BlockSpec((TM, D_MODEL), row),
            pl.BlockSpec((TM, D_MODEL), row),
            pl.BlockSpec((1, D_MODEL), fixed),
            pl.BlockSpec(small.shape, fixed),
        ] + [any_spec] * rs.n,
        out_specs=[pl.BlockSpec((TM, D_MODEL), row), any_spec] + [any_spec] * n_rs_out,
        out_shape=[jax.ShapeDtypeStruct((S, D_MODEL), F32),
                   jax.ShapeDtypeStruct((N_DEV, small_rows, 128), F32)] + rs.out_shape(),
        scratch_shapes=[
            pltpu.VMEM((1, D_MODEL), F32),
            pltpu.VMEM((small_rows, 128), F32),
            pltpu.SemaphoreType.DMA((N_DEV - 1,)),
            pltpu.SemaphoreType.DMA((N_DEV - 1,)),
            pltpu.SemaphoreType.DMA,
        ] + rs.scratch_shapes(),
        compiler_params=_params(("arbitrary",)),
    )(da, dc, wt, x, dh, norm_g, small, *grads)
    return outs[0], outs[1], outs[2:2 + rs.n], outs[2 + rs.n:2 + 2 * rs.n]


def _grad_w_in(da, pc, dmix, conv_w, xn):
    S = xn.shape[0]
    nt = S // TM
    t16 = TM // HALO

    def body(da_ref, pc_ref, prev_ref, next_ref, dm_ref, dmn_ref, cw_ref, xn_ref, gw_ref, dc_ref, gcw_ref):
        i = pl.program_id(0)

        @pl.when(i == 0)
        def _():
            gw_ref[...] = jnp.zeros_like(gw_ref)
            gcw_ref[...] = jnp.zeros_like(gcw_ref)

        xn = xn_ref[...]
        gw_ref[0:512, :] += _tn(da_ref[:, 0:512], xn)
        gw_ref[768:1280, :] += _tn(da_ref[:, 512:1024], xn)
        gw_ref[512:768, :] += _tn(da_ref[:, 1024:1280], xn)
        def piece_grad(k):
            rows = slice(PA_W + k * CONV_W, PA_W + (k + 1) * CONV_W)
            gw_ref[rows, :] += _tn(dc_ref[:, k * CONV_W:(k + 1) * CONV_W], xn)

        _conv_bwd_tile(pc_ref, prev_ref, next_ref, dm_ref, dmn_ref, cw_ref, dc_ref, gcw_ref, i > 0, i < nt - 1,
                       piece_grad)

    row = lambda i: (i, 0)
    fixed = lambda i: (0, 0)
    nxt = lambda i: jnp.minimum((i + 1) * t16, nt * t16 - 1)
    return pl.pallas_call(
        body,
        name="grad_w_in",
        grid=(nt,),
        in_specs=[
            pl.BlockSpec((TM, PA_W), row),
            pl.BlockSpec((TM, PC_W), row),
            _prev_rows(PC_W),
            pl.BlockSpec((HALO, PC_W), lambda i: (nxt(i), 0)),
            pl.BlockSpec((TM, CONV_W), lambda i: (i, 1)),
            pl.BlockSpec((HALO, CONV_W), lambda i: (nxt(i), 1)),
            pl.BlockSpec((CONV_K, CONV_W), fixed),
            pl.BlockSpec((TM, D_MODEL), row),
        ],
        out_specs=[
            pl.BlockSpec((IN_W, D_MODEL), fixed),
            pl.BlockSpec((TM, PC_W), row),
            pl.BlockSpec((CONV_K, CONV_W), fixed),
        ],
        out_shape=[
            jax.ShapeDtypeStruct((IN_W, D_MODEL), F32),
            jax.ShapeDtypeStruct((S, PC_W), BF16),
            jax.ShapeDtypeStruct((CONV_K, CONV_W), F32),
        ],
        compiler_params=_params(("arbitrary",)),
    )(da, pc, pc, pc, dmix, dmix, conv_w, xn)


def _adam_update(w, g, m, v):
    c1 = 1.0 - ADAM_B1 ** ADAM_STEP
    c2 = 1.0 - ADAM_B2 ** ADAM_STEP
    nm = ADAM_B1 * m + (1.0 - ADAM_B1) * g
    nv = ADAM_B2 * v + (1.0 - ADAM_B2) * (g * g)
    return -ADAM_LR * ((nm / c1) / (jnp.sqrt(nv / c2) + ADAM_EPS) + ADAM_WD * w), nm, nv


def _sum_chips_adamw(own, others, w, m, v, name):
    def body(own_ref, p_ref, w_ref, m_ref, v_ref, g_ref, d_ref, nm_ref, nv_ref):
        g = own_ref[...]
        for k in range(N_CHIP - 1):
            g = g + p_ref[k].astype(F32)
        g_ref[...] = g
        d_ref[...], nm_ref[...], nv_ref[...] = _adam_update(w_ref[...], g, m_ref[...], v_ref[...])

    rows, cols = w.shape
    half = rows // 2
    blk = pl.BlockSpec((half, cols), lambda i: (i, 0))
    shape = jax.ShapeDtypeStruct(w.shape, F32)
    return pl.pallas_call(
        body,
        name=name,
        grid=(2,),
        in_specs=[blk, pl.BlockSpec((N_CHIP - 1, half, cols), lambda i: (0, i, 0)), blk, blk, blk],
        out_specs=[blk] * 4,
        out_shape=[shape] * 4,
        compiler_params=_params(("arbitrary",)),
    )(own, others, w, m, v)


SMALL_ROWS = 96


def _small_adamw(parts, params):
    def body(parts_ref, *rest):
        prm, outs, total = rest[:12], rest[12:29], rest[29]
        me = 4 * lax.axis_index("x") + 2 * lax.axis_index("y") + lax.axis_index("c")
        acc = parts_ref[0]
        for d in range(1, N_DEV):
            acc = acc + parts_ref[d]
        total[...] = acc
        grads = (total[0:8, :], total[8:16, :], total[16:17, 0:8],
                 total[pl.ds(pl.multiple_of(32 + me * 8, 8), CONV_K), 0:64])
        outs[0][...] = total[24:25, 0:1]
        for k, g in enumerate(grads):
            w_ref, m_ref, v_ref = prm[3 * k:3 * k + 3]
            g_ref, d_ref, nm_ref, nv_ref = outs[1 + 4 * k:5 + 4 * k]
            g_ref[...] = g
            d_ref[...], nm_ref[...], nv_ref[...] = _adam_update(w_ref[...], g, m_ref[...], v_ref[...])

    flat = [a for p in params for a in p]
    out_shape = [jax.ShapeDtypeStruct((1, 1), F32)]
    for p in params:
        out_shape += [jax.ShapeDtypeStruct(p[0].shape, F32)] * 4
    return pl.pallas_call(
        body,
        name="adamw_small",
        out_shape=out_shape,
        scratch_shapes=[pltpu.VMEM((SMALL_ROWS, 128), F32)],
        compiler_params=_params(),
    )(parts, *flat)


def kernel(x, norm_g, w_in, sinks, conv_w, w_out, final_g, loss_target, m_norm_g, m_w_in, m_sinks, m_conv_w, m_w_out, m_final_g, v_norm_g, v_w_in, v_sinks, v_conv_w, v_w_out, v_final_g):
    S = x.shape[1]
    x2 = x.reshape(S, D_MODEL)
    t2 = loss_target.reshape(S, D_MODEL)
    ng = norm_g.reshape(1, D_MODEL)
    fg = final_g.reshape(1, D_MODEL)

    cw_pad = jnp.zeros((8, 128), F32).at[0:CONV_K, 0:64].set(conv_w)
    xn, tab, wt = _prologue(x2, ng, w_in.T.astype(BF16))
    pa, pc, (wo, cw_all) = _fwd_proj(xn, wt, [w_out.astype(BF16), cw_pad])
    cw = cw_all.reshape(N_DEV, 8, 128)[:, 0:CONV_K, 0:64].transpose(1, 0, 2).reshape(CONV_K, CONV_W)
    ya, att, probs, psinks, q_stack = _attn_fwd(pa, tab, sinks)
    dh, dmix, g_wo, g_fg, loss_part = _out_loss(x2, t2, ya, pc, cw, wo, fg)
    da, g_sinks = _attn_bwd(pa, dmix, att, probs, psinks, q_stack, tab)
    g_wt, dc, g_cw = _grad_w_in(da, pc, dmix, cw, xn)
    cw_pack = jnp.pad(g_cw.reshape(CONV_K, N_DEV, 64).transpose(1, 0, 2),
                      ((0, 0), (0, 8 - CONV_K), (0, 64))).reshape(N_DEV * 8, 128)
    small = jnp.concatenate([g_fg.reshape(8, 128), g_sinks, loss_part, cw_pack], axis=0)
    grad_x, parts, own, others = _grad_x(
        da, dc, wt, x2, dh, ng, small,
        [g_wt.reshape(N_DEV, SHARD_IN, D_MODEL), g_wo.reshape(N_DEV, SHARD_OUT, D_MODEL)])
    gt, dt, nmt, nvt = _sum_chips_adamw(own[0], others[0], w_in.T, m_w_in.T, v_w_in.T, "adamw_w_in")
    grad_w_in, d_w_in, nm_w_in, nv_w_in = gt.T, dt.T, nmt.T, nvt.T
    grad_w_out, d_w_out, nm_w_out, nv_w_out = _sum_chips_adamw(
        own[1], others[1], w_out, m_w_out, v_w_out, "adamw_w_out")
    vec = lambda a: a.reshape(8, 128)
    row = lambda a: a.reshape(1, 8)
    res = _small_adamw(parts, [
        (vec(norm_g), vec(m_norm_g), vec(v_norm_g)), (vec(final_g), vec(m_final_g), vec(v_final_g)),
        (row(sinks), row(m_sinks), row(v_sinks)), (conv_w, m_conv_w, v_conv_w)])
    loss = res[0].reshape(())
    grad_norm_g, d_ng, nm_ng, nv_ng = [a.reshape(D_MODEL) for a in res[1:5]]
    grad_final_g, d_fg, nm_fg, nv_fg = [a.reshape(D_MODEL) for a in res[5:9]]
    grad_sinks, d_sk, nm_sk, nv_sk = [a.reshape(N_Q_HEADS) for a in res[9:13]]
    grad_conv_w, d_cw, nm_cw, nv_cw = res[13:17]

    return (loss, grad_x.reshape(1, S, D_MODEL), grad_norm_g, grad_w_in, grad_sinks, grad_conv_w, grad_w_out, grad_final_g,
            d_ng, d_w_in, d_sk, d_cw, d_w_out, d_fg,
            nm_ng, nm_w_in, nm_sk, nm_cw, nm_w_out, nm_fg,
            nv_ng, nv_w_in, nv_sk, nv_cw, nv_w_out, nv_fg)
```

```python
import numpy as np
import jax
import jax.numpy as jnp
from jax import lax
from jax.experimental import pallas as pl
from jax.experimental.pallas import tpu as pltpu

F32 = jnp.float32
BF16 = jnp.bfloat16
MESH = pl.DeviceIdType.MESH

D_MODEL = 1024
HEAD_DIM = 64
N_Q_HEADS = 8
GROUP = 4
ATTN_W = 512
KV_W = 128
BLK = 128
CONV_W = 512
CONV_K = 3
IN_W = 3328
PA_W = 1280
PC_W = 2048
EPS = 1e-5
ROPE_THETA = 500000.0
ROT_DIM = 16
N_DEV = 8
N_CHIP = 4
SHARD_IN = IN_W // N_DEV
SHARD_OUT = D_MODEL // N_DEV

ADAM_LR = 0.001
ADAM_B1 = 0.9
ADAM_B2 = 0.999
ADAM_EPS = 1e-08
ADAM_WD = 0.01
ADAM_STEP = 10

ACT = jnp.bfloat16

TM = 512
TQ = 1024
HALO = 16
VMEM_LIMIT = 56 * 1024 * 1024

NT_DIMS = (((1,), (1,)), ((), ()))
TN_DIMS = (((0,), (0,)), ((), ()))


def _params(sem=None):
    kw = dict(vmem_limit_bytes=VMEM_LIMIT)
    if sem is not None:
        kw["dimension_semantics"] = sem
    return pltpu.CompilerParams(**kw)


def _nt(a, b):
    return lax.dot_general(a, b, NT_DIMS, preferred_element_type=F32)


def _tn(a, b):
    return lax.dot_general(a, b, TN_DIMS, preferred_element_type=F32)


def _nn(a, b):
    return jnp.dot(a, b, preferred_element_type=F32)


def _silu(g):
    return g * jax.nn.sigmoid(g)


def _silu_and_grad(g):
    s = jax.nn.sigmoid(g)
    return g * s, s * (1.0 + g * (1.0 - s))


class _AllGatherInSteps:
    def __init__(self, arrs, forward_step):
        self.blocks = [(a.shape, a.dtype) for a in arrs]
        self.n = len(arrs)
        self.forward_step = forward_step

    def out_shape(self):
        return [jax.ShapeDtypeStruct((N_DEV * s[0], s[1]), d) for s, d in self.blocks]

    def scratch_shapes(self):
        return [pltpu.SemaphoreType.DMA((7 * self.n,)), pltpu.SemaphoreType.DMA((7 * self.n,)),
                pltpu.SemaphoreType.DMA((self.n,))]

    def emit(self, step, n_steps, x_refs, out_refs, scratch):
        assert n_steps > self.forward_step + 1
        send_sems, recv_sems, local_sems = scratch
        x, y, c = lax.axis_index("x"), lax.axis_index("y"), lax.axis_index("c")
        me, sibling = (x, y, c), (x, y, 1 - c)
        chips = [(1 - x, y), (x, 1 - y), (1 - x, 1 - y)]

        def rows(a, px, py, pc):
            m = self.blocks[a][0][0]
            return out_refs[a].at[pl.ds((4 * px + 2 * py + pc) * m, m), :]

        def copy(a, k, block, to, src=None):
            return pltpu.make_async_remote_copy(
                src_ref=rows(a, *block) if src is None else src, dst_ref=rows(a, *block),
                send_sem=send_sems.at[a * 7 + k], recv_sem=recv_sems.at[a * 7 + k],
                device_id=to, device_id_type=MESH)

        def mine(a):
            return pltpu.make_async_copy(x_refs[a], rows(a, *me), local_sems.at[a])

        def first(a):
            return ([copy(a, 0, me, sibling, src=x_refs[a])]
                    + [copy(a, 1 + j, me, (*chip, c), src=x_refs[a]) for j, chip in enumerate(chips)])

        def passed(a):
            return [copy(a, 4 + j, (*chip, c), sibling) for j, chip in enumerate(chips)]

        @pl.when(step == 0)
        def _():
            for a in range(self.n):
                mine(a).start()
                for cp in first(a):
                    cp.start()

        @pl.when(step == self.forward_step)
        def _():
            for j, chip in enumerate(chips):
                for a in range(self.n):
                    copy(a, 1 + j, (*chip, c), me).wait_recv()
                    copy(a, 4 + j, (*chip, c), sibling).start()

        def finish():
            @pl.when(step == n_steps - 1)
            def _():
                for a in range(self.n):
                    copy(a, 0, sibling, me).wait_recv()
                    for j, chip in enumerate(chips):
                        copy(a, 4 + j, (*chip, 1 - c), me).wait_recv()
                    for cp in first(a) + passed(a):
                        cp.wait_send()
                    mine(a).wait()

        return finish


class _AllGatherViaNeighbours:
    def __init__(self, arr, first, second):
        (self.m, self.ncol), self.dtype = arr.shape, arr.dtype
        assert self.m % 32 == 0
        self.first, self.second = first, second

    def out_shape(self):
        return [jax.ShapeDtypeStruct((N_DEV * self.m, self.ncol), self.dtype)]

    def scratch_shapes(self):
        return [pltpu.SemaphoreType.DMA((9,)), pltpu.SemaphoreType.DMA((9,)), pltpu.SemaphoreType.DMA]

    def emit(self, step, n_steps, x_ref, out_ref, scratch):
        assert 0 < self.first < self.second < n_steps - 1
        send_sems, recv_sems, local_sem = scratch
        x, y, c = lax.axis_index("x"), lax.axis_index("y"), lax.axis_index("c")
        half = self.m // 2
        sibling, xn, yn = (x, y, 1 - c), (1 - x, y, c), (x, 1 - y, c)

        def rows(dev, part=None):
            px, py, pc = dev
            base = (4 * px + 2 * py + pc) * self.m
            if part is None:
                return out_ref.at[pl.ds(base, self.m), :]
            return out_ref.at[pl.ds(base + part * half, half), :]

        def copy(k, dev, to, part=None, src=None):
            return pltpu.make_async_remote_copy(
                src_ref=rows(dev, part) if src is None else src, dst_ref=rows(dev, part),
                send_sem=send_sems.at[k], recv_sem=recv_sems.at[k], device_id=to, device_id_type=MESH)

        me, dg = (x, y, c), (1 - x, 1 - y, c)
        mine = pltpu.make_async_copy(x_ref, rows(me), local_sem)
        sends = [
            copy(0, me, sibling, src=x_ref), copy(1, me, xn, src=x_ref), copy(2, me, yn, src=x_ref),
            copy(3, xn, yn, part=0), copy(4, yn, xn, part=1),
            copy(5, xn, sibling), copy(6, yn, sibling), copy(7, dg, sibling, part=0), copy(8, dg, sibling, part=1),
        ]
        other = lambda dev: (dev[0], dev[1], 1 - c)
        arrivals = [
            copy(0, other(me), sibling), copy(1, xn, xn), copy(2, yn, yn), copy(3, dg, yn, part=0),
            copy(4, dg, xn, part=1), copy(5, other(xn), sibling), copy(6, other(yn), sibling),
            copy(7, other(dg), sibling, part=0), copy(8, other(dg), sibling, part=1),
        ]

        @pl.when(step == 0)
        def _():
            mine.start()
            for k in (0, 1, 2):
                sends[k].start()

        @pl.when(step == self.first)
        def _():
            arrivals[1].wait_recv()
            sends[3].start()
            sends[5].start()
            arrivals[2].wait_recv()
            sends[4].start()
            sends[6].start()

        @pl.when(step == self.second)
        def _():
            arrivals[3].wait_recv()
            sends[7].start()
            arrivals[4].wait_recv()
            sends[8].start()

        def finish():
            @pl.when(step == n_steps - 1)
            def _():
                for k in (0, 5, 6, 7, 8):
                    arrivals[k].wait_recv()
                for cp in sends:
                    cp.wait_send()
                mine.wait()

        return finish


class _ReduceScatter:
    def __init__(self, grads):
        self.shapes = [g.shape[1:] for g in grads]
        self.n = len(grads)
        self.items = tuple((a, r) for r in (1, 2, 3, 0) for a in range(self.n))
        self.steps = len(self.items) + 2

    def out_shape(self):
        own = [jax.ShapeDtypeStruct(s, F32) for s in self.shapes]
        ici = [jax.ShapeDtypeStruct((N_CHIP - 1,) + s, BF16) for s in self.shapes]
        land = [jax.ShapeDtypeStruct((N_CHIP,) + s, F32) for s in self.shapes]
        return own + ici + land

    def scratch_shapes(self):
        n_items = len(self.items)
        return ([pltpu.VMEM((2,) + s, F32) for s in self.shapes]
                + [pltpu.VMEM((N_CHIP - 1,) + s, BF16) for s in self.shapes]
                + [pltpu.VMEM(s, F32) for s in self.shapes]
                + [pltpu.SemaphoreType.DMA((self.n * N_CHIP,))] * 2
                + [pltpu.SemaphoreType.DMA((2 * n_items,))]
                + [pltpu.SemaphoreType.DMA((self.n * (N_CHIP - 1),))] * 2
                + [pltpu.SemaphoreType.DMA((self.n,))])

    def emit(self, step, n_steps, g_refs, out_refs, scratch):
        assert n_steps > self.steps
        n = self.n
        own_refs, ici_refs, land_refs = out_refs[:n], out_refs[n:2 * n], out_refs[2 * n:]
        stage, pair_bf, pair_own = scratch[:n], scratch[n:2 * n], scratch[2 * n:3 * n]
        sib_send, sib_recv, load_sems, ici_send, ici_recv, own_sems = scratch[3 * n:]
        x, y, c = lax.axis_index("x"), lax.axis_index("y"), lax.axis_index("c")

        def chip_of(r):
            return (x ^ (r >> 1), y ^ (r & 1))

        def block_of(r, core):
            cx, cy = chip_of(r)
            return 4 * cx + 2 * cy + core

        def to_sibling(a, r):
            return pltpu.make_async_remote_copy(
                src_ref=g_refs[a].at[block_of(r, 1 - c)], dst_ref=land_refs[a].at[r],
                send_sem=sib_send.at[a * N_CHIP + r], recv_sem=sib_recv.at[a * N_CHIP + r],
                device_id=(x, y, 1 - c), device_id_type=MESH)

        def loads(k):
            a, r = self.items[k]
            return (pltpu.make_async_copy(g_refs[a].at[block_of(r, c)], stage[a].at[0], load_sems.at[2 * k]),
                    pltpu.make_async_copy(land_refs[a].at[r], stage[a].at[1], load_sems.at[2 * k + 1]))

        def to_owner(k):
            a, r = self.items[k]
            if r == 0:
                return pltpu.make_async_copy(pair_own[a], own_refs[a], own_sems.at[a])
            return pltpu.make_async_remote_copy(
                src_ref=pair_bf[a].at[r - 1], dst_ref=ici_refs[a].at[r - 1],
                send_sem=ici_send.at[a * (N_CHIP - 1) + r - 1], recv_sem=ici_recv.at[a * (N_CHIP - 1) + r - 1],
                device_id=(*chip_of(r), c), device_id_type=MESH)

        @pl.when(step == 0)
        def _():
            for a, r in self.items:
                to_sibling(a, r).start()

        for k, (a, r) in enumerate(self.items):
            @pl.when(step == 1 + k)
            def _(k=k, a=a, r=r):
                to_sibling(a, r).wait_recv()
                for cp in loads(k):
                    cp.start()

            @pl.when(step == 2 + k)
            def _(k=k, a=a, r=r):
                for cp in loads(k):
                    cp.wait()
                total = stage[a][0] + stage[a][1]
                if r == 0:
                    pair_own[a][...] = total
                else:
                    pair_bf[a][r - 1] = total.astype(BF16)
                to_owner(k).start()

        def finish():
            @pl.when(step == n_steps - 1)
            def _():
                for k, (a, r) in enumerate(self.items):
                    if r == 0:
                        to_owner(k).wait()
                    else:
                        to_owner(k).wait_send()
                        to_owner(k).wait_recv()
                for a, r in self.items:
                    to_sibling(a, r).wait_send()

        return finish


def _prologue(x, norm_g, w_shard):
    S = x.shape[0]
    n_steps = S // TM
    half = ROT_DIM // 2
    pos = jnp.arange(S, dtype=jnp.int32).astype(F32)
    inv_freq = ROPE_THETA ** (-jnp.arange(0, ROT_DIM, 2, dtype=F32) / ROT_DIM)
    ang = inv_freq[:, None] * pos[None, :]
    cs = jnp.concatenate([jnp.cos(ang), jnp.sin(ang)], axis=0)
    ag = _AllGatherViaNeighbours(w_shard, first=n_steps // 2 - 2, second=n_steps - 4)

    def body(x_ref, g_ref, cs_ref, w_ref, xn_ref, tab_ref, wt_ref, *ag_scratch):
        step = pl.program_id(0)
        finish = ag.emit(step, n_steps, w_ref, wt_ref, ag_scratch)
        xv = x_ref[...]
        r = lax.rsqrt(jnp.mean(xv * xv, axis=-1, keepdims=True) + EPS)
        xn_ref[...] = (xv * r * g_ref[...]).astype(BF16)

        xt = jnp.concatenate([cs_ref[...], jnp.zeros((128 - 2 * half, TM), F32)], axis=0).T
        lane = lax.broadcasted_iota(jnp.int32, (TM, 128), 1)
        rr = lane & (HEAD_DIM - 1)
        first = lane < HEAD_DIM

        def at(shift_first, shift_second):
            return jnp.where(first, pltpu.roll(xt, shift_first, 1) if shift_first else xt,
                             pltpu.roll(xt, shift_second, 1))

        cos_lo, cos_hi = at(0, HEAD_DIM), at(half, HEAD_DIM + half)
        sin_lo, sin_hi = at(128 - half, HEAD_DIM - half), at(0, HEAD_DIM)
        tab_ref[:, 0:128] = jnp.where(rr < half, cos_lo, jnp.where(rr < ROT_DIM, cos_hi, 1.0))
        tab_ref[:, 128:256] = jnp.where(rr < half, -sin_lo, 0.0)
        tab_ref[:, 256:384] = jnp.where((rr >= half) & (rr < ROT_DIM), sin_hi, 0.0)
        finish()

    any_spec = pl.BlockSpec(memory_space=pl.ANY)
    return pl.pallas_call(
        body,
        name="prologue_all_gather_w_in",
        grid=(n_steps,),
        in_specs=[
            pl.BlockSpec((TM, D_MODEL), lambda i: (i, 0)),
            pl.BlockSpec((1, D_MODEL), lambda i: (0, 0)),
            pl.BlockSpec((2 * half, TM), lambda i: (0, i)),
            any_spec,
        ],
        out_specs=[
            pl.BlockSpec((TM, D_MODEL), lambda i: (i, 0)),
            pl.BlockSpec((TM, 384), lambda i: (i, 0)),
            any_spec,
        ],
        out_shape=[
            jax.ShapeDtypeStruct((S, D_MODEL), BF16),
            jax.ShapeDtypeStruct((S, 384), F32),
        ] + ag.out_shape(),
        scratch_shapes=ag.scratch_shapes(),
        compiler_params=_params(("arbitrary",)),
    )(x, norm_g, cs, w_shard)


def _fwd_proj(xn, wt, later):
    S = xn.shape[0]
    tm = 2 * TM
    n_steps = S // tm
    ag = _AllGatherInSteps(later, forward_step=n_steps // 2)

    def body(xn_ref, wt_ref, *rest):
        later_refs, rest = rest[:ag.n], rest[ag.n:]
        pa_ref, pc_ref = rest[:2]
        gathered, ag_scratch = rest[2:2 + ag.n], rest[2 + ag.n:]
        step = pl.program_id(0)
        finish = ag.emit(step, n_steps, later_refs, gathered, ag_scratch)
        xn = xn_ref[...]
        pa_ref[:, 0:512] = _nt(xn, wt_ref[0:512, :]).astype(ACT)
        pa_ref[:, 512:1024] = _nt(xn, wt_ref[768:1280, :]).astype(ACT)
        pa_ref[:, 1024:1280] = _nt(xn, wt_ref[512:768, :]).astype(ACT)
        pc_ref[...] = _nt(xn, wt_ref[1280:3328, :]).astype(ACT)
        finish()

    any_spec = pl.BlockSpec(memory_space=pl.ANY)
    outs = pl.pallas_call(
        body,
        name="fwd_proj_all_gather",
        grid=(n_steps,),
        in_specs=[
            pl.BlockSpec((tm, D_MODEL), lambda i: (i, 0)),
            pl.BlockSpec((IN_W, D_MODEL), lambda i: (0, 0)),
        ] + [any_spec] * ag.n,
        out_specs=[
            pl.BlockSpec((tm, PA_W), lambda i: (i, 0)),
            pl.BlockSpec((tm, PC_W), lambda i: (i, 0)),
        ] + [any_spec] * ag.n,
        out_shape=[
            jax.ShapeDtypeStruct((S, PA_W), ACT),
            jax.ShapeDtypeStruct((S, PC_W), ACT),
        ] + ag.out_shape(),
        scratch_shapes=ag.scratch_shapes(),
        compiler_params=_params(("arbitrary",)),
    )(xn, wt, *later)
    return outs[0], outs[1], outs[2:]


def _rope(t, tab):
    return (t * tab[:, 0:128] + pltpu.roll(t, 120, 1) * tab[:, 128:256]
            + pltpu.roll(t, 8, 1) * tab[:, 256:384])


def _rope_t(d, tab):
    return (d * tab[:, 0:128] + pltpu.roll(d * tab[:, 128:256], 8, 1)
            + pltpu.roll(d * tab[:, 256:384], 120, 1))


def _fill_kv(kall, kvc_ref, kvp_ref, tabc_ref, tabp_ref):
    for lo, kv_ref, tab_ref, n in ((0, kvp_ref, tabp_ref, BLK), (BLK, kvc_ref, tabc_ref, TQ)):
        k = _rope(kv_ref[:, 0:128].astype(F32), tab_ref[...])
        v = kv_ref[:, 128:256].astype(F32)
        kall[0, lo:lo + n, :] = k.astype(BF16)
        kall[1, lo:lo + n, :] = pltpu.roll(k, 64, 1).astype(BF16)
        kall[2, lo:lo + n, :] = v.astype(BF16)
        kall[3, lo:lo + n, :] = pltpu.roll(v, 64, 1).astype(BF16)


HEADS = (((0, 0), (1, 0), (2, 1), (3, 1)), ((0, 1), (1, 1), (2, 0), (3, 0)))


def _upper():
    kj = lax.broadcasted_iota(jnp.int32, (BLK, 4 * BLK), 0)
    qi = lax.broadcasted_iota(jnp.int32, (BLK, 4 * BLK), 1) & (BLK - 1)
    return kj > qi


def _merge(upper, both):
    return jnp.where(upper, both[0:BLK, :], both[BLK:2 * BLK, :])


def _split_store(ref, s, upper_b, vb):
    first = vb * upper_b
    ref[s, 0:BLK, :] = first
    ref[s, BLK:2 * BLK, :] = vb - first


def _sink_rows(sink_ref):
    return [jnp.concatenate([jnp.full((1, BLK), sink_ref[2 * p + e], F32) for p, e in HEADS[s]], axis=1)
            for s in range(2)]


def _stack_heads(ref, slot, half, pairs, s=None):
    for a, (p, e) in enumerate(HEADS[slot if s is None else s]):
        ref[slot, a * BLK:(a + 1) * BLK, :] = jnp.where(half[e], pairs[p], 0.0).astype(BF16)


def _unstack_pair(half, outs, p):
    lo = 0 if p < 2 else 1
    rows = slice(p * BLK, (p + 1) * BLK)
    return jnp.where(half[0], outs[lo][rows, :], outs[1 - lo][rows, :])


def _softmax(sm, sinks):
    m = jnp.maximum(jnp.max(sm, axis=0, keepdims=True), sinks)
    p = jnp.exp(sm - m)
    es = jnp.exp(sinks - m)
    inv = 1.0 / (jnp.sum(p, axis=0, keepdims=True) + es)
    return p * inv, es * inv


def _scores(kk, q_stack, first):
    st = _nt(kk, q_stack)
    prev = st[0:BLK, :]
    if first is not None:
        prev = prev + jnp.where(first, -jnp.inf, 0.0)
    return prev, st[BLK:2 * BLK, :]


def _attn_specs(tile):
    nb = TQ // BLK
    prev = lambda i: jnp.maximum(tile(i) * nb - 1, 0)
    return [
        pl.BlockSpec(memory_space=pltpu.SMEM),
        pl.BlockSpec((TQ, ATTN_W), lambda i: (tile(i), 0)),
        pl.BlockSpec((TQ, ATTN_W), lambda i: (tile(i), 1)),
        pl.BlockSpec((TQ, 2 * KV_W), lambda i: (tile(i), 4)),
        pl.BlockSpec((BLK, 2 * KV_W), lambda i: (prev(i), 4)),
        pl.BlockSpec((TQ, 384), lambda i: (tile(i), 0)),
        pl.BlockSpec((BLK, 384), lambda i: (prev(i), 0)),
    ]


def _attn_fwd(pa, tab, sinks):
    S = pa.shape[0]
    nb = TQ // BLK

    def body(sink_ref, q_ref, g_ref, kvc_ref, kvp_ref, tabc_ref, tabp_ref, o_ref, att_ref, pm_ref, ps_ref,
             qs_ref, kall, p_sc):
        i = pl.program_id(0)
        _fill_kv(kall, kvc_ref, kvp_ref, tabc_ref, tabp_ref)
        lane = lax.broadcasted_iota(jnp.int32, (BLK, 128), 1)
        half = [lane < HEAD_DIM, lane >= HEAD_DIM]
        upper = _upper()
        upper_b = upper.astype(BF16)
        sinks = _sink_rows(sink_ref)
        for j in range(nb):
            rq = slice(j * BLK, (j + 1) * BLK)
            rk = slice(j * BLK, (j + 2) * BLK)
            tab = tabc_ref[rq, :]
            qr = [_rope(q_ref[rq, p * 128:(p + 1) * 128].astype(F32), tab) * 0.125 for p in range(4)]
            outs = []
            for s in range(2):
                _stack_heads(qs_ref, 2 * j + s, half, qr, s)
                prev, cur = _scores(kall[s, rk, :], qs_ref[2 * j + s], i == 0 if j == 0 else None)
                prob, psink = _softmax(jnp.where(upper, prev, cur), sinks[s])
                pb = prob.astype(BF16)
                pm_ref[(2 * j + s) * BLK:(2 * j + s + 1) * BLK, :] = pb
                ps_ref[2 * j + s:2 * j + s + 1, :] = psink
                _split_store(p_sc, s, upper_b, pb)
                outs.append(_tn(p_sc[s], kall[2 + s, rk, :]))
            for p in range(4):
                cols = slice(p * 128, (p + 1) * 128)
                att = _unstack_pair(half, outs, p)
                att_ref[rq, cols] = att.astype(BF16)
                o_ref[rq, cols] = (att * _silu(g_ref[rq, cols].astype(F32))).astype(BF16)

    return pl.pallas_call(
        body,
        name="attn_fwd",
        grid=(S // TQ,),
        in_specs=_attn_specs(lambda i: i),
        out_specs=[pl.BlockSpec((TQ, ATTN_W), lambda i: (i, 0))] * 2 + [
            pl.BlockSpec((2 * TQ, 4 * BLK), lambda i: (i, 0)),
            pl.BlockSpec((2 * nb, 4 * BLK), lambda i: (i, 0)),
            pl.BlockSpec((2 * nb, 4 * BLK, 128), lambda i: (i, 0, 0)),
        ],
        out_shape=[jax.ShapeDtypeStruct((S, ATTN_W), BF16)] * 2 + [
            jax.ShapeDtypeStruct((2 * S, 4 * BLK), BF16),
            jax.ShapeDtypeStruct((2 * S // BLK, 4 * BLK), F32),
            jax.ShapeDtypeStruct((2 * S // BLK, 4 * BLK, 128), BF16),
        ],
        scratch_shapes=[
            pltpu.VMEM((4, BLK + TQ, 128), BF16),
            pltpu.VMEM((2, 2 * BLK, 4 * BLK), BF16),
        ],
        compiler_params=_params(("arbitrary",)),
    )(sinks, pa, pa, pa, pa, tab, tab)


def _shift_down(u, halo_ref, has_prev):
    def halo_u(r):
        hu = halo_ref[r:r + 1, 512:1024].astype(F32) * halo_ref[r:r + 1, 1024:1536].astype(F32)
        return jnp.where(has_prev, hu, 0.0)

    row = lax.broadcasted_iota(jnp.int32, u.shape, 0)
    um1 = jnp.where(row == 0, halo_u(HALO - 1), pltpu.roll(u, 1, 0))
    um2 = jnp.where(row == 0, halo_u(HALO - 2), jnp.where(row == 1, halo_u(HALO - 1), pltpu.roll(u, 2, 0)))
    return um1, um2


def _conv_tile(pc_ref, halo_ref, w_ref, has_prev):
    b = pc_ref[:, 0:512].astype(F32)
    c = pc_ref[:, 512:1024].astype(F32)
    hh = pc_ref[:, 1024:1536].astype(F32)
    gc = pc_ref[:, 1536:2048].astype(F32)
    u = c * hh
    um1, um2 = _shift_down(u, halo_ref, has_prev)
    cv = w_ref[0:1, :] * um2 + w_ref[1:2, :] * um1 + w_ref[2:3, :] * u
    return b, c, hh, gc, u, um1, um2, cv


def _prev_rows(width, col=0):
    return pl.BlockSpec((HALO, width), lambda i: (jnp.maximum(i * (TM // HALO) - 1, 0), col))


def _out_loss(x, target, ya, pc, conv_w, w_out, final_g):
    S = x.shape[0]

    def body(x_ref, t_ref, ya_ref, pc_ref, halo_ref, cw_ref, wo_ref, fg_ref,
             dh_ref, dmix_ref, gwo_ref, gfg_ref, loss_ref):
        @pl.when(pl.program_id(0) == 0)
        def _():
            gwo_ref[...] = jnp.zeros_like(gwo_ref)
            gfg_ref[...] = jnp.zeros_like(gfg_ref)
            loss_ref[...] = jnp.zeros_like(loss_ref)

        b, _, _, gc, _, _, _, cv = _conv_tile(pc_ref, halo_ref, cw_ref, pl.program_id(0) > 0)
        yc = (b * cv * _silu(gc)).astype(BF16)
        mix = jnp.concatenate([ya_ref[...], yc], axis=1)
        wo = wo_ref[...]
        fg = fg_ref[...]
        h = x_ref[...] + _nn(mix, wo)
        r = lax.rsqrt(jnp.mean(h * h, axis=-1, keepdims=True) + EPS)
        n = h * r
        err = n * fg - t_ref[...]
        loss_ref[...] += jnp.broadcast_to(
            0.5 * jnp.sum(jnp.mean(err * err, axis=-1, keepdims=True), axis=0, keepdims=True), (8, 128))
        gfg_ref[...] += jnp.sum(err * n, axis=0, keepdims=True) * (1.0 / D_MODEL)
        dyg = err * (fg * (1.0 / D_MODEL))
        dh = r * (dyg - n * jnp.mean(dyg * n, axis=-1, keepdims=True))
        dh_ref[...] = dh
        dhb = dh.astype(BF16)
        dmix_ref[...] = _nt(dhb, wo).astype(ACT)
        gwo_ref[...] += _tn(mix, dhb)

    row = lambda i: (i, 0)
    fixed = lambda i: (0, 0)
    return pl.pallas_call(
        body,
        name="out_loss",
        grid=(S // TM,),
        in_specs=[
            pl.BlockSpec((TM, D_MODEL), row),
            pl.BlockSpec((TM, D_MODEL), row),
            pl.BlockSpec((TM, ATTN_W), row),
            pl.BlockSpec((TM, PC_W), row),
            _prev_rows(PC_W),
            pl.BlockSpec((CONV_K, CONV_W), fixed),
            pl.BlockSpec((D_MODEL, D_MODEL), fixed),
            pl.BlockSpec((1, D_MODEL), fixed),
        ],
        out_specs=[
            pl.BlockSpec((TM, D_MODEL), row),
            pl.BlockSpec((TM, D_MODEL), row),
            pl.BlockSpec((D_MODEL, D_MODEL), fixed),
            pl.BlockSpec((1, D_MODEL), fixed),
            pl.BlockSpec((8, 128), fixed),
        ],
        out_shape=[
            jax.ShapeDtypeStruct((S, D_MODEL), F32),
            jax.ShapeDtypeStruct((S, D_MODEL), ACT),
            jax.ShapeDtypeStruct((D_MODEL, D_MODEL), F32),
            jax.ShapeDtypeStruct((1, D_MODEL), F32),
            jax.ShapeDtypeStruct((8, 128), F32),
        ],
        compiler_params=_params(("arbitrary",)),
    )(x, target, ya, pc, pc, conv_w, w_out, final_g)


def _attn_bwd(pa, dmix, att, probs, psinks, q_stack, tab):
    S = pa.shape[0]
    nt = S // TQ
    nb = TQ // BLK

    def body(g_ref, kvc_ref, kvp_ref, tabc_ref, tabp_ref, dm_ref, att_ref, pm_ref, ps_ref, qs_ref,
             d_ref, dsink_ref, kall, dkv, carry, do_sc, p_sc, ds_sc, dsink_acc):
        step = pl.program_id(0)

        @pl.when(step == 0)
        def _():
            carry[...] = jnp.zeros_like(carry)
            dsink_acc[...] = jnp.zeros_like(dsink_acc)

        _fill_kv(kall, kvc_ref, kvp_ref, tabc_ref, tabp_ref)
        dkv[0:TQ, :] = jnp.zeros((TQ, 2 * KV_W), F32)
        dkv[TQ:TQ + BLK, :] = carry[...]
        lane = lax.broadcasted_iota(jnp.int32, (BLK, 128), 1)
        half = [lane < HEAD_DIM, lane >= HEAD_DIM]
        upper = _upper()
        upper_b = upper.astype(BF16)
        for j in range(nb):
            rq = slice(j * BLK, (j + 1) * BLK)
            rk = slice(j * BLK, (j + 2) * BLK)
            tab = tabc_ref[rq, :]
            pair = [slice(p * 128, (p + 1) * 128) for p in range(4)]
            g = [g_ref[rq, c].astype(F32) for c in pair]
            da = [dm_ref[rq, c].astype(F32) for c in pair]
            gate = [_silu_and_grad(g[p]) for p in range(4)]
            do = [da[p] * gate[p][0] for p in range(4)]
            dqs, dks, dvs = [], [], []
            for s in range(2):
                kk = kall[s, rk, :]
                vv = kall[2 + s, rk, :]
                _stack_heads(do_sc, s, half, do)
                pb = pm_ref[(2 * j + s) * BLK:(2 * j + s + 1) * BLK, :]
                prob = pb.astype(F32)
                _split_store(p_sc, s, upper_b, pb)
                dprob = _merge(upper, _nt(vv, do_sc[s]))
                dsum = jnp.sum(dprob * prob, axis=0, keepdims=True)
                _split_store(ds_sc, s, upper_b, (prob * (dprob - dsum)).astype(BF16))
                dsink_acc[s, 0:1, :] += ps_ref[2 * j + s:2 * j + s + 1, :] * dsum
                dqs.append(_tn(ds_sc[s], kk))
                dks.append(_nn(ds_sc[s], qs_ref[2 * j + s]))
                dvs.append(_nn(p_sc[s], do_sc[s]))
            for p in range(4):
                d_ref[rq, pair[p]] = _rope_t(_unstack_pair(half, dqs, p) * 0.125, tab).astype(BF16)
                d_ref[rq, 512 + p * 128:512 + (p + 1) * 128] = (
                    da[p] * att_ref[rq, pair[p]].astype(F32) * gate[p][1]).astype(BF16)
            dkv[rk, 0:128] += dks[0] + pltpu.roll(dks[1], 64, 1)
            dkv[rk, 128:256] += dvs[0] + pltpu.roll(dvs[1], 64, 1)
        d_ref[:, 1024:1152] = _rope_t(dkv[BLK:BLK + TQ, 0:128], tabc_ref[...]).astype(BF16)
        d_ref[:, 1152:1280] = dkv[BLK:BLK + TQ, 128:256].astype(BF16)
        carry[...] = dkv[0:BLK, :]

        @pl.when(step == nt - 1)
        def _():
            lanes = lax.broadcasted_iota(jnp.int32, (8, 128), 1)
            out = jnp.zeros((8, 128), F32)
            for s in range(2):
                for a, (p, e) in enumerate(HEADS[s]):
                    tot = jnp.sum(dsink_acc[s, 0:1, a * BLK:(a + 1) * BLK], axis=1, keepdims=True)
                    out = jnp.where(lanes == 2 * p + e, -tot, out)
            dsink_ref[...] = out

    rev = lambda s: nt - 1 - s
    return pl.pallas_call(
        body,
        name="attn_bwd",
        grid=(nt,),
        in_specs=_attn_specs(rev)[2:] + [pl.BlockSpec((TQ, ATTN_W), lambda s: (nt - 1 - s, 0))] * 2 + [
            pl.BlockSpec((2 * TQ, 4 * BLK), lambda s: (nt - 1 - s, 0)),
            pl.BlockSpec((2 * nb, 4 * BLK), lambda s: (nt - 1 - s, 0)),
            pl.BlockSpec((2 * nb, 4 * BLK, 128), lambda s: (nt - 1 - s, 0, 0)),
        ],
        out_specs=[
            pl.BlockSpec((TQ, PA_W), lambda s: (nt - 1 - s, 0)),
            pl.BlockSpec((8, 128), lambda s: (0, 0)),
        ],
        out_shape=[
            jax.ShapeDtypeStruct((S, PA_W), BF16),
            jax.ShapeDtypeStruct((8, 128), F32),
        ],
        scratch_shapes=[
            pltpu.VMEM((4, BLK + TQ, 128), BF16),
            pltpu.VMEM((BLK + TQ, 2 * KV_W), F32),
            pltpu.VMEM((BLK, 2 * KV_W), F32),
            pltpu.VMEM((2, 4 * BLK, 128), BF16),
            pltpu.VMEM((2, 2 * BLK, 4 * BLK), BF16),
            pltpu.VMEM((2, 2 * BLK, 4 * BLK), BF16),
            pltpu.VMEM((2, 8, 4 * BLK), F32),
        ],
        compiler_params=_params(("arbitrary",)),
    )(pa, pa, pa, tab, tab, dmix, att, probs, psinks, q_stack)


def _conv_bwd_tile(pc_ref, prev_ref, next_ref, dm_ref, dmn_ref, w_ref, d_ref, gw_ref, has_prev, has_next,
                   on_piece):
    rows = pc_ref.shape[0]
    w0, w1, w2 = w_ref[0:1, :], w_ref[1:2, :], w_ref[2:3, :]
    b, c, hh, gc, u, um1, um2, cv = _conv_tile(pc_ref, prev_ref, w_ref, has_prev)
    sg, dsg = _silu_and_grad(gc)
    dy = dm_ref[...].astype(F32)
    dyb = dy * b
    dcv = dyb * sg

    def next_dcv(r):
        nd = (dmn_ref[r:r + 1, :].astype(F32) * next_ref[r:r + 1, 0:512].astype(F32)
              * _silu(next_ref[r:r + 1, 1536:2048].astype(F32)))
        return jnp.where(has_next, nd, 0.0)

    row = lax.broadcasted_iota(jnp.int32, (rows, CONV_W), 0)
    dp1 = jnp.where(row == rows - 1, next_dcv(0), pltpu.roll(dcv, rows - 1, 0))
    dp2 = jnp.where(row == rows - 1, next_dcv(1),
                    jnp.where(row == rows - 2, next_dcv(0), pltpu.roll(dcv, rows - 2, 0)))
    du = w2 * dcv + w1 * dp1 + w0 * dp2
    pieces = (lambda: dy * cv * sg, lambda: du * hh, lambda: du * c, lambda: dyb * cv * dsg)
    for k, piece in enumerate(pieces):
        d_ref[:, k * CONV_W:(k + 1) * CONV_W] = piece().astype(BF16)
        on_piece(k)
    gw_ref[0:1, :] += jnp.sum(dcv * um2, axis=0, keepdims=True)
    gw_ref[1:2, :] += jnp.sum(dcv * um1, axis=0, keepdims=True)
    gw_ref[2:3, :] += jnp.sum(dcv * u, axis=0, keepdims=True)


def _grad_x(da, dc, wt, x, dh, norm_g, small, grads):
    S = x.shape[0]
    n_steps = S // TM
    rs = _ReduceScatter(grads)
    n_rs_out = len(rs.out_shape())
    small_rows = 8 + small.shape[0]

    def body(da_ref, dc_ref, wt_ref, x_ref, dh_ref, g_ref, small_ref, *rest):
        grad_refs, rest = rest[:rs.n], rest[rs.n:]
        gx_ref, all_ref = rest[:2]
        rs_out, rest = rest[2:2 + n_rs_out], rest[2 + n_rs_out:]
        gng, stage, small_send, small_recv, small_own = rest[:5]
        rs_scratch = rest[5:]
        step = pl.program_id(0)
        finish = rs.emit(step, n_steps, grad_refs, rs_out, rs_scratch)

        @pl.when(step == 0)
        def _():
            gng[...] = jnp.zeros_like(gng)

        dxn = (_nn(da_ref[:, 0:512], wt_ref[0:512, :]) + _nn(da_ref[:, 512:1024], wt_ref[768:1280, :])
               + _nn(da_ref[:, 1024:1280], wt_ref[512:768, :]) + _nn(dc_ref[...], wt_ref[1280:3328, :]))
        xv = x_ref[...]
        r = lax.rsqrt(jnp.mean(xv * xv, axis=-1, keepdims=True) + EPS)
        n = xv * r
        gng[...] += jnp.sum(dxn * n, axis=0, keepdims=True)
        dxg = dxn * g_ref[...]
        gx_ref[...] = dh_ref[...] + r * (dxg - n * jnp.mean(dxg * n, axis=-1, keepdims=True))

        @pl.when(step == n_steps - 1)
        def _():
            x_, y_, c_ = lax.axis_index("x"), lax.axis_index("y"), lax.axis_index("c")
            me = 4 * x_ + 2 * y_ + c_
            for q in range(8):
                stage[q:q + 1, :] = gng[:, q * 128:(q + 1) * 128]
            stage[8:small_rows, :] = small_ref[...]
            own = pltpu.make_async_copy(stage, all_ref.at[me], small_own)
            own.start()
            sends = []
            for k in range(1, N_DEV):
                cp = pltpu.make_async_remote_copy(
                    src_ref=stage, dst_ref=all_ref.at[me],
                    send_sem=small_send.at[k - 1], recv_sem=small_recv.at[k - 1],
                    device_id=(x_ ^ (k >> 2), y_ ^ ((k >> 1) & 1), c_ ^ (k & 1)), device_id_type=MESH)
                cp.start()
                sends.append(cp)
            for cp in sends:
                cp.wait_send()
                cp.wait_recv()
            own.wait()

        finish()

    row = lambda i: (i, 0)
    fixed = lambda i: (0, 0)
    any_spec = pl.BlockSpec(memory_space=pl.ANY)
    outs = pl.pallas_call(
        body,
        name="grad_x_reduce_scatter",
        grid=(n_steps,),
        in_specs=[
            pl.BlockSpec((TM, PA_W), row),
            pl.BlockSpec((TM, PC_W), row),
            pl.BlockSpec((IN_W, D_MODEL), fixed),
            pl.BlockSpec((TM, D_MODEL), row),
            pl.BlockSpec((TM, D_MODEL), row),
            pl.BlockSpec((1, D_MODEL), fixed),
            pl.BlockSpec(small.shape, fixed),
        ] + [any_spec] * rs.n,
        out_specs=[pl.BlockSpec((TM, D_MODEL), row), any_spec] + [any_spec] * n_rs_out,
        out_shape=[jax.ShapeDtypeStruct((S, D_MODEL), F32),
                   jax.ShapeDtypeStruct((N_DEV, small_rows, 128), F32)] + rs.out_shape(),
        scratch_shapes=[
            pltpu.VMEM((1, D_MODEL), F32),
            pltpu.VMEM((small_rows, 128), F32),
            pltpu.SemaphoreType.DMA((N_DEV - 1,)),
            pltpu.SemaphoreType.DMA((N_DEV - 1,)),
            pltpu.SemaphoreType.DMA,
        ] + rs.scratch_shapes(),
        compiler_params=_params(("arbitrary",)),
    )(da, dc, wt, x, dh, norm_g, small, *grads)
    return outs[0], outs[1], outs[2:2 + rs.n], outs[2 + rs.n:2 + 2 * rs.n]


def _grad_w_in(da, pc, dmix, conv_w, xn):
    S = xn.shape[0]
    tm = 2 * TM
    nt = S // tm
    t16 = tm // HALO

    def body(da_ref, pc_ref, prev_ref, next_ref, dm_ref, dmn_ref, cw_ref, xn_ref, gw_ref, dc_ref, gcw_ref):
        i = pl.program_id(0)

        @pl.when(i == 0)
        def _():
            gw_ref[...] = jnp.zeros_like(gw_ref)
            gcw_ref[...] = jnp.zeros_like(gcw_ref)

        xn = xn_ref[...]
        gw_ref[0:512, :] += _tn(da_ref[:, 0:512], xn)
        gw_ref[768:1280, :] += _tn(da_ref[:, 512:1024], xn)
        gw_ref[512:768, :] += _tn(da_ref[:, 1024:1280], xn)
        def piece_grad(k):
            rows = slice(PA_W + k * CONV_W, PA_W + (k + 1) * CONV_W)
            gw_ref[rows, :] += _tn(dc_ref[:, k * CONV_W:(k + 1) * CONV_W], xn)

        _conv_bwd_tile(pc_ref, prev_ref, next_ref, dm_ref, dmn_ref, cw_ref, dc_ref, gcw_ref, i > 0, i < nt - 1,
                       piece_grad)

    row = lambda i: (i, 0)
    fixed = lambda i: (0, 0)
    nxt = lambda i: jnp.minimum((i + 1) * t16, nt * t16 - 1)
    return pl.pallas_call(
        body,
        name="grad_w_in",
        grid=(nt,),
        in_specs=[
            pl.BlockSpec((tm, PA_W), row),
            pl.BlockSpec((tm, PC_W), row),
            pl.BlockSpec((HALO, PC_W), lambda i: (jnp.maximum(i * t16 - 1, 0), 0)),
            pl.BlockSpec((HALO, PC_W), lambda i: (nxt(i), 0)),
            pl.BlockSpec((tm, CONV_W), lambda i: (i, 1)),
            pl.BlockSpec((HALO, CONV_W), lambda i: (nxt(i), 1)),
            pl.BlockSpec((CONV_K, CONV_W), fixed),
            pl.BlockSpec((tm, D_MODEL), row),
        ],
        out_specs=[
            pl.BlockSpec((IN_W, D_MODEL), fixed, pipeline_mode=pl.Buffered(1)),
            pl.BlockSpec((tm, PC_W), row),
            pl.BlockSpec((CONV_K, CONV_W), fixed),
        ],
        out_shape=[
            jax.ShapeDtypeStruct((IN_W, D_MODEL), F32),
            jax.ShapeDtypeStruct((S, PC_W), BF16),
            jax.ShapeDtypeStruct((CONV_K, CONV_W), F32),
        ],
        compiler_params=_params(("arbitrary",)),
    )(da, pc, pc, pc, dmix, dmix, conv_w, xn)


def _adam_update(w, g, m, v):
    c1 = 1.0 - ADAM_B1 ** ADAM_STEP
    c2 = 1.0 - ADAM_B2 ** ADAM_STEP
    nm = ADAM_B1 * m + (1.0 - ADAM_B1) * g
    nv = ADAM_B2 * v + (1.0 - ADAM_B2) * (g * g)
    return -ADAM_LR * ((nm / c1) / (jnp.sqrt(nv / c2) + ADAM_EPS) + ADAM_WD * w), nm, nv


def _sum_chips_adamw(own, others, w, m, v, name):
    def body(own_ref, p_ref, w_ref, m_ref, v_ref, g_ref, d_ref, nm_ref, nv_ref):
        g = own_ref[...]
        for k in range(N_CHIP - 1):
            g = g + p_ref[k].astype(F32)
        g_ref[...] = g
        d_ref[...], nm_ref[...], nv_ref[...] = _adam_update(w_ref[...], g, m_ref[...], v_ref[...])

    rows, cols = w.shape
    half = rows // 2
    blk = pl.BlockSpec((half, cols), lambda i: (i, 0))
    shape = jax.ShapeDtypeStruct(w.shape, F32)
    return pl.pallas_call(
        body,
        name=name,
        grid=(2,),
        in_specs=[blk, pl.BlockSpec((N_CHIP - 1, half, cols), lambda i: (0, i, 0)), blk, blk, blk],
        out_specs=[blk] * 4,
        out_shape=[shape] * 4,
        compiler_params=_params(("arbitrary",)),
    )(own, others, w, m, v)


SMALL_ROWS = 96


def _small_adamw(parts, params):
    def body(parts_ref, *rest):
        prm, outs, total = rest[:12], rest[12:29], rest[29]
        me = 4 * lax.axis_index("x") + 2 * lax.axis_index("y") + lax.axis_index("c")
        acc = parts_ref[0]
        for d in range(1, N_DEV):
            acc = acc + parts_ref[d]
        total[...] = acc
        grads = (total[0:8, :], total[8:16, :], total[16:17, 0:8],
                 total[pl.ds(pl.multiple_of(32 + me * 8, 8), CONV_K), 0:64])
        outs[0][...] = total[24:25, 0:1]
        for k, g in enumerate(grads):
            w_ref, m_ref, v_ref = prm[3 * k:3 * k + 3]
            g_ref, d_ref, nm_ref, nv_ref = outs[1 + 4 * k:5 + 4 * k]
            g_ref[...] = g
            d_ref[...], nm_ref[...], nv_ref[...] = _adam_update(w_ref[...], g, m_ref[...], v_ref[...])

    flat = [a for p in params for a in p]
    out_shape = [jax.ShapeDtypeStruct((1, 1), F32)]
    for p in params:
        out_shape += [jax.ShapeDtypeStruct(p[0].shape, F32)] * 4
    return pl.pallas_call(
        body,
        name="adamw_small",
        out_shape=out_shape,
        scratch_shapes=[pltpu.VMEM((SMALL_ROWS, 128), F32)],
        compiler_params=_params(),
    )(parts, *flat)


def kernel(x, norm_g, w_in, sinks, conv_w, w_out, final_g, loss_target, m_norm_g, m_w_in, m_sinks, m_conv_w, m_w_out, m_final_g, v_norm_g, v_w_in, v_sinks, v_conv_w, v_w_out, v_final_g):
    S = x.shape[1]
    x2 = x.reshape(S, D_MODEL)
    t2 = loss_target.reshape(S, D_MODEL)
    ng = norm_g.reshape(1, D_MODEL)
    fg = final_g.reshape(1, D_MODEL)

    cw_pad = jnp.zeros((8, 128), F32).at[0:CONV_K, 0:64].set(conv_w)
    xn, tab, wt = _prologue(x2, ng, w_in.T.astype(BF16))
    pa, pc, (wo, cw_all) = _fwd_proj(xn, wt, [w_out.astype(BF16), cw_pad])
    cw = cw_all.reshape(N_DEV, 8, 128)[:, 0:CONV_K, 0:64].transpose(1, 0, 2).reshape(CONV_K, CONV_W)
    ya, att, probs, psinks, q_stack = _attn_fwd(pa, tab, sinks)
    dh, dmix, g_wo, g_fg, loss_part = _out_loss(x2, t2, ya, pc, cw, wo, fg)
    da, g_sinks = _attn_bwd(pa, dmix, att, probs, psinks, q_stack, tab)
    g_wt, dc, g_cw = _grad_w_in(da, pc, dmix, cw, xn)
    cw_pack = jnp.pad(g_cw.reshape(CONV_K, N_DEV, 64).transpose(1, 0, 2),
                      ((0, 0), (0, 8 - CONV_K), (0, 64))).reshape(N_DEV * 8, 128)
    small = jnp.concatenate([g_fg.reshape(8, 128), g_sinks, loss_part, cw_pack], axis=0)
    grad_x, parts, own, others = _grad_x(
        da, dc, wt, x2, dh, ng, small,
        [g_wt.reshape(N_DEV, SHARD_IN, D_MODEL), g_wo.reshape(N_DEV, SHARD_OUT, D_MODEL)])
    gt, dt, nmt, nvt = _sum_chips_adamw(own[0], others[0], w_in.T, m_w_in.T, v_w_in.T, "adamw_w_in")
    grad_w_in, d_w_in, nm_w_in, nv_w_in = gt.T, dt.T, nmt.T, nvt.T
    grad_w_out, d_w_out, nm_w_out, nv_w_out = _sum_chips_adamw(
        own[1], others[1], w_out, m_w_out, v_w_out, "adamw_w_out")
    vec = lambda a: a.reshape(8, 128)
    row = lambda a: a.reshape(1, 8)
    res = _small_adamw(parts, [
        (vec(norm_g), vec(m_norm_g), vec(v_norm_g)), (vec(final_g), vec(m_final_g), vec(v_final_g)),
        (row(sinks), row(m_sinks), row(v_sinks)), (conv_w, m_conv_w, v_conv_w)])
    loss = res[0].reshape(())
    grad_norm_g, d_ng, nm_ng, nv_ng = [a.reshape(D_MODEL) for a in res[1:5]]
    grad_final_g, d_fg, nm_fg, nv_fg = [a.reshape(D_MODEL) for a in res[5:9]]
    grad_sinks, d_sk, nm_sk, nv_sk = [a.reshape(N_Q_HEADS) for a in res[9:13]]
    grad_conv_w, d_cw, nm_cw, nv_cw = res[13:17]

    return (loss, grad_x.reshape(1, S, D_MODEL), grad_norm_g, grad_w_in, grad_sinks, grad_conv_w, grad_w_out, grad_final_g,
            d_ng, d_w_in, d_sk, d_cw, d_w_out, d_fg,
            nm_ng, nm_w_in, nm_sk, nm_cw, nm_w_out, nm_fg,
            nv_ng, nv_w_in, nv_sk, nv_cw, nv_w_out, nv_fg)
```

```python
import jax
import jax.numpy as jnp
from jax import lax
from jax.experimental import pallas as pl
from jax.experimental.pallas import tpu as pltpu

F32 = jnp.float32
BF16 = jnp.bfloat16
MESH = pl.DeviceIdType.MESH

D_MODEL = 1024
HEAD_DIM = 64
N_Q_HEADS = 8
ATTN_W = 512
KV_W = 128
BLK = 128
CONV_W = 512
CONV_K = 3
IN_W = 3328
PA_W = 1280
PC_W = 2048
EPS = 1e-5
ROPE_THETA = 500000.0
ROT_DIM = 16
N_DEV = 8
N_CHIP = 4
SHARD_IN = IN_W // N_DEV
SHARD_OUT = D_MODEL // N_DEV

ADAM_LR = 0.001
ADAM_B1 = 0.9
ADAM_B2 = 0.999
ADAM_EPS = 1e-08
ADAM_WD = 0.01
ADAM_STEP = 10

ACT = jnp.bfloat16

TM = 512
TQ = 1024
HALO = 16
VMEM_LIMIT = 56 * 1024 * 1024

NT_DIMS = (((1,), (1,)), ((), ()))
TN_DIMS = (((0,), (0,)), ((), ()))


def _params(sem=None):
    kw = dict(vmem_limit_bytes=VMEM_LIMIT)
    if sem is not None:
        kw["dimension_semantics"] = sem
    return pltpu.CompilerParams(**kw)


def _nt(a, b):
    return lax.dot_general(a, b, NT_DIMS, preferred_element_type=F32)


def _tn(a, b):
    return lax.dot_general(a, b, TN_DIMS, preferred_element_type=F32)


def _nn(a, b):
    return jnp.dot(a, b, preferred_element_type=F32)


def _silu(g):
    return g * jax.nn.sigmoid(g)


def _silu_and_grad(g):
    s = jax.nn.sigmoid(g)
    return g * s, s * (1.0 + g * (1.0 - s))


class _AllGatherInSteps:
    def __init__(self, arrs, forward_step):
        self.blocks = [(a.shape, a.dtype) for a in arrs]
        self.n = len(arrs)
        self.forward_step = forward_step

    def out_shape(self):
        return [jax.ShapeDtypeStruct((N_DEV * s[0], s[1]), d) for s, d in self.blocks]

    def scratch_shapes(self):
        return [pltpu.SemaphoreType.DMA((7 * self.n,)), pltpu.SemaphoreType.DMA((7 * self.n,)),
                pltpu.SemaphoreType.DMA((self.n,))]

    def emit(self, step, n_steps, x_refs, out_refs, scratch):
        assert n_steps > self.forward_step + 1
        send_sems, recv_sems, local_sems = scratch
        x, y, c = lax.axis_index("x"), lax.axis_index("y"), lax.axis_index("c")
        me, sibling = (x, y, c), (x, y, 1 - c)
        chips = [(1 - x, y), (x, 1 - y), (1 - x, 1 - y)]

        def rows(a, px, py, pc):
            m = self.blocks[a][0][0]
            return out_refs[a].at[pl.ds((4 * px + 2 * py + pc) * m, m), :]

        def copy(a, k, block, to, src=None):
            return pltpu.make_async_remote_copy(
                src_ref=rows(a, *block) if src is None else src, dst_ref=rows(a, *block),
                send_sem=send_sems.at[a * 7 + k], recv_sem=recv_sems.at[a * 7 + k],
                device_id=to, device_id_type=MESH)

        def mine(a):
            return pltpu.make_async_copy(x_refs[a], rows(a, *me), local_sems.at[a])

        def first(a):
            return ([copy(a, 0, me, sibling, src=x_refs[a])]
                    + [copy(a, 1 + j, me, (*chip, c), src=x_refs[a]) for j, chip in enumerate(chips)])

        def passed(a):
            return [copy(a, 4 + j, (*chip, c), sibling) for j, chip in enumerate(chips)]

        @pl.when(step == 0)
        def _():
            for a in range(self.n):
                mine(a).start()
                for cp in first(a):
                    cp.start()

        @pl.when(step == self.forward_step)
        def _():
            for j, chip in enumerate(chips):
                for a in range(self.n):
                    copy(a, 1 + j, (*chip, c), me).wait_recv()
                    copy(a, 4 + j, (*chip, c), sibling).start()

        def finish():
            @pl.when(step == n_steps - 1)
            def _():
                for a in range(self.n):
                    copy(a, 0, sibling, me).wait_recv()
                    for j, chip in enumerate(chips):
                        copy(a, 4 + j, (*chip, 1 - c), me).wait_recv()
                    for cp in first(a) + passed(a):
                        cp.wait_send()
                    mine(a).wait()

        return finish


class _AllGatherViaNeighbours:
    def __init__(self, arr, first, second):
        (self.m, self.ncol), self.dtype = arr.shape, arr.dtype
        assert self.m % 32 == 0
        self.first, self.second = first, second

    def out_shape(self):
        return [jax.ShapeDtypeStruct((N_DEV * self.m, self.ncol), self.dtype)]

    def scratch_shapes(self):
        return [pltpu.SemaphoreType.DMA((9,)), pltpu.SemaphoreType.DMA((9,)), pltpu.SemaphoreType.DMA]

    def emit(self, step, n_steps, x_ref, out_ref, scratch):
        assert 0 < self.first < self.second < n_steps - 1
        send_sems, recv_sems, local_sem = scratch
        x, y, c = lax.axis_index("x"), lax.axis_index("y"), lax.axis_index("c")
        half = self.m // 2
        sibling, xn, yn = (x, y, 1 - c), (1 - x, y, c), (x, 1 - y, c)

        def rows(dev, part=None):
            px, py, pc = dev
            base = (4 * px + 2 * py + pc) * self.m
            if part is None:
                return out_ref.at[pl.ds(base, self.m), :]
            return out_ref.at[pl.ds(base + part * half, half), :]

        def copy(k, dev, to, part=None, src=None):
            return pltpu.make_async_remote_copy(
                src_ref=rows(dev, part) if src is None else src, dst_ref=rows(dev, part),
                send_sem=send_sems.at[k], recv_sem=recv_sems.at[k], device_id=to, device_id_type=MESH)

        me, dg = (x, y, c), (1 - x, 1 - y, c)
        mine = pltpu.make_async_copy(x_ref, rows(me), local_sem)
        sends = [
            copy(0, me, sibling, src=x_ref), copy(1, me, xn, src=x_ref), copy(2, me, yn, src=x_ref),
            copy(3, xn, yn, part=0), copy(4, yn, xn, part=1),
            copy(5, xn, sibling), copy(6, yn, sibling), copy(7, dg, sibling, part=0), copy(8, dg, sibling, part=1),
        ]
        other = lambda dev: (dev[0], dev[1], 1 - c)
        arrivals = [
            copy(0, other(me), sibling), copy(1, xn, xn), copy(2, yn, yn), copy(3, dg, yn, part=0),
            copy(4, dg, xn, part=1), copy(5, other(xn), sibling), copy(6, other(yn), sibling),
            copy(7, other(dg), sibling, part=0), copy(8, other(dg), sibling, part=1),
        ]

        @pl.when(step == 0)
        def _():
            mine.start()
            for k in (0, 1, 2):
                sends[k].start()

        @pl.when(step == self.first)
        def _():
            arrivals[1].wait_recv()
            sends[3].start()
            sends[5].start()
            arrivals[2].wait_recv()
            sends[4].start()
            sends[6].start()

        @pl.when(step == self.second)
        def _():
            arrivals[3].wait_recv()
            sends[7].start()
            arrivals[4].wait_recv()
            sends[8].start()

        def finish():
            @pl.when(step == n_steps - 1)
            def _():
                for k in (0, 5, 6, 7, 8):
                    arrivals[k].wait_recv()
                for cp in sends:
                    cp.wait_send()
                mine.wait()

        return finish


class _ReduceScatter:
    def __init__(self, grads):
        self.shapes = [g.shape[1:] for g in grads]
        self.n = len(grads)
        self.items = tuple((a, r) for r in (1, 2, 3, 0) for a in range(self.n))
        self.steps = N_CHIP + 2

    def out_shape(self):
        own = [jax.ShapeDtypeStruct(s, F32) for s in self.shapes]
        ici = [jax.ShapeDtypeStruct((N_CHIP - 1,) + s, BF16) for s in self.shapes]
        land = [jax.ShapeDtypeStruct((N_CHIP,) + s, F32) for s in self.shapes]
        return own + ici + land

    def scratch_shapes(self):
        n_items = len(self.items)
        return ([pltpu.VMEM((2,) + s, F32) for s in self.shapes]
                + [pltpu.VMEM((N_CHIP - 1,) + s, BF16) for s in self.shapes]
                + [pltpu.VMEM(s, F32) for s in self.shapes]
                + [pltpu.SemaphoreType.DMA((self.n * N_CHIP,))] * 2
                + [pltpu.SemaphoreType.DMA((2 * n_items,))]
                + [pltpu.SemaphoreType.DMA((self.n * (N_CHIP - 1),))] * 2
                + [pltpu.SemaphoreType.DMA((self.n,))])

    def emit(self, step, n_steps, g_refs, out_refs, scratch):
        assert n_steps > self.steps
        n = self.n
        own_refs, ici_refs, land_refs = out_refs[:n], out_refs[n:2 * n], out_refs[2 * n:]
        stage, pair_bf, pair_own = scratch[:n], scratch[n:2 * n], scratch[2 * n:3 * n]
        sib_send, sib_recv, load_sems, ici_send, ici_recv, own_sems = scratch[3 * n:]
        x, y, c = lax.axis_index("x"), lax.axis_index("y"), lax.axis_index("c")

        def chip_of(r):
            return (x ^ (r >> 1), y ^ (r & 1))

        def block_of(r, core):
            cx, cy = chip_of(r)
            return 4 * cx + 2 * cy + core

        def to_sibling(a, r):
            return pltpu.make_async_remote_copy(
                src_ref=g_refs[a].at[block_of(r, 1 - c)], dst_ref=land_refs[a].at[r],
                send_sem=sib_send.at[a * N_CHIP + r], recv_sem=sib_recv.at[a * N_CHIP + r],
                device_id=(x, y, 1 - c), device_id_type=MESH)

        def loads(k):
            a, r = self.items[k]
            return (pltpu.make_async_copy(g_refs[a].at[block_of(r, c)], stage[a].at[0], load_sems.at[2 * k]),
                    pltpu.make_async_copy(land_refs[a].at[r], stage[a].at[1], load_sems.at[2 * k + 1]))

        def to_owner(k):
            a, r = self.items[k]
            if r == 0:
                return pltpu.make_async_copy(pair_own[a], own_refs[a], own_sems.at[a])
            return pltpu.make_async_remote_copy(
                src_ref=pair_bf[a].at[r - 1], dst_ref=ici_refs[a].at[r - 1],
                send_sem=ici_send.at[a * (N_CHIP - 1) + r - 1], recv_sem=ici_recv.at[a * (N_CHIP - 1) + r - 1],
                device_id=(*chip_of(r), c), device_id_type=MESH)

        @pl.when(step == 0)
        def _():
            for a, r in self.items:
                to_sibling(a, r).start()

        def fetch(k):
            a, r = self.items[k]
            to_sibling(a, r).wait_recv()
            for cp in loads(k):
                cp.start()

        def add_and_send(k):
            a, r = self.items[k]
            for cp in loads(k):
                cp.wait()
            total = stage[a][0] + stage[a][1]
            if r == 0:
                pair_own[a][...] = total
            else:
                pair_bf[a][r - 1] = total.astype(BF16)
            to_owner(k).start()

        for g in range(N_CHIP + 1):
            @pl.when(step == 1 + g)
            def _(g=g):
                if g > 0:
                    for k in range((g - 1) * n, g * n):
                        add_and_send(k)
                if g < N_CHIP:
                    for k in range(g * n, (g + 1) * n):
                        fetch(k)

        def finish():
            @pl.when(step == n_steps - 1)
            def _():
                for k, (a, r) in enumerate(self.items):
                    if r == 0:
                        to_owner(k).wait()
                    else:
                        to_owner(k).wait_send()
                        to_owner(k).wait_recv()
                for a, r in self.items:
                    to_sibling(a, r).wait_send()

        return finish


def _prologue(x, norm_g, w_shard):
    S = x.shape[0]
    n_steps = S // TM
    half = ROT_DIM // 2
    pos = jnp.arange(S, dtype=jnp.int32).astype(F32)
    inv_freq = ROPE_THETA ** (-jnp.arange(0, ROT_DIM, 2, dtype=F32) / ROT_DIM)
    ang = inv_freq[:, None] * pos[None, :]
    cs = jnp.concatenate([jnp.cos(ang), jnp.sin(ang)], axis=0)
    ag = _AllGatherViaNeighbours(w_shard, first=n_steps // 2 - 2, second=n_steps - 4)

    def body(x_ref, g_ref, cs_ref, w_ref, xn_ref, tab_ref, wt_ref, *ag_scratch):
        step = pl.program_id(0)
        finish = ag.emit(step, n_steps, w_ref, wt_ref, ag_scratch)
        xv = x_ref[...]
        r = lax.rsqrt(jnp.mean(xv * xv, axis=-1, keepdims=True) + EPS)
        xn_ref[...] = (xv * r * g_ref[...]).astype(BF16)

        xt = jnp.concatenate([cs_ref[...], jnp.zeros((128 - 2 * half, TM), F32)], axis=0).T
        lane = lax.broadcasted_iota(jnp.int32, (TM, 128), 1)
        rr = lane & (HEAD_DIM - 1)
        first = lane < HEAD_DIM

        def at(shift_first, shift_second):
            return jnp.where(first, pltpu.roll(xt, shift_first, 1) if shift_first else xt,
                             pltpu.roll(xt, shift_second, 1))

        cos_lo, cos_hi = at(0, HEAD_DIM), at(half, HEAD_DIM + half)
        sin_lo, sin_hi = at(128 - half, HEAD_DIM - half), at(0, HEAD_DIM)
        tab_ref[:, 0:128] = jnp.where(rr < half, cos_lo, jnp.where(rr < ROT_DIM, cos_hi, 1.0))
        tab_ref[:, 128:256] = jnp.where(rr < half, -sin_lo, 0.0)
        tab_ref[:, 256:384] = jnp.where((rr >= half) & (rr < ROT_DIM), sin_hi, 0.0)
        finish()

    any_spec = pl.BlockSpec(memory_space=pl.ANY)
    return pl.pallas_call(
        body,
        name="prologue_all_gather_w_in",
        grid=(n_steps,),
        in_specs=[
            pl.BlockSpec((TM, D_MODEL), lambda i: (i, 0)),
            pl.BlockSpec((1, D_MODEL), lambda i: (0, 0)),
            pl.BlockSpec((2 * half, TM), lambda i: (0, i)),
            any_spec,
        ],
        out_specs=[
            pl.BlockSpec((TM, D_MODEL), lambda i: (i, 0)),
            pl.BlockSpec((TM, 384), lambda i: (i, 0)),
            any_spec,
        ],
        out_shape=[
            jax.ShapeDtypeStruct((S, D_MODEL), BF16),
            jax.ShapeDtypeStruct((S, 384), F32),
        ] + ag.out_shape(),
        scratch_shapes=ag.scratch_shapes(),
        compiler_params=_params(("arbitrary",)),
    )(x, norm_g, cs, w_shard)


def _fwd_proj(xn, wt, later):
    S = xn.shape[0]
    tm = 2 * TM
    n_steps = S // tm
    ag = _AllGatherInSteps(later, forward_step=n_steps // 2)

    def body(xn_ref, wt_ref, *rest):
        later_refs, rest = rest[:ag.n], rest[ag.n:]
        pa_ref, pc_ref = rest[:2]
        gathered, ag_scratch = rest[2:2 + ag.n], rest[2 + ag.n:]
        step = pl.program_id(0)
        finish = ag.emit(step, n_steps, later_refs, gathered, ag_scratch)
        xn = xn_ref[...]
        pa_ref[:, 0:512] = _nt(xn, wt_ref[0:512, :]).astype(ACT)
        pa_ref[:, 512:1024] = _nt(xn, wt_ref[768:1280, :]).astype(ACT)
        pa_ref[:, 1024:1280] = _nt(xn, wt_ref[512:768, :]).astype(ACT)
        pc_ref[...] = _nt(xn, wt_ref[1280:3328, :]).astype(ACT)
        finish()

    any_spec = pl.BlockSpec(memory_space=pl.ANY)
    outs = pl.pallas_call(
        body,
        name="fwd_proj_all_gather",
        grid=(n_steps,),
        in_specs=[
            pl.BlockSpec((tm, D_MODEL), lambda i: (i, 0)),
            pl.BlockSpec((IN_W, D_MODEL), lambda i: (0, 0)),
        ] + [any_spec] * ag.n,
        out_specs=[
            pl.BlockSpec((tm, PA_W), lambda i: (i, 0)),
            pl.BlockSpec((tm, PC_W), lambda i: (i, 0)),
        ] + [any_spec] * ag.n,
        out_shape=[
            jax.ShapeDtypeStruct((S, PA_W), ACT),
            jax.ShapeDtypeStruct((S, PC_W), ACT),
        ] + ag.out_shape(),
        scratch_shapes=ag.scratch_shapes(),
        compiler_params=_params(("arbitrary",)),
    )(xn, wt, *later)
    return outs[0], outs[1], outs[2:]


def _rope(t, tab):
    return (t * tab[:, 0:128] + pltpu.roll(t, 120, 1) * tab[:, 128:256]
            + pltpu.roll(t, 8, 1) * tab[:, 256:384])


def _rope_t(d, tab):
    return (d * tab[:, 0:128] + pltpu.roll(d * tab[:, 128:256], 8, 1)
            + pltpu.roll(d * tab[:, 256:384], 120, 1))


def _fill_kv(kall, kvc_ref, kvp_ref, tabc_ref, tabp_ref):
    for lo, kv_ref, tab_ref, n in ((0, kvp_ref, tabp_ref, BLK), (BLK, kvc_ref, tabc_ref, TQ)):
        k = _rope(kv_ref[:, 0:128].astype(F32), tab_ref[...])
        v = kv_ref[:, 128:256].astype(F32)
        kall[0, lo:lo + n, :] = k.astype(BF16)
        kall[1, lo:lo + n, :] = pltpu.roll(k, 64, 1).astype(BF16)
        kall[2, lo:lo + n, :] = v.astype(BF16)
        kall[3, lo:lo + n, :] = pltpu.roll(v, 64, 1).astype(BF16)


HEADS = (((0, 0), (1, 0), (2, 1), (3, 1)), ((0, 1), (1, 1), (2, 0), (3, 0)))


def _upper():
    kj = lax.broadcasted_iota(jnp.int32, (BLK, 4 * BLK), 0)
    qi = lax.broadcasted_iota(jnp.int32, (BLK, 4 * BLK), 1) & (BLK - 1)
    return kj > qi


def _merge(upper, both):
    return jnp.where(upper, both[0:BLK, :], both[BLK:2 * BLK, :])


def _split_store(ref, s, upper_b, vb):
    first = vb * upper_b
    ref[s, 0:BLK, :] = first
    ref[s, BLK:2 * BLK, :] = vb - first


def _sink_rows(sink_ref):
    return [jnp.concatenate([jnp.full((1, BLK), sink_ref[2 * p + e], F32) for p, e in HEADS[s]], axis=1)
            for s in range(2)]


def _stack_heads(ref, slot, half, pairs, s=None):
    for a, (p, e) in enumerate(HEADS[slot if s is None else s]):
        ref[slot, a * BLK:(a + 1) * BLK, :] = jnp.where(half[e], pairs[p], 0.0).astype(BF16)


def _unstack_pair(half, outs, p):
    lo = 0 if p < 2 else 1
    rows = slice(p * BLK, (p + 1) * BLK)
    return jnp.where(half[0], outs[lo][rows, :], outs[1 - lo][rows, :])


def _softmax(sm, sinks):
    m = jnp.maximum(jnp.max(sm, axis=0, keepdims=True), sinks)
    p = jnp.exp(sm - m)
    es = jnp.exp(sinks - m)
    inv = 1.0 / (jnp.sum(p, axis=0, keepdims=True) + es)
    return p * inv, es * inv


def _scores(kk, q_stack, first):
    st = _nt(kk, q_stack)
    prev = st[0:BLK, :]
    if first is not None:
        prev = prev + jnp.where(first, -jnp.inf, 0.0)
    return prev, st[BLK:2 * BLK, :]


def _attn_specs(tile):
    nb = TQ // BLK
    prev = lambda i: jnp.maximum(tile(i) * nb - 1, 0)
    return [
        pl.BlockSpec(memory_space=pltpu.SMEM),
        pl.BlockSpec((TQ, ATTN_W), lambda i: (tile(i), 0)),
        pl.BlockSpec((TQ, ATTN_W), lambda i: (tile(i), 1)),
        pl.BlockSpec((TQ, 2 * KV_W), lambda i: (tile(i), 4)),
        pl.BlockSpec((BLK, 2 * KV_W), lambda i: (prev(i), 4)),
        pl.BlockSpec((TQ, 384), lambda i: (tile(i), 0)),
        pl.BlockSpec((BLK, 384), lambda i: (prev(i), 0)),
    ]


def _attn_fwd(pa, tab, sinks):
    S = pa.shape[0]
    nb = TQ // BLK

    def body(sink_ref, q_ref, g_ref, kvc_ref, kvp_ref, tabc_ref, tabp_ref, o_ref, att_ref, pm_ref, ps_ref,
             qs_ref, kall, p_sc):
        i = pl.program_id(0)
        _fill_kv(kall, kvc_ref, kvp_ref, tabc_ref, tabp_ref)
        lane = lax.broadcasted_iota(jnp.int32, (BLK, 128), 1)
        half = [lane < HEAD_DIM, lane >= HEAD_DIM]
        upper = _upper()
        upper_b = upper.astype(BF16)
        sinks = _sink_rows(sink_ref)
        for j in range(nb):
            rq = slice(j * BLK, (j + 1) * BLK)
            rk = slice(j * BLK, (j + 2) * BLK)
            tab = tabc_ref[rq, :]
            qr = [_rope(q_ref[rq, p * 128:(p + 1) * 128].astype(F32), tab) * 0.125 for p in range(4)]
            outs = []
            for s in range(2):
                _stack_heads(qs_ref, 2 * j + s, half, qr, s)
                prev, cur = _scores(kall[s, rk, :], qs_ref[2 * j + s], i == 0 if j == 0 else None)
                prob, psink = _softmax(jnp.where(upper, prev, cur), sinks[s])
                pb = prob.astype(BF16)
                pm_ref[(2 * j + s) * BLK:(2 * j + s + 1) * BLK, :] = pb
                ps_ref[2 * j + s:2 * j + s + 1, :] = psink
                _split_store(p_sc, s, upper_b, pb)
                outs.append(_tn(p_sc[s], kall[2 + s, rk, :]))
            for p in range(4):
                cols = slice(p * 128, (p + 1) * 128)
                att = _unstack_pair(half, outs, p)
                att_ref[rq, cols] = att.astype(BF16)
                o_ref[rq, cols] = (att * _silu(g_ref[rq, cols].astype(F32))).astype(BF16)

    return pl.pallas_call(
        body,
        name="attn_fwd",
        grid=(S // TQ,),
        in_specs=_attn_specs(lambda i: i),
        out_specs=[pl.BlockSpec((TQ, ATTN_W), lambda i: (i, 0))] * 2 + [
            pl.BlockSpec((2 * TQ, 4 * BLK), lambda i: (i, 0)),
            pl.BlockSpec((2 * nb, 4 * BLK), lambda i: (i, 0)),
            pl.BlockSpec((2 * nb, 4 * BLK, 128), lambda i: (i, 0, 0)),
        ],
        out_shape=[jax.ShapeDtypeStruct((S, ATTN_W), BF16)] * 2 + [
            jax.ShapeDtypeStruct((2 * S, 4 * BLK), BF16),
            jax.ShapeDtypeStruct((2 * S // BLK, 4 * BLK), F32),
            jax.ShapeDtypeStruct((2 * S // BLK, 4 * BLK, 128), BF16),
        ],
        scratch_shapes=[
            pltpu.VMEM((4, BLK + TQ, 128), BF16),
            pltpu.VMEM((2, 2 * BLK, 4 * BLK), BF16),
        ],
        compiler_params=_params(("arbitrary",)),
    )(sinks, pa, pa, pa, pa, tab, tab)


def _shift_down(u, halo_ref, has_prev):
    def halo_u(r):
        hu = halo_ref[r:r + 1, 512:1024].astype(F32) * halo_ref[r:r + 1, 1024:1536].astype(F32)
        return jnp.where(has_prev, hu, 0.0)

    row = lax.broadcasted_iota(jnp.int32, u.shape, 0)
    um1 = jnp.where(row == 0, halo_u(HALO - 1), pltpu.roll(u, 1, 0))
    um2 = jnp.where(row == 0, halo_u(HALO - 2), jnp.where(row == 1, halo_u(HALO - 1), pltpu.roll(u, 2, 0)))
    return um1, um2


def _conv_tile(pc_ref, halo_ref, w_ref, has_prev):
    b = pc_ref[:, 0:512].astype(F32)
    c = pc_ref[:, 512:1024].astype(F32)
    hh = pc_ref[:, 1024:1536].astype(F32)
    gc = pc_ref[:, 1536:2048].astype(F32)
    u = c * hh
    um1, um2 = _shift_down(u, halo_ref, has_prev)
    cv = w_ref[0:1, :] * um2 + w_ref[1:2, :] * um1 + w_ref[2:3, :] * u
    return b, c, hh, gc, u, um1, um2, cv


def _prev_rows(width, col=0):
    return pl.BlockSpec((HALO, width), lambda i: (jnp.maximum(i * (TM // HALO) - 1, 0), col))


def _out_loss(x, target, ya, pc, conv_w, w_out, final_g):
    S = x.shape[0]

    def body(x_ref, t_ref, ya_ref, pc_ref, halo_ref, cw_ref, wo_ref, fg_ref,
             dh_ref, dmix_ref, gwo_ref, gfg_ref, loss_ref):
        @pl.when(pl.program_id(0) == 0)
        def _():
            gwo_ref[...] = jnp.zeros_like(gwo_ref)
            gfg_ref[...] = jnp.zeros_like(gfg_ref)
            loss_ref[...] = jnp.zeros_like(loss_ref)

        b, _, _, gc, _, _, _, cv = _conv_tile(pc_ref, halo_ref, cw_ref, pl.program_id(0) > 0)
        yc = (b * cv * _silu(gc)).astype(BF16)
        mix = jnp.concatenate([ya_ref[...], yc], axis=1)
        wo = wo_ref[...]
        fg = fg_ref[...]
        h = x_ref[...] + _nn(mix, wo)
        r = lax.rsqrt(jnp.mean(h * h, axis=-1, keepdims=True) + EPS)
        n = h * r
        err = n * fg - t_ref[...]
        loss_ref[...] += jnp.broadcast_to(
            0.5 * jnp.sum(jnp.mean(err * err, axis=-1, keepdims=True), axis=0, keepdims=True), (8, 128))
        gfg_ref[...] += jnp.sum(err * n, axis=0, keepdims=True) * (1.0 / D_MODEL)
        dyg = err * (fg * (1.0 / D_MODEL))
        dh = r * (dyg - n * jnp.mean(dyg * n, axis=-1, keepdims=True))
        dh_ref[...] = dh
        dhb = dh.astype(BF16)
        dmix_ref[...] = _nt(dhb, wo).astype(ACT)
        gwo_ref[...] += _tn(mix, dhb)

    row = lambda i: (i, 0)
    fixed = lambda i: (0, 0)
    return pl.pallas_call(
        body,
        name="out_loss",
        grid=(S // TM,),
        in_specs=[
            pl.BlockSpec((TM, D_MODEL), row),
            pl.BlockSpec((TM, D_MODEL), row),
            pl.BlockSpec((TM, ATTN_W), row),
            pl.BlockSpec((TM, PC_W), row),
            _prev_rows(PC_W),
            pl.BlockSpec((CONV_K, CONV_W), fixed),
            pl.BlockSpec((D_MODEL, D_MODEL), fixed),
            pl.BlockSpec((1, D_MODEL), fixed),
        ],
        out_specs=[
            pl.BlockSpec((TM, D_MODEL), row),
            pl.BlockSpec((TM, D_MODEL), row),
            pl.BlockSpec((D_MODEL, D_MODEL), fixed),
            pl.BlockSpec((1, D_MODEL), fixed),
            pl.BlockSpec((8, 128), fixed),
        ],
        out_shape=[
            jax.ShapeDtypeStruct((S, D_MODEL), F32),
            jax.ShapeDtypeStruct((S, D_MODEL), ACT),
            jax.ShapeDtypeStruct((D_MODEL, D_MODEL), F32),
            jax.ShapeDtypeStruct((1, D_MODEL), F32),
            jax.ShapeDtypeStruct((8, 128), F32),
        ],
        compiler_params=_params(("arbitrary",)),
    )(x, target, ya, pc, pc, conv_w, w_out, final_g)


def _attn_bwd(pa, dmix, att, probs, psinks, q_stack, tab):
    S = pa.shape[0]
    nt = S // TQ
    nb = TQ // BLK

    def body(g_ref, kvc_ref, kvp_ref, tabc_ref, tabp_ref, dm_ref, att_ref, pm_ref, ps_ref, qs_ref,
             d_ref, dsink_ref, kall, dkv, carry, do_sc, p_sc, ds_sc, dsink_acc):
        step = pl.program_id(0)

        @pl.when(step == 0)
        def _():
            carry[...] = jnp.zeros_like(carry)
            dsink_acc[...] = jnp.zeros_like(dsink_acc)

        _fill_kv(kall, kvc_ref, kvp_ref, tabc_ref, tabp_ref)
        dkv[0:TQ, :] = jnp.zeros((TQ, 2 * KV_W), F32)
        dkv[TQ:TQ + BLK, :] = carry[...]
        lane = lax.broadcasted_iota(jnp.int32, (BLK, 128), 1)
        half = [lane < HEAD_DIM, lane >= HEAD_DIM]
        upper = _upper()
        upper_b = upper.astype(BF16)
        for j in range(nb):
            rq = slice(j * BLK, (j + 1) * BLK)
            rk = slice(j * BLK, (j + 2) * BLK)
            tab = tabc_ref[rq, :]
            pair = [slice(p * 128, (p + 1) * 128) for p in range(4)]
            g = [g_ref[rq, c].astype(F32) for c in pair]
            da = [dm_ref[rq, c].astype(F32) for c in pair]
            gate = [_silu_and_grad(g[p]) for p in range(4)]
            do = [da[p] * gate[p][0] for p in range(4)]
            dqs, dks, dvs = [], [], []
            for s in range(2):
                kk = kall[s, rk, :]
                vv = kall[2 + s, rk, :]
                _stack_heads(do_sc, s, half, do)
                pb = pm_ref[(2 * j + s) * BLK:(2 * j + s + 1) * BLK, :]
                prob = pb.astype(F32)
                _split_store(p_sc, s, upper_b, pb)
                dprob = _merge(upper, _nt(vv, do_sc[s]))
                dsum = jnp.sum(dprob * prob, axis=0, keepdims=True)
                _split_store(ds_sc, s, upper_b, (prob * (dprob - dsum)).astype(BF16))
                dsink_acc[s, 0:1, :] += ps_ref[2 * j + s:2 * j + s + 1, :] * dsum
                dqs.append(_tn(ds_sc[s], kk))
                dks.append(_nn(ds_sc[s], qs_ref[2 * j + s]))
                dvs.append(_nn(p_sc[s], do_sc[s]))
            for p in range(4):
                d_ref[rq, pair[p]] = _rope_t(_unstack_pair(half, dqs, p) * 0.125, tab).astype(BF16)
                d_ref[rq, 512 + p * 128:512 + (p + 1) * 128] = (
                    da[p] * att_ref[rq, pair[p]].astype(F32) * gate[p][1]).astype(BF16)
            dkv[rk, 0:128] += dks[0] + pltpu.roll(dks[1], 64, 1)
            dkv[rk, 128:256] += dvs[0] + pltpu.roll(dvs[1], 64, 1)
        d_ref[:, 1024:1152] = _rope_t(dkv[BLK:BLK + TQ, 0:128], tabc_ref[...]).astype(BF16)
        d_ref[:, 1152:1280] = dkv[BLK:BLK + TQ, 128:256].astype(BF16)
        carry[...] = dkv[0:BLK, :]

        @pl.when(step == nt - 1)
        def _():
            lanes = lax.broadcasted_iota(jnp.int32, (8, 128), 1)
            out = jnp.zeros((8, 128), F32)
            for s in range(2):
                for a, (p, e) in enumerate(HEADS[s]):
                    tot = jnp.sum(dsink_acc[s, 0:1, a * BLK:(a + 1) * BLK], axis=1, keepdims=True)
                    out = jnp.where(lanes == 2 * p + e, -tot, out)
            dsink_ref[...] = out

    rev = lambda s: nt - 1 - s
    return pl.pallas_call(
        body,
        name="attn_bwd",
        grid=(nt,),
        in_specs=_attn_specs(rev)[2:] + [pl.BlockSpec((TQ, ATTN_W), lambda s: (nt - 1 - s, 0))] * 2 + [
            pl.BlockSpec((2 * TQ, 4 * BLK), lambda s: (nt - 1 - s, 0)),
            pl.BlockSpec((2 * nb, 4 * BLK), lambda s: (nt - 1 - s, 0)),
            pl.BlockSpec((2 * nb, 4 * BLK, 128), lambda s: (nt - 1 - s, 0, 0)),
        ],
        out_specs=[
            pl.BlockSpec((TQ, PA_W), lambda s: (nt - 1 - s, 0)),
            pl.BlockSpec((8, 128), lambda s: (0, 0)),
        ],
        out_shape=[
            jax.ShapeDtypeStruct((S, PA_W), BF16),
            jax.ShapeDtypeStruct((8, 128), F32),
        ],
        scratch_shapes=[
            pltpu.VMEM((4, BLK + TQ, 128), BF16),
            pltpu.VMEM((BLK + TQ, 2 * KV_W), F32),
            pltpu.VMEM((BLK, 2 * KV_W), F32),
            pltpu.VMEM((2, 4 * BLK, 128), BF16),
            pltpu.VMEM((2, 2 * BLK, 4 * BLK), BF16),
            pltpu.VMEM((2, 2 * BLK, 4 * BLK), BF16),
            pltpu.VMEM((2, 8, 4 * BLK), F32),
        ],
        compiler_params=_params(("arbitrary",)),
    )(pa, pa, pa, tab, tab, dmix, att, probs, psinks, q_stack)


def _conv_bwd_tile(pc_ref, prev_ref, next_ref, dm_ref, dmn_ref, w_ref, d_ref, gw_ref, has_prev, has_next,
                   on_piece):
    rows = pc_ref.shape[0]
    w0, w1, w2 = w_ref[0:1, :], w_ref[1:2, :], w_ref[2:3, :]
    b, c, hh, gc, u, um1, um2, cv = _conv_tile(pc_ref, prev_ref, w_ref, has_prev)
    sg, dsg = _silu_and_grad(gc)
    dy = dm_ref[...].astype(F32)
    dyb = dy * b
    dcv = dyb * sg

    def next_dcv(r):
        nd = (dmn_ref[r:r + 1, :].astype(F32) * next_ref[r:r + 1, 0:512].astype(F32)
              * _silu(next_ref[r:r + 1, 1536:2048].astype(F32)))
        return jnp.where(has_next, nd, 0.0)

    row = lax.broadcasted_iota(jnp.int32, (rows, CONV_W), 0)
    dp1 = jnp.where(row == rows - 1, next_dcv(0), pltpu.roll(dcv, rows - 1, 0))
    dp2 = jnp.where(row == rows - 1, next_dcv(1),
                    jnp.where(row == rows - 2, next_dcv(0), pltpu.roll(dcv, rows - 2, 0)))
    du = w2 * dcv + w1 * dp1 + w0 * dp2
    pieces = (lambda: dy * cv * sg, lambda: du * hh, lambda: du * c, lambda: dyb * cv * dsg)
    for k, piece in enumerate(pieces):
        d_ref[:, k * CONV_W:(k + 1) * CONV_W] = piece().astype(BF16)
        on_piece(k)
    gw_ref[0:1, :] += jnp.sum(dcv * um2, axis=0, keepdims=True)
    gw_ref[1:2, :] += jnp.sum(dcv * um1, axis=0, keepdims=True)
    gw_ref[2:3, :] += jnp.sum(dcv * u, axis=0, keepdims=True)


def _grad_x(da, dc, wt, x, dh, norm_g, small, grads):
    S = x.shape[0]
    n_steps = S // TM
    rs = _ReduceScatter(grads)
    n_rs_out = len(rs.out_shape())
    small_rows = 8 + small.shape[0]

    def body(da_ref, dc_ref, wt_ref, x_ref, dh_ref, g_ref, small_ref, *rest):
        grad_refs, rest = rest[:rs.n], rest[rs.n:]
        gx_ref, all_ref = rest[:2]
        rs_out, rest = rest[2:2 + n_rs_out], rest[2 + n_rs_out:]
        gng, stage, small_send, small_recv, small_own = rest[:5]
        rs_scratch = rest[5:]
        step = pl.program_id(0)
        finish = rs.emit(step, n_steps, grad_refs, rs_out, rs_scratch)

        @pl.when(step == 0)
        def _():
            gng[...] = jnp.zeros_like(gng)

        dxn = (_nn(da_ref[:, 0:512], wt_ref[0:512, :]) + _nn(da_ref[:, 512:1024], wt_ref[768:1280, :])
               + _nn(da_ref[:, 1024:1280], wt_ref[512:768, :]) + _nn(dc_ref[...], wt_ref[1280:3328, :]))
        xv = x_ref[...]
        r = lax.rsqrt(jnp.mean(xv * xv, axis=-1, keepdims=True) + EPS)
        n = xv * r
        gng[...] += jnp.sum(dxn * n, axis=0, keepdims=True)
        dxg = dxn * g_ref[...]
        gx_ref[...] = dh_ref[...] + r * (dxg - n * jnp.mean(dxg * n, axis=-1, keepdims=True))

        @pl.when(step == n_steps - 1)
        def _():
            x_, y_, c_ = lax.axis_index("x"), lax.axis_index("y"), lax.axis_index("c")
            me = 4 * x_ + 2 * y_ + c_
            for q in range(8):
                stage[q:q + 1, :] = gng[:, q * 128:(q + 1) * 128]
            stage[8:small_rows, :] = small_ref[...]
            own = pltpu.make_async_copy(stage, all_ref.at[me], small_own)
            own.start()
            sends = []
            for k in range(1, N_DEV):
                cp = pltpu.make_async_remote_copy(
                    src_ref=stage, dst_ref=all_ref.at[me],
                    send_sem=small_send.at[k - 1], recv_sem=small_recv.at[k - 1],
                    device_id=(x_ ^ (k >> 2), y_ ^ ((k >> 1) & 1), c_ ^ (k & 1)), device_id_type=MESH)
                cp.start()
                sends.append(cp)
            for cp in sends:
                cp.wait_send()
                cp.wait_recv()
            own.wait()

        finish()

    row = lambda i: (i, 0)
    fixed = lambda i: (0, 0)
    any_spec = pl.BlockSpec(memory_space=pl.ANY)
    outs = pl.pallas_call(
        body,
        name="grad_x_reduce_scatter",
        grid=(n_steps,),
        in_specs=[
            pl.BlockSpec((TM, PA_W), row),
            pl.BlockSpec((TM, PC_W), row),
            pl.BlockSpec((IN_W, D_MODEL), fixed),
            pl.BlockSpec((TM, D_MODEL), row),
            pl.BlockSpec((TM, D_MODEL), row),
            pl.BlockSpec((1, D_MODEL), fixed),
            pl.BlockSpec(small.shape, fixed),
        ] + [any_spec] * rs.n,
        out_specs=[pl.BlockSpec((TM, D_MODEL), row), any_spec] + [any_spec] * n_rs_out,
        out_shape=[jax.ShapeDtypeStruct((S, D_MODEL), F32),
                   jax.ShapeDtypeStruct((N_DEV, small_rows, 128), F32)] + rs.out_shape(),
        scratch_shapes=[
            pltpu.VMEM((1, D_MODEL), F32),
            pltpu.VMEM((small_rows, 128), F32),
            pltpu.SemaphoreType.DMA((N_DEV - 1,)),
            pltpu.SemaphoreType.DMA((N_DEV - 1,)),
            pltpu.SemaphoreType.DMA,
        ] + rs.scratch_shapes(),
        compiler_params=_params(("arbitrary",)),
    )(da, dc, wt, x, dh, norm_g, small, *grads)
    return outs[0], outs[1], outs[2:2 + rs.n], outs[2 + rs.n:2 + 2 * rs.n]


def _grad_w_in(da, pc, dmix, conv_w, xn):
    S = xn.shape[0]
    tm = 2 * TM
    nt = S // tm
    t16 = tm // HALO

    def body(da_ref, pc_ref, prev_ref, next_ref, dm_ref, dmn_ref, cw_ref, xn_ref, gw_ref, dc_ref, gcw_ref):
        i = pl.program_id(0)

        @pl.when(i == 0)
        def _():
            gw_ref[...] = jnp.zeros_like(gw_ref)
            gcw_ref[...] = jnp.zeros_like(gcw_ref)

        xn = xn_ref[...]
        gw_ref[0:512, :] += _tn(da_ref[:, 0:512], xn)
        gw_ref[768:1280, :] += _tn(da_ref[:, 512:1024], xn)
        gw_ref[512:768, :] += _tn(da_ref[:, 1024:1280], xn)
        def piece_grad(k):
            rows = slice(PA_W + k * CONV_W, PA_W + (k + 1) * CONV_W)
            gw_ref[rows, :] += _tn(dc_ref[:, k * CONV_W:(k + 1) * CONV_W], xn)

        _conv_bwd_tile(pc_ref, prev_ref, next_ref, dm_ref, dmn_ref, cw_ref, dc_ref, gcw_ref, i > 0, i < nt - 1,
                       piece_grad)

    row = lambda i: (i, 0)
    fixed = lambda i: (0, 0)
    nxt = lambda i: jnp.minimum((i + 1) * t16, nt * t16 - 1)
    return pl.pallas_call(
        body,
        name="grad_w_in",
        grid=(nt,),
        in_specs=[
            pl.BlockSpec((tm, PA_W), row),
            pl.BlockSpec((tm, PC_W), row),
            pl.BlockSpec((HALO, PC_W), lambda i: (jnp.maximum(i * t16 - 1, 0), 0)),
            pl.BlockSpec((HALO, PC_W), lambda i: (nxt(i), 0)),
            pl.BlockSpec((tm, CONV_W), lambda i: (i, 1)),
            pl.BlockSpec((HALO, CONV_W), lambda i: (nxt(i), 1)),
            pl.BlockSpec((CONV_K, CONV_W), fixed),
            pl.BlockSpec((tm, D_MODEL), row),
        ],
        out_specs=[
            pl.BlockSpec((IN_W, D_MODEL), fixed, pipeline_mode=pl.Buffered(1)),
            pl.BlockSpec((tm, PC_W), row),
            pl.BlockSpec((CONV_K, CONV_W), fixed),
        ],
        out_shape=[
            jax.ShapeDtypeStruct((IN_W, D_MODEL), F32),
            jax.ShapeDtypeStruct((S, PC_W), BF16),
            jax.ShapeDtypeStruct((CONV_K, CONV_W), F32),
        ],
        compiler_params=_params(("arbitrary",)),
    )(da, pc, pc, pc, dmix, dmix, conv_w, xn)


def _adam_update(w, g, m, v):
    c1 = 1.0 - ADAM_B1 ** ADAM_STEP
    c2 = 1.0 - ADAM_B2 ** ADAM_STEP
    nm = ADAM_B1 * m + (1.0 - ADAM_B1) * g
    nv = ADAM_B2 * v + (1.0 - ADAM_B2) * (g * g)
    return -ADAM_LR * ((nm / c1) / (jnp.sqrt(nv / c2) + ADAM_EPS) + ADAM_WD * w), nm, nv


def _sum_chips_adamw(own, others, w, m, v, name):
    def body(own_ref, p_ref, w_ref, m_ref, v_ref, g_ref, d_ref, nm_ref, nv_ref):
        g = own_ref[...]
        for k in range(N_CHIP - 1):
            g = g + p_ref[k].astype(F32)
        g_ref[...] = g
        d_ref[...], nm_ref[...], nv_ref[...] = _adam_update(w_ref[...], g, m_ref[...], v_ref[...])

    rows, cols = w.shape
    half = rows // 2
    blk = pl.BlockSpec((half, cols), lambda i: (i, 0))
    shape = jax.ShapeDtypeStruct(w.shape, F32)
    return pl.pallas_call(
        body,
        name=name,
        grid=(2,),
        in_specs=[blk, pl.BlockSpec((N_CHIP - 1, half, cols), lambda i: (0, i, 0)), blk, blk, blk],
        out_specs=[blk] * 4,
        out_shape=[shape] * 4,
        compiler_params=_params(("arbitrary",)),
    )(own, others, w, m, v)


SMALL_ROWS = 96


def _small_adamw(parts, params):
    def body(parts_ref, *rest):
        prm, outs, total = rest[:12], rest[12:29], rest[29]
        me = 4 * lax.axis_index("x") + 2 * lax.axis_index("y") + lax.axis_index("c")
        acc = parts_ref[0]
        for d in range(1, N_DEV):
            acc = acc + parts_ref[d]
        total[...] = acc
        grads = (total[0:8, :], total[8:16, :], total[16:17, 0:8],
                 total[pl.ds(pl.multiple_of(32 + me * 8, 8), CONV_K), 0:64])
        outs[0][...] = total[24:25, 0:1]
        for k, g in enumerate(grads):
            w_ref, m_ref, v_ref = prm[3 * k:3 * k + 3]
            g_ref, d_ref, nm_ref, nv_ref = outs[1 + 4 * k:5 + 4 * k]
            g_ref[...] = g
            d_ref[...], nm_ref[...], nv_ref[...] = _adam_update(w_ref[...], g, m_ref[...], v_ref[...])

    flat = [a for p in params for a in p]
    out_shape = [jax.ShapeDtypeStruct((1, 1), F32)]
    for p in params:
        out_shape += [jax.ShapeDtypeStruct(p[0].shape, F32)] * 4
    return pl.pallas_call(
        body,
        name="adamw_small",
        out_shape=out_shape,
        scratch_shapes=[pltpu.VMEM((SMALL_ROWS, 128), F32)],
        compiler_params=_params(),
    )(parts, *flat)


def kernel(x, norm_g, w_in, sinks, conv_w, w_out, final_g, loss_target, m_norm_g, m_w_in, m_sinks, m_conv_w, m_w_out, m_final_g, v_norm_g, v_w_in, v_sinks, v_conv_w, v_w_out, v_final_g):
    S = x.shape[1]
    x2 = x.reshape(S, D_MODEL)
    t2 = loss_target.reshape(S, D_MODEL)
    ng = norm_g.reshape(1, D_MODEL)
    fg = final_g.reshape(1, D_MODEL)

    cw_pad = jnp.zeros((8, 128), F32).at[0:CONV_K, 0:64].set(conv_w)
    xn, tab, wt = _prologue(x2, ng, w_in.T.astype(BF16))
    pa, pc, (wo, cw_all) = _fwd_proj(xn, wt, [w_out.astype(BF16), cw_pad])
    cw = cw_all.reshape(N_DEV, 8, 128)[:, 0:CONV_K, 0:64].transpose(1, 0, 2).reshape(CONV_K, CONV_W)
    ya, att, probs, psinks, q_stack = _attn_fwd(pa, tab, sinks)
    dh, dmix, g_wo, g_fg, loss_part = _out_loss(x2, t2, ya, pc, cw, wo, fg)
    da, g_sinks = _attn_bwd(pa, dmix, att, probs, psinks, q_stack, tab)
    g_wt, dc, g_cw = _grad_w_in(da, pc, dmix, cw, xn)
    cw_pack = jnp.pad(g_cw.reshape(CONV_K, N_DEV, 64).transpose(1, 0, 2),
                      ((0, 0), (0, 8 - CONV_K), (0, 64))).reshape(N_DEV * 8, 128)
    small = jnp.concatenate([g_fg.reshape(8, 128), g_sinks, loss_part, cw_pack], axis=0)
    grad_x, parts, own, others = _grad_x(
        da, dc, wt, x2, dh, ng, small,
        [g_wt.reshape(N_DEV, SHARD_IN, D_MODEL), g_wo.reshape(N_DEV, SHARD_OUT, D_MODEL)])
    gt, dt, nmt, nvt = _sum_chips_adamw(own[0], others[0], w_in.T, m_w_in.T, v_w_in.T, "adamw_w_in")
    grad_w_in, d_w_in, nm_w_in, nv_w_in = gt.T, dt.T, nmt.T, nvt.T
    grad_w_out, d_w_out, nm_w_out, nv_w_out = _sum_chips_adamw(
        own[1], others[1], w_out, m_w_out, v_w_out, "adamw_w_out")
    vec = lambda a: a.reshape(8, 128)
    row = lambda a: a.reshape(1, 8)
    res = _small_adamw(parts, [
        (vec(norm_g), vec(m_norm_g), vec(v_norm_g)), (vec(final_g), vec(m_final_g), vec(v_final_g)),
        (row(sinks), row(m_sinks), row(v_sinks)), (conv_w, m_conv_w, v_conv_w)])
    loss = res[0].reshape(())
    grad_norm_g, d_ng, nm_ng, nv_ng = [a.reshape(D_MODEL) for a in res[1:5]]
    grad_final_g, d_fg, nm_fg, nv_fg = [a.reshape(D_MODEL) for a in res[5:9]]
    grad_sinks, d_sk, nm_sk, nv_sk = [a.reshape(N_Q_HEADS) for a in res[9:13]]
    grad_conv_w, d_cw, nm_cw, nv_cw = res[13:17]

    return (loss, grad_x.reshape(1, S, D_MODEL), grad_norm_g, grad_w_in, grad_sinks, grad_conv_w, grad_w_out, grad_final_g,
            d_ng, d_w_in, d_sk, d_cw, d_w_out, d_fg,
            nm_ng, nm_w_in, nm_sk, nm_cw, nm_w_out, nm_fg,
            nv_ng, nv_w_in, nv_sk, nv_cw, nv_w_out, nv_fg)
```

```python
import jax
import jax.numpy as jnp
from jax import lax
from jax.experimental import pallas as pl
from jax.experimental.pallas import tpu as pltpu

F32 = jnp.float32
BF16 = jnp.bfloat16
MESH = pl.DeviceIdType.MESH

D_MODEL = 1024
HEAD_DIM = 64
N_Q_HEADS = 8
ATTN_W = 512
KV_W = 128
BLK = 128
CONV_W = 512
CONV_K = 3
IN_W = 3328
PA_W = 1280
PC_W = 2048
EPS = 1e-5
ROPE_THETA = 500000.0
ROT_DIM = 16
N_DEV = 8
N_CHIP = 4
SHARD_IN = IN_W // N_DEV
SHARD_OUT = D_MODEL // N_DEV

ADAM_LR = 0.001
ADAM_B1 = 0.9
ADAM_B2 = 0.999
ADAM_EPS = 1e-08
ADAM_WD = 0.01
ADAM_STEP = 10

ACT = jnp.bfloat16

TM = 512
TQ = 1024
HALO = 16
VMEM_LIMIT = 56 * 1024 * 1024

NT_DIMS = (((1,), (1,)), ((), ()))
TN_DIMS = (((0,), (0,)), ((), ()))


def _params(sem=None):
    kw = dict(vmem_limit_bytes=VMEM_LIMIT)
    if sem is not None:
        kw["dimension_semantics"] = sem
    return pltpu.CompilerParams(**kw)


def _nt(a, b):
    return lax.dot_general(a, b, NT_DIMS, preferred_element_type=F32)


def _tn(a, b):
    return lax.dot_general(a, b, TN_DIMS, preferred_element_type=F32)


def _nn(a, b):
    return jnp.dot(a, b, preferred_element_type=F32)


def _silu(g):
    return g * jax.nn.sigmoid(g)


def _silu_and_grad(g):
    s = jax.nn.sigmoid(g)
    return g * s, s * (1.0 + g * (1.0 - s))


class _AllGatherInSteps:
    def __init__(self, arrs, forward_step):
        self.blocks = [(a.shape, a.dtype) for a in arrs]
        self.n = len(arrs)
        self.forward_step = forward_step

    def out_shape(self):
        return [jax.ShapeDtypeStruct((N_DEV * s[0], s[1]), d) for s, d in self.blocks]

    def scratch_shapes(self):
        return [pltpu.SemaphoreType.DMA((7 * self.n,)), pltpu.SemaphoreType.DMA((7 * self.n,)),
                pltpu.SemaphoreType.DMA((self.n,))]

    def emit(self, step, n_steps, x_refs, out_refs, scratch):
        assert n_steps > self.forward_step + 1
        send_sems, recv_sems, local_sems = scratch
        x, y, c = lax.axis_index("x"), lax.axis_index("y"), lax.axis_index("c")
        me, sibling = (x, y, c), (x, y, 1 - c)
        chips = [(1 - x, y), (x, 1 - y), (1 - x, 1 - y)]

        def rows(a, px, py, pc):
            m = self.blocks[a][0][0]
            return out_refs[a].at[pl.ds((4 * px + 2 * py + pc) * m, m), :]

        def copy(a, k, block, to, src=None):
            return pltpu.make_async_remote_copy(
                src_ref=rows(a, *block) if src is None else src, dst_ref=rows(a, *block),
                send_sem=send_sems.at[a * 7 + k], recv_sem=recv_sems.at[a * 7 + k],
                device_id=to, device_id_type=MESH)

        def mine(a):
            return pltpu.make_async_copy(x_refs[a], rows(a, *me), local_sems.at[a])

        def first(a):
            return ([copy(a, 0, me, sibling, src=x_refs[a])]
                    + [copy(a, 1 + j, me, (*chip, c), src=x_refs[a]) for j, chip in enumerate(chips)])

        def passed(a):
            return [copy(a, 4 + j, (*chip, c), sibling) for j, chip in enumerate(chips)]

        @pl.when(step == 0)
        def _():
            for a in range(self.n):
                mine(a).start()
                for cp in first(a):
                    cp.start()

        @pl.when(step == self.forward_step)
        def _():
            for j, chip in enumerate(chips):
                for a in range(self.n):
                    copy(a, 1 + j, (*chip, c), me).wait_recv()
                    copy(a, 4 + j, (*chip, c), sibling).start()

        def finish():
            @pl.when(step == n_steps - 1)
            def _():
                for a in range(self.n):
                    copy(a, 0, sibling, me).wait_recv()
                    for j, chip in enumerate(chips):
                        copy(a, 4 + j, (*chip, 1 - c), me).wait_recv()
                    for cp in first(a) + passed(a):
                        cp.wait_send()
                    mine(a).wait()

        return finish


class _AllGatherViaNeighbours:
    def __init__(self, arr, first, mid, second):
        (self.m, self.ncol), self.dtype = arr.shape, arr.dtype
        assert self.m % 32 == 0
        self.first, self.mid, self.second = first, mid, second

    def out_shape(self):
        return [jax.ShapeDtypeStruct((N_DEV * self.m, self.ncol), self.dtype)]

    def scratch_shapes(self):
        return [pltpu.SemaphoreType.DMA((11,)), pltpu.SemaphoreType.DMA((11,)), pltpu.SemaphoreType.DMA]

    def emit(self, step, n_steps, x_ref, out_ref, scratch):
        assert 0 < self.first < self.mid < self.second < n_steps - 1
        send_sems, recv_sems, local_sem = scratch
        x, y, c = lax.axis_index("x"), lax.axis_index("y"), lax.axis_index("c")
        half = self.m // 2
        sibling, xn, yn = (x, y, 1 - c), (1 - x, y, c), (x, 1 - y, c)

        def rows(dev, part=None):
            px, py, pc = dev
            base = (4 * px + 2 * py + pc) * self.m
            if part is None:
                return out_ref.at[pl.ds(base, self.m), :]
            return out_ref.at[pl.ds(base + part * half, half), :]

        def copy(k, dev, to, part=None, src=None):
            return pltpu.make_async_remote_copy(
                src_ref=rows(dev, part) if src is None else src, dst_ref=rows(dev, part),
                send_sem=send_sems.at[k], recv_sem=recv_sems.at[k], device_id=to, device_id_type=MESH)

        me, dg = (x, y, c), (1 - x, 1 - y, c)
        mine = pltpu.make_async_copy(x_ref, rows(me), local_sem)
        my_half = lambda part: x_ref.at[pl.ds(part * half, half), :]
        sends = [
            copy(0, me, sibling, src=x_ref), copy(1, me, xn, part=0, src=my_half(0)),
            copy(2, me, yn, part=1, src=my_half(1)), copy(3, xn, yn, part=0), copy(4, yn, xn, part=1),
            copy(5, xn, sibling), copy(6, yn, sibling), copy(7, dg, sibling, part=0), copy(8, dg, sibling, part=1),
            copy(9, me, xn, part=1, src=my_half(1)), copy(10, me, yn, part=0, src=my_half(0)),
        ]
        other = lambda dev: (dev[0], dev[1], 1 - c)
        arrivals = [
            copy(0, other(me), sibling), copy(1, xn, xn, part=0), copy(2, yn, yn, part=1), copy(3, dg, yn, part=0),
            copy(4, dg, xn, part=1), copy(5, other(xn), sibling), copy(6, other(yn), sibling),
            copy(7, other(dg), sibling, part=0), copy(8, other(dg), sibling, part=1),
            copy(9, xn, xn, part=1), copy(10, yn, yn, part=0),
        ]

        @pl.when(step == 0)
        def _():
            mine.start()
            for k in (0, 1, 2, 9, 10):
                sends[k].start()

        @pl.when(step == self.first)
        def _():
            arrivals[1].wait_recv()
            sends[3].start()
            arrivals[2].wait_recv()
            sends[4].start()

        @pl.when(step == self.mid)
        def _():
            arrivals[9].wait_recv()
            sends[5].start()
            arrivals[10].wait_recv()
            sends[6].start()

        @pl.when(step == self.second)
        def _():
            arrivals[3].wait_recv()
            sends[7].start()
            arrivals[4].wait_recv()
            sends[8].start()

        def finish():
            @pl.when(step == n_steps - 1)
            def _():
                for k in (0, 5, 6, 7, 8):
                    arrivals[k].wait_recv()
                for cp in sends:
                    cp.wait_send()
                mine.wait()

        return finish


class _ReduceScatter:
    def __init__(self, grads):
        self.shapes = [g.shape[1:] for g in grads]
        self.n = len(grads)
        self.items = tuple((a, r) for r in (1, 2, 3, 0) for a in range(self.n))
        self.steps = N_CHIP + 2

    def out_shape(self):
        own = [jax.ShapeDtypeStruct(s, F32) for s in self.shapes]
        ici = [jax.ShapeDtypeStruct((N_CHIP - 1,) + s, BF16) for s in self.shapes]
        land = [jax.ShapeDtypeStruct((N_CHIP,) + s, F32) for s in self.shapes]
        return own + ici + land

    def scratch_shapes(self):
        n_items = len(self.items)
        return ([pltpu.VMEM((2,) + s, F32) for s in self.shapes]
                + [pltpu.VMEM((N_CHIP - 1,) + s, BF16) for s in self.shapes]
                + [pltpu.VMEM(s, F32) for s in self.shapes]
                + [pltpu.SemaphoreType.DMA((self.n * N_CHIP,))] * 2
                + [pltpu.SemaphoreType.DMA((2 * n_items,))]
                + [pltpu.SemaphoreType.DMA((self.n * (N_CHIP - 1),))] * 2
                + [pltpu.SemaphoreType.DMA((self.n,))])

    def emit(self, step, n_steps, g_refs, out_refs, scratch):
        assert n_steps > self.steps
        n = self.n
        own_refs, ici_refs, land_refs = out_refs[:n], out_refs[n:2 * n], out_refs[2 * n:]
        stage, pair_bf, pair_own = scratch[:n], scratch[n:2 * n], scratch[2 * n:3 * n]
        sib_send, sib_recv, load_sems, ici_send, ici_recv, own_sems = scratch[3 * n:]
        x, y, c = lax.axis_index("x"), lax.axis_index("y"), lax.axis_index("c")

        def chip_of(r):
            return (x ^ (r >> 1), y ^ (r & 1))

        def block_of(r, core):
            cx, cy = chip_of(r)
            return 4 * cx + 2 * cy + core

        def to_sibling(a, r):
            return pltpu.make_async_remote_copy(
                src_ref=g_refs[a].at[block_of(r, 1 - c)], dst_ref=land_refs[a].at[r],
                send_sem=sib_send.at[a * N_CHIP + r], recv_sem=sib_recv.at[a * N_CHIP + r],
                device_id=(x, y, 1 - c), device_id_type=MESH)

        def loads(k):
            a, r = self.items[k]
            return (pltpu.make_async_copy(g_refs[a].at[block_of(r, c)], stage[a].at[0], load_sems.at[2 * k]),
                    pltpu.make_async_copy(land_refs[a].at[r], stage[a].at[1], load_sems.at[2 * k + 1]))

        def to_owner(k):
            a, r = self.items[k]
            if r == 0:
                return pltpu.make_async_copy(pair_own[a], own_refs[a], own_sems.at[a])
            return pltpu.make_async_remote_copy(
                src_ref=pair_bf[a].at[r - 1], dst_ref=ici_refs[a].at[r - 1],
                send_sem=ici_send.at[a * (N_CHIP - 1) + r - 1], recv_sem=ici_recv.at[a * (N_CHIP - 1) + r - 1],
                device_id=(*chip_of(r), c), device_id_type=MESH)

        @pl.when(step == 0)
        def _():
            for a, r in self.items:
                to_sibling(a, r).start()

        def fetch(k):
            a, r = self.items[k]
            to_sibling(a, r).wait_recv()
            for cp in loads(k):
                cp.start()

        def add_and_send(k):
            a, r = self.items[k]
            for cp in loads(k):
                cp.wait()
            total = stage[a][0] + stage[a][1]
            if r == 0:
                pair_own[a][...] = total
            else:
                pair_bf[a][r - 1] = total.astype(BF16)
            to_owner(k).start()

        for g in range(N_CHIP + 1):
            @pl.when(step == 1 + g)
            def _(g=g):
                if g > 0:
                    for k in range((g - 1) * n, g * n):
                        add_and_send(k)
                if g < N_CHIP:
                    for k in range(g * n, (g + 1) * n):
                        fetch(k)

        def finish():
            @pl.when(step == n_steps - 1)
            def _():
                for k, (a, r) in enumerate(self.items):
                    if r == 0:
                        to_owner(k).wait()
                    else:
                        to_owner(k).wait_send()
                        to_owner(k).wait_recv()
                for a, r in self.items:
                    to_sibling(a, r).wait_send()

        return finish


def _prologue(x, norm_g, w_shard):
    S = x.shape[0]
    n_steps = S // TM
    half = ROT_DIM // 2
    pos = jnp.arange(S, dtype=jnp.int32).astype(F32)
    inv_freq = ROPE_THETA ** (-jnp.arange(0, ROT_DIM, 2, dtype=F32) / ROT_DIM)
    ang = inv_freq[:, None] * pos[None, :]
    cs = jnp.concatenate([jnp.cos(ang), jnp.sin(ang)], axis=0)
    ag = _AllGatherViaNeighbours(w_shard, first=n_steps // 4, mid=n_steps // 2 + 2, second=n_steps - 2)

    def body(x_ref, g_ref, cs_ref, w_ref, xn_ref, tab_ref, wt_ref, *ag_scratch):
        step = pl.program_id(0)
        finish = ag.emit(step, n_steps, w_ref, wt_ref, ag_scratch)
        xv = x_ref[...]
        r = lax.rsqrt(jnp.mean(xv * xv, axis=-1, keepdims=True) + EPS)
        xn_ref[...] = (xv * r * g_ref[...]).astype(BF16)

        xt = jnp.concatenate([cs_ref[...], jnp.zeros((128 - 2 * half, TM), F32)], axis=0).T
        lane = lax.broadcasted_iota(jnp.int32, (TM, 128), 1)
        rr = lane & (HEAD_DIM - 1)
        first = lane < HEAD_DIM

        def at(shift_first, shift_second):
            return jnp.where(first, pltpu.roll(xt, shift_first, 1) if shift_first else xt,
                             pltpu.roll(xt, shift_second, 1))

        cos_lo, cos_hi = at(0, HEAD_DIM), at(half, HEAD_DIM + half)
        sin_lo, sin_hi = at(128 - half, HEAD_DIM - half), at(0, HEAD_DIM)
        tab_ref[:, 0:128] = jnp.where(rr < half, cos_lo, jnp.where(rr < ROT_DIM, cos_hi, 1.0))
        tab_ref[:, 128:256] = jnp.where(rr < half, -sin_lo, 0.0)
        tab_ref[:, 256:384] = jnp.where((rr >= half) & (rr < ROT_DIM), sin_hi, 0.0)
        finish()

    any_spec = pl.BlockSpec(memory_space=pl.ANY)
    return pl.pallas_call(
        body,
        name="prologue_all_gather_w_in",
        grid=(n_steps,),
        in_specs=[
            pl.BlockSpec((TM, D_MODEL), lambda i: (i, 0)),
            pl.BlockSpec((1, D_MODEL), lambda i: (0, 0)),
            pl.BlockSpec((2 * half, TM), lambda i: (0, i)),
            any_spec,
        ],
        out_specs=[
            pl.BlockSpec((TM, D_MODEL), lambda i: (i, 0)),
            pl.BlockSpec((TM, 384), lambda i: (i, 0)),
            any_spec,
        ],
        out_shape=[
            jax.ShapeDtypeStruct((S, D_MODEL), BF16),
            jax.ShapeDtypeStruct((S, 384), F32),
        ] + ag.out_shape(),
        scratch_shapes=ag.scratch_shapes(),
        compiler_params=_params(("arbitrary",)),
    )(x, norm_g, cs, w_shard)


def _fwd_proj(xn, wt, later):
    S = xn.shape[0]
    tm = 2 * TM
    n_steps = S // tm
    ag = _AllGatherInSteps(later, forward_step=n_steps // 2)

    def body(xn_ref, wt_ref, *rest):
        later_refs, rest = rest[:ag.n], rest[ag.n:]
        pa_ref, pc_ref = rest[:2]
        gathered, ag_scratch = rest[2:2 + ag.n], rest[2 + ag.n:]
        step = pl.program_id(0)
        finish = ag.emit(step, n_steps, later_refs, gathered, ag_scratch)
        xn = xn_ref[...]
        pa_ref[:, 0:512] = _nt(xn, wt_ref[0:512, :]).astype(ACT)
        pa_ref[:, 512:1024] = _nt(xn, wt_ref[768:1280, :]).astype(ACT)
        pa_ref[:, 1024:1280] = _nt(xn, wt_ref[512:768, :]).astype(ACT)
        pc_ref[...] = _nt(xn, wt_ref[1280:3328, :]).astype(ACT)
        finish()

    any_spec = pl.BlockSpec(memory_space=pl.ANY)
    outs = pl.pallas_call(
        body,
        name="fwd_proj_all_gather",
        grid=(n_steps,),
        in_specs=[
            pl.BlockSpec((tm, D_MODEL), lambda i: (i, 0)),
            pl.BlockSpec((IN_W, D_MODEL), lambda i: (0, 0)),
        ] + [any_spec] * ag.n,
        out_specs=[
            pl.BlockSpec((tm, PA_W), lambda i: (i, 0)),
            pl.BlockSpec((tm, PC_W), lambda i: (i, 0)),
        ] + [any_spec] * ag.n,
        out_shape=[
            jax.ShapeDtypeStruct((S, PA_W), ACT),
            jax.ShapeDtypeStruct((S, PC_W), ACT),
        ] + ag.out_shape(),
        scratch_shapes=ag.scratch_shapes(),
        compiler_params=_params(("arbitrary",)),
    )(xn, wt, *later)
    return outs[0], outs[1], outs[2:]


def _rope(t, tab):
    return (t * tab[:, 0:128] + pltpu.roll(t, 120, 1) * tab[:, 128:256]
            + pltpu.roll(t, 8, 1) * tab[:, 256:384])


def _rope_t(d, tab):
    return (d * tab[:, 0:128] + pltpu.roll(d * tab[:, 128:256], 8, 1)
            + pltpu.roll(d * tab[:, 256:384], 120, 1))


def _fill_kv(kall, kvc_ref, kvp_ref, tabc_ref, tabp_ref):
    for lo, kv_ref, tab_ref, n in ((0, kvp_ref, tabp_ref, BLK), (BLK, kvc_ref, tabc_ref, TQ)):
        k = _rope(kv_ref[:, 0:128].astype(F32), tab_ref[...])
        v = kv_ref[:, 128:256].astype(F32)
        kall[0, lo:lo + n, :] = k.astype(BF16)
        kall[1, lo:lo + n, :] = pltpu.roll(k, 64, 1).astype(BF16)
        kall[2, lo:lo + n, :] = v.astype(BF16)
        kall[3, lo:lo + n, :] = pltpu.roll(v, 64, 1).astype(BF16)


HEADS = (((0, 0), (1, 0), (2, 1), (3, 1)), ((0, 1), (1, 1), (2, 0), (3, 0)))


def _upper():
    kj = lax.broadcasted_iota(jnp.int32, (BLK, 4 * BLK), 0)
    qi = lax.broadcasted_iota(jnp.int32, (BLK, 4 * BLK), 1) & (BLK - 1)
    return kj > qi


def _merge(upper, both):
    return jnp.where(upper, both[0:BLK, :], both[BLK:2 * BLK, :])


def _split_store(ref, s, upper_b, vb):
    first = vb * upper_b
    ref[s, 0:BLK, :] = first
    ref[s, BLK:2 * BLK, :] = vb - first


def _sink_rows(sink_ref):
    return [jnp.concatenate([jnp.full((1, BLK), sink_ref[2 * p + e], F32) for p, e in HEADS[s]], axis=1)
            for s in range(2)]


def _stack_heads(ref, slot, half, pairs, s=None):
    for a, (p, e) in enumerate(HEADS[slot if s is None else s]):
        ref[slot, a * BLK:(a + 1) * BLK, :] = jnp.where(half[e], pairs[p], 0.0).astype(BF16)


def _unstack_pair(half, outs, p):
    lo = 0 if p < 2 else 1
    rows = slice(p * BLK, (p + 1) * BLK)
    return jnp.where(half[0], outs[lo][rows, :], outs[1 - lo][rows, :])


def _softmax(sm, sinks):
    m = jnp.maximum(jnp.max(sm, axis=0, keepdims=True), sinks)
    p = jnp.exp(sm - m)
    es = jnp.exp(sinks - m)
    inv = 1.0 / (jnp.sum(p, axis=0, keepdims=True) + es)
    return p * inv, es * inv


def _scores(kk, q_stack, first):
    st = _nt(kk, q_stack)
    prev = st[0:BLK, :]
    if first is not None:
        prev = prev + jnp.where(first, -jnp.inf, 0.0)
    return prev, st[BLK:2 * BLK, :]


def _attn_specs(tile):
    nb = TQ // BLK
    prev = lambda i: jnp.maximum(tile(i) * nb - 1, 0)
    return [
        pl.BlockSpec(memory_space=pltpu.SMEM),
        pl.BlockSpec((TQ, ATTN_W), lambda i: (tile(i), 0)),
        pl.BlockSpec((TQ, ATTN_W), lambda i: (tile(i), 1)),
        pl.BlockSpec((TQ, 2 * KV_W), lambda i: (tile(i), 4)),
        pl.BlockSpec((BLK, 2 * KV_W), lambda i: (prev(i), 4)),
        pl.BlockSpec((TQ, 384), lambda i: (tile(i), 0)),
        pl.BlockSpec((BLK, 384), lambda i: (prev(i), 0)),
    ]


def _attn_fwd(pa, tab, sinks):
    S = pa.shape[0]
    nb = TQ // BLK

    def body(sink_ref, q_ref, g_ref, kvc_ref, kvp_ref, tabc_ref, tabp_ref, o_ref, att_ref, pm_ref, ps_ref,
             qs_ref, kall, p_sc):
        i = pl.program_id(0)
        _fill_kv(kall, kvc_ref, kvp_ref, tabc_ref, tabp_ref)
        lane = lax.broadcasted_iota(jnp.int32, (BLK, 128), 1)
        half = [lane < HEAD_DIM, lane >= HEAD_DIM]
        upper = _upper()
        upper_b = upper.astype(BF16)
        sinks = _sink_rows(sink_ref)
        for j in range(nb):
            rq = slice(j * BLK, (j + 1) * BLK)
            rk = slice(j * BLK, (j + 2) * BLK)
            tab = tabc_ref[rq, :]
            qr = [_rope(q_ref[rq, p * 128:(p + 1) * 128].astype(F32), tab) * 0.125 for p in range(4)]
            outs = []
            for s in range(2):
                _stack_heads(qs_ref, 2 * j + s, half, qr, s)
                prev, cur = _scores(kall[s, rk, :], qs_ref[2 * j + s], i == 0 if j == 0 else None)
                prob, psink = _softmax(jnp.where(upper, prev, cur), sinks[s])
                pb = prob.astype(BF16)
                pm_ref[(2 * j + s) * BLK:(2 * j + s + 1) * BLK, :] = pb
                ps_ref[2 * j + s:2 * j + s + 1, :] = psink
                _split_store(p_sc, s, upper_b, pb)
                outs.append(_tn(p_sc[s], kall[2 + s, rk, :]))
            for p in range(4):
                cols = slice(p * 128, (p + 1) * 128)
                att = _unstack_pair(half, outs, p)
                att_ref[rq, cols] = att.astype(BF16)
                o_ref[rq, cols] = (att * _silu(g_ref[rq, cols].astype(F32))).astype(BF16)

    return pl.pallas_call(
        body,
        name="attn_fwd",
        grid=(S // TQ,),
        in_specs=_attn_specs(lambda i: i),
        out_specs=[pl.BlockSpec((TQ, ATTN_W), lambda i: (i, 0))] * 2 + [
            pl.BlockSpec((2 * TQ, 4 * BLK), lambda i: (i, 0)),
            pl.BlockSpec((2 * nb, 4 * BLK), lambda i: (i, 0)),
            pl.BlockSpec((2 * nb, 4 * BLK, 128), lambda i: (i, 0, 0)),
        ],
        out_shape=[jax.ShapeDtypeStruct((S, ATTN_W), BF16)] * 2 + [
            jax.ShapeDtypeStruct((2 * S, 4 * BLK), BF16),
            jax.ShapeDtypeStruct((2 * S // BLK, 4 * BLK), F32),
            jax.ShapeDtypeStruct((2 * S // BLK, 4 * BLK, 128), BF16),
        ],
        scratch_shapes=[
            pltpu.VMEM((4, BLK + TQ, 128), BF16),
            pltpu.VMEM((2, 2 * BLK, 4 * BLK), BF16),
        ],
        compiler_params=_params(("arbitrary",)),
    )(sinks, pa, pa, pa, pa, tab, tab)


def _shift_down(u, halo_ref, has_prev):
    def halo_u(r):
        hu = halo_ref[r:r + 1, 512:1024].astype(F32) * halo_ref[r:r + 1, 1024:1536].astype(F32)
        return jnp.where(has_prev, hu, 0.0)

    row = lax.broadcasted_iota(jnp.int32, u.shape, 0)
    um1 = jnp.where(row == 0, halo_u(HALO - 1), pltpu.roll(u, 1, 0))
    um2 = jnp.where(row == 0, halo_u(HALO - 2), jnp.where(row == 1, halo_u(HALO - 1), pltpu.roll(u, 2, 0)))
    return um1, um2


def _conv_tile(pc_ref, halo_ref, w_ref, has_prev):
    b = pc_ref[:, 0:512].astype(F32)
    c = pc_ref[:, 512:1024].astype(F32)
    hh = pc_ref[:, 1024:1536].astype(F32)
    gc = pc_ref[:, 1536:2048].astype(F32)
    u = c * hh
    um1, um2 = _shift_down(u, halo_ref, has_prev)
    cv = w_ref[0:1, :] * um2 + w_ref[1:2, :] * um1 + w_ref[2:3, :] * u
    return b, c, hh, gc, u, um1, um2, cv


def _prev_rows(width, col=0):
    return pl.BlockSpec((HALO, width), lambda i: (jnp.maximum(i * (TM // HALO) - 1, 0), col))


def _out_loss(x, target, ya, pc, conv_w, w_out, final_g):
    S = x.shape[0]

    def body(x_ref, t_ref, ya_ref, pc_ref, halo_ref, cw_ref, wo_ref, fg_ref,
             dh_ref, dmix_ref, gwo_ref, gfg_ref, loss_ref):
        @pl.when(pl.program_id(0) == 0)
        def _():
            gwo_ref[...] = jnp.zeros_like(gwo_ref)
            gfg_ref[...] = jnp.zeros_like(gfg_ref)
            loss_ref[...] = jnp.zeros_like(loss_ref)

        b, _, _, gc, _, _, _, cv = _conv_tile(pc_ref, halo_ref, cw_ref, pl.program_id(0) > 0)
        yc = (b * cv * _silu(gc)).astype(BF16)
        mix = jnp.concatenate([ya_ref[...], yc], axis=1)
        wo = wo_ref[...]
        fg = fg_ref[...]
        h = x_ref[...] + _nn(mix, wo)
        r = lax.rsqrt(jnp.mean(h * h, axis=-1, keepdims=True) + EPS)
        n = h * r
        err = n * fg - t_ref[...]
        loss_ref[...] += jnp.broadcast_to(
            0.5 * jnp.sum(jnp.mean(err * err, axis=-1, keepdims=True), axis=0, keepdims=True), (8, 128))
        gfg_ref[...] += jnp.sum(err * n, axis=0, keepdims=True) * (1.0 / D_MODEL)
        dyg = err * (fg * (1.0 / D_MODEL))
        dh = r * (dyg - n * jnp.mean(dyg * n, axis=-1, keepdims=True))
        dh_ref[...] = dh
        dhb = dh.astype(BF16)
        dmix_ref[...] = _nt(dhb, wo).astype(ACT)
        gwo_ref[...] += _tn(mix, dhb)

    row = lambda i: (i, 0)
    fixed = lambda i: (0, 0)
    return pl.pallas_call(
        body,
        name="out_loss",
        grid=(S // TM,),
        in_specs=[
            pl.BlockSpec((TM, D_MODEL), row),
            pl.BlockSpec((TM, D_MODEL), row),
            pl.BlockSpec((TM, ATTN_W), row),
            pl.BlockSpec((TM, PC_W), row),
            _prev_rows(PC_W),
            pl.BlockSpec((CONV_K, CONV_W), fixed),
            pl.BlockSpec((D_MODEL, D_MODEL), fixed),
            pl.BlockSpec((1, D_MODEL), fixed),
        ],
        out_specs=[
            pl.BlockSpec((TM, D_MODEL), row),
            pl.BlockSpec((TM, D_MODEL), row),
            pl.BlockSpec((D_MODEL, D_MODEL), fixed),
            pl.BlockSpec((1, D_MODEL), fixed),
            pl.BlockSpec((8, 128), fixed),
        ],
        out_shape=[
            jax.ShapeDtypeStruct((S, D_MODEL), F32),
            jax.ShapeDtypeStruct((S, D_MODEL), ACT),
            jax.ShapeDtypeStruct((D_MODEL, D_MODEL), F32),
            jax.ShapeDtypeStruct((1, D_MODEL), F32),
            jax.ShapeDtypeStruct((8, 128), F32),
        ],
        compiler_params=_params(("arbitrary",)),
    )(x, target, ya, pc, pc, conv_w, w_out, final_g)


def _attn_bwd(pa, dmix, att, probs, psinks, q_stack, tab):
    S = pa.shape[0]
    nt = S // TQ
    nb = TQ // BLK

    def body(g_ref, kvc_ref, kvp_ref, tabc_ref, tabp_ref, dm_ref, att_ref, pm_ref, ps_ref, qs_ref,
             d_ref, dsink_ref, kall, dkv, carry, do_sc, p_sc, ds_sc, dsink_acc):
        step = pl.program_id(0)

        @pl.when(step == 0)
        def _():
            carry[...] = jnp.zeros_like(carry)
            dsink_acc[...] = jnp.zeros_like(dsink_acc)

        _fill_kv(kall, kvc_ref, kvp_ref, tabc_ref, tabp_ref)
        dkv[0:TQ, :] = jnp.zeros((TQ, 2 * KV_W), F32)
        dkv[TQ:TQ + BLK, :] = carry[...]
        lane = lax.broadcasted_iota(jnp.int32, (BLK, 128), 1)
        half = [lane < HEAD_DIM, lane >= HEAD_DIM]
        upper = _upper()
        upper_b = upper.astype(BF16)
        for j in range(nb):
            rq = slice(j * BLK, (j + 1) * BLK)
            rk = slice(j * BLK, (j + 2) * BLK)
            tab = tabc_ref[rq, :]
            pair = [slice(p * 128, (p + 1) * 128) for p in range(4)]
            g = [g_ref[rq, c].astype(F32) for c in pair]
            da = [dm_ref[rq, c].astype(F32) for c in pair]
            gate = [_silu_and_grad(g[p]) for p in range(4)]
            do = [da[p] * gate[p][0] for p in range(4)]
            dqs, dks, dvs = [], [], []
            for s in range(2):
                kk = kall[s, rk, :]
                vv = kall[2 + s, rk, :]
                _stack_heads(do_sc, s, half, do)
                pb = pm_ref[(2 * j + s) * BLK:(2 * j + s + 1) * BLK, :]
                prob = pb.astype(F32)
                _split_store(p_sc, s, upper_b, pb)
                dprob = _merge(upper, _nt(vv, do_sc[s]))
                dsum = jnp.sum(dprob * prob, axis=0, keepdims=True)
                _split_store(ds_sc, s, upper_b, (prob * (dprob - dsum)).astype(BF16))
                dsink_acc[s, 0:1, :] += ps_ref[2 * j + s:2 * j + s + 1, :] * dsum
                dqs.append(_tn(ds_sc[s], kk))
                dks.append(_nn(ds_sc[s], qs_ref[2 * j + s]))
                dvs.append(_nn(p_sc[s], do_sc[s]))
            for p in range(4):
                d_ref[rq, pair[p]] = _rope_t(_unstack_pair(half, dqs, p) * 0.125, tab).astype(BF16)
                d_ref[rq, 512 + p * 128:512 + (p + 1) * 128] = (
                    da[p] * att_ref[rq, pair[p]].astype(F32) * gate[p][1]).astype(BF16)
            dkv[rk, 0:128] += dks[0] + pltpu.roll(dks[1], 64, 1)
            dkv[rk, 128:256] += dvs[0] + pltpu.roll(dvs[1], 64, 1)
        d_ref[:, 1024:1152] = _rope_t(dkv[BLK:BLK + TQ, 0:128], tabc_ref[...]).astype(BF16)
        d_ref[:, 1152:1280] = dkv[BLK:BLK + TQ, 128:256].astype(BF16)
        carry[...] = dkv[0:BLK, :]

        @pl.when(step == nt - 1)
        def _():
            lanes = lax.broadcasted_iota(jnp.int32, (8, 128), 1)
            out = jnp.zeros((8, 128), F32)
            for s in range(2):
                for a, (p, e) in enumerate(HEADS[s]):
                    tot = jnp.sum(dsink_acc[s, 0:1, a * BLK:(a + 1) * BLK], axis=1, keepdims=True)
                    out = jnp.where(lanes == 2 * p + e, -tot, out)
            dsink_ref[...] = out

    rev = lambda s: nt - 1 - s
    return pl.pallas_call(
        body,
        name="attn_bwd",
        grid=(nt,),
        in_specs=_attn_specs(rev)[2:] + [pl.BlockSpec((TQ, ATTN_W), lambda s: (nt - 1 - s, 0))] * 2 + [
            pl.BlockSpec((2 * TQ, 4 * BLK), lambda s: (nt - 1 - s, 0)),
            pl.BlockSpec((2 * nb, 4 * BLK), lambda s: (nt - 1 - s, 0)),
            pl.BlockSpec((2 * nb, 4 * BLK, 128), lambda s: (nt - 1 - s, 0, 0)),
        ],
        out_specs=[
            pl.BlockSpec((TQ, PA_W), lambda s: (nt - 1 - s, 0)),
            pl.BlockSpec((8, 128), lambda s: (0, 0)),
        ],
        out_shape=[
            jax.ShapeDtypeStruct((S, PA_W), BF16),
            jax.ShapeDtypeStruct((8, 128), F32),
        ],
        scratch_shapes=[
            pltpu.VMEM((4, BLK + TQ, 128), BF16),
            pltpu.VMEM((BLK + TQ, 2 * KV_W), F32),
            pltpu.VMEM((BLK, 2 * KV_W), F32),
            pltpu.VMEM((2, 4 * BLK, 128), BF16),
            pltpu.VMEM((2, 2 * BLK, 4 * BLK), BF16),
            pltpu.VMEM((2, 2 * BLK, 4 * BLK), BF16),
            pltpu.VMEM((2, 8, 4 * BLK), F32),
        ],
        compiler_params=_params(("arbitrary",)),
    )(pa, pa, pa, tab, tab, dmix, att, probs, psinks, q_stack)


def _conv_bwd_tile(pc_ref, prev_ref, next_ref, dm_ref, dmn_ref, w_ref, d_ref, gw_ref, has_prev, has_next,
                   on_piece):
    rows = pc_ref.shape[0]
    w0, w1, w2 = w_ref[0:1, :], w_ref[1:2, :], w_ref[2:3, :]
    b, c, hh, gc, u, um1, um2, cv = _conv_tile(pc_ref, prev_ref, w_ref, has_prev)
    sg, dsg = _silu_and_grad(gc)
    dy = dm_ref[...].astype(F32)
    dyb = dy * b
    dcv = dyb * sg

    def next_dcv(r):
        nd = (dmn_ref[r:r + 1, :].astype(F32) * next_ref[r:r + 1, 0:512].astype(F32)
              * _silu(next_ref[r:r + 1, 1536:2048].astype(F32)))
        return jnp.where(has_next, nd, 0.0)

    row = lax.broadcasted_iota(jnp.int32, (rows, CONV_W), 0)
    dp1 = jnp.where(row == rows - 1, next_dcv(0), pltpu.roll(dcv, rows - 1, 0))
    dp2 = jnp.where(row == rows - 1, next_dcv(1),
                    jnp.where(row == rows - 2, next_dcv(0), pltpu.roll(dcv, rows - 2, 0)))
    du = w2 * dcv + w1 * dp1 + w0 * dp2
    pieces = (lambda: dy * cv * sg, lambda: du * hh, lambda: du * c, lambda: dyb * cv * dsg)
    for k, piece in enumerate(pieces):
        d_ref[:, k * CONV_W:(k + 1) * CONV_W] = piece().astype(BF16)
        on_piece(k)
    gw_ref[0:1, :] += jnp.sum(dcv * um2, axis=0, keepdims=True)
    gw_ref[1:2, :] += jnp.sum(dcv * um1, axis=0, keepdims=True)
    gw_ref[2:3, :] += jnp.sum(dcv * u, axis=0, keepdims=True)


def _grad_x(da, dc, wt, x, dh, norm_g, small, grads):
    S = x.shape[0]
    n_steps = S // TM
    rs = _ReduceScatter(grads)
    n_rs_out = len(rs.out_shape())
    small_rows = 8 + small.shape[0]

    def body(da_ref, dc_ref, wt_ref, x_ref, dh_ref, g_ref, small_ref, *rest):
        grad_refs, rest = rest[:rs.n], rest[rs.n:]
        gx_ref, all_ref = rest[:2]
        rs_out, rest = rest[2:2 + n_rs_out], rest[2 + n_rs_out:]
        gng, stage, small_send, small_recv, small_own = rest[:5]
        rs_scratch = rest[5:]
        step = pl.program_id(0)
        finish = rs.emit(step, n_steps, grad_refs, rs_out, rs_scratch)

        @pl.when(step == 0)
        def _():
            gng[...] = jnp.zeros_like(gng)

        dxn = (_nn(da_ref[:, 0:512], wt_ref[0:512, :]) + _nn(da_ref[:, 512:1024], wt_ref[768:1280, :])
               + _nn(da_ref[:, 1024:1280], wt_ref[512:768, :]) + _nn(dc_ref[...], wt_ref[1280:3328, :]))
        xv = x_ref[...]
        r = lax.rsqrt(jnp.mean(xv * xv, axis=-1, keepdims=True) + EPS)
        n = xv * r
        gng[...] += jnp.sum(dxn * n, axis=0, keepdims=True)
        dxg = dxn * g_ref[...]
        gx_ref[...] = dh_ref[...] + r * (dxg - n * jnp.mean(dxg * n, axis=-1, keepdims=True))

        @pl.when(step == n_steps - 1)
        def _():
            x_, y_, c_ = lax.axis_index("x"), lax.axis_index("y"), lax.axis_index("c")
            me = 4 * x_ + 2 * y_ + c_
            for q in range(8):
                stage[q:q + 1, :] = gng[:, q * 128:(q + 1) * 128]
            stage[8:small_rows, :] = small_ref[...]
            own = pltpu.make_async_copy(stage, all_ref.at[me], small_own)
            own.start()
            sends = []
            for k in range(1, N_DEV):
                cp = pltpu.make_async_remote_copy(
                    src_ref=stage, dst_ref=all_ref.at[me],
                    send_sem=small_send.at[k - 1], recv_sem=small_recv.at[k - 1],
                    device_id=(x_ ^ (k >> 2), y_ ^ ((k >> 1) & 1), c_ ^ (k & 1)), device_id_type=MESH)
                cp.start()
                sends.append(cp)
            for cp in sends:
                cp.wait_send()
                cp.wait_recv()
            own.wait()

        finish()

    row = lambda i: (i, 0)
    fixed = lambda i: (0, 0)
    any_spec = pl.BlockSpec(memory_space=pl.ANY)
    outs = pl.pallas_call(
        body,
        name="grad_x_reduce_scatter",
        grid=(n_steps,),
        in_specs=[
            pl.BlockSpec((TM, PA_W), row),
            pl.BlockSpec((TM, PC_W), row),
            pl.BlockSpec((IN_W, D_MODEL), fixed),
            pl.BlockSpec((TM, D_MODEL), row),
            pl.BlockSpec((TM, D_MODEL), row),
            pl.BlockSpec((1, D_MODEL), fixed),
            pl.BlockSpec(small.shape, fixed),
        ] + [any_spec] * rs.n,
        out_specs=[pl.BlockSpec((TM, D_MODEL), row), any_spec] + [any_spec] * n_rs_out,
        out_shape=[jax.ShapeDtypeStruct((S, D_MODEL), F32),
                   jax.ShapeDtypeStruct((N_DEV, small_rows, 128), F32)] + rs.out_shape(),
        scratch_shapes=[
            pltpu.VMEM((1, D_MODEL), F32),
            pltpu.VMEM((small_rows, 128), F32),
            pltpu.SemaphoreType.DMA((N_DEV - 1,)),
            pltpu.SemaphoreType.DMA((N_DEV - 1,)),
            pltpu.SemaphoreType.DMA,
        ] + rs.scratch_shapes(),
        compiler_params=_params(("arbitrary",)),
    )(da, dc, wt, x, dh, norm_g, small, *grads)
    return outs[0], outs[1], outs[2:2 + rs.n], outs[2 + rs.n:2 + 2 * rs.n]


def _grad_w_in(da, pc, dmix, conv_w, xn):
    S = xn.shape[0]
    tm = 2 * TM
    nt = S // tm
    t16 = tm // HALO

    def body(da_ref, pc_ref, prev_ref, next_ref, dm_ref, dmn_ref, cw_ref, xn_ref, gw_ref, dc_ref, gcw_ref):
        i = pl.program_id(0)

        @pl.when(i == 0)
        def _():
            gw_ref[...] = jnp.zeros_like(gw_ref)
            gcw_ref[...] = jnp.zeros_like(gcw_ref)

        xn = xn_ref[...]
        gw_ref[0:512, :] += _tn(da_ref[:, 0:512], xn)
        gw_ref[768:1280, :] += _tn(da_ref[:, 512:1024], xn)
        gw_ref[512:768, :] += _tn(da_ref[:, 1024:1280], xn)
        def piece_grad(k):
            rows = slice(PA_W + k * CONV_W, PA_W + (k + 1) * CONV_W)
            gw_ref[rows, :] += _tn(dc_ref[:, k * CONV_W:(k + 1) * CONV_W], xn)

        _conv_bwd_tile(pc_ref, prev_ref, next_ref, dm_ref, dmn_ref, cw_ref, dc_ref, gcw_ref, i > 0, i < nt - 1,
                       piece_grad)

    row = lambda i: (i, 0)
    fixed = lambda i: (0, 0)
    nxt = lambda i: jnp.minimum((i + 1) * t16, nt * t16 - 1)
    return pl.pallas_call(
        body,
        name="grad_w_in",
        grid=(nt,),
        in_specs=[
            pl.BlockSpec((tm, PA_W), row),
            pl.BlockSpec((tm, PC_W), row),
            pl.BlockSpec((HALO, PC_W), lambda i: (jnp.maximum(i * t16 - 1, 0), 0)),
            pl.BlockSpec((HALO, PC_W), lambda i: (nxt(i), 0)),
            pl.BlockSpec((tm, CONV_W), lambda i: (i, 1)),
            pl.BlockSpec((HALO, CONV_W), lambda i: (nxt(i), 1)),
            pl.BlockSpec((CONV_K, CONV_W), fixed),
            pl.BlockSpec((tm, D_MODEL), row),
        ],
        out_specs=[
            pl.BlockSpec((IN_W, D_MODEL), fixed, pipeline_mode=pl.Buffered(1)),
            pl.BlockSpec((tm, PC_W), row),
            pl.BlockSpec((CONV_K, CONV_W), fixed),
        ],
        out_shape=[
            jax.ShapeDtypeStruct((IN_W, D_MODEL), F32),
            jax.ShapeDtypeStruct((S, PC_W), BF16),
            jax.ShapeDtypeStruct((CONV_K, CONV_W), F32),
        ],
        compiler_params=_params(("arbitrary",)),
    )(da, pc, pc, pc, dmix, dmix, conv_w, xn)


def _adam_update(w, g, m, v):
    c1 = 1.0 - ADAM_B1 ** ADAM_STEP
    c2 = 1.0 - ADAM_B2 ** ADAM_STEP
    nm = ADAM_B1 * m + (1.0 - ADAM_B1) * g
    nv = ADAM_B2 * v + (1.0 - ADAM_B2) * (g * g)
    return -ADAM_LR * ((nm / c1) / (jnp.sqrt(nv / c2) + ADAM_EPS) + ADAM_WD * w), nm, nv


def _sum_chips_adamw(own, others, w, m, v, name):
    def body(own_ref, p_ref, w_ref, m_ref, v_ref, g_ref, d_ref, nm_ref, nv_ref):
        g = own_ref[...]
        for k in range(N_CHIP - 1):
            g = g + p_ref[k].astype(F32)
        g_ref[...] = g
        d_ref[...], nm_ref[...], nv_ref[...] = _adam_update(w_ref[...], g, m_ref[...], v_ref[...])

    rows, cols = w.shape
    half = rows // 2
    blk = pl.BlockSpec((half, cols), lambda i: (i, 0))
    shape = jax.ShapeDtypeStruct(w.shape, F32)
    return pl.pallas_call(
        body,
        name=name,
        grid=(2,),
        in_specs=[blk, pl.BlockSpec((N_CHIP - 1, half, cols), lambda i: (0, i, 0)), blk, blk, blk],
        out_specs=[blk] * 4,
        out_shape=[shape] * 4,
        compiler_params=_params(("arbitrary",)),
    )(own, others, w, m, v)


SMALL_ROWS = 96


def _small_adamw(parts, params):
    def body(parts_ref, *rest):
        prm, outs, total = rest[:12], rest[12:29], rest[29]
        me = 4 * lax.axis_index("x") + 2 * lax.axis_index("y") + lax.axis_index("c")
        acc = parts_ref[0]
        for d in range(1, N_DEV):
            acc = acc + parts_ref[d]
        total[...] = acc
        grads = (total[0:8, :], total[8:16, :], total[16:17, 0:8],
                 total[pl.ds(pl.multiple_of(32 + me * 8, 8), CONV_K), 0:64])
        outs[0][...] = total[24:25, 0:1]
        for k, g in enumerate(grads):
            w_ref, m_ref, v_ref = prm[3 * k:3 * k + 3]
            g_ref, d_ref, nm_ref, nv_ref = outs[1 + 4 * k:5 + 4 * k]
            g_ref[...] = g
            d_ref[...], nm_ref[...], nv_ref[...] = _adam_update(w_ref[...], g, m_ref[...], v_ref[...])

    flat = [a for p in params for a in p]
    out_shape = [jax.ShapeDtypeStruct((1, 1), F32)]
    for p in params:
        out_shape += [jax.ShapeDtypeStruct(p[0].shape, F32)] * 4
    return pl.pallas_call(
        body,
        name="adamw_small",
        out_shape=out_shape,
        scratch_shapes=[pltpu.VMEM((SMALL_ROWS, 128), F32)],
        compiler_params=_params(),
    )(parts, *flat)


def kernel(x, norm_g, w_in, sinks, conv_w, w_out, final_g, loss_target, m_norm_g, m_w_in, m_sinks, m_conv_w, m_w_out, m_final_g, v_norm_g, v_w_in, v_sinks, v_conv_w, v_w_out, v_final_g):
    S = x.shape[1]
    x2 = x.reshape(S, D_MODEL)
    t2 = loss_target.reshape(S, D_MODEL)
    ng = norm_g.reshape(1, D_MODEL)
    fg = final_g.reshape(1, D_MODEL)

    cw_pad = jnp.zeros((8, 128), F32).at[0:CONV_K, 0:64].set(conv_w)
    xn, tab, wt = _prologue(x2, ng, w_in.T.astype(BF16))
    pa, pc, (wo, cw_all) = _fwd_proj(xn, wt, [w_out.astype(BF16), cw_pad])
    cw = cw_all.reshape(N_DEV, 8, 128)[:, 0:CONV_K, 0:64].transpose(1, 0, 2).reshape(CONV_K, CONV_W)
    ya, att, probs, psinks, q_stack = _attn_fwd(pa, tab, sinks)
    dh, dmix, g_wo, g_fg, loss_part = _out_loss(x2, t2, ya, pc, cw, wo, fg)
    da, g_sinks = _attn_bwd(pa, dmix, att, probs, psinks, q_stack, tab)
    g_wt, dc, g_cw = _grad_w_in(da, pc, dmix, cw, xn)
    cw_pack = jnp.pad(g_cw.reshape(CONV_K, N_DEV, 64).transpose(1, 0, 2),
                      ((0, 0), (0, 8 - CONV_K), (0, 64))).reshape(N_DEV * 8, 128)
    small = jnp.concatenate([g_fg.reshape(8, 128), g_sinks, loss_part, cw_pack], axis=0)
    grad_x, parts, own, others = _grad_x(
        da, dc, wt, x2, dh, ng, small,
        [g_wt.reshape(N_DEV, SHARD_IN, D_MODEL), g_wo.reshape(N_DEV, SHARD_OUT, D_MODEL)])
    gt, dt, nmt, nvt = _sum_chips_adamw(own[0], others[0], w_in.T, m_w_in.T, v_w_in.T, "adamw_w_in")
    grad_w_in, d_w_in, nm_w_in, nv_w_in = gt.T, dt.T, nmt.T, nvt.T
    grad_w_out, d_w_out, nm_w_out, nv_w_out = _sum_chips_adamw(
        own[1], others[1], w_out, m_w_out, v_w_out, "adamw_w_out")
    vec = lambda a: a.reshape(8, 128)
    row = lambda a: a.reshape(1, 8)
    res = _small_adamw(parts, [
        (vec(norm_g), vec(m_norm_g), vec(v_norm_g)), (vec(final_g), vec(m_final_g), vec(v_final_g)),
        (row(sinks), row(m_sinks), row(v_sinks)), (conv_w, m_conv_w, v_conv_w)])
    loss = res[0].reshape(())
    grad_norm_g, d_ng, nm_ng, nv_ng = [a.reshape(D_MODEL) for a in res[1:5]]
    grad_final_g, d_fg, nm_fg, nv_fg = [a.reshape(D_MODEL) for a in res[5:9]]
    grad_sinks, d_sk, nm_sk, nv_sk = [a.reshape(N_Q_HEADS) for a in res[9:13]]
    grad_conv_w, d_cw, nm_cw, nv_cw = res[13:17]

    return (loss, grad_x.reshape(1, S, D_MODEL), grad_norm_g, grad_w_in, grad_sinks, grad_conv_w, grad_w_out, grad_final_g,
            d_ng, d_w_in, d_sk, d_cw, d_w_out, d_fg,
            nm_ng, nm_w_in, nm_sk, nm_cw, nm_w_out, nm_fg,
            nv_ng, nv_w_in, nv_sk, nv_cw, nv_w_out, nv_fg)
```

```python
import jax
import jax.numpy as jnp
from jax import lax
from jax.experimental import pallas as pl
from jax.experimental.pallas import tpu as pltpu

F32 = jnp.float32
BF16 = jnp.bfloat16
MESH = pl.DeviceIdType.MESH

D_MODEL = 1024
HEAD_DIM = 64
N_Q_HEADS = 8
ATTN_W = 512
KV_W = 128
BLK = 128
CONV_W = 512
CONV_K = 3
IN_W = 3328
PA_W = 1280
PC_W = 2048
EPS = 1e-5
ROPE_THETA = 500000.0
ROT_DIM = 16
N_DEV = 8
N_CHIP = 4
SHARD_IN = IN_W // N_DEV
SHARD_OUT = D_MODEL // N_DEV

ADAM_LR = 0.001
ADAM_B1 = 0.9
ADAM_B2 = 0.999
ADAM_EPS = 1e-08
ADAM_WD = 0.01
ADAM_STEP = 10

ACT = jnp.bfloat16

TM = 512
TQ = 1024
HALO = 16
VMEM_LIMIT = 56 * 1024 * 1024

NT_DIMS = (((1,), (1,)), ((), ()))
TN_DIMS = (((0,), (0,)), ((), ()))


def _params(sem=None):
    kw = dict(vmem_limit_bytes=VMEM_LIMIT)
    if sem is not None:
        kw["dimension_semantics"] = sem
    return pltpu.CompilerParams(**kw)


def _nt(a, b):
    return lax.dot_general(a, b, NT_DIMS, preferred_element_type=F32)


def _tn(a, b):
    return lax.dot_general(a, b, TN_DIMS, preferred_element_type=F32)


def _nn(a, b):
    return jnp.dot(a, b, preferred_element_type=F32)


def _silu(g):
    return g * jax.nn.sigmoid(g)


def _silu_and_grad(g):
    s = jax.nn.sigmoid(g)
    return g * s, s * (1.0 + g * (1.0 - s))


class _AllGatherInSteps:
    def __init__(self, arrs, forward_step):
        self.blocks = [(a.shape, a.dtype) for a in arrs]
        self.n = len(arrs)
        self.forward_step = forward_step

    def out_shape(self):
        return [jax.ShapeDtypeStruct((N_DEV * s[0], s[1]), d) for s, d in self.blocks]

    def scratch_shapes(self):
        return [pltpu.SemaphoreType.DMA((7 * self.n,)), pltpu.SemaphoreType.DMA((7 * self.n,)),
                pltpu.SemaphoreType.DMA((self.n,))]

    def emit(self, step, n_steps, x_refs, out_refs, scratch):
        assert n_steps > self.forward_step + 1
        send_sems, recv_sems, local_sems = scratch
        x, y, c = lax.axis_index("x"), lax.axis_index("y"), lax.axis_index("c")
        me, sibling = (x, y, c), (x, y, 1 - c)
        chips = [(1 - x, y), (x, 1 - y), (1 - x, 1 - y)]

        def rows(a, px, py, pc):
            m = self.blocks[a][0][0]
            return out_refs[a].at[pl.ds((4 * px + 2 * py + pc) * m, m), :]

        def copy(a, k, block, to, src=None):
            return pltpu.make_async_remote_copy(
                src_ref=rows(a, *block) if src is None else src, dst_ref=rows(a, *block),
                send_sem=send_sems.at[a * 7 + k], recv_sem=recv_sems.at[a * 7 + k],
                device_id=to, device_id_type=MESH)

        def mine(a):
            return pltpu.make_async_copy(x_refs[a], rows(a, *me), local_sems.at[a])

        def first(a):
            return ([copy(a, 0, me, sibling, src=x_refs[a])]
                    + [copy(a, 1 + j, me, (*chip, c), src=x_refs[a]) for j, chip in enumerate(chips)])

        def passed(a):
            return [copy(a, 4 + j, (*chip, c), sibling) for j, chip in enumerate(chips)]

        @pl.when(step == 0)
        def _():
            for a in range(self.n):
                mine(a).start()
                for cp in first(a):
                    cp.start()

        @pl.when(step == self.forward_step)
        def _():
            for j, chip in enumerate(chips):
                for a in range(self.n):
                    copy(a, 1 + j, (*chip, c), me).wait_recv()
                    copy(a, 4 + j, (*chip, c), sibling).start()

        def finish():
            @pl.when(step == n_steps - 1)
            def _():
                for a in range(self.n):
                    copy(a, 0, sibling, me).wait_recv()
                    for j, chip in enumerate(chips):
                        copy(a, 4 + j, (*chip, 1 - c), me).wait_recv()
                    for cp in first(a) + passed(a):
                        cp.wait_send()
                    mine(a).wait()

        return finish


class _AllGatherViaNeighbours:
    def __init__(self, arr, first, mid, second):
        (self.m, self.ncol), self.dtype = arr.shape, arr.dtype
        assert self.m % 32 == 0
        self.first, self.mid, self.second = first, mid, second

    def out_shape(self):
        return [jax.ShapeDtypeStruct((N_DEV * self.m, self.ncol), self.dtype)]

    def scratch_shapes(self):
        return [pltpu.SemaphoreType.DMA((11,)), pltpu.SemaphoreType.DMA((11,)), pltpu.SemaphoreType.DMA]

    def emit(self, step, n_steps, x_ref, out_ref, scratch):
        assert 0 < self.first < self.mid < self.second < n_steps - 1
        send_sems, recv_sems, local_sem = scratch
        x, y, c = lax.axis_index("x"), lax.axis_index("y"), lax.axis_index("c")
        half = self.m // 2
        sibling, xn, yn = (x, y, 1 - c), (1 - x, y, c), (x, 1 - y, c)

        def rows(dev, part=None):
            px, py, pc = dev
            base = (4 * px + 2 * py + pc) * self.m
            if part is None:
                return out_ref.at[pl.ds(base, self.m), :]
            return out_ref.at[pl.ds(base + part * half, half), :]

        def copy(k, dev, to, part=None, src=None):
            return pltpu.make_async_remote_copy(
                src_ref=rows(dev, part) if src is None else src, dst_ref=rows(dev, part),
                send_sem=send_sems.at[k], recv_sem=recv_sems.at[k], device_id=to, device_id_type=MESH)

        me, dg = (x, y, c), (1 - x, 1 - y, c)
        mine = pltpu.make_async_copy(x_ref, rows(me), local_sem)
        my_half = lambda part: x_ref.at[pl.ds(part * half, half), :]
        sends = [
            copy(0, me, sibling, src=x_ref), copy(1, me, xn, part=0, src=my_half(0)),
            copy(2, me, yn, part=1, src=my_half(1)), copy(3, xn, yn, part=0), copy(4, yn, xn, part=1),
            copy(5, xn, sibling), copy(6, yn, sibling), copy(7, dg, sibling, part=0), copy(8, dg, sibling, part=1),
            copy(9, me, xn, part=1, src=my_half(1)), copy(10, me, yn, part=0, src=my_half(0)),
        ]
        other = lambda dev: (dev[0], dev[1], 1 - c)
        arrivals = [
            copy(0, other(me), sibling), copy(1, xn, xn, part=0), copy(2, yn, yn, part=1), copy(3, dg, yn, part=0),
            copy(4, dg, xn, part=1), copy(5, other(xn), sibling), copy(6, other(yn), sibling),
            copy(7, other(dg), sibling, part=0), copy(8, other(dg), sibling, part=1),
            copy(9, xn, xn, part=1), copy(10, yn, yn, part=0),
        ]

        @pl.when(step == 0)
        def _():
            mine.start()
            for k in (0, 1, 2, 9, 10):
                sends[k].start()

        @pl.when(step == self.first)
        def _():
            arrivals[1].wait_recv()
            sends[3].start()
            arrivals[2].wait_recv()
            sends[4].start()

        @pl.when(step == self.mid)
        def _():
            arrivals[9].wait_recv()
            sends[5].start()
            arrivals[10].wait_recv()
            sends[6].start()

        @pl.when(step == self.second)
        def _():
            arrivals[3].wait_recv()
            sends[7].start()
            arrivals[4].wait_recv()
            sends[8].start()

        def finish():
            @pl.when(step == n_steps - 1)
            def _():
                for k in (0, 5, 6, 7, 8):
                    arrivals[k].wait_recv()
                for cp in sends:
                    cp.wait_send()
                mine.wait()

        return finish


class _ReduceScatter:
    def __init__(self, grads):
        self.shapes = [g.shape[1:] for g in grads]
        self.n = len(grads)
        self.items = tuple((a, r) for r in (1, 2, 3, 0) for a in range(self.n))
        self.steps = N_CHIP + 2

    def out_shape(self):
        own = [jax.ShapeDtypeStruct(s, F32) for s in self.shapes]
        ici = [jax.ShapeDtypeStruct((N_CHIP - 1,) + s, BF16) for s in self.shapes]
        land = [jax.ShapeDtypeStruct((N_CHIP,) + s, F32) for s in self.shapes]
        return own + ici + land

    def scratch_shapes(self):
        n_items = len(self.items)
        return ([pltpu.VMEM((2,) + s, F32) for s in self.shapes]
                + [pltpu.VMEM((N_CHIP - 1,) + s, BF16) for s in self.shapes]
                + [pltpu.VMEM(s, F32) for s in self.shapes]
                + [pltpu.SemaphoreType.DMA((self.n * N_CHIP,))] * 2
                + [pltpu.SemaphoreType.DMA((2 * n_items,))]
                + [pltpu.SemaphoreType.DMA((self.n * (N_CHIP - 1),))] * 2
                + [pltpu.SemaphoreType.DMA((self.n,))])

    def emit(self, step, n_steps, g_refs, out_refs, scratch):
        assert n_steps > self.steps
        n = self.n
        own_refs, ici_refs, land_refs = out_refs[:n], out_refs[n:2 * n], out_refs[2 * n:]
        stage, pair_bf, pair_own = scratch[:n], scratch[n:2 * n], scratch[2 * n:3 * n]
        sib_send, sib_recv, load_sems, ici_send, ici_recv, own_sems = scratch[3 * n:]
        x, y, c = lax.axis_index("x"), lax.axis_index("y"), lax.axis_index("c")

        def chip_of(r):
            return (x ^ (r >> 1), y ^ (r & 1))

        def block_of(r, core):
            cx, cy = chip_of(r)
            return 4 * cx + 2 * cy + core

        def to_sibling(a, r):
            return pltpu.make_async_remote_copy(
                src_ref=g_refs[a].at[block_of(r, 1 - c)], dst_ref=land_refs[a].at[r],
                send_sem=sib_send.at[a * N_CHIP + r], recv_sem=sib_recv.at[a * N_CHIP + r],
                device_id=(x, y, 1 - c), device_id_type=MESH)

        def loads(k):
            a, r = self.items[k]
            return (pltpu.make_async_copy(g_refs[a].at[block_of(r, c)], stage[a].at[0], load_sems.at[2 * k]),
                    pltpu.make_async_copy(land_refs[a].at[r], stage[a].at[1], load_sems.at[2 * k + 1]))

        def to_owner(k):
            a, r = self.items[k]
            if r == 0:
                return pltpu.make_async_copy(pair_own[a], own_refs[a], own_sems.at[a])
            return pltpu.make_async_remote_copy(
                src_ref=pair_bf[a].at[r - 1], dst_ref=ici_refs[a].at[r - 1],
                send_sem=ici_send.at[a * (N_CHIP - 1) + r - 1], recv_sem=ici_recv.at[a * (N_CHIP - 1) + r - 1],
                device_id=(*chip_of(r), c), device_id_type=MESH)

        @pl.when(step == 0)
        def _():
            for a, r in self.items:
                to_sibling(a, r).start()

        def fetch(k):
            a, r = self.items[k]
            to_sibling(a, r).wait_recv()
            for cp in loads(k):
                cp.start()

        def add_and_send(k):
            a, r = self.items[k]
            for cp in loads(k):
                cp.wait()
            total = stage[a][0] + stage[a][1]
            if r == 0:
                pair_own[a][...] = total
            else:
                pair_bf[a][r - 1] = total.astype(BF16)
            to_owner(k).start()

        for g in range(N_CHIP + 1):
            @pl.when(step == 1 + g)
            def _(g=g):
                if g > 0:
                    for k in range((g - 1) * n, g * n):
                        add_and_send(k)
                if g < N_CHIP:
                    for k in range(g * n, (g + 1) * n):
                        fetch(k)

        def finish():
            @pl.when(step == n_steps - 1)
            def _():
                for k, (a, r) in enumerate(self.items):
                    if r == 0:
                        to_owner(k).wait()
                    else:
                        to_owner(k).wait_send()
                        to_owner(k).wait_recv()
                for a, r in self.items:
                    to_sibling(a, r).wait_send()

        return finish


def _prologue(x, norm_g, w_shard):
    S = x.shape[0]
    n_steps = S // TM
    half = ROT_DIM // 2
    pos = jnp.arange(S, dtype=jnp.int32).astype(F32)
    inv_freq = ROPE_THETA ** (-jnp.arange(0, ROT_DIM, 2, dtype=F32) / ROT_DIM)
    ang = inv_freq[:, None] * pos[None, :]
    cs = jnp.concatenate([jnp.cos(ang), jnp.sin(ang)], axis=0)
    ag = _AllGatherViaNeighbours(w_shard, first=n_steps // 4, mid=n_steps // 2 + 2, second=n_steps - 2)

    def body(x_ref, g_ref, cs_ref, w_ref, xn_ref, tab_ref, wt_ref, *ag_scratch):
        step = pl.program_id(0)
        finish = ag.emit(step, n_steps, w_ref, wt_ref, ag_scratch)
        xv = x_ref[...]
        r = lax.rsqrt(jnp.mean(xv * xv, axis=-1, keepdims=True) + EPS)
        xn_ref[...] = (xv * r * g_ref[...]).astype(BF16)

        xt = jnp.concatenate([cs_ref[...], jnp.zeros((128 - 2 * half, TM), F32)], axis=0).T
        lane = lax.broadcasted_iota(jnp.int32, (TM, 128), 1)
        rr = lane & (HEAD_DIM - 1)
        first = lane < HEAD_DIM

        def at(shift_first, shift_second):
            return jnp.where(first, pltpu.roll(xt, shift_first, 1) if shift_first else xt,
                             pltpu.roll(xt, shift_second, 1))

        cos_lo, cos_hi = at(0, HEAD_DIM), at(half, HEAD_DIM + half)
        sin_lo, sin_hi = at(128 - half, HEAD_DIM - half), at(0, HEAD_DIM)
        tab_ref[:, 0:128] = jnp.where(rr < half, cos_lo, jnp.where(rr < ROT_DIM, cos_hi, 1.0))
        tab_ref[:, 128:256] = jnp.where(rr < half, -sin_lo, 0.0)
        tab_ref[:, 256:384] = jnp.where((rr >= half) & (rr < ROT_DIM), sin_hi, 0.0)
        finish()

    any_spec = pl.BlockSpec(memory_space=pl.ANY)
    return pl.pallas_call(
        body,
        name="prologue_all_gather_w_in",
        grid=(n_steps,),
        in_specs=[
            pl.BlockSpec((TM, D_MODEL), lambda i: (i, 0)),
            pl.BlockSpec((1, D_MODEL), lambda i: (0, 0)),
            pl.BlockSpec((2 * half, TM), lambda i: (0, i)),
            any_spec,
        ],
        out_specs=[
            pl.BlockSpec((TM, D_MODEL), lambda i: (i, 0)),
            pl.BlockSpec((TM, 384), lambda i: (i, 0)),
            any_spec,
        ],
        out_shape=[
            jax.ShapeDtypeStruct((S, D_MODEL), BF16),
            jax.ShapeDtypeStruct((S, 384), F32),
        ] + ag.out_shape(),
        scratch_shapes=ag.scratch_shapes(),
        compiler_params=_params(("arbitrary",)),
    )(x, norm_g, cs, w_shard)


def _fwd_proj(xn, wt, later):
    S = xn.shape[0]
    tm = 2 * TM
    n_steps = S // tm
    ag = _AllGatherInSteps(later, forward_step=n_steps // 2)

    def body(xn_ref, wt_hbm, *rest):
        later_refs, rest = rest[:ag.n], rest[ag.n:]
        pa_ref, pc_ref = rest[:2]
        gathered, rest = rest[2:2 + ag.n], rest[2 + ag.n:]
        wt_ref, wt_sems = rest[:2]
        step = pl.program_id(0)
        finish = ag.emit(step, n_steps, later_refs, gathered, rest[2:])
        head = pltpu.make_async_copy(wt_hbm.at[pl.ds(0, PA_W), :], wt_ref.at[pl.ds(0, PA_W), :], wt_sems.at[0])
        tail = pltpu.make_async_copy(wt_hbm.at[pl.ds(PA_W, PC_W), :], wt_ref.at[pl.ds(PA_W, PC_W), :],
                                     wt_sems.at[1])

        @pl.when(step == 0)
        def _():
            head.start()
            tail.start()
            head.wait()

        xn = xn_ref[...]
        pa_ref[:, 0:512] = _nt(xn, wt_ref[0:512, :]).astype(ACT)
        pa_ref[:, 512:1024] = _nt(xn, wt_ref[768:1280, :]).astype(ACT)
        pa_ref[:, 1024:1280] = _nt(xn, wt_ref[512:768, :]).astype(ACT)

        @pl.when(step == 0)
        def _():
            tail.wait()

        pc_ref[...] = _nt(xn, wt_ref[1280:3328, :]).astype(ACT)
        finish()

    any_spec = pl.BlockSpec(memory_space=pl.ANY)
    outs = pl.pallas_call(
        body,
        name="fwd_proj_all_gather",
        grid=(n_steps,),
        in_specs=[
            pl.BlockSpec((tm, D_MODEL), lambda i: (i, 0)),
            any_spec,
        ] + [any_spec] * ag.n,
        out_specs=[
            pl.BlockSpec((tm, PA_W), lambda i: (i, 0)),
            pl.BlockSpec((tm, PC_W), lambda i: (i, 0)),
        ] + [any_spec] * ag.n,
        out_shape=[
            jax.ShapeDtypeStruct((S, PA_W), ACT),
            jax.ShapeDtypeStruct((S, PC_W), ACT),
        ] + ag.out_shape(),
        scratch_shapes=[pltpu.VMEM((IN_W, D_MODEL), BF16), pltpu.SemaphoreType.DMA((2,))] + ag.scratch_shapes(),
        compiler_params=_params(("arbitrary",)),
    )(xn, wt, *later)
    return outs[0], outs[1], outs[2:]


def _rope(t, tab):
    return (t * tab[:, 0:128] + pltpu.roll(t, 120, 1) * tab[:, 128:256]
            + pltpu.roll(t, 8, 1) * tab[:, 256:384])


def _rope_t(d, tab):
    return (d * tab[:, 0:128] + pltpu.roll(d * tab[:, 128:256], 8, 1)
            + pltpu.roll(d * tab[:, 256:384], 120, 1))


def _fill_kv(kall, kvc_ref, kvp_ref, tabc_ref, tabp_ref):
    for lo, kv_ref, tab_ref, n in ((0, kvp_ref, tabp_ref, BLK), (BLK, kvc_ref, tabc_ref, TQ)):
        k = _rope(kv_ref[:, 0:128].astype(F32), tab_ref[...])
        v = kv_ref[:, 128:256].astype(F32)
        kall[0, lo:lo + n, :] = k.astype(BF16)
        kall[1, lo:lo + n, :] = pltpu.roll(k, 64, 1).astype(BF16)
        kall[2, lo:lo + n, :] = v.astype(BF16)
        kall[3, lo:lo + n, :] = pltpu.roll(v, 64, 1).astype(BF16)


HEADS = (((0, 0), (1, 0), (2, 1), (3, 1)), ((0, 1), (1, 1), (2, 0), (3, 0)))


def _upper():
    kj = lax.broadcasted_iota(jnp.int32, (BLK, 4 * BLK), 0)
    qi = lax.broadcasted_iota(jnp.int32, (BLK, 4 * BLK), 1) & (BLK - 1)
    return kj > qi


def _merge(upper, both):
    return jnp.where(upper, both[0:BLK, :], both[BLK:2 * BLK, :])


def _split_store(ref, s, upper_b, vb):
    first = vb * upper_b
    ref[s, 0:BLK, :] = first
    ref[s, BLK:2 * BLK, :] = vb - first


def _sink_rows(sink_ref):
    return [jnp.concatenate([jnp.full((1, BLK), sink_ref[2 * p + e], F32) for p, e in HEADS[s]], axis=1)
            for s in range(2)]


def _stack_heads(ref, slot, half, pairs, s=None):
    for a, (p, e) in enumerate(HEADS[slot if s is None else s]):
        ref[slot, a * BLK:(a + 1) * BLK, :] = jnp.where(half[e], pairs[p], 0.0).astype(BF16)


def _unstack_pair(half, outs, p):
    lo = 0 if p < 2 else 1
    rows = slice(p * BLK, (p + 1) * BLK)
    return jnp.where(half[0], outs[lo][rows, :], outs[1 - lo][rows, :])


def _softmax(sm, sinks):
    m = jnp.maximum(jnp.max(sm, axis=0, keepdims=True), sinks)
    p = jnp.exp(sm - m)
    es = jnp.exp(sinks - m)
    inv = 1.0 / (jnp.sum(p, axis=0, keepdims=True) + es)
    return p * inv, es * inv


def _scores(kk, q_stack, first):
    st = _nt(kk, q_stack)
    prev = st[0:BLK, :]
    if first is not None:
        prev = prev + jnp.where(first, -jnp.inf, 0.0)
    return prev, st[BLK:2 * BLK, :]


def _attn_specs(tile):
    nb = TQ // BLK
    prev = lambda i: jnp.maximum(tile(i) * nb - 1, 0)
    return [
        pl.BlockSpec(memory_space=pltpu.SMEM),
        pl.BlockSpec((TQ, ATTN_W), lambda i: (tile(i), 0)),
        pl.BlockSpec((TQ, ATTN_W), lambda i: (tile(i), 1)),
        pl.BlockSpec((TQ, 2 * KV_W), lambda i: (tile(i), 4)),
        pl.BlockSpec((BLK, 2 * KV_W), lambda i: (prev(i), 4)),
        pl.BlockSpec((TQ, 384), lambda i: (tile(i), 0)),
        pl.BlockSpec((BLK, 384), lambda i: (prev(i), 0)),
    ]


def _attn_fwd(pa, tab, sinks):
    S = pa.shape[0]
    nb = TQ // BLK

    def body(sink_ref, q_ref, g_ref, kvc_ref, kvp_ref, tabc_ref, tabp_ref, o_ref, att_ref, pm_ref, ps_ref,
             qs_ref, kall, p_sc):
        i = pl.program_id(0)
        _fill_kv(kall, kvc_ref, kvp_ref, tabc_ref, tabp_ref)
        lane = lax.broadcasted_iota(jnp.int32, (BLK, 128), 1)
        half = [lane < HEAD_DIM, lane >= HEAD_DIM]
        upper = _upper()
        upper_b = upper.astype(BF16)
        sinks = _sink_rows(sink_ref)
        for j in range(nb):
            rq = slice(j * BLK, (j + 1) * BLK)
            rk = slice(j * BLK, (j + 2) * BLK)
            tab = tabc_ref[rq, :]
            qr = [_rope(q_ref[rq, p * 128:(p + 1) * 128].astype(F32), tab) * 0.125 for p in range(4)]
            outs = []
            for s in range(2):
                _stack_heads(qs_ref, 2 * j + s, half, qr, s)
                prev, cur = _scores(kall[s, rk, :], qs_ref[2 * j + s], i == 0 if j == 0 else None)
                prob, psink = _softmax(jnp.where(upper, prev, cur), sinks[s])
                pb = prob.astype(BF16)
                pm_ref[(2 * j + s) * BLK:(2 * j + s + 1) * BLK, :] = pb
                ps_ref[2 * j + s:2 * j + s + 1, :] = psink
                _split_store(p_sc, s, upper_b, pb)
                outs.append(_tn(p_sc[s], kall[2 + s, rk, :]))
            for p in range(4):
                cols = slice(p * 128, (p + 1) * 128)
                att = _unstack_pair(half, outs, p)
                att_ref[rq, cols] = att.astype(BF16)
                o_ref[rq, cols] = (att * _silu(g_ref[rq, cols].astype(F32))).astype(BF16)

    return pl.pallas_call(
        body,
        name="attn_fwd",
        grid=(S // TQ,),
        in_specs=_attn_specs(lambda i: i),
        out_specs=[pl.BlockSpec((TQ, ATTN_W), lambda i: (i, 0))] * 2 + [
            pl.BlockSpec((2 * TQ, 4 * BLK), lambda i: (i, 0)),
            pl.BlockSpec((2 * nb, 4 * BLK), lambda i: (i, 0)),
            pl.BlockSpec((2 * nb, 4 * BLK, 128), lambda i: (i, 0, 0)),
        ],
        out_shape=[jax.ShapeDtypeStruct((S, ATTN_W), BF16)] * 2 + [
            jax.ShapeDtypeStruct((2 * S, 4 * BLK), BF16),
            jax.ShapeDtypeStruct((2 * S // BLK, 4 * BLK), F32),
            jax.ShapeDtypeStruct((2 * S // BLK, 4 * BLK, 128), BF16),
        ],
        scratch_shapes=[
            pltpu.VMEM((4, BLK + TQ, 128), BF16),
            pltpu.VMEM((2, 2 * BLK, 4 * BLK), BF16),
        ],
        compiler_params=_params(("arbitrary",)),
    )(sinks, pa, pa, pa, pa, tab, tab)


def _shift_down(u, halo_ref, has_prev):
    def halo_u(r):
        hu = halo_ref[r:r + 1, 512:1024].astype(F32) * halo_ref[r:r + 1, 1024:1536].astype(F32)
        return jnp.where(has_prev, hu, 0.0)

    row = lax.broadcasted_iota(jnp.int32, u.shape, 0)
    um1 = jnp.where(row == 0, halo_u(HALO - 1), pltpu.roll(u, 1, 0))
    um2 = jnp.where(row == 0, halo_u(HALO - 2), jnp.where(row == 1, halo_u(HALO - 1), pltpu.roll(u, 2, 0)))
    return um1, um2


def _conv_tile(pc_ref, halo_ref, w_ref, has_prev):
    b = pc_ref[:, 0:512].astype(F32)
    c = pc_ref[:, 512:1024].astype(F32)
    hh = pc_ref[:, 1024:1536].astype(F32)
    gc = pc_ref[:, 1536:2048].astype(F32)
    u = c * hh
    um1, um2 = _shift_down(u, halo_ref, has_prev)
    cv = w_ref[0:1, :] * um2 + w_ref[1:2, :] * um1 + w_ref[2:3, :] * u
    return b, c, hh, gc, u, um1, um2, cv


def _prev_rows(width, col=0):
    return pl.BlockSpec((HALO, width), lambda i: (jnp.maximum(i * (TM // HALO) - 1, 0), col))


def _out_loss(x, target, ya, pc, conv_w, w_out, final_g):
    S = x.shape[0]

    def body(x_ref, t_ref, ya_ref, pc_ref, halo_ref, cw_ref, wo_ref, fg_ref,
             dh_ref, dmix_ref, gwo_ref, gfg_ref, loss_ref):
        @pl.when(pl.program_id(0) == 0)
        def _():
            gwo_ref[...] = jnp.zeros_like(gwo_ref)
            gfg_ref[...] = jnp.zeros_like(gfg_ref)
            loss_ref[...] = jnp.zeros_like(loss_ref)

        b, _, _, gc, _, _, _, cv = _conv_tile(pc_ref, halo_ref, cw_ref, pl.program_id(0) > 0)
        yc = (b * cv * _silu(gc)).astype(BF16)
        mix = jnp.concatenate([ya_ref[...], yc], axis=1)
        wo = wo_ref[...]
        fg = fg_ref[...]
        h = x_ref[...] + _nn(mix, wo)
        r = lax.rsqrt(jnp.mean(h * h, axis=-1, keepdims=True) + EPS)
        n = h * r
        err = n * fg - t_ref[...]
        loss_ref[...] += jnp.broadcast_to(
            0.5 * jnp.sum(jnp.mean(err * err, axis=-1, keepdims=True), axis=0, keepdims=True), (8, 128))
        gfg_ref[...] += jnp.sum(err * n, axis=0, keepdims=True) * (1.0 / D_MODEL)
        dyg = err * (fg * (1.0 / D_MODEL))
        dh = r * (dyg - n * jnp.mean(dyg * n, axis=-1, keepdims=True))
        dh_ref[...] = dh
        dhb = dh.astype(BF16)
        dmix_ref[...] = _nt(dhb, wo).astype(ACT)
        gwo_ref[...] += _tn(mix, dhb)

    row = lambda i: (i, 0)
    fixed = lambda i: (0, 0)
    return pl.pallas_call(
        body,
        name="out_loss",
        grid=(S // TM,),
        in_specs=[
            pl.BlockSpec((TM, D_MODEL), row),
            pl.BlockSpec((TM, D_MODEL), row),
            pl.BlockSpec((TM, ATTN_W), row),
            pl.BlockSpec((TM, PC_W), row),
            _prev_rows(PC_W),
            pl.BlockSpec((CONV_K, CONV_W), fixed),
            pl.BlockSpec((D_MODEL, D_MODEL), fixed),
            pl.BlockSpec((1, D_MODEL), fixed),
        ],
        out_specs=[
            pl.BlockSpec((TM, D_MODEL), row),
            pl.BlockSpec((TM, D_MODEL), row),
            pl.BlockSpec((D_MODEL, D_MODEL), fixed),
            pl.BlockSpec((1, D_MODEL), fixed),
            pl.BlockSpec((8, 128), fixed),
        ],
        out_shape=[
            jax.ShapeDtypeStruct((S, D_MODEL), F32),
            jax.ShapeDtypeStruct((S, D_MODEL), ACT),
            jax.ShapeDtypeStruct((D_MODEL, D_MODEL), F32),
            jax.ShapeDtypeStruct((1, D_MODEL), F32),
            jax.ShapeDtypeStruct((8, 128), F32),
        ],
        compiler_params=_params(("arbitrary",)),
    )(x, target, ya, pc, pc, conv_w, w_out, final_g)


def _attn_bwd(pa, dmix, att, probs, psinks, q_stack, tab):
    S = pa.shape[0]
    nt = S // TQ
    nb = TQ // BLK

    def body(g_ref, kvc_ref, kvp_ref, tabc_ref, tabp_ref, dm_ref, att_ref, pm_ref, ps_ref, qs_ref,
             d_ref, dsink_ref, kall, dkv, carry, do_sc, p_sc, ds_sc, dsink_acc):
        step = pl.program_id(0)

        @pl.when(step == 0)
        def _():
            carry[...] = jnp.zeros_like(carry)
            dsink_acc[...] = jnp.zeros_like(dsink_acc)

        _fill_kv(kall, kvc_ref, kvp_ref, tabc_ref, tabp_ref)
        dkv[0:TQ, :] = jnp.zeros((TQ, 2 * KV_W), F32)
        dkv[TQ:TQ + BLK, :] = carry[...]
        lane = lax.broadcasted_iota(jnp.int32, (BLK, 128), 1)
        half = [lane < HEAD_DIM, lane >= HEAD_DIM]
        upper = _upper()
        upper_b = upper.astype(BF16)
        for j in range(nb):
            rq = slice(j * BLK, (j + 1) * BLK)
            rk = slice(j * BLK, (j + 2) * BLK)
            tab = tabc_ref[rq, :]
            pair = [slice(p * 128, (p + 1) * 128) for p in range(4)]
            g = [g_ref[rq, c].astype(F32) for c in pair]
            da = [dm_ref[rq, c].astype(F32) for c in pair]
            gate = [_silu_and_grad(g[p]) for p in range(4)]
            do = [da[p] * gate[p][0] for p in range(4)]
            dqs, dks, dvs = [], [], []
            for s in range(2):
                kk = kall[s, rk, :]
                vv = kall[2 + s, rk, :]
                _stack_heads(do_sc, s, half, do)
                pb = pm_ref[(2 * j + s) * BLK:(2 * j + s + 1) * BLK, :]
                prob = pb.astype(F32)
                _split_store(p_sc, s, upper_b, pb)
                dprob = _merge(upper, _nt(vv, do_sc[s]))
                dsum = jnp.sum(dprob * prob, axis=0, keepdims=True)
                _split_store(ds_sc, s, upper_b, (prob * (dprob - dsum)).astype(BF16))
                dsink_acc[s, 0:1, :] += ps_ref[2 * j + s:2 * j + s + 1, :] * dsum
                dqs.append(_tn(ds_sc[s], kk))
                dks.append(_nn(ds_sc[s], qs_ref[2 * j + s]))
                dvs.append(_nn(p_sc[s], do_sc[s]))
            for p in range(4):
                d_ref[rq, pair[p]] = _rope_t(_unstack_pair(half, dqs, p) * 0.125, tab).astype(BF16)
                d_ref[rq, 512 + p * 128:512 + (p + 1) * 128] = (
                    da[p] * att_ref[rq, pair[p]].astype(F32) * gate[p][1]).astype(BF16)
            dkv[rk, 0:128] += dks[0] + pltpu.roll(dks[1], 64, 1)
            dkv[rk, 128:256] += dvs[0] + pltpu.roll(dvs[1], 64, 1)
        d_ref[:, 1024:1152] = _rope_t(dkv[BLK:BLK + TQ, 0:128], tabc_ref[...]).astype(BF16)
        d_ref[:, 1152:1280] = dkv[BLK:BLK + TQ, 128:256].astype(BF16)
        carry[...] = dkv[0:BLK, :]

        @pl.when(step == nt - 1)
        def _():
            lanes = lax.broadcasted_iota(jnp.int32, (8, 128), 1)
            out = jnp.zeros((8, 128), F32)
            for s in range(2):
                for a, (p, e) in enumerate(HEADS[s]):
                    tot = jnp.sum(dsink_acc[s, 0:1, a * BLK:(a + 1) * BLK], axis=1, keepdims=True)
                    out = jnp.where(lanes == 2 * p + e, -tot, out)
            dsink_ref[...] = out

    rev = lambda s: nt - 1 - s
    return pl.pallas_call(
        body,
        name="attn_bwd",
        grid=(nt,),
        in_specs=_attn_specs(rev)[2:] + [pl.BlockSpec((TQ, ATTN_W), lambda s: (nt - 1 - s, 0))] * 2 + [
            pl.BlockSpec((2 * TQ, 4 * BLK), lambda s: (nt - 1 - s, 0)),
            pl.BlockSpec((2 * nb, 4 * BLK), lambda s: (nt - 1 - s, 0)),
            pl.BlockSpec((2 * nb, 4 * BLK, 128), lambda s: (nt - 1 - s, 0, 0)),
        ],
        out_specs=[
            pl.BlockSpec((TQ, PA_W), lambda s: (nt - 1 - s, 0)),
            pl.BlockSpec((8, 128), lambda s: (0, 0)),
        ],
        out_shape=[
            jax.ShapeDtypeStruct((S, PA_W), BF16),
            jax.ShapeDtypeStruct((8, 128), F32),
        ],
        scratch_shapes=[
            pltpu.VMEM((4, BLK + TQ, 128), BF16),
            pltpu.VMEM((BLK + TQ, 2 * KV_W), F32),
            pltpu.VMEM((BLK, 2 * KV_W), F32),
            pltpu.VMEM((2, 4 * BLK, 128), BF16),
            pltpu.VMEM((2, 2 * BLK, 4 * BLK), BF16),
            pltpu.VMEM((2, 2 * BLK, 4 * BLK), BF16),
            pltpu.VMEM((2, 8, 4 * BLK), F32),
        ],
        compiler_params=_params(("arbitrary",)),
    )(pa, pa, pa, tab, tab, dmix, att, probs, psinks, q_stack)


def _conv_bwd_tile(pc_ref, prev_ref, next_ref, dm_ref, dmn_ref, w_ref, d_ref, gw_ref, has_prev, has_next,
                   on_piece):
    rows = pc_ref.shape[0]
    w0, w1, w2 = w_ref[0:1, :], w_ref[1:2, :], w_ref[2:3, :]
    b, c, hh, gc, u, um1, um2, cv = _conv_tile(pc_ref, prev_ref, w_ref, has_prev)
    sg, dsg = _silu_and_grad(gc)
    dy = dm_ref[...].astype(F32)
    dyb = dy * b
    dcv = dyb * sg

    def next_dcv(r):
        nd = (dmn_ref[r:r + 1, :].astype(F32) * next_ref[r:r + 1, 0:512].astype(F32)
              * _silu(next_ref[r:r + 1, 1536:2048].astype(F32)))
        return jnp.where(has_next, nd, 0.0)

    row = lax.broadcasted_iota(jnp.int32, (rows, CONV_W), 0)
    dp1 = jnp.where(row == rows - 1, next_dcv(0), pltpu.roll(dcv, rows - 1, 0))
    dp2 = jnp.where(row == rows - 1, next_dcv(1),
                    jnp.where(row == rows - 2, next_dcv(0), pltpu.roll(dcv, rows - 2, 0)))
    du = w2 * dcv + w1 * dp1 + w0 * dp2
    pieces = (lambda: dy * cv * sg, lambda: du * hh, lambda: du * c, lambda: dyb * cv * dsg)
    for k, piece in enumerate(pieces):
        d_ref[:, k * CONV_W:(k + 1) * CONV_W] = piece().astype(BF16)
        on_piece(k)
    gw_ref[0:1, :] += jnp.sum(dcv * um2, axis=0, keepdims=True)
    gw_ref[1:2, :] += jnp.sum(dcv * um1, axis=0, keepdims=True)
    gw_ref[2:3, :] += jnp.sum(dcv * u, axis=0, keepdims=True)


def _grad_x(da, dc, wt, x, dh, norm_g, small, grads):
    S = x.shape[0]
    n_steps = S // TM
    rs = _ReduceScatter(grads)
    n_rs_out = len(rs.out_shape())
    small_rows = 8 + small.shape[0]

    def body(da_ref, dc_ref, wt_hbm, x_ref, dh_ref, g_ref, small_ref, *rest):
        grad_refs, rest = rest[:rs.n], rest[rs.n:]
        gx_ref, all_ref = rest[:2]
        rs_out, rest = rest[2:2 + n_rs_out], rest[2 + n_rs_out:]
        gng, wt_ref, wt_sems, stage, small_send, small_recv, small_own = rest[:7]
        rs_scratch = rest[7:]
        step = pl.program_id(0)
        finish = rs.emit(step, n_steps, grad_refs, rs_out, rs_scratch)

        head = pltpu.make_async_copy(wt_hbm.at[pl.ds(0, PA_W), :], wt_ref.at[pl.ds(0, PA_W), :], wt_sems.at[0])
        tail = pltpu.make_async_copy(wt_hbm.at[pl.ds(PA_W, PC_W), :], wt_ref.at[pl.ds(PA_W, PC_W), :],
                                     wt_sems.at[1])

        @pl.when(step == 0)
        def _():
            gng[...] = jnp.zeros_like(gng)
            head.start()
            tail.start()
            head.wait()

        dxn = (_nn(da_ref[:, 0:512], wt_ref[0:512, :]) + _nn(da_ref[:, 512:1024], wt_ref[768:1280, :])
               + _nn(da_ref[:, 1024:1280], wt_ref[512:768, :]))

        @pl.when(step == 0)
        def _():
            tail.wait()

        dxn = dxn + _nn(dc_ref[...], wt_ref[1280:3328, :])
        xv = x_ref[...]
        r = lax.rsqrt(jnp.mean(xv * xv, axis=-1, keepdims=True) + EPS)
        n = xv * r
        gng[...] += jnp.sum(dxn * n, axis=0, keepdims=True)
        dxg = dxn * g_ref[...]
        gx_ref[...] = dh_ref[...] + r * (dxg - n * jnp.mean(dxg * n, axis=-1, keepdims=True))

        @pl.when(step == n_steps - 1)
        def _():
            x_, y_, c_ = lax.axis_index("x"), lax.axis_index("y"), lax.axis_index("c")
            me = 4 * x_ + 2 * y_ + c_
            for q in range(8):
                stage[q:q + 1, :] = gng[:, q * 128:(q + 1) * 128]
            stage[8:small_rows, :] = small_ref[...]
            own = pltpu.make_async_copy(stage, all_ref.at[me], small_own)
            own.start()
            sends = []
            for k in range(1, N_DEV):
                cp = pltpu.make_async_remote_copy(
                    src_ref=stage, dst_ref=all_ref.at[me],
                    send_sem=small_send.at[k - 1], recv_sem=small_recv.at[k - 1],
                    device_id=(x_ ^ (k >> 2), y_ ^ ((k >> 1) & 1), c_ ^ (k & 1)), device_id_type=MESH)
                cp.start()
                sends.append(cp)
            for cp in sends:
                cp.wait_send()
                cp.wait_recv()
            own.wait()

        finish()

    row = lambda i: (i, 0)
    fixed = lambda i: (0, 0)
    any_spec = pl.BlockSpec(memory_space=pl.ANY)
    outs = pl.pallas_call(
        body,
        name="grad_x_reduce_scatter",
        grid=(n_steps,),
        in_specs=[
            pl.BlockSpec((TM, PA_W), row),
            pl.BlockSpec((TM, PC_W), row),
            any_spec,
            pl.BlockSpec((TM, D_MODEL), row),
            pl.BlockSpec((TM, D_MODEL), row),
            pl.BlockSpec((1, D_MODEL), fixed),
            pl.BlockSpec(small.shape, fixed),
        ] + [any_spec] * rs.n,
        out_specs=[pl.BlockSpec((TM, D_MODEL), row), any_spec] + [any_spec] * n_rs_out,
        out_shape=[jax.ShapeDtypeStruct((S, D_MODEL), F32),
                   jax.ShapeDtypeStruct((N_DEV, small_rows, 128), F32)] + rs.out_shape(),
        scratch_shapes=[
            pltpu.VMEM((1, D_MODEL), F32),
            pltpu.VMEM((IN_W, D_MODEL), BF16),
            pltpu.SemaphoreType.DMA((2,)),
            pltpu.VMEM((small_rows, 128), F32),
            pltpu.SemaphoreType.DMA((N_DEV - 1,)),
            pltpu.SemaphoreType.DMA((N_DEV - 1,)),
            pltpu.SemaphoreType.DMA,
        ] + rs.scratch_shapes(),
        compiler_params=_params(("arbitrary",)),
    )(da, dc, wt, x, dh, norm_g, small, *grads)
    return outs[0], outs[1], outs[2:2 + rs.n], outs[2 + rs.n:2 + 2 * rs.n]


def _grad_w_in(da, pc, dmix, conv_w, xn):
    S = xn.shape[0]
    tm = 2 * TM
    nt = S // tm
    t16 = tm // HALO

    def body(da_ref, pc_ref, prev_ref, next_ref, dm_ref, dmn_ref, cw_ref, xn_ref, gw_ref, dc_ref, gcw_ref):
        i = pl.program_id(0)

        @pl.when(i == 0)
        def _():
            gw_ref[...] = jnp.zeros_like(gw_ref)
            gcw_ref[...] = jnp.zeros_like(gcw_ref)

        xn = xn_ref[...]
        gw_ref[0:512, :] += _tn(da_ref[:, 0:512], xn)
        gw_ref[768:1280, :] += _tn(da_ref[:, 512:1024], xn)
        gw_ref[512:768, :] += _tn(da_ref[:, 1024:1280], xn)
        def piece_grad(k):
            rows = slice(PA_W + k * CONV_W, PA_W + (k + 1) * CONV_W)
            gw_ref[rows, :] += _tn(dc_ref[:, k * CONV_W:(k + 1) * CONV_W], xn)

        _conv_bwd_tile(pc_ref, prev_ref, next_ref, dm_ref, dmn_ref, cw_ref, dc_ref, gcw_ref, i > 0, i < nt - 1,
                       piece_grad)

    row = lambda i: (i, 0)
    fixed = lambda i: (0, 0)
    nxt = lambda i: jnp.minimum((i + 1) * t16, nt * t16 - 1)
    return pl.pallas_call(
        body,
        name="grad_w_in",
        grid=(nt,),
        in_specs=[
            pl.BlockSpec((tm, PA_W), row),
            pl.BlockSpec((tm, PC_W), row),
            pl.BlockSpec((HALO, PC_W), lambda i: (jnp.maximum(i * t16 - 1, 0), 0)),
            pl.BlockSpec((HALO, PC_W), lambda i: (nxt(i), 0)),
            pl.BlockSpec((tm, CONV_W), lambda i: (i, 1)),
            pl.BlockSpec((HALO, CONV_W), lambda i: (nxt(i), 1)),
            pl.BlockSpec((CONV_K, CONV_W), fixed),
            pl.BlockSpec((tm, D_MODEL), row),
        ],
        out_specs=[
            pl.BlockSpec((IN_W, D_MODEL), fixed, pipeline_mode=pl.Buffered(1)),
            pl.BlockSpec((tm, PC_W), row),
            pl.BlockSpec((CONV_K, CONV_W), fixed),
        ],
        out_shape=[
            jax.ShapeDtypeStruct((IN_W, D_MODEL), F32),
            jax.ShapeDtypeStruct((S, PC_W), BF16),
            jax.ShapeDtypeStruct((CONV_K, CONV_W), F32),
        ],
        compiler_params=_params(("arbitrary",)),
    )(da, pc, pc, pc, dmix, dmix, conv_w, xn)


def _adam_update(w, g, m, v):
    c1 = 1.0 - ADAM_B1 ** ADAM_STEP
    c2 = 1.0 - ADAM_B2 ** ADAM_STEP
    nm = ADAM_B1 * m + (1.0 - ADAM_B1) * g
    nv = ADAM_B2 * v + (1.0 - ADAM_B2) * (g * g)
    return -ADAM_LR * ((nm / c1) / (jnp.sqrt(nv / c2) + ADAM_EPS) + ADAM_WD * w), nm, nv


def _sum_chips_adamw(own, others, w, m, v, name):
    def body(own_ref, p_ref, w_ref, m_ref, v_ref, g_ref, d_ref, nm_ref, nv_ref):
        g = own_ref[...]
        for k in range(N_CHIP - 1):
            g = g + p_ref[k].astype(F32)
        g_ref[...] = g
        d_ref[...], nm_ref[...], nv_ref[...] = _adam_update(w_ref[...], g, m_ref[...], v_ref[...])

    rows, cols = w.shape
    half = rows // 2
    blk = pl.BlockSpec((half, cols), lambda i: (i, 0))
    shape = jax.ShapeDtypeStruct(w.shape, F32)
    return pl.pallas_call(
        body,
        name=name,
        grid=(2,),
        in_specs=[blk, pl.BlockSpec((N_CHIP - 1, half, cols), lambda i: (0, i, 0)), blk, blk, blk],
        out_specs=[blk] * 4,
        out_shape=[shape] * 4,
        compiler_params=_params(("arbitrary",)),
    )(own, others, w, m, v)


SMALL_ROWS = 96


def _small_adamw(parts, params):
    def body(parts_ref, *rest):
        prm, outs, total = rest[:12], rest[12:29], rest[29]
        me = 4 * lax.axis_index("x") + 2 * lax.axis_index("y") + lax.axis_index("c")
        acc = parts_ref[0]
        for d in range(1, N_DEV):
            acc = acc + parts_ref[d]
        total[...] = acc
        grads = (total[0:8, :], total[8:16, :], total[16:17, 0:8],
                 total[pl.ds(pl.multiple_of(32 + me * 8, 8), CONV_K), 0:64])
        outs[0][...] = total[24:25, 0:1]
        for k, g in enumerate(grads):
            w_ref, m_ref, v_ref = prm[3 * k:3 * k + 3]
            g_ref, d_ref, nm_ref, nv_ref = outs[1 + 4 * k:5 + 4 * k]
            g_ref[...] = g
            d_ref[...], nm_ref[...], nv_ref[...] = _adam_update(w_ref[...], g, m_ref[...], v_ref[...])

    flat = [a for p in params for a in p]
    out_shape = [jax.ShapeDtypeStruct((1, 1), F32)]
    for p in params:
        out_shape += [jax.ShapeDtypeStruct(p[0].shape, F32)] * 4
    return pl.pallas_call(
        body,
        name="adamw_small",
        out_shape=out_shape,
        scratch_shapes=[pltpu.VMEM((SMALL_ROWS, 128), F32)],
        compiler_params=_params(),
    )(parts, *flat)


def kernel(x, norm_g, w_in, sinks, conv_w, w_out, final_g, loss_target, m_norm_g, m_w_in, m_sinks, m_conv_w, m_w_out, m_final_g, v_norm_g, v_w_in, v_sinks, v_conv_w, v_w_out, v_final_g):
    S = x.shape[1]
    x2 = x.reshape(S, D_MODEL)
    t2 = loss_target.reshape(S, D_MODEL)
    ng = norm_g.reshape(1, D_MODEL)
    fg = final_g.reshape(1, D_MODEL)

    cw_pad = jnp.zeros((8, 128), F32).at[0:CONV_K, 0:64].set(conv_w)
    xn, tab, wt = _prologue(x2, ng, w_in.T.astype(BF16))
    pa, pc, (wo, cw_all) = _fwd_proj(xn, wt, [w_out.astype(BF16), cw_pad])
    cw = cw_all.reshape(N_DEV, 8, 128)[:, 0:CONV_K, 0:64].transpose(1, 0, 2).reshape(CONV_K, CONV_W)
    ya, att, probs, psinks, q_stack = _attn_fwd(pa, tab, sinks)
    dh, dmix, g_wo, g_fg, loss_part = _out_loss(x2, t2, ya, pc, cw, wo, fg)
    da, g_sinks = _attn_bwd(pa, dmix, att, probs, psinks, q_stack, tab)
    g_wt, dc, g_cw = _grad_w_in(da, pc, dmix, cw, xn)
    cw_pack = jnp.pad(g_cw.reshape(CONV_K, N_DEV, 64).transpose(1, 0, 2),
                      ((0, 0), (0, 8 - CONV_K), (0, 64))).reshape(N_DEV * 8, 128)
    small = jnp.concatenate([g_fg.reshape(8, 128), g_sinks, loss_part, cw_pack], axis=0)
    grad_x, parts, own, others = _grad_x(
        da, dc, wt, x2, dh, ng, small,
        [g_wt.reshape(N_DEV, SHARD_IN, D_MODEL), g_wo.reshape(N_DEV, SHARD_OUT, D_MODEL)])
    gt, dt, nmt, nvt = _sum_chips_adamw(own[0], others[0], w_in.T, m_w_in.T, v_w_in.T, "adamw_w_in")
    grad_w_in, d_w_in, nm_w_in, nv_w_in = gt.T, dt.T, nmt.T, nvt.T
    grad_w_out, d_w_out, nm_w_out, nv_w_out = _sum_chips_adamw(
        own[1], others[1], w_out, m_w_out, v_w_out, "adamw_w_out")
    vec = lambda a: a.reshape(8, 128)
    row = lambda a: a.reshape(1, 8)
    res = _small_adamw(parts, [
        (vec(norm_g), vec(m_norm_g), vec(v_norm_g)), (vec(final_g), vec(m_final_g), vec(v_final_g)),
        (row(sinks), row(m_sinks), row(v_sinks)), (conv_w, m_conv_w, v_conv_w)])
    loss = res[0].reshape(())
    grad_norm_g, d_ng, nm_ng, nv_ng = [a.reshape(D_MODEL) for a in res[1:5]]
    grad_final_g, d_fg, nm_fg, nv_fg = [a.reshape(D_MODEL) for a in res[5:9]]
    grad_sinks, d_sk, nm_sk, nv_sk = [a.reshape(N_Q_HEADS) for a in res[9:13]]
    grad_conv_w, d_cw, nm_cw, nv_cw = res[13:17]

    return (loss, grad_x.reshape(1, S, D_MODEL), grad_norm_g, grad_w_in, grad_sinks, grad_conv_w, grad_w_out, grad_final_g,
            d_ng, d_w_in, d_sk, d_cw, d_w_out, d_fg,
            nm_ng, nm_w_in, nm_sk, nm_cw, nm_w_out, nm_fg,
            nv_ng, nv_w_in, nv_sk, nv_cw, nv_w_out, nv_fg)
```

```python
import jax
import jax.numpy as jnp
from jax import lax
from jax.experimental import pallas as pl
from jax.experimental.pallas import tpu as pltpu

F32 = jnp.float32
BF16 = jnp.bfloat16
MESH = pl.DeviceIdType.MESH

D_MODEL = 1024
HEAD_DIM = 64
N_Q_HEADS = 8
ATTN_W = 512
KV_W = 128
BLK = 128
CONV_W = 512
CONV_K = 3
IN_W = 3328
PA_W = 1280
PC_W = 2048
EPS = 1e-5
ROPE_THETA = 500000.0
ROT_DIM = 16
N_DEV = 8
N_CHIP = 4
SHARD_IN = IN_W // N_DEV
SHARD_OUT = D_MODEL // N_DEV

ADAM_LR = 0.001
ADAM_B1 = 0.9
ADAM_B2 = 0.999
ADAM_EPS = 1e-08
ADAM_WD = 0.01
ADAM_STEP = 10

ACT = jnp.bfloat16

TM = 512
TQ = 512
HALO = 16
VMEM_LIMIT = 56 * 1024 * 1024

NT_DIMS = (((1,), (1,)), ((), ()))
TN_DIMS = (((0,), (0,)), ((), ()))


def _params(sem=None):
    kw = dict(vmem_limit_bytes=VMEM_LIMIT)
    if sem is not None:
        kw["dimension_semantics"] = sem
    return pltpu.CompilerParams(**kw)


def _nt(a, b):
    return lax.dot_general(a, b, NT_DIMS, preferred_element_type=F32)


def _tn(a, b):
    return lax.dot_general(a, b, TN_DIMS, preferred_element_type=F32)


def _nn(a, b):
    return jnp.dot(a, b, preferred_element_type=F32)


def _silu(g):
    return g * jax.nn.sigmoid(g)


def _silu_and_grad(g):
    s = jax.nn.sigmoid(g)
    return g * s, s * (1.0 + g * (1.0 - s))


class _AllGatherInSteps:
    def __init__(self, arrs, forward_step):
        self.blocks = [(a.shape, a.dtype) for a in arrs]
        self.n = len(arrs)
        self.forward_step = forward_step

    def out_shape(self):
        return [jax.ShapeDtypeStruct((N_DEV * s[0], s[1]), d) for s, d in self.blocks]

    def scratch_shapes(self):
        return [pltpu.SemaphoreType.DMA((7 * self.n,)), pltpu.SemaphoreType.DMA((7 * self.n,)),
                pltpu.SemaphoreType.DMA((self.n,))]

    def emit(self, step, n_steps, x_refs, out_refs, scratch):
        assert n_steps > self.forward_step + 1
        send_sems, recv_sems, local_sems = scratch
        x, y, c = lax.axis_index("x"), lax.axis_index("y"), lax.axis_index("c")
        me, sibling = (x, y, c), (x, y, 1 - c)
        chips = [(1 - x, y), (x, 1 - y), (1 - x, 1 - y)]

        def rows(a, px, py, pc):
            m = self.blocks[a][0][0]
            return out_refs[a].at[pl.ds((4 * px + 2 * py + pc) * m, m), :]

        def copy(a, k, block, to, src=None):
            return pltpu.make_async_remote_copy(
                src_ref=rows(a, *block) if src is None else src, dst_ref=rows(a, *block),
                send_sem=send_sems.at[a * 7 + k], recv_sem=recv_sems.at[a * 7 + k],
                device_id=to, device_id_type=MESH)

        def mine(a):
            return pltpu.make_async_copy(x_refs[a], rows(a, *me), local_sems.at[a])

        def first(a):
            return ([copy(a, 0, me, sibling, src=x_refs[a])]
                    + [copy(a, 1 + j, me, (*chip, c), src=x_refs[a]) for j, chip in enumerate(chips)])

        def passed(a):
            return [copy(a, 4 + j, (*chip, c), sibling) for j, chip in enumerate(chips)]

        @pl.when(step == 0)
        def _():
            for a in range(self.n):
                mine(a).start()
                for cp in first(a):
                    cp.start()

        @pl.when(step == self.forward_step)
        def _():
            for j, chip in enumerate(chips):
                for a in range(self.n):
                    copy(a, 1 + j, (*chip, c), me).wait_recv()
                    copy(a, 4 + j, (*chip, c), sibling).start()

        def finish():
            @pl.when(step == n_steps - 1)
            def _():
                for a in range(self.n):
                    copy(a, 0, sibling, me).wait_recv()
                    for j, chip in enumerate(chips):
                        copy(a, 4 + j, (*chip, 1 - c), me).wait_recv()
                    for cp in first(a) + passed(a):
                        cp.wait_send()
                    mine(a).wait()

        return finish


class _AllGatherViaNeighbours:
    def __init__(self, arr, first, mid, second):
        (self.m, self.ncol), self.dtype = arr.shape, arr.dtype
        assert self.m % 32 == 0
        self.first, self.mid, self.second = first, mid, second

    def out_shape(self):
        return [jax.ShapeDtypeStruct((N_DEV * self.m, self.ncol), self.dtype)]

    def scratch_shapes(self):
        return [pltpu.SemaphoreType.DMA((11,)), pltpu.SemaphoreType.DMA((11,)), pltpu.SemaphoreType.DMA]

    def emit(self, step, n_steps, x_ref, out_ref, scratch):
        assert 0 < self.first < self.mid < self.second < n_steps - 1
        send_sems, recv_sems, local_sem = scratch
        x, y, c = lax.axis_index("x"), lax.axis_index("y"), lax.axis_index("c")
        half = self.m // 2
        sibling, xn, yn = (x, y, 1 - c), (1 - x, y, c), (x, 1 - y, c)

        def rows(dev, part=None):
            px, py, pc = dev
            base = (4 * px + 2 * py + pc) * self.m
            if part is None:
                return out_ref.at[pl.ds(base, self.m), :]
            return out_ref.at[pl.ds(base + part * half, half), :]

        def copy(k, dev, to, part=None, src=None):
            return pltpu.make_async_remote_copy(
                src_ref=rows(dev, part) if src is None else src, dst_ref=rows(dev, part),
                send_sem=send_sems.at[k], recv_sem=recv_sems.at[k], device_id=to, device_id_type=MESH)

        me, dg = (x, y, c), (1 - x, 1 - y, c)
        mine = pltpu.make_async_copy(x_ref, rows(me), local_sem)
        my_half = lambda part: x_ref.at[pl.ds(part * half, half), :]
        sends = [
            copy(0, me, sibling, src=x_ref), copy(1, me, xn, part=0, src=my_half(0)),
            copy(2, me, yn, part=1, src=my_half(1)), copy(3, xn, yn, part=0), copy(4, yn, xn, part=1),
            copy(5, xn, sibling), copy(6, yn, sibling), copy(7, dg, sibling, part=0), copy(8, dg, sibling, part=1),
            copy(9, me, xn, part=1, src=my_half(1)), copy(10, me, yn, part=0, src=my_half(0)),
        ]
        other = lambda dev: (dev[0], dev[1], 1 - c)
        arrivals = [
            copy(0, other(me), sibling), copy(1, xn, xn, part=0), copy(2, yn, yn, part=1), copy(3, dg, yn, part=0),
            copy(4, dg, xn, part=1), copy(5, other(xn), sibling), copy(6, other(yn), sibling),
            copy(7, other(dg), sibling, part=0), copy(8, other(dg), sibling, part=1),
            copy(9, xn, xn, part=1), copy(10, yn, yn, part=0),
        ]

        @pl.when(step == 0)
        def _():
            mine.start()
            for k in (0, 1, 2, 9, 10):
                sends[k].start()

        @pl.when(step == self.first)
        def _():
            arrivals[1].wait_recv()
            sends[3].start()
            arrivals[2].wait_recv()
            sends[4].start()

        @pl.when(step == self.mid)
        def _():
            arrivals[9].wait_recv()
            sends[5].start()
            arrivals[10].wait_recv()
            sends[6].start()

        @pl.when(step == self.second)
        def _():
            arrivals[3].wait_recv()
            sends[7].start()
            arrivals[4].wait_recv()
            sends[8].start()

        def finish():
            @pl.when(step == n_steps - 1)
            def _():
                for k in (0, 5, 6, 7, 8):
                    arrivals[k].wait_recv()
                for cp in sends:
                    cp.wait_send()
                mine.wait()

        return finish


class _ReduceScatter:
    def __init__(self, grads):
        self.shapes = [g.shape[1:] for g in grads]
        self.n = len(grads)
        self.items = tuple((a, r) for r in (1, 2, 3, 0) for a in range(self.n))
        self.steps = N_CHIP + 2

    def out_shape(self):
        own = [jax.ShapeDtypeStruct(s, F32) for s in self.shapes]
        ici = [jax.ShapeDtypeStruct((N_CHIP - 1,) + s, BF16) for s in self.shapes]
        land = [jax.ShapeDtypeStruct((N_CHIP,) + s, F32) for s in self.shapes]
        return own + ici + land

    def scratch_shapes(self):
        n_items = len(self.items)
        return ([pltpu.VMEM((2,) + s, F32) for s in self.shapes]
                + [pltpu.VMEM((N_CHIP - 1,) + s, BF16) for s in self.shapes]
                + [pltpu.VMEM(s, F32) for s in self.shapes]
                + [pltpu.SemaphoreType.DMA((self.n * N_CHIP,))] * 2
                + [pltpu.SemaphoreType.DMA((2 * n_items,))]
                + [pltpu.SemaphoreType.DMA((self.n * (N_CHIP - 1),))] * 2
                + [pltpu.SemaphoreType.DMA((self.n,))])

    def emit(self, step, n_steps, g_refs, out_refs, scratch):
        assert n_steps > self.steps
        n = self.n
        own_refs, ici_refs, land_refs = out_refs[:n], out_refs[n:2 * n], out_refs[2 * n:]
        stage, pair_bf, pair_own = scratch[:n], scratch[n:2 * n], scratch[2 * n:3 * n]
        sib_send, sib_recv, load_sems, ici_send, ici_recv, own_sems = scratch[3 * n:]
        x, y, c = lax.axis_index("x"), lax.axis_index("y"), lax.axis_index("c")

        def chip_of(r):
            return (x ^ (r >> 1), y ^ (r & 1))

        def block_of(r, core):
            cx, cy = chip_of(r)
            return 4 * cx + 2 * cy + core

        def to_sibling(a, r):
            return pltpu.make_async_remote_copy(
                src_ref=g_refs[a].at[block_of(r, 1 - c)], dst_ref=land_refs[a].at[r],
                send_sem=sib_send.at[a * N_CHIP + r], recv_sem=sib_recv.at[a * N_CHIP + r],
                device_id=(x, y, 1 - c), device_id_type=MESH)

        def loads(k):
            a, r = self.items[k]
            return (pltpu.make_async_copy(g_refs[a].at[block_of(r, c)], stage[a].at[0], load_sems.at[2 * k]),
                    pltpu.make_async_copy(land_refs[a].at[r], stage[a].at[1], load_sems.at[2 * k + 1]))

        def to_owner(k):
            a, r = self.items[k]
            if r == 0:
                return pltpu.make_async_copy(pair_own[a], own_refs[a], own_sems.at[a])
            return pltpu.make_async_remote_copy(
                src_ref=pair_bf[a].at[r - 1], dst_ref=ici_refs[a].at[r - 1],
                send_sem=ici_send.at[a * (N_CHIP - 1) + r - 1], recv_sem=ici_recv.at[a * (N_CHIP - 1) + r - 1],
                device_id=(*chip_of(r), c), device_id_type=MESH)

        @pl.when(step == 0)
        def _():
            for a, r in self.items:
                to_sibling(a, r).start()

        def fetch(k):
            a, r = self.items[k]
            to_sibling(a, r).wait_recv()
            for cp in loads(k):
                cp.start()

        def add_and_send(k):
            a, r = self.items[k]
            for cp in loads(k):
                cp.wait()
            total = stage[a][0] + stage[a][1]
            if r == 0:
                pair_own[a][...] = total
            else:
                pair_bf[a][r - 1] = total.astype(BF16)
            to_owner(k).start()

        for g in range(N_CHIP + 1):
            @pl.when(step == 1 + g)
            def _(g=g):
                if g > 0:
                    for k in range((g - 1) * n, g * n):
                        add_and_send(k)
                if g < N_CHIP:
                    for k in range(g * n, (g + 1) * n):
                        fetch(k)

        def finish():
            @pl.when(step == n_steps - 1)
            def _():
                for k, (a, r) in enumerate(self.items):
                    if r == 0:
                        to_owner(k).wait()
                    else:
                        to_owner(k).wait_send()
                        to_owner(k).wait_recv()
                for a, r in self.items:
                    to_sibling(a, r).wait_send()

        return finish


def _prologue(x, norm_g, w_shard):
    S = x.shape[0]
    n_steps = S // TM
    half = ROT_DIM // 2
    pos = jnp.arange(S, dtype=jnp.int32).astype(F32)
    inv_freq = ROPE_THETA ** (-jnp.arange(0, ROT_DIM, 2, dtype=F32) / ROT_DIM)
    ang = inv_freq[:, None] * pos[None, :]
    cs = jnp.concatenate([jnp.cos(ang), jnp.sin(ang)], axis=0)
    ag = _AllGatherViaNeighbours(w_shard, first=n_steps // 4, mid=n_steps // 2 + 2, second=n_steps - 2)

    def body(x_ref, g_ref, cs_ref, w_ref, xn_ref, tab_ref, wt_ref, *ag_scratch):
        step = pl.program_id(0)
        finish = ag.emit(step, n_steps, w_ref, wt_ref, ag_scratch)
        xv = x_ref[...]
        r = lax.rsqrt(jnp.mean(xv * xv, axis=-1, keepdims=True) + EPS)
        xn_ref[...] = (xv * r * g_ref[...]).astype(BF16)

        xt = jnp.concatenate([cs_ref[...], jnp.zeros((128 - 2 * half, TM), F32)], axis=0).T
        lane = lax.broadcasted_iota(jnp.int32, (TM, 128), 1)
        rr = lane & (HEAD_DIM - 1)
        first = lane < HEAD_DIM

        def at(shift_first, shift_second):
            return jnp.where(first, pltpu.roll(xt, shift_first, 1) if shift_first else xt,
                             pltpu.roll(xt, shift_second, 1))

        cos_lo, cos_hi = at(0, HEAD_DIM), at(half, HEAD_DIM + half)
        sin_lo, sin_hi = at(128 - half, HEAD_DIM - half), at(0, HEAD_DIM)
        tab_ref[:, 0:128] = jnp.where(rr < half, cos_lo, jnp.where(rr < ROT_DIM, cos_hi, 1.0))
        tab_ref[:, 128:256] = jnp.where(rr < half, -sin_lo, 0.0)
        tab_ref[:, 256:384] = jnp.where((rr >= half) & (rr < ROT_DIM), sin_hi, 0.0)
        finish()

    any_spec = pl.BlockSpec(memory_space=pl.ANY)
    return pl.pallas_call(
        body,
        name="prologue_all_gather_w_in",
        grid=(n_steps,),
        in_specs=[
            pl.BlockSpec((TM, D_MODEL), lambda i: (i, 0)),
            pl.BlockSpec((1, D_MODEL), lambda i: (0, 0)),
            pl.BlockSpec((2 * half, TM), lambda i: (0, i)),
            any_spec,
        ],
        out_specs=[
            pl.BlockSpec((TM, D_MODEL), lambda i: (i, 0)),
            pl.BlockSpec((TM, 384), lambda i: (i, 0)),
            any_spec,
        ],
        out_shape=[
            jax.ShapeDtypeStruct((S, D_MODEL), BF16),
            jax.ShapeDtypeStruct((S, 384), F32),
        ] + ag.out_shape(),
        scratch_shapes=ag.scratch_shapes(),
        compiler_params=_params(("arbitrary",)),
    )(x, norm_g, cs, w_shard)


def _fwd_proj(xn, wt, later):
    S = xn.shape[0]
    tm = 2 * TM
    n_steps = S // tm
    ag = _AllGatherInSteps(later, forward_step=n_steps // 2)

    def body(xn_ref, wt_ref, *rest):
        later_refs, rest = rest[:ag.n], rest[ag.n:]
        pa_ref, pc_ref = rest[:2]
        gathered, ag_scratch = rest[2:2 + ag.n], rest[2 + ag.n:]
        step = pl.program_id(0)
        finish = ag.emit(step, n_steps, later_refs, gathered, ag_scratch)
        xn = xn_ref[...]
        pa_ref[:, 0:512] = _nt(xn, wt_ref[0:512, :]).astype(ACT)
        pa_ref[:, 512:1024] = _nt(xn, wt_ref[768:1280, :]).astype(ACT)
        pa_ref[:, 1024:1280] = _nt(xn, wt_ref[512:768, :]).astype(ACT)
        pc_ref[...] = _nt(xn, wt_ref[1280:3328, :]).astype(ACT)
        finish()

    any_spec = pl.BlockSpec(memory_space=pl.ANY)
    outs = pl.pallas_call(
        body,
        name="fwd_proj_all_gather",
        grid=(n_steps,),
        in_specs=[
            pl.BlockSpec((tm, D_MODEL), lambda i: (i, 0)),
            pl.BlockSpec((IN_W, D_MODEL), lambda i: (0, 0)),
        ] + [any_spec] * ag.n,
        out_specs=[
            pl.BlockSpec((tm, PA_W), lambda i: (i, 0)),
            pl.BlockSpec((tm, PC_W), lambda i: (i, 0)),
        ] + [any_spec] * ag.n,
        out_shape=[
            jax.ShapeDtypeStruct((S, PA_W), ACT),
            jax.ShapeDtypeStruct((S, PC_W), ACT),
        ] + ag.out_shape(),
        scratch_shapes=ag.scratch_shapes(),
        compiler_params=_params(("arbitrary",)),
    )(xn, wt, *later)
    return outs[0], outs[1], outs[2:]


def _rope(t, tab):
    return (t * tab[:, 0:128] + pltpu.roll(t, 120, 1) * tab[:, 128:256]
            + pltpu.roll(t, 8, 1) * tab[:, 256:384])


def _rope_t(d, tab):
    return (d * tab[:, 0:128] + pltpu.roll(d * tab[:, 128:256], 8, 1)
            + pltpu.roll(d * tab[:, 256:384], 120, 1))


def _fill_kv(kall, kvc_ref, kvp_ref, tabc_ref, tabp_ref):
    for lo, kv_ref, tab_ref, n in ((0, kvp_ref, tabp_ref, BLK), (BLK, kvc_ref, tabc_ref, TQ)):
        k = _rope(kv_ref[:, 0:128].astype(F32), tab_ref[...])
        v = kv_ref[:, 128:256].astype(F32)
        kall[0, lo:lo + n, :] = k.astype(BF16)
        kall[1, lo:lo + n, :] = pltpu.roll(k, 64, 1).astype(BF16)
        kall[2, lo:lo + n, :] = v.astype(BF16)
        kall[3, lo:lo + n, :] = pltpu.roll(v, 64, 1).astype(BF16)


HEADS = (((0, 0), (1, 0), (2, 1), (3, 1)), ((0, 1), (1, 1), (2, 0), (3, 0)))


def _upper():
    kj = lax.broadcasted_iota(jnp.int32, (BLK, 4 * BLK), 0)
    qi = lax.broadcasted_iota(jnp.int32, (BLK, 4 * BLK), 1) & (BLK - 1)
    return kj > qi


def _merge(upper, both):
    return jnp.where(upper, both[0:BLK, :], both[BLK:2 * BLK, :])


def _split_store(ref, s, upper_b, vb):
    first = vb * upper_b
    ref[s, 0:BLK, :] = first
    ref[s, BLK:2 * BLK, :] = vb - first


def _sink_rows(sink_ref):
    return [jnp.concatenate([jnp.full((1, BLK), sink_ref[2 * p + e], F32) for p, e in HEADS[s]], axis=1)
            for s in range(2)]


def _stack_heads(ref, slot, half, pairs, s=None):
    for a, (p, e) in enumerate(HEADS[slot if s is None else s]):
        ref[slot, a * BLK:(a + 1) * BLK, :] = jnp.where(half[e], pairs[p], 0.0).astype(BF16)


def _unstack_pair(half, outs, p):
    lo = 0 if p < 2 else 1
    rows = slice(p * BLK, (p + 1) * BLK)
    return jnp.where(half[0], outs[lo][rows, :], outs[1 - lo][rows, :])


def _softmax(sm, sinks):
    m = jnp.maximum(jnp.max(sm, axis=0, keepdims=True), sinks)
    p = jnp.exp(sm - m)
    es = jnp.exp(sinks - m)
    inv = 1.0 / (jnp.sum(p, axis=0, keepdims=True) + es)
    return p * inv, es * inv


def _scores(kk, q_stack, first):
    st = _nt(kk, q_stack)
    prev = st[0:BLK, :]
    if first is not None:
        prev = prev + jnp.where(first, -jnp.inf, 0.0)
    return prev, st[BLK:2 * BLK, :]


def _attn_specs(tile):
    nb = TQ // BLK
    prev = lambda i: jnp.maximum(tile(i) * nb - 1, 0)
    return [
        pl.BlockSpec(memory_space=pltpu.SMEM),
        pl.BlockSpec((TQ, ATTN_W), lambda i: (tile(i), 0)),
        pl.BlockSpec((TQ, ATTN_W), lambda i: (tile(i), 1)),
        pl.BlockSpec((TQ, 2 * KV_W), lambda i: (tile(i), 4)),
        pl.BlockSpec((BLK, 2 * KV_W), lambda i: (prev(i), 4)),
        pl.BlockSpec((TQ, 384), lambda i: (tile(i), 0)),
        pl.BlockSpec((BLK, 384), lambda i: (prev(i), 0)),
    ]


def _attn_fwd(pa, tab, sinks):
    S = pa.shape[0]
    nb = TQ // BLK

    def body(sink_ref, q_ref, g_ref, kvc_ref, kvp_ref, tabc_ref, tabp_ref, o_ref, att_ref, pm_ref, ps_ref,
             qs_ref, kall, p_sc):
        i = pl.program_id(0)
        _fill_kv(kall, kvc_ref, kvp_ref, tabc_ref, tabp_ref)
        lane = lax.broadcasted_iota(jnp.int32, (BLK, 128), 1)
        half = [lane < HEAD_DIM, lane >= HEAD_DIM]
        upper = _upper()
        upper_b = upper.astype(BF16)
        sinks = _sink_rows(sink_ref)
        for j in range(nb):
            rq = slice(j * BLK, (j + 1) * BLK)
            rk = slice(j * BLK, (j + 2) * BLK)
            tab = tabc_ref[rq, :]
            qr = [_rope(q_ref[rq, p * 128:(p + 1) * 128].astype(F32), tab) * 0.125 for p in range(4)]
            outs = []
            for s in range(2):
                _stack_heads(qs_ref, 2 * j + s, half, qr, s)
                prev, cur = _scores(kall[s, rk, :], qs_ref[2 * j + s], i == 0 if j == 0 else None)
                prob, psink = _softmax(jnp.where(upper, prev, cur), sinks[s])
                pb = prob.astype(BF16)
                pm_ref[(2 * j + s) * BLK:(2 * j + s + 1) * BLK, :] = pb
                ps_ref[2 * j + s:2 * j + s + 1, :] = psink
                _split_store(p_sc, s, upper_b, pb)
                outs.append(_tn(p_sc[s], kall[2 + s, rk, :]))
            for p in range(4):
                cols = slice(p * 128, (p + 1) * 128)
                att = _unstack_pair(half, outs, p)
                att_ref[rq, cols] = att.astype(BF16)
                o_ref[rq, cols] = (att * _silu(g_ref[rq, cols].astype(F32))).astype(BF16)

    return pl.pallas_call(
        body,
        name="attn_fwd",
        grid=(S // TQ,),
        in_specs=_attn_specs(lambda i: i),
        out_specs=[pl.BlockSpec((TQ, ATTN_W), lambda i: (i, 0))] * 2 + [
            pl.BlockSpec((2 * TQ, 4 * BLK), lambda i: (i, 0)),
            pl.BlockSpec((2 * nb, 4 * BLK), lambda i: (i, 0)),
            pl.BlockSpec((2 * nb, 4 * BLK, 128), lambda i: (i, 0, 0)),
        ],
        out_shape=[jax.ShapeDtypeStruct((S, ATTN_W), BF16)] * 2 + [
            jax.ShapeDtypeStruct((2 * S, 4 * BLK), BF16),
            jax.ShapeDtypeStruct((2 * S // BLK, 4 * BLK), F32),
            jax.ShapeDtypeStruct((2 * S // BLK, 4 * BLK, 128), BF16),
        ],
        scratch_shapes=[
            pltpu.VMEM((4, BLK + TQ, 128), BF16),
            pltpu.VMEM((2, 2 * BLK, 4 * BLK), BF16),
        ],
        compiler_params=_params(("arbitrary",)),
    )(sinks, pa, pa, pa, pa, tab, tab)


def _shift_down(u, halo_ref, has_prev):
    def halo_u(r):
        hu = halo_ref[r:r + 1, 512:1024].astype(F32) * halo_ref[r:r + 1, 1024:1536].astype(F32)
        return jnp.where(has_prev, hu, 0.0)

    row = lax.broadcasted_iota(jnp.int32, u.shape, 0)
    um1 = jnp.where(row == 0, halo_u(HALO - 1), pltpu.roll(u, 1, 0))
    um2 = jnp.where(row == 0, halo_u(HALO - 2), jnp.where(row == 1, halo_u(HALO - 1), pltpu.roll(u, 2, 0)))
    return um1, um2


def _conv_tile(pc_ref, halo_ref, w_ref, has_prev):
    b = pc_ref[:, 0:512].astype(F32)
    c = pc_ref[:, 512:1024].astype(F32)
    hh = pc_ref[:, 1024:1536].astype(F32)
    gc = pc_ref[:, 1536:2048].astype(F32)
    u = c * hh
    um1, um2 = _shift_down(u, halo_ref, has_prev)
    cv = w_ref[0:1, :] * um2 + w_ref[1:2, :] * um1 + w_ref[2:3, :] * u
    return b, c, hh, gc, u, um1, um2, cv


def _prev_rows(width, col=0):
    return pl.BlockSpec((HALO, width), lambda i: (jnp.maximum(i * (TM // HALO) - 1, 0), col))


def _out_loss(x, target, ya, pc, conv_w, w_out, final_g):
    S = x.shape[0]

    def body(x_ref, t_ref, ya_ref, pc_ref, halo_ref, cw_ref, wo_ref, fg_ref,
             dh_ref, dmix_ref, gwo_ref, gfg_ref, loss_ref):
        @pl.when(pl.program_id(0) == 0)
        def _():
            gwo_ref[...] = jnp.zeros_like(gwo_ref)
            gfg_ref[...] = jnp.zeros_like(gfg_ref)
            loss_ref[...] = jnp.zeros_like(loss_ref)

        b, _, _, gc, _, _, _, cv = _conv_tile(pc_ref, halo_ref, cw_ref, pl.program_id(0) > 0)
        yc = (b * cv * _silu(gc)).astype(BF16)
        mix = jnp.concatenate([ya_ref[...], yc], axis=1)
        wo = wo_ref[...]
        fg = fg_ref[...]
        h = x_ref[...] + _nn(mix, wo)
        r = lax.rsqrt(jnp.mean(h * h, axis=-1, keepdims=True) + EPS)
        n = h * r
        err = n * fg - t_ref[...]
        loss_ref[...] += jnp.broadcast_to(
            0.5 * jnp.sum(jnp.mean(err * err, axis=-1, keepdims=True), axis=0, keepdims=True), (8, 128))
        gfg_ref[...] += jnp.sum(err * n, axis=0, keepdims=True) * (1.0 / D_MODEL)
        dyg = err * (fg * (1.0 / D_MODEL))
        dh = r * (dyg - n * jnp.mean(dyg * n, axis=-1, keepdims=True))
        dh_ref[...] = dh
        dhb = dh.astype(BF16)
        dmix_ref[...] = _nt(dhb, wo).astype(ACT)
        gwo_ref[...] += _tn(mix, dhb)

    row = lambda i: (i, 0)
    fixed = lambda i: (0, 0)
    return pl.pallas_call(
        body,
        name="out_loss",
        grid=(S // TM,),
        in_specs=[
            pl.BlockSpec((TM, D_MODEL), row),
            pl.BlockSpec((TM, D_MODEL), row),
            pl.BlockSpec((TM, ATTN_W), row),
            pl.BlockSpec((TM, PC_W), row),
            _prev_rows(PC_W),
            pl.BlockSpec((CONV_K, CONV_W), fixed),
            pl.BlockSpec((D_MODEL, D_MODEL), fixed),
            pl.BlockSpec((1, D_MODEL), fixed),
        ],
        out_specs=[
            pl.BlockSpec((TM, D_MODEL), row),
            pl.BlockSpec((TM, D_MODEL), row),
            pl.BlockSpec((D_MODEL, D_MODEL), fixed),
            pl.BlockSpec((1, D_MODEL), fixed),
            pl.BlockSpec((8, 128), fixed),
        ],
        out_shape=[
            jax.ShapeDtypeStruct((S, D_MODEL), F32),
            jax.ShapeDtypeStruct((S, D_MODEL), ACT),
            jax.ShapeDtypeStruct((D_MODEL, D_MODEL), F32),
            jax.ShapeDtypeStruct((1, D_MODEL), F32),
            jax.ShapeDtypeStruct((8, 128), F32),
        ],
        compiler_params=_params(("arbitrary",)),
    )(x, target, ya, pc, pc, conv_w, w_out, final_g)


def _attn_bwd(pa, dmix, att, probs, psinks, q_stack, tab):
    S = pa.shape[0]
    nt = S // TQ
    nb = TQ // BLK

    def body(g_ref, kvc_ref, kvp_ref, tabc_ref, tabp_ref, dm_ref, att_ref, pm_ref, ps_ref, qs_ref,
             d_ref, dsink_ref, kall, dkv, carry, do_sc, p_sc, ds_sc, dsink_acc):
        step = pl.program_id(0)

        @pl.when(step == 0)
        def _():
            carry[...] = jnp.zeros_like(carry)
            dsink_acc[...] = jnp.zeros_like(dsink_acc)

        _fill_kv(kall, kvc_ref, kvp_ref, tabc_ref, tabp_ref)
        dkv[0:TQ, :] = jnp.zeros((TQ, 2 * KV_W), F32)
        dkv[TQ:TQ + BLK, :] = carry[...]
        lane = lax.broadcasted_iota(jnp.int32, (BLK, 128), 1)
        half = [lane < HEAD_DIM, lane >= HEAD_DIM]
        upper = _upper()
        upper_b = upper.astype(BF16)
        for j in range(nb):
            rq = slice(j * BLK, (j + 1) * BLK)
            rk = slice(j * BLK, (j + 2) * BLK)
            tab = tabc_ref[rq, :]
            pair = [slice(p * 128, (p + 1) * 128) for p in range(4)]
            g = [g_ref[rq, c].astype(F32) for c in pair]
            da = [dm_ref[rq, c].astype(F32) for c in pair]
            gate = [_silu_and_grad(g[p]) for p in range(4)]
            do = [da[p] * gate[p][0] for p in range(4)]
            dqs, dks, dvs = [], [], []
            for s in range(2):
                kk = kall[s, rk, :]
                vv = kall[2 + s, rk, :]
                _stack_heads(do_sc, s, half, do)
                pb = pm_ref[(2 * j + s) * BLK:(2 * j + s + 1) * BLK, :]
                prob = pb.astype(F32)
                _split_store(p_sc, s, upper_b, pb)
                dprob = _merge(upper, _nt(vv, do_sc[s]))
                dsum = jnp.sum(dprob * prob, axis=0, keepdims=True)
                _split_store(ds_sc, s, upper_b, (prob * (dprob - dsum)).astype(BF16))
                dsink_acc[s, 0:1, :] += ps_ref[2 * j + s:2 * j + s + 1, :] * dsum
                dqs.append(_tn(ds_sc[s], kk))
                dks.append(_nn(ds_sc[s], qs_ref[2 * j + s]))
                dvs.append(_nn(p_sc[s], do_sc[s]))
            for p in range(4):
                d_ref[rq, pair[p]] = _rope_t(_unstack_pair(half, dqs, p) * 0.125, tab).astype(BF16)
                d_ref[rq, 512 + p * 128:512 + (p + 1) * 128] = (
                    da[p] * att_ref[rq, pair[p]].astype(F32) * gate[p][1]).astype(BF16)
            dkv[rk, 0:128] += dks[0] + pltpu.roll(dks[1], 64, 1)
            dkv[rk, 128:256] += dvs[0] + pltpu.roll(dvs[1], 64, 1)
        d_ref[:, 1024:1152] = _rope_t(dkv[BLK:BLK + TQ, 0:128], tabc_ref[...]).astype(BF16)
        d_ref[:, 1152:1280] = dkv[BLK:BLK + TQ, 128:256].astype(BF16)
        carry[...] = dkv[0:BLK, :]

        @pl.when(step == nt - 1)
        def _():
            lanes = lax.broadcasted_iota(jnp.int32, (8, 128), 1)
            out = jnp.zeros((8, 128), F32)
            for s in range(2):
                for a, (p, e) in enumerate(HEADS[s]):
                    tot = jnp.sum(dsink_acc[s, 0:1, a * BLK:(a + 1) * BLK], axis=1, keepdims=True)
                    out = jnp.where(lanes == 2 * p + e, -tot, out)
            dsink_ref[...] = out

    rev = lambda s: nt - 1 - s
    return pl.pallas_call(
        body,
        name="attn_bwd",
        grid=(nt,),
        in_specs=_attn_specs(rev)[2:] + [pl.BlockSpec((TQ, ATTN_W), lambda s: (nt - 1 - s, 0))] * 2 + [
            pl.BlockSpec((2 * TQ, 4 * BLK), lambda s: (nt - 1 - s, 0)),
            pl.BlockSpec((2 * nb, 4 * BLK), lambda s: (nt - 1 - s, 0)),
            pl.BlockSpec((2 * nb, 4 * BLK, 128), lambda s: (nt - 1 - s, 0, 0)),
        ],
        out_specs=[
            pl.BlockSpec((TQ, PA_W), lambda s: (nt - 1 - s, 0)),
            pl.BlockSpec((8, 128), lambda s: (0, 0)),
        ],
        out_shape=[
            jax.ShapeDtypeStruct((S, PA_W), BF16),
            jax.ShapeDtypeStruct((8, 128), F32),
        ],
        scratch_shapes=[
            pltpu.VMEM((4, BLK + TQ, 128), BF16),
            pltpu.VMEM((BLK + TQ, 2 * KV_W), F32),
            pltpu.VMEM((BLK, 2 * KV_W), F32),
            pltpu.VMEM((2, 4 * BLK, 128), BF16),
            pltpu.VMEM((2, 2 * BLK, 4 * BLK), BF16),
            pltpu.VMEM((2, 2 * BLK, 4 * BLK), BF16),
            pltpu.VMEM((2, 8, 4 * BLK), F32),
        ],
        compiler_params=_params(("arbitrary",)),
    )(pa, pa, pa, tab, tab, dmix, att, probs, psinks, q_stack)


def _conv_bwd_tile(pc_ref, prev_ref, next_ref, dm_ref, dmn_ref, w_ref, d_ref, gw_ref, has_prev, has_next,
                   on_piece):
    rows = pc_ref.shape[0]
    w0, w1, w2 = w_ref[0:1, :], w_ref[1:2, :], w_ref[2:3, :]
    b, c, hh, gc, u, um1, um2, cv = _conv_tile(pc_ref, prev_ref, w_ref, has_prev)
    sg, dsg = _silu_and_grad(gc)
    dy = dm_ref[...].astype(F32)
    dyb = dy * b
    dcv = dyb * sg

    def next_dcv(r):
        nd = (dmn_ref[r:r + 1, :].astype(F32) * next_ref[r:r + 1, 0:512].astype(F32)
              * _silu(next_ref[r:r + 1, 1536:2048].astype(F32)))
        return jnp.where(has_next, nd, 0.0)

    row = lax.broadcasted_iota(jnp.int32, (rows, CONV_W), 0)
    dp1 = jnp.where(row == rows - 1, next_dcv(0), pltpu.roll(dcv, rows - 1, 0))
    dp2 = jnp.where(row == rows - 1, next_dcv(1),
                    jnp.where(row == rows - 2, next_dcv(0), pltpu.roll(dcv, rows - 2, 0)))
    du = w2 * dcv + w1 * dp1 + w0 * dp2
    pieces = (lambda: dy * cv * sg, lambda: du * hh, lambda: du * c, lambda: dyb * cv * dsg)
    for k, piece in enumerate(pieces):
        d_ref[:, k * CONV_W:(k + 1) * CONV_W] = piece().astype(BF16)
        on_piece(k)
    gw_ref[0:1, :] += jnp.sum(dcv * um2, axis=0, keepdims=True)
    gw_ref[1:2, :] += jnp.sum(dcv * um1, axis=0, keepdims=True)
    gw_ref[2:3, :] += jnp.sum(dcv * u, axis=0, keepdims=True)


def _grad_x(da, dc, wt, x, dh, norm_g, small, grads):
    S = x.shape[0]
    n_steps = S // TM
    rs = _ReduceScatter(grads)
    n_rs_out = len(rs.out_shape())
    small_rows = 8 + small.shape[0]

    def body(da_ref, dc_ref, wt_ref, x_ref, dh_ref, g_ref, small_ref, *rest):
        grad_refs, rest = rest[:rs.n], rest[rs.n:]
        gx_ref, all_ref = rest[:2]
        rs_out, rest = rest[2:2 + n_rs_out], rest[2 + n_rs_out:]
        gng, stage, small_send, small_recv, small_own = rest[:5]
        rs_scratch = rest[5:]
        step = pl.program_id(0)
        finish = rs.emit(step, n_steps, grad_refs, rs_out, rs_scratch)

        @pl.when(step == 0)
        def _():
            gng[...] = jnp.zeros_like(gng)

        dxn = (_nn(da_ref[:, 0:512], wt_ref[0:512, :]) + _nn(da_ref[:, 512:1024], wt_ref[768:1280, :])
               + _nn(da_ref[:, 1024:1280], wt_ref[512:768, :]) + _nn(dc_ref[...], wt_ref[1280:3328, :]))
        xv = x_ref[...]
        r = lax.rsqrt(jnp.mean(xv * xv, axis=-1, keepdims=True) + EPS)
        n = xv * r
        gng[...] += jnp.sum(dxn * n, axis=0, keepdims=True)
        dxg = dxn * g_ref[...]
        gx_ref[...] = dh_ref[...] + r * (dxg - n * jnp.mean(dxg * n, axis=-1, keepdims=True))

        @pl.when(step == n_steps - 1)
        def _():
            x_, y_, c_ = lax.axis_index("x"), lax.axis_index("y"), lax.axis_index("c")
            me = 4 * x_ + 2 * y_ + c_
            for q in range(8):
                stage[q:q + 1, :] = gng[:, q * 128:(q + 1) * 128]
            stage[8:small_rows, :] = small_ref[...]
            own = pltpu.make_async_copy(stage, all_ref.at[me], small_own)
            own.start()
            sends = []
            for k in range(1, N_DEV):
                cp = pltpu.make_async_remote_copy(
                    src_ref=stage, dst_ref=all_ref.at[me],
                    send_sem=small_send.at[k - 1], recv_sem=small_recv.at[k - 1],
                    device_id=(x_ ^ (k >> 2), y_ ^ ((k >> 1) & 1), c_ ^ (k & 1)), device_id_type=MESH)
                cp.start()
                sends.append(cp)
            for cp in sends:
                cp.wait_send()
                cp.wait_recv()
            own.wait()

        finish()

    row = lambda i: (i, 0)
    fixed = lambda i: (0, 0)
    any_spec = pl.BlockSpec(memory_space=pl.ANY)
    outs = pl.pallas_call(
        body,
        name="grad_x_reduce_scatter",
        grid=(n_steps,),
        in_specs=[
            pl.BlockSpec((TM, PA_W), row),
            pl.BlockSpec((TM, PC_W), row),
            pl.BlockSpec((IN_W, D_MODEL), fixed),
            pl.BlockSpec((TM, D_MODEL), row),
            pl.BlockSpec((TM, D_MODEL), row),
            pl.BlockSpec((1, D_MODEL), fixed),
            pl.BlockSpec(small.shape, fixed),
        ] + [any_spec] * rs.n,
        out_specs=[pl.BlockSpec((TM, D_MODEL), row), any_spec] + [any_spec] * n_rs_out,
        out_shape=[jax.ShapeDtypeStruct((S, D_MODEL), F32),
                   jax.ShapeDtypeStruct((N_DEV, small_rows, 128), F32)] + rs.out_shape(),
        scratch_shapes=[
            pltpu.VMEM((1, D_MODEL), F32),
            pltpu.VMEM((small_rows, 128), F32),
            pltpu.SemaphoreType.DMA((N_DEV - 1,)),
            pltpu.SemaphoreType.DMA((N_DEV - 1,)),
            pltpu.SemaphoreType.DMA,
        ] + rs.scratch_shapes(),
        compiler_params=_params(("arbitrary",)),
    )(da, dc, wt, x, dh, norm_g, small, *grads)
    return outs[0], outs[1], outs[2:2 + rs.n], outs[2 + rs.n:2 + 2 * rs.n]


def _grad_w_in(da, pc, dmix, conv_w, xn):
    S = xn.shape[0]
    tm = 2 * TM
    nt = S // tm
    t16 = tm // HALO

    def body(da_ref, pc_ref, prev_ref, next_ref, dm_ref, dmn_ref, cw_ref, xn_ref, gw_ref, dc_ref, gcw_ref):
        i = pl.program_id(0)

        @pl.when(i == 0)
        def _():
            gw_ref[...] = jnp.zeros_like(gw_ref)
            gcw_ref[...] = jnp.zeros_like(gcw_ref)

        xn = xn_ref[...]
        gw_ref[0:512, :] += _tn(da_ref[:, 0:512], xn)
        gw_ref[768:1280, :] += _tn(da_ref[:, 512:1024], xn)
        gw_ref[512:768, :] += _tn(da_ref[:, 1024:1280], xn)
        def piece_grad(k):
            rows = slice(PA_W + k * CONV_W, PA_W + (k + 1) * CONV_W)
            gw_ref[rows, :] += _tn(dc_ref[:, k * CONV_W:(k + 1) * CONV_W], xn)

        _conv_bwd_tile(pc_ref, prev_ref, next_ref, dm_ref, dmn_ref, cw_ref, dc_ref, gcw_ref, i > 0, i < nt - 1,
                       piece_grad)

    row = lambda i: (i, 0)
    fixed = lambda i: (0, 0)
    nxt = lambda i: jnp.minimum((i + 1) * t16, nt * t16 - 1)
    return pl.pallas_call(
        body,
        name="grad_w_in",
        grid=(nt,),
        in_specs=[
            pl.BlockSpec((tm, PA_W), row),
            pl.BlockSpec((tm, PC_W), row),
            pl.BlockSpec((HALO, PC_W), lambda i: (jnp.maximum(i * t16 - 1, 0), 0)),
            pl.BlockSpec((HALO, PC_W), lambda i: (nxt(i), 0)),
            pl.BlockSpec((tm, CONV_W), lambda i: (i, 1)),
            pl.BlockSpec((HALO, CONV_W), lambda i: (nxt(i), 1)),
            pl.BlockSpec((CONV_K, CONV_W), fixed),
            pl.BlockSpec((tm, D_MODEL), row),
        ],
        out_specs=[
            pl.BlockSpec((IN_W, D_MODEL), fixed, pipeline_mode=pl.Buffered(1)),
            pl.BlockSpec((tm, PC_W), row),
            pl.BlockSpec((CONV_K, CONV_W), fixed),
        ],
        out_shape=[
            jax.ShapeDtypeStruct((IN_W, D_MODEL), F32),
            jax.ShapeDtypeStruct((S, PC_W), BF16),
            jax.ShapeDtypeStruct((CONV_K, CONV_W), F32),
        ],
        compiler_params=_params(("arbitrary",)),
    )(da, pc, pc, pc, dmix, dmix, conv_w, xn)


def _adam_update(w, g, m, v):
    c1 = 1.0 - ADAM_B1 ** ADAM_STEP
    c2 = 1.0 - ADAM_B2 ** ADAM_STEP
    nm = ADAM_B1 * m + (1.0 - ADAM_B1) * g
    nv = ADAM_B2 * v + (1.0 - ADAM_B2) * (g * g)
    return -ADAM_LR * ((nm / c1) / (jnp.sqrt(nv / c2) + ADAM_EPS) + ADAM_WD * w), nm, nv


def _sum_chips_adamw(own, others, w, m, v, name):
    def body(own_ref, p_ref, w_ref, m_ref, v_ref, g_ref, d_ref, nm_ref, nv_ref):
        g = own_ref[...]
        for k in range(N_CHIP - 1):
            g = g + p_ref[k].astype(F32)
        g_ref[...] = g
        d_ref[...], nm_ref[...], nv_ref[...] = _adam_update(w_ref[...], g, m_ref[...], v_ref[...])

    rows, cols = w.shape
    half = rows // 2
    blk = pl.BlockSpec((half, cols), lambda i: (i, 0))
    shape = jax.ShapeDtypeStruct(w.shape, F32)
    return pl.pallas_call(
        body,
        name=name,
        grid=(2,),
        in_specs=[blk, pl.BlockSpec((N_CHIP - 1, half, cols), lambda i: (0, i, 0)), blk, blk, blk],
        out_specs=[blk] * 4,
        out_shape=[shape] * 4,
        compiler_params=_params(("arbitrary",)),
    )(own, others, w, m, v)


SMALL_ROWS = 96


def _small_adamw(parts, params):
    def body(parts_ref, *rest):
        prm, outs, total = rest[:12], rest[12:29], rest[29]
        me = 4 * lax.axis_index("x") + 2 * lax.axis_index("y") + lax.axis_index("c")
        acc = parts_ref[0]
        for d in range(1, N_DEV):
            acc = acc + parts_ref[d]
        total[...] = acc
        grads = (total[0:8, :], total[8:16, :], total[16:17, 0:8],
                 total[pl.ds(pl.multiple_of(32 + me * 8, 8), CONV_K), 0:64])
        outs[0][...] = total[24:25, 0:1]
        for k, g in enumerate(grads):
            w_ref, m_ref, v_ref = prm[3 * k:3 * k + 3]
            g_ref, d_ref, nm_ref, nv_ref = outs[1 + 4 * k:5 + 4 * k]
            g_ref[...] = g
            d_ref[...], nm_ref[...], nv_ref[...] = _adam_update(w_ref[...], g, m_ref[...], v_ref[...])

    flat = [a for p in params for a in p]
    out_shape = [jax.ShapeDtypeStruct((1, 1), F32)]
    for p in params:
        out_shape += [jax.ShapeDtypeStruct(p[0].shape, F32)] * 4
    return pl.pallas_call(
        body,
        name="adamw_small",
        out_shape=out_shape,
        scratch_shapes=[pltpu.VMEM((SMALL_ROWS, 128), F32)],
        compiler_params=_params(),
    )(parts, *flat)


def kernel(x, norm_g, w_in, sinks, conv_w, w_out, final_g, loss_target, m_norm_g, m_w_in, m_sinks, m_conv_w, m_w_out, m_final_g, v_norm_g, v_w_in, v_sinks, v_conv_w, v_w_out, v_final_g):
    S = x.shape[1]
    x2 = x.reshape(S, D_MODEL)
    t2 = loss_target.reshape(S, D_MODEL)
    ng = norm_g.reshape(1, D_MODEL)
    fg = final_g.reshape(1, D_MODEL)

    cw_pad = jnp.zeros((8, 128), F32).at[0:CONV_K, 0:64].set(conv_w)
    xn, tab, wt = _prologue(x2, ng, w_in.T.astype(BF16))
    pa, pc, (wo, cw_all) = _fwd_proj(xn, wt, [w_out.astype(BF16), cw_pad])
    cw = cw_all.reshape(N_DEV, 8, 128)[:, 0:CONV_K, 0:64].transpose(1, 0, 2).reshape(CONV_K, CONV_W)
    ya, att, probs, psinks, q_stack = _attn_fwd(pa, tab, sinks)
    dh, dmix, g_wo, g_fg, loss_part = _out_loss(x2, t2, ya, pc, cw, wo, fg)
    da, g_sinks = _attn_bwd(pa, dmix, att, probs, psinks, q_stack, tab)
    g_wt, dc, g_cw = _grad_w_in(da, pc, dmix, cw, xn)
    cw_pack = jnp.pad(g_cw.reshape(CONV_K, N_DEV, 64).transpose(1, 0, 2),
                      ((0, 0), (0, 8 - CONV_K), (0, 64))).reshape(N_DEV * 8, 128)
    small = jnp.concatenate([g_fg.reshape(8, 128), g_sinks, loss_part, cw_pack], axis=0)
    grad_x, parts, own, others = _grad_x(
        da, dc, wt, x2, dh, ng, small,
        [g_wt.reshape(N_DEV, SHARD_IN, D_MODEL), g_wo.reshape(N_DEV, SHARD_OUT, D_MODEL)])
    gt, dt, nmt, nvt = _sum_chips_adamw(own[0], others[0], w_in.T, m_w_in.T, v_w_in.T, "adamw_w_in")
    grad_w_in, d_w_in, nm_w_in, nv_w_in = gt.T, dt.T, nmt.T, nvt.T
    grad_w_out, d_w_out, nm_w_out, nv_w_out = _sum_chips_adamw(
        own[1], others[1], w_out, m_w_out, v_w_out, "adamw_w_out")
    vec = lambda a: a.reshape(8, 128)
    row = lambda a: a.reshape(1, 8)
    res = _small_adamw(parts, [
        (vec(norm_g), vec(m_norm_g), vec(v_norm_g)), (vec(final_g), vec(m_final_g), vec(v_final_g)),
        (row(sinks), row(m_sinks), row(v_sinks)), (conv_w, m_conv_w, v_conv_w)])
    loss = res[0].reshape(())
    grad_norm_g, d_ng, nm_ng, nv_ng = [a.reshape(D_MODEL) for a in res[1:5]]
    grad_final_g, d_fg, nm_fg, nv_fg = [a.reshape(D_MODEL) for a in res[5:9]]
    grad_sinks, d_sk, nm_sk, nv_sk = [a.reshape(N_Q_HEADS) for a in res[9:13]]
    grad_conv_w, d_cw, nm_cw, nv_cw = res[13:17]

    return (loss, grad_x.reshape(1, S, D_MODEL), grad_norm_g, grad_w_in, grad_sinks, grad_conv_w, grad_w_out, grad_final_g,
            d_ng, d_w_in, d_sk, d_cw, d_w_out, d_fg,
            nm_ng, nm_w_in, nm_sk, nm_cw, nm_w_out, nm_fg,
            nv_ng, nv_w_in, nv_sk, nv_cw, nv_w_out, nv_fg)
```

```python
import jax
import jax.numpy as jnp
from jax import lax
from jax.experimental import pallas as pl
from jax.experimental.pallas import tpu as pltpu

F32 = jnp.float32
BF16 = jnp.bfloat16
MESH = pl.DeviceIdType.MESH

D_MODEL = 1024
HEAD_DIM = 64
N_Q_HEADS = 8
ATTN_W = 512
KV_W = 128
BLK = 128
CONV_W = 512
CONV_K = 3
IN_W = 3328
PA_W = 1280
PC_W = 2048
EPS = 1e-5
ROPE_THETA = 500000.0
ROT_DIM = 16
N_DEV = 8
N_CHIP = 4
SHARD_IN = IN_W // N_DEV
SHARD_OUT = D_MODEL // N_DEV

ADAM_LR = 0.001
ADAM_B1 = 0.9
ADAM_B2 = 0.999
ADAM_EPS = 1e-08
ADAM_WD = 0.01
ADAM_STEP = 10

ACT = jnp.bfloat16

TM = 512
TQ = 1024
HALO = 16
VMEM_LIMIT = 56 * 1024 * 1024

NT_DIMS = (((1,), (1,)), ((), ()))
TN_DIMS = (((0,), (0,)), ((), ()))


def _params(sem=None):
    kw = dict(vmem_limit_bytes=VMEM_LIMIT)
    if sem is not None:
        kw["dimension_semantics"] = sem
    return pltpu.CompilerParams(**kw)


def _nt(a, b):
    return lax.dot_general(a, b, NT_DIMS, preferred_element_type=F32)


def _tn(a, b):
    return lax.dot_general(a, b, TN_DIMS, preferred_element_type=F32)


def _nn(a, b):
    return jnp.dot(a, b, preferred_element_type=F32)


def _silu(g):
    return g * jax.nn.sigmoid(g)


def _silu_and_grad(g):
    s = jax.nn.sigmoid(g)
    return g * s, s * (1.0 + g * (1.0 - s))


class _AllGatherInSteps:
    def __init__(self, arrs, forward_step):
        self.blocks = [(a.shape, a.dtype) for a in arrs]
        self.n = len(arrs)
        self.forward_step = forward_step

    def out_shape(self):
        return [jax.ShapeDtypeStruct((N_DEV * s[0], s[1]), d) for s, d in self.blocks]

    def scratch_shapes(self):
        return [pltpu.SemaphoreType.DMA((7 * self.n,)), pltpu.SemaphoreType.DMA((7 * self.n,)),
                pltpu.SemaphoreType.DMA((self.n,))]

    def emit(self, step, n_steps, x_refs, out_refs, scratch):
        assert n_steps > self.forward_step + 1
        send_sems, recv_sems, local_sems = scratch
        x, y, c = lax.axis_index("x"), lax.axis_index("y"), lax.axis_index("c")
        me, sibling = (x, y, c), (x, y, 1 - c)
        chips = [(1 - x, y), (x, 1 - y), (1 - x, 1 - y)]

        def rows(a, px, py, pc):
            m = self.blocks[a][0][0]
            return out_refs[a].at[pl.ds((4 * px + 2 * py + pc) * m, m), :]

        def copy(a, k, block, to, src=None):
            return pltpu.make_async_remote_copy(
                src_ref=rows(a, *block) if src is None else src, dst_ref=rows(a, *block),
                send_sem=send_sems.at[a * 7 + k], recv_sem=recv_sems.at[a * 7 + k],
                device_id=to, device_id_type=MESH)

        def mine(a):
            return pltpu.make_async_copy(x_refs[a], rows(a, *me), local_sems.at[a])

        def first(a):
            return ([copy(a, 0, me, sibling, src=x_refs[a])]
                    + [copy(a, 1 + j, me, (*chip, c), src=x_refs[a]) for j, chip in enumerate(chips)])

        def passed(a):
            return [copy(a, 4 + j, (*chip, c), sibling) for j, chip in enumerate(chips)]

        @pl.when(step == 0)
        def _():
            for a in range(self.n):
                mine(a).start()
                for cp in first(a):
                    cp.start()

        @pl.when(step == self.forward_step)
        def _():
            for j, chip in enumerate(chips):
                for a in range(self.n):
                    copy(a, 1 + j, (*chip, c), me).wait_recv()
                    copy(a, 4 + j, (*chip, c), sibling).start()

        def finish():
            @pl.when(step == n_steps - 1)
            def _():
                for a in range(self.n):
                    copy(a, 0, sibling, me).wait_recv()
                    for j, chip in enumerate(chips):
                        copy(a, 4 + j, (*chip, 1 - c), me).wait_recv()
                    for cp in first(a) + passed(a):
                        cp.wait_send()
                    mine(a).wait()

        return finish


class _AllGatherViaNeighbours:
    def __init__(self, arr, first, mid, second):
        (self.m, self.ncol), self.dtype = arr.shape, arr.dtype
        assert self.m % 32 == 0
        self.first, self.mid, self.second = first, mid, second

    def out_shape(self):
        return [jax.ShapeDtypeStruct((N_DEV * self.m, self.ncol), self.dtype)]

    def scratch_shapes(self):
        return [pltpu.SemaphoreType.DMA((11,)), pltpu.SemaphoreType.DMA((11,)), pltpu.SemaphoreType.DMA]

    def emit(self, step, n_steps, x_ref, out_ref, scratch):
        assert 0 < self.first < self.mid < self.second < n_steps - 1
        send_sems, recv_sems, local_sem = scratch
        x, y, c = lax.axis_index("x"), lax.axis_index("y"), lax.axis_index("c")
        half = self.m // 2
        sibling, xn, yn = (x, y, 1 - c), (1 - x, y, c), (x, 1 - y, c)

        def rows(dev, part=None):
            px, py, pc = dev
            base = (4 * px + 2 * py + pc) * self.m
            if part is None:
                return out_ref.at[pl.ds(base, self.m), :]
            return out_ref.at[pl.ds(base + part * half, half), :]

        def copy(k, dev, to, part=None, src=None):
            return pltpu.make_async_remote_copy(
                src_ref=rows(dev, part) if src is None else src, dst_ref=rows(dev, part),
                send_sem=send_sems.at[k], recv_sem=recv_sems.at[k], device_id=to, device_id_type=MESH)

        me, dg = (x, y, c), (1 - x, 1 - y, c)
        mine = pltpu.make_async_copy(x_ref, rows(me), local_sem)
        my_half = lambda part: x_ref.at[pl.ds(part * half, half), :]
        sends = [
            copy(0, me, sibling, src=x_ref), copy(1, me, xn, part=0, src=my_half(0)),
            copy(2, me, yn, part=1, src=my_half(1)), copy(3, xn, yn, part=0), copy(4, yn, xn, part=1),
            copy(5, xn, sibling), copy(6, yn, sibling), copy(7, dg, sibling, part=0), copy(8, dg, sibling, part=1),
            copy(9, me, xn, part=1, src=my_half(1)), copy(10, me, yn, part=0, src=my_half(0)),
        ]
        other = lambda dev: (dev[0], dev[1], 1 - c)
        arrivals = [
            copy(0, other(me), sibling), copy(1, xn, xn, part=0), copy(2, yn, yn, part=1), copy(3, dg, yn, part=0),
            copy(4, dg, xn, part=1), copy(5, other(xn), sibling), copy(6, other(yn), sibling),
            copy(7, other(dg), sibling, part=0), copy(8, other(dg), sibling, part=1),
            copy(9, xn, xn, part=1), copy(10, yn, yn, part=0),
        ]

        @pl.when(step == 0)
        def _():
            mine.start()
            for k in (0, 1, 2, 9, 10):
                sends[k].start()

        @pl.when(step == self.first)
        def _():
            arrivals[1].wait_recv()
            sends[3].start()
            arrivals[2].wait_recv()
            sends[4].start()

        @pl.when(step == self.mid)
        def _():
            arrivals[9].wait_recv()
            sends[5].start()
            arrivals[10].wait_recv()
            sends[6].start()

        @pl.when(step == self.second)
        def _():
            arrivals[3].wait_recv()
            sends[7].start()
            arrivals[4].wait_recv()
            sends[8].start()

        def finish():
            @pl.when(step == n_steps - 1)
            def _():
                for k in (0, 5, 6, 7, 8):
                    arrivals[k].wait_recv()
                for cp in sends:
                    cp.wait_send()
                mine.wait()

        return finish


class _ReduceScatter:
    def __init__(self, grads):
        self.shapes = [g.shape[1:] for g in grads]
        self.n = len(grads)
        self.items = tuple((a, r) for r in (1, 2, 3, 0) for a in range(self.n))
        self.steps = N_CHIP + 2

    def out_shape(self):
        own = [jax.ShapeDtypeStruct(s, F32) for s in self.shapes]
        ici = [jax.ShapeDtypeStruct((N_CHIP - 1,) + s, BF16) for s in self.shapes]
        land = [jax.ShapeDtypeStruct((N_CHIP,) + s, F32) for s in self.shapes]
        return own + ici + land

    def scratch_shapes(self):
        n_items = len(self.items)
        return ([pltpu.VMEM((2,) + s, F32) for s in self.shapes]
                + [pltpu.VMEM((N_CHIP - 1,) + s, BF16) for s in self.shapes]
                + [pltpu.VMEM(s, F32) for s in self.shapes]
                + [pltpu.SemaphoreType.DMA((self.n * N_CHIP,))] * 2
                + [pltpu.SemaphoreType.DMA((2 * n_items,))]
                + [pltpu.SemaphoreType.DMA((self.n * (N_CHIP - 1),))] * 2
                + [pltpu.SemaphoreType.DMA((self.n,))])

    def emit(self, step, n_steps, g_refs, out_refs, scratch):
        assert n_steps > self.steps
        n = self.n
        own_refs, ici_refs, land_refs = out_refs[:n], out_refs[n:2 * n], out_refs[2 * n:]
        stage, pair_bf, pair_own = scratch[:n], scratch[n:2 * n], scratch[2 * n:3 * n]
        sib_send, sib_recv, load_sems, ici_send, ici_recv, own_sems = scratch[3 * n:]
        x, y, c = lax.axis_index("x"), lax.axis_index("y"), lax.axis_index("c")

        def chip_of(r):
            return (x ^ (r >> 1), y ^ (r & 1))

        def block_of(r, core):
            cx, cy = chip_of(r)
            return 4 * cx + 2 * cy + core

        def to_sibling(a, r):
            return pltpu.make_async_remote_copy(
                src_ref=g_refs[a].at[block_of(r, 1 - c)], dst_ref=land_refs[a].at[r],
                send_sem=sib_send.at[a * N_CHIP + r], recv_sem=sib_recv.at[a * N_CHIP + r],
                device_id=(x, y, 1 - c), device_id_type=MESH)

        def loads(k):
            a, r = self.items[k]
            return (pltpu.make_async_copy(g_refs[a].at[block_of(r, c)], stage[a].at[0], load_sems.at[2 * k]),
                    pltpu.make_async_copy(land_refs[a].at[r], stage[a].at[1], load_sems.at[2 * k + 1]))

        def to_owner(k):
            a, r = self.items[k]
            if r == 0:
                return pltpu.make_async_copy(pair_own[a], own_refs[a], own_sems.at[a])
            return pltpu.make_async_remote_copy(
                src_ref=pair_bf[a].at[r - 1], dst_ref=ici_refs[a].at[r - 1],
                send_sem=ici_send.at[a * (N_CHIP - 1) + r - 1], recv_sem=ici_recv.at[a * (N_CHIP - 1) + r - 1],
                device_id=(*chip_of(r), c), device_id_type=MESH)

        @pl.when(step == 0)
        def _():
            for a, r in self.items:
                to_sibling(a, r).start()

        def fetch(k):
            a, r = self.items[k]
            to_sibling(a, r).wait_recv()
            for cp in loads(k):
                cp.start()

        def add_and_send(k):
            a, r = self.items[k]
            for cp in loads(k):
                cp.wait()
            total = stage[a][0] + stage[a][1]
            if r == 0:
                pair_own[a][...] = total
            else:
                pair_bf[a][r - 1] = total.astype(BF16)
            to_owner(k).start()

        for g in range(N_CHIP + 1):
            @pl.when(step == 1 + g)
            def _(g=g):
                if g > 0:
                    for k in range((g - 1) * n, g * n):
                        add_and_send(k)
                if g < N_CHIP:
                    for k in range(g * n, (g + 1) * n):
                        fetch(k)

        def finish():
            @pl.when(step == n_steps - 1)
            def _():
                for k, (a, r) in enumerate(self.items):
                    if r == 0:
                        to_owner(k).wait()
                    else:
                        to_owner(k).wait_send()
                        to_owner(k).wait_recv()
                for a, r in self.items:
                    to_sibling(a, r).wait_send()

        return finish


def _prologue(x, norm_g, w_shard):
    S = x.shape[0]
    n_steps = S // TM
    half = ROT_DIM // 2
    pos = jnp.arange(S, dtype=jnp.int32).astype(F32)
    inv_freq = ROPE_THETA ** (-jnp.arange(0, ROT_DIM, 2, dtype=F32) / ROT_DIM)
    ang = inv_freq[:, None] * pos[None, :]
    cs = jnp.concatenate([jnp.cos(ang), jnp.sin(ang)], axis=0)
    ag = _AllGatherViaNeighbours(w_shard, first=n_steps // 4, mid=n_steps // 2 + 2, second=n_steps - 2)

    def body(x_ref, g_ref, cs_ref, w_ref, xn_ref, tab_ref, wt_ref, *ag_scratch):
        step = pl.program_id(0)
        finish = ag.emit(step, n_steps, w_ref, wt_ref, ag_scratch)
        xv = x_ref[...]
        r = lax.rsqrt(jnp.mean(xv * xv, axis=-1, keepdims=True) + EPS)
        xn_ref[...] = (xv * r * g_ref[...]).astype(BF16)

        xt = jnp.concatenate([cs_ref[...], jnp.zeros((128 - 2 * half, TM), F32)], axis=0).T
        lane = lax.broadcasted_iota(jnp.int32, (TM, 128), 1)
        rr = lane & (HEAD_DIM - 1)
        first = lane < HEAD_DIM

        def at(shift_first, shift_second):
            return jnp.where(first, pltpu.roll(xt, shift_first, 1) if shift_first else xt,
                             pltpu.roll(xt, shift_second, 1))

        cos_lo, cos_hi = at(0, HEAD_DIM), at(half, HEAD_DIM + half)
        sin_lo, sin_hi = at(128 - half, HEAD_DIM - half), at(0, HEAD_DIM)
        tab_ref[:, 0:128] = jnp.where(rr < half, cos_lo, jnp.where(rr < ROT_DIM, cos_hi, 1.0))
        tab_ref[:, 128:256] = jnp.where(rr < half, -sin_lo, 0.0)
        tab_ref[:, 256:384] = jnp.where((rr >= half) & (rr < ROT_DIM), sin_hi, 0.0)
        finish()

    any_spec = pl.BlockSpec(memory_space=pl.ANY)
    return pl.pallas_call(
        body,
        name="prologue_all_gather_w_in",
        grid=(n_steps,),
        in_specs=[
            pl.BlockSpec((TM, D_MODEL), lambda i: (i, 0)),
            pl.BlockSpec((1, D_MODEL), lambda i: (0, 0)),
            pl.BlockSpec((2 * half, TM), lambda i: (0, i)),
            any_spec,
        ],
        out_specs=[
            pl.BlockSpec((TM, D_MODEL), lambda i: (i, 0)),
            pl.BlockSpec((TM, 384), lambda i: (i, 0)),
            any_spec,
        ],
        out_shape=[
            jax.ShapeDtypeStruct((S, D_MODEL), BF16),
            jax.ShapeDtypeStruct((S, 384), F32),
        ] + ag.out_shape(),
        scratch_shapes=ag.scratch_shapes(),
        compiler_params=_params(("arbitrary",)),
    )(x, norm_g, cs, w_shard)


def _fwd_proj(xn, wt, later):
    S = xn.shape[0]
    tm = 2 * TM
    n_steps = S // tm
    ag = _AllGatherInSteps(later, forward_step=n_steps // 2)

    def body(xn_ref, wt_ref, *rest):
        later_refs, rest = rest[:ag.n], rest[ag.n:]
        pa_ref, pc_ref = rest[:2]
        gathered, ag_scratch = rest[2:2 + ag.n], rest[2 + ag.n:]
        step = pl.program_id(0)
        finish = ag.emit(step, n_steps, later_refs, gathered, ag_scratch)
        xn = xn_ref[...]
        pa_ref[:, 0:512] = _nt(xn, wt_ref[0:512, :]).astype(ACT)
        pa_ref[:, 512:1024] = _nt(xn, wt_ref[768:1280, :]).astype(ACT)
        pa_ref[:, 1024:1280] = _nt(xn, wt_ref[512:768, :]).astype(ACT)
        pc_ref[...] = _nt(xn, wt_ref[1280:3328, :]).astype(ACT)
        finish()

    any_spec = pl.BlockSpec(memory_space=pl.ANY)
    outs = pl.pallas_call(
        body,
        name="fwd_proj_all_gather",
        grid=(n_steps,),
        in_specs=[
            pl.BlockSpec((tm, D_MODEL), lambda i: (i, 0)),
            pl.BlockSpec((IN_W, D_MODEL), lambda i: (0, 0)),
        ] + [any_spec] * ag.n,
        out_specs=[
            pl.BlockSpec((tm, PA_W), lambda i: (i, 0)),
            pl.BlockSpec((tm, PC_W), lambda i: (i, 0)),
        ] + [any_spec] * ag.n,
        out_shape=[
            jax.ShapeDtypeStruct((S, PA_W), ACT),
            jax.ShapeDtypeStruct((S, PC_W), ACT),
        ] + ag.out_shape(),
        scratch_shapes=ag.scratch_shapes(),
        compiler_params=_params(("arbitrary",)),
    )(xn, wt, *later)
    return outs[0], outs[1], outs[2:]


def _rope(t, tab):
    return (t * tab[:, 0:128] + pltpu.roll(t, 120, 1) * tab[:, 128:256]
            + pltpu.roll(t, 8, 1) * tab[:, 256:384])


def _rope_t(d, tab):
    return (d * tab[:, 0:128] + pltpu.roll(d * tab[:, 128:256], 8, 1)
            + pltpu.roll(d * tab[:, 256:384], 120, 1))


def _fill_kv(kall, kvc_ref, kvp_ref, tabc_ref, tabp_ref):
    for lo, kv_ref, tab_ref, n in ((0, kvp_ref, tabp_ref, BLK), (BLK, kvc_ref, tabc_ref, TQ)):
        k = _rope(kv_ref[:, 0:128].astype(F32), tab_ref[...])
        v = kv_ref[:, 128:256].astype(F32)
        kall[0, lo:lo + n, :] = k.astype(BF16)
        kall[1, lo:lo + n, :] = pltpu.roll(k, 64, 1).astype(BF16)
        kall[2, lo:lo + n, :] = v.astype(BF16)
        kall[3, lo:lo + n, :] = pltpu.roll(v, 64, 1).astype(BF16)


HEADS = (((0, 0), (1, 0), (2, 1), (3, 1)), ((0, 1), (1, 1), (2, 0), (3, 0)))


def _upper():
    kj = lax.broadcasted_iota(jnp.int32, (BLK, 4 * BLK), 0)
    qi = lax.broadcasted_iota(jnp.int32, (BLK, 4 * BLK), 1) & (BLK - 1)
    return kj > qi


def _merge(upper, both):
    return jnp.where(upper, both[0:BLK, :], both[BLK:2 * BLK, :])


def _split_store(ref, s, upper_b, vb):
    first = vb * upper_b
    ref[s, 0:BLK, :] = first
    ref[s, BLK:2 * BLK, :] = vb - first


def _sink_rows(sink_ref):
    return [jnp.concatenate([jnp.full((1, BLK), sink_ref[2 * p + e], F32) for p, e in HEADS[s]], axis=1)
            for s in range(2)]


def _stack_heads(ref, slot, half, pairs, s=None):
    for a, (p, e) in enumerate(HEADS[slot if s is None else s]):
        ref[slot, a * BLK:(a + 1) * BLK, :] = jnp.where(half[e], pairs[p], 0.0).astype(BF16)


def _unstack_pair(half, outs, p):
    lo = 0 if p < 2 else 1
    rows = slice(p * BLK, (p + 1) * BLK)
    return jnp.where(half[0], outs[lo][rows, :], outs[1 - lo][rows, :])


def _softmax(sm, sinks):
    m = jnp.maximum(jnp.max(sm, axis=0, keepdims=True), sinks)
    p = jnp.exp(sm - m)
    es = jnp.exp(sinks - m)
    inv = 1.0 / (jnp.sum(p, axis=0, keepdims=True) + es)
    return p * inv, es * inv


def _scores(kk, q_stack, first):
    st = _nt(kk, q_stack)
    prev = st[0:BLK, :]
    if first is not None:
        prev = prev + jnp.where(first, -jnp.inf, 0.0)
    return prev, st[BLK:2 * BLK, :]


def _attn_specs(tile):
    nb = TQ // BLK
    prev = lambda i: jnp.maximum(tile(i) * nb - 1, 0)
    return [
        pl.BlockSpec(memory_space=pltpu.SMEM),
        pl.BlockSpec((TQ, ATTN_W), lambda i: (tile(i), 0)),
        pl.BlockSpec((TQ, ATTN_W), lambda i: (tile(i), 1)),
        pl.BlockSpec((TQ, 2 * KV_W), lambda i: (tile(i), 4)),
        pl.BlockSpec((BLK, 2 * KV_W), lambda i: (prev(i), 4)),
        pl.BlockSpec((TQ, 384), lambda i: (tile(i), 0)),
        pl.BlockSpec((BLK, 384), lambda i: (prev(i), 0)),
    ]


def _attn_fwd(pa, tab, sinks):
    S = pa.shape[0]
    nb = TQ // BLK

    def body(sink_ref, q_ref, g_ref, kvc_ref, kvp_ref, tabc_ref, tabp_ref, o_ref, att_ref, pm_ref, ps_ref,
             qr_ref, kall, q_sc, p_sc):
        i = pl.program_id(0)
        _fill_kv(kall, kvc_ref, kvp_ref, tabc_ref, tabp_ref)
        lane = lax.broadcasted_iota(jnp.int32, (BLK, 128), 1)
        half = [lane < HEAD_DIM, lane >= HEAD_DIM]
        upper = _upper()
        upper_b = upper.astype(BF16)
        sinks = _sink_rows(sink_ref)
        for j in range(nb):
            rq = slice(j * BLK, (j + 1) * BLK)
            rk = slice(j * BLK, (j + 2) * BLK)
            tab = tabc_ref[rq, :]
            qr = [_rope(q_ref[rq, p * 128:(p + 1) * 128].astype(F32), tab) * 0.125 for p in range(4)]
            for p in range(4):
                qr_ref[rq, p * 128:(p + 1) * 128] = qr[p].astype(BF16)
            outs = []
            for s in range(2):
                _stack_heads(q_sc, s, half, qr)
                prev, cur = _scores(kall[s, rk, :], q_sc[s], i == 0 if j == 0 else None)
                prob, psink = _softmax(jnp.where(upper, prev, cur), sinks[s])
                pb = prob.astype(BF16)
                pm_ref[(2 * j + s) * BLK:(2 * j + s + 1) * BLK, :] = pb
                ps_ref[2 * j + s:2 * j + s + 1, :] = psink
                _split_store(p_sc, s, upper_b, pb)
                outs.append(_tn(p_sc[s], kall[2 + s, rk, :]))
            for p in range(4):
                cols = slice(p * 128, (p + 1) * 128)
                att = _unstack_pair(half, outs, p)
                att_ref[rq, cols] = att.astype(BF16)
                o_ref[rq, cols] = (att * _silu(g_ref[rq, cols].astype(F32))).astype(BF16)

    return pl.pallas_call(
        body,
        name="attn_fwd",
        grid=(S // TQ,),
        in_specs=_attn_specs(lambda i: i),
        out_specs=[pl.BlockSpec((TQ, ATTN_W), lambda i: (i, 0))] * 2 + [
            pl.BlockSpec((2 * TQ, 4 * BLK), lambda i: (i, 0)),
            pl.BlockSpec((2 * nb, 4 * BLK), lambda i: (i, 0)),
            pl.BlockSpec((TQ, ATTN_W), lambda i: (i, 0)),
        ],
        out_shape=[jax.ShapeDtypeStruct((S, ATTN_W), BF16)] * 2 + [
            jax.ShapeDtypeStruct((2 * S, 4 * BLK), BF16),
            jax.ShapeDtypeStruct((2 * S // BLK, 4 * BLK), F32),
            jax.ShapeDtypeStruct((S, ATTN_W), BF16),
        ],
        scratch_shapes=[
            pltpu.VMEM((4, BLK + TQ, 128), BF16),
            pltpu.VMEM((2, 4 * BLK, 128), BF16),
            pltpu.VMEM((2, 2 * BLK, 4 * BLK), BF16),
        ],
        compiler_params=_params(("arbitrary",)),
    )(sinks, pa, pa, pa, pa, tab, tab)


def _shift_down(u, halo_ref, has_prev):
    def halo_u(r):
        hu = halo_ref[r:r + 1, 512:1024].astype(F32) * halo_ref[r:r + 1, 1024:1536].astype(F32)
        return jnp.where(has_prev, hu, 0.0)

    row = lax.broadcasted_iota(jnp.int32, u.shape, 0)
    um1 = jnp.where(row == 0, halo_u(HALO - 1), pltpu.roll(u, 1, 0))
    um2 = jnp.where(row == 0, halo_u(HALO - 2), jnp.where(row == 1, halo_u(HALO - 1), pltpu.roll(u, 2, 0)))
    return um1, um2


def _conv_tile(pc_ref, halo_ref, w_ref, has_prev):
    b = pc_ref[:, 0:512].astype(F32)
    c = pc_ref[:, 512:1024].astype(F32)
    hh = pc_ref[:, 1024:1536].astype(F32)
    gc = pc_ref[:, 1536:2048].astype(F32)
    u = c * hh
    um1, um2 = _shift_down(u, halo_ref, has_prev)
    cv = w_ref[0:1, :] * um2 + w_ref[1:2, :] * um1 + w_ref[2:3, :] * u
    return b, c, hh, gc, u, um1, um2, cv


def _prev_rows(width, col=0):
    return pl.BlockSpec((HALO, width), lambda i: (jnp.maximum(i * (TM // HALO) - 1, 0), col))


def _out_loss(x, target, ya, pc, conv_w, w_out, final_g):
    S = x.shape[0]

    def body(x_ref, t_ref, ya_ref, pc_ref, halo_ref, cw_ref, wo_ref, fg_ref,
             dh_ref, dmix_ref, gwo_ref, gfg_ref, loss_ref):
        @pl.when(pl.program_id(0) == 0)
        def _():
            gwo_ref[...] = jnp.zeros_like(gwo_ref)
            gfg_ref[...] = jnp.zeros_like(gfg_ref)
            loss_ref[...] = jnp.zeros_like(loss_ref)

        b, _, _, gc, _, _, _, cv = _conv_tile(pc_ref, halo_ref, cw_ref, pl.program_id(0) > 0)
        yc = (b * cv * _silu(gc)).astype(BF16)
        mix = jnp.concatenate([ya_ref[...], yc], axis=1)
        wo = wo_ref[...]
        fg = fg_ref[...]
        h = x_ref[...] + _nn(mix, wo)
        r = lax.rsqrt(jnp.mean(h * h, axis=-1, keepdims=True) + EPS)
        n = h * r
        err = n * fg - t_ref[...]
        loss_ref[...] += jnp.broadcast_to(
            0.5 * jnp.sum(jnp.mean(err * err, axis=-1, keepdims=True), axis=0, keepdims=True), (8, 128))
        gfg_ref[...] += jnp.sum(err * n, axis=0, keepdims=True) * (1.0 / D_MODEL)
        dyg = err * (fg * (1.0 / D_MODEL))
        dh = r * (dyg - n * jnp.mean(dyg * n, axis=-1, keepdims=True))
        dh_ref[...] = dh
        dhb = dh.astype(BF16)
        dmix_ref[...] = _nt(dhb, wo).astype(ACT)
        gwo_ref[...] += _tn(mix, dhb)

    row = lambda i: (i, 0)
    fixed = lambda i: (0, 0)
    return pl.pallas_call(
        body,
        name="out_loss",
        grid=(S // TM,),
        in_specs=[
            pl.BlockSpec((TM, D_MODEL), row),
            pl.BlockSpec((TM, D_MODEL), row),
            pl.BlockSpec((TM, ATTN_W), row),
            pl.BlockSpec((TM, PC_W), row),
            _prev_rows(PC_W),
            pl.BlockSpec((CONV_K, CONV_W), fixed),
            pl.BlockSpec((D_MODEL, D_MODEL), fixed),
            pl.BlockSpec((1, D_MODEL), fixed),
        ],
        out_specs=[
            pl.BlockSpec((TM, D_MODEL), row),
            pl.BlockSpec((TM, D_MODEL), row),
            pl.BlockSpec((D_MODEL, D_MODEL), fixed),
            pl.BlockSpec((1, D_MODEL), fixed),
            pl.BlockSpec((8, 128), fixed),
        ],
        out_shape=[
            jax.ShapeDtypeStruct((S, D_MODEL), F32),
            jax.ShapeDtypeStruct((S, D_MODEL), ACT),
            jax.ShapeDtypeStruct((D_MODEL, D_MODEL), F32),
            jax.ShapeDtypeStruct((1, D_MODEL), F32),
            jax.ShapeDtypeStruct((8, 128), F32),
        ],
        compiler_params=_params(("arbitrary",)),
    )(x, target, ya, pc, pc, conv_w, w_out, final_g)


def _attn_bwd(pa, dmix, att, probs, psinks, q_roped, tab):
    S = pa.shape[0]
    nt = S // TQ
    nb = TQ // BLK

    def body(g_ref, kvc_ref, kvp_ref, tabc_ref, tabp_ref, dm_ref, att_ref, pm_ref, ps_ref, qr_ref,
             d_ref, dsink_ref, kall, dkv, carry, q_sc, do_sc, p_sc, ds_sc, dsink_acc):
        step = pl.program_id(0)

        @pl.when(step == 0)
        def _():
            carry[...] = jnp.zeros_like(carry)
            dsink_acc[...] = jnp.zeros_like(dsink_acc)

        _fill_kv(kall, kvc_ref, kvp_ref, tabc_ref, tabp_ref)
        dkv[0:TQ, :] = jnp.zeros((TQ, 2 * KV_W), F32)
        dkv[TQ:TQ + BLK, :] = carry[...]
        lane = lax.broadcasted_iota(jnp.int32, (BLK, 128), 1)
        half = [lane < HEAD_DIM, lane >= HEAD_DIM]
        half_b = [h.astype(BF16) for h in half]
        upper = _upper()
        upper_b = upper.astype(BF16)
        for j in range(nb):
            rq = slice(j * BLK, (j + 1) * BLK)
            rk = slice(j * BLK, (j + 2) * BLK)
            tab = tabc_ref[rq, :]
            pair = [slice(p * 128, (p + 1) * 128) for p in range(4)]
            g = [g_ref[rq, c].astype(F32) for c in pair]
            da = [dm_ref[rq, c].astype(F32) for c in pair]
            gate = [_silu_and_grad(g[p]) for p in range(4)]
            do = [da[p] * gate[p][0] for p in range(4)]
            dqs, dks, dvs = [], [], []
            for s in range(2):
                kk = kall[s, rk, :]
                vv = kall[2 + s, rk, :]
                _stack_heads(do_sc, s, half, do)
                for a, (p, e) in enumerate(HEADS[s]):
                    q_sc[s, a * BLK:(a + 1) * BLK, :] = qr_ref[rq, pair[p]] * half_b[e]
                pb = pm_ref[(2 * j + s) * BLK:(2 * j + s + 1) * BLK, :]
                prob = pb.astype(F32)
                _split_store(p_sc, s, upper_b, pb)
                dprob = _merge(upper, _nt(vv, do_sc[s]))
                dsum = jnp.sum(dprob * prob, axis=0, keepdims=True)
                _split_store(ds_sc, s, upper_b, (prob * (dprob - dsum)).astype(BF16))
                dsink_acc[s, 0:1, :] += ps_ref[2 * j + s:2 * j + s + 1, :] * dsum
                dqs.append(_tn(ds_sc[s], kk))
                dks.append(_nn(ds_sc[s], q_sc[s]))
                dvs.append(_nn(p_sc[s], do_sc[s]))
            for p in range(4):
                d_ref[rq, pair[p]] = _rope_t(_unstack_pair(half, dqs, p) * 0.125, tab).astype(BF16)
                d_ref[rq, 512 + p * 128:512 + (p + 1) * 128] = (
                    da[p] * att_ref[rq, pair[p]].astype(F32) * gate[p][1]).astype(BF16)
            dkv[rk, 0:128] += dks[0] + pltpu.roll(dks[1], 64, 1)
            dkv[rk, 128:256] += dvs[0] + pltpu.roll(dvs[1], 64, 1)
        d_ref[:, 1024:1152] = _rope_t(dkv[BLK:BLK + TQ, 0:128], tabc_ref[...]).astype(BF16)
        d_ref[:, 1152:1280] = dkv[BLK:BLK + TQ, 128:256].astype(BF16)
        carry[...] = dkv[0:BLK, :]

        @pl.when(step == nt - 1)
        def _():
            lanes = lax.broadcasted_iota(jnp.int32, (8, 128), 1)
            out = jnp.zeros((8, 128), F32)
            for s in range(2):
                for a, (p, e) in enumerate(HEADS[s]):
                    tot = jnp.sum(dsink_acc[s, 0:1, a * BLK:(a + 1) * BLK], axis=1, keepdims=True)
                    out = jnp.where(lanes == 2 * p + e, -tot, out)
            dsink_ref[...] = out

    rev = lambda s: nt - 1 - s
    return pl.pallas_call(
        body,
        name="attn_bwd",
        grid=(nt,),
        in_specs=_attn_specs(rev)[2:] + [pl.BlockSpec((TQ, ATTN_W), lambda s: (nt - 1 - s, 0))] * 2 + [
            pl.BlockSpec((2 * TQ, 4 * BLK), lambda s: (nt - 1 - s, 0)),
            pl.BlockSpec((2 * nb, 4 * BLK), lambda s: (nt - 1 - s, 0)),
            pl.BlockSpec((TQ, ATTN_W), lambda s: (nt - 1 - s, 0)),
        ],
        out_specs=[
            pl.BlockSpec((TQ, PA_W), lambda s: (nt - 1 - s, 0)),
            pl.BlockSpec((8, 128), lambda s: (0, 0)),
        ],
        out_shape=[
            jax.ShapeDtypeStruct((S, PA_W), BF16),
            jax.ShapeDtypeStruct((8, 128), F32),
        ],
        scratch_shapes=[
            pltpu.VMEM((4, BLK + TQ, 128), BF16),
            pltpu.VMEM((BLK + TQ, 2 * KV_W), F32),
            pltpu.VMEM((BLK, 2 * KV_W), F32),
            pltpu.VMEM((2, 4 * BLK, 128), BF16),
            pltpu.VMEM((2, 4 * BLK, 128), BF16),
            pltpu.VMEM((2, 2 * BLK, 4 * BLK), BF16),
            pltpu.VMEM((2, 2 * BLK, 4 * BLK), BF16),
            pltpu.VMEM((2, 8, 4 * BLK), F32),
        ],
        compiler_params=_params(("arbitrary",)),
    )(pa, pa, pa, tab, tab, dmix, att, probs, psinks, q_roped)


def _conv_bwd_tile(pc_ref, prev_ref, next_ref, dm_ref, dmn_ref, w_ref, d_ref, gw_ref, has_prev, has_next,
                   on_piece):
    rows = pc_ref.shape[0]
    w0, w1, w2 = w_ref[0:1, :], w_ref[1:2, :], w_ref[2:3, :]
    b, c, hh, gc, u, um1, um2, cv = _conv_tile(pc_ref, prev_ref, w_ref, has_prev)
    sg, dsg = _silu_and_grad(gc)
    dy = dm_ref[...].astype(F32)
    dyb = dy * b
    dcv = dyb * sg

    def next_dcv(r):
        nd = (dmn_ref[r:r + 1, :].astype(F32) * next_ref[r:r + 1, 0:512].astype(F32)
              * _silu(next_ref[r:r + 1, 1536:2048].astype(F32)))
        return jnp.where(has_next, nd, 0.0)

    row = lax.broadcasted_iota(jnp.int32, (rows, CONV_W), 0)
    dp1 = jnp.where(row == rows - 1, next_dcv(0), pltpu.roll(dcv, rows - 1, 0))
    dp2 = jnp.where(row == rows - 1, next_dcv(1),
                    jnp.where(row == rows - 2, next_dcv(0), pltpu.roll(dcv, rows - 2, 0)))
    du = w2 * dcv + w1 * dp1 + w0 * dp2
    pieces = (lambda: dy * cv * sg, lambda: du * hh, lambda: du * c, lambda: dyb * cv * dsg)
    for k, piece in enumerate(pieces):
        d_ref[:, k * CONV_W:(k + 1) * CONV_W] = piece().astype(BF16)
        on_piece(k)
    gw_ref[0:1, :] += jnp.sum(dcv * um2, axis=0, keepdims=True)
    gw_ref[1:2, :] += jnp.sum(dcv * um1, axis=0, keepdims=True)
    gw_ref[2:3, :] += jnp.sum(dcv * u, axis=0, keepdims=True)


def _grad_x(da, dc, wt, x, dh, norm_g, small, grads):
    S = x.shape[0]
    n_steps = S // TM
    rs = _ReduceScatter(grads)
    n_rs_out = len(rs.out_shape())
    small_rows = 8 + small.shape[0]

    def body(da_ref, dc_ref, wt_ref, x_ref, dh_ref, g_ref, small_ref, *rest):
        grad_refs, rest = rest[:rs.n], rest[rs.n:]
        gx_ref, all_ref = rest[:2]
        rs_out, rest = rest[2:2 + n_rs_out], rest[2 + n_rs_out:]
        gng, stage, small_send, small_recv, small_own = rest[:5]
        rs_scratch = rest[5:]
        step = pl.program_id(0)
        finish = rs.emit(step, n_steps, grad_refs, rs_out, rs_scratch)

        @pl.when(step == 0)
        def _():
            gng[...] = jnp.zeros_like(gng)

        dxn = (_nn(da_ref[:, 0:512], wt_ref[0:512, :]) + _nn(da_ref[:, 512:1024], wt_ref[768:1280, :])
               + _nn(da_ref[:, 1024:1280], wt_ref[512:768, :]) + _nn(dc_ref[...], wt_ref[1280:3328, :]))
        xv = x_ref[...]
        r = lax.rsqrt(jnp.mean(xv * xv, axis=-1, keepdims=True) + EPS)
        n = xv * r
        gng[...] += jnp.sum(dxn * n, axis=0, keepdims=True)
        dxg = dxn * g_ref[...]
        gx_ref[...] = dh_ref[...] + r * (dxg - n * jnp.mean(dxg * n, axis=-1, keepdims=True))

        @pl.when(step == n_steps - 1)
        def _():
            x_, y_, c_ = lax.axis_index("x"), lax.axis_index("y"), lax.axis_index("c")
            me = 4 * x_ + 2 * y_ + c_
            for q in range(8):
                stage[q:q + 1, :] = gng[:, q * 128:(q + 1) * 128]
            stage[8:small_rows, :] = small_ref[...]
            own = pltpu.make_async_copy(stage, all_ref.at[me], small_own)
            own.start()
            sends = []
            for k in range(1, N_DEV):
                cp = pltpu.make_async_remote_copy(
                    src_ref=stage, dst_ref=all_ref.at[me],
                    send_sem=small_send.at[k - 1], recv_sem=small_recv.at[k - 1],
                    device_id=(x_ ^ (k >> 2), y_ ^ ((k >> 1) & 1), c_ ^ (k & 1)), device_id_type=MESH)
                cp.start()
                sends.append(cp)
            for cp in sends:
                cp.wait_send()
                cp.wait_recv()
            own.wait()

        finish()

    row = lambda i: (i, 0)
    fixed = lambda i: (0, 0)
    any_spec = pl.BlockSpec(memory_space=pl.ANY)
    outs = pl.pallas_call(
        body,
        name="grad_x_reduce_scatter",
        grid=(n_steps,),
        in_specs=[
            pl.BlockSpec((TM, PA_W), row),
            pl.BlockSpec((TM, PC_W), row),
            pl.BlockSpec((IN_W, D_MODEL), fixed),
            pl.BlockSpec((TM, D_MODEL), row),
            pl.BlockSpec((TM, D_MODEL), row),
            pl.BlockSpec((1, D_MODEL), fixed),
            pl.BlockSpec(small.shape, fixed),
        ] + [any_spec] * rs.n,
        out_specs=[pl.BlockSpec((TM, D_MODEL), row), any_spec] + [any_spec] * n_rs_out,
        out_shape=[jax.ShapeDtypeStruct((S, D_MODEL), F32),
                   jax.ShapeDtypeStruct((N_DEV, small_rows, 128), F32)] + rs.out_shape(),
        scratch_shapes=[
            pltpu.VMEM((1, D_MODEL), F32),
            pltpu.VMEM((small_rows, 128), F32),
            pltpu.SemaphoreType.DMA((N_DEV - 1,)),
            pltpu.SemaphoreType.DMA((N_DEV - 1,)),
            pltpu.SemaphoreType.DMA,
        ] + rs.scratch_shapes(),
        compiler_params=_params(("arbitrary",)),
    )(da, dc, wt, x, dh, norm_g, small, *grads)
    return outs[0], outs[1], outs[2:2 + rs.n], outs[2 + rs.n:2 + 2 * rs.n]


def _grad_w_in(da, pc, dmix, conv_w, xn):
    S = xn.shape[0]
    tm = 2 * TM
    nt = S // tm
    t16 = tm // HALO

    def body(da_ref, pc_ref, prev_ref, next_ref, dm_ref, dmn_ref, cw_ref, xn_ref, gw_ref, dc_ref, gcw_ref):
        i = pl.program_id(0)

        @pl.when(i == 0)
        def _():
            gw_ref[...] = jnp.zeros_like(gw_ref)
            gcw_ref[...] = jnp.zeros_like(gcw_ref)

        xn = xn_ref[...]
        gw_ref[0:512, :] += _tn(da_ref[:, 0:512], xn)
        gw_ref[768:1280, :] += _tn(da_ref[:, 512:1024], xn)
        gw_ref[512:768, :] += _tn(da_ref[:, 1024:1280], xn)
        def piece_grad(k):
            rows = slice(PA_W + k * CONV_W, PA_W + (k + 1) * CONV_W)
            gw_ref[rows, :] += _tn(dc_ref[:, k * CONV_W:(k + 1) * CONV_W], xn)

        _conv_bwd_tile(pc_ref, prev_ref, next_ref, dm_ref, dmn_ref, cw_ref, dc_ref, gcw_ref, i > 0, i < nt - 1,
                       piece_grad)

    row = lambda i: (i, 0)
    fixed = lambda i: (0, 0)
    nxt = lambda i: jnp.minimum((i + 1) * t16, nt * t16 - 1)
    return pl.pallas_call(
        body,
        name="grad_w_in",
        grid=(nt,),
        in_specs=[
            pl.BlockSpec((tm, PA_W), row),
            pl.BlockSpec((tm, PC_W), row),
            pl.BlockSpec((HALO, PC_W), lambda i: (jnp.maximum(i * t16 - 1, 0), 0)),
            pl.BlockSpec((HALO, PC_W), lambda i: (nxt(i), 0)),
            pl.BlockSpec((tm, CONV_W), lambda i: (i, 1)),
            pl.BlockSpec((HALO, CONV_W), lambda i: (nxt(i), 1)),
            pl.BlockSpec((CONV_K, CONV_W), fixed),
            pl.BlockSpec((tm, D_MODEL), row),
        ],
        out_specs=[
            pl.BlockSpec((IN_W, D_MODEL), fixed, pipeline_mode=pl.Buffered(1)),
            pl.BlockSpec((tm, PC_W), row),
            pl.BlockSpec((CONV_K, CONV_W), fixed),
        ],
        out_shape=[
            jax.ShapeDtypeStruct((IN_W, D_MODEL), F32),
            jax.ShapeDtypeStruct((S, PC_W), BF16),
            jax.ShapeDtypeStruct((CONV_K, CONV_W), F32),
        ],
        compiler_params=_params(("arbitrary",)),
    )(da, pc, pc, pc, dmix, dmix, conv_w, xn)


def _adam_update(w, g, m, v):
    c1 = 1.0 - ADAM_B1 ** ADAM_STEP
    c2 = 1.0 - ADAM_B2 ** ADAM_STEP
    nm = ADAM_B1 * m + (1.0 - ADAM_B1) * g
    nv = ADAM_B2 * v + (1.0 - ADAM_B2) * (g * g)
    return -ADAM_LR * ((nm / c1) / (jnp.sqrt(nv / c2) + ADAM_EPS) + ADAM_WD * w), nm, nv


def _sum_chips_adamw(own, others, w, m, v, name):
    def body(own_ref, p_ref, w_ref, m_ref, v_ref, g_ref, d_ref, nm_ref, nv_ref):
        g = own_ref[...]
        for k in range(N_CHIP - 1):
            g = g + p_ref[k].astype(F32)
        g_ref[...] = g
        d_ref[...], nm_ref[...], nv_ref[...] = _adam_update(w_ref[...], g, m_ref[...], v_ref[...])

    rows, cols = w.shape
    half = rows // 2
    blk = pl.BlockSpec((half, cols), lambda i: (i, 0))
    shape = jax.ShapeDtypeStruct(w.shape, F32)
    return pl.pallas_call(
        body,
        name=name,
        grid=(2,),
        in_specs=[blk, pl.BlockSpec((N_CHIP - 1, half, cols), lambda i: (0, i, 0)), blk, blk, blk],
        out_specs=[blk] * 4,
        out_shape=[shape] * 4,
        compiler_params=_params(("arbitrary",)),
    )(own, others, w, m, v)


SMALL_ROWS = 96


def _small_adamw(parts, params):
    def body(parts_ref, *rest):
        prm, outs, total = rest[:12], rest[12:29], rest[29]
        me = 4 * lax.axis_index("x") + 2 * lax.axis_index("y") + lax.axis_index("c")
        acc = parts_ref[0]
        for d in range(1, N_DEV):
            acc = acc + parts_ref[d]
        total[...] = acc
        grads = (total[0:8, :], total[8:16, :], total[16:17, 0:8],
                 total[pl.ds(pl.multiple_of(32 + me * 8, 8), CONV_K), 0:64])
        outs[0][...] = total[24:25, 0:1]
        for k, g in enumerate(grads):
            w_ref, m_ref, v_ref = prm[3 * k:3 * k + 3]
            g_ref, d_ref, nm_ref, nv_ref = outs[1 + 4 * k:5 + 4 * k]
            g_ref[...] = g
            d_ref[...], nm_ref[...], nv_ref[...] = _adam_update(w_ref[...], g, m_ref[...], v_ref[...])

    flat = [a for p in params for a in p]
    out_shape = [jax.ShapeDtypeStruct((1, 1), F32)]
    for p in params:
        out_shape += [jax.ShapeDtypeStruct(p[0].shape, F32)] * 4
    return pl.pallas_call(
        body,
        name="adamw_small",
        out_shape=out_shape,
        scratch_shapes=[pltpu.VMEM((SMALL_ROWS, 128), F32)],
        compiler_params=_params(),
    )(parts, *flat)


def kernel(x, norm_g, w_in, sinks, conv_w, w_out, final_g, loss_target, m_norm_g, m_w_in, m_sinks, m_conv_w, m_w_out, m_final_g, v_norm_g, v_w_in, v_sinks, v_conv_w, v_w_out, v_final_g):
    S = x.shape[1]
    x2 = x.reshape(S, D_MODEL)
    t2 = loss_target.reshape(S, D_MODEL)
    ng = norm_g.reshape(1, D_MODEL)
    fg = final_g.reshape(1, D_MODEL)

    cw_pad = jnp.zeros((8, 128), F32).at[0:CONV_K, 0:64].set(conv_w)
    xn, tab, wt = _prologue(x2, ng, w_in.T.astype(BF16))
    pa, pc, (wo, cw_all) = _fwd_proj(xn, wt, [w_out.astype(BF16), cw_pad])
    cw = cw_all.reshape(N_DEV, 8, 128)[:, 0:CONV_K, 0:64].transpose(1, 0, 2).reshape(CONV_K, CONV_W)
    ya, att, probs, psinks, q_roped = _attn_fwd(pa, tab, sinks)
    dh, dmix, g_wo, g_fg, loss_part = _out_loss(x2, t2, ya, pc, cw, wo, fg)
    da, g_sinks = _attn_bwd(pa, dmix, att, probs, psinks, q_roped, tab)
    g_wt, dc, g_cw = _grad_w_in(da, pc, dmix, cw, xn)
    cw_pack = jnp.pad(g_cw.reshape(CONV_K, N_DEV, 64).transpose(1, 0, 2),
                      ((0, 0), (0, 8 - CONV_K), (0, 64))).reshape(N_DEV * 8, 128)
    small = jnp.concatenate([g_fg.reshape(8, 128), g_sinks, loss_part, cw_pack], axis=0)
    grad_x, parts, own, others = _grad_x(
        da, dc, wt, x2, dh, ng, small,
        [g_wt.reshape(N_DEV, SHARD_IN, D_MODEL), g_wo.reshape(N_DEV, SHARD_OUT, D_MODEL)])
    gt, dt, nmt, nvt = _sum_chips_adamw(own[0], others[0], w_in.T, m_w_in.T, v_w_in.T, "adamw_w_in")
    grad_w_in, d_w_in, nm_w_in, nv_w_in = gt.T, dt.T, nmt.T, nvt.T
    grad_w_out, d_w_out, nm_w_out, nv_w_out = _sum_chips_adamw(
        own[1], others[1], w_out, m_w_out, v_w_out, "adamw_w_out")
    vec = lambda a: a.reshape(8, 128)
    row = lambda a: a.reshape(1, 8)
    res = _small_adamw(parts, [
        (vec(norm_g), vec(m_norm_g), vec(v_norm_g)), (vec(final_g), vec(m_final_g), vec(v_final_g)),
        (row(sinks), row(m_sinks), row(v_sinks)), (conv_w, m_conv_w, v_conv_w)])
    loss = res[0].reshape(())
    grad_norm_g, d_ng, nm_ng, nv_ng = [a.reshape(D_MODEL) for a in res[1:5]]
    grad_final_g, d_fg, nm_fg, nv_fg = [a.reshape(D_MODEL) for a in res[5:9]]
    grad_sinks, d_sk, nm_sk, nv_sk = [a.reshape(N_Q_HEADS) for a in res[9:13]]
    grad_conv_w, d_cw, nm_cw, nv_cw = res[13:17]

    return (loss, grad_x.reshape(1, S, D_MODEL), grad_norm_g, grad_w_in, grad_sinks, grad_conv_w, grad_w_out, grad_final_g,
            d_ng, d_w_in, d_sk, d_cw, d_w_out, d_fg,
            nm_ng, nm_w_in, nm_sk, nm_cw, nm_w_out, nm_fg,
            nv_ng, nv_w_in, nv_sk, nv_cw, nv_w_out, nv_fg)
```

```python
import jax
import jax.numpy as jnp
from jax import lax
from jax.experimental import pallas as pl
from jax.experimental.pallas import tpu as pltpu

F32 = jnp.float32
BF16 = jnp.bfloat16
MESH = pl.DeviceIdType.MESH

D_MODEL = 1024
HEAD_DIM = 64
N_Q_HEADS = 8
ATTN_W = 512
KV_W = 128
BLK = 128
CONV_W = 512
CONV_K = 3
IN_W = 3328
PA_W = 1280
PC_W = 2048
EPS = 1e-5
ROPE_THETA = 500000.0
ROT_DIM = 16
N_DEV = 8
N_CHIP = 4
SHARD_IN = IN_W // N_DEV
SHARD_OUT = D_MODEL // N_DEV

ADAM_LR = 0.001
ADAM_B1 = 0.9
ADAM_B2 = 0.999
ADAM_EPS = 1e-08
ADAM_WD = 0.01
ADAM_STEP = 10

ACT = jnp.bfloat16

TM = 512
TQ = 1024
HALO = 16
VMEM_LIMIT = 56 * 1024 * 1024

NT_DIMS = (((1,), (1,)), ((), ()))
TN_DIMS = (((0,), (0,)), ((), ()))


def _params(sem=None):
    kw = dict(vmem_limit_bytes=VMEM_LIMIT)
    if sem is not None:
        kw["dimension_semantics"] = sem
    return pltpu.CompilerParams(**kw)


def _nt(a, b):
    return lax.dot_general(a, b, NT_DIMS, preferred_element_type=F32)


def _tn(a, b):
    return lax.dot_general(a, b, TN_DIMS, preferred_element_type=F32)


def _nn(a, b):
    return jnp.dot(a, b, preferred_element_type=F32)


def _silu(g):
    return g * jax.nn.sigmoid(g)


def _silu_and_grad(g):
    s = jax.nn.sigmoid(g)
    return g * s, s * (1.0 + g * (1.0 - s))


class _AllGatherInSteps:
    def __init__(self, arrs, forward_step):
        self.blocks = [(a.shape, a.dtype) for a in arrs]
        self.n = len(arrs)
        self.forward_step = forward_step

    def out_shape(self):
        return [jax.ShapeDtypeStruct((N_DEV * s[0], s[1]), d) for s, d in self.blocks]

    def scratch_shapes(self):
        return [pltpu.SemaphoreType.DMA((7 * self.n,)), pltpu.SemaphoreType.DMA((7 * self.n,)),
                pltpu.SemaphoreType.DMA((self.n,))]

    def emit(self, step, n_steps, x_refs, out_refs, scratch):
        assert n_steps > self.forward_step + 1
        send_sems, recv_sems, local_sems = scratch
        x, y, c = lax.axis_index("x"), lax.axis_index("y"), lax.axis_index("c")
        me, sibling = (x, y, c), (x, y, 1 - c)
        chips = [(1 - x, y), (x, 1 - y), (1 - x, 1 - y)]

        def rows(a, px, py, pc):
            m = self.blocks[a][0][0]
            return out_refs[a].at[pl.ds((4 * px + 2 * py + pc) * m, m), :]

        def copy(a, k, block, to, src=None):
            return pltpu.make_async_remote_copy(
                src_ref=rows(a, *block) if src is None else src, dst_ref=rows(a, *block),
                send_sem=send_sems.at[a * 7 + k], recv_sem=recv_sems.at[a * 7 + k],
                device_id=to, device_id_type=MESH)

        def mine(a):
            return pltpu.make_async_copy(x_refs[a], rows(a, *me), local_sems.at[a])

        def first(a):
            return ([copy(a, 0, me, sibling, src=x_refs[a])]
                    + [copy(a, 1 + j, me, (*chip, c), src=x_refs[a]) for j, chip in enumerate(chips)])

        def passed(a):
            return [copy(a, 4 + j, (*chip, c), sibling) for j, chip in enumerate(chips)]

        @pl.when(step == 0)
        def _():
            for a in range(self.n):
                mine(a).start()
                for cp in first(a):
                    cp.start()

        @pl.when(step == self.forward_step)
        def _():
            for j, chip in enumerate(chips):
                for a in range(self.n):
                    copy(a, 1 + j, (*chip, c), me).wait_recv()
                    copy(a, 4 + j, (*chip, c), sibling).start()

        def finish():
            @pl.when(step == n_steps - 1)
            def _():
                for a in range(self.n):
                    copy(a, 0, sibling, me).wait_recv()
                    for j, chip in enumerate(chips):
                        copy(a, 4 + j, (*chip, 1 - c), me).wait_recv()
                    for cp in first(a) + passed(a):
                        cp.wait_send()
                    mine(a).wait()

        return finish


class _AllGatherViaNeighbours:
    def __init__(self, arr, first, mid, second):
        (self.m, self.ncol), self.dtype = arr.shape, arr.dtype
        assert self.m % 32 == 0
        self.first, self.mid, self.second = first, mid, second

    def out_shape(self):
        return [jax.ShapeDtypeStruct((N_DEV * self.m, self.ncol), self.dtype)]

    def scratch_shapes(self):
        return [pltpu.SemaphoreType.DMA((11,)), pltpu.SemaphoreType.DMA((11,)), pltpu.SemaphoreType.DMA]

    def emit(self, step, n_steps, x_ref, out_ref, scratch):
        assert 0 < self.first < self.mid < self.second < n_steps - 1
        send_sems, recv_sems, local_sem = scratch
        x, y, c = lax.axis_index("x"), lax.axis_index("y"), lax.axis_index("c")
        half = self.m // 2
        sibling, xn, yn = (x, y, 1 - c), (1 - x, y, c), (x, 1 - y, c)

        def rows(dev, part=None):
            px, py, pc = dev
            base = (4 * px + 2 * py + pc) * self.m
            if part is None:
                return out_ref.at[pl.ds(base, self.m), :]
            return out_ref.at[pl.ds(base + part * half, half), :]

        def copy(k, dev, to, part=None, src=None):
            return pltpu.make_async_remote_copy(
                src_ref=rows(dev, part) if src is None else src, dst_ref=rows(dev, part),
                send_sem=send_sems.at[k], recv_sem=recv_sems.at[k], device_id=to, device_id_type=MESH)

        me, dg = (x, y, c), (1 - x, 1 - y, c)
        mine = pltpu.make_async_copy(x_ref, rows(me), local_sem)
        my_half = lambda part: x_ref.at[pl.ds(part * half, half), :]
        sends = [
            copy(0, me, sibling, src=x_ref), copy(1, me, xn, part=0, src=my_half(0)),
            copy(2, me, yn, part=1, src=my_half(1)), copy(3, xn, yn, part=0), copy(4, yn, xn, part=1),
            copy(5, xn, sibling), copy(6, yn, sibling), copy(7, dg, sibling, part=0), copy(8, dg, sibling, part=1),
            copy(9, me, xn, part=1, src=my_half(1)), copy(10, me, yn, part=0, src=my_half(0)),
        ]
        other = lambda dev: (dev[0], dev[1], 1 - c)
        arrivals = [
            copy(0, other(me), sibling), copy(1, xn, xn, part=0), copy(2, yn, yn, part=1), copy(3, dg, yn, part=0),
            copy(4, dg, xn, part=1), copy(5, other(xn), sibling), copy(6, other(yn), sibling),
            copy(7, other(dg), sibling, part=0), copy(8, other(dg), sibling, part=1),
            copy(9, xn, xn, part=1), copy(10, yn, yn, part=0),
        ]

        @pl.when(step == 0)
        def _():
            mine.start()
            for k in (0, 1, 2, 9, 10):
                sends[k].start()

        @pl.when(step == self.first)
        def _():
            arrivals[1].wait_recv()
            sends[3].start()
            arrivals[2].wait_recv()
            sends[4].start()

        @pl.when(step == self.mid)
        def _():
            arrivals[9].wait_recv()
            sends[5].start()
            arrivals[10].wait_recv()
            sends[6].start()

        @pl.when(step == self.second)
        def _():
            arrivals[3].wait_recv()
            sends[7].start()
            arrivals[4].wait_recv()
            sends[8].start()

        def finish():
            @pl.when(step == n_steps - 1)
            def _():
                for k in (0, 5, 6, 7, 8):
                    arrivals[k].wait_recv()
                for cp in sends:
                    cp.wait_send()
                mine.wait()

        return finish


class _ReduceScatter:
    def __init__(self, grads):
        self.shapes = [g.shape[1:] for g in grads]
        self.n = len(grads)
        self.items = tuple((a, r) for r in (1, 2, 3, 0) for a in range(self.n))
        self.steps = N_CHIP + 2

    def out_shape(self):
        own = [jax.ShapeDtypeStruct(s, F32) for s in self.shapes]
        ici = [jax.ShapeDtypeStruct((N_CHIP - 1,) + s, BF16) for s in self.shapes]
        land = [jax.ShapeDtypeStruct((N_CHIP,) + s, F32) for s in self.shapes]
        return own + ici + land

    def scratch_shapes(self):
        n_items = len(self.items)
        return ([pltpu.VMEM((2,) + s, F32) for s in self.shapes]
                + [pltpu.VMEM((N_CHIP - 1,) + s, BF16) for s in self.shapes]
                + [pltpu.VMEM(s, F32) for s in self.shapes]
                + [pltpu.SemaphoreType.DMA((self.n * N_CHIP,))] * 2
                + [pltpu.SemaphoreType.DMA((2 * n_items,))]
                + [pltpu.SemaphoreType.DMA((self.n * (N_CHIP - 1),))] * 2
                + [pltpu.SemaphoreType.DMA((self.n,))])

    def emit(self, step, n_steps, g_refs, out_refs, scratch):
        assert n_steps > self.steps
        n = self.n
        own_refs, ici_refs, land_refs = out_refs[:n], out_refs[n:2 * n], out_refs[2 * n:]
        stage, pair_bf, pair_own = scratch[:n], scratch[n:2 * n], scratch[2 * n:3 * n]
        sib_send, sib_recv, load_sems, ici_send, ici_recv, own_sems = scratch[3 * n:]
        x, y, c = lax.axis_index("x"), lax.axis_index("y"), lax.axis_index("c")

        def chip_of(r):
            return (x ^ (r >> 1), y ^ (r & 1))

        def block_of(r, core):
            cx, cy = chip_of(r)
            return 4 * cx + 2 * cy + core

        def to_sibling(a, r):
            return pltpu.make_async_remote_copy(
                src_ref=g_refs[a].at[block_of(r, 1 - c)], dst_ref=land_refs[a].at[r],
                send_sem=sib_send.at[a * N_CHIP + r], recv_sem=sib_recv.at[a * N_CHIP + r],
                device_id=(x, y, 1 - c), device_id_type=MESH)

        def loads(k):
            a, r = self.items[k]
            return (pltpu.make_async_copy(g_refs[a].at[block_of(r, c)], stage[a].at[0], load_sems.at[2 * k]),
                    pltpu.make_async_copy(land_refs[a].at[r], stage[a].at[1], load_sems.at[2 * k + 1]))

        def to_owner(k):
            a, r = self.items[k]
            if r == 0:
                return pltpu.make_async_copy(pair_own[a], own_refs[a], own_sems.at[a])
            return pltpu.make_async_remote_copy(
                src_ref=pair_bf[a].at[r - 1], dst_ref=ici_refs[a].at[r - 1],
                send_sem=ici_send.at[a * (N_CHIP - 1) + r - 1], recv_sem=ici_recv.at[a * (N_CHIP - 1) + r - 1],
                device_id=(*chip_of(r), c), device_id_type=MESH)

        @pl.when(step == 0)
        def _():
            for a, r in self.items:
                to_sibling(a, r).start()

        def fetch(k):
            a, r = self.items[k]
            to_sibling(a, r).wait_recv()
            for cp in loads(k):
                cp.start()

        def add_and_send(k):
            a, r = self.items[k]
            for cp in loads(k):
                cp.wait()
            total = stage[a][0] + stage[a][1]
            if r == 0:
                pair_own[a][...] = total
            else:
                pair_bf[a][r - 1] = total.astype(BF16)
            to_owner(k).start()

        for g in range(N_CHIP + 1):
            @pl.when(step == 1 + g)
            def _(g=g):
                if g > 0:
                    for k in range((g - 1) * n, g * n):
                        add_and_send(k)
                if g < N_CHIP:
                    for k in range(g * n, (g + 1) * n):
                        fetch(k)

        def finish():
            @pl.when(step == n_steps - 1)
            def _():
                for k, (a, r) in enumerate(self.items):
                    if r == 0:
                        to_owner(k).wait()
                    else:
                        to_owner(k).wait_send()
                        to_owner(k).wait_recv()
                for a, r in self.items:
                    to_sibling(a, r).wait_send()

        return finish


def _prologue(x, norm_g, w_shard):
    S = x.shape[0]
    n_steps = S // TM
    half = ROT_DIM // 2
    pos = jnp.arange(S, dtype=jnp.int32).astype(F32)
    inv_freq = ROPE_THETA ** (-jnp.arange(0, ROT_DIM, 2, dtype=F32) / ROT_DIM)
    ang = inv_freq[:, None] * pos[None, :]
    cs = jnp.concatenate([jnp.cos(ang), jnp.sin(ang)], axis=0)
    ag = _AllGatherViaNeighbours(jax.ShapeDtypeStruct(w_shard.shape, BF16),
                                 first=n_steps // 4, mid=n_steps // 2 + 2, second=n_steps - 2)

    def body(x_ref, g_ref, cs_ref, w_ref, xn_ref, tab_ref, wt_ref, w_bf, *ag_scratch):
        step = pl.program_id(0)

        @pl.when(step == 0)
        def _():
            w_bf[...] = w_ref[...].astype(BF16)

        finish = ag.emit(step, n_steps, w_bf, wt_ref, ag_scratch)
        xv = x_ref[...]
        r = lax.rsqrt(jnp.mean(xv * xv, axis=-1, keepdims=True) + EPS)
        xn_ref[...] = (xv * r * g_ref[...]).astype(BF16)

        xt = jnp.concatenate([cs_ref[...], jnp.zeros((128 - 2 * half, TM), F32)], axis=0).T
        lane = lax.broadcasted_iota(jnp.int32, (TM, 128), 1)
        rr = lane & (HEAD_DIM - 1)
        first = lane < HEAD_DIM

        def at(shift_first, shift_second):
            return jnp.where(first, pltpu.roll(xt, shift_first, 1) if shift_first else xt,
                             pltpu.roll(xt, shift_second, 1))

        cos_lo, cos_hi = at(0, HEAD_DIM), at(half, HEAD_DIM + half)
        sin_lo, sin_hi = at(128 - half, HEAD_DIM - half), at(0, HEAD_DIM)
        tab_ref[:, 0:128] = jnp.where(rr < half, cos_lo, jnp.where(rr < ROT_DIM, cos_hi, 1.0))
        tab_ref[:, 128:256] = jnp.where(rr < half, -sin_lo, 0.0)
        tab_ref[:, 256:384] = jnp.where((rr >= half) & (rr < ROT_DIM), sin_hi, 0.0)
        finish()

    any_spec = pl.BlockSpec(memory_space=pl.ANY)
    return pl.pallas_call(
        body,
        name="prologue_all_gather_w_in",
        grid=(n_steps,),
        in_specs=[
            pl.BlockSpec((TM, D_MODEL), lambda i: (i, 0)),
            pl.BlockSpec((1, D_MODEL), lambda i: (0, 0)),
            pl.BlockSpec((2 * half, TM), lambda i: (0, i)),
            pl.BlockSpec(w_shard.shape, lambda i: (0, 0)),
        ],
        out_specs=[
            pl.BlockSpec((TM, D_MODEL), lambda i: (i, 0)),
            pl.BlockSpec((TM, 384), lambda i: (i, 0)),
            any_spec,
        ],
        out_shape=[
            jax.ShapeDtypeStruct((S, D_MODEL), BF16),
            jax.ShapeDtypeStruct((S, 384), F32),
        ] + ag.out_shape(),
        scratch_shapes=[pltpu.VMEM(w_shard.shape, BF16)] + ag.scratch_shapes(),
        compiler_params=_params(("arbitrary",)),
    )(x, norm_g, cs, w_shard)


def _fwd_proj(xn, wt, later):
    S = xn.shape[0]
    tm = 2 * TM
    n_steps = S // tm
    ag = _AllGatherInSteps([jax.ShapeDtypeStruct(later[0].shape, BF16)] + list(later[1:]),
                           forward_step=n_steps // 2)

    def body(xn_ref, wt_ref, *rest):
        later_refs, rest = rest[:ag.n], rest[ag.n:]
        pa_ref, pc_ref = rest[:2]
        gathered, w_bf, ag_scratch = rest[2:2 + ag.n], rest[2 + ag.n], rest[3 + ag.n:]
        step = pl.program_id(0)

        @pl.when(step == 0)
        def _():
            w_bf[...] = later_refs[0][...].astype(BF16)

        finish = ag.emit(step, n_steps, (w_bf,) + tuple(later_refs[1:]), gathered, ag_scratch)
        xn = xn_ref[...]
        pa_ref[:, 0:512] = _nt(xn, wt_ref[0:512, :]).astype(ACT)
        pa_ref[:, 512:1024] = _nt(xn, wt_ref[768:1280, :]).astype(ACT)
        pa_ref[:, 1024:1280] = _nt(xn, wt_ref[512:768, :]).astype(ACT)
        pc_ref[...] = _nt(xn, wt_ref[1280:3328, :]).astype(ACT)
        finish()

    any_spec = pl.BlockSpec(memory_space=pl.ANY)
    outs = pl.pallas_call(
        body,
        name="fwd_proj_all_gather",
        grid=(n_steps,),
        in_specs=[
            pl.BlockSpec((tm, D_MODEL), lambda i: (i, 0)),
            pl.BlockSpec((IN_W, D_MODEL), lambda i: (0, 0)),
            pl.BlockSpec(later[0].shape, lambda i: (0, 0)),
        ] + [any_spec] * (ag.n - 1),
        out_specs=[
            pl.BlockSpec((tm, PA_W), lambda i: (i, 0)),
            pl.BlockSpec((tm, PC_W), lambda i: (i, 0)),
        ] + [any_spec] * ag.n,
        out_shape=[
            jax.ShapeDtypeStruct((S, PA_W), ACT),
            jax.ShapeDtypeStruct((S, PC_W), ACT),
        ] + ag.out_shape(),
        scratch_shapes=[pltpu.VMEM(later[0].shape, BF16)] + ag.scratch_shapes(),
        compiler_params=_params(("arbitrary",)),
    )(xn, wt, *later)
    return outs[0], outs[1], outs[2:]


def _rope(t, tab):
    return (t * tab[:, 0:128] + pltpu.roll(t, 120, 1) * tab[:, 128:256]
            + pltpu.roll(t, 8, 1) * tab[:, 256:384])


def _rope_t(d, tab):
    return (d * tab[:, 0:128] + pltpu.roll(d * tab[:, 128:256], 8, 1)
            + pltpu.roll(d * tab[:, 256:384], 120, 1))


def _fill_kv(kall, kvc_ref, kvp_ref, tabc_ref, tabp_ref):
    for lo, kv_ref, tab_ref, n in ((0, kvp_ref, tabp_ref, BLK), (BLK, kvc_ref, tabc_ref, TQ)):
        k = _rope(kv_ref[:, 0:128].astype(F32), tab_ref[...])
        v = kv_ref[:, 128:256].astype(F32)
        kall[0, lo:lo + n, :] = k.astype(BF16)
        kall[1, lo:lo + n, :] = pltpu.roll(k, 64, 1).astype(BF16)
        kall[2, lo:lo + n, :] = v.astype(BF16)
        kall[3, lo:lo + n, :] = pltpu.roll(v, 64, 1).astype(BF16)


HEADS = (((0, 0), (1, 0), (2, 1), (3, 1)), ((0, 1), (1, 1), (2, 0), (3, 0)))


def _upper():
    kj = lax.broadcasted_iota(jnp.int32, (BLK, 4 * BLK), 0)
    qi = lax.broadcasted_iota(jnp.int32, (BLK, 4 * BLK), 1) & (BLK - 1)
    return kj > qi


def _merge(upper, both):
    return jnp.where(upper, both[0:BLK, :], both[BLK:2 * BLK, :])


def _split_store(ref, s, upper_b, vb):
    first = vb * upper_b
    ref[s, 0:BLK, :] = first
    ref[s, BLK:2 * BLK, :] = vb - first


def _sink_rows(sink_ref):
    return [jnp.concatenate([jnp.full((1, BLK), sink_ref[2 * p + e], F32) for p, e in HEADS[s]], axis=1)
            for s in range(2)]


def _stack_heads(ref, slot, half, pairs, s=None):
    for a, (p, e) in enumerate(HEADS[slot if s is None else s]):
        ref[slot, a * BLK:(a + 1) * BLK, :] = jnp.where(half[e], pairs[p], 0.0).astype(BF16)


def _unstack_pair(half, outs, p):
    lo = 0 if p < 2 else 1
    rows = slice(p * BLK, (p + 1) * BLK)
    return jnp.where(half[0], outs[lo][rows, :], outs[1 - lo][rows, :])


def _softmax(sm, sinks):
    m = jnp.maximum(jnp.max(sm, axis=0, keepdims=True), sinks)
    p = jnp.exp(sm - m)
    es = jnp.exp(sinks - m)
    inv = 1.0 / (jnp.sum(p, axis=0, keepdims=True) + es)
    return p * inv, es * inv


def _scores(kk, q_stack, first):
    st = _nt(kk, q_stack)
    prev = st[0:BLK, :]
    if first is not None:
        prev = prev + jnp.where(first, -jnp.inf, 0.0)
    return prev, st[BLK:2 * BLK, :]


def _attn_specs(tile):
    nb = TQ // BLK
    prev = lambda i: jnp.maximum(tile(i) * nb - 1, 0)
    return [
        pl.BlockSpec(memory_space=pltpu.SMEM),
        pl.BlockSpec((TQ, ATTN_W), lambda i: (tile(i), 0)),
        pl.BlockSpec((TQ, ATTN_W), lambda i: (tile(i), 1)),
        pl.BlockSpec((TQ, 2 * KV_W), lambda i: (tile(i), 4)),
        pl.BlockSpec((BLK, 2 * KV_W), lambda i: (prev(i), 4)),
        pl.BlockSpec((TQ, 384), lambda i: (tile(i), 0)),
        pl.BlockSpec((BLK, 384), lambda i: (prev(i), 0)),
    ]


def _attn_fwd(pa, tab, sinks):
    S = pa.shape[0]
    nb = TQ // BLK

    def body(sink_ref, q_ref, g_ref, kvc_ref, kvp_ref, tabc_ref, tabp_ref, o_ref, att_ref, pm_ref, ps_ref,
             qs_ref, kall, p_sc):
        i = pl.program_id(0)
        _fill_kv(kall, kvc_ref, kvp_ref, tabc_ref, tabp_ref)
        lane = lax.broadcasted_iota(jnp.int32, (BLK, 128), 1)
        half = [lane < HEAD_DIM, lane >= HEAD_DIM]
        upper = _upper()
        upper_b = upper.astype(BF16)
        sinks = _sink_rows(sink_ref)
        for j in range(nb):
            rq = slice(j * BLK, (j + 1) * BLK)
            rk = slice(j * BLK, (j + 2) * BLK)
            tab = tabc_ref[rq, :]
            qr = [_rope(q_ref[rq, p * 128:(p + 1) * 128].astype(F32), tab) * 0.125 for p in range(4)]
            outs = []
            for s in range(2):
                _stack_heads(qs_ref, 2 * j + s, half, qr, s)
                prev, cur = _scores(kall[s, rk, :], qs_ref[2 * j + s], i == 0 if j == 0 else None)
                prob, psink = _softmax(jnp.where(upper, prev, cur), sinks[s])
                pb = prob.astype(BF16)
                pm_ref[(2 * j + s) * BLK:(2 * j + s + 1) * BLK, :] = pb
                ps_ref[2 * j + s:2 * j + s + 1, :] = psink
                _split_store(p_sc, s, upper_b, pb)
                outs.append(_tn(p_sc[s], kall[2 + s, rk, :]))
            for p in range(4):
                cols = slice(p * 128, (p + 1) * 128)
                att = _unstack_pair(half, outs, p)
                att_ref[rq, cols] = att.astype(BF16)
                o_ref[rq, cols] = (att * _silu(g_ref[rq, cols].astype(F32))).astype(BF16)

    return pl.pallas_call(
        body,
        name="attn_fwd",
        grid=(S // TQ,),
        in_specs=_attn_specs(lambda i: i),
        out_specs=[pl.BlockSpec((TQ, ATTN_W), lambda i: (i, 0))] * 2 + [
            pl.BlockSpec((2 * TQ, 4 * BLK), lambda i: (i, 0)),
            pl.BlockSpec((2 * nb, 4 * BLK), lambda i: (i, 0)),
            pl.BlockSpec((2 * nb, 4 * BLK, 128), lambda i: (i, 0, 0)),
        ],
        out_shape=[jax.ShapeDtypeStruct((S, ATTN_W), BF16)] * 2 + [
            jax.ShapeDtypeStruct((2 * S, 4 * BLK), BF16),
            jax.ShapeDtypeStruct((2 * S // BLK, 4 * BLK), F32),
            jax.ShapeDtypeStruct((2 * S // BLK, 4 * BLK, 128), BF16),
        ],
        scratch_shapes=[
            pltpu.VMEM((4, BLK + TQ, 128), BF16),
            pltpu.VMEM((2, 2 * BLK, 4 * BLK), BF16),
        ],
        compiler_params=_params(("arbitrary",)),
    )(sinks, pa, pa, pa, pa, tab, tab)


def _shift_down(u, halo_ref, has_prev):
    def halo_u(r):
        hu = halo_ref[r:r + 1, 512:1024].astype(F32) * halo_ref[r:r + 1, 1024:1536].astype(F32)
        return jnp.where(has_prev, hu, 0.0)

    row = lax.broadcasted_iota(jnp.int32, u.shape, 0)
    um1 = jnp.where(row == 0, halo_u(HALO - 1), pltpu.roll(u, 1, 0))
    um2 = jnp.where(row == 0, halo_u(HALO - 2), jnp.where(row == 1, halo_u(HALO - 1), pltpu.roll(u, 2, 0)))
    return um1, um2


def _conv_tile(pc_ref, halo_ref, w_ref, has_prev):
    b = pc_ref[:, 0:512].astype(F32)
    c = pc_ref[:, 512:1024].astype(F32)
    hh = pc_ref[:, 1024:1536].astype(F32)
    gc = pc_ref[:, 1536:2048].astype(F32)
    u = c * hh
    um1, um2 = _shift_down(u, halo_ref, has_prev)
    cv = w_ref[0:1, :] * um2 + w_ref[1:2, :] * um1 + w_ref[2:3, :] * u
    return b, c, hh, gc, u, um1, um2, cv


def _prev_rows(width, col=0):
    return pl.BlockSpec((HALO, width), lambda i: (jnp.maximum(i * (TM // HALO) - 1, 0), col))


def _out_loss(x, target, ya, pc, conv_w, w_out, final_g):
    S = x.shape[0]

    def body(x_ref, t_ref, ya_ref, pc_ref, halo_ref, cw_ref, wo_ref, fg_ref,
             dh_ref, dmix_ref, gwo_ref, gfg_ref, loss_ref):
        @pl.when(pl.program_id(0) == 0)
        def _():
            gwo_ref[...] = jnp.zeros_like(gwo_ref)
            gfg_ref[...] = jnp.zeros_like(gfg_ref)
            loss_ref[...] = jnp.zeros_like(loss_ref)

        b, _, _, gc, _, _, _, cv = _conv_tile(pc_ref, halo_ref, cw_ref, pl.program_id(0) > 0)
        yc = (b * cv * _silu(gc)).astype(BF16)
        mix = jnp.concatenate([ya_ref[...], yc], axis=1)
        wo = wo_ref[...]
        fg = fg_ref[...]
        h = x_ref[...] + _nn(mix, wo)
        r = lax.rsqrt(jnp.mean(h * h, axis=-1, keepdims=True) + EPS)
        n = h * r
        err = n * fg - t_ref[...]
        loss_ref[...] += jnp.broadcast_to(
            0.5 * jnp.sum(jnp.mean(err * err, axis=-1, keepdims=True), axis=0, keepdims=True), (8, 128))
        gfg_ref[...] += jnp.sum(err * n, axis=0, keepdims=True) * (1.0 / D_MODEL)
        dyg = err * (fg * (1.0 / D_MODEL))
        dh = r * (dyg - n * jnp.mean(dyg * n, axis=-1, keepdims=True))
        dh_ref[...] = dh
        dhb = dh.astype(BF16)
        dmix_ref[...] = _nt(dhb, wo).astype(ACT)
        gwo_ref[...] += _tn(mix, dhb)

    row = lambda i: (i, 0)
    fixed = lambda i: (0, 0)
    return pl.pallas_call(
        body,
        name="out_loss",
        grid=(S // TM,),
        in_specs=[
            pl.BlockSpec((TM, D_MODEL), row),
            pl.BlockSpec((TM, D_MODEL), row),
            pl.BlockSpec((TM, ATTN_W), row),
            pl.BlockSpec((TM, PC_W), row),
            _prev_rows(PC_W),
            pl.BlockSpec((CONV_K, CONV_W), fixed),
            pl.BlockSpec((D_MODEL, D_MODEL), fixed),
            pl.BlockSpec((1, D_MODEL), fixed),
        ],
        out_specs=[
            pl.BlockSpec((TM, D_MODEL), row),
            pl.BlockSpec((TM, D_MODEL), row),
            pl.BlockSpec((D_MODEL, D_MODEL), fixed),
            pl.BlockSpec((1, D_MODEL), fixed),
            pl.BlockSpec((8, 128), fixed),
        ],
        out_shape=[
            jax.ShapeDtypeStruct((S, D_MODEL), F32),
            jax.ShapeDtypeStruct((S, D_MODEL), ACT),
            jax.ShapeDtypeStruct((D_MODEL, D_MODEL), F32),
            jax.ShapeDtypeStruct((1, D_MODEL), F32),
            jax.ShapeDtypeStruct((8, 128), F32),
        ],
        compiler_params=_params(("arbitrary",)),
    )(x, target, ya, pc, pc, conv_w, w_out, final_g)


def _attn_bwd(pa, dmix, att, probs, psinks, q_stack, tab):
    S = pa.shape[0]
    nt = S // TQ
    nb = TQ // BLK

    def body(g_ref, kvc_ref, kvp_ref, tabc_ref, tabp_ref, dm_ref, att_ref, pm_ref, ps_ref, qs_ref,
             d_ref, dsink_ref, kall, dkv, carry, do_sc, p_sc, ds_sc, dsink_acc):
        step = pl.program_id(0)

        @pl.when(step == 0)
        def _():
            carry[...] = jnp.zeros_like(carry)
            dsink_acc[...] = jnp.zeros_like(dsink_acc)

        _fill_kv(kall, kvc_ref, kvp_ref, tabc_ref, tabp_ref)
        dkv[0:TQ, :] = jnp.zeros((TQ, 2 * KV_W), F32)
        dkv[TQ:TQ + BLK, :] = carry[...]
        lane = lax.broadcasted_iota(jnp.int32, (BLK, 128), 1)
        half = [lane < HEAD_DIM, lane >= HEAD_DIM]
        upper = _upper()
        upper_b = upper.astype(BF16)
        for j in range(nb):
            rq = slice(j * BLK, (j + 1) * BLK)
            rk = slice(j * BLK, (j + 2) * BLK)
            tab = tabc_ref[rq, :]
            pair = [slice(p * 128, (p + 1) * 128) for p in range(4)]
            g = [g_ref[rq, c].astype(F32) for c in pair]
            da = [dm_ref[rq, c].astype(F32) for c in pair]
            gate = [_silu_and_grad(g[p]) for p in range(4)]
            do = [da[p] * gate[p][0] for p in range(4)]
            dqs, dks, dvs = [], [], []
            for s in range(2):
                kk = kall[s, rk, :]
                vv = kall[2 + s, rk, :]
                _stack_heads(do_sc, s, half, do)
                pb = pm_ref[(2 * j + s) * BLK:(2 * j + s + 1) * BLK, :]
                prob = pb.astype(F32)
                _split_store(p_sc, s, upper_b, pb)
                dprob = _merge(upper, _nt(vv, do_sc[s]))
                dsum = jnp.sum(dprob * prob, axis=0, keepdims=True)
                _split_store(ds_sc, s, upper_b, (prob * (dprob - dsum)).astype(BF16))
                dsink_acc[s, 0:1, :] += ps_ref[2 * j + s:2 * j + s + 1, :] * dsum
                dqs.append(_tn(ds_sc[s], kk))
                dks.append(_nn(ds_sc[s], qs_ref[2 * j + s]))
                dvs.append(_nn(p_sc[s], do_sc[s]))
            for p in range(4):
                d_ref[rq, pair[p]] = _rope_t(_unstack_pair(half, dqs, p) * 0.125, tab).astype(BF16)
                d_ref[rq, 512 + p * 128:512 + (p + 1) * 128] = (
                    da[p] * att_ref[rq, pair[p]].astype(F32) * gate[p][1]).astype(BF16)
            dkv[rk, 0:128] += dks[0] + pltpu.roll(dks[1], 64, 1)
            dkv[rk, 128:256] += dvs[0] + pltpu.roll(dvs[1], 64, 1)
        d_ref[:, 1024:1152] = _rope_t(dkv[BLK:BLK + TQ, 0:128], tabc_ref[...]).astype(BF16)
        d_ref[:, 1152:1280] = dkv[BLK:BLK + TQ, 128:256].astype(BF16)
        carry[...] = dkv[0:BLK, :]

        @pl.when(step == nt - 1)
        def _():
            lanes = lax.broadcasted_iota(jnp.int32, (8, 128), 1)
            out = jnp.zeros((8, 128), F32)
            for s in range(2):
                for a, (p, e) in enumerate(HEADS[s]):
                    tot = jnp.sum(dsink_acc[s, 0:1, a * BLK:(a + 1) * BLK], axis=1, keepdims=True)
                    out = jnp.where(lanes == 2 * p + e, -tot, out)
            dsink_ref[...] = out

    rev = lambda s: nt - 1 - s
    return pl.pallas_call(
        body,
        name="attn_bwd",
        grid=(nt,),
        in_specs=_attn_specs(rev)[2:] + [pl.BlockSpec((TQ, ATTN_W), lambda s: (nt - 1 - s, 0))] * 2 + [
            pl.BlockSpec((2 * TQ, 4 * BLK), lambda s: (nt - 1 - s, 0)),
            pl.BlockSpec((2 * nb, 4 * BLK), lambda s: (nt - 1 - s, 0)),
            pl.BlockSpec((2 * nb, 4 * BLK, 128), lambda s: (nt - 1 - s, 0, 0)),
        ],
        out_specs=[
            pl.BlockSpec((TQ, PA_W), lambda s: (nt - 1 - s, 0)),
            pl.BlockSpec((8, 128), lambda s: (0, 0)),
        ],
        out_shape=[
            jax.ShapeDtypeStruct((S, PA_W), BF16),
            jax.ShapeDtypeStruct((8, 128), F32),
        ],
        scratch_shapes=[
            pltpu.VMEM((4, BLK + TQ, 128), BF16),
            pltpu.VMEM((BLK + TQ, 2 * KV_W), F32),
            pltpu.VMEM((BLK, 2 * KV_W), F32),
            pltpu.VMEM((2, 4 * BLK, 128), BF16),
            pltpu.VMEM((2, 2 * BLK, 4 * BLK), BF16),
            pltpu.VMEM((2, 2 * BLK, 4 * BLK), BF16),
            pltpu.VMEM((2, 8, 4 * BLK), F32),
        ],
        compiler_params=_params(("arbitrary",)),
    )(pa, pa, pa, tab, tab, dmix, att, probs, psinks, q_stack)


def _conv_bwd_tile(pc_ref, prev_ref, next_ref, dm_ref, dmn_ref, w_ref, d_ref, gw_ref, has_prev, has_next,
                   on_piece):
    rows = pc_ref.shape[0]
    w0, w1, w2 = w_ref[0:1, :], w_ref[1:2, :], w_ref[2:3, :]
    b, c, hh, gc, u, um1, um2, cv = _conv_tile(pc_ref, prev_ref, w_ref, has_prev)
    sg, dsg = _silu_and_grad(gc)
    dy = dm_ref[...].astype(F32)
    dyb = dy * b
    dcv = dyb * sg

    def next_dcv(r):
        nd = (dmn_ref[r:r + 1, :].astype(F32) * next_ref[r:r + 1, 0:512].astype(F32)
              * _silu(next_ref[r:r + 1, 1536:2048].astype(F32)))
        return jnp.where(has_next, nd, 0.0)

    row = lax.broadcasted_iota(jnp.int32, (rows, CONV_W), 0)
    dp1 = jnp.where(row == rows - 1, next_dcv(0), pltpu.roll(dcv, rows - 1, 0))
    dp2 = jnp.where(row == rows - 1, next_dcv(1),
                    jnp.where(row == rows - 2, next_dcv(0), pltpu.roll(dcv, rows - 2, 0)))
    du = w2 * dcv + w1 * dp1 + w0 * dp2
    pieces = (lambda: dy * cv * sg, lambda: du * hh, lambda: du * c, lambda: dyb * cv * dsg)
    for k, piece in enumerate(pieces):
        d_ref[:, k * CONV_W:(k + 1) * CONV_W] = piece().astype(BF16)
        on_piece(k)
    gw_ref[0:1, :] += jnp.sum(dcv * um2, axis=0, keepdims=True)
    gw_ref[1:2, :] += jnp.sum(dcv * um1, axis=0, keepdims=True)
    gw_ref[2:3, :] += jnp.sum(dcv * u, axis=0, keepdims=True)


def _grad_x(da, dc, wt, x, dh, norm_g, small, grads):
    S = x.shape[0]
    n_steps = S // TM
    rs = _ReduceScatter(grads)
    n_rs_out = len(rs.out_shape())
    small_rows = 8 + small.shape[0]

    def body(da_ref, dc_ref, wt_ref, x_ref, dh_ref, g_ref, small_ref, *rest):
        grad_refs, rest = rest[:rs.n], rest[rs.n:]
        gx_ref, all_ref = rest[:2]
        rs_out, rest = rest[2:2 + n_rs_out], rest[2 + n_rs_out:]
        gng, stage, small_send, small_recv, small_own = rest[:5]
        rs_scratch = rest[5:]
        step = pl.program_id(0)
        finish = rs.emit(step, n_steps, grad_refs, rs_out, rs_scratch)

        @pl.when(step == 0)
        def _():
            gng[...] = jnp.zeros_like(gng)

        dxn = (_nn(da_ref[:, 0:512], wt_ref[0:512, :]) + _nn(da_ref[:, 512:1024], wt_ref[768:1280, :])
               + _nn(da_ref[:, 1024:1280], wt_ref[512:768, :]) + _nn(dc_ref[...], wt_ref[1280:3328, :]))
        xv = x_ref[...]
        r = lax.rsqrt(jnp.mean(xv * xv, axis=-1, keepdims=True) + EPS)
        n = xv * r
        gng[...] += jnp.sum(dxn * n, axis=0, keepdims=True)
        dxg = dxn * g_ref[...]
        gx_ref[...] = dh_ref[...] + r * (dxg - n * jnp.mean(dxg * n, axis=-1, keepdims=True))

        @pl.when(step == n_steps - 1)
        def _():
            x_, y_, c_ = lax.axis_index("x"), lax.axis_index("y"), lax.axis_index("c")
            me = 4 * x_ + 2 * y_ + c_
            for q in range(8):
                stage[q:q + 1, :] = gng[:, q * 128:(q + 1) * 128]
            stage[8:small_rows, :] = small_ref[...]
            own = pltpu.make_async_copy(stage, all_ref.at[me], small_own)
            own.start()
            sends = []
            for k in range(1, N_DEV):
                cp = pltpu.make_async_remote_copy(
                    src_ref=stage, dst_ref=all_ref.at[me],
                    send_sem=small_send.at[k - 1], recv_sem=small_recv.at[k - 1],
                    device_id=(x_ ^ (k >> 2), y_ ^ ((k >> 1) & 1), c_ ^ (k & 1)), device_id_type=MESH)
                cp.start()
                sends.append(cp)
            for cp in sends:
                cp.wait_send()
                cp.wait_recv()
            own.wait()

        finish()

    row = lambda i: (i, 0)
    fixed = lambda i: (0, 0)
    any_spec = pl.BlockSpec(memory_space=pl.ANY)
    outs = pl.pallas_call(
        body,
        name="grad_x_reduce_scatter",
        grid=(n_steps,),
        in_specs=[
            pl.BlockSpec((TM, PA_W), row),
            pl.BlockSpec((TM, PC_W), row),
            pl.BlockSpec((IN_W, D_MODEL), fixed),
            pl.BlockSpec((TM, D_MODEL), row),
            pl.BlockSpec((TM, D_MODEL), row),
            pl.BlockSpec((1, D_MODEL), fixed),
            pl.BlockSpec(small.shape, fixed),
        ] + [any_spec] * rs.n,
        out_specs=[pl.BlockSpec((TM, D_MODEL), row), any_spec] + [any_spec] * n_rs_out,
        out_shape=[jax.ShapeDtypeStruct((S, D_MODEL), F32),
                   jax.ShapeDtypeStruct((N_DEV, small_rows, 128), F32)] + rs.out_shape(),
        scratch_shapes=[
            pltpu.VMEM((1, D_MODEL), F32),
            pltpu.VMEM((small_rows, 128), F32),
            pltpu.SemaphoreType.DMA((N_DEV - 1,)),
            pltpu.SemaphoreType.DMA((N_DEV - 1,)),
            pltpu.SemaphoreType.DMA,
        ] + rs.scratch_shapes(),
        compiler_params=_params(("arbitrary",)),
    )(da, dc, wt, x, dh, norm_g, small, *grads)
    return outs[0], outs[1], outs[2:2 + rs.n], outs[2 + rs.n:2 + 2 * rs.n]


def _grad_w_in(da, pc, dmix, conv_w, xn):
    S = xn.shape[0]
    tm = 2 * TM
    nt = S // tm
    t16 = tm // HALO

    def body(da_ref, pc_ref, prev_ref, next_ref, dm_ref, dmn_ref, cw_ref, xn_ref, gw_ref, dc_ref, gcw_ref):
        i = pl.program_id(0)

        @pl.when(i == 0)
        def _():
            gw_ref[...] = jnp.zeros_like(gw_ref)
            gcw_ref[...] = jnp.zeros_like(gcw_ref)

        xn = xn_ref[...]
        gw_ref[0:512, :] += _tn(da_ref[:, 0:512], xn)
        gw_ref[768:1280, :] += _tn(da_ref[:, 512:1024], xn)
        gw_ref[512:768, :] += _tn(da_ref[:, 1024:1280], xn)
        def piece_grad(k):
            rows = slice(PA_W + k * CONV_W, PA_W + (k + 1) * CONV_W)
            gw_ref[rows, :] += _tn(dc_ref[:, k * CONV_W:(k + 1) * CONV_W], xn)

        _conv_bwd_tile(pc_ref, prev_ref, next_ref, dm_ref, dmn_ref, cw_ref, dc_ref, gcw_ref, i > 0, i < nt - 1,
                       piece_grad)

    row = lambda i: (i, 0)
    fixed = lambda i: (0, 0)
    nxt = lambda i: jnp.minimum((i + 1) * t16, nt * t16 - 1)
    return pl.pallas_call(
        body,
        name="grad_w_in",
        grid=(nt,),
        in_specs=[
            pl.BlockSpec((tm, PA_W), row),
            pl.BlockSpec((tm, PC_W), row),
            pl.BlockSpec((HALO, PC_W), lambda i: (jnp.maximum(i * t16 - 1, 0), 0)),
            pl.BlockSpec((HALO, PC_W), lambda i: (nxt(i), 0)),
            pl.BlockSpec((tm, CONV_W), lambda i: (i, 1)),
            pl.BlockSpec((HALO, CONV_W), lambda i: (nxt(i), 1)),
            pl.BlockSpec((CONV_K, CONV_W), fixed),
            pl.BlockSpec((tm, D_MODEL), row),
        ],
        out_specs=[
            pl.BlockSpec((IN_W, D_MODEL), fixed, pipeline_mode=pl.Buffered(1)),
            pl.BlockSpec((tm, PC_W), row),
            pl.BlockSpec((CONV_K, CONV_W), fixed),
        ],
        out_shape=[
            jax.ShapeDtypeStruct((IN_W, D_MODEL), F32),
            jax.ShapeDtypeStruct((S, PC_W), BF16),
            jax.ShapeDtypeStruct((CONV_K, CONV_W), F32),
        ],
        compiler_params=_params(("arbitrary",)),
    )(da, pc, pc, pc, dmix, dmix, conv_w, xn)


def _adam_update(w, g, m, v):
    c1 = 1.0 - ADAM_B1 ** ADAM_STEP
    c2 = 1.0 - ADAM_B2 ** ADAM_STEP
    nm = ADAM_B1 * m + (1.0 - ADAM_B1) * g
    nv = ADAM_B2 * v + (1.0 - ADAM_B2) * (g * g)
    return -ADAM_LR * ((nm / c1) / (jnp.sqrt(nv / c2) + ADAM_EPS) + ADAM_WD * w), nm, nv


def _sum_chips_adamw(own, others, w, m, v, name):
    def body(own_ref, p_ref, w_ref, m_ref, v_ref, g_ref, d_ref, nm_ref, nv_ref):
        g = own_ref[...]
        for k in range(N_CHIP - 1):
            g = g + p_ref[k].astype(F32)
        g_ref[...] = g
        d_ref[...], nm_ref[...], nv_ref[...] = _adam_update(w_ref[...], g, m_ref[...], v_ref[...])

    rows, cols = w.shape
    half = rows // 2
    blk = pl.BlockSpec((half, cols), lambda i: (i, 0))
    shape = jax.ShapeDtypeStruct(w.shape, F32)
    return pl.pallas_call(
        body,
        name=name,
        grid=(2,),
        in_specs=[blk, pl.BlockSpec((N_CHIP - 1, half, cols), lambda i: (0, i, 0)), blk, blk, blk],
        out_specs=[blk] * 4,
        out_shape=[shape] * 4,
        compiler_params=_params(("arbitrary",)),
    )(own, others, w, m, v)


SMALL_ROWS = 96


def _small_adamw(parts, params):
    def body(parts_ref, *rest):
        prm, outs, total = rest[:12], rest[12:29], rest[29]
        me = 4 * lax.axis_index("x") + 2 * lax.axis_index("y") + lax.axis_index("c")
        acc = parts_ref[0]
        for d in range(1, N_DEV):
            acc = acc + parts_ref[d]
        total[...] = acc
        grads = (total[0:8, :], total[8:16, :], total[16:17, 0:8],
                 total[pl.ds(pl.multiple_of(32 + me * 8, 8), CONV_K), 0:64])
        outs[0][...] = total[24:25, 0:1]
        for k, g in enumerate(grads):
            w_ref, m_ref, v_ref = prm[3 * k:3 * k + 3]
            g_ref, d_ref, nm_ref, nv_ref = outs[1 + 4 * k:5 + 4 * k]
            g_ref[...] = g
            d_ref[...], nm_ref[...], nv_ref[...] = _adam_update(w_ref[...], g, m_ref[...], v_ref[...])

    flat = [a for p in params for a in p]
    out_shape = [jax.ShapeDtypeStruct((1, 1), F32)]
    for p in params:
        out_shape += [jax.ShapeDtypeStruct(p[0].shape, F32)] * 4
    return pl.pallas_call(
        body,
        name="adamw_small",
        out_shape=out_shape,
        scratch_shapes=[pltpu.VMEM((SMALL_ROWS, 128), F32)],
        compiler_params=_params(),
    )(parts, *flat)


def kernel(x, norm_g, w_in, sinks, conv_w, w_out, final_g, loss_target, m_norm_g, m_w_in, m_sinks, m_conv_w, m_w_out, m_final_g, v_norm_g, v_w_in, v_sinks, v_conv_w, v_w_out, v_final_g):
    S = x.shape[1]
    x2 = x.reshape(S, D_MODEL)
    t2 = loss_target.reshape(S, D_MODEL)
    ng = norm_g.reshape(1, D_MODEL)
    fg = final_g.reshape(1, D_MODEL)

    cw_pad = jnp.zeros((8, 128), F32).at[0:CONV_K, 0:64].set(conv_w)
    xn, tab, wt = _prologue(x2, ng, w_in.T)
    pa, pc, (wo, cw_all) = _fwd_proj(xn, wt, [w_out, cw_pad])
    cw = cw_all.reshape(N_DEV, 8, 128)[:, 0:CONV_K, 0:64].transpose(1, 0, 2).reshape(CONV_K, CONV_W)
    ya, att, probs, psinks, q_stack = _attn_fwd(pa, tab, sinks)
    dh, dmix, g_wo, g_fg, loss_part = _out_loss(x2, t2, ya, pc, cw, wo, fg)
    da, g_sinks = _attn_bwd(pa, dmix, att, probs, psinks, q_stack, tab)
    g_wt, dc, g_cw = _grad_w_in(da, pc, dmix, cw, xn)
    cw_pack = jnp.pad(g_cw.reshape(CONV_K, N_DEV, 64).transpose(1, 0, 2),
                      ((0, 0), (0, 8 - CONV_K), (0, 64))).reshape(N_DEV * 8, 128)
    small = jnp.concatenate([g_fg.reshape(8, 128), g_sinks, loss_part, cw_pack], axis=0)
    grad_x, parts, own, others = _grad_x(
        da, dc, wt, x2, dh, ng, small,
        [g_wt.reshape(N_DEV, SHARD_IN, D_MODEL), g_wo.reshape(N_DEV, SHARD_OUT, D_MODEL)])
    gt, dt, nmt, nvt = _sum_chips_adamw(own[0], others[0], w_in.T, m_w_in.T, v_w_in.T, "adamw_w_in")
    grad_w_in, d_w_in, nm_w_in, nv_w_in = gt.T, dt.T, nmt.T, nvt.T
    grad_w_out, d_w_out, nm_w_out, nv_w_out = _sum_chips_adamw(
        own[1], others[1], w_out, m_w_out, v_w_out, "adamw_w_out")
    vec = lambda a: a.reshape(8, 128)
    row = lambda a: a.reshape(1, 8)
    res = _small_adamw(parts, [
        (vec(norm_g), vec(m_norm_g), vec(v_norm_g)), (vec(final_g), vec(m_final_g), vec(v_final_g)),
        (row(sinks), row(m_sinks), row(v_sinks)), (conv_w, m_conv_w, v_conv_w)])
    loss = res[0].reshape(())
    grad_norm_g, d_ng, nm_ng, nv_ng = [a.reshape(D_MODEL) for a in res[1:5]]
    grad_final_g, d_fg, nm_fg, nv_fg = [a.reshape(D_MODEL) for a in res[5:9]]
    grad_sinks, d_sk, nm_sk, nv_sk = [a.reshape(N_Q_HEADS) for a in res[9:13]]
    grad_conv_w, d_cw, nm_cw, nv_cw = res[13:17]

    return (loss, grad_x.reshape(1, S, D_MODEL), grad_norm_g, grad_w_in, grad_sinks, grad_conv_w, grad_w_out, grad_final_g,
            d_ng, d_w_in, d_sk, d_cw, d_w_out, d_fg,
            nm_ng, nm_w_in, nm_sk, nm_cw, nm_w_out, nm_fg,
            nv_ng, nv_w_in, nv_sk, nv_cw, nv_w_out, nv_fg)
```

```python
import numpy as np
import jax
import jax.numpy as jnp
from jax import lax
from jax.experimental import pallas as pl
from jax.experimental.pallas import tpu as pltpu

F32 = jnp.float32
BF16 = jnp.bfloat16
MESH = pl.DeviceIdType.MESH

D_MODEL = 1024
HEAD_DIM = 64
N_Q_HEADS = 8
ATTN_W = 512
KV_W = 128
BLK = 128
CONV_W = 512
CONV_K = 3
IN_W = 3328
PA_W = 1280
PC_W = 2048
EPS = 1e-5
ROPE_THETA = 500000.0
ROT_DIM = 16
N_DEV = 8
N_CHIP = 4
SHARD_IN = IN_W // N_DEV
SHARD_OUT = D_MODEL // N_DEV

ADAM_LR = 0.001
ADAM_B1 = 0.9
ADAM_B2 = 0.999
ADAM_EPS = 1e-08
ADAM_WD = 0.01
ADAM_STEP = 10

ACT = jnp.bfloat16

TM = 512
TQ = 1024
HALO = 16
VMEM_LIMIT = 56 * 1024 * 1024

NT_DIMS = (((1,), (1,)), ((), ()))
TN_DIMS = (((0,), (0,)), ((), ()))


def _params(sem=None):
    kw = dict(vmem_limit_bytes=VMEM_LIMIT)
    if sem is not None:
        kw["dimension_semantics"] = sem
    return pltpu.CompilerParams(**kw)


def _nt(a, b):
    return lax.dot_general(a, b, NT_DIMS, preferred_element_type=F32)


def _tn(a, b):
    return lax.dot_general(a, b, TN_DIMS, preferred_element_type=F32)


def _nn(a, b):
    return jnp.dot(a, b, preferred_element_type=F32)


def _silu(g):
    return g * jax.nn.sigmoid(g)


def _silu_and_grad(g):
    s = jax.nn.sigmoid(g)
    return g * s, s * (1.0 + g * (1.0 - s))


class _AllGatherInSteps:
    def __init__(self, arrs, forward_step):
        self.blocks = [(a.shape, a.dtype) for a in arrs]
        self.n = len(arrs)
        self.forward_step = forward_step

    def out_shape(self):
        return [jax.ShapeDtypeStruct((N_DEV * s[0], s[1]), d) for s, d in self.blocks]

    def scratch_shapes(self):
        return [pltpu.SemaphoreType.DMA((7 * self.n,)), pltpu.SemaphoreType.DMA((7 * self.n,)),
                pltpu.SemaphoreType.DMA((self.n,))]

    def emit(self, step, n_steps, x_refs, out_refs, scratch):
        assert n_steps > self.forward_step + 1
        send_sems, recv_sems, local_sems = scratch
        x, y, c = lax.axis_index("x"), lax.axis_index("y"), lax.axis_index("c")
        me, sibling = (x, y, c), (x, y, 1 - c)
        chips = [(1 - x, y), (x, 1 - y), (1 - x, 1 - y)]

        def rows(a, px, py, pc):
            m = self.blocks[a][0][0]
            return out_refs[a].at[pl.ds((4 * px + 2 * py + pc) * m, m), :]

        def copy(a, k, block, to, src=None):
            return pltpu.make_async_remote_copy(
                src_ref=rows(a, *block) if src is None else src, dst_ref=rows(a, *block),
                send_sem=send_sems.at[a * 7 + k], recv_sem=recv_sems.at[a * 7 + k],
                device_id=to, device_id_type=MESH)

        def mine(a):
            return pltpu.make_async_copy(x_refs[a], rows(a, *me), local_sems.at[a])

        def first(a):
            return ([copy(a, 0, me, sibling, src=x_refs[a])]
                    + [copy(a, 1 + j, me, (*chip, c), src=x_refs[a]) for j, chip in enumerate(chips)])

        def passed(a):
            return [copy(a, 4 + j, (*chip, c), sibling) for j, chip in enumerate(chips)]

        @pl.when(step == 0)
        def _():
            for a in range(self.n):
                mine(a).start()
                for cp in first(a):
                    cp.start()

        @pl.when(step == self.forward_step)
        def _():
            for j, chip in enumerate(chips):
                for a in range(self.n):
                    copy(a, 1 + j, (*chip, c), me).wait_recv()
                    copy(a, 4 + j, (*chip, c), sibling).start()

        def finish():
            @pl.when(step == n_steps - 1)
            def _():
                for a in range(self.n):
                    copy(a, 0, sibling, me).wait_recv()
                    for j, chip in enumerate(chips):
                        copy(a, 4 + j, (*chip, 1 - c), me).wait_recv()
                    for cp in first(a) + passed(a):
                        cp.wait_send()
                    mine(a).wait()

        return finish


class _AllGatherViaNeighbours:
    def __init__(self, arr, first, mid, second):
        (self.m, self.ncol), self.dtype = arr.shape, arr.dtype
        assert self.m % 32 == 0
        self.first, self.mid, self.second = first, mid, second

    def out_shape(self):
        return [jax.ShapeDtypeStruct((N_DEV * self.m, self.ncol), self.dtype)]

    def scratch_shapes(self):
        return [pltpu.SemaphoreType.DMA((11,)), pltpu.SemaphoreType.DMA((11,)), pltpu.SemaphoreType.DMA]

    def emit(self, step, n_steps, x_ref, out_ref, scratch):
        assert 0 < self.first < self.mid < self.second < n_steps - 1
        send_sems, recv_sems, local_sem = scratch
        x, y, c = lax.axis_index("x"), lax.axis_index("y"), lax.axis_index("c")
        half = self.m // 2
        sibling, xn, yn = (x, y, 1 - c), (1 - x, y, c), (x, 1 - y, c)

        def rows(dev, part=None):
            px, py, pc = dev
            base = (4 * px + 2 * py + pc) * self.m
            if part is None:
                return out_ref.at[pl.ds(base, self.m), :]
            return out_ref.at[pl.ds(base + part * half, half), :]

        def copy(k, dev, to, part=None, src=None):
            return pltpu.make_async_remote_copy(
                src_ref=rows(dev, part) if src is None else src, dst_ref=rows(dev, part),
                send_sem=send_sems.at[k], recv_sem=recv_sems.at[k], device_id=to, device_id_type=MESH)

        me, dg = (x, y, c), (1 - x, 1 - y, c)
        mine = pltpu.make_async_copy(x_ref, rows(me), local_sem)
        my_half = lambda part: x_ref.at[pl.ds(part * half, half), :]
        sends = [
            copy(0, me, sibling, src=x_ref), copy(1, me, xn, part=0, src=my_half(0)),
            copy(2, me, yn, part=1, src=my_half(1)), copy(3, xn, yn, part=0), copy(4, yn, xn, part=1),
            copy(5, xn, sibling), copy(6, yn, sibling), copy(7, dg, sibling, part=0), copy(8, dg, sibling, part=1),
            copy(9, me, xn, part=1, src=my_half(1)), copy(10, me, yn, part=0, src=my_half(0)),
        ]
        other = lambda dev: (dev[0], dev[1], 1 - c)
        arrivals = [
            copy(0, other(me), sibling), copy(1, xn, xn, part=0), copy(2, yn, yn, part=1), copy(3, dg, yn, part=0),
            copy(4, dg, xn, part=1), copy(5, other(xn), sibling), copy(6, other(yn), sibling),
            copy(7, other(dg), sibling, part=0), copy(8, other(dg), sibling, part=1),
            copy(9, xn, xn, part=1), copy(10, yn, yn, part=0),
        ]

        @pl.when(step == 0)
        def _():
            mine.start()
            for k in (0, 1, 2, 9, 10):
                sends[k].start()

        @pl.when(step == self.first)
        def _():
            arrivals[1].wait_recv()
            sends[3].start()
            arrivals[2].wait_recv()
            sends[4].start()

        @pl.when(step == self.mid)
        def _():
            arrivals[9].wait_recv()
            sends[5].start()
            arrivals[10].wait_recv()
            sends[6].start()

        @pl.when(step == self.second)
        def _():
            arrivals[3].wait_recv()
            sends[7].start()
            arrivals[4].wait_recv()
            sends[8].start()

        def finish():
            @pl.when(step == n_steps - 1)
            def _():
                for k in (0, 5, 6, 7, 8):
                    arrivals[k].wait_recv()
                for cp in sends:
                    cp.wait_send()
                mine.wait()

        return finish


class _ReduceScatter:
    def __init__(self, grads):
        self.shapes = [g.shape[1:] for g in grads]
        self.n = len(grads)
        self.items = tuple((a, r) for r in (1, 2, 3, 0) for a in range(self.n))
        self.steps = N_CHIP + 2

    def out_shape(self):
        own = [jax.ShapeDtypeStruct(s, F32) for s in self.shapes]
        ici = [jax.ShapeDtypeStruct((N_CHIP - 1,) + s, BF16) for s in self.shapes]
        land = [jax.ShapeDtypeStruct((N_CHIP,) + s, F32) for s in self.shapes]
        return own + ici + land

    def scratch_shapes(self):
        n_items = len(self.items)
        return ([pltpu.VMEM((2,) + s, F32) for s in self.shapes]
                + [pltpu.VMEM((N_CHIP - 1,) + s, BF16) for s in self.shapes]
                + [pltpu.VMEM(s, F32) for s in self.shapes]
                + [pltpu.SemaphoreType.DMA((self.n * N_CHIP,))] * 2
                + [pltpu.SemaphoreType.DMA((2 * n_items,))]
                + [pltpu.SemaphoreType.DMA((self.n * (N_CHIP - 1),))] * 2
                + [pltpu.SemaphoreType.DMA((self.n,))])

    def emit(self, step, n_steps, g_refs, out_refs, scratch):
        assert n_steps > self.steps
        n = self.n
        own_refs, ici_refs, land_refs = out_refs[:n], out_refs[n:2 * n], out_refs[2 * n:]
        stage, pair_bf, pair_own = scratch[:n], scratch[n:2 * n], scratch[2 * n:3 * n]
        sib_send, sib_recv, load_sems, ici_send, ici_recv, own_sems = scratch[3 * n:]
        x, y, c = lax.axis_index("x"), lax.axis_index("y"), lax.axis_index("c")

        def chip_of(r):
            return (x ^ (r >> 1), y ^ (r & 1))

        def block_of(r, core):
            cx, cy = chip_of(r)
            return 4 * cx + 2 * cy + core

        def to_sibling(a, r):
            return pltpu.make_async_remote_copy(
                src_ref=g_refs[a].at[block_of(r, 1 - c)], dst_ref=land_refs[a].at[r],
                send_sem=sib_send.at[a * N_CHIP + r], recv_sem=sib_recv.at[a * N_CHIP + r],
                device_id=(x, y, 1 - c), device_id_type=MESH)

        def loads(k):
            a, r = self.items[k]
            return (pltpu.make_async_copy(g_refs[a].at[block_of(r, c)], stage[a].at[0], load_sems.at[2 * k]),
                    pltpu.make_async_copy(land_refs[a].at[r], stage[a].at[1], load_sems.at[2 * k + 1]))

        def to_owner(k):
            a, r = self.items[k]
            if r == 0:
                return pltpu.make_async_copy(pair_own[a], own_refs[a], own_sems.at[a])
            return pltpu.make_async_remote_copy(
                src_ref=pair_bf[a].at[r - 1], dst_ref=ici_refs[a].at[r - 1],
                send_sem=ici_send.at[a * (N_CHIP - 1) + r - 1], recv_sem=ici_recv.at[a * (N_CHIP - 1) + r - 1],
                device_id=(*chip_of(r), c), device_id_type=MESH)

        @pl.when(step == 0)
        def _():
            for a, r in self.items:
                to_sibling(a, r).start()

        def fetch(k):
            a, r = self.items[k]
            to_sibling(a, r).wait_recv()
            for cp in loads(k):
                cp.start()

        def add_and_send(k):
            a, r = self.items[k]
            for cp in loads(k):
                cp.wait()
            total = stage[a][0] + stage[a][1]
            if r == 0:
                pair_own[a][...] = total
            else:
                pair_bf[a][r - 1] = total.astype(BF16)
            to_owner(k).start()

        for g in range(N_CHIP + 1):
            @pl.when(step == 1 + g)
            def _(g=g):
                if g > 0:
                    for k in range((g - 1) * n, g * n):
                        add_and_send(k)
                if g < N_CHIP:
                    for k in range(g * n, (g + 1) * n):
                        fetch(k)

        def finish():
            @pl.when(step == n_steps - 1)
            def _():
                for k, (a, r) in enumerate(self.items):
                    if r == 0:
                        to_owner(k).wait()
                    else:
                        to_owner(k).wait_send()
                        to_owner(k).wait_recv()
                for a, r in self.items:
                    to_sibling(a, r).wait_send()

        return finish


def _prologue(x, norm_g, w_shard):
    S = x.shape[0]
    n_steps = S // TM
    half = ROT_DIM // 2
    pos = np.arange(S, dtype=np.float32)
    inv_freq = np.float32(ROPE_THETA) ** (-np.arange(0, ROT_DIM, 2, dtype=np.float32) / np.float32(ROT_DIM))
    ang = inv_freq.astype(np.float32)[:, None] * pos[None, :]
    cs = jnp.asarray(np.concatenate([np.cos(ang), np.sin(ang)], axis=0).astype(np.float32))
    ag = _AllGatherViaNeighbours(jax.ShapeDtypeStruct(w_shard.shape, BF16),
                                 first=n_steps // 4, mid=n_steps // 2 + 2, second=n_steps - 2)

    def body(x_ref, g_ref, cs_ref, w_ref, xn_ref, tab_ref, wt_ref, w_bf, *ag_scratch):
        step = pl.program_id(0)

        @pl.when(step == 0)
        def _():
            w_bf[...] = w_ref[...].astype(BF16)

        finish = ag.emit(step, n_steps, w_bf, wt_ref, ag_scratch)
        xv = x_ref[...]
        r = lax.rsqrt(jnp.mean(xv * xv, axis=-1, keepdims=True) + EPS)
        xn_ref[...] = (xv * r * g_ref[...]).astype(BF16)

        xt = jnp.concatenate([cs_ref[...], jnp.zeros((128 - 2 * half, TM), F32)], axis=0).T
        lane = lax.broadcasted_iota(jnp.int32, (TM, 128), 1)
        rr = lane & (HEAD_DIM - 1)
        first = lane < HEAD_DIM

        def at(shift_first, shift_second):
            return jnp.where(first, pltpu.roll(xt, shift_first, 1) if shift_first else xt,
                             pltpu.roll(xt, shift_second, 1))

        cos_lo, cos_hi = at(0, HEAD_DIM), at(half, HEAD_DIM + half)
        sin_lo, sin_hi = at(128 - half, HEAD_DIM - half), at(0, HEAD_DIM)
        tab_ref[:, 0:128] = jnp.where(rr < half, cos_lo, jnp.where(rr < ROT_DIM, cos_hi, 1.0))
        tab_ref[:, 128:256] = jnp.where(rr < half, -sin_lo, 0.0)
        tab_ref[:, 256:384] = jnp.where((rr >= half) & (rr < ROT_DIM), sin_hi, 0.0)
        finish()

    any_spec = pl.BlockSpec(memory_space=pl.ANY)
    return pl.pallas_call(
        body,
        name="prologue_all_gather_w_in",
        grid=(n_steps,),
        in_specs=[
            pl.BlockSpec((TM, D_MODEL), lambda i: (i, 0)),
            pl.BlockSpec((1, D_MODEL), lambda i: (0, 0)),
            pl.BlockSpec((2 * half, TM), lambda i: (0, i)),
            pl.BlockSpec(w_shard.shape, lambda i: (0, 0)),
        ],
        out_specs=[
            pl.BlockSpec((TM, D_MODEL), lambda i: (i, 0)),
            pl.BlockSpec((TM, 384), lambda i: (i, 0)),
            any_spec,
        ],
        out_shape=[
            jax.ShapeDtypeStruct((S, D_MODEL), BF16),
            jax.ShapeDtypeStruct((S, 384), F32),
        ] + ag.out_shape(),
        scratch_shapes=[pltpu.VMEM(w_shard.shape, BF16)] + ag.scratch_shapes(),
        compiler_params=_params(("arbitrary",)),
    )(x, norm_g, cs, w_shard)


def _fwd_proj(xn, wt, later):
    S = xn.shape[0]
    tm = 2 * TM
    n_steps = S // tm
    ag = _AllGatherInSteps([jax.ShapeDtypeStruct(later[0].shape, BF16), jax.ShapeDtypeStruct((8, 128), F32)],
                           forward_step=n_steps // 2)

    def body(xn_ref, wt_ref, *rest):
        later_refs, rest = rest[:ag.n], rest[ag.n:]
        pa_ref, pc_ref = rest[:2]
        gathered, w_bf, cw_pad, ag_scratch = rest[2:2 + ag.n], rest[2 + ag.n], rest[3 + ag.n], rest[4 + ag.n:]
        step = pl.program_id(0)

        @pl.when(step == 0)
        def _():
            w_bf[...] = later_refs[0][...].astype(BF16)
            cw_pad[...] = jnp.zeros_like(cw_pad)
            cw_pad[0:CONV_K, 0:64] = later_refs[1][...]

        finish = ag.emit(step, n_steps, (w_bf, cw_pad), gathered, ag_scratch)
        xn = xn_ref[...]
        pa_ref[:, 0:512] = _nt(xn, wt_ref[0:512, :]).astype(ACT)
        pa_ref[:, 512:1024] = _nt(xn, wt_ref[768:1280, :]).astype(ACT)
        pa_ref[:, 1024:1280] = _nt(xn, wt_ref[512:768, :]).astype(ACT)
        pc_ref[...] = _nt(xn, wt_ref[1280:3328, :]).astype(ACT)
        finish()

    any_spec = pl.BlockSpec(memory_space=pl.ANY)
    outs = pl.pallas_call(
        body,
        name="fwd_proj_all_gather",
        grid=(n_steps,),
        in_specs=[
            pl.BlockSpec((tm, D_MODEL), lambda i: (i, 0)),
            pl.BlockSpec((IN_W, D_MODEL), lambda i: (0, 0)),
            pl.BlockSpec(later[0].shape, lambda i: (0, 0)),
            pl.BlockSpec(later[1].shape, lambda i: (0, 0)),
        ],
        out_specs=[
            pl.BlockSpec((tm, PA_W), lambda i: (i, 0)),
            pl.BlockSpec((tm, PC_W), lambda i: (i, 0)),
        ] + [any_spec] * ag.n,
        out_shape=[
            jax.ShapeDtypeStruct((S, PA_W), ACT),
            jax.ShapeDtypeStruct((S, PC_W), ACT),
        ] + ag.out_shape(),
        scratch_shapes=[pltpu.VMEM(later[0].shape, BF16), pltpu.VMEM((8, 128), F32)] + ag.scratch_shapes(),
        compiler_params=_params(("arbitrary",)),
    )(xn, wt, *later)
    return outs[0], outs[1], outs[2:]


def _rope(t, tab):
    return (t * tab[:, 0:128] + pltpu.roll(t, 120, 1) * tab[:, 128:256]
            + pltpu.roll(t, 8, 1) * tab[:, 256:384])


def _rope_t(d, tab):
    return (d * tab[:, 0:128] + pltpu.roll(d * tab[:, 128:256], 8, 1)
            + pltpu.roll(d * tab[:, 256:384], 120, 1))


def _fill_kv(kall, kvc_ref, kvp_ref, tabc_ref, tabp_ref):
    for lo, kv_ref, tab_ref, n in ((0, kvp_ref, tabp_ref, BLK), (BLK, kvc_ref, tabc_ref, TQ)):
        k = _rope(kv_ref[:, 0:128].astype(F32), tab_ref[...])
        v = kv_ref[:, 128:256].astype(F32)
        kall[0, lo:lo + n, :] = k.astype(BF16)
        kall[1, lo:lo + n, :] = pltpu.roll(k, 64, 1).astype(BF16)
        kall[2, lo:lo + n, :] = v.astype(BF16)
        kall[3, lo:lo + n, :] = pltpu.roll(v, 64, 1).astype(BF16)


HEADS = (((0, 0), (1, 0), (2, 1), (3, 1)), ((0, 1), (1, 1), (2, 0), (3, 0)))


def _upper():
    kj = lax.broadcasted_iota(jnp.int32, (BLK, 4 * BLK), 0)
    qi = lax.broadcasted_iota(jnp.int32, (BLK, 4 * BLK), 1) & (BLK - 1)
    return kj > qi


def _merge(upper, both):
    return jnp.where(upper, both[0:BLK, :], both[BLK:2 * BLK, :])


def _split_store(ref, s, upper_b, vb):
    first = vb * upper_b
    ref[s, 0:BLK, :] = first
    ref[s, BLK:2 * BLK, :] = vb - first


def _sink_rows(sink_ref):
    return [jnp.concatenate([jnp.full((1, BLK), sink_ref[2 * p + e], F32) for p, e in HEADS[s]], axis=1)
            for s in range(2)]


def _stack_heads(ref, slot, half, pairs, s=None):
    for a, (p, e) in enumerate(HEADS[slot if s is None else s]):
        ref[slot, a * BLK:(a + 1) * BLK, :] = jnp.where(half[e], pairs[p], 0.0).astype(BF16)


def _unstack_pair(half, outs, p):
    lo = 0 if p < 2 else 1
    rows = slice(p * BLK, (p + 1) * BLK)
    return jnp.where(half[0], outs[lo][rows, :], outs[1 - lo][rows, :])


def _softmax(sm, sinks):
    m = jnp.maximum(jnp.max(sm, axis=0, keepdims=True), sinks)
    p = jnp.exp(sm - m)
    es = jnp.exp(sinks - m)
    inv = 1.0 / (jnp.sum(p, axis=0, keepdims=True) + es)
    return p * inv, es * inv


def _scores(kk, q_stack, first):
    st = _nt(kk, q_stack)
    prev = st[0:BLK, :]
    if first is not None:
        prev = prev + jnp.where(first, -jnp.inf, 0.0)
    return prev, st[BLK:2 * BLK, :]


def _attn_specs(tile):
    nb = TQ // BLK
    prev = lambda i: jnp.maximum(tile(i) * nb - 1, 0)
    return [
        pl.BlockSpec(memory_space=pltpu.SMEM),
        pl.BlockSpec((TQ, ATTN_W), lambda i: (tile(i), 0)),
        pl.BlockSpec((TQ, ATTN_W), lambda i: (tile(i), 1)),
        pl.BlockSpec((TQ, 2 * KV_W), lambda i: (tile(i), 4)),
        pl.BlockSpec((BLK, 2 * KV_W), lambda i: (prev(i), 4)),
        pl.BlockSpec((TQ, 384), lambda i: (tile(i), 0)),
        pl.BlockSpec((BLK, 384), lambda i: (prev(i), 0)),
    ]


def _attn_fwd(pa, tab, sinks):
    S = pa.shape[0]
    nb = TQ // BLK

    def body(sink_ref, q_ref, g_ref, kvc_ref, kvp_ref, tabc_ref, tabp_ref, o_ref, att_ref, pm_ref, ps_ref,
             qs_ref, kall, p_sc):
        i = pl.program_id(0)
        _fill_kv(kall, kvc_ref, kvp_ref, tabc_ref, tabp_ref)
        lane = lax.broadcasted_iota(jnp.int32, (BLK, 128), 1)
        half = [lane < HEAD_DIM, lane >= HEAD_DIM]
        upper = _upper()
        upper_b = upper.astype(BF16)
        sinks = _sink_rows(sink_ref)
        for j in range(nb):
            rq = slice(j * BLK, (j + 1) * BLK)
            rk = slice(j * BLK, (j + 2) * BLK)
            tab = tabc_ref[rq, :]
            qr = [_rope(q_ref[rq, p * 128:(p + 1) * 128].astype(F32), tab) * 0.125 for p in range(4)]
            outs = []
            for s in range(2):
                _stack_heads(qs_ref, 2 * j + s, half, qr, s)
                prev, cur = _scores(kall[s, rk, :], qs_ref[2 * j + s], i == 0 if j == 0 else None)
                prob, psink = _softmax(jnp.where(upper, prev, cur), sinks[s])
                pb = prob.astype(BF16)
                pm_ref[(2 * j + s) * BLK:(2 * j + s + 1) * BLK, :] = pb
                ps_ref[2 * j + s:2 * j + s + 1, :] = psink
                _split_store(p_sc, s, upper_b, pb)
                outs.append(_tn(p_sc[s], kall[2 + s, rk, :]))
            for p in range(4):
                cols = slice(p * 128, (p + 1) * 128)
                att = _unstack_pair(half, outs, p)
                att_ref[rq, cols] = att.astype(BF16)
                o_ref[rq, cols] = (att * _silu(g_ref[rq, cols].astype(F32))).astype(BF16)

    return pl.pallas_call(
        body,
        name="attn_fwd",
        grid=(S // TQ,),
        in_specs=_attn_specs(lambda i: i),
        out_specs=[pl.BlockSpec((TQ, ATTN_W), lambda i: (i, 0))] * 2 + [
            pl.BlockSpec((2 * TQ, 4 * BLK), lambda i: (i, 0)),
            pl.BlockSpec((2 * nb, 4 * BLK), lambda i: (i, 0)),
            pl.BlockSpec((2 * nb, 4 * BLK, 128), lambda i: (i, 0, 0)),
        ],
        out_shape=[jax.ShapeDtypeStruct((S, ATTN_W), BF16)] * 2 + [
            jax.ShapeDtypeStruct((2 * S, 4 * BLK), BF16),
            jax.ShapeDtypeStruct((2 * S // BLK, 4 * BLK), F32),
            jax.ShapeDtypeStruct((2 * S // BLK, 4 * BLK, 128), BF16),
        ],
        scratch_shapes=[
            pltpu.VMEM((4, BLK + TQ, 128), BF16),
            pltpu.VMEM((2, 2 * BLK, 4 * BLK), BF16),
        ],
        compiler_params=_params(("arbitrary",)),
    )(sinks, pa, pa, pa, pa, tab, tab)


def _shift_down(u, halo_ref, has_prev):
    def halo_u(r):
        hu = halo_ref[r:r + 1, 512:1024].astype(F32) * halo_ref[r:r + 1, 1024:1536].astype(F32)
        return jnp.where(has_prev, hu, 0.0)

    row = lax.broadcasted_iota(jnp.int32, u.shape, 0)
    um1 = jnp.where(row == 0, halo_u(HALO - 1), pltpu.roll(u, 1, 0))
    um2 = jnp.where(row == 0, halo_u(HALO - 2), jnp.where(row == 1, halo_u(HALO - 1), pltpu.roll(u, 2, 0)))
    return um1, um2


def _gathered_conv_w(all_ref):
    pairs = [all_ref[16 * p:16 * p + 8, :] + pltpu.roll(all_ref[16 * p + 8:16 * p + 16, :], 64, 1)
             for p in range(N_DEV // 2)]
    return jnp.concatenate(pairs, axis=1)


def _conv_tile(pc_ref, halo_ref, w_ref, has_prev):
    b = pc_ref[:, 0:512].astype(F32)
    c = pc_ref[:, 512:1024].astype(F32)
    hh = pc_ref[:, 1024:1536].astype(F32)
    gc = pc_ref[:, 1536:2048].astype(F32)
    u = c * hh
    um1, um2 = _shift_down(u, halo_ref, has_prev)
    cv = w_ref[0:1, :] * um2 + w_ref[1:2, :] * um1 + w_ref[2:3, :] * u
    return b, c, hh, gc, u, um1, um2, cv


def _prev_rows(width, col=0):
    return pl.BlockSpec((HALO, width), lambda i: (jnp.maximum(i * (TM // HALO) - 1, 0), col))


def _out_loss(x, target, ya, pc, conv_w, w_out, final_g):
    S = x.shape[0]

    def body(x_ref, t_ref, ya_ref, pc_ref, halo_ref, cw_ref, wo_ref, fg_ref,
             dh_ref, dmix_ref, gwo_ref, gfg_ref, loss_ref, cw_out_ref):
        cw = _gathered_conv_w(cw_ref)

        @pl.when(pl.program_id(0) == 0)
        def _():
            gwo_ref[...] = jnp.zeros_like(gwo_ref)
            gfg_ref[...] = jnp.zeros_like(gfg_ref)
            loss_ref[...] = jnp.zeros_like(loss_ref)
            cw_out_ref[...] = cw[0:CONV_K, :]

        b, _, _, gc, _, _, _, cv = _conv_tile(pc_ref, halo_ref, cw, pl.program_id(0) > 0)
        yc = (b * cv * _silu(gc)).astype(BF16)
        mix = jnp.concatenate([ya_ref[...], yc], axis=1)
        wo = wo_ref[...]
        fg = fg_ref[...]
        h = x_ref[...] + _nn(mix, wo)
        r = lax.rsqrt(jnp.mean(h * h, axis=-1, keepdims=True) + EPS)
        n = h * r
        err = n * fg - t_ref[...]
        loss_ref[...] += jnp.broadcast_to(
            0.5 * jnp.sum(jnp.mean(err * err, axis=-1, keepdims=True), axis=0, keepdims=True), (8, 128))
        gfg_ref[...] += jnp.sum(err * n, axis=0, keepdims=True) * (1.0 / D_MODEL)
        dyg = err * (fg * (1.0 / D_MODEL))
        dh = r * (dyg - n * jnp.mean(dyg * n, axis=-1, keepdims=True))
        dh_ref[...] = dh
        dhb = dh.astype(BF16)
        dmix_ref[...] = _nt(dhb, wo).astype(ACT)
        gwo_ref[...] += _tn(mix, dhb)

    row = lambda i: (i, 0)
    fixed = lambda i: (0, 0)
    return pl.pallas_call(
        body,
        name="out_loss",
        grid=(S // TM,),
        in_specs=[
            pl.BlockSpec((TM, D_MODEL), row),
            pl.BlockSpec((TM, D_MODEL), row),
            pl.BlockSpec((TM, ATTN_W), row),
            pl.BlockSpec((TM, PC_W), row),
            _prev_rows(PC_W),
            pl.BlockSpec((N_DEV * 8, 128), fixed),
            pl.BlockSpec((D_MODEL, D_MODEL), fixed),
            pl.BlockSpec((1, D_MODEL), fixed),
        ],
        out_specs=[
            pl.BlockSpec((TM, D_MODEL), row),
            pl.BlockSpec((TM, D_MODEL), row),
            pl.BlockSpec((D_MODEL, D_MODEL), fixed),
            pl.BlockSpec((1, D_MODEL), fixed),
            pl.BlockSpec((8, 128), fixed),
            pl.BlockSpec((CONV_K, CONV_W), fixed),
        ],
        out_shape=[
            jax.ShapeDtypeStruct((S, D_MODEL), F32),
            jax.ShapeDtypeStruct((S, D_MODEL), ACT),
            jax.ShapeDtypeStruct((D_MODEL, D_MODEL), F32),
            jax.ShapeDtypeStruct((1, D_MODEL), F32),
            jax.ShapeDtypeStruct((8, 128), F32),
            jax.ShapeDtypeStruct((CONV_K, CONV_W), F32),
        ],
        compiler_params=_params(("arbitrary",)),
    )(x, target, ya, pc, pc, conv_w, w_out, final_g)


def _attn_bwd(pa, dmix, att, probs, psinks, q_stack, tab):
    S = pa.shape[0]
    nt = S // TQ
    nb = TQ // BLK

    def body(g_ref, kvc_ref, kvp_ref, tabc_ref, tabp_ref, dm_ref, att_ref, pm_ref, ps_ref, qs_ref,
             d_ref, dsink_ref, kall, dkv, carry, do_sc, p_sc, ds_sc, dsink_acc):
        step = pl.program_id(0)

        @pl.when(step == 0)
        def _():
            carry[...] = jnp.zeros_like(carry)
            dsink_acc[...] = jnp.zeros_like(dsink_acc)

        _fill_kv(kall, kvc_ref, kvp_ref, tabc_ref, tabp_ref)
        dkv[0:TQ, :] = jnp.zeros((TQ, 2 * KV_W), F32)
        dkv[TQ:TQ + BLK, :] = carry[...]
        lane = lax.broadcasted_iota(jnp.int32, (BLK, 128), 1)
        half = [lane < HEAD_DIM, lane >= HEAD_DIM]
        upper = _upper()
        upper_b = upper.astype(BF16)
        for j in range(nb):
            rq = slice(j * BLK, (j + 1) * BLK)
            rk = slice(j * BLK, (j + 2) * BLK)
            tab = tabc_ref[rq, :]
            pair = [slice(p * 128, (p + 1) * 128) for p in range(4)]
            g = [g_ref[rq, c].astype(F32) for c in pair]
            da = [dm_ref[rq, c].astype(F32) for c in pair]
            gate = [_silu_and_grad(g[p]) for p in range(4)]
            do = [da[p] * gate[p][0] for p in range(4)]
            dqs, dks, dvs = [], [], []
            for s in range(2):
                kk = kall[s, rk, :]
                vv = kall[2 + s, rk, :]
                _stack_heads(do_sc, s, half, do)
                pb = pm_ref[(2 * j + s) * BLK:(2 * j + s + 1) * BLK, :]
                prob = pb.astype(F32)
                _split_store(p_sc, s, upper_b, pb)
                dprob = _merge(upper, _nt(vv, do_sc[s]))
                dsum = jnp.sum(dprob * prob, axis=0, keepdims=True)
                _split_store(ds_sc, s, upper_b, (prob * (dprob - dsum)).astype(BF16))
                dsink_acc[s, 0:1, :] += ps_ref[2 * j + s:2 * j + s + 1, :] * dsum
                dqs.append(_tn(ds_sc[s], kk))
                dks.append(_nn(ds_sc[s], qs_ref[2 * j + s]))
                dvs.append(_nn(p_sc[s], do_sc[s]))
            for p in range(4):
                d_ref[rq, pair[p]] = _rope_t(_unstack_pair(half, dqs, p) * 0.125, tab).astype(BF16)
                d_ref[rq, 512 + p * 128:512 + (p + 1) * 128] = (
                    da[p] * att_ref[rq, pair[p]].astype(F32) * gate[p][1]).astype(BF16)
            dkv[rk, 0:128] += dks[0] + pltpu.roll(dks[1], 64, 1)
            dkv[rk, 128:256] += dvs[0] + pltpu.roll(dvs[1], 64, 1)
        d_ref[:, 1024:1152] = _rope_t(dkv[BLK:BLK + TQ, 0:128], tabc_ref[...]).astype(BF16)
        d_ref[:, 1152:1280] = dkv[BLK:BLK + TQ, 128:256].astype(BF16)
        carry[...] = dkv[0:BLK, :]

        @pl.when(step == nt - 1)
        def _():
            lanes = lax.broadcasted_iota(jnp.int32, (8, 128), 1)
            out = jnp.zeros((8, 128), F32)
            for s in range(2):
                for a, (p, e) in enumerate(HEADS[s]):
                    tot = jnp.sum(dsink_acc[s, 0:1, a * BLK:(a + 1) * BLK], axis=1, keepdims=True)
                    out = jnp.where(lanes == 2 * p + e, -tot, out)
            dsink_ref[...] = out

    rev = lambda s: nt - 1 - s
    return pl.pallas_call(
        body,
        name="attn_bwd",
        grid=(nt,),
        in_specs=_attn_specs(rev)[2:] + [pl.BlockSpec((TQ, ATTN_W), lambda s: (nt - 1 - s, 0))] * 2 + [
            pl.BlockSpec((2 * TQ, 4 * BLK), lambda s: (nt - 1 - s, 0)),
            pl.BlockSpec((2 * nb, 4 * BLK), lambda s: (nt - 1 - s, 0)),
            pl.BlockSpec((2 * nb, 4 * BLK, 128), lambda s: (nt - 1 - s, 0, 0)),
        ],
        out_specs=[
            pl.BlockSpec((TQ, PA_W), lambda s: (nt - 1 - s, 0)),
            pl.BlockSpec((8, 128), lambda s: (0, 0)),
        ],
        out_shape=[
            jax.ShapeDtypeStruct((S, PA_W), BF16),
            jax.ShapeDtypeStruct((8, 128), F32),
        ],
        scratch_shapes=[
            pltpu.VMEM((4, BLK + TQ, 128), BF16),
            pltpu.VMEM((BLK + TQ, 2 * KV_W), F32),
            pltpu.VMEM((BLK, 2 * KV_W), F32),
            pltpu.VMEM((2, 4 * BLK, 128), BF16),
            pltpu.VMEM((2, 2 * BLK, 4 * BLK), BF16),
            pltpu.VMEM((2, 2 * BLK, 4 * BLK), BF16),
            pltpu.VMEM((2, 8, 4 * BLK), F32),
        ],
        compiler_params=_params(("arbitrary",)),
    )(pa, pa, pa, tab, tab, dmix, att, probs, psinks, q_stack)


def _conv_bwd_tile(pc_ref, prev_ref, next_ref, dm_ref, dmn_ref, w_ref, d_ref, gw_ref, has_prev, has_next,
                   on_piece):
    rows = pc_ref.shape[0]
    w0, w1, w2 = w_ref[0:1, :], w_ref[1:2, :], w_ref[2:3, :]
    b, c, hh, gc, u, um1, um2, cv = _conv_tile(pc_ref, prev_ref, w_ref, has_prev)
    sg, dsg = _silu_and_grad(gc)
    dy = dm_ref[...].astype(F32)
    dyb = dy * b
    dcv = dyb * sg

    def next_dcv(r):
        nd = (dmn_ref[r:r + 1, :].astype(F32) * next_ref[r:r + 1, 0:512].astype(F32)
              * _silu(next_ref[r:r + 1, 1536:2048].astype(F32)))
        return jnp.where(has_next, nd, 0.0)

    row = lax.broadcasted_iota(jnp.int32, (rows, CONV_W), 0)
    dp1 = jnp.where(row == rows - 1, next_dcv(0), pltpu.roll(dcv, rows - 1, 0))
    dp2 = jnp.where(row == rows - 1, next_dcv(1),
                    jnp.where(row == rows - 2, next_dcv(0), pltpu.roll(dcv, rows - 2, 0)))
    du = w2 * dcv + w1 * dp1 + w0 * dp2
    pieces = (lambda: dy * cv * sg, lambda: du * hh, lambda: du * c, lambda: dyb * cv * dsg)
    for k, piece in enumerate(pieces):
        d_ref[:, k * CONV_W:(k + 1) * CONV_W] = piece().astype(BF16)
        on_piece(k)
    gw_ref[0:1, :] += jnp.sum(dcv * um2, axis=0, keepdims=True)
    gw_ref[1:2, :] += jnp.sum(dcv * um1, axis=0, keepdims=True)
    gw_ref[2:3, :] += jnp.sum(dcv * u, axis=0, keepdims=True)


def _grad_x(da, dc, wt, x, dh, norm_g, small, grads):
    S = x.shape[0]
    n_steps = S // TM
    rs = _ReduceScatter(grads)
    n_rs_out = len(rs.out_shape())
    small_rows = 8 + small.shape[0]

    def body(da_ref, dc_ref, wt_ref, x_ref, dh_ref, g_ref, small_ref, *rest):
        grad_refs, rest = rest[:rs.n], rest[rs.n:]
        gx_ref, all_ref = rest[:2]
        rs_out, rest = rest[2:2 + n_rs_out], rest[2 + n_rs_out:]
        gng, stage, small_send, small_recv, small_own = rest[:5]
        rs_scratch = rest[5:]
        step = pl.program_id(0)
        finish = rs.emit(step, n_steps, grad_refs, rs_out, rs_scratch)

        @pl.when(step == 0)
        def _():
            gng[...] = jnp.zeros_like(gng)

        dxn = (_nn(da_ref[:, 0:512], wt_ref[0:512, :]) + _nn(da_ref[:, 512:1024], wt_ref[768:1280, :])
               + _nn(da_ref[:, 1024:1280], wt_ref[512:768, :]) + _nn(dc_ref[...], wt_ref[1280:3328, :]))
        xv = x_ref[...]
        r = lax.rsqrt(jnp.mean(xv * xv, axis=-1, keepdims=True) + EPS)
        n = xv * r
        gng[...] += jnp.sum(dxn * n, axis=0, keepdims=True)
        dxg = dxn * g_ref[...]
        gx_ref[...] = dh_ref[...] + r * (dxg - n * jnp.mean(dxg * n, axis=-1, keepdims=True))

        @pl.when(step == n_steps - 1)
        def _():
            x_, y_, c_ = lax.axis_index("x"), lax.axis_index("y"), lax.axis_index("c")
            me = 4 * x_ + 2 * y_ + c_
            for q in range(8):
                stage[q:q + 1, :] = gng[:, q * 128:(q + 1) * 128]
            stage[8:small_rows, :] = small_ref[...]
            own = pltpu.make_async_copy(stage, all_ref.at[me], small_own)
            own.start()
            sends = []
            for k in range(1, N_DEV):
                cp = pltpu.make_async_remote_copy(
                    src_ref=stage, dst_ref=all_ref.at[me],
                    send_sem=small_send.at[k - 1], recv_sem=small_recv.at[k - 1],
                    device_id=(x_ ^ (k >> 2), y_ ^ ((k >> 1) & 1), c_ ^ (k & 1)), device_id_type=MESH)
                cp.start()
                sends.append(cp)
            for cp in sends:
                cp.wait_send()
                cp.wait_recv()
            own.wait()

        finish()

    row = lambda i: (i, 0)
    fixed = lambda i: (0, 0)
    any_spec = pl.BlockSpec(memory_space=pl.ANY)
    outs = pl.pallas_call(
        body,
        name="grad_x_reduce_scatter",
        grid=(n_steps,),
        in_specs=[
            pl.BlockSpec((TM, PA_W), row),
            pl.BlockSpec((TM, PC_W), row),
            pl.BlockSpec((IN_W, D_MODEL), fixed),
            pl.BlockSpec((TM, D_MODEL), row),
            pl.BlockSpec((TM, D_MODEL), row),
            pl.BlockSpec((1, D_MODEL), fixed),
            pl.BlockSpec(small.shape, fixed),
        ] + [any_spec] * rs.n,
        out_specs=[pl.BlockSpec((TM, D_MODEL), row), any_spec] + [any_spec] * n_rs_out,
        out_shape=[jax.ShapeDtypeStruct((S, D_MODEL), F32),
                   jax.ShapeDtypeStruct((N_DEV, small_rows, 128), F32)] + rs.out_shape(),
        scratch_shapes=[
            pltpu.VMEM((1, D_MODEL), F32),
            pltpu.VMEM((small_rows, 128), F32),
            pltpu.SemaphoreType.DMA((N_DEV - 1,)),
            pltpu.SemaphoreType.DMA((N_DEV - 1,)),
            pltpu.SemaphoreType.DMA,
        ] + rs.scratch_shapes(),
        compiler_params=_params(("arbitrary",)),
    )(da, dc, wt, x, dh, norm_g, small, *grads)
    return outs[0], outs[1], outs[2:2 + rs.n], outs[2 + rs.n:2 + 2 * rs.n]


def _grad_w_in(da, pc, dmix, conv_w, xn):
    S = xn.shape[0]
    tm = 2 * TM
    nt = S // tm
    t16 = tm // HALO

    def body(da_ref, pc_ref, prev_ref, next_ref, dm_ref, dmn_ref, cw_ref, xn_ref, gw_ref, dc_ref, gcw_ref):
        i = pl.program_id(0)

        @pl.when(i == 0)
        def _():
            gw_ref[...] = jnp.zeros_like(gw_ref)
            gcw_ref[...] = jnp.zeros_like(gcw_ref)

        xn = xn_ref[...]
        gw_ref[0:512, :] += _tn(da_ref[:, 0:512], xn)
        gw_ref[768:1280, :] += _tn(da_ref[:, 512:1024], xn)
        gw_ref[512:768, :] += _tn(da_ref[:, 1024:1280], xn)
        def piece_grad(k):
            rows = slice(PA_W + k * CONV_W, PA_W + (k + 1) * CONV_W)
            gw_ref[rows, :] += _tn(dc_ref[:, k * CONV_W:(k + 1) * CONV_W], xn)

        _conv_bwd_tile(pc_ref, prev_ref, next_ref, dm_ref, dmn_ref, cw_ref, dc_ref, gcw_ref, i > 0, i < nt - 1,
                       piece_grad)

    row = lambda i: (i, 0)
    fixed = lambda i: (0, 0)
    nxt = lambda i: jnp.minimum((i + 1) * t16, nt * t16 - 1)
    return pl.pallas_call(
        body,
        name="grad_w_in",
        grid=(nt,),
        in_specs=[
            pl.BlockSpec((tm, PA_W), row),
            pl.BlockSpec((tm, PC_W), row),
            pl.BlockSpec((HALO, PC_W), lambda i: (jnp.maximum(i * t16 - 1, 0), 0)),
            pl.BlockSpec((HALO, PC_W), lambda i: (nxt(i), 0)),
            pl.BlockSpec((tm, CONV_W), lambda i: (i, 1)),
            pl.BlockSpec((HALO, CONV_W), lambda i: (nxt(i), 1)),
            pl.BlockSpec((CONV_K, CONV_W), fixed),
            pl.BlockSpec((tm, D_MODEL), row),
        ],
        out_specs=[
            pl.BlockSpec((IN_W, D_MODEL), fixed, pipeline_mode=pl.Buffered(1)),
            pl.BlockSpec((tm, PC_W), row),
            pl.BlockSpec((CONV_K, CONV_W), fixed),
        ],
        out_shape=[
            jax.ShapeDtypeStruct((IN_W, D_MODEL), F32),
            jax.ShapeDtypeStruct((S, PC_W), BF16),
            jax.ShapeDtypeStruct((CONV_K, CONV_W), F32),
        ],
        compiler_params=_params(("arbitrary",)),
    )(da, pc, pc, pc, dmix, dmix, conv_w, xn)


def _adam_update(w, g, m, v):
    c1 = 1.0 - ADAM_B1 ** ADAM_STEP
    c2 = 1.0 - ADAM_B2 ** ADAM_STEP
    nm = ADAM_B1 * m + (1.0 - ADAM_B1) * g
    nv = ADAM_B2 * v + (1.0 - ADAM_B2) * (g * g)
    return -ADAM_LR * ((nm / c1) / (jnp.sqrt(nv / c2) + ADAM_EPS) + ADAM_WD * w), nm, nv


def _sum_chips_adamw(own, others, w, m, v, name):
    def body(own_ref, p_ref, w_ref, m_ref, v_ref, g_ref, d_ref, nm_ref, nv_ref):
        g = own_ref[...]
        for k in range(N_CHIP - 1):
            g = g + p_ref[k].astype(F32)
        g_ref[...] = g
        d_ref[...], nm_ref[...], nv_ref[...] = _adam_update(w_ref[...], g, m_ref[...], v_ref[...])

    rows, cols = w.shape
    half = rows // 2
    blk = pl.BlockSpec((half, cols), lambda i: (i, 0))
    shape = jax.ShapeDtypeStruct(w.shape, F32)
    return pl.pallas_call(
        body,
        name=name,
        grid=(2,),
        in_specs=[blk, pl.BlockSpec((N_CHIP - 1, half, cols), lambda i: (0, i, 0)), blk, blk, blk],
        out_specs=[blk] * 4,
        out_shape=[shape] * 4,
        compiler_params=_params(("arbitrary",)),
    )(own, others, w, m, v)


SMALL_ROWS = 96


def _small_adamw(parts, params):
    def body(parts_ref, *rest):
        prm, outs, total = rest[:12], rest[12:29], rest[29]
        me = 4 * lax.axis_index("x") + 2 * lax.axis_index("y") + lax.axis_index("c")
        acc = parts_ref[0]
        for d in range(1, N_DEV):
            acc = acc + parts_ref[d]
        total[...] = acc
        grads = (total[0:8, :], total[8:16, :], total[16:17, 0:8],
                 total[pl.ds(pl.multiple_of(32 + me * 8, 8), CONV_K), 0:64])
        outs[0][...] = total[24:25, 0:1]
        for k, g in enumerate(grads):
            w_ref, m_ref, v_ref = prm[3 * k:3 * k + 3]
            g_ref, d_ref, nm_ref, nv_ref = outs[1 + 4 * k:5 + 4 * k]
            g_ref[...] = g
            d_ref[...], nm_ref[...], nv_ref[...] = _adam_update(w_ref[...], g, m_ref[...], v_ref[...])

    flat = [a for p in params for a in p]
    out_shape = [jax.ShapeDtypeStruct((1, 1), F32)]
    for p in params:
        out_shape += [jax.ShapeDtypeStruct(p[0].shape, F32)] * 4
    return pl.pallas_call(
        body,
        name="adamw_small",
        out_shape=out_shape,
        scratch_shapes=[pltpu.VMEM((SMALL_ROWS, 128), F32)],
        compiler_params=_params(),
    )(parts, *flat)


def kernel(x, norm_g, w_in, sinks, conv_w, w_out, final_g, loss_target, m_norm_g, m_w_in, m_sinks, m_conv_w, m_w_out, m_final_g, v_norm_g, v_w_in, v_sinks, v_conv_w, v_w_out, v_final_g):
    S = x.shape[1]
    x2 = x.reshape(S, D_MODEL)
    t2 = loss_target.reshape(S, D_MODEL)
    ng = norm_g.reshape(1, D_MODEL)
    fg = final_g.reshape(1, D_MODEL)

    xn, tab, wt = _prologue(x2, ng, w_in.T)
    pa, pc, (wo, cw) = _fwd_proj(xn, wt, [w_out, conv_w])
    ya, att, probs, psinks, q_stack = _attn_fwd(pa, tab, sinks)
    dh, dmix, g_wo, g_fg, loss_part, cw3 = _out_loss(x2, t2, ya, pc, cw, wo, fg)
    da, g_sinks = _attn_bwd(pa, dmix, att, probs, psinks, q_stack, tab)
    g_wt, dc, g_cw = _grad_w_in(da, pc, dmix, cw3, xn)
    cw_pack = jnp.pad(g_cw.reshape(CONV_K, N_DEV, 64).transpose(1, 0, 2),
                      ((0, 0), (0, 8 - CONV_K), (0, 64))).reshape(N_DEV * 8, 128)
    small = jnp.concatenate([g_fg.reshape(8, 128), g_sinks, loss_part, cw_pack], axis=0)
    grad_x, parts, own, others = _grad_x(
        da, dc, wt, x2, dh, ng, small,
        [g_wt.reshape(N_DEV, SHARD_IN, D_MODEL), g_wo.reshape(N_DEV, SHARD_OUT, D_MODEL)])
    gt, dt, nmt, nvt = _sum_chips_adamw(own[0], others[0], w_in.T, m_w_in.T, v_w_in.T, "adamw_w_in")
    grad_w_in, d_w_in, nm_w_in, nv_w_in = gt.T, dt.T, nmt.T, nvt.T
    grad_w_out, d_w_out, nm_w_out, nv_w_out = _sum_chips_adamw(
        own[1], others[1], w_out, m_w_out, v_w_out, "adamw_w_out")
    vec = lambda a: a.reshape(8, 128)
    row = lambda a: a.reshape(1, 8)
    res = _small_adamw(parts, [
        (vec(norm_g), vec(m_norm_g), vec(v_norm_g)), (vec(final_g), vec(m_final_g), vec(v_final_g)),
        (row(sinks), row(m_sinks), row(v_sinks)), (conv_w, m_conv_w, v_conv_w)])
    loss = res[0].reshape(())
    grad_norm_g, d_ng, nm_ng, nv_ng = [a.reshape(D_MODEL) for a in res[1:5]]
    grad_final_g, d_fg, nm_fg, nv_fg = [a.reshape(D_MODEL) for a in res[5:9]]
    grad_sinks, d_sk, nm_sk, nv_sk = [a.reshape(N_Q_HEADS) for a in res[9:13]]
    grad_conv_w, d_cw, nm_cw, nv_cw = res[13:17]

    return (loss, grad_x.reshape(1, S, D_MODEL), grad_norm_g, grad_w_in, grad_sinks, grad_conv_w, grad_w_out, grad_final_g,
            d_ng, d_w_in, d_sk, d_cw, d_w_out, d_fg,
            nm_ng, nm_w_in, nm_sk, nm_cw, nm_w_out, nm_fg,
            nv_ng, nv_w_in, nv_sk, nv_cw, nv_w_out, nv_fg)
```

```python
import numpy as np
import jax
import jax.numpy as jnp
from jax import lax
from jax.experimental import pallas as pl
from jax.experimental.pallas import tpu as pltpu

F32 = jnp.float32
BF16 = jnp.bfloat16
MESH = pl.DeviceIdType.MESH

D_MODEL = 1024
HEAD_DIM = 64
N_Q_HEADS = 8
ATTN_W = 512
KV_W = 128
BLK = 128
CONV_W = 512
CONV_K = 3
IN_W = 3328
PA_W = 1280
PC_W = 2048
EPS = 1e-5
ROPE_THETA = 500000.0
ROT_DIM = 16
N_DEV = 8
N_CHIP = 4
SHARD_IN = IN_W // N_DEV
SHARD_OUT = D_MODEL // N_DEV

ADAM_LR = 0.001
ADAM_B1 = 0.9
ADAM_B2 = 0.999
ADAM_EPS = 1e-08
ADAM_WD = 0.01
ADAM_STEP = 10

ACT = jnp.bfloat16

TM = 512
TQ = 1024
HALO = 16
VMEM_LIMIT = 56 * 1024 * 1024

NT_DIMS = (((1,), (1,)), ((), ()))
TN_DIMS = (((0,), (0,)), ((), ()))


def _params(sem=None):
    kw = dict(vmem_limit_bytes=VMEM_LIMIT)
    if sem is not None:
        kw["dimension_semantics"] = sem
    return pltpu.CompilerParams(**kw)


def _nt(a, b):
    return lax.dot_general(a, b, NT_DIMS, preferred_element_type=F32)


def _tn(a, b):
    return lax.dot_general(a, b, TN_DIMS, preferred_element_type=F32)


def _nn(a, b):
    return jnp.dot(a, b, preferred_element_type=F32)


def _silu(g):
    return g * jax.nn.sigmoid(g)


def _silu_and_grad(g):
    s = jax.nn.sigmoid(g)
    return g * s, s * (1.0 + g * (1.0 - s))


class _AllGatherInSteps:
    def __init__(self, arrs, forward_step):
        self.blocks = [(a.shape, a.dtype) for a in arrs]
        self.n = len(arrs)
        self.forward_step = forward_step

    def out_shape(self):
        return [jax.ShapeDtypeStruct((N_DEV * s[0], s[1]), d) for s, d in self.blocks]

    def scratch_shapes(self):
        return [pltpu.SemaphoreType.DMA((7 * self.n,)), pltpu.SemaphoreType.DMA((7 * self.n,)),
                pltpu.SemaphoreType.DMA((self.n,))]

    def emit(self, step, n_steps, x_refs, out_refs, scratch):
        assert n_steps > self.forward_step + 1
        send_sems, recv_sems, local_sems = scratch
        x, y, c = lax.axis_index("x"), lax.axis_index("y"), lax.axis_index("c")
        me, sibling = (x, y, c), (x, y, 1 - c)
        chips = [(1 - x, y), (x, 1 - y), (1 - x, 1 - y)]

        def rows(a, px, py, pc):
            m = self.blocks[a][0][0]
            return out_refs[a].at[pl.ds((4 * px + 2 * py + pc) * m, m), :]

        def copy(a, k, block, to, src=None):
            return pltpu.make_async_remote_copy(
                src_ref=rows(a, *block) if src is None else src, dst_ref=rows(a, *block),
                send_sem=send_sems.at[a * 7 + k], recv_sem=recv_sems.at[a * 7 + k],
                device_id=to, device_id_type=MESH)

        def mine(a):
            return pltpu.make_async_copy(x_refs[a], rows(a, *me), local_sems.at[a])

        def first(a):
            return ([copy(a, 0, me, sibling, src=x_refs[a])]
                    + [copy(a, 1 + j, me, (*chip, c), src=x_refs[a]) for j, chip in enumerate(chips)])

        def passed(a):
            return [copy(a, 4 + j, (*chip, c), sibling) for j, chip in enumerate(chips)]

        @pl.when(step == 0)
        def _():
            for a in range(self.n):
                mine(a).start()
                for cp in first(a):
                    cp.start()

        @pl.when(step == self.forward_step)
        def _():
            for j, chip in enumerate(chips):
                for a in range(self.n):
                    copy(a, 1 + j, (*chip, c), me).wait_recv()
                    copy(a, 4 + j, (*chip, c), sibling).start()

        def finish():
            @pl.when(step == n_steps - 1)
            def _():
                for a in range(self.n):
                    copy(a, 0, sibling, me).wait_recv()
                    for j, chip in enumerate(chips):
                        copy(a, 4 + j, (*chip, 1 - c), me).wait_recv()
                    for cp in first(a) + passed(a):
                        cp.wait_send()
                    mine(a).wait()

        return finish


class _AllGatherViaNeighbours:
    def __init__(self, arr, first, mid, second):
        (self.m, self.ncol), self.dtype = arr.shape, arr.dtype
        assert self.m % 32 == 0
        self.first, self.mid, self.second = first, mid, second

    def out_shape(self):
        return [jax.ShapeDtypeStruct((N_DEV * self.m, self.ncol), self.dtype)]

    def scratch_shapes(self):
        return [pltpu.SemaphoreType.DMA((11,)), pltpu.SemaphoreType.DMA((11,)), pltpu.SemaphoreType.DMA]

    def emit(self, step, n_steps, x_ref, out_ref, scratch):
        assert 0 < self.first < self.mid < self.second < n_steps - 1
        send_sems, recv_sems, local_sem = scratch
        x, y, c = lax.axis_index("x"), lax.axis_index("y"), lax.axis_index("c")
        half = self.m // 2
        sibling, xn, yn = (x, y, 1 - c), (1 - x, y, c), (x, 1 - y, c)

        def rows(dev, part=None):
            px, py, pc = dev
            base = (4 * px + 2 * py + pc) * self.m
            if part is None:
                return out_ref.at[pl.ds(base, self.m), :]
            return out_ref.at[pl.ds(base + part * half, half), :]

        def copy(k, dev, to, part=None, src=None):
            return pltpu.make_async_remote_copy(
                src_ref=rows(dev, part) if src is None else src, dst_ref=rows(dev, part),
                send_sem=send_sems.at[k], recv_sem=recv_sems.at[k], device_id=to, device_id_type=MESH)

        me, dg = (x, y, c), (1 - x, 1 - y, c)
        mine = pltpu.make_async_copy(x_ref, rows(me), local_sem)
        my_half = lambda part: x_ref.at[pl.ds(part * half, half), :]
        sends = [
            copy(0, me, sibling, src=x_ref), copy(1, me, xn, part=0, src=my_half(0)),
            copy(2, me, yn, part=1, src=my_half(1)), copy(3, xn, yn, part=0), copy(4, yn, xn, part=1),
            copy(5, xn, sibling), copy(6, yn, sibling), copy(7, dg, sibling, part=0), copy(8, dg, sibling, part=1),
            copy(9, me, xn, part=1, src=my_half(1)), copy(10, me, yn, part=0, src=my_half(0)),
        ]
        other = lambda dev: (dev[0], dev[1], 1 - c)
        arrivals = [
            copy(0, other(me), sibling), copy(1, xn, xn, part=0), copy(2, yn, yn, part=1), copy(3, dg, yn, part=0),
            copy(4, dg, xn, part=1), copy(5, other(xn), sibling), copy(6, other(yn), sibling),
            copy(7, other(dg), sibling, part=0), copy(8, other(dg), sibling, part=1),
            copy(9, xn, xn, part=1), copy(10, yn, yn, part=0),
        ]

        @pl.when(step == 0)
        def _():
            mine.start()
            for k in (0, 1, 2, 9, 10):
                sends[k].start()

        @pl.when(step == self.first)
        def _():
            arrivals[1].wait_recv()
            sends[3].start()
            arrivals[2].wait_recv()
            sends[4].start()

        @pl.when(step == self.mid)
        def _():
            arrivals[9].wait_recv()
            sends[5].start()
            arrivals[10].wait_recv()
            sends[6].start()

        @pl.when(step == self.second)
        def _():
            arrivals[3].wait_recv()
            sends[7].start()
            arrivals[4].wait_recv()
            sends[8].start()

        def finish():
            @pl.when(step == n_steps - 1)
            def _():
                for k in (0, 5, 6, 7, 8):
                    arrivals[k].wait_recv()
                for cp in sends:
                    cp.wait_send()
                mine.wait()

        return finish


class _ReduceScatter:
    def __init__(self, grads):
        self.shapes = [g.shape[1:] for g in grads]
        self.n = len(grads)
        self.items = tuple((a, r) for r in (1, 2, 3, 0) for a in range(self.n))
        self.steps = N_CHIP + 2

    def out_shape(self):
        own = [jax.ShapeDtypeStruct(s, F32) for s in self.shapes]
        ici = [jax.ShapeDtypeStruct((N_CHIP - 1,) + s, BF16) for s in self.shapes]
        land = [jax.ShapeDtypeStruct((N_CHIP,) + s, F32) for s in self.shapes]
        return own + ici + land

    def scratch_shapes(self):
        n_items = len(self.items)
        return ([pltpu.VMEM((2,) + s, F32) for s in self.shapes]
                + [pltpu.VMEM((N_CHIP - 1,) + s, BF16) for s in self.shapes]
                + [pltpu.VMEM(s, F32) for s in self.shapes]
                + [pltpu.SemaphoreType.DMA((self.n * N_CHIP,))] * 2
                + [pltpu.SemaphoreType.DMA((2 * n_items,))]
                + [pltpu.SemaphoreType.DMA((self.n * (N_CHIP - 1),))] * 2
                + [pltpu.SemaphoreType.DMA((self.n,))])

    def emit(self, step, n_steps, g_refs, out_refs, scratch):
        assert n_steps > self.steps
        n = self.n
        own_refs, ici_refs, land_refs = out_refs[:n], out_refs[n:2 * n], out_refs[2 * n:]
        stage, pair_bf, pair_own = scratch[:n], scratch[n:2 * n], scratch[2 * n:3 * n]
        sib_send, sib_recv, load_sems, ici_send, ici_recv, own_sems = scratch[3 * n:]
        x, y, c = lax.axis_index("x"), lax.axis_index("y"), lax.axis_index("c")

        def chip_of(r):
            return (x ^ (r >> 1), y ^ (r & 1))

        def block_of(r, core):
            cx, cy = chip_of(r)
            return 4 * cx + 2 * cy + core

        def to_sibling(a, r):
            return pltpu.make_async_remote_copy(
                src_ref=g_refs[a].at[block_of(r, 1 - c)], dst_ref=land_refs[a].at[r],
                send_sem=sib_send.at[a * N_CHIP + r], recv_sem=sib_recv.at[a * N_CHIP + r],
                device_id=(x, y, 1 - c), device_id_type=MESH)

        def loads(k):
            a, r = self.items[k]
            return (pltpu.make_async_copy(g_refs[a].at[block_of(r, c)], stage[a].at[0], load_sems.at[2 * k]),
                    pltpu.make_async_copy(land_refs[a].at[r], stage[a].at[1], load_sems.at[2 * k + 1]))

        def to_owner(k):
            a, r = self.items[k]
            if r == 0:
                return pltpu.make_async_copy(pair_own[a], own_refs[a], own_sems.at[a])
            return pltpu.make_async_remote_copy(
                src_ref=pair_bf[a].at[r - 1], dst_ref=ici_refs[a].at[r - 1],
                send_sem=ici_send.at[a * (N_CHIP - 1) + r - 1], recv_sem=ici_recv.at[a * (N_CHIP - 1) + r - 1],
                device_id=(*chip_of(r), c), device_id_type=MESH)

        @pl.when(step == 0)
        def _():
            for a, r in self.items:
                to_sibling(a, r).start()

        def fetch(k):
            a, r = self.items[k]
            to_sibling(a, r).wait_recv()
            for cp in loads(k):
                cp.start()

        def add_and_send(k):
            a, r = self.items[k]
            for cp in loads(k):
                cp.wait()
            total = stage[a][0] + stage[a][1]
            if r == 0:
                pair_own[a][...] = total
            else:
                pair_bf[a][r - 1] = total.astype(BF16)
            to_owner(k).start()

        for g in range(N_CHIP + 1):
            @pl.when(step == 1 + g)
            def _(g=g):
                if g > 0:
                    for k in range((g - 1) * n, g * n):
                        add_and_send(k)
                if g < N_CHIP:
                    for k in range(g * n, (g + 1) * n):
                        fetch(k)

        def finish():
            @pl.when(step == n_steps - 1)
            def _():
                for k, (a, r) in enumerate(self.items):
                    if r == 0:
                        to_owner(k).wait()
                    else:
                        to_owner(k).wait_send()
                        to_owner(k).wait_recv()
                for a, r in self.items:
                    to_sibling(a, r).wait_send()

        return finish


def _prologue(x, norm_g, w_shard):
    S = x.shape[0]
    n_steps = S // TM
    half = ROT_DIM // 2
    pos = np.arange(S, dtype=np.float32)
    inv_freq = np.float32(ROPE_THETA) ** (-np.arange(0, ROT_DIM, 2, dtype=np.float32) / np.float32(ROT_DIM))
    ang = inv_freq.astype(np.float32)[:, None] * pos[None, :]
    cs = jnp.asarray(np.concatenate([np.cos(ang), np.sin(ang)], axis=0).astype(np.float32))
    ag = _AllGatherViaNeighbours(jax.ShapeDtypeStruct(w_shard.shape, BF16),
                                 first=n_steps // 4, mid=n_steps // 2 + 2, second=n_steps - 2)

    def body(x_ref, g_ref, cs_ref, w_ref, xn_ref, tab_ref, wt_ref, w_bf, *ag_scratch):
        step = pl.program_id(0)

        @pl.when(step == 0)
        def _():
            w_bf[...] = w_ref[...].astype(BF16)

        finish = ag.emit(step, n_steps, w_bf, wt_ref, ag_scratch)
        xv = x_ref[...]
        r = lax.rsqrt(jnp.mean(xv * xv, axis=-1, keepdims=True) + EPS)
        xn_ref[...] = (xv * r * g_ref[...]).astype(BF16)

        xt = jnp.concatenate([cs_ref[...], jnp.zeros((128 - 2 * half, TM), F32)], axis=0).T
        lane = lax.broadcasted_iota(jnp.int32, (TM, 128), 1)
        rr = lane & (HEAD_DIM - 1)
        first = lane < HEAD_DIM

        def at(shift_first, shift_second):
            return jnp.where(first, pltpu.roll(xt, shift_first, 1) if shift_first else xt,
                             pltpu.roll(xt, shift_second, 1))

        cos_lo, cos_hi = at(0, HEAD_DIM), at(half, HEAD_DIM + half)
        sin_lo, sin_hi = at(128 - half, HEAD_DIM - half), at(0, HEAD_DIM)
        tab_ref[:, 0:128] = jnp.where(rr < half, cos_lo, jnp.where(rr < ROT_DIM, cos_hi, 1.0))
        tab_ref[:, 128:256] = jnp.where(rr < half, -sin_lo, 0.0)
        tab_ref[:, 256:384] = jnp.where((rr >= half) & (rr < ROT_DIM), sin_hi, 0.0)
        finish()

    any_spec = pl.BlockSpec(memory_space=pl.ANY)
    return pl.pallas_call(
        body,
        name="prologue_all_gather_w_in",
        grid=(n_steps,),
        in_specs=[
            pl.BlockSpec((TM, D_MODEL), lambda i: (i, 0)),
            pl.BlockSpec((1, D_MODEL), lambda i: (0, 0)),
            pl.BlockSpec((2 * half, TM), lambda i: (0, i)),
            pl.BlockSpec(w_shard.shape, lambda i: (0, 0)),
        ],
        out_specs=[
            pl.BlockSpec((TM, D_MODEL), lambda i: (i, 0)),
            pl.BlockSpec((TM, 384), lambda i: (i, 0)),
            any_spec,
        ],
        out_shape=[
            jax.ShapeDtypeStruct((S, D_MODEL), BF16),
            jax.ShapeDtypeStruct((S, 384), F32),
        ] + ag.out_shape(),
        scratch_shapes=[pltpu.VMEM(w_shard.shape, BF16)] + ag.scratch_shapes(),
        compiler_params=_params(("arbitrary",)),
    )(x, norm_g, cs, w_shard)


def _fwd_proj(xn, wt, later):
    S = xn.shape[0]
    tm = 2 * TM
    n_steps = S // tm
    ag = _AllGatherInSteps([jax.ShapeDtypeStruct(later[0].shape, BF16), jax.ShapeDtypeStruct((8, 128), F32)],
                           forward_step=n_steps // 2)

    def body(xn_ref, wt_ref, *rest):
        later_refs, rest = rest[:ag.n], rest[ag.n:]
        pa_ref, pc_ref = rest[:2]
        gathered, w_bf, cw_pad, ag_scratch = rest[2:2 + ag.n], rest[2 + ag.n], rest[3 + ag.n], rest[4 + ag.n:]
        step = pl.program_id(0)

        @pl.when(step == 0)
        def _():
            w_bf[...] = later_refs[0][...].astype(BF16)
            cw_pad[...] = jnp.zeros_like(cw_pad)
            cw_pad[0:CONV_K, 0:64] = later_refs[1][...]

        finish = ag.emit(step, n_steps, (w_bf, cw_pad), gathered, ag_scratch)
        xn = xn_ref[...]
        pa_ref[:, 0:512] = _nt(xn, wt_ref[0:512, :]).astype(ACT)
        pa_ref[:, 512:1024] = _nt(xn, wt_ref[768:1280, :]).astype(ACT)
        pa_ref[:, 1024:1280] = _nt(xn, wt_ref[512:768, :]).astype(ACT)
        pc_ref[...] = _nt(xn, wt_ref[1280:3328, :]).astype(ACT)
        finish()

    any_spec = pl.BlockSpec(memory_space=pl.ANY)
    outs = pl.pallas_call(
        body,
        name="fwd_proj_all_gather",
        grid=(n_steps,),
        in_specs=[
            pl.BlockSpec((tm, D_MODEL), lambda i: (i, 0)),
            pl.BlockSpec((IN_W, D_MODEL), lambda i: (0, 0)),
            pl.BlockSpec(later[0].shape, lambda i: (0, 0)),
            pl.BlockSpec(later[1].shape, lambda i: (0, 0)),
        ],
        out_specs=[
            pl.BlockSpec((tm, PA_W), lambda i: (i, 0)),
            pl.BlockSpec((tm, PC_W), lambda i: (i, 0)),
        ] + [any_spec] * ag.n,
        out_shape=[
            jax.ShapeDtypeStruct((S, PA_W), ACT),
            jax.ShapeDtypeStruct((S, PC_W), ACT),
        ] + ag.out_shape(),
        scratch_shapes=[pltpu.VMEM(later[0].shape, BF16), pltpu.VMEM((8, 128), F32)] + ag.scratch_shapes(),
        compiler_params=_params(("arbitrary",)),
    )(xn, wt, *later)
    return outs[0], outs[1], outs[2:]


def _rope(t, tab):
    return (t * tab[:, 0:128] + pltpu.roll(t, 120, 1) * tab[:, 128:256]
            + pltpu.roll(t, 8, 1) * tab[:, 256:384])


def _rope_t(d, tab):
    return (d * tab[:, 0:128] + pltpu.roll(d * tab[:, 128:256], 8, 1)
            + pltpu.roll(d * tab[:, 256:384], 120, 1))


def _fill_kv(kall, kvc_ref, kvp_ref, tabc_ref, tabp_ref):
    for lo, kv_ref, tab_ref, n in ((0, kvp_ref, tabp_ref, BLK), (BLK, kvc_ref, tabc_ref, TQ)):
        k = _rope(kv_ref[:, 0:128].astype(F32), tab_ref[...])
        v = kv_ref[:, 128:256].astype(F32)
        kall[0, lo:lo + n, :] = k.astype(BF16)
        kall[1, lo:lo + n, :] = pltpu.roll(k, 64, 1).astype(BF16)
        kall[2, lo:lo + n, :] = v.astype(BF16)
        kall[3, lo:lo + n, :] = pltpu.roll(v, 64, 1).astype(BF16)


HEADS = (((0, 0), (1, 0), (2, 1), (3, 1)), ((0, 1), (1, 1), (2, 0), (3, 0)))


def _upper():
    kj = lax.broadcasted_iota(jnp.int32, (BLK, 4 * BLK), 0)
    qi = lax.broadcasted_iota(jnp.int32, (BLK, 4 * BLK), 1) & (BLK - 1)
    return kj > qi


def _merge(upper, both):
    return jnp.where(upper, both[0:BLK, :], both[BLK:2 * BLK, :])


def _split_store(ref, s, upper_b, vb):
    first = vb * upper_b
    ref[s, 0:BLK, :] = first
    ref[s, BLK:2 * BLK, :] = vb - first


def _sink_rows(sink_ref):
    return [jnp.concatenate([jnp.full((1, BLK), sink_ref[2 * p + e], F32) for p, e in HEADS[s]], axis=1)
            for s in range(2)]


def _stack_heads(ref, slot, half, pairs, s=None):
    for a, (p, e) in enumerate(HEADS[slot if s is None else s]):
        ref[slot, a * BLK:(a + 1) * BLK, :] = jnp.where(half[e], pairs[p], 0.0).astype(BF16)


def _unstack_pair(half, outs, p):
    lo = 0 if p < 2 else 1
    rows = slice(p * BLK, (p + 1) * BLK)
    return jnp.where(half[0], outs[lo][rows, :], outs[1 - lo][rows, :])


def _softmax(sm, sinks):
    m = jnp.maximum(jnp.max(sm, axis=0, keepdims=True), sinks)
    p = jnp.exp(sm - m)
    es = jnp.exp(sinks - m)
    inv = 1.0 / (jnp.sum(p, axis=0, keepdims=True) + es)
    return p * inv, es * inv


def _scores(kk, q_stack, first):
    st = _nt(kk, q_stack)
    prev = st[0:BLK, :]
    if first is not None:
        prev = prev + jnp.where(first, -jnp.inf, 0.0)
    return prev, st[BLK:2 * BLK, :]


def _attn_specs(tile):
    nb = TQ // BLK
    prev = lambda i: jnp.maximum(tile(i) * nb - 1, 0)
    return [
        pl.BlockSpec(memory_space=pltpu.SMEM),
        pl.BlockSpec((TQ, ATTN_W), lambda i: (tile(i), 0)),
        pl.BlockSpec((TQ, ATTN_W), lambda i: (tile(i), 1)),
        pl.BlockSpec((TQ, 2 * KV_W), lambda i: (tile(i), 4)),
        pl.BlockSpec((BLK, 2 * KV_W), lambda i: (prev(i), 4)),
        pl.BlockSpec((TQ, 384), lambda i: (tile(i), 0)),
        pl.BlockSpec((BLK, 384), lambda i: (prev(i), 0)),
    ]


def _attn_fwd(pa, tab, sinks):
    S = pa.shape[0]
    nb = TQ // BLK

    def body(sink_ref, q_ref, g_ref, kvc_ref, kvp_ref, tabc_ref, tabp_ref, o_ref, att_ref, pm_ref, ps_ref,
             qs_ref, kall, p_sc):
        i = pl.program_id(0)
        _fill_kv(kall, kvc_ref, kvp_ref, tabc_ref, tabp_ref)
        lane = lax.broadcasted_iota(jnp.int32, (BLK, 128), 1)
        half = [lane < HEAD_DIM, lane >= HEAD_DIM]
        upper = _upper()
        upper_b = upper.astype(BF16)
        sinks = _sink_rows(sink_ref)
        for j in range(nb):
            rq = slice(j * BLK, (j + 1) * BLK)
            rk = slice(j * BLK, (j + 2) * BLK)
            tab = tabc_ref[rq, :]
            qr = [_rope(q_ref[rq, p * 128:(p + 1) * 128].astype(F32), tab) * 0.125 for p in range(4)]
            outs = []
            for s in range(2):
                _stack_heads(qs_ref, 2 * j + s, half, qr, s)
                prev, cur = _scores(kall[s, rk, :], qs_ref[2 * j + s], i == 0 if j == 0 else None)
                prob, psink = _softmax(jnp.where(upper, prev, cur), sinks[s])
                pb = prob.astype(BF16)
                pm_ref[(2 * j + s) * BLK:(2 * j + s + 1) * BLK, :] = pb
                ps_ref[2 * j + s:2 * j + s + 1, :] = psink
                _split_store(p_sc, s, upper_b, pb)
                outs.append(_tn(p_sc[s], kall[2 + s, rk, :]))
            for p in range(4):
                cols = slice(p * 128, (p + 1) * 128)
                att = _unstack_pair(half, outs, p)
                att_ref[rq, cols] = att.astype(BF16)
                o_ref[rq, cols] = (att * _silu(g_ref[rq, cols].astype(F32))).astype(BF16)

    return pl.pallas_call(
        body,
        name="attn_fwd",
        grid=(S // TQ,),
        in_specs=_attn_specs(lambda i: i),
        out_specs=[pl.BlockSpec((TQ, ATTN_W), lambda i: (i, 0))] * 2 + [
            pl.BlockSpec((2 * TQ, 4 * BLK), lambda i: (i, 0)),
            pl.BlockSpec((2 * nb, 4 * BLK), lambda i: (i, 0)),
            pl.BlockSpec((2 * nb, 4 * BLK, 128), lambda i: (i, 0, 0)),
        ],
        out_shape=[jax.ShapeDtypeStruct((S, ATTN_W), BF16)] * 2 + [
            jax.ShapeDtypeStruct((2 * S, 4 * BLK), BF16),
            jax.ShapeDtypeStruct((2 * S // BLK, 4 * BLK), F32),
            jax.ShapeDtypeStruct((2 * S // BLK, 4 * BLK, 128), BF16),
        ],
        scratch_shapes=[
            pltpu.VMEM((4, BLK + TQ, 128), BF16),
            pltpu.VMEM((2, 2 * BLK, 4 * BLK), BF16),
        ],
        compiler_params=_params(("arbitrary",)),
    )(sinks, pa, pa, pa, pa, tab, tab)


def _shift_down(u, halo_ref, has_prev):
    def halo_u(r):
        hu = halo_ref[r:r + 1, 512:1024].astype(F32) * halo_ref[r:r + 1, 1024:1536].astype(F32)
        return jnp.where(has_prev, hu, 0.0)

    row = lax.broadcasted_iota(jnp.int32, u.shape, 0)
    um1 = jnp.where(row == 0, halo_u(HALO - 1), pltpu.roll(u, 1, 0))
    um2 = jnp.where(row == 0, halo_u(HALO - 2), jnp.where(row == 1, halo_u(HALO - 1), pltpu.roll(u, 2, 0)))
    return um1, um2


def _gathered_conv_w(all_ref):
    pairs = [all_ref[16 * p:16 * p + 8, :] + pltpu.roll(all_ref[16 * p + 8:16 * p + 16, :], 64, 1)
             for p in range(N_DEV // 2)]
    return jnp.concatenate(pairs, axis=1)


def _conv_tile(pc_ref, halo_ref, w_ref, has_prev):
    b = pc_ref[:, 0:512].astype(F32)
    c = pc_ref[:, 512:1024].astype(F32)
    hh = pc_ref[:, 1024:1536].astype(F32)
    gc = pc_ref[:, 1536:2048].astype(F32)
    u = c * hh
    um1, um2 = _shift_down(u, halo_ref, has_prev)
    cv = w_ref[0:1, :] * um2 + w_ref[1:2, :] * um1 + w_ref[2:3, :] * u
    return b, c, hh, gc, u, um1, um2, cv


def _prev_rows(width, col=0):
    return pl.BlockSpec((HALO, width), lambda i: (jnp.maximum(i * (TM // HALO) - 1, 0), col))


def _out_loss(x, target, ya, pc, conv_w, w_out, final_g):
    S = x.shape[0]

    def body(x_ref, t_ref, ya_ref, pc_ref, halo_ref, cw_ref, wo_ref, fg_ref,
             dh_ref, dmix_ref, gwo_ref, gfg_ref, loss_ref, cw_out_ref):
        cw = _gathered_conv_w(cw_ref)

        @pl.when(pl.program_id(0) == 0)
        def _():
            gwo_ref[...] = jnp.zeros_like(gwo_ref)
            gfg_ref[...] = jnp.zeros_like(gfg_ref)
            loss_ref[...] = jnp.zeros_like(loss_ref)
            cw_out_ref[...] = cw[0:CONV_K, :]

        b, _, _, gc, _, _, _, cv = _conv_tile(pc_ref, halo_ref, cw, pl.program_id(0) > 0)
        yc = (b * cv * _silu(gc)).astype(BF16)
        mix = jnp.concatenate([ya_ref[...], yc], axis=1)
        wo = wo_ref[...]
        fg = fg_ref[...]
        h = x_ref[...] + _nn(mix, wo)
        r = lax.rsqrt(jnp.mean(h * h, axis=-1, keepdims=True) + EPS)
        n = h * r
        err = n * fg - t_ref[...]
        loss_ref[...] += jnp.broadcast_to(
            0.5 * jnp.sum(jnp.mean(err * err, axis=-1, keepdims=True), axis=0, keepdims=True), (8, 128))
        gfg_ref[...] += jnp.sum(err * n, axis=0, keepdims=True) * (1.0 / D_MODEL)
        dyg = err * (fg * (1.0 / D_MODEL))
        dh = r * (dyg - n * jnp.mean(dyg * n, axis=-1, keepdims=True))
        dh_ref[...] = dh
        dhb = dh.astype(BF16)
        dmix_ref[...] = _nt(dhb, wo).astype(ACT)
        gwo_ref[...] += _tn(mix, dhb)

    row = lambda i: (i, 0)
    fixed = lambda i: (0, 0)
    return pl.pallas_call(
        body,
        name="out_loss",
        grid=(S // TM,),
        in_specs=[
            pl.BlockSpec((TM, D_MODEL), row),
            pl.BlockSpec((TM, D_MODEL), row),
            pl.BlockSpec((TM, ATTN_W), row),
            pl.BlockSpec((TM, PC_W), row),
            _prev_rows(PC_W),
            pl.BlockSpec((N_DEV * 8, 128), fixed),
            pl.BlockSpec((D_MODEL, D_MODEL), fixed),
            pl.BlockSpec((1, D_MODEL), fixed),
        ],
        out_specs=[
            pl.BlockSpec((TM, D_MODEL), row),
            pl.BlockSpec((TM, D_MODEL), row),
            pl.BlockSpec((D_MODEL, D_MODEL), fixed),
            pl.BlockSpec((1, D_MODEL), fixed),
            pl.BlockSpec((8, 128), fixed),
            pl.BlockSpec((CONV_K, CONV_W), fixed),
        ],
        out_shape=[
            jax.ShapeDtypeStruct((S, D_MODEL), F32),
            jax.ShapeDtypeStruct((S, D_MODEL), ACT),
            jax.ShapeDtypeStruct((D_MODEL, D_MODEL), F32),
            jax.ShapeDtypeStruct((1, D_MODEL), F32),
            jax.ShapeDtypeStruct((8, 128), F32),
            jax.ShapeDtypeStruct((CONV_K, CONV_W), F32),
        ],
        compiler_params=_params(("arbitrary",)),
    )(x, target, ya, pc, pc, conv_w, w_out, final_g)


def _attn_bwd(pa, dmix, att, probs, psinks, q_stack, tab):
    S = pa.shape[0]
    nt = S // TQ
    nb = TQ // BLK

    def body(g_ref, kvc_ref, kvp_ref, tabc_ref, tabp_ref, dm_ref, att_ref, pm_ref, ps_ref, qs_ref,
             d_ref, dsink_ref, kall, dkv, carry, do_sc, p_sc, ds_sc, dsink_acc):
        step = pl.program_id(0)

        @pl.when(step == 0)
        def _():
            carry[...] = jnp.zeros_like(carry)
            dsink_acc[...] = jnp.zeros_like(dsink_acc)

        _fill_kv(kall, kvc_ref, kvp_ref, tabc_ref, tabp_ref)
        dkv[0:TQ, :] = jnp.zeros((TQ, 2 * KV_W), F32)
        dkv[TQ:TQ + BLK, :] = carry[...]
        lane = lax.broadcasted_iota(jnp.int32, (BLK, 128), 1)
        half = [lane < HEAD_DIM, lane >= HEAD_DIM]
        upper = _upper()
        upper_b = upper.astype(BF16)
        for j in range(nb):
            rq = slice(j * BLK, (j + 1) * BLK)
            rk = slice(j * BLK, (j + 2) * BLK)
            tab = tabc_ref[rq, :]
            pair = [slice(p * 128, (p + 1) * 128) for p in range(4)]
            g = [g_ref[rq, c].astype(F32) for c in pair]
            da = [dm_ref[rq, c].astype(F32) for c in pair]
            gate = [_silu_and_grad(g[p]) for p in range(4)]
            do = [da[p] * gate[p][0] for p in range(4)]
            dqs, dks, dvs = [], [], []
            for s in range(2):
                kk = kall[s, rk, :]
                vv = kall[2 + s, rk, :]
                _stack_heads(do_sc, s, half, do)
                pb = pm_ref[(2 * j + s) * BLK:(2 * j + s + 1) * BLK, :]
                prob = pb.astype(F32)
                _split_store(p_sc, s, upper_b, pb)
                dprob = _merge(upper, _nt(vv, do_sc[s]))
                dsum = jnp.sum(dprob * prob, axis=0, keepdims=True)
                _split_store(ds_sc, s, upper_b, (prob * (dprob - dsum)).astype(BF16))
                dsink_acc[s, 0:1, :] += ps_ref[2 * j + s:2 * j + s + 1, :] * dsum
                dqs.append(_tn(ds_sc[s], kk))
                dks.append(_nn(ds_sc[s], qs_ref[2 * j + s]))
                dvs.append(_nn(p_sc[s], do_sc[s]))
            for p in range(4):
                d_ref[rq, pair[p]] = _rope_t(_unstack_pair(half, dqs, p) * 0.125, tab).astype(BF16)
                d_ref[rq, 512 + p * 128:512 + (p + 1) * 128] = (
                    da[p] * att_ref[rq, pair[p]].astype(F32) * gate[p][1]).astype(BF16)
            dkv[rk, 0:128] += dks[0] + pltpu.roll(dks[1], 64, 1)
            dkv[rk, 128:256] += dvs[0] + pltpu.roll(dvs[1], 64, 1)
        d_ref[:, 1024:1152] = _rope_t(dkv[BLK:BLK + TQ, 0:128], tabc_ref[...]).astype(BF16)
        d_ref[:, 1152:1280] = dkv[BLK:BLK + TQ, 128:256].astype(BF16)
        carry[...] = dkv[0:BLK, :]

        @pl.when(step == nt - 1)
        def _():
            lanes = lax.broadcasted_iota(jnp.int32, (8, 128), 1)
            out = jnp.zeros((8, 128), F32)
            for s in range(2):
                for a, (p, e) in enumerate(HEADS[s]):
                    tot = jnp.sum(dsink_acc[s, 0:1, a * BLK:(a + 1) * BLK], axis=1, keepdims=True)
                    out = jnp.where(lanes == 2 * p + e, -tot, out)
            dsink_ref[...] = out

    rev = lambda s: nt - 1 - s
    return pl.pallas_call(
        body,
        name="attn_bwd",
        grid=(nt,),
        in_specs=_attn_specs(rev)[2:] + [pl.BlockSpec((TQ, ATTN_W), lambda s: (nt - 1 - s, 0))] * 2 + [
            pl.BlockSpec((2 * TQ, 4 * BLK), lambda s: (nt - 1 - s, 0)),
            pl.BlockSpec((2 * nb, 4 * BLK), lambda s: (nt - 1 - s, 0)),
            pl.BlockSpec((2 * nb, 4 * BLK, 128), lambda s: (nt - 1 - s, 0, 0)),
        ],
        out_specs=[
            pl.BlockSpec((TQ, PA_W), lambda s: (nt - 1 - s, 0)),
            pl.BlockSpec((8, 128), lambda s: (0, 0)),
        ],
        out_shape=[
            jax.ShapeDtypeStruct((S, PA_W), BF16),
            jax.ShapeDtypeStruct((8, 128), F32),
        ],
        scratch_shapes=[
            pltpu.VMEM((4, BLK + TQ, 128), BF16),
            pltpu.VMEM((BLK + TQ, 2 * KV_W), F32),
            pltpu.VMEM((BLK, 2 * KV_W), F32),
            pltpu.VMEM((2, 4 * BLK, 128), BF16),
            pltpu.VMEM((2, 2 * BLK, 4 * BLK), BF16),
            pltpu.VMEM((2, 2 * BLK, 4 * BLK), BF16),
            pltpu.VMEM((2, 8, 4 * BLK), F32),
        ],
        compiler_params=_params(("arbitrary",)),
    )(pa, pa, pa, tab, tab, dmix, att, probs, psinks, q_stack)


def _conv_bwd_tile(pc_ref, prev_ref, next_ref, dm_ref, dmn_ref, w_ref, d_ref, gw_ref, has_prev, has_next,
                   on_piece):
    rows = pc_ref.shape[0]
    w0, w1, w2 = w_ref[0:1, :], w_ref[1:2, :], w_ref[2:3, :]
    b, c, hh, gc, u, um1, um2, cv = _conv_tile(pc_ref, prev_ref, w_ref, has_prev)
    sg, dsg = _silu_and_grad(gc)
    dy = dm_ref[...].astype(F32)
    dyb = dy * b
    dcv = dyb * sg

    def next_dcv(r):
        nd = (dmn_ref[r:r + 1, :].astype(F32) * next_ref[r:r + 1, 0:512].astype(F32)
              * _silu(next_ref[r:r + 1, 1536:2048].astype(F32)))
        return jnp.where(has_next, nd, 0.0)

    row = lax.broadcasted_iota(jnp.int32, (rows, CONV_W), 0)
    dp1 = jnp.where(row == rows - 1, next_dcv(0), pltpu.roll(dcv, rows - 1, 0))
    dp2 = jnp.where(row == rows - 1, next_dcv(1),
                    jnp.where(row == rows - 2, next_dcv(0), pltpu.roll(dcv, rows - 2, 0)))
    du = w2 * dcv + w1 * dp1 + w0 * dp2
    pieces = (lambda: dy * cv * sg, lambda: du * hh, lambda: du * c, lambda: dyb * cv * dsg)
    for k, piece in enumerate(pieces):
        d_ref[:, k * CONV_W:(k + 1) * CONV_W] = piece().astype(BF16)
        on_piece(k)
    gw_ref[0:1, :] += jnp.sum(dcv * um2, axis=0, keepdims=True)
    gw_ref[1:2, :] += jnp.sum(dcv * um1, axis=0, keepdims=True)
    gw_ref[2:3, :] += jnp.sum(dcv * u, axis=0, keepdims=True)


def _grad_x(da, dc, wt, x, dh, norm_g, small, grads):
    S = x.shape[0]
    n_steps = S // TM
    rs = _ReduceScatter(grads)
    n_rs_out = len(rs.out_shape())
    small_rows = SMALL_ROWS

    def body(da_ref, dc_ref, wt_ref, x_ref, dh_ref, g_ref, gfg_ref, gsink_ref, loss_ref, gcw_ref, *rest):
        grad_refs, rest = rest[:rs.n], rest[rs.n:]
        gx_ref, all_ref = rest[:2]
        rs_out, rest = rest[2:2 + n_rs_out], rest[2 + n_rs_out:]
        gng, stage, small_send, small_recv, small_own = rest[:5]
        rs_scratch = rest[5:]
        step = pl.program_id(0)
        finish = rs.emit(step, n_steps, grad_refs, rs_out, rs_scratch)

        @pl.when(step == 0)
        def _():
            gng[...] = jnp.zeros_like(gng)

        dxn = (_nn(da_ref[:, 0:512], wt_ref[0:512, :]) + _nn(da_ref[:, 512:1024], wt_ref[768:1280, :])
               + _nn(da_ref[:, 1024:1280], wt_ref[512:768, :]) + _nn(dc_ref[...], wt_ref[1280:3328, :]))
        xv = x_ref[...]
        r = lax.rsqrt(jnp.mean(xv * xv, axis=-1, keepdims=True) + EPS)
        n = xv * r
        gng[...] += jnp.sum(dxn * n, axis=0, keepdims=True)
        dxg = dxn * g_ref[...]
        gx_ref[...] = dh_ref[...] + r * (dxg - n * jnp.mean(dxg * n, axis=-1, keepdims=True))

        @pl.when(step == n_steps - 1)
        def _():
            x_, y_, c_ = lax.axis_index("x"), lax.axis_index("y"), lax.axis_index("c")
            me = 4 * x_ + 2 * y_ + c_
            for q in range(8):
                stage[q:q + 1, :] = gng[:, q * 128:(q + 1) * 128]
                stage[8 + q:9 + q, :] = gfg_ref[:, q * 128:(q + 1) * 128]
            stage[16:24, :] = gsink_ref[...]
            stage[24:32, :] = loss_ref[...]
            stage[32:small_rows, :] = jnp.zeros((small_rows - 32, 128), F32)
            for j in range(N_DEV):
                piece = gcw_ref[:, (j // 2) * 128:(j // 2 + 1) * 128]
                if j % 2:
                    piece = pltpu.roll(piece, 64, 1)
                stage[32 + 8 * j:32 + 8 * j + CONV_K, 0:64] = piece[:, 0:64]
            own = pltpu.make_async_copy(stage, all_ref.at[me], small_own)
            own.start()
            sends = []
            for k in range(1, N_DEV):
                cp = pltpu.make_async_remote_copy(
                    src_ref=stage, dst_ref=all_ref.at[me],
                    send_sem=small_send.at[k - 1], recv_sem=small_recv.at[k - 1],
                    device_id=(x_ ^ (k >> 2), y_ ^ ((k >> 1) & 1), c_ ^ (k & 1)), device_id_type=MESH)
                cp.start()
                sends.append(cp)
            for cp in sends:
                cp.wait_send()
                cp.wait_recv()
            own.wait()

        finish()

    row = lambda i: (i, 0)
    fixed = lambda i: (0, 0)
    any_spec = pl.BlockSpec(memory_space=pl.ANY)
    outs = pl.pallas_call(
        body,
        name="grad_x_reduce_scatter",
        grid=(n_steps,),
        in_specs=[
            pl.BlockSpec((TM, PA_W), row),
            pl.BlockSpec((TM, PC_W), row),
            pl.BlockSpec((IN_W, D_MODEL), fixed),
            pl.BlockSpec((TM, D_MODEL), row),
            pl.BlockSpec((TM, D_MODEL), row),
            pl.BlockSpec((1, D_MODEL), fixed),
        ] + [pl.BlockSpec(a.shape, fixed) for a in small] + [any_spec] * rs.n,
        out_specs=[pl.BlockSpec((TM, D_MODEL), row), any_spec] + [any_spec] * n_rs_out,
        out_shape=[jax.ShapeDtypeStruct((S, D_MODEL), F32),
                   jax.ShapeDtypeStruct((N_DEV, small_rows, 128), F32)] + rs.out_shape(),
        scratch_shapes=[
            pltpu.VMEM((1, D_MODEL), F32),
            pltpu.VMEM((small_rows, 128), F32),
            pltpu.SemaphoreType.DMA((N_DEV - 1,)),
            pltpu.SemaphoreType.DMA((N_DEV - 1,)),
            pltpu.SemaphoreType.DMA,
        ] + rs.scratch_shapes(),
        compiler_params=_params(("arbitrary",)),
    )(da, dc, wt, x, dh, norm_g, *small, *grads)
    return outs[0], outs[1], outs[2:2 + rs.n], outs[2 + rs.n:2 + 2 * rs.n]


def _grad_w_in(da, pc, dmix, conv_w, xn):
    S = xn.shape[0]
    tm = 2 * TM
    nt = S // tm
    t16 = tm // HALO

    def body(da_ref, pc_ref, prev_ref, next_ref, dm_ref, dmn_ref, cw_ref, xn_ref, gw_ref, dc_ref, gcw_ref):
        i = pl.program_id(0)

        @pl.when(i == 0)
        def _():
            gw_ref[...] = jnp.zeros_like(gw_ref)
            gcw_ref[...] = jnp.zeros_like(gcw_ref)

        xn = xn_ref[...]
        gw_ref[0:512, :] += _tn(da_ref[:, 0:512], xn)
        gw_ref[768:1280, :] += _tn(da_ref[:, 512:1024], xn)
        gw_ref[512:768, :] += _tn(da_ref[:, 1024:1280], xn)
        def piece_grad(k):
            rows = slice(PA_W + k * CONV_W, PA_W + (k + 1) * CONV_W)
            gw_ref[rows, :] += _tn(dc_ref[:, k * CONV_W:(k + 1) * CONV_W], xn)

        _conv_bwd_tile(pc_ref, prev_ref, next_ref, dm_ref, dmn_ref, cw_ref, dc_ref, gcw_ref, i > 0, i < nt - 1,
                       piece_grad)

    row = lambda i: (i, 0)
    fixed = lambda i: (0, 0)
    nxt = lambda i: jnp.minimum((i + 1) * t16, nt * t16 - 1)
    return pl.pallas_call(
        body,
        name="grad_w_in",
        grid=(nt,),
        in_specs=[
            pl.BlockSpec((tm, PA_W), row),
            pl.BlockSpec((tm, PC_W), row),
            pl.BlockSpec((HALO, PC_W), lambda i: (jnp.maximum(i * t16 - 1, 0), 0)),
            pl.BlockSpec((HALO, PC_W), lambda i: (nxt(i), 0)),
            pl.BlockSpec((tm, CONV_W), lambda i: (i, 1)),
            pl.BlockSpec((HALO, CONV_W), lambda i: (nxt(i), 1)),
            pl.BlockSpec((CONV_K, CONV_W), fixed),
            pl.BlockSpec((tm, D_MODEL), row),
        ],
        out_specs=[
            pl.BlockSpec((IN_W, D_MODEL), fixed, pipeline_mode=pl.Buffered(1)),
            pl.BlockSpec((tm, PC_W), row),
            pl.BlockSpec((CONV_K, CONV_W), fixed),
        ],
        out_shape=[
            jax.ShapeDtypeStruct((IN_W, D_MODEL), F32),
            jax.ShapeDtypeStruct((S, PC_W), BF16),
            jax.ShapeDtypeStruct((CONV_K, CONV_W), F32),
        ],
        compiler_params=_params(("arbitrary",)),
    )(da, pc, pc, pc, dmix, dmix, conv_w, xn)


def _adam_update(w, g, m, v):
    c1 = 1.0 - ADAM_B1 ** ADAM_STEP
    c2 = 1.0 - ADAM_B2 ** ADAM_STEP
    nm = ADAM_B1 * m + (1.0 - ADAM_B1) * g
    nv = ADAM_B2 * v + (1.0 - ADAM_B2) * (g * g)
    return -ADAM_LR * ((nm / c1) / (jnp.sqrt(nv / c2) + ADAM_EPS) + ADAM_WD * w), nm, nv


def _sum_chips_adamw(own, others, w, m, v, name):
    def body(own_ref, p_ref, w_ref, m_ref, v_ref, g_ref, d_ref, nm_ref, nv_ref):
        g = own_ref[...]
        for k in range(N_CHIP - 1):
            g = g + p_ref[k].astype(F32)
        g_ref[...] = g
        d_ref[...], nm_ref[...], nv_ref[...] = _adam_update(w_ref[...], g, m_ref[...], v_ref[...])

    rows, cols = w.shape
    half = rows // 2
    blk = pl.BlockSpec((half, cols), lambda i: (i, 0))
    shape = jax.ShapeDtypeStruct(w.shape, F32)
    return pl.pallas_call(
        body,
        name=name,
        grid=(2,),
        in_specs=[blk, pl.BlockSpec((N_CHIP - 1, half, cols), lambda i: (0, i, 0)), blk, blk, blk],
        out_specs=[blk] * 4,
        out_shape=[shape] * 4,
        compiler_params=_params(("arbitrary",)),
    )(own, others, w, m, v)


SMALL_ROWS = 96


def _small_adamw(parts, params):
    def body(parts_ref, *rest):
        prm, outs, total = rest[:12], rest[12:29], rest[29]
        me = 4 * lax.axis_index("x") + 2 * lax.axis_index("y") + lax.axis_index("c")
        acc = parts_ref[0]
        for d in range(1, N_DEV):
            acc = acc + parts_ref[d]
        total[...] = acc
        grads = (total[0:8, :], total[8:16, :], total[16:17, 0:8],
                 total[pl.ds(pl.multiple_of(32 + me * 8, 8), CONV_K), 0:64])
        outs[0][...] = total[24:25, 0:1]
        for k, g in enumerate(grads):
            w_ref, m_ref, v_ref = prm[3 * k:3 * k + 3]
            g_ref, d_ref, nm_ref, nv_ref = outs[1 + 4 * k:5 + 4 * k]
            g_ref[...] = g
            d_ref[...], nm_ref[...], nv_ref[...] = _adam_update(w_ref[...], g, m_ref[...], v_ref[...])

    flat = [a for p in params for a in p]
    out_shape = [jax.ShapeDtypeStruct((1, 1), F32)]
    for p in params:
        out_shape += [jax.ShapeDtypeStruct(p[0].shape, F32)] * 4
    return pl.pallas_call(
        body,
        name="adamw_small",
        out_shape=out_shape,
        scratch_shapes=[pltpu.VMEM((SMALL_ROWS, 128), F32)],
        compiler_params=_params(),
    )(parts, *flat)


def kernel(x, norm_g, w_in, sinks, conv_w, w_out, final_g, loss_target, m_norm_g, m_w_in, m_sinks, m_conv_w, m_w_out, m_final_g, v_norm_g, v_w_in, v_sinks, v_conv_w, v_w_out, v_final_g):
    S = x.shape[1]
    x2 = x.reshape(S, D_MODEL)
    t2 = loss_target.reshape(S, D_MODEL)
    ng = norm_g.reshape(1, D_MODEL)
    fg = final_g.reshape(1, D_MODEL)

    xn, tab, wt = _prologue(x2, ng, w_in.T)
    pa, pc, (wo, cw) = _fwd_proj(xn, wt, [w_out, conv_w])
    ya, att, probs, psinks, q_stack = _attn_fwd(pa, tab, sinks)
    dh, dmix, g_wo, g_fg, loss_part, cw3 = _out_loss(x2, t2, ya, pc, cw, wo, fg)
    da, g_sinks = _attn_bwd(pa, dmix, att, probs, psinks, q_stack, tab)
    g_wt, dc, g_cw = _grad_w_in(da, pc, dmix, cw3, xn)
    grad_x, parts, own, others = _grad_x(
        da, dc, wt, x2, dh, ng, (g_fg, g_sinks, loss_part, g_cw),
        [g_wt.reshape(N_DEV, SHARD_IN, D_MODEL), g_wo.reshape(N_DEV, SHARD_OUT, D_MODEL)])
    gt, dt, nmt, nvt = _sum_chips_adamw(own[0], others[0], w_in.T, m_w_in.T, v_w_in.T, "adamw_w_in")
    grad_w_in, d_w_in, nm_w_in, nv_w_in = gt.T, dt.T, nmt.T, nvt.T
    grad_w_out, d_w_out, nm_w_out, nv_w_out = _sum_chips_adamw(
        own[1], others[1], w_out, m_w_out, v_w_out, "adamw_w_out")
    vec = lambda a: a.reshape(8, 128)
    row = lambda a: a.reshape(1, 8)
    res = _small_adamw(parts, [
        (vec(norm_g), vec(m_norm_g), vec(v_norm_g)), (vec(final_g), vec(m_final_g), vec(v_final_g)),
        (row(sinks), row(m_sinks), row(v_sinks)), (conv_w, m_conv_w, v_conv_w)])
    loss = res[0].reshape(())
    grad_norm_g, d_ng, nm_ng, nv_ng = [a.reshape(D_MODEL) for a in res[1:5]]
    grad_final_g, d_fg, nm_fg, nv_fg = [a.reshape(D_MODEL) for a in res[5:9]]
    grad_sinks, d_sk, nm_sk, nv_sk = [a.reshape(N_Q_HEADS) for a in res[9:13]]
    grad_conv_w, d_cw, nm_cw, nv_cw = res[13:17]

    return (loss, grad_x.reshape(1, S, D_MODEL), grad_norm_g, grad_w_in, grad_sinks, grad_conv_w, grad_w_out, grad_final_g,
            d_ng, d_w_in, d_sk, d_cw, d_w_out, d_fg,
            nm_ng, nm_w_in, nm_sk, nm_cw, nm_w_out, nm_fg,
            nv_ng, nv_w_in, nv_sk, nv_cw, nv_w_out, nv_fg)
```

```python
import numpy as np
import jax
import jax.numpy as jnp
from jax import lax
from jax.experimental import pallas as pl
from jax.experimental.pallas import tpu as pltpu

F32 = jnp.float32
BF16 = jnp.bfloat16
MESH = pl.DeviceIdType.MESH

D_MODEL = 1024
HEAD_DIM = 64
N_Q_HEADS = 8
ATTN_W = 512
KV_W = 128
BLK = 128
CONV_W = 512
CONV_K = 3
IN_W = 3328
PA_W = 1280
PC_W = 2048
EPS = 1e-5
ROPE_THETA = 500000.0
ROT_DIM = 16
N_DEV = 8
N_CHIP = 4
SHARD_IN = IN_W // N_DEV
SHARD_OUT = D_MODEL // N_DEV

ADAM_LR = 0.001
ADAM_B1 = 0.9
ADAM_B2 = 0.999
ADAM_EPS = 1e-08
ADAM_WD = 0.01
ADAM_STEP = 10

ACT = jnp.bfloat16

TM = 512
TQ = 1024
HALO = 16
VMEM_LIMIT = 56 * 1024 * 1024

NT_DIMS = (((1,), (1,)), ((), ()))
TN_DIMS = (((0,), (0,)), ((), ()))


def _params(sem=None):
    kw = dict(vmem_limit_bytes=VMEM_LIMIT)
    if sem is not None:
        kw["dimension_semantics"] = sem
    return pltpu.CompilerParams(**kw)


def _nt(a, b):
    return lax.dot_general(a, b, NT_DIMS, preferred_element_type=F32)


def _tn(a, b):
    return lax.dot_general(a, b, TN_DIMS, preferred_element_type=F32)


def _nn(a, b):
    return jnp.dot(a, b, preferred_element_type=F32)


def _silu(g):
    return g * jax.nn.sigmoid(g)


def _silu_and_grad(g):
    s = jax.nn.sigmoid(g)
    return g * s, s * (1.0 + g * (1.0 - s))


class _AllGatherInSteps:
    def __init__(self, arrs, forward_step):
        self.blocks = [(a.shape, a.dtype) for a in arrs]
        self.n = len(arrs)
        self.forward_step = forward_step

    def out_shape(self):
        return [jax.ShapeDtypeStruct((N_DEV * s[0], s[1]), d) for s, d in self.blocks]

    def scratch_shapes(self):
        return [pltpu.SemaphoreType.DMA((7 * self.n,)), pltpu.SemaphoreType.DMA((7 * self.n,)),
                pltpu.SemaphoreType.DMA((self.n,))]

    def emit(self, step, n_steps, x_refs, out_refs, scratch):
        assert n_steps > self.forward_step + 1
        send_sems, recv_sems, local_sems = scratch
        x, y, c = lax.axis_index("x"), lax.axis_index("y"), lax.axis_index("c")
        me, sibling = (x, y, c), (x, y, 1 - c)
        chips = [(1 - x, y), (x, 1 - y), (1 - x, 1 - y)]

        def rows(a, px, py, pc):
            m = self.blocks[a][0][0]
            return out_refs[a].at[pl.ds((4 * px + 2 * py + pc) * m, m), :]

        def copy(a, k, block, to, src=None):
            return pltpu.make_async_remote_copy(
                src_ref=rows(a, *block) if src is None else src, dst_ref=rows(a, *block),
                send_sem=send_sems.at[a * 7 + k], recv_sem=recv_sems.at[a * 7 + k],
                device_id=to, device_id_type=MESH)

        def mine(a):
            return pltpu.make_async_copy(x_refs[a], rows(a, *me), local_sems.at[a])

        def first(a):
            return ([copy(a, 0, me, sibling, src=x_refs[a])]
                    + [copy(a, 1 + j, me, (*chip, c), src=x_refs[a]) for j, chip in enumerate(chips)])

        def passed(a):
            return [copy(a, 4 + j, (*chip, c), sibling) for j, chip in enumerate(chips)]

        @pl.when(step == 0)
        def _():
            for a in range(self.n):
                mine(a).start()
                for cp in first(a):
                    cp.start()

        @pl.when(step == self.forward_step)
        def _():
            for j, chip in enumerate(chips):
                for a in range(self.n):
                    copy(a, 1 + j, (*chip, c), me).wait_recv()
                    copy(a, 4 + j, (*chip, c), sibling).start()

        def finish():
            @pl.when(step == n_steps - 1)
            def _():
                for a in range(self.n):
                    copy(a, 0, sibling, me).wait_recv()
                    for j, chip in enumerate(chips):
                        copy(a, 4 + j, (*chip, 1 - c), me).wait_recv()
                    for cp in first(a) + passed(a):
                        cp.wait_send()
                    mine(a).wait()

        return finish


class _AllGatherViaNeighbours:
    def __init__(self, arr, first, mid, second):
        (self.m, self.ncol), self.dtype = arr.shape, arr.dtype
        assert self.m % 32 == 0
        self.first, self.mid, self.second = first, mid, second

    def out_shape(self):
        return [jax.ShapeDtypeStruct((N_DEV * self.m, self.ncol), self.dtype)]

    def scratch_shapes(self):
        return [pltpu.SemaphoreType.DMA((11,)), pltpu.SemaphoreType.DMA((11,)), pltpu.SemaphoreType.DMA]

    def emit(self, step, n_steps, x_ref, out_ref, scratch):
        assert 0 < self.first < self.mid < self.second < n_steps - 1
        send_sems, recv_sems, local_sem = scratch
        x, y, c = lax.axis_index("x"), lax.axis_index("y"), lax.axis_index("c")
        half = self.m // 2
        sibling, xn, yn = (x, y, 1 - c), (1 - x, y, c), (x, 1 - y, c)

        def rows(dev, part=None):
            px, py, pc = dev
            base = (4 * px + 2 * py + pc) * self.m
            if part is None:
                return out_ref.at[pl.ds(base, self.m), :]
            return out_ref.at[pl.ds(base + part * half, half), :]

        def copy(k, dev, to, part=None, src=None):
            return pltpu.make_async_remote_copy(
                src_ref=rows(dev, part) if src is None else src, dst_ref=rows(dev, part),
                send_sem=send_sems.at[k], recv_sem=recv_sems.at[k], device_id=to, device_id_type=MESH)

        me, dg = (x, y, c), (1 - x, 1 - y, c)
        mine = pltpu.make_async_copy(x_ref, rows(me), local_sem)
        my_half = lambda part: x_ref.at[pl.ds(part * half, half), :]
        sends = [
            copy(0, me, sibling, src=x_ref), copy(1, me, xn, part=0, src=my_half(0)),
            copy(2, me, yn, part=1, src=my_half(1)), copy(3, xn, yn, part=0), copy(4, yn, xn, part=1),
            copy(5, xn, sibling), copy(6, yn, sibling), copy(7, dg, sibling, part=0), copy(8, dg, sibling, part=1),
            copy(9, me, xn, part=1, src=my_half(1)), copy(10, me, yn, part=0, src=my_half(0)),
        ]
        other = lambda dev: (dev[0], dev[1], 1 - c)
        arrivals = [
            copy(0, other(me), sibling), copy(1, xn, xn, part=0), copy(2, yn, yn, part=1), copy(3, dg, yn, part=0),
            copy(4, dg, xn, part=1), copy(5, other(xn), sibling), copy(6, other(yn), sibling),
            copy(7, other(dg), sibling, part=0), copy(8, other(dg), sibling, part=1),
            copy(9, xn, xn, part=1), copy(10, yn, yn, part=0),
        ]

        @pl.when(step == 0)
        def _():
            mine.start()
            for k in (0, 1, 2, 9, 10):
                sends[k].start()

        @pl.when(step == self.first)
        def _():
            arrivals[1].wait_recv()
            sends[3].start()
            arrivals[2].wait_recv()
            sends[4].start()

        @pl.when(step == self.mid)
        def _():
            arrivals[9].wait_recv()
            sends[5].start()
            arrivals[10].wait_recv()
            sends[6].start()

        @pl.when(step == self.second)
        def _():
            arrivals[3].wait_recv()
            sends[7].start()
            arrivals[4].wait_recv()
            sends[8].start()

        def finish():
            @pl.when(step == n_steps - 1)
            def _():
                for k in (0, 5, 6, 7, 8):
                    arrivals[k].wait_recv()
                for cp in sends:
                    cp.wait_send()
                mine.wait()

        return finish


class _ReduceScatter:
    def __init__(self, grads):
        self.shapes = [g.shape[1:] for g in grads]
        self.n = len(grads)
        self.items = tuple((a, r) for r in (1, 2, 3, 0) for a in range(self.n))
        self.steps = N_CHIP + 2

    def out_shape(self):
        own = [jax.ShapeDtypeStruct(s, F32) for s in self.shapes]
        ici = [jax.ShapeDtypeStruct((N_CHIP - 1,) + s, BF16) for s in self.shapes]
        land = [jax.ShapeDtypeStruct((N_CHIP,) + s, F32) for s in self.shapes]
        return own + ici + land

    def scratch_shapes(self):
        n_items = len(self.items)
        return ([pltpu.VMEM((2,) + s, F32) for s in self.shapes]
                + [pltpu.VMEM((N_CHIP - 1,) + s, BF16) for s in self.shapes]
                + [pltpu.VMEM(s, F32) for s in self.shapes]
                + [pltpu.SemaphoreType.DMA((self.n * N_CHIP,))] * 2
                + [pltpu.SemaphoreType.DMA((2 * n_items,))]
                + [pltpu.SemaphoreType.DMA((self.n * (N_CHIP - 1),))] * 2
                + [pltpu.SemaphoreType.DMA((self.n,))])

    def emit(self, step, n_steps, g_refs, out_refs, scratch):
        assert n_steps > self.steps
        n = self.n
        own_refs, ici_refs, land_refs = out_refs[:n], out_refs[n:2 * n], out_refs[2 * n:]
        stage, pair_bf, pair_own = scratch[:n], scratch[n:2 * n], scratch[2 * n:3 * n]
        sib_send, sib_recv, load_sems, ici_send, ici_recv, own_sems = scratch[3 * n:]
        x, y, c = lax.axis_index("x"), lax.axis_index("y"), lax.axis_index("c")

        def chip_of(r):
            return (x ^ (r >> 1), y ^ (r & 1))

        def block_of(r, core):
            cx, cy = chip_of(r)
            return 4 * cx + 2 * cy + core

        def to_sibling(a, r):
            return pltpu.make_async_remote_copy(
                src_ref=g_refs[a].at[block_of(r, 1 - c)], dst_ref=land_refs[a].at[r],
                send_sem=sib_send.at[a * N_CHIP + r], recv_sem=sib_recv.at[a * N_CHIP + r],
                device_id=(x, y, 1 - c), device_id_type=MESH)

        def loads(k):
            a, r = self.items[k]
            return (pltpu.make_async_copy(g_refs[a].at[block_of(r, c)], stage[a].at[0], load_sems.at[2 * k]),
                    pltpu.make_async_copy(land_refs[a].at[r], stage[a].at[1], load_sems.at[2 * k + 1]))

        def to_owner(k):
            a, r = self.items[k]
            if r == 0:
                return pltpu.make_async_copy(pair_own[a], own_refs[a], own_sems.at[a])
            return pltpu.make_async_remote_copy(
                src_ref=pair_bf[a].at[r - 1], dst_ref=ici_refs[a].at[r - 1],
                send_sem=ici_send.at[a * (N_CHIP - 1) + r - 1], recv_sem=ici_recv.at[a * (N_CHIP - 1) + r - 1],
                device_id=(*chip_of(r), c), device_id_type=MESH)

        @pl.when(step == 0)
        def _():
            for a, r in self.items:
                to_sibling(a, r).start()

        def fetch(k):
            a, r = self.items[k]
            to_sibling(a, r).wait_recv()
            for cp in loads(k):
                cp.start()

        def add_and_send(k):
            a, r = self.items[k]
            for cp in loads(k):
                cp.wait()
            total = stage[a][0] + stage[a][1]
            if r == 0:
                pair_own[a][...] = total
            else:
                pair_bf[a][r - 1] = total.astype(BF16)
            to_owner(k).start()

        for g in range(N_CHIP + 1):
            @pl.when(step == 1 + g)
            def _(g=g):
                if g > 0:
                    for k in range((g - 1) * n, g * n):
                        add_and_send(k)
                if g < N_CHIP:
                    for k in range(g * n, (g + 1) * n):
                        fetch(k)

        def finish():
            @pl.when(step == n_steps - 1)
            def _():
                for k, (a, r) in enumerate(self.items):
                    if r == 0:
                        to_owner(k).wait()
                    else:
                        to_owner(k).wait_send()
                        to_owner(k).wait_recv()
                for a, r in self.items:
                    to_sibling(a, r).wait_send()

        return finish


def _prologue(x, norm_g, w_shard):
    S = x.shape[0]
    n_steps = S // TM
    half = ROT_DIM // 2
    pos = np.arange(S, dtype=np.float32)
    inv_freq = np.float32(ROPE_THETA) ** (-np.arange(0, ROT_DIM, 2, dtype=np.float32) / np.float32(ROT_DIM))
    ang = inv_freq.astype(np.float32)[:, None] * pos[None, :]
    cs = jnp.asarray(np.concatenate([np.cos(ang), np.sin(ang)], axis=0).astype(np.float32))
    ag = _AllGatherViaNeighbours(jax.ShapeDtypeStruct(w_shard.shape, BF16),
                                 first=n_steps // 4, mid=n_steps // 2 + 2, second=n_steps - 2)

    def body(x_ref, g_ref, cs_ref, w_ref, xn_ref, tab_ref, wt_ref, w_bf, *ag_scratch):
        step = pl.program_id(0)

        @pl.when(step == 0)
        def _():
            w_bf[...] = w_ref[...].astype(BF16)

        finish = ag.emit(step, n_steps, w_bf, wt_ref, ag_scratch)
        xv = x_ref[...]
        r = lax.rsqrt(jnp.mean(xv * xv, axis=-1, keepdims=True) + EPS)
        xn_ref[...] = (xv * r * g_ref[...]).astype(BF16)

        xt = jnp.concatenate([cs_ref[...], jnp.zeros((128 - 2 * half, TM), F32)], axis=0).T
        lane = lax.broadcasted_iota(jnp.int32, (TM, 128), 1)
        rr = lane & (HEAD_DIM - 1)
        first = lane < HEAD_DIM

        def at(shift_first, shift_second):
            return jnp.where(first, pltpu.roll(xt, shift_first, 1) if shift_first else xt,
                             pltpu.roll(xt, shift_second, 1))

        cos_lo, cos_hi = at(0, HEAD_DIM), at(half, HEAD_DIM + half)
        sin_lo, sin_hi = at(128 - half, HEAD_DIM - half), at(0, HEAD_DIM)
        tab_ref[:, 0:128] = jnp.where(rr < half, cos_lo, jnp.where(rr < ROT_DIM, cos_hi, 1.0))
        tab_ref[:, 128:256] = jnp.where(rr < half, -sin_lo, 0.0)
        tab_ref[:, 256:384] = jnp.where((rr >= half) & (rr < ROT_DIM), sin_hi, 0.0)
        finish()

    any_spec = pl.BlockSpec(memory_space=pl.ANY)
    return pl.pallas_call(
        body,
        name="prologue_all_gather_w_in",
        grid=(n_steps,),
        in_specs=[
            pl.BlockSpec((TM, D_MODEL), lambda i: (i, 0)),
            pl.BlockSpec((1, D_MODEL), lambda i: (0, 0)),
            pl.BlockSpec((2 * half, TM), lambda i: (0, i)),
            pl.BlockSpec(w_shard.shape, lambda i: (0, 0)),
        ],
        out_specs=[
            pl.BlockSpec((TM, D_MODEL), lambda i: (i, 0)),
            pl.BlockSpec((TM, 384), lambda i: (i, 0)),
            any_spec,
        ],
        out_shape=[
            jax.ShapeDtypeStruct((S, D_MODEL), BF16),
            jax.ShapeDtypeStruct((S, 384), F32),
        ] + ag.out_shape(),
        scratch_shapes=[pltpu.VMEM(w_shard.shape, BF16)] + ag.scratch_shapes(),
        compiler_params=_params(("arbitrary",)),
    )(x, norm_g, cs, w_shard)


def _fwd_proj(xn, wt, later):
    S = xn.shape[0]
    tm = 2 * TM
    n_steps = S // tm
    ag = _AllGatherInSteps([jax.ShapeDtypeStruct(later[0].shape, BF16), jax.ShapeDtypeStruct((8, 128), F32)],
                           forward_step=n_steps // 2)

    def body(xn_ref, wt_ref, *rest):
        later_refs, rest = rest[:ag.n], rest[ag.n:]
        pa_ref, pc_ref = rest[:2]
        gathered, w_bf, cw_pad, ag_scratch = rest[2:2 + ag.n], rest[2 + ag.n], rest[3 + ag.n], rest[4 + ag.n:]
        step = pl.program_id(0)

        @pl.when(step == 0)
        def _():
            w_bf[...] = later_refs[0][...].astype(BF16)
            cw_pad[...] = jnp.zeros_like(cw_pad)
            cw_pad[0:CONV_K, 0:64] = later_refs[1][...]

        finish = ag.emit(step, n_steps, (w_bf, cw_pad), gathered, ag_scratch)
        xn = xn_ref[...]
        pa_ref[:, 0:512] = _nt(xn, wt_ref[0:512, :]).astype(ACT)
        pa_ref[:, 512:1024] = _nt(xn, wt_ref[768:1280, :]).astype(ACT)
        pa_ref[:, 1024:1280] = _nt(xn, wt_ref[512:768, :]).astype(ACT)
        pc_ref[...] = _nt(xn, wt_ref[1280:3328, :]).astype(ACT)
        finish()

    any_spec = pl.BlockSpec(memory_space=pl.ANY)
    outs = pl.pallas_call(
        body,
        name="fwd_proj_all_gather",
        grid=(n_steps,),
        in_specs=[
            pl.BlockSpec((tm, D_MODEL), lambda i: (i, 0)),
            pl.BlockSpec((IN_W, D_MODEL), lambda i: (0, 0)),
            pl.BlockSpec(later[0].shape, lambda i: (0, 0)),
            pl.BlockSpec(later[1].shape, lambda i: (0, 0)),
        ],
        out_specs=[
            pl.BlockSpec((tm, PA_W), lambda i: (i, 0)),
            pl.BlockSpec((tm, PC_W), lambda i: (i, 0)),
        ] + [any_spec] * ag.n,
        out_shape=[
            jax.ShapeDtypeStruct((S, PA_W), ACT),
            jax.ShapeDtypeStruct((S, PC_W), ACT),
        ] + ag.out_shape(),
        scratch_shapes=[pltpu.VMEM(later[0].shape, BF16), pltpu.VMEM((8, 128), F32)] + ag.scratch_shapes(),
        compiler_params=_params(("arbitrary",)),
    )(xn, wt, *later)
    return outs[0], outs[1], outs[2:]


def _rope(t, tab):
    return (t * tab[:, 0:128] + pltpu.roll(t, 120, 1) * tab[:, 128:256]
            + pltpu.roll(t, 8, 1) * tab[:, 256:384])


def _rope_t(d, tab):
    return (d * tab[:, 0:128] + pltpu.roll(d * tab[:, 128:256], 8, 1)
            + pltpu.roll(d * tab[:, 256:384], 120, 1))


def _fill_kv(kall, kvc_ref, kvp_ref, tabc_ref, tabp_ref):
    for lo, kv_ref, tab_ref, n in ((0, kvp_ref, tabp_ref, BLK), (BLK, kvc_ref, tabc_ref, TQ)):
        k = _rope(kv_ref[:, 0:128].astype(F32), tab_ref[...])
        v = kv_ref[:, 128:256].astype(F32)
        kall[0, lo:lo + n, :] = k.astype(BF16)
        kall[1, lo:lo + n, :] = pltpu.roll(k, 64, 1).astype(BF16)
        kall[2, lo:lo + n, :] = v.astype(BF16)
        kall[3, lo:lo + n, :] = pltpu.roll(v, 64, 1).astype(BF16)


HEADS = (((0, 0), (1, 0), (2, 1), (3, 1)), ((0, 1), (1, 1), (2, 0), (3, 0)))


def _upper():
    kj = lax.broadcasted_iota(jnp.int32, (BLK, 4 * BLK), 0)
    qi = lax.broadcasted_iota(jnp.int32, (BLK, 4 * BLK), 1) & (BLK - 1)
    return kj > qi


def _merge(upper, both):
    return jnp.where(upper, both[0:BLK, :], both[BLK:2 * BLK, :])


def _split_store(ref, s, upper_b, vb):
    first = vb * upper_b
    ref[s, 0:BLK, :] = first
    ref[s, BLK:2 * BLK, :] = vb - first


def _sink_rows(sink_ref):
    return [jnp.concatenate([jnp.full((1, BLK), sink_ref[2 * p + e], F32) for p, e in HEADS[s]], axis=1)
            for s in range(2)]


def _stack_heads(ref, slot, half, pairs, s=None):
    for a, (p, e) in enumerate(HEADS[slot if s is None else s]):
        ref[slot, a * BLK:(a + 1) * BLK, :] = jnp.where(half[e], pairs[p], 0.0).astype(BF16)


def _unstack_pair(half, outs, p):
    lo = 0 if p < 2 else 1
    rows = slice(p * BLK, (p + 1) * BLK)
    return jnp.where(half[0], outs[lo][rows, :], outs[1 - lo][rows, :])


def _softmax(sm, sinks):
    m = jnp.maximum(jnp.max(sm, axis=0, keepdims=True), sinks)
    p = jnp.exp(sm - m)
    es = jnp.exp(sinks - m)
    inv = 1.0 / (jnp.sum(p, axis=0, keepdims=True) + es)
    return p * inv, es * inv


def _scores(kk, q_stack, first):
    st = _nt(kk, q_stack)
    prev = st[0:BLK, :]
    if first is not None:
        prev = prev + jnp.where(first, -jnp.inf, 0.0)
    return prev, st[BLK:2 * BLK, :]


def _attn_specs(tile):
    nb = TQ // BLK
    prev = lambda i: jnp.maximum(tile(i) * nb - 1, 0)
    return [
        pl.BlockSpec(memory_space=pltpu.SMEM),
        pl.BlockSpec((TQ, ATTN_W), lambda i: (tile(i), 0)),
        pl.BlockSpec((TQ, ATTN_W), lambda i: (tile(i), 1)),
        pl.BlockSpec((TQ, 2 * KV_W), lambda i: (tile(i), 4)),
        pl.BlockSpec((BLK, 2 * KV_W), lambda i: (prev(i), 4)),
        pl.BlockSpec((TQ, 384), lambda i: (tile(i), 0)),
        pl.BlockSpec((BLK, 384), lambda i: (prev(i), 0)),
    ]


def _attn_fwd(pa, tab, sinks):
    S = pa.shape[0]
    nb = TQ // BLK

    def body(sink_ref, q_ref, g_ref, kvc_ref, kvp_ref, tabc_ref, tabp_ref, o_ref, att_ref, pm_ref, ps_ref,
             qs_ref, kall, p_sc):
        i = pl.program_id(0)
        _fill_kv(kall, kvc_ref, kvp_ref, tabc_ref, tabp_ref)
        lane = lax.broadcasted_iota(jnp.int32, (BLK, 128), 1)
        half = [lane < HEAD_DIM, lane >= HEAD_DIM]
        upper = _upper()
        upper_b = upper.astype(BF16)
        sinks = _sink_rows(sink_ref)
        for j in range(nb):
            rq = slice(j * BLK, (j + 1) * BLK)
            rk = slice(j * BLK, (j + 2) * BLK)
            tab = tabc_ref[rq, :]
            qr = [_rope(q_ref[rq, p * 128:(p + 1) * 128].astype(F32), tab) * 0.125 for p in range(4)]
            outs = []
            for s in range(2):
                _stack_heads(qs_ref, 2 * j + s, half, qr, s)
                prev, cur = _scores(kall[s, rk, :], qs_ref[2 * j + s], i == 0 if j == 0 else None)
                prob, psink = _softmax(jnp.where(upper, prev, cur), sinks[s])
                pb = prob.astype(BF16)
                pm_ref[(2 * j + s) * BLK:(2 * j + s + 1) * BLK, :] = pb
                ps_ref[2 * j + s:2 * j + s + 1, :] = psink
                _split_store(p_sc, s, upper_b, pb)
                outs.append(_tn(p_sc[s], kall[2 + s, rk, :]))
            for p in range(4):
                cols = slice(p * 128, (p + 1) * 128)
                att = _unstack_pair(half, outs, p)
                att_ref[rq, cols] = att.astype(BF16)
                o_ref[rq, cols] = (att * _silu(g_ref[rq, cols].astype(F32))).astype(BF16)

    return pl.pallas_call(
        body,
        name="attn_fwd",
        grid=(S // TQ,),
        in_specs=_attn_specs(lambda i: i),
        out_specs=[pl.BlockSpec((TQ, ATTN_W), lambda i: (i, 0))] * 2 + [
            pl.BlockSpec((2 * TQ, 4 * BLK), lambda i: (i, 0)),
            pl.BlockSpec((2 * nb, 4 * BLK), lambda i: (i, 0)),
            pl.BlockSpec((2 * nb, 4 * BLK, 128), lambda i: (i, 0, 0)),
        ],
        out_shape=[jax.ShapeDtypeStruct((S, ATTN_W), BF16)] * 2 + [
            jax.ShapeDtypeStruct((2 * S, 4 * BLK), BF16),
            jax.ShapeDtypeStruct((2 * S // BLK, 4 * BLK), F32),
            jax.ShapeDtypeStruct((2 * S // BLK, 4 * BLK, 128), BF16),
        ],
        scratch_shapes=[
            pltpu.VMEM((4, BLK + TQ, 128), BF16),
            pltpu.VMEM((2, 2 * BLK, 4 * BLK), BF16),
        ],
        compiler_params=_params(("arbitrary",)),
    )(sinks, pa, pa, pa, pa, tab, tab)


def _shift_down(u, halo_ref, has_prev):
    def halo_u(r):
        hu = halo_ref[r:r + 1, 512:1024].astype(F32) * halo_ref[r:r + 1, 1024:1536].astype(F32)
        return jnp.where(has_prev, hu, 0.0)

    row = lax.broadcasted_iota(jnp.int32, u.shape, 0)
    um1 = jnp.where(row == 0, halo_u(HALO - 1), pltpu.roll(u, 1, 0))
    um2 = jnp.where(row == 0, halo_u(HALO - 2), jnp.where(row == 1, halo_u(HALO - 1), pltpu.roll(u, 2, 0)))
    return um1, um2


def _gathered_conv_w(all_ref):
    pairs = [all_ref[16 * p:16 * p + 8, :] + pltpu.roll(all_ref[16 * p + 8:16 * p + 16, :], 64, 1)
             for p in range(N_DEV // 2)]
    return jnp.concatenate(pairs, axis=1)


def _conv_tile(pc_ref, halo_ref, w_ref, has_prev):
    b = pc_ref[:, 0:512].astype(F32)
    c = pc_ref[:, 512:1024].astype(F32)
    hh = pc_ref[:, 1024:1536].astype(F32)
    gc = pc_ref[:, 1536:2048].astype(F32)
    u = c * hh
    um1, um2 = _shift_down(u, halo_ref, has_prev)
    cv = w_ref[0:1, :] * um2 + w_ref[1:2, :] * um1 + w_ref[2:3, :] * u
    return b, c, hh, gc, u, um1, um2, cv


def _prev_rows(width, col=0):
    return pl.BlockSpec((HALO, width), lambda i: (jnp.maximum(i * (TM // HALO) - 1, 0), col))


def _out_loss(x, target, ya, pc, conv_w, w_out, final_g):
    S = x.shape[0]

    def body(x_ref, t_ref, ya_ref, pc_ref, halo_ref, cw_ref, wo_ref, fg_ref,
             dh_ref, dmix_ref, gwo_ref, gfg_ref, loss_ref, cw_out_ref):
        cw = _gathered_conv_w(cw_ref)

        @pl.when(pl.program_id(0) == 0)
        def _():
            gwo_ref[...] = jnp.zeros_like(gwo_ref)
            gfg_ref[...] = jnp.zeros_like(gfg_ref)
            loss_ref[...] = jnp.zeros_like(loss_ref)
            cw_out_ref[...] = cw[0:CONV_K, :]

        b, _, _, gc, _, _, _, cv = _conv_tile(pc_ref, halo_ref, cw, pl.program_id(0) > 0)
        yc = (b * cv * _silu(gc)).astype(BF16)
        mix = jnp.concatenate([ya_ref[...], yc], axis=1)
        wo = wo_ref[...]
        fg = fg_ref[...]
        h = x_ref[...] + _nn(mix, wo)
        r = lax.rsqrt(jnp.mean(h * h, axis=-1, keepdims=True) + EPS)
        n = h * r
        err = n * fg - t_ref[...]
        loss_ref[...] += jnp.broadcast_to(
            0.5 * jnp.sum(jnp.mean(err * err, axis=-1, keepdims=True), axis=0, keepdims=True), (8, 128))
        gfg_ref[...] += jnp.sum(err * n, axis=0, keepdims=True) * (1.0 / D_MODEL)
        dyg = err * (fg * (1.0 / D_MODEL))
        dh = r * (dyg - n * jnp.mean(dyg * n, axis=-1, keepdims=True))
        dh_ref[...] = dh
        dhb = dh.astype(BF16)
        dmix_ref[...] = _nt(dhb, wo).astype(ACT)
        gwo_ref[...] += _tn(mix, dhb)

    row = lambda i: (i, 0)
    fixed = lambda i: (0, 0)
    return pl.pallas_call(
        body,
        name="out_loss",
        grid=(S // TM,),
        in_specs=[
            pl.BlockSpec((TM, D_MODEL), row),
            pl.BlockSpec((TM, D_MODEL), row),
            pl.BlockSpec((TM, ATTN_W), row),
            pl.BlockSpec((TM, PC_W), row),
            _prev_rows(PC_W),
            pl.BlockSpec((N_DEV * 8, 128), fixed),
            pl.BlockSpec((D_MODEL, D_MODEL), fixed),
            pl.BlockSpec((1, D_MODEL), fixed),
        ],
        out_specs=[
            pl.BlockSpec((TM, D_MODEL), row),
            pl.BlockSpec((TM, D_MODEL), row),
            pl.BlockSpec((D_MODEL, D_MODEL), fixed),
            pl.BlockSpec((1, D_MODEL), fixed),
            pl.BlockSpec((8, 128), fixed),
            pl.BlockSpec((CONV_K, CONV_W), fixed),
        ],
        out_shape=[
            jax.ShapeDtypeStruct((S, D_MODEL), F32),
            jax.ShapeDtypeStruct((S, D_MODEL), ACT),
            jax.ShapeDtypeStruct((D_MODEL, D_MODEL), F32),
            jax.ShapeDtypeStruct((1, D_MODEL), F32),
            jax.ShapeDtypeStruct((8, 128), F32),
            jax.ShapeDtypeStruct((CONV_K, CONV_W), F32),
        ],
        compiler_params=_params(("arbitrary",)),
    )(x, target, ya, pc, pc, conv_w, w_out, final_g)


def _attn_bwd(pa, dmix, att, probs, psinks, q_stack, tab):
    S = pa.shape[0]
    nt = S // TQ
    nb = TQ // BLK

    def body(g_ref, kvc_ref, kvp_ref, tabc_ref, tabp_ref, dm_ref, att_ref, pm_ref, ps_ref, qs_ref,
             d_ref, dsink_ref, kall, dkv, carry, do_sc, p_sc, ds_sc, dsink_acc):
        step = pl.program_id(0)

        @pl.when(step == 0)
        def _():
            carry[...] = jnp.zeros_like(carry)
            dsink_acc[...] = jnp.zeros_like(dsink_acc)

        _fill_kv(kall, kvc_ref, kvp_ref, tabc_ref, tabp_ref)
        dkv[0:TQ, :] = jnp.zeros((TQ, 2 * KV_W), F32)
        dkv[TQ:TQ + BLK, :] = carry[...]
        lane = lax.broadcasted_iota(jnp.int32, (BLK, 128), 1)
        half = [lane < HEAD_DIM, lane >= HEAD_DIM]
        upper = _upper()
        upper_b = upper.astype(BF16)
        for j in range(nb):
            rq = slice(j * BLK, (j + 1) * BLK)
            rk = slice(j * BLK, (j + 2) * BLK)
            tab = tabc_ref[rq, :]
            pair = [slice(p * 128, (p + 1) * 128) for p in range(4)]
            g = [g_ref[rq, c].astype(F32) for c in pair]
            da = [dm_ref[rq, c].astype(F32) for c in pair]
            gate = [_silu_and_grad(g[p]) for p in range(4)]
            do = [da[p] * gate[p][0] for p in range(4)]
            dqs, dks, dvs = [], [], []
            for s in range(2):
                kk = kall[s, rk, :]
                vv = kall[2 + s, rk, :]
                _stack_heads(do_sc, s, half, do)
                pb = pm_ref[(2 * j + s) * BLK:(2 * j + s + 1) * BLK, :]
                prob = pb.astype(F32)
                _split_store(p_sc, s, upper_b, pb)
                dprob = _merge(upper, _nt(vv, do_sc[s]))
                dsum = jnp.sum(dprob * prob, axis=0, keepdims=True)
                _split_store(ds_sc, s, upper_b, (prob * (dprob - dsum)).astype(BF16))
                dsink_acc[s, 0:1, :] += ps_ref[2 * j + s:2 * j + s + 1, :] * dsum
                dqs.append(_tn(ds_sc[s], kk))
                dks.append(_nn(ds_sc[s], qs_ref[2 * j + s]))
                dvs.append(_nn(p_sc[s], do_sc[s]))
            for p in range(4):
                d_ref[rq, pair[p]] = _rope_t(_unstack_pair(half, dqs, p) * 0.125, tab).astype(BF16)
                d_ref[rq, 512 + p * 128:512 + (p + 1) * 128] = (
                    da[p] * att_ref[rq, pair[p]].astype(F32) * gate[p][1]).astype(BF16)
            dkv[rk, 0:128] += dks[0] + pltpu.roll(dks[1], 64, 1)
            dkv[rk, 128:256] += dvs[0] + pltpu.roll(dvs[1], 64, 1)
        d_ref[:, 1024:1152] = _rope_t(dkv[BLK:BLK + TQ, 0:128], tabc_ref[...]).astype(BF16)
        d_ref[:, 1152:1280] = dkv[BLK:BLK + TQ, 128:256].astype(BF16)
        carry[...] = dkv[0:BLK, :]

        @pl.when(step == nt - 1)
        def _():
            lanes = lax.broadcasted_iota(jnp.int32, (8, 128), 1)
            out = jnp.zeros((8, 128), F32)
            for s in range(2):
                for a, (p, e) in enumerate(HEADS[s]):
                    tot = jnp.sum(dsink_acc[s, 0:1, a * BLK:(a + 1) * BLK], axis=1, keepdims=True)
                    out = jnp.where(lanes == 2 * p + e, -tot, out)
            dsink_ref[...] = out

    rev = lambda s: nt - 1 - s
    return pl.pallas_call(
        body,
        name="attn_bwd",
        grid=(nt,),
        in_specs=_attn_specs(rev)[2:] + [pl.BlockSpec((TQ, ATTN_W), lambda s: (nt - 1 - s, 0))] * 2 + [
            pl.BlockSpec((2 * TQ, 4 * BLK), lambda s: (nt - 1 - s, 0)),
            pl.BlockSpec((2 * nb, 4 * BLK), lambda s: (nt - 1 - s, 0)),
            pl.BlockSpec((2 * nb, 4 * BLK, 128), lambda s: (nt - 1 - s, 0, 0)),
        ],
        out_specs=[
            pl.BlockSpec((TQ, PA_W), lambda s: (nt - 1 - s, 0)),
            pl.BlockSpec((8, 128), lambda s: (0, 0)),
        ],
        out_shape=[
            jax.ShapeDtypeStruct((S, PA_W), BF16),
            jax.ShapeDtypeStruct((8, 128), F32),
        ],
        scratch_shapes=[
            pltpu.VMEM((4, BLK + TQ, 128), BF16),
            pltpu.VMEM((BLK + TQ, 2 * KV_W), F32),
            pltpu.VMEM((BLK, 2 * KV_W), F32),
            pltpu.VMEM((2, 4 * BLK, 128), BF16),
            pltpu.VMEM((2, 2 * BLK, 4 * BLK), BF16),
            pltpu.VMEM((2, 2 * BLK, 4 * BLK), BF16),
            pltpu.VMEM((2, 8, 4 * BLK), F32),
        ],
        compiler_params=_params(("arbitrary",)),
    )(pa, pa, pa, tab, tab, dmix, att, probs, psinks, q_stack)


def _conv_bwd_tile(pc_ref, prev_ref, next_ref, dm_ref, dmn_ref, w_ref, d_ref, gw_ref, has_prev, has_next,
                   on_piece):
    rows = pc_ref.shape[0]
    w0, w1, w2 = w_ref[0:1, :], w_ref[1:2, :], w_ref[2:3, :]
    b, c, hh, gc, u, um1, um2, cv = _conv_tile(pc_ref, prev_ref, w_ref, has_prev)
    sg, dsg = _silu_and_grad(gc)
    dy = dm_ref[...].astype(F32)
    dyb = dy * b
    dcv = dyb * sg

    def next_dcv(r):
        nd = (dmn_ref[r:r + 1, :].astype(F32) * next_ref[r:r + 1, 0:512].astype(F32)
              * _silu(next_ref[r:r + 1, 1536:2048].astype(F32)))
        return jnp.where(has_next, nd, 0.0)

    row = lax.broadcasted_iota(jnp.int32, (rows, CONV_W), 0)
    dp1 = jnp.where(row == rows - 1, next_dcv(0), pltpu.roll(dcv, rows - 1, 0))
    dp2 = jnp.where(row == rows - 1, next_dcv(1),
                    jnp.where(row == rows - 2, next_dcv(0), pltpu.roll(dcv, rows - 2, 0)))
    du = w2 * dcv + w1 * dp1 + w0 * dp2
    pieces = (lambda: dy * cv * sg, lambda: du * hh, lambda: du * c, lambda: dyb * cv * dsg)
    for k, piece in enumerate(pieces):
        d_ref[:, k * CONV_W:(k + 1) * CONV_W] = piece().astype(BF16)
        on_piece(k)
    gw_ref[0:1, :] += jnp.sum(dcv * um2, axis=0, keepdims=True)
    gw_ref[1:2, :] += jnp.sum(dcv * um1, axis=0, keepdims=True)
    gw_ref[2:3, :] += jnp.sum(dcv * u, axis=0, keepdims=True)


def _grad_x(da, dc, wt, x, dh, norm_g, small, grads):
    S = x.shape[0]
    n_steps = S // TM
    rs = _ReduceScatter(grads)
    n_rs_out = len(rs.out_shape())
    small_rows = SMALL_ROWS

    def body(da_ref, dc_ref, wt_ref, x_ref, dh_ref, g_ref, gfg_ref, gsink_ref, loss_ref, gcw_ref, *rest):
        grad_refs, rest = rest[:rs.n], rest[rs.n:]
        gx_ref, all_ref = rest[:2]
        rs_out, rest = rest[2:2 + n_rs_out], rest[2 + n_rs_out:]
        gng, stage, small_send, small_recv, small_own = rest[:5]
        rs_scratch = rest[5:]
        step = pl.program_id(0)
        finish = rs.emit(step, n_steps, grad_refs, rs_out, rs_scratch)

        @pl.when(step == 0)
        def _():
            gng[...] = jnp.zeros_like(gng)

        dxn = (_nn(da_ref[:, 0:512], wt_ref[0:512, :]) + _nn(da_ref[:, 512:1024], wt_ref[768:1280, :])
               + _nn(da_ref[:, 1024:1280], wt_ref[512:768, :]) + _nn(dc_ref[...], wt_ref[1280:3328, :]))
        xv = x_ref[...]
        r = lax.rsqrt(jnp.mean(xv * xv, axis=-1, keepdims=True) + EPS)
        n = xv * r
        gng[...] += jnp.sum(dxn * n, axis=0, keepdims=True)
        dxg = dxn * g_ref[...]
        gx_ref[...] = dh_ref[...] + r * (dxg - n * jnp.mean(dxg * n, axis=-1, keepdims=True))

        @pl.when(step == n_steps - 1)
        def _():
            x_, y_, c_ = lax.axis_index("x"), lax.axis_index("y"), lax.axis_index("c")
            me = 4 * x_ + 2 * y_ + c_
            for q in range(8):
                stage[q:q + 1, :] = gng[:, q * 128:(q + 1) * 128]
                stage[8 + q:9 + q, :] = gfg_ref[:, q * 128:(q + 1) * 128]
            stage[16:24, :] = gsink_ref[...]
            stage[24:32, :] = loss_ref[...]
            stage[32:small_rows, :] = jnp.zeros((small_rows - 32, 128), F32)
            for j in range(N_DEV):
                piece = gcw_ref[:, (j // 2) * 128:(j // 2 + 1) * 128]
                if j % 2:
                    piece = pltpu.roll(piece, 64, 1)
                stage[32 + 8 * j:32 + 8 * j + CONV_K, 0:64] = piece[:, 0:64]
            own = pltpu.make_async_copy(stage, all_ref.at[me], small_own)
            own.start()
            sends = []
            for k in range(1, N_DEV):
                cp = pltpu.make_async_remote_copy(
                    src_ref=stage, dst_ref=all_ref.at[me],
                    send_sem=small_send.at[k - 1], recv_sem=small_recv.at[k - 1],
                    device_id=(x_ ^ (k >> 2), y_ ^ ((k >> 1) & 1), c_ ^ (k & 1)), device_id_type=MESH)
                cp.start()
                sends.append(cp)
            for cp in sends:
                cp.wait_send()
                cp.wait_recv()
            own.wait()

        finish()

    row = lambda i: (i, 0)
    fixed = lambda i: (0, 0)
    any_spec = pl.BlockSpec(memory_space=pl.ANY)
    outs = pl.pallas_call(
        body,
        name="grad_x_reduce_scatter",
        grid=(n_steps,),
        in_specs=[
            pl.BlockSpec((TM, PA_W), row),
            pl.BlockSpec((TM, PC_W), row),
            pl.BlockSpec((IN_W, D_MODEL), fixed),
            pl.BlockSpec((TM, D_MODEL), row),
            pl.BlockSpec((TM, D_MODEL), row),
            pl.BlockSpec((1, D_MODEL), fixed),
        ] + [pl.BlockSpec(a.shape, fixed) for a in small] + [any_spec] * rs.n,
        out_specs=[pl.BlockSpec((TM, D_MODEL), row), any_spec] + [any_spec] * n_rs_out,
        out_shape=[jax.ShapeDtypeStruct((S, D_MODEL), F32),
                   jax.ShapeDtypeStruct((N_DEV, small_rows, 128), F32)] + rs.out_shape(),
        scratch_shapes=[
            pltpu.VMEM((1, D_MODEL), F32),
            pltpu.VMEM((small_rows, 128), F32),
            pltpu.SemaphoreType.DMA((N_DEV - 1,)),
            pltpu.SemaphoreType.DMA((N_DEV - 1,)),
            pltpu.SemaphoreType.DMA,
        ] + rs.scratch_shapes(),
        compiler_params=_params(("arbitrary",)),
    )(da, dc, wt, x, dh, norm_g, *small, *grads)
    return outs[0], outs[1], outs[2:2 + rs.n], outs[2 + rs.n:2 + 2 * rs.n]


def _grad_w_in(da, pc, dmix, conv_w, xn):
    S = xn.shape[0]
    tm = 2 * TM
    nt = S // tm
    t16 = tm // HALO

    def body(da_ref, pc_ref, prev_ref, next_ref, dm_ref, dmn_ref, cw_ref, xn_ref, gw_ref, dc_ref, gcw_ref):
        i = pl.program_id(0)

        @pl.when(i == 0)
        def _():
            gw_ref[...] = jnp.zeros_like(gw_ref)
            gcw_ref[...] = jnp.zeros_like(gcw_ref)

        xn = xn_ref[...]
        gw_ref[0:512, :] += _tn(da_ref[:, 0:512], xn)
        gw_ref[768:1280, :] += _tn(da_ref[:, 512:1024], xn)
        gw_ref[512:768, :] += _tn(da_ref[:, 1024:1280], xn)
        def piece_grad(k):
            rows = slice(PA_W + k * CONV_W, PA_W + (k + 1) * CONV_W)
            gw_ref[rows, :] += _tn(dc_ref[:, k * CONV_W:(k + 1) * CONV_W], xn)

        _conv_bwd_tile(pc_ref, prev_ref, next_ref, dm_ref, dmn_ref, cw_ref, dc_ref, gcw_ref, i > 0, i < nt - 1,
                       piece_grad)

    row = lambda i: (i, 0)
    fixed = lambda i: (0, 0)
    nxt = lambda i: jnp.minimum((i + 1) * t16, nt * t16 - 1)
    return pl.pallas_call(
        body,
        name="grad_w_in",
        grid=(nt,),
        in_specs=[
            pl.BlockSpec((tm, PA_W), row),
            pl.BlockSpec((tm, PC_W), row),
            pl.BlockSpec((HALO, PC_W), lambda i: (jnp.maximum(i * t16 - 1, 0), 0)),
            pl.BlockSpec((HALO, PC_W), lambda i: (nxt(i), 0)),
            pl.BlockSpec((tm, CONV_W), lambda i: (i, 1)),
            pl.BlockSpec((HALO, CONV_W), lambda i: (nxt(i), 1)),
            pl.BlockSpec((CONV_K, CONV_W), fixed),
            pl.BlockSpec((tm, D_MODEL), row),
        ],
        out_specs=[
            pl.BlockSpec((IN_W, D_MODEL), fixed, pipeline_mode=pl.Buffered(1)),
            pl.BlockSpec((tm, PC_W), row),
            pl.BlockSpec((CONV_K, CONV_W), fixed),
        ],
        out_shape=[
            jax.ShapeDtypeStruct((IN_W, D_MODEL), F32),
            jax.ShapeDtypeStruct((S, PC_W), BF16),
            jax.ShapeDtypeStruct((CONV_K, CONV_W), F32),
        ],
        compiler_params=_params(("arbitrary",)),
    )(da, pc, pc, pc, dmix, dmix, conv_w, xn)


def _adam_update(w, g, m, v):
    c1 = 1.0 - ADAM_B1 ** ADAM_STEP
    c2 = 1.0 - ADAM_B2 ** ADAM_STEP
    nm = ADAM_B1 * m + (1.0 - ADAM_B1) * g
    nv = ADAM_B2 * v + (1.0 - ADAM_B2) * (g * g)
    return -ADAM_LR * ((nm / c1) / (jnp.sqrt(nv / c2) + ADAM_EPS) + ADAM_WD * w), nm, nv


SMALL_ROWS = 96


def _small_adamw_body(parts_ref, prm, outs, total):
    me = 4 * lax.axis_index("x") + 2 * lax.axis_index("y") + lax.axis_index("c")
    acc = parts_ref[0]
    for d in range(1, N_DEV):
        acc = acc + parts_ref[d]
    total[...] = acc
    grads = (total[0:8, :], total[8:16, :], total[16:17, 0:8],
             total[pl.ds(pl.multiple_of(32 + me * 8, 8), CONV_K), 0:64])
    outs[0][...] = total[24:25, 0:1]
    for k, g in enumerate(grads):
        w_ref, m_ref, v_ref = prm[3 * k:3 * k + 3]
        g_ref, d_ref, nm_ref, nv_ref = outs[1 + 4 * k:5 + 4 * k]
        g_ref[...] = g
        d_ref[...], nm_ref[...], nv_ref[...] = _adam_update(w_ref[...], g, m_ref[...], v_ref[...])


def _sum_chips_adamw(own, others, w, m, v, name, small=None):
    small_in = [small[0]] + [a for p in small[1] for a in p] if small else []
    n_in = len(small_in)

    def body(own_ref, p_ref, w_ref, m_ref, v_ref, *rest):
        extra, rest = rest[:n_in], rest[n_in:]
        g_ref, d_ref, nm_ref, nv_ref = rest[:4]
        g = own_ref[...]
        for k in range(N_CHIP - 1):
            g = g + p_ref[k].astype(F32)
        g_ref[...] = g
        d_ref[...], nm_ref[...], nv_ref[...] = _adam_update(w_ref[...], g, m_ref[...], v_ref[...])
        if small:
            @pl.when(pl.program_id(0) == 0)
            def _():
                _small_adamw_body(extra[0], extra[1:], rest[4:-1], rest[-1])

    rows, cols = w.shape
    half = rows // 2
    blk = pl.BlockSpec((half, cols), lambda i: (i, 0))
    whole = lambda shape: pl.BlockSpec(shape, lambda i: (0,) * len(shape))
    shape = jax.ShapeDtypeStruct(w.shape, F32)
    small_out = []
    if small:
        small_out = [(1, 1)] + [p[0].shape for p in small[1] for _ in range(4)]
    return pl.pallas_call(
        body,
        name=name,
        grid=(2,),
        in_specs=[blk, pl.BlockSpec((N_CHIP - 1, half, cols), lambda i: (0, i, 0)), blk, blk, blk]
        + [whole(a.shape) for a in small_in],
        out_specs=[blk] * 4 + [whole(s) for s in small_out],
        out_shape=[shape] * 4 + [jax.ShapeDtypeStruct(s, F32) for s in small_out],
        scratch_shapes=[pltpu.VMEM((SMALL_ROWS, 128), F32)] if small else [],
        compiler_params=_params(("arbitrary",)),
    )(own, others, w, m, v, *small_in)


def kernel(x, norm_g, w_in, sinks, conv_w, w_out, final_g, loss_target, m_norm_g, m_w_in, m_sinks, m_conv_w, m_w_out, m_final_g, v_norm_g, v_w_in, v_sinks, v_conv_w, v_w_out, v_final_g):
    S = x.shape[1]
    x2 = x.reshape(S, D_MODEL)
    t2 = loss_target.reshape(S, D_MODEL)
    ng = norm_g.reshape(1, D_MODEL)
    fg = final_g.reshape(1, D_MODEL)

    xn, tab, wt = _prologue(x2, ng, w_in.T)
    pa, pc, (wo, cw) = _fwd_proj(xn, wt, [w_out, conv_w])
    ya, att, probs, psinks, q_stack = _attn_fwd(pa, tab, sinks)
    dh, dmix, g_wo, g_fg, loss_part, cw3 = _out_loss(x2, t2, ya, pc, cw, wo, fg)
    da, g_sinks = _attn_bwd(pa, dmix, att, probs, psinks, q_stack, tab)
    g_wt, dc, g_cw = _grad_w_in(da, pc, dmix, cw3, xn)
    grad_x, parts, own, others = _grad_x(
        da, dc, wt, x2, dh, ng, (g_fg, g_sinks, loss_part, g_cw),
        [g_wt.reshape(N_DEV, SHARD_IN, D_MODEL), g_wo.reshape(N_DEV, SHARD_OUT, D_MODEL)])
    gt, dt, nmt, nvt = _sum_chips_adamw(own[0], others[0], w_in.T, m_w_in.T, v_w_in.T, "adamw_w_in")
    grad_w_in, d_w_in, nm_w_in, nv_w_in = gt.T, dt.T, nmt.T, nvt.T
    vec = lambda a: a.reshape(8, 128)
    row = lambda a: a.reshape(1, 8)
    grad_w_out, d_w_out, nm_w_out, nv_w_out, *res = _sum_chips_adamw(
        own[1], others[1], w_out, m_w_out, v_w_out, "adamw_w_out_and_small", small=(parts, [
            (vec(norm_g), vec(m_norm_g), vec(v_norm_g)), (vec(final_g), vec(m_final_g), vec(v_final_g)),
            (row(sinks), row(m_sinks), row(v_sinks)), (conv_w, m_conv_w, v_conv_w)]))
    loss = res[0].reshape(())
    grad_norm_g, d_ng, nm_ng, nv_ng = [a.reshape(D_MODEL) for a in res[1:5]]
    grad_final_g, d_fg, nm_fg, nv_fg = [a.reshape(D_MODEL) for a in res[5:9]]
    grad_sinks, d_sk, nm_sk, nv_sk = [a.reshape(N_Q_HEADS) for a in res[9:13]]
    grad_conv_w, d_cw, nm_cw, nv_cw = res[13:17]

    return (loss, grad_x.reshape(1, S, D_MODEL), grad_norm_g, grad_w_in, grad_sinks, grad_conv_w, grad_w_out, grad_final_g,
            d_ng, d_w_in, d_sk, d_cw, d_w_out, d_fg,
            nm_ng, nm_w_in, nm_sk, nm_cw, nm_w_out, nm_fg,
            nv_ng, nv_w_in, nv_sk, nv_cw, nv_w_out, nv_fg)
```

```python
import numpy as np
import jax
import jax.numpy as jnp
from jax import lax
from jax.experimental import pallas as pl
from jax.experimental.pallas import tpu as pltpu

F32 = jnp.float32
BF16 = jnp.bfloat16
MESH = pl.DeviceIdType.MESH

D_MODEL = 1024
HEAD_DIM = 64
N_Q_HEADS = 8
ATTN_W = 512
KV_W = 128
BLK = 128
CONV_W = 512
CONV_K = 3
IN_W = 3328
PA_W = 1280
PC_W = 2048
EPS = 1e-5
ROPE_THETA = 500000.0
ROT_DIM = 16
N_DEV = 8
N_CHIP = 4
SHARD_IN = IN_W // N_DEV
SHARD_OUT = D_MODEL // N_DEV

ADAM_LR = 0.001
ADAM_B1 = 0.9
ADAM_B2 = 0.999
ADAM_EPS = 1e-08
ADAM_WD = 0.01
ADAM_STEP = 10

ACT = jnp.bfloat16

TM = 512
TQ = 1024
HALO = 16
VMEM_LIMIT = 56 * 1024 * 1024

NT_DIMS = (((1,), (1,)), ((), ()))
TN_DIMS = (((0,), (0,)), ((), ()))


def _params(sem=None):
    kw = dict(vmem_limit_bytes=VMEM_LIMIT)
    if sem is not None:
        kw["dimension_semantics"] = sem
    return pltpu.CompilerParams(**kw)


def _nt(a, b):
    return lax.dot_general(a, b, NT_DIMS, preferred_element_type=F32)


def _tn(a, b):
    return lax.dot_general(a, b, TN_DIMS, preferred_element_type=F32)


def _nn(a, b):
    return jnp.dot(a, b, preferred_element_type=F32)


def _silu(g):
    return g * jax.nn.sigmoid(g)


def _silu_and_grad(g):
    s = jax.nn.sigmoid(g)
    return g * s, s * (1.0 + g * (1.0 - s))


class _AllGatherInSteps:
    def __init__(self, arrs, forward_step):
        self.blocks = [(a.shape, a.dtype) for a in arrs]
        self.n = len(arrs)
        self.forward_step = forward_step

    def out_shape(self):
        return [jax.ShapeDtypeStruct((N_DEV * s[0], s[1]), d) for s, d in self.blocks]

    def scratch_shapes(self):
        return [pltpu.SemaphoreType.DMA((7 * self.n,)), pltpu.SemaphoreType.DMA((7 * self.n,)),
                pltpu.SemaphoreType.DMA((self.n,))]

    def emit(self, step, n_steps, x_refs, out_refs, scratch):
        assert n_steps > self.forward_step + 1
        send_sems, recv_sems, local_sems = scratch
        x, y, c = lax.axis_index("x"), lax.axis_index("y"), lax.axis_index("c")
        me, sibling = (x, y, c), (x, y, 1 - c)
        chips = [(1 - x, y), (x, 1 - y), (1 - x, 1 - y)]

        def rows(a, px, py, pc):
            m = self.blocks[a][0][0]
            return out_refs[a].at[pl.ds((4 * px + 2 * py + pc) * m, m), :]

        def copy(a, k, block, to, src=None):
            return pltpu.make_async_remote_copy(
                src_ref=rows(a, *block) if src is None else src, dst_ref=rows(a, *block),
                send_sem=send_sems.at[a * 7 + k], recv_sem=recv_sems.at[a * 7 + k],
                device_id=to, device_id_type=MESH)

        def mine(a):
            return pltpu.make_async_copy(x_refs[a], rows(a, *me), local_sems.at[a])

        def first(a):
            return ([copy(a, 0, me, sibling, src=x_refs[a])]
                    + [copy(a, 1 + j, me, (*chip, c), src=x_refs[a]) for j, chip in enumerate(chips)])

        def passed(a):
            return [copy(a, 4 + j, (*chip, c), sibling) for j, chip in enumerate(chips)]

        @pl.when(step == 0)
        def _():
            for a in range(self.n):
                mine(a).start()
                for cp in first(a):
                    cp.start()

        @pl.when(step == self.forward_step)
        def _():
            for j, chip in enumerate(chips):
                for a in range(self.n):
                    copy(a, 1 + j, (*chip, c), me).wait_recv()
                    copy(a, 4 + j, (*chip, c), sibling).start()

        def finish():
            @pl.when(step == n_steps - 1)
            def _():
                for a in range(self.n):
                    copy(a, 0, sibling, me).wait_recv()
                    for j, chip in enumerate(chips):
                        copy(a, 4 + j, (*chip, 1 - c), me).wait_recv()
                    for cp in first(a) + passed(a):
                        cp.wait_send()
                    mine(a).wait()

        return finish


class _AllGatherViaNeighbours:
    def __init__(self, arr, first, mid, second):
        (self.m, self.ncol), self.dtype = arr.shape, arr.dtype
        assert self.m % 32 == 0
        self.first, self.mid, self.second = first, mid, second

    def out_shape(self):
        return [jax.ShapeDtypeStruct((N_DEV * self.m, self.ncol), self.dtype)]

    def scratch_shapes(self):
        return [pltpu.SemaphoreType.DMA((11,)), pltpu.SemaphoreType.DMA((11,)), pltpu.SemaphoreType.DMA]

    def emit(self, step, n_steps, x_ref, out_ref, scratch):
        assert 0 < self.first < self.mid < self.second < n_steps - 1
        send_sems, recv_sems, local_sem = scratch
        x, y, c = lax.axis_index("x"), lax.axis_index("y"), lax.axis_index("c")
        half = self.m // 2
        sibling, xn, yn = (x, y, 1 - c), (1 - x, y, c), (x, 1 - y, c)

        def rows(dev, part=None):
            px, py, pc = dev
            base = (4 * px + 2 * py + pc) * self.m
            if part is None:
                return out_ref.at[pl.ds(base, self.m), :]
            return out_ref.at[pl.ds(base + part * half, half), :]

        def copy(k, dev, to, part=None, src=None):
            return pltpu.make_async_remote_copy(
                src_ref=rows(dev, part) if src is None else src, dst_ref=rows(dev, part),
                send_sem=send_sems.at[k], recv_sem=recv_sems.at[k], device_id=to, device_id_type=MESH)

        me, dg = (x, y, c), (1 - x, 1 - y, c)
        mine = pltpu.make_async_copy(x_ref, rows(me), local_sem)
        my_half = lambda part: x_ref.at[pl.ds(part * half, half), :]
        sends = [
            copy(0, me, sibling, src=x_ref), copy(1, me, xn, part=0, src=my_half(0)),
            copy(2, me, yn, part=1, src=my_half(1)), copy(3, xn, yn, part=0), copy(4, yn, xn, part=1),
            copy(5, xn, sibling), copy(6, yn, sibling), copy(7, dg, sibling, part=0), copy(8, dg, sibling, part=1),
            copy(9, me, xn, part=1, src=my_half(1)), copy(10, me, yn, part=0, src=my_half(0)),
        ]
        other = lambda dev: (dev[0], dev[1], 1 - c)
        arrivals = [
            copy(0, other(me), sibling), copy(1, xn, xn, part=0), copy(2, yn, yn, part=1), copy(3, dg, yn, part=0),
            copy(4, dg, xn, part=1), copy(5, other(xn), sibling), copy(6, other(yn), sibling),
            copy(7, other(dg), sibling, part=0), copy(8, other(dg), sibling, part=1),
            copy(9, xn, xn, part=1), copy(10, yn, yn, part=0),
        ]

        @pl.when(step == 0)
        def _():
            mine.start()
            for k in (0, 1, 2, 9, 10):
                sends[k].start()

        @pl.when(step == self.first)
        def _():
            arrivals[1].wait_recv()
            sends[3].start()
            arrivals[2].wait_recv()
            sends[4].start()

        @pl.when(step == self.mid)
        def _():
            arrivals[9].wait_recv()
            sends[5].start()
            arrivals[10].wait_recv()
            sends[6].start()

        @pl.when(step == self.second)
        def _():
            arrivals[3].wait_recv()
            sends[7].start()
            arrivals[4].wait_recv()
            sends[8].start()

        def finish():
            @pl.when(step == n_steps - 1)
            def _():
                for k in (0, 5, 6, 7, 8):
                    arrivals[k].wait_recv()
                for cp in sends:
                    cp.wait_send()
                mine.wait()

        return finish


class _ReduceScatter:
    def __init__(self, grads):
        self.shapes = [g.shape[1:] for g in grads]
        self.n = len(grads)
        self.items = tuple((a, r) for r in (1, 2, 3, 0) for a in range(self.n))
        self.steps = N_CHIP + 2

    def out_shape(self):
        own = [jax.ShapeDtypeStruct(s, F32) for s in self.shapes]
        ici = [jax.ShapeDtypeStruct((N_CHIP - 1,) + s, BF16) for s in self.shapes]
        land = [jax.ShapeDtypeStruct((N_CHIP,) + s, F32) for s in self.shapes]
        return own + ici + land

    def scratch_shapes(self):
        n_items = len(self.items)
        return ([pltpu.VMEM((2,) + s, F32) for s in self.shapes]
                + [pltpu.VMEM((N_CHIP - 1,) + s, BF16) for s in self.shapes]
                + [pltpu.VMEM(s, F32) for s in self.shapes]
                + [pltpu.SemaphoreType.DMA((self.n * N_CHIP,))] * 2
                + [pltpu.SemaphoreType.DMA((2 * n_items,))]
                + [pltpu.SemaphoreType.DMA((self.n * (N_CHIP - 1),))] * 2
                + [pltpu.SemaphoreType.DMA((self.n,))])

    def emit(self, step, n_steps, g_refs, out_refs, scratch):
        assert n_steps > self.steps
        n = self.n
        own_refs, ici_refs, land_refs = out_refs[:n], out_refs[n:2 * n], out_refs[2 * n:]
        stage, pair_bf, pair_own = scratch[:n], scratch[n:2 * n], scratch[2 * n:3 * n]
        sib_send, sib_recv, load_sems, ici_send, ici_recv, own_sems = scratch[3 * n:]
        x, y, c = lax.axis_index("x"), lax.axis_index("y"), lax.axis_index("c")

        def chip_of(r):
            return (x ^ (r >> 1), y ^ (r & 1))

        def block_of(r, core):
            cx, cy = chip_of(r)
            return 4 * cx + 2 * cy + core

        def to_sibling(a, r):
            return pltpu.make_async_remote_copy(
                src_ref=g_refs[a].at[block_of(r, 1 - c)], dst_ref=land_refs[a].at[r],
                send_sem=sib_send.at[a * N_CHIP + r], recv_sem=sib_recv.at[a * N_CHIP + r],
                device_id=(x, y, 1 - c), device_id_type=MESH)

        def loads(k):
            a, r = self.items[k]
            return (pltpu.make_async_copy(g_refs[a].at[block_of(r, c)], stage[a].at[0], load_sems.at[2 * k]),
                    pltpu.make_async_copy(land_refs[a].at[r], stage[a].at[1], load_sems.at[2 * k + 1]))

        def to_owner(k):
            a, r = self.items[k]
            if r == 0:
                return pltpu.make_async_copy(pair_own[a], own_refs[a], own_sems.at[a])
            return pltpu.make_async_remote_copy(
                src_ref=pair_bf[a].at[r - 1], dst_ref=ici_refs[a].at[r - 1],
                send_sem=ici_send.at[a * (N_CHIP - 1) + r - 1], recv_sem=ici_recv.at[a * (N_CHIP - 1) + r - 1],
                device_id=(*chip_of(r), c), device_id_type=MESH)

        @pl.when(step == 0)
        def _():
            for a, r in self.items:
                to_sibling(a, r).start()

        def fetch(k):
            a, r = self.items[k]
            to_sibling(a, r).wait_recv()
            for cp in loads(k):
                cp.start()

        def add_and_send(k):
            a, r = self.items[k]
            for cp in loads(k):
                cp.wait()
            total = stage[a][0] + stage[a][1]
            if r == 0:
                pair_own[a][...] = total
            else:
                pair_bf[a][r - 1] = total.astype(BF16)
            to_owner(k).start()

        for g in range(N_CHIP + 1):
            @pl.when(step == 1 + g)
            def _(g=g):
                if g > 0:
                    for k in range((g - 1) * n, g * n):
                        add_and_send(k)
                if g < N_CHIP:
                    for k in range(g * n, (g + 1) * n):
                        fetch(k)

        def finish():
            @pl.when(step == n_steps - 1)
            def _():
                for k, (a, r) in enumerate(self.items):
                    if r == 0:
                        to_owner(k).wait()
                    else:
                        to_owner(k).wait_send()
                        to_owner(k).wait_recv()
                for a, r in self.items:
                    to_sibling(a, r).wait_send()

        return finish


def _prologue(x, norm_g, w_shard):
    S = x.shape[0]
    n_steps = S // TM
    half = ROT_DIM // 2
    pos = np.arange(S, dtype=np.float32)
    inv_freq = np.float32(ROPE_THETA) ** (-np.arange(0, ROT_DIM, 2, dtype=np.float32) / np.float32(ROT_DIM))
    ang = inv_freq.astype(np.float32)[:, None] * pos[None, :]
    cs = jnp.asarray(np.concatenate([np.cos(ang), np.sin(ang)], axis=0).astype(np.float32))
    ag = _AllGatherViaNeighbours(jax.ShapeDtypeStruct(w_shard.shape, BF16),
                                 first=n_steps // 4, mid=n_steps // 2 + 2, second=n_steps - 2)

    def body(x_ref, g_ref, cs_ref, w_ref, xn_ref, tab_ref, wt_ref, w_bf, *ag_scratch):
        step = pl.program_id(0)

        @pl.when(step == 0)
        def _():
            w_bf[...] = w_ref[...].astype(BF16)

        finish = ag.emit(step, n_steps, w_bf, wt_ref, ag_scratch)
        xv = x_ref[...]
        r = lax.rsqrt(jnp.mean(xv * xv, axis=-1, keepdims=True) + EPS)
        xn_ref[...] = (xv * r * g_ref[...]).astype(BF16)

        xt = jnp.concatenate([cs_ref[...], jnp.zeros((128 - 2 * half, TM), F32)], axis=0).T
        lane = lax.broadcasted_iota(jnp.int32, (TM, 128), 1)
        rr = lane & (HEAD_DIM - 1)
        first = lane < HEAD_DIM

        def at(shift_first, shift_second):
            return jnp.where(first, pltpu.roll(xt, shift_first, 1) if shift_first else xt,
                             pltpu.roll(xt, shift_second, 1))

        cos_lo, cos_hi = at(0, HEAD_DIM), at(half, HEAD_DIM + half)
        sin_lo, sin_hi = at(128 - half, HEAD_DIM - half), at(0, HEAD_DIM)
        tab_ref[:, 0:128] = jnp.where(rr < half, cos_lo, jnp.where(rr < ROT_DIM, cos_hi, 1.0))
        tab_ref[:, 128:256] = jnp.where(rr < half, -sin_lo, 0.0)
        tab_ref[:, 256:384] = jnp.where((rr >= half) & (rr < ROT_DIM), sin_hi, 0.0)
        finish()

    any_spec = pl.BlockSpec(memory_space=pl.ANY)
    return pl.pallas_call(
        body,
        name="prologue_all_gather_w_in",
        grid=(n_steps,),
        in_specs=[
            pl.BlockSpec((TM, D_MODEL), lambda i: (i, 0)),
            pl.BlockSpec((1, D_MODEL), lambda i: (0, 0)),
            pl.BlockSpec((2 * half, TM), lambda i: (0, i)),
            pl.BlockSpec(w_shard.shape, lambda i: (0, 0)),
        ],
        out_specs=[
            pl.BlockSpec((TM, D_MODEL), lambda i: (i, 0)),
            pl.BlockSpec((TM, 384), lambda i: (i, 0)),
            any_spec,
        ],
        out_shape=[
            jax.ShapeDtypeStruct((S, D_MODEL), BF16),
            jax.ShapeDtypeStruct((S, 384), F32),
        ] + ag.out_shape(),
        scratch_shapes=[pltpu.VMEM(w_shard.shape, BF16)] + ag.scratch_shapes(),
        compiler_params=_params(("arbitrary",)),
    )(x, norm_g, cs, w_shard)


def _fwd_proj(xn, wt, later):
    S = xn.shape[0]
    tm = 2 * TM
    n_steps = S // tm
    ag = _AllGatherInSteps([jax.ShapeDtypeStruct(later[0].shape, BF16), jax.ShapeDtypeStruct((8, 128), F32)],
                           forward_step=n_steps // 2)

    def body(xn_ref, wt_ref, *rest):
        later_refs, rest = rest[:ag.n], rest[ag.n:]
        pa_ref, pc_ref = rest[:2]
        gathered, w_bf, cw_pad, ag_scratch = rest[2:2 + ag.n], rest[2 + ag.n], rest[3 + ag.n], rest[4 + ag.n:]
        step = pl.program_id(0)

        @pl.when(step == 0)
        def _():
            w_bf[...] = later_refs[0][...].astype(BF16)
            cw_pad[...] = jnp.zeros_like(cw_pad)
            cw_pad[0:CONV_K, 0:64] = later_refs[1][...]

        finish = ag.emit(step, n_steps, (w_bf, cw_pad), gathered, ag_scratch)
        xn = xn_ref[...]
        pa_ref[:, 0:512] = _nt(xn, wt_ref[0:512, :]).astype(ACT)
        pa_ref[:, 512:1024] = _nt(xn, wt_ref[768:1280, :]).astype(ACT)
        pa_ref[:, 1024:1280] = _nt(xn, wt_ref[512:768, :]).astype(ACT)
        pc_ref[...] = _nt(xn, wt_ref[1280:3328, :]).astype(ACT)
        finish()

    any_spec = pl.BlockSpec(memory_space=pl.ANY)
    outs = pl.pallas_call(
        body,
        name="fwd_proj_all_gather",
        grid=(n_steps,),
        in_specs=[
            pl.BlockSpec((tm, D_MODEL), lambda i: (i, 0)),
            pl.BlockSpec((IN_W, D_MODEL), lambda i: (0, 0)),
            pl.BlockSpec(later[0].shape, lambda i: (0, 0)),
            pl.BlockSpec(later[1].shape, lambda i: (0, 0)),
        ],
        out_specs=[
            pl.BlockSpec((tm, PA_W), lambda i: (i, 0)),
            pl.BlockSpec((tm, PC_W), lambda i: (i, 0)),
        ] + [any_spec] * ag.n,
        out_shape=[
            jax.ShapeDtypeStruct((S, PA_W), ACT),
            jax.ShapeDtypeStruct((S, PC_W), ACT),
        ] + ag.out_shape(),
        scratch_shapes=[pltpu.VMEM(later[0].shape, BF16), pltpu.VMEM((8, 128), F32)] + ag.scratch_shapes(),
        compiler_params=_params(("arbitrary",)),
    )(xn, wt, *later)
    return outs[0], outs[1], outs[2:]


def _rope(t, tab):
    return (t * tab[:, 0:128] + pltpu.roll(t, 120, 1) * tab[:, 128:256]
            + pltpu.roll(t, 8, 1) * tab[:, 256:384])


def _rope_t(d, tab):
    return (d * tab[:, 0:128] + pltpu.roll(d * tab[:, 128:256], 8, 1)
            + pltpu.roll(d * tab[:, 256:384], 120, 1))


def _fill_kv(kall, kvc_ref, kvp_ref, tabc_ref, tabp_ref):
    for lo, kv_ref, tab_ref, n in ((0, kvp_ref, tabp_ref, BLK), (BLK, kvc_ref, tabc_ref, TQ)):
        k = _rope(kv_ref[:, 0:128].astype(F32), tab_ref[...])
        v = kv_ref[:, 128:256].astype(F32)
        kall[0, lo:lo + n, :] = k.astype(BF16)
        kall[1, lo:lo + n, :] = pltpu.roll(k, 64, 1).astype(BF16)
        kall[2, lo:lo + n, :] = v.astype(BF16)
        kall[3, lo:lo + n, :] = pltpu.roll(v, 64, 1).astype(BF16)


HEADS = (((0, 0), (1, 0), (2, 1), (3, 1)), ((0, 1), (1, 1), (2, 0), (3, 0)))


def _upper():
    kj = lax.broadcasted_iota(jnp.int32, (BLK, 4 * BLK), 0)
    qi = lax.broadcasted_iota(jnp.int32, (BLK, 4 * BLK), 1) & (BLK - 1)
    return kj > qi


def _merge(upper, both):
    return jnp.where(upper, both[0:BLK, :], both[BLK:2 * BLK, :])


def _split_store(ref, s, upper_b, vb):
    first = vb * upper_b
    ref[s, 0:BLK, :] = first
    ref[s, BLK:2 * BLK, :] = vb - first


def _sink_rows(sink_ref):
    return [jnp.concatenate([jnp.full((1, BLK), sink_ref[2 * p + e], F32) for p, e in HEADS[s]], axis=1)
            for s in range(2)]


def _stack_heads(ref, slot, half, pairs, s=None):
    for a, (p, e) in enumerate(HEADS[slot if s is None else s]):
        ref[slot, a * BLK:(a + 1) * BLK, :] = jnp.where(half[e], pairs[p], 0.0).astype(BF16)


def _unstack_pair(half, outs, p):
    lo = 0 if p < 2 else 1
    rows = slice(p * BLK, (p + 1) * BLK)
    return jnp.where(half[0], outs[lo][rows, :], outs[1 - lo][rows, :])


def _softmax(sm, sinks):
    m = jnp.maximum(jnp.max(sm, axis=0, keepdims=True), sinks)
    p = jnp.exp(sm - m)
    es = jnp.exp(sinks - m)
    inv = 1.0 / (jnp.sum(p, axis=0, keepdims=True) + es)
    return p * inv, es * inv


def _scores(kk, q_stack, first):
    st = _nt(kk, q_stack)
    prev = st[0:BLK, :]
    if first is not None:
        prev = prev + jnp.where(first, -jnp.inf, 0.0)
    return prev, st[BLK:2 * BLK, :]


def _attn_specs(tile):
    nb = TQ // BLK
    prev = lambda i: jnp.maximum(tile(i) * nb - 1, 0)
    return [
        pl.BlockSpec(memory_space=pltpu.SMEM),
        pl.BlockSpec((TQ, ATTN_W), lambda i: (tile(i), 0)),
        pl.BlockSpec((TQ, ATTN_W), lambda i: (tile(i), 1)),
        pl.BlockSpec((TQ, 2 * KV_W), lambda i: (tile(i), 4)),
        pl.BlockSpec((BLK, 2 * KV_W), lambda i: (prev(i), 4)),
        pl.BlockSpec((TQ, 384), lambda i: (tile(i), 0)),
        pl.BlockSpec((BLK, 384), lambda i: (prev(i), 0)),
    ]


def _attn_fwd(pa, tab, sinks):
    S = pa.shape[0]
    nb = TQ // BLK

    def body(sink_ref, q_ref, g_ref, kvc_ref, kvp_ref, tabc_ref, tabp_ref, o_ref, att_ref, pm_ref, ps_ref,
             qs_ref, kall, p_sc):
        i = pl.program_id(0)
        _fill_kv(kall, kvc_ref, kvp_ref, tabc_ref, tabp_ref)
        lane = lax.broadcasted_iota(jnp.int32, (BLK, 128), 1)
        half = [lane < HEAD_DIM, lane >= HEAD_DIM]
        upper = _upper()
        upper_b = upper.astype(BF16)
        sinks = _sink_rows(sink_ref)
        for j in range(nb):
            rq = slice(j * BLK, (j + 1) * BLK)
            rk = slice(j * BLK, (j + 2) * BLK)
            tab = tabc_ref[rq, :]
            qr = [_rope(q_ref[rq, p * 128:(p + 1) * 128].astype(F32), tab) * 0.125 for p in range(4)]
            outs = []
            for s in range(2):
                _stack_heads(qs_ref, 2 * j + s, half, qr, s)
                prev, cur = _scores(kall[s, rk, :], qs_ref[2 * j + s], i == 0 if j == 0 else None)
                prob, psink = _softmax(jnp.where(upper, prev, cur), sinks[s])
                pb = prob.astype(BF16)
                pm_ref[(2 * j + s) * BLK:(2 * j + s + 1) * BLK, :] = pb
                ps_ref[2 * j + s:2 * j + s + 1, :] = psink
                _split_store(p_sc, s, upper_b, pb)
                outs.append(_tn(p_sc[s], kall[2 + s, rk, :]))
            for p in range(4):
                cols = slice(p * 128, (p + 1) * 128)
                att = _unstack_pair(half, outs, p)
                att_ref[rq, cols] = att.astype(BF16)
                o_ref[rq, cols] = (att * _silu(g_ref[rq, cols].astype(F32))).astype(BF16)

    return pl.pallas_call(
        body,
        name="attn_fwd",
        grid=(S // TQ,),
        in_specs=_attn_specs(lambda i: i),
        out_specs=[pl.BlockSpec((TQ, ATTN_W), lambda i: (i, 0))] * 2 + [
            pl.BlockSpec((2 * TQ, 4 * BLK), lambda i: (i, 0)),
            pl.BlockSpec((2 * nb, 4 * BLK), lambda i: (i, 0)),
            pl.BlockSpec((2 * nb, 4 * BLK, 128), lambda i: (i, 0, 0)),
        ],
        out_shape=[jax.ShapeDtypeStruct((S, ATTN_W), BF16)] * 2 + [
            jax.ShapeDtypeStruct((2 * S, 4 * BLK), BF16),
            jax.ShapeDtypeStruct((2 * S // BLK, 4 * BLK), F32),
            jax.ShapeDtypeStruct((2 * S // BLK, 4 * BLK, 128), BF16),
        ],
        scratch_shapes=[
            pltpu.VMEM((4, BLK + TQ, 128), BF16),
            pltpu.VMEM((2, 2 * BLK, 4 * BLK), BF16),
        ],
        compiler_params=_params(("arbitrary",)),
    )(sinks, pa, pa, pa, pa, tab, tab)


def _shift_down(u, halo_ref, has_prev):
    def halo_u(r):
        hu = halo_ref[r:r + 1, 512:1024].astype(F32) * halo_ref[r:r + 1, 1024:1536].astype(F32)
        return jnp.where(has_prev, hu, 0.0)

    row = lax.broadcasted_iota(jnp.int32, u.shape, 0)
    um1 = jnp.where(row == 0, halo_u(HALO - 1), pltpu.roll(u, 1, 0))
    um2 = jnp.where(row == 0, halo_u(HALO - 2), jnp.where(row == 1, halo_u(HALO - 1), pltpu.roll(u, 2, 0)))
    return um1, um2


def _gathered_conv_w(all_ref):
    pairs = [all_ref[16 * p:16 * p + 8, :] + pltpu.roll(all_ref[16 * p + 8:16 * p + 16, :], 64, 1)
             for p in range(N_DEV // 2)]
    return jnp.concatenate(pairs, axis=1)


def _conv_tile(pc_ref, halo_ref, w_ref, has_prev):
    b = pc_ref[:, 0:512].astype(F32)
    c = pc_ref[:, 512:1024].astype(F32)
    hh = pc_ref[:, 1024:1536].astype(F32)
    gc = pc_ref[:, 1536:2048].astype(F32)
    u = c * hh
    um1, um2 = _shift_down(u, halo_ref, has_prev)
    cv = w_ref[0:1, :] * um2 + w_ref[1:2, :] * um1 + w_ref[2:3, :] * u
    return b, c, hh, gc, u, um1, um2, cv


def _prev_rows(width, col=0):
    return pl.BlockSpec((HALO, width), lambda i: (jnp.maximum(i * (TM // HALO) - 1, 0), col))


def _out_loss(x, target, ya, pc, conv_w, w_out, final_g):
    S = x.shape[0]

    def body(x_ref, t_ref, ya_ref, pc_ref, halo_ref, cw_ref, wo_ref, fg_ref,
             dh_ref, dmix_ref, gwo_ref, gfg_ref, loss_ref, cw_out_ref):
        cw = _gathered_conv_w(cw_ref)

        @pl.when(pl.program_id(0) == 0)
        def _():
            gwo_ref[...] = jnp.zeros_like(gwo_ref)
            gfg_ref[...] = jnp.zeros_like(gfg_ref)
            loss_ref[...] = jnp.zeros_like(loss_ref)
            cw_out_ref[...] = cw[0:CONV_K, :]

        b, _, _, gc, _, _, _, cv = _conv_tile(pc_ref, halo_ref, cw, pl.program_id(0) > 0)
        yc = (b * cv * _silu(gc)).astype(BF16)
        mix = jnp.concatenate([ya_ref[...], yc], axis=1)
        wo = wo_ref[...]
        fg = fg_ref[...]
        h = x_ref[...] + _nn(mix, wo)
        r = lax.rsqrt(jnp.mean(h * h, axis=-1, keepdims=True) + EPS)
        n = h * r
        err = n * fg - t_ref[...]
        loss_ref[...] += jnp.broadcast_to(
            0.5 * jnp.sum(jnp.mean(err * err, axis=-1, keepdims=True), axis=0, keepdims=True), (8, 128))
        gfg_ref[...] += jnp.sum(err * n, axis=0, keepdims=True) * (1.0 / D_MODEL)
        dyg = err * (fg * (1.0 / D_MODEL))
        dh = r * (dyg - n * jnp.mean(dyg * n, axis=-1, keepdims=True))
        dh_ref[...] = dh
        dhb = dh.astype(BF16)
        dmix_ref[...] = _nt(dhb, wo).astype(ACT)
        gwo_ref[...] += _tn(mix, dhb)

    row = lambda i: (i, 0)
    fixed = lambda i: (0, 0)
    return pl.pallas_call(
        body,
        name="out_loss",
        grid=(S // TM,),
        in_specs=[
            pl.BlockSpec((TM, D_MODEL), row),
            pl.BlockSpec((TM, D_MODEL), row),
            pl.BlockSpec((TM, ATTN_W), row),
            pl.BlockSpec((TM, PC_W), row),
            _prev_rows(PC_W),
            pl.BlockSpec((N_DEV * 8, 128), fixed),
            pl.BlockSpec((D_MODEL, D_MODEL), fixed),
            pl.BlockSpec((1, D_MODEL), fixed),
        ],
        out_specs=[
            pl.BlockSpec((TM, D_MODEL), row),
            pl.BlockSpec((TM, D_MODEL), row),
            pl.BlockSpec((D_MODEL, D_MODEL), fixed),
            pl.BlockSpec((1, D_MODEL), fixed),
            pl.BlockSpec((8, 128), fixed),
            pl.BlockSpec((CONV_K, CONV_W), fixed),
        ],
        out_shape=[
            jax.ShapeDtypeStruct((S, D_MODEL), F32),
            jax.ShapeDtypeStruct((S, D_MODEL), ACT),
            jax.ShapeDtypeStruct((D_MODEL, D_MODEL), F32),
            jax.ShapeDtypeStruct((1, D_MODEL), F32),
            jax.ShapeDtypeStruct((8, 128), F32),
            jax.ShapeDtypeStruct((CONV_K, CONV_W), F32),
        ],
        compiler_params=_params(("arbitrary",)),
    )(x, target, ya, pc, pc, conv_w, w_out, final_g)


def _attn_bwd(pa, dmix, att, probs, psinks, q_stack, tab):
    S = pa.shape[0]
    nt = S // TQ
    nb = TQ // BLK

    def body(g_ref, kvc_ref, kvp_ref, tabc_ref, tabp_ref, dm_ref, att_ref, pm_ref, ps_ref, qs_ref,
             d_ref, dsink_ref, kall, dkv, carry, do_sc, p_sc, ds_sc, dsink_acc):
        step = pl.program_id(0)

        @pl.when(step == 0)
        def _():
            carry[...] = jnp.zeros_like(carry)
            dsink_acc[...] = jnp.zeros_like(dsink_acc)

        _fill_kv(kall, kvc_ref, kvp_ref, tabc_ref, tabp_ref)
        dkv[0:TQ, :] = jnp.zeros((TQ, 2 * KV_W), F32)
        dkv[TQ:TQ + BLK, :] = carry[...]
        lane = lax.broadcasted_iota(jnp.int32, (BLK, 128), 1)
        half = [lane < HEAD_DIM, lane >= HEAD_DIM]
        upper = _upper()
        upper_b = upper.astype(BF16)
        for j in range(nb):
            rq = slice(j * BLK, (j + 1) * BLK)
            rk = slice(j * BLK, (j + 2) * BLK)
            tab = tabc_ref[rq, :]
            pair = [slice(p * 128, (p + 1) * 128) for p in range(4)]
            g = [g_ref[rq, c].astype(F32) for c in pair]
            da = [dm_ref[rq, c].astype(F32) for c in pair]
            gate = [_silu_and_grad(g[p]) for p in range(4)]
            do = [da[p] * gate[p][0] for p in range(4)]
            dqs, dks, dvs = [], [], []
            for s in range(2):
                kk = kall[s, rk, :]
                vv = kall[2 + s, rk, :]
                _stack_heads(do_sc, s, half, do)
                pb = pm_ref[(2 * j + s) * BLK:(2 * j + s + 1) * BLK, :]
                prob = pb.astype(F32)
                _split_store(p_sc, s, upper_b, pb)
                dprob = _merge(upper, _nt(vv, do_sc[s]))
                dsum = jnp.sum(dprob * prob, axis=0, keepdims=True)
                _split_store(ds_sc, s, upper_b, (prob * (dprob - dsum)).astype(BF16))
                dsink_acc[s, 0:1, :] += ps_ref[2 * j + s:2 * j + s + 1, :] * dsum
                dqs.append(_tn(ds_sc[s], kk))
                dks.append(_nn(ds_sc[s], qs_ref[2 * j + s]))
                dvs.append(_nn(p_sc[s], do_sc[s]))
            for p in range(4):
                d_ref[rq, pair[p]] = _rope_t(_unstack_pair(half, dqs, p) * 0.125, tab).astype(BF16)
                d_ref[rq, 512 + p * 128:512 + (p + 1) * 128] = (
                    da[p] * att_ref[rq, pair[p]].astype(F32) * gate[p][1]).astype(BF16)
            dkv[rk, 0:128] += dks[0] + pltpu.roll(dks[1], 64, 1)
            dkv[rk, 128:256] += dvs[0] + pltpu.roll(dvs[1], 64, 1)
        d_ref[:, 1024:1152] = _rope_t(dkv[BLK:BLK + TQ, 0:128], tabc_ref[...]).astype(BF16)
        d_ref[:, 1152:1280] = dkv[BLK:BLK + TQ, 128:256].astype(BF16)
        carry[...] = dkv[0:BLK, :]

        @pl.when(step == nt - 1)
        def _():
            lanes = lax.broadcasted_iota(jnp.int32, (8, 128), 1)
            out = jnp.zeros((8, 128), F32)
            for s in range(2):
                for a, (p, e) in enumerate(HEADS[s]):
                    tot = jnp.sum(dsink_acc[s, 0:1, a * BLK:(a + 1) * BLK], axis=1, keepdims=True)
                    out = jnp.where(lanes == 2 * p + e, -tot, out)
            dsink_ref[...] = out

    rev = lambda s: nt - 1 - s
    return pl.pallas_call(
        body,
        name="attn_bwd",
        grid=(nt,),
        in_specs=_attn_specs(rev)[2:] + [pl.BlockSpec((TQ, ATTN_W), lambda s: (nt - 1 - s, 0))] * 2 + [
            pl.BlockSpec((2 * TQ, 4 * BLK), lambda s: (nt - 1 - s, 0)),
            pl.BlockSpec((2 * nb, 4 * BLK), lambda s: (nt - 1 - s, 0)),
            pl.BlockSpec((2 * nb, 4 * BLK, 128), lambda s: (nt - 1 - s, 0, 0)),
        ],
        out_specs=[
            pl.BlockSpec((TQ, PA_W), lambda s: (nt - 1 - s, 0)),
            pl.BlockSpec((8, 128), lambda s: (0, 0)),
        ],
        out_shape=[
            jax.ShapeDtypeStruct((S, PA_W), BF16),
            jax.ShapeDtypeStruct((8, 128), F32),
        ],
        scratch_shapes=[
            pltpu.VMEM((4, BLK + TQ, 128), BF16),
            pltpu.VMEM((BLK + TQ, 2 * KV_W), F32),
            pltpu.VMEM((BLK, 2 * KV_W), F32),
            pltpu.VMEM((2, 4 * BLK, 128), BF16),
            pltpu.VMEM((2, 2 * BLK, 4 * BLK), BF16),
            pltpu.VMEM((2, 2 * BLK, 4 * BLK), BF16),
            pltpu.VMEM((2, 8, 4 * BLK), F32),
        ],
        compiler_params=_params(("arbitrary",)),
    )(pa, pa, pa, tab, tab, dmix, att, probs, psinks, q_stack)


def _conv_bwd_tile(pc_ref, prev_ref, next_ref, dm_ref, dmn_ref, w_ref, d_ref, gw_ref, has_prev, has_next,
                   on_piece):
    rows = pc_ref.shape[0]
    w0, w1, w2 = w_ref[0:1, :], w_ref[1:2, :], w_ref[2:3, :]
    b, c, hh, gc, u, um1, um2, cv = _conv_tile(pc_ref, prev_ref, w_ref, has_prev)
    sg, dsg = _silu_and_grad(gc)
    dy = dm_ref[...].astype(F32)
    dyb = dy * b
    dcv = dyb * sg

    def next_dcv(r):
        nd = (dmn_ref[r:r + 1, :].astype(F32) * next_ref[r:r + 1, 0:512].astype(F32)
              * _silu(next_ref[r:r + 1, 1536:2048].astype(F32)))
        return jnp.where(has_next, nd, 0.0)

    row = lax.broadcasted_iota(jnp.int32, (rows, CONV_W), 0)
    dp1 = jnp.where(row == rows - 1, next_dcv(0), pltpu.roll(dcv, rows - 1, 0))
    dp2 = jnp.where(row == rows - 1, next_dcv(1),
                    jnp.where(row == rows - 2, next_dcv(0), pltpu.roll(dcv, rows - 2, 0)))
    du = w2 * dcv + w1 * dp1 + w0 * dp2
    pieces = (lambda: dy * cv * sg, lambda: du * hh, lambda: du * c, lambda: dyb * cv * dsg)
    for k, piece in enumerate(pieces):
        d_ref[:, k * CONV_W:(k + 1) * CONV_W] = piece().astype(BF16)
        on_piece(k)
    gw_ref[0:1, :] += jnp.sum(dcv * um2, axis=0, keepdims=True)
    gw_ref[1:2, :] += jnp.sum(dcv * um1, axis=0, keepdims=True)
    gw_ref[2:3, :] += jnp.sum(dcv * u, axis=0, keepdims=True)


def _grad_x(da, dc, wt, x, dh, norm_g, small, grads):
    S = x.shape[0]
    n_steps = S // TM
    rs = _ReduceScatter(grads)
    n_rs_out = len(rs.out_shape())
    small_rows = SMALL_ROWS

    def body(da_ref, dc_ref, wt_ref, x_ref, dh_ref, g_ref, gfg_ref, gsink_ref, loss_ref, gcw_ref, *rest):
        grad_refs, rest = rest[:rs.n], rest[rs.n:]
        gx_ref, all_ref = rest[:2]
        rs_out, rest = rest[2:2 + n_rs_out], rest[2 + n_rs_out:]
        gng, stage, small_send, small_recv, small_own = rest[:5]
        rs_scratch = rest[5:]
        step = pl.program_id(0)
        finish = rs.emit(step, n_steps, grad_refs, rs_out, rs_scratch)

        @pl.when(step == 0)
        def _():
            gng[...] = jnp.zeros_like(gng)

        dxn = (_nn(da_ref[:, 0:512], wt_ref[0:512, :]) + _nn(da_ref[:, 512:1024], wt_ref[768:1280, :])
               + _nn(da_ref[:, 1024:1280], wt_ref[512:768, :]) + _nn(dc_ref[...], wt_ref[1280:3328, :]))
        xv = x_ref[...]
        r = lax.rsqrt(jnp.mean(xv * xv, axis=-1, keepdims=True) + EPS)
        n = xv * r
        gng[...] += jnp.sum(dxn * n, axis=0, keepdims=True)
        dxg = dxn * g_ref[...]
        gx_ref[...] = dh_ref[...] + r * (dxg - n * jnp.mean(dxg * n, axis=-1, keepdims=True))

        @pl.when(step == n_steps - 1)
        def _():
            x_, y_, c_ = lax.axis_index("x"), lax.axis_index("y"), lax.axis_index("c")
            me = 4 * x_ + 2 * y_ + c_
            for q in range(8):
                stage[q:q + 1, :] = gng[:, q * 128:(q + 1) * 128]
                stage[8 + q:9 + q, :] = gfg_ref[:, q * 128:(q + 1) * 128]
            stage[16:24, :] = gsink_ref[...]
            stage[24:32, :] = loss_ref[...]
            stage[32:small_rows, :] = jnp.zeros((small_rows - 32, 128), F32)
            for j in range(N_DEV):
                piece = gcw_ref[:, (j // 2) * 128:(j // 2 + 1) * 128]
                if j % 2:
                    piece = pltpu.roll(piece, 64, 1)
                stage[32 + 8 * j:32 + 8 * j + CONV_K, 0:64] = piece[:, 0:64]
            own = pltpu.make_async_copy(stage, all_ref.at[me], small_own)
            own.start()
            sends = []
            for k in range(1, N_DEV):
                cp = pltpu.make_async_remote_copy(
                    src_ref=stage, dst_ref=all_ref.at[me],
                    send_sem=small_send.at[k - 1], recv_sem=small_recv.at[k - 1],
                    device_id=(x_ ^ (k >> 2), y_ ^ ((k >> 1) & 1), c_ ^ (k & 1)), device_id_type=MESH)
                cp.start()
                sends.append(cp)
            for cp in sends:
                cp.wait_send()
                cp.wait_recv()
            own.wait()

        finish()

    row = lambda i: (i, 0)
    fixed = lambda i: (0, 0)
    any_spec = pl.BlockSpec(memory_space=pl.ANY)
    outs = pl.pallas_call(
        body,
        name="grad_x_reduce_scatter",
        grid=(n_steps,),
        in_specs=[
            pl.BlockSpec((TM, PA_W), row),
            pl.BlockSpec((TM, PC_W), row),
            pl.BlockSpec((IN_W, D_MODEL), fixed),
            pl.BlockSpec((TM, D_MODEL), row),
            pl.BlockSpec((TM, D_MODEL), row),
            pl.BlockSpec((1, D_MODEL), fixed),
        ] + [pl.BlockSpec(a.shape, fixed) for a in small] + [any_spec] * rs.n,
        out_specs=[pl.BlockSpec((TM, D_MODEL), row), any_spec] + [any_spec] * n_rs_out,
        out_shape=[jax.ShapeDtypeStruct((S, D_MODEL), F32),
                   jax.ShapeDtypeStruct((N_DEV, small_rows, 128), F32)] + rs.out_shape(),
        scratch_shapes=[
            pltpu.VMEM((1, D_MODEL), F32),
            pltpu.VMEM((small_rows, 128), F32),
            pltpu.SemaphoreType.DMA((N_DEV - 1,)),
            pltpu.SemaphoreType.DMA((N_DEV - 1,)),
            pltpu.SemaphoreType.DMA,
        ] + rs.scratch_shapes(),
        compiler_params=_params(("arbitrary",)),
    )(da, dc, wt, x, dh, norm_g, *small, *grads)
    return outs[0], outs[1], outs[2:2 + rs.n], outs[2 + rs.n:2 + 2 * rs.n]


def _grad_w_in(da, pc, dmix, conv_w, xn):
    S = xn.shape[0]
    tm = 2 * TM
    nt = S // tm
    t16 = tm // HALO

    def body(da_ref, pc_ref, prev_ref, next_ref, dm_ref, dmn_ref, cw_ref, xn_ref, gw_ref, dc_ref, gcw_ref):
        i = pl.program_id(0)

        @pl.when(i == 0)
        def _():
            gw_ref[...] = jnp.zeros_like(gw_ref)
            gcw_ref[...] = jnp.zeros_like(gcw_ref)

        xn = xn_ref[...]
        gw_ref[0:512, :] += _tn(da_ref[:, 0:512], xn)
        gw_ref[768:1280, :] += _tn(da_ref[:, 512:1024], xn)
        gw_ref[512:768, :] += _tn(da_ref[:, 1024:1280], xn)
        def piece_grad(k):
            rows = slice(PA_W + k * CONV_W, PA_W + (k + 1) * CONV_W)
            gw_ref[rows, :] += _tn(dc_ref[:, k * CONV_W:(k + 1) * CONV_W], xn)

        _conv_bwd_tile(pc_ref, prev_ref, next_ref, dm_ref, dmn_ref, cw_ref, dc_ref, gcw_ref, i > 0, i < nt - 1,
                       piece_grad)

    row = lambda i: (i, 0)
    fixed = lambda i: (0, 0)
    nxt = lambda i: jnp.minimum((i + 1) * t16, nt * t16 - 1)
    return pl.pallas_call(
        body,
        name="grad_w_in",
        grid=(nt,),
        in_specs=[
            pl.BlockSpec((tm, PA_W), row),
            pl.BlockSpec((tm, PC_W), row),
            pl.BlockSpec((HALO, PC_W), lambda i: (jnp.maximum(i * t16 - 1, 0), 0)),
            pl.BlockSpec((HALO, PC_W), lambda i: (nxt(i), 0)),
            pl.BlockSpec((tm, CONV_W), lambda i: (i, 1)),
            pl.BlockSpec((HALO, CONV_W), lambda i: (nxt(i), 1)),
            pl.BlockSpec((CONV_K, CONV_W), fixed),
            pl.BlockSpec((tm, D_MODEL), row),
        ],
        out_specs=[
            pl.BlockSpec((IN_W, D_MODEL), fixed, pipeline_mode=pl.Buffered(1)),
            pl.BlockSpec((tm, PC_W), row),
            pl.BlockSpec((CONV_K, CONV_W), fixed),
        ],
        out_shape=[
            jax.ShapeDtypeStruct((IN_W, D_MODEL), F32),
            jax.ShapeDtypeStruct((S, PC_W), BF16),
            jax.ShapeDtypeStruct((CONV_K, CONV_W), F32),
        ],
        compiler_params=_params(("arbitrary",)),
    )(da, pc, pc, pc, dmix, dmix, conv_w, xn)


def _adam_update(w, g, m, v):
    c1 = 1.0 - ADAM_B1 ** ADAM_STEP
    c2 = 1.0 - ADAM_B2 ** ADAM_STEP
    nm = ADAM_B1 * m + (1.0 - ADAM_B1) * g
    nv = ADAM_B2 * v + (1.0 - ADAM_B2) * (g * g)
    return -ADAM_LR * ((nm / c1) / (jnp.sqrt(nv / c2) + ADAM_EPS) + ADAM_WD * w), nm, nv


SMALL_ROWS = 96


def _small_adamw_body(parts_ref, prm, outs, total):
    me = 4 * lax.axis_index("x") + 2 * lax.axis_index("y") + lax.axis_index("c")
    acc = parts_ref[0]
    for d in range(1, N_DEV):
        acc = acc + parts_ref[d]
    total[...] = acc
    grads = (total[0:8, :], total[8:16, :], total[16:17, 0:8],
             total[pl.ds(pl.multiple_of(32 + me * 8, 8), CONV_K), 0:64])
    outs[0][...] = total[24:25, 0:1]
    for k, g in enumerate(grads):
        w_ref, m_ref, v_ref = prm[3 * k:3 * k + 3]
        g_ref, d_ref, nm_ref, nv_ref = outs[1 + 4 * k:5 + 4 * k]
        g_ref[...] = g
        d_ref[...], nm_ref[...], nv_ref[...] = _adam_update(w_ref[...], g, m_ref[...], v_ref[...])


def _sum_chips_adamw(tensors, small):
    nt = len(tensors)
    small_in = [small[0]] + [a for p in small[1] for a in p]

    def body(*refs):
        ins, extra, rest = refs[:5 * nt], refs[5 * nt:5 * nt + len(small_in)], refs[5 * nt + len(small_in):]
        for t in range(nt):
            own_ref, p_ref, w_ref, m_ref, v_ref = ins[5 * t:5 * t + 5]
            g_ref, d_ref, nm_ref, nv_ref = rest[4 * t:4 * t + 4]
            g = own_ref[...]
            for k in range(N_CHIP - 1):
                g = g + p_ref[k].astype(F32)
            g_ref[...] = g
            d_ref[...], nm_ref[...], nv_ref[...] = _adam_update(w_ref[...], g, m_ref[...], v_ref[...])

        @pl.when(pl.program_id(0) == 0)
        def _():
            _small_adamw_body(extra[0], extra[1:], rest[4 * nt:-1], rest[-1])

    whole = lambda shape: pl.BlockSpec(shape, lambda i: (0,) * len(shape))
    in_specs, out_specs, out_shape, args = [], [], [], []
    for own, others, w, m, v in tensors:
        rows, cols = w.shape
        half = rows // 2
        blk = pl.BlockSpec((half, cols), lambda i: (i, 0))
        in_specs += [blk, pl.BlockSpec((N_CHIP - 1, half, cols), lambda i: (0, i, 0)), blk, blk, blk]
        out_specs += [blk] * 4
        out_shape += [jax.ShapeDtypeStruct(w.shape, F32)] * 4
        args += [own, others, w, m, v]
    small_out = [(1, 1)] + [p[0].shape for p in small[1] for _ in range(4)]
    return pl.pallas_call(
        body,
        name="adamw",
        grid=(2,),
        in_specs=in_specs + [whole(a.shape) for a in small_in],
        out_specs=out_specs + [whole(s) for s in small_out],
        out_shape=out_shape + [jax.ShapeDtypeStruct(s, F32) for s in small_out],
        scratch_shapes=[pltpu.VMEM((SMALL_ROWS, 128), F32)],
        compiler_params=_params(("arbitrary",)),
    )(*args, *small_in)


def kernel(x, norm_g, w_in, sinks, conv_w, w_out, final_g, loss_target, m_norm_g, m_w_in, m_sinks, m_conv_w, m_w_out, m_final_g, v_norm_g, v_w_in, v_sinks, v_conv_w, v_w_out, v_final_g):
    S = x.shape[1]
    x2 = x.reshape(S, D_MODEL)
    t2 = loss_target.reshape(S, D_MODEL)
    ng = norm_g.reshape(1, D_MODEL)
    fg = final_g.reshape(1, D_MODEL)

    xn, tab, wt = _prologue(x2, ng, w_in.T)
    pa, pc, (wo, cw) = _fwd_proj(xn, wt, [w_out, conv_w])
    ya, att, probs, psinks, q_stack = _attn_fwd(pa, tab, sinks)
    dh, dmix, g_wo, g_fg, loss_part, cw3 = _out_loss(x2, t2, ya, pc, cw, wo, fg)
    da, g_sinks = _attn_bwd(pa, dmix, att, probs, psinks, q_stack, tab)
    g_wt, dc, g_cw = _grad_w_in(da, pc, dmix, cw3, xn)
    grad_x, parts, own, others = _grad_x(
        da, dc, wt, x2, dh, ng, (g_fg, g_sinks, loss_part, g_cw),
        [g_wt.reshape(N_DEV, SHARD_IN, D_MODEL), g_wo.reshape(N_DEV, SHARD_OUT, D_MODEL)])
    vec = lambda a: a.reshape(8, 128)
    row = lambda a: a.reshape(1, 8)
    gt, dt, nmt, nvt, grad_w_out, d_w_out, nm_w_out, nv_w_out, *res = _sum_chips_adamw(
        [(own[0], others[0], w_in.T, m_w_in.T, v_w_in.T), (own[1], others[1], w_out, m_w_out, v_w_out)],
        (parts, [
            (vec(norm_g), vec(m_norm_g), vec(v_norm_g)), (vec(final_g), vec(m_final_g), vec(v_final_g)),
            (row(sinks), row(m_sinks), row(v_sinks)), (conv_w, m_conv_w, v_conv_w)]))
    grad_w_in, d_w_in, nm_w_in, nv_w_in = gt.T, dt.T, nmt.T, nvt.T
    loss = res[0].reshape(())
    grad_norm_g, d_ng, nm_ng, nv_ng = [a.reshape(D_MODEL) for a in res[1:5]]
    grad_final_g, d_fg, nm_fg, nv_fg = [a.reshape(D_MODEL) for a in res[5:9]]
    grad_sinks, d_sk, nm_sk, nv_sk = [a.reshape(N_Q_HEADS) for a in res[9:13]]
    grad_conv_w, d_cw, nm_cw, nv_cw = res[13:17]

    return (loss, grad_x.reshape(1, S, D_MODEL), grad_norm_g, grad_w_in, grad_sinks, grad_conv_w, grad_w_out, grad_final_g,
            d_ng, d_w_in, d_sk, d_cw, d_w_out, d_fg,
            nm_ng, nm_w_in, nm_sk, nm_cw, nm_w_out, nm_fg,
            nv_ng, nv_w_in, nv_sk, nv_cw, nv_w_out, nv_fg)
```

```python
import numpy as np
import jax
import jax.numpy as jnp
from jax import lax
from jax.experimental import pallas as pl
from jax.experimental.pallas import tpu as pltpu

F32 = jnp.float32
BF16 = jnp.bfloat16
MESH = pl.DeviceIdType.MESH

D_MODEL = 1024
HEAD_DIM = 64
N_Q_HEADS = 8
ATTN_W = 512
KV_W = 128
BLK = 128
CONV_W = 512
CONV_K = 3
IN_W = 3328
PA_W = 1280
PC_W = 2048
EPS = 1e-5
ROPE_THETA = 500000.0
ROT_DIM = 16
N_DEV = 8
N_CHIP = 4
SHARD_IN = IN_W // N_DEV
SHARD_OUT = D_MODEL // N_DEV

ADAM_LR = 0.001
ADAM_B1 = 0.9
ADAM_B2 = 0.999
ADAM_EPS = 1e-08
ADAM_WD = 0.01
ADAM_STEP = 10

ACT = jnp.bfloat16

TM = 512
TQ = 1024
HALO = 16
VMEM_LIMIT = 56 * 1024 * 1024

NT_DIMS = (((1,), (1,)), ((), ()))
TN_DIMS = (((0,), (0,)), ((), ()))


def _params(sem=None):
    kw = dict(vmem_limit_bytes=VMEM_LIMIT)
    if sem is not None:
        kw["dimension_semantics"] = sem
    return pltpu.CompilerParams(**kw)


def _nt(a, b):
    return lax.dot_general(a, b, NT_DIMS, preferred_element_type=F32)


def _tn(a, b):
    return lax.dot_general(a, b, TN_DIMS, preferred_element_type=F32)


def _nn(a, b):
    return jnp.dot(a, b, preferred_element_type=F32)


def _silu(g):
    return g * jax.nn.sigmoid(g)


def _silu_and_grad(g):
    s = jax.nn.sigmoid(g)
    return g * s, s * (1.0 + g * (1.0 - s))


class _AllGatherInSteps:
    def __init__(self, arrs, forward_step):
        self.blocks = [(a.shape, a.dtype) for a in arrs]
        self.n = len(arrs)
        self.forward_step = forward_step

    def out_shape(self):
        return [jax.ShapeDtypeStruct((N_DEV * s[0], s[1]), d) for s, d in self.blocks]

    def scratch_shapes(self):
        return [pltpu.SemaphoreType.DMA((7 * self.n,)), pltpu.SemaphoreType.DMA((7 * self.n,)),
                pltpu.SemaphoreType.DMA((self.n,))]

    def emit(self, step, n_steps, x_refs, out_refs, scratch):
        assert n_steps > self.forward_step + 1
        send_sems, recv_sems, local_sems = scratch
        x, y, c = lax.axis_index("x"), lax.axis_index("y"), lax.axis_index("c")
        me, sibling = (x, y, c), (x, y, 1 - c)
        chips = [(1 - x, y), (x, 1 - y), (1 - x, 1 - y)]

        def rows(a, px, py, pc):
            m = self.blocks[a][0][0]
            return out_refs[a].at[pl.ds((4 * px + 2 * py + pc) * m, m), :]

        def copy(a, k, block, to, src=None):
            return pltpu.make_async_remote_copy(
                src_ref=rows(a, *block) if src is None else src, dst_ref=rows(a, *block),
                send_sem=send_sems.at[a * 7 + k], recv_sem=recv_sems.at[a * 7 + k],
                device_id=to, device_id_type=MESH)

        def mine(a):
            return pltpu.make_async_copy(x_refs[a], rows(a, *me), local_sems.at[a])

        def first(a):
            return ([copy(a, 0, me, sibling, src=x_refs[a])]
                    + [copy(a, 1 + j, me, (*chip, c), src=x_refs[a]) for j, chip in enumerate(chips)])

        def passed(a):
            return [copy(a, 4 + j, (*chip, c), sibling) for j, chip in enumerate(chips)]

        @pl.when(step == 0)
        def _():
            for a in range(self.n):
                mine(a).start()
                for cp in first(a):
                    cp.start()

        @pl.when(step == self.forward_step)
        def _():
            for j, chip in enumerate(chips):
                for a in range(self.n):
                    copy(a, 1 + j, (*chip, c), me).wait_recv()
                    copy(a, 4 + j, (*chip, c), sibling).start()

        def finish():
            @pl.when(step == n_steps - 1)
            def _():
                for a in range(self.n):
                    copy(a, 0, sibling, me).wait_recv()
                    for j, chip in enumerate(chips):
                        copy(a, 4 + j, (*chip, 1 - c), me).wait_recv()
                    for cp in first(a) + passed(a):
                        cp.wait_send()
                    mine(a).wait()

        return finish


class _AllGatherViaNeighbours:
    def __init__(self, arr, first, mid, second):
        (self.m, self.ncol), self.dtype = arr.shape, arr.dtype
        assert self.m % 32 == 0
        self.first, self.mid, self.second = first, mid, second

    def out_shape(self):
        return [jax.ShapeDtypeStruct((N_DEV * self.m, self.ncol), self.dtype)]

    def scratch_shapes(self):
        return [pltpu.SemaphoreType.DMA((11,)), pltpu.SemaphoreType.DMA((11,)), pltpu.SemaphoreType.DMA]

    def emit(self, step, n_steps, x_ref, out_ref, scratch):
        assert 0 < self.first < self.mid < self.second < n_steps - 1
        send_sems, recv_sems, local_sem = scratch
        x, y, c = lax.axis_index("x"), lax.axis_index("y"), lax.axis_index("c")
        half = self.m // 2
        sibling, xn, yn = (x, y, 1 - c), (1 - x, y, c), (x, 1 - y, c)

        def rows(dev, part=None):
            px, py, pc = dev
            base = (4 * px + 2 * py + pc) * self.m
            if part is None:
                return out_ref.at[pl.ds(base, self.m), :]
            return out_ref.at[pl.ds(base + part * half, half), :]

        def copy(k, dev, to, part=None, src=None):
            return pltpu.make_async_remote_copy(
                src_ref=rows(dev, part) if src is None else src, dst_ref=rows(dev, part),
                send_sem=send_sems.at[k], recv_sem=recv_sems.at[k], device_id=to, device_id_type=MESH)

        me, dg = (x, y, c), (1 - x, 1 - y, c)
        mine = pltpu.make_async_copy(x_ref, rows(me), local_sem)
        my_half = lambda part: x_ref.at[pl.ds(part * half, half), :]
        sends = [
            copy(0, me, sibling, src=x_ref), copy(1, me, xn, part=0, src=my_half(0)),
            copy(2, me, yn, part=1, src=my_half(1)), copy(3, xn, yn, part=0), copy(4, yn, xn, part=1),
            copy(5, xn, sibling), copy(6, yn, sibling), copy(7, dg, sibling, part=0), copy(8, dg, sibling, part=1),
            copy(9, me, xn, part=1, src=my_half(1)), copy(10, me, yn, part=0, src=my_half(0)),
        ]
        other = lambda dev: (dev[0], dev[1], 1 - c)
        arrivals = [
            copy(0, other(me), sibling), copy(1, xn, xn, part=0), copy(2, yn, yn, part=1), copy(3, dg, yn, part=0),
            copy(4, dg, xn, part=1), copy(5, other(xn), sibling), copy(6, other(yn), sibling),
            copy(7, other(dg), sibling, part=0), copy(8, other(dg), sibling, part=1),
            copy(9, xn, xn, part=1), copy(10, yn, yn, part=0),
        ]

        @pl.when(step == 0)
        def _():
            mine.start()
            for k in (0, 1, 2, 9, 10):
                sends[k].start()

        @pl.when(step == self.first)
        def _():
            arrivals[1].wait_recv()
            sends[3].start()
            arrivals[2].wait_recv()
            sends[4].start()

        @pl.when(step == self.mid)
        def _():
            arrivals[9].wait_recv()
            sends[5].start()
            arrivals[10].wait_recv()
            sends[6].start()

        @pl.when(step == self.second)
        def _():
            arrivals[3].wait_recv()
            sends[7].start()
            arrivals[4].wait_recv()
            sends[8].start()

        def finish():
            @pl.when(step == n_steps - 1)
            def _():
                for k in (0, 5, 6, 7, 8):
                    arrivals[k].wait_recv()
                for cp in sends:
                    cp.wait_send()
                mine.wait()

        return finish


class _ReduceScatter:
    def __init__(self, grads):
        self.shapes = [g.shape[1:] for g in grads]
        self.n = len(grads)
        self.items = tuple((a, r) for r in (1, 2, 3, 0) for a in range(self.n))
        self.steps = N_CHIP + 2

    def out_shape(self):
        own = [jax.ShapeDtypeStruct(s, F32) for s in self.shapes]
        ici = [jax.ShapeDtypeStruct((N_CHIP - 1,) + s, BF16) for s in self.shapes]
        land = [jax.ShapeDtypeStruct((N_CHIP,) + s, F32) for s in self.shapes]
        return own + ici + land

    def scratch_shapes(self):
        n_items = len(self.items)
        return ([pltpu.VMEM((2,) + s, F32) for s in self.shapes]
                + [pltpu.VMEM((N_CHIP - 1,) + s, BF16) for s in self.shapes]
                + [pltpu.VMEM(s, F32) for s in self.shapes]
                + [pltpu.SemaphoreType.DMA((self.n * N_CHIP,))] * 2
                + [pltpu.SemaphoreType.DMA((2 * n_items,))]
                + [pltpu.SemaphoreType.DMA((self.n * (N_CHIP - 1),))] * 2
                + [pltpu.SemaphoreType.DMA((self.n,))])

    def emit(self, step, n_steps, g_refs, out_refs, scratch):
        assert n_steps > self.steps
        n = self.n
        own_refs, ici_refs, land_refs = out_refs[:n], out_refs[n:2 * n], out_refs[2 * n:]
        stage, pair_bf, pair_own = scratch[:n], scratch[n:2 * n], scratch[2 * n:3 * n]
        sib_send, sib_recv, load_sems, ici_send, ici_recv, own_sems = scratch[3 * n:]
        x, y, c = lax.axis_index("x"), lax.axis_index("y"), lax.axis_index("c")

        def chip_of(r):
            return (x ^ (r >> 1), y ^ (r & 1))

        def block_of(r, core):
            cx, cy = chip_of(r)
            return 4 * cx + 2 * cy + core

        def to_sibling(a, r):
            return pltpu.make_async_remote_copy(
                src_ref=g_refs[a].at[block_of(r, 1 - c)], dst_ref=land_refs[a].at[r],
                send_sem=sib_send.at[a * N_CHIP + r], recv_sem=sib_recv.at[a * N_CHIP + r],
                device_id=(x, y, 1 - c), device_id_type=MESH)

        def loads(k):
            a, r = self.items[k]
            return (pltpu.make_async_copy(g_refs[a].at[block_of(r, c)], stage[a].at[0], load_sems.at[2 * k]),
                    pltpu.make_async_copy(land_refs[a].at[r], stage[a].at[1], load_sems.at[2 * k + 1]))

        def to_owner(k):
            a, r = self.items[k]
            if r == 0:
                return pltpu.make_async_copy(pair_own[a], own_refs[a], own_sems.at[a])
            return pltpu.make_async_remote_copy(
                src_ref=pair_bf[a].at[r - 1], dst_ref=ici_refs[a].at[r - 1],
                send_sem=ici_send.at[a * (N_CHIP - 1) + r - 1], recv_sem=ici_recv.at[a * (N_CHIP - 1) + r - 1],
                device_id=(*chip_of(r), c), device_id_type=MESH)

        @pl.when(step == 0)
        def _():
            for a, r in self.items:
                to_sibling(a, r).start()

        def fetch(k):
            a, r = self.items[k]
            to_sibling(a, r).wait_recv()
            for cp in loads(k):
                cp.start()

        def add_and_send(k):
            a, r = self.items[k]
            for cp in loads(k):
                cp.wait()
            total = stage[a][0] + stage[a][1]
            if r == 0:
                pair_own[a][...] = total
            else:
                pair_bf[a][r - 1] = total.astype(BF16)
            to_owner(k).start()

        for g in range(N_CHIP + 1):
            @pl.when(step == 1 + g)
            def _(g=g):
                if g > 0:
                    for k in range((g - 1) * n, g * n):
                        add_and_send(k)
                if g < N_CHIP:
                    for k in range(g * n, (g + 1) * n):
                        fetch(k)

        def finish():
            @pl.when(step == n_steps - 1)
            def _():
                for k, (a, r) in enumerate(self.items):
                    if r == 0:
                        to_owner(k).wait()
                    else:
                        to_owner(k).wait_send()
                        to_owner(k).wait_recv()
                for a, r in self.items:
                    to_sibling(a, r).wait_send()

        return finish


def _prologue(x, norm_g, w_shard):
    S = x.shape[0]
    n_steps = S // TM
    half = ROT_DIM // 2
    pos = np.arange(S, dtype=np.float32)
    inv_freq = np.float32(ROPE_THETA) ** (-np.arange(0, ROT_DIM, 2, dtype=np.float32) / np.float32(ROT_DIM))
    ang = inv_freq.astype(np.float32)[:, None] * pos[None, :]
    cs = jnp.asarray(np.concatenate([np.cos(ang), np.sin(ang)], axis=0).astype(np.float32))
    ag = _AllGatherViaNeighbours(jax.ShapeDtypeStruct(w_shard.shape, BF16),
                                 first=n_steps // 4, mid=n_steps // 2 + 2, second=n_steps - 2)

    def body(x_ref, g_ref, cs_ref, w_ref, xn_ref, tab_ref, wt_ref, w_bf, *ag_scratch):
        step = pl.program_id(0)

        @pl.when(step == 0)
        def _():
            w_bf[...] = w_ref[...].astype(BF16)

        finish = ag.emit(step, n_steps, w_bf, wt_ref, ag_scratch)
        xv = x_ref[...]
        r = lax.rsqrt(jnp.mean(xv * xv, axis=-1, keepdims=True) + EPS)
        xn_ref[...] = (xv * r * g_ref[...]).astype(BF16)

        xt = jnp.concatenate([cs_ref[...], jnp.zeros((128 - 2 * half, TM), F32)], axis=0).T
        lane = lax.broadcasted_iota(jnp.int32, (TM, 128), 1)
        rr = lane & (HEAD_DIM - 1)
        first = lane < HEAD_DIM

        def at(shift_first, shift_second):
            return jnp.where(first, pltpu.roll(xt, shift_first, 1) if shift_first else xt,
                             pltpu.roll(xt, shift_second, 1))

        cos_lo, cos_hi = at(0, HEAD_DIM), at(half, HEAD_DIM + half)
        sin_lo, sin_hi = at(128 - half, HEAD_DIM - half), at(0, HEAD_DIM)
        tab_ref[:, 0:128] = jnp.where(rr < half, cos_lo, jnp.where(rr < ROT_DIM, cos_hi, 1.0))
        tab_ref[:, 128:256] = jnp.where(rr < half, -sin_lo, 0.0)
        tab_ref[:, 256:384] = jnp.where((rr >= half) & (rr < ROT_DIM), sin_hi, 0.0)
        finish()

    any_spec = pl.BlockSpec(memory_space=pl.ANY)
    return pl.pallas_call(
        body,
        name="prologue_all_gather_w_in",
        grid=(n_steps,),
        in_specs=[
            pl.BlockSpec((TM, D_MODEL), lambda i: (i, 0)),
            pl.BlockSpec((1, D_MODEL), lambda i: (0, 0)),
            pl.BlockSpec((2 * half, TM), lambda i: (0, i)),
            pl.BlockSpec(w_shard.shape, lambda i: (0, 0)),
        ],
        out_specs=[
            pl.BlockSpec((TM, D_MODEL), lambda i: (i, 0)),
            pl.BlockSpec((TM, 384), lambda i: (i, 0)),
            any_spec,
        ],
        out_shape=[
            jax.ShapeDtypeStruct((S, D_MODEL), BF16),
            jax.ShapeDtypeStruct((S, 384), F32),
        ] + ag.out_shape(),
        scratch_shapes=[pltpu.VMEM(w_shard.shape, BF16)] + ag.scratch_shapes(),
        compiler_params=_params(("arbitrary",)),
    )(x, norm_g, cs, w_shard)


def _fwd_proj(xn, wt, later):
    S = xn.shape[0]
    tm = 2 * TM
    n_steps = S // tm
    ag = _AllGatherInSteps([jax.ShapeDtypeStruct(later[0].shape, BF16), jax.ShapeDtypeStruct((8, 128), F32)],
                           forward_step=n_steps // 2)

    def body(xn_ref, wt_ref, *rest):
        later_refs, rest = rest[:ag.n], rest[ag.n:]
        pa_ref, pc_ref = rest[:2]
        gathered, w_bf, cw_pad, ag_scratch = rest[2:2 + ag.n], rest[2 + ag.n], rest[3 + ag.n], rest[4 + ag.n:]
        step = pl.program_id(0)

        @pl.when(step == 0)
        def _():
            w_bf[...] = later_refs[0][...].astype(BF16)
            cw_pad[...] = jnp.zeros_like(cw_pad)
            cw_pad[0:CONV_K, 0:64] = later_refs[1][...]

        finish = ag.emit(step, n_steps, (w_bf, cw_pad), gathered, ag_scratch)
        xn = xn_ref[...]
        pa_ref[:, 0:512] = _nt(xn, wt_ref[0:512, :]).astype(ACT)
        pa_ref[:, 512:1024] = _nt(xn, wt_ref[768:1280, :]).astype(ACT)
        pa_ref[:, 1024:1280] = _nt(xn, wt_ref[512:768, :]).astype(ACT)
        pc_ref[...] = _nt(xn, wt_ref[1280:3328, :]).astype(ACT)
        finish()

    any_spec = pl.BlockSpec(memory_space=pl.ANY)
    outs = pl.pallas_call(
        body,
        name="fwd_proj_all_gather",
        grid=(n_steps,),
        in_specs=[
            pl.BlockSpec((tm, D_MODEL), lambda i: (i, 0)),
            pl.BlockSpec((IN_W, D_MODEL), lambda i: (0, 0)),
            pl.BlockSpec(later[0].shape, lambda i: (0, 0)),
            pl.BlockSpec(later[1].shape, lambda i: (0, 0)),
        ],
        out_specs=[
            pl.BlockSpec((tm, PA_W), lambda i: (i, 0)),
            pl.BlockSpec((tm, PC_W), lambda i: (i, 0)),
        ] + [any_spec] * ag.n,
        out_shape=[
            jax.ShapeDtypeStruct((S, PA_W), ACT),
            jax.ShapeDtypeStruct((S, PC_W), ACT),
        ] + ag.out_shape(),
        scratch_shapes=[pltpu.VMEM(later[0].shape, BF16), pltpu.VMEM((8, 128), F32)] + ag.scratch_shapes(),
        compiler_params=_params(("arbitrary",)),
    )(xn, wt, *later)
    return outs[0], outs[1], outs[2:]


def _rope(t, tab):
    return (t * tab[:, 0:128] + pltpu.roll(t, 120, 1) * tab[:, 128:256]
            + pltpu.roll(t, 8, 1) * tab[:, 256:384])


def _rope_t(d, tab):
    return (d * tab[:, 0:128] + pltpu.roll(d * tab[:, 128:256], 8, 1)
            + pltpu.roll(d * tab[:, 256:384], 120, 1))


def _fill_kv(kall, kvc_ref, kvp_ref, tabc_ref, tabp_ref):
    for lo, kv_ref, tab_ref, n in ((0, kvp_ref, tabp_ref, BLK), (BLK, kvc_ref, tabc_ref, TQ)):
        k = _rope(kv_ref[:, 0:128].astype(F32), tab_ref[...])
        v = kv_ref[:, 128:256].astype(F32)
        kall[0, lo:lo + n, :] = k.astype(BF16)
        kall[1, lo:lo + n, :] = pltpu.roll(k, 64, 1).astype(BF16)
        kall[2, lo:lo + n, :] = v.astype(BF16)
        kall[3, lo:lo + n, :] = pltpu.roll(v, 64, 1).astype(BF16)


HEADS = (((0, 0), (1, 0), (2, 1), (3, 1)), ((0, 1), (1, 1), (2, 0), (3, 0)))


def _upper():
    kj = lax.broadcasted_iota(jnp.int32, (BLK, 4 * BLK), 0)
    qi = lax.broadcasted_iota(jnp.int32, (BLK, 4 * BLK), 1) & (BLK - 1)
    return kj > qi


def _merge(upper, both):
    return jnp.where(upper, both[0:BLK, :], both[BLK:2 * BLK, :])


def _split_store(ref, s, upper_b, vb):
    first = vb * upper_b
    ref[s, 0:BLK, :] = first
    ref[s, BLK:2 * BLK, :] = vb - first


def _sink_rows(sink_ref):
    return [jnp.concatenate([jnp.full((1, BLK), sink_ref[2 * p + e], F32) for p, e in HEADS[s]], axis=1)
            for s in range(2)]


def _stack_heads(ref, slot, half, pairs, s=None):
    for a, (p, e) in enumerate(HEADS[slot if s is None else s]):
        ref[slot, a * BLK:(a + 1) * BLK, :] = jnp.where(half[e], pairs[p], 0.0).astype(BF16)


def _unstack_pair(half, outs, p):
    lo = 0 if p < 2 else 1
    rows = slice(p * BLK, (p + 1) * BLK)
    return jnp.where(half[0], outs[lo][rows, :], outs[1 - lo][rows, :])


def _softmax(sm, sinks):
    m = jnp.maximum(jnp.max(sm, axis=0, keepdims=True), sinks)
    p = jnp.exp(sm - m)
    es = jnp.exp(sinks - m)
    inv = 1.0 / (jnp.sum(p, axis=0, keepdims=True) + es)
    return p * inv, es * inv


def _scores(kk, q_stack, first):
    st = _nt(kk, q_stack)
    prev = st[0:BLK, :]
    if first is not None:
        prev = prev + jnp.where(first, -jnp.inf, 0.0)
    return prev, st[BLK:2 * BLK, :]


def _attn_specs(tile):
    nb = TQ // BLK
    prev = lambda i: jnp.maximum(tile(i) * nb - 1, 0)
    return [
        pl.BlockSpec(memory_space=pltpu.SMEM),
        pl.BlockSpec((TQ, ATTN_W), lambda i: (tile(i), 0)),
        pl.BlockSpec((TQ, ATTN_W), lambda i: (tile(i), 1)),
        pl.BlockSpec((TQ, 2 * KV_W), lambda i: (tile(i), 4)),
        pl.BlockSpec((BLK, 2 * KV_W), lambda i: (prev(i), 4)),
        pl.BlockSpec((TQ, 384), lambda i: (tile(i), 0)),
        pl.BlockSpec((BLK, 384), lambda i: (prev(i), 0)),
    ]


def _attn_fwd(pa, tab, sinks):
    S = pa.shape[0]
    nb = TQ // BLK

    def body(sink_ref, q_ref, g_ref, kvc_ref, kvp_ref, tabc_ref, tabp_ref, o_ref, att_ref, pm_ref, ps_ref,
             qs_ref, kall, p_sc):
        i = pl.program_id(0)
        _fill_kv(kall, kvc_ref, kvp_ref, tabc_ref, tabp_ref)
        lane = lax.broadcasted_iota(jnp.int32, (BLK, 128), 1)
        half = [lane < HEAD_DIM, lane >= HEAD_DIM]
        upper = _upper()
        upper_b = upper.astype(BF16)
        sinks = _sink_rows(sink_ref)
        for j in range(nb):
            rq = slice(j * BLK, (j + 1) * BLK)
            rk = slice(j * BLK, (j + 2) * BLK)
            tab = tabc_ref[rq, :]
            qr = [_rope(q_ref[rq, p * 128:(p + 1) * 128].astype(F32), tab) * 0.125 for p in range(4)]
            outs = []
            for s in range(2):
                _stack_heads(qs_ref, 2 * j + s, half, qr, s)
                prev, cur = _scores(kall[s, rk, :], qs_ref[2 * j + s], i == 0 if j == 0 else None)
                prob, psink = _softmax(jnp.where(upper, prev, cur), sinks[s])
                pb = prob.astype(BF16)
                pm_ref[(2 * j + s) * BLK:(2 * j + s + 1) * BLK, :] = pb
                ps_ref[2 * j + s:2 * j + s + 1, :] = psink
                _split_store(p_sc, s, upper_b, pb)
                outs.append(_tn(p_sc[s], kall[2 + s, rk, :]))
            for p in range(4):
                cols = slice(p * 128, (p + 1) * 128)
                att = _unstack_pair(half, outs, p)
                att_ref[rq, cols] = att.astype(BF16)
                o_ref[rq, cols] = (att * _silu(g_ref[rq, cols].astype(F32))).astype(BF16)

    return pl.pallas_call(
        body,
        name="attn_fwd",
        grid=(S // TQ,),
        in_specs=_attn_specs(lambda i: i),
        out_specs=[pl.BlockSpec((TQ, ATTN_W), lambda i: (i, 0))] * 2 + [
            pl.BlockSpec((2 * TQ, 4 * BLK), lambda i: (i, 0)),
            pl.BlockSpec((2 * nb, 4 * BLK), lambda i: (i, 0)),
            pl.BlockSpec((2 * nb, 4 * BLK, 128), lambda i: (i, 0, 0)),
        ],
        out_shape=[jax.ShapeDtypeStruct((S, ATTN_W), BF16)] * 2 + [
            jax.ShapeDtypeStruct((2 * S, 4 * BLK), BF16),
            jax.ShapeDtypeStruct((2 * S // BLK, 4 * BLK), F32),
            jax.ShapeDtypeStruct((2 * S // BLK, 4 * BLK, 128), BF16),
        ],
        scratch_shapes=[
            pltpu.VMEM((4, BLK + TQ, 128), BF16),
            pltpu.VMEM((2, 2 * BLK, 4 * BLK), BF16),
        ],
        compiler_params=_params(("arbitrary",)),
    )(sinks, pa, pa, pa, pa, tab, tab)


def _shift_down(u, halo_ref, has_prev):
    def halo_u(r):
        hu = halo_ref[r:r + 1, 512:1024].astype(F32) * halo_ref[r:r + 1, 1024:1536].astype(F32)
        return jnp.where(has_prev, hu, 0.0)

    row = lax.broadcasted_iota(jnp.int32, u.shape, 0)
    um1 = jnp.where(row == 0, halo_u(HALO - 1), pltpu.roll(u, 1, 0))
    um2 = jnp.where(row == 0, halo_u(HALO - 2), jnp.where(row == 1, halo_u(HALO - 1), pltpu.roll(u, 2, 0)))
    return um1, um2


def _gathered_conv_w(all_ref):
    pairs = [all_ref[16 * p:16 * p + 8, :] + pltpu.roll(all_ref[16 * p + 8:16 * p + 16, :], 64, 1)
             for p in range(N_DEV // 2)]
    return jnp.concatenate(pairs, axis=1)


def _conv_tile(pc_ref, halo_ref, w_ref, has_prev):
    b = pc_ref[:, 0:512].astype(F32)
    c = pc_ref[:, 512:1024].astype(F32)
    hh = pc_ref[:, 1024:1536].astype(F32)
    gc = pc_ref[:, 1536:2048].astype(F32)
    u = c * hh
    um1, um2 = _shift_down(u, halo_ref, has_prev)
    cv = w_ref[0:1, :] * um2 + w_ref[1:2, :] * um1 + w_ref[2:3, :] * u
    return b, c, hh, gc, u, um1, um2, cv


def _prev_rows(width, col=0):
    return pl.BlockSpec((HALO, width), lambda i: (jnp.maximum(i * (TM // HALO) - 1, 0), col))


def _out_loss(x, target, ya, pc, conv_w, w_out, final_g):
    S = x.shape[0]

    def body(x_ref, t_ref, ya_ref, pc_ref, halo_ref, cw_ref, wo_ref, fg_ref,
             dh_ref, dmix_ref, gwo_ref, gfg_ref, loss_ref, cw_out_ref):
        cw = _gathered_conv_w(cw_ref)

        @pl.when(pl.program_id(0) == 0)
        def _():
            gwo_ref[...] = jnp.zeros_like(gwo_ref)
            gfg_ref[...] = jnp.zeros_like(gfg_ref)
            loss_ref[...] = jnp.zeros_like(loss_ref)
            cw_out_ref[...] = cw[0:CONV_K, :]

        b, _, _, gc, _, _, _, cv = _conv_tile(pc_ref, halo_ref, cw, pl.program_id(0) > 0)
        yc = (b * cv * _silu(gc)).astype(BF16)
        mix = jnp.concatenate([ya_ref[...], yc], axis=1)
        wo = wo_ref[...]
        fg = fg_ref[...]
        h = x_ref[...] + _nn(mix, wo)
        r = lax.rsqrt(jnp.mean(h * h, axis=-1, keepdims=True) + EPS)
        n = h * r
        err = n * fg - t_ref[...]
        loss_ref[...] += jnp.broadcast_to(
            0.5 * jnp.sum(jnp.mean(err * err, axis=-1, keepdims=True), axis=0, keepdims=True), (8, 128))
        gfg_ref[...] += jnp.sum(err * n, axis=0, keepdims=True) * (1.0 / D_MODEL)
        dyg = err * (fg * (1.0 / D_MODEL))
        dh = r * (dyg - n * jnp.mean(dyg * n, axis=-1, keepdims=True))
        dh_ref[...] = dh
        dhb = dh.astype(BF16)
        dmix_ref[...] = _nt(dhb, wo).astype(ACT)
        gwo_ref[...] += _tn(mix, dhb)

    row = lambda i: (i, 0)
    fixed = lambda i: (0, 0)
    return pl.pallas_call(
        body,
        name="out_loss",
        grid=(S // TM,),
        in_specs=[
            pl.BlockSpec((TM, D_MODEL), row),
            pl.BlockSpec((TM, D_MODEL), row),
            pl.BlockSpec((TM, ATTN_W), row),
            pl.BlockSpec((TM, PC_W), row),
            _prev_rows(PC_W),
            pl.BlockSpec((N_DEV * 8, 128), fixed),
            pl.BlockSpec((D_MODEL, D_MODEL), fixed),
            pl.BlockSpec((1, D_MODEL), fixed),
        ],
        out_specs=[
            pl.BlockSpec((TM, D_MODEL), row),
            pl.BlockSpec((TM, D_MODEL), row),
            pl.BlockSpec((D_MODEL, D_MODEL), fixed),
            pl.BlockSpec((1, D_MODEL), fixed),
            pl.BlockSpec((8, 128), fixed),
            pl.BlockSpec((CONV_K, CONV_W), fixed),
        ],
        out_shape=[
            jax.ShapeDtypeStruct((S, D_MODEL), F32),
            jax.ShapeDtypeStruct((S, D_MODEL), ACT),
            jax.ShapeDtypeStruct((D_MODEL, D_MODEL), F32),
            jax.ShapeDtypeStruct((1, D_MODEL), F32),
            jax.ShapeDtypeStruct((8, 128), F32),
            jax.ShapeDtypeStruct((CONV_K, CONV_W), F32),
        ],
        compiler_params=_params(("arbitrary",)),
    )(x, target, ya, pc, pc, conv_w, w_out, final_g)


def _attn_bwd(pa, dmix, att, probs, psinks, q_stack, tab):
    S = pa.shape[0]
    nt = S // TQ
    nb = TQ // BLK

    def body(g_ref, kvc_ref, kvp_ref, tabc_ref, tabp_ref, dm_ref, att_ref, pm_ref, ps_ref, qs_ref,
             d_ref, dsink_ref, kall, dkv, carry, do_sc, p_sc, ds_sc, dsink_acc):
        step = pl.program_id(0)

        @pl.when(step == 0)
        def _():
            carry[...] = jnp.zeros_like(carry)
            dsink_acc[...] = jnp.zeros_like(dsink_acc)

        _fill_kv(kall, kvc_ref, kvp_ref, tabc_ref, tabp_ref)
        dkv[0:TQ, :] = jnp.zeros((TQ, 2 * KV_W), F32)
        dkv[TQ:TQ + BLK, :] = carry[...]
        lane = lax.broadcasted_iota(jnp.int32, (BLK, 128), 1)
        half = [lane < HEAD_DIM, lane >= HEAD_DIM]
        upper = _upper()
        upper_b = upper.astype(BF16)
        for j in range(nb):
            rq = slice(j * BLK, (j + 1) * BLK)
            rk = slice(j * BLK, (j + 2) * BLK)
            tab = tabc_ref[rq, :]
            pair = [slice(p * 128, (p + 1) * 128) for p in range(4)]
            g = [g_ref[rq, c].astype(F32) for c in pair]
            da = [dm_ref[rq, c].astype(F32) for c in pair]
            gate = [_silu_and_grad(g[p]) for p in range(4)]
            do = [da[p] * gate[p][0] for p in range(4)]
            dqs, dks, dvs = [], [], []
            for s in range(2):
                kk = kall[s, rk, :]
                vv = kall[2 + s, rk, :]
                _stack_heads(do_sc, s, half, do)
                pb = pm_ref[(2 * j + s) * BLK:(2 * j + s + 1) * BLK, :]
                prob = pb.astype(F32)
                _split_store(p_sc, s, upper_b, pb)
                dprob = _merge(upper, _nt(vv, do_sc[s]))
                dsum = jnp.sum(dprob * prob, axis=0, keepdims=True)
                _split_store(ds_sc, s, upper_b, (prob * (dprob - dsum)).astype(BF16))
                dsink_acc[s, 0:1, :] += ps_ref[2 * j + s:2 * j + s + 1, :] * dsum
                dqs.append(_tn(ds_sc[s], kk))
                dks.append(_nn(ds_sc[s], qs_ref[2 * j + s]))
                dvs.append(_nn(p_sc[s], do_sc[s]))
            for p in range(4):
                d_ref[rq, pair[p]] = _rope_t(_unstack_pair(half, dqs, p) * 0.125, tab).astype(BF16)
                d_ref[rq, 512 + p * 128:512 + (p + 1) * 128] = (
                    da[p] * att_ref[rq, pair[p]].astype(F32) * gate[p][1]).astype(BF16)
            dkv[rk, 0:128] += dks[0] + pltpu.roll(dks[1], 64, 1)
            dkv[rk, 128:256] += dvs[0] + pltpu.roll(dvs[1], 64, 1)
        d_ref[:, 1024:1152] = _rope_t(dkv[BLK:BLK + TQ, 0:128], tabc_ref[...]).astype(BF16)
        d_ref[:, 1152:1280] = dkv[BLK:BLK + TQ, 128:256].astype(BF16)
        carry[...] = dkv[0:BLK, :]

        @pl.when(step == nt - 1)
        def _():
            lanes = lax.broadcasted_iota(jnp.int32, (8, 128), 1)
            out = jnp.zeros((8, 128), F32)
            for s in range(2):
                for a, (p, e) in enumerate(HEADS[s]):
                    tot = jnp.sum(dsink_acc[s, 0:1, a * BLK:(a + 1) * BLK], axis=1, keepdims=True)
                    out = jnp.where(lanes == 2 * p + e, -tot, out)
            dsink_ref[...] = out

    rev = lambda s: nt - 1 - s
    return pl.pallas_call(
        body,
        name="attn_bwd",
        grid=(nt,),
        in_specs=_attn_specs(rev)[2:] + [pl.BlockSpec((TQ, ATTN_W), lambda s: (nt - 1 - s, 0))] * 2 + [
            pl.BlockSpec((2 * TQ, 4 * BLK), lambda s: (nt - 1 - s, 0)),
            pl.BlockSpec((2 * nb, 4 * BLK), lambda s: (nt - 1 - s, 0)),
            pl.BlockSpec((2 * nb, 4 * BLK, 128), lambda s: (nt - 1 - s, 0, 0)),
        ],
        out_specs=[
            pl.BlockSpec((TQ, PA_W), lambda s: (nt - 1 - s, 0)),
            pl.BlockSpec((8, 128), lambda s: (0, 0)),
        ],
        out_shape=[
            jax.ShapeDtypeStruct((S, PA_W), BF16),
            jax.ShapeDtypeStruct((8, 128), F32),
        ],
        scratch_shapes=[
            pltpu.VMEM((4, BLK + TQ, 128), BF16),
            pltpu.VMEM((BLK + TQ, 2 * KV_W), F32),
            pltpu.VMEM((BLK, 2 * KV_W), F32),
            pltpu.VMEM((2, 4 * BLK, 128), BF16),
            pltpu.VMEM((2, 2 * BLK, 4 * BLK), BF16),
            pltpu.VMEM((2, 2 * BLK, 4 * BLK), BF16),
            pltpu.VMEM((2, 8, 4 * BLK), F32),
        ],
        compiler_params=_params(("arbitrary",)),
    )(pa, pa, pa, tab, tab, dmix, att, probs, psinks, q_stack)


def _conv_bwd_tile(pc_ref, prev_ref, next_ref, dm_ref, dmn_ref, w_ref, d_ref, gw_ref, has_prev, has_next,
                   on_piece):
    rows = pc_ref.shape[0]
    w0, w1, w2 = w_ref[0:1, :], w_ref[1:2, :], w_ref[2:3, :]
    b, c, hh, gc, u, um1, um2, cv = _conv_tile(pc_ref, prev_ref, w_ref, has_prev)
    sg, dsg = _silu_and_grad(gc)
    dy = dm_ref[...].astype(F32)
    dyb = dy * b
    dcv = dyb * sg

    def next_dcv(r):
        nd = (dmn_ref[r:r + 1, :].astype(F32) * next_ref[r:r + 1, 0:512].astype(F32)
              * _silu(next_ref[r:r + 1, 1536:2048].astype(F32)))
        return jnp.where(has_next, nd, 0.0)

    row = lax.broadcasted_iota(jnp.int32, (rows, CONV_W), 0)
    dp1 = jnp.where(row == rows - 1, next_dcv(0), pltpu.roll(dcv, rows - 1, 0))
    dp2 = jnp.where(row == rows - 1, next_dcv(1),
                    jnp.where(row == rows - 2, next_dcv(0), pltpu.roll(dcv, rows - 2, 0)))
    du = w2 * dcv + w1 * dp1 + w0 * dp2
    pieces = (lambda: dy * cv * sg, lambda: du * hh, lambda: du * c, lambda: dyb * cv * dsg)
    for k, piece in enumerate(pieces):
        d_ref[:, k * CONV_W:(k + 1) * CONV_W] = piece().astype(BF16)
        on_piece(k)
    gw_ref[0:1, :] += jnp.sum(dcv * um2, axis=0, keepdims=True)
    gw_ref[1:2, :] += jnp.sum(dcv * um1, axis=0, keepdims=True)
    gw_ref[2:3, :] += jnp.sum(dcv * u, axis=0, keepdims=True)


def _grad_x(da, dc, wt, x, dh, norm_g, small, grads):
    S = x.shape[0]
    n_steps = S // TM
    rs = _ReduceScatter(grads)
    n_rs_out = len(rs.out_shape())
    small_rows = SMALL_ROWS

    def body(da_ref, dc_ref, wt_ref, x_ref, dh_ref, g_ref, gfg_ref, gsink_ref, loss_ref, gcw_ref, *rest):
        grad_refs, rest = rest[:rs.n], rest[rs.n:]
        gx_ref, all_ref = rest[:2]
        rs_out, rest = rest[2:2 + n_rs_out], rest[2 + n_rs_out:]
        gng, stage, small_send, small_recv, small_own = rest[:5]
        rs_scratch = rest[5:]
        step = pl.program_id(0)
        finish = rs.emit(step, n_steps, grad_refs, rs_out, rs_scratch)

        @pl.when(step == 0)
        def _():
            gng[...] = jnp.zeros_like(gng)

        dxn = (_nn(da_ref[:, 0:512], wt_ref[0:512, :]) + _nn(da_ref[:, 512:1024], wt_ref[768:1280, :])
               + _nn(da_ref[:, 1024:1280], wt_ref[512:768, :]) + _nn(dc_ref[...], wt_ref[1280:3328, :]))
        xv = x_ref[...]
        r = lax.rsqrt(jnp.mean(xv * xv, axis=-1, keepdims=True) + EPS)
        n = xv * r
        gng[...] += jnp.sum(dxn * n, axis=0, keepdims=True)
        dxg = dxn * g_ref[...]
        gx_ref[...] = dh_ref[...] + r * (dxg - n * jnp.mean(dxg * n, axis=-1, keepdims=True))

        @pl.when(step == n_steps - 1)
        def _():
            x_, y_, c_ = lax.axis_index("x"), lax.axis_index("y"), lax.axis_index("c")
            me = 4 * x_ + 2 * y_ + c_
            for q in range(8):
                stage[q:q + 1, :] = gng[:, q * 128:(q + 1) * 128]
                stage[8 + q:9 + q, :] = gfg_ref[:, q * 128:(q + 1) * 128]
            stage[16:24, :] = gsink_ref[...]
            stage[24:32, :] = loss_ref[...]
            stage[32:small_rows, :] = jnp.zeros((small_rows - 32, 128), F32)
            for j in range(N_DEV):
                piece = gcw_ref[:, (j // 2) * 128:(j // 2 + 1) * 128]
                if j % 2:
                    piece = pltpu.roll(piece, 64, 1)
                stage[32 + 8 * j:32 + 8 * j + CONV_K, 0:64] = piece[:, 0:64]
            own = pltpu.make_async_copy(stage, all_ref.at[me], small_own)
            own.start()
            sends = []
            for k in range(1, N_DEV):
                cp = pltpu.make_async_remote_copy(
                    src_ref=stage, dst_ref=all_ref.at[me],
                    send_sem=small_send.at[k - 1], recv_sem=small_recv.at[k - 1],
                    device_id=(x_ ^ (k >> 2), y_ ^ ((k >> 1) & 1), c_ ^ (k & 1)), device_id_type=MESH)
                cp.start()
                sends.append(cp)
            for cp in sends:
                cp.wait_send()
                cp.wait_recv()
            own.wait()

        finish()

    row = lambda i: (i, 0)
    fixed = lambda i: (0, 0)
    any_spec = pl.BlockSpec(memory_space=pl.ANY)
    outs = pl.pallas_call(
        body,
        name="grad_x_reduce_scatter",
        grid=(n_steps,),
        in_specs=[
            pl.BlockSpec((TM, PA_W), row),
            pl.BlockSpec((TM, PC_W), row),
            pl.BlockSpec((IN_W, D_MODEL), fixed),
            pl.BlockSpec((TM, D_MODEL), row),
            pl.BlockSpec((TM, D_MODEL), row),
            pl.BlockSpec((1, D_MODEL), fixed),
        ] + [pl.BlockSpec(a.shape, fixed) for a in small] + [any_spec] * rs.n,
        out_specs=[pl.BlockSpec((TM, D_MODEL), row), any_spec] + [any_spec] * n_rs_out,
        out_shape=[jax.ShapeDtypeStruct((S, D_MODEL), F32),
                   jax.ShapeDtypeStruct((N_DEV, small_rows, 128), F32)] + rs.out_shape(),
        scratch_shapes=[
            pltpu.VMEM((1, D_MODEL), F32),
            pltpu.VMEM((small_rows, 128), F32),
            pltpu.SemaphoreType.DMA((N_DEV - 1,)),
            pltpu.SemaphoreType.DMA((N_DEV - 1,)),
            pltpu.SemaphoreType.DMA,
        ] + rs.scratch_shapes(),
        input_output_aliases={4: 0},
        compiler_params=_params(("arbitrary",)),
    )(da, dc, wt, x, dh, norm_g, *small, *grads)
    return outs[0], outs[1], outs[2:2 + rs.n], outs[2 + rs.n:2 + 2 * rs.n]


def _grad_w_in(da, pc, dmix, conv_w, xn):
    S = xn.shape[0]
    tm = 2 * TM
    nt = S // tm
    t16 = tm // HALO

    def body(da_ref, pc_ref, prev_ref, next_ref, dm_ref, dmn_ref, cw_ref, xn_ref, gw_ref, dc_ref, gcw_ref):
        i = pl.program_id(0)

        @pl.when(i == 0)
        def _():
            gw_ref[...] = jnp.zeros_like(gw_ref)
            gcw_ref[...] = jnp.zeros_like(gcw_ref)

        xn = xn_ref[...]
        gw_ref[0:512, :] += _tn(da_ref[:, 0:512], xn)
        gw_ref[768:1280, :] += _tn(da_ref[:, 512:1024], xn)
        gw_ref[512:768, :] += _tn(da_ref[:, 1024:1280], xn)
        def piece_grad(k):
            rows = slice(PA_W + k * CONV_W, PA_W + (k + 1) * CONV_W)
            gw_ref[rows, :] += _tn(dc_ref[:, k * CONV_W:(k + 1) * CONV_W], xn)

        _conv_bwd_tile(pc_ref, prev_ref, next_ref, dm_ref, dmn_ref, cw_ref, dc_ref, gcw_ref, i > 0, i < nt - 1,
                       piece_grad)

    row = lambda i: (i, 0)
    fixed = lambda i: (0, 0)
    nxt = lambda i: jnp.minimum((i + 1) * t16, nt * t16 - 1)
    return pl.pallas_call(
        body,
        name="grad_w_in",
        grid=(nt,),
        in_specs=[
            pl.BlockSpec((tm, PA_W), row),
            pl.BlockSpec((tm, PC_W), row),
            pl.BlockSpec((HALO, PC_W), lambda i: (jnp.maximum(i * t16 - 1, 0), 0)),
            pl.BlockSpec((HALO, PC_W), lambda i: (nxt(i), 0)),
            pl.BlockSpec((tm, CONV_W), lambda i: (i, 1)),
            pl.BlockSpec((HALO, CONV_W), lambda i: (nxt(i), 1)),
            pl.BlockSpec((CONV_K, CONV_W), fixed),
            pl.BlockSpec((tm, D_MODEL), row),
        ],
        out_specs=[
            pl.BlockSpec((IN_W, D_MODEL), fixed, pipeline_mode=pl.Buffered(1)),
            pl.BlockSpec((tm, PC_W), row),
            pl.BlockSpec((CONV_K, CONV_W), fixed),
        ],
        out_shape=[
            jax.ShapeDtypeStruct((IN_W, D_MODEL), F32),
            jax.ShapeDtypeStruct((S, PC_W), BF16),
            jax.ShapeDtypeStruct((CONV_K, CONV_W), F32),
        ],
        compiler_params=_params(("arbitrary",)),
    )(da, pc, pc, pc, dmix, dmix, conv_w, xn)


def _adam_update(w, g, m, v):
    c1 = 1.0 - ADAM_B1 ** ADAM_STEP
    c2 = 1.0 - ADAM_B2 ** ADAM_STEP
    nm = ADAM_B1 * m + (1.0 - ADAM_B1) * g
    nv = ADAM_B2 * v + (1.0 - ADAM_B2) * (g * g)
    return -ADAM_LR * ((nm / c1) / (jnp.sqrt(nv / c2) + ADAM_EPS) + ADAM_WD * w), nm, nv


SMALL_ROWS = 96


def _small_adamw_body(parts_ref, prm, outs, total):
    me = 4 * lax.axis_index("x") + 2 * lax.axis_index("y") + lax.axis_index("c")
    acc = parts_ref[0]
    for d in range(1, N_DEV):
        acc = acc + parts_ref[d]
    total[...] = acc
    grads = (total[0:8, :], total[8:16, :], total[16:17, 0:8],
             total[pl.ds(pl.multiple_of(32 + me * 8, 8), CONV_K), 0:64])
    outs[0][...] = total[24:25, 0:1]
    for k, g in enumerate(grads):
        w_ref, m_ref, v_ref = prm[3 * k:3 * k + 3]
        g_ref, d_ref, nm_ref, nv_ref = outs[1 + 4 * k:5 + 4 * k]
        g_ref[...] = g
        d_ref[...], nm_ref[...], nv_ref[...] = _adam_update(w_ref[...], g, m_ref[...], v_ref[...])


def _sum_chips_adamw(tensors, small):
    nt = len(tensors)
    small_in = [small[0]] + [a for p in small[1] for a in p]

    def body(*refs):
        ins, extra, rest = refs[:5 * nt], refs[5 * nt:5 * nt + len(small_in)], refs[5 * nt + len(small_in):]
        for t in range(nt):
            own_ref, p_ref, w_ref, m_ref, v_ref = ins[5 * t:5 * t + 5]
            g_ref, d_ref, nm_ref, nv_ref = rest[4 * t:4 * t + 4]
            g = own_ref[...]
            for k in range(N_CHIP - 1):
                g = g + p_ref[k].astype(F32)
            g_ref[...] = g
            d_ref[...], nm_ref[...], nv_ref[...] = _adam_update(w_ref[...], g, m_ref[...], v_ref[...])

        @pl.when(pl.program_id(0) == 0)
        def _():
            _small_adamw_body(extra[0], extra[1:], rest[4 * nt:-1], rest[-1])

    whole = lambda shape: pl.BlockSpec(shape, lambda i: (0,) * len(shape))
    in_specs, out_specs, out_shape, args = [], [], [], []
    for own, others, w, m, v in tensors:
        rows, cols = w.shape
        half = rows // 2
        blk = pl.BlockSpec((half, cols), lambda i: (i, 0))
        in_specs += [blk, pl.BlockSpec((N_CHIP - 1, half, cols), lambda i: (0, i, 0)), blk, blk, blk]
        out_specs += [blk] * 4
        out_shape += [jax.ShapeDtypeStruct(w.shape, F32)] * 4
        args += [own, others, w, m, v]
    small_out = [(1, 1)] + [p[0].shape for p in small[1] for _ in range(4)]
    return pl.pallas_call(
        body,
        name="adamw",
        grid=(2,),
        in_specs=in_specs + [whole(a.shape) for a in small_in],
        out_specs=out_specs + [whole(s) for s in small_out],
        out_shape=out_shape + [jax.ShapeDtypeStruct(s, F32) for s in small_out],
        scratch_shapes=[pltpu.VMEM((SMALL_ROWS, 128), F32)],
        compiler_params=_params(("arbitrary",)),
    )(*args, *small_in)


def kernel(x, norm_g, w_in, sinks, conv_w, w_out, final_g, loss_target, m_norm_g, m_w_in, m_sinks, m_conv_w, m_w_out, m_final_g, v_norm_g, v_w_in, v_sinks, v_conv_w, v_w_out, v_final_g):
    S = x.shape[1]
    x2 = x.reshape(S, D_MODEL)
    t2 = loss_target.reshape(S, D_MODEL)
    ng = norm_g.reshape(1, D_MODEL)
    fg = final_g.reshape(1, D_MODEL)

    xn, tab, wt = _prologue(x2, ng, w_in.T)
    pa, pc, (wo, cw) = _fwd_proj(xn, wt, [w_out, conv_w])
    ya, att, probs, psinks, q_stack = _attn_fwd(pa, tab, sinks)
    dh, dmix, g_wo, g_fg, loss_part, cw3 = _out_loss(x2, t2, ya, pc, cw, wo, fg)
    da, g_sinks = _attn_bwd(pa, dmix, att, probs, psinks, q_stack, tab)
    g_wt, dc, g_cw = _grad_w_in(da, pc, dmix, cw3, xn)
    grad_x, parts, own, others = _grad_x(
        da, dc, wt, x2, dh, ng, (g_fg, g_sinks, loss_part, g_cw),
        [g_wt.reshape(N_DEV, SHARD_IN, D_MODEL), g_wo.reshape(N_DEV, SHARD_OUT, D_MODEL)])
    vec = lambda a: a.reshape(8, 128)
    row = lambda a: a.reshape(1, 8)
    gt, dt, nmt, nvt, grad_w_out, d_w_out, nm_w_out, nv_w_out, *res = _sum_chips_adamw(
        [(own[0], others[0], w_in.T, m_w_in.T, v_w_in.T), (own[1], others[1], w_out, m_w_out, v_w_out)],
        (parts, [
            (vec(norm_g), vec(m_norm_g), vec(v_norm_g)), (vec(final_g), vec(m_final_g), vec(v_final_g)),
            (row(sinks), row(m_sinks), row(v_sinks)), (conv_w, m_conv_w, v_conv_w)]))
    grad_w_in, d_w_in, nm_w_in, nv_w_in = gt.T, dt.T, nmt.T, nvt.T
    loss = res[0].reshape(())
    grad_norm_g, d_ng, nm_ng, nv_ng = [a.reshape(D_MODEL) for a in res[1:5]]
    grad_final_g, d_fg, nm_fg, nv_fg = [a.reshape(D_MODEL) for a in res[5:9]]
    grad_sinks, d_sk, nm_sk, nv_sk = [a.reshape(N_Q_HEADS) for a in res[9:13]]
    grad_conv_w, d_cw, nm_cw, nv_cw = res[13:17]

    return (loss, grad_x.reshape(1, S, D_MODEL), grad_norm_g, grad_w_in, grad_sinks, grad_conv_w, grad_w_out, grad_final_g,
            d_ng, d_w_in, d_sk, d_cw, d_w_out, d_fg,
            nm_ng, nm_w_in, nm_sk, nm_cw, nm_w_out, nm_fg,
            nv_ng, nv_w_in, nv_sk, nv_cw, nv_w_out, nv_fg)
```
